```python
import math
import jax, jax.numpy as jnp
from jax import lax
import numpy as np

D_MODEL = 2048
BATCH = 8
SEQ = 8192
DEPTH = 1

MIX_WIDTH = D_MODEL
ATTN_WIDTH = D_MODEL // 2
HEAD_DIM = 128
N_HEADS = ATTN_WIDTH // HEAD_DIM
DILATION_PATTERNS = ((128, 1), (512, 4), (2048, 16))
SSM_WIDTH = MIX_WIDTH - ATTN_WIDTH
SSM_GROUP = 16
N_SSM_GROUPS = SSM_WIDTH // SSM_GROUP
STATE_DIM = 64
SSM_CHUNK = 128
IN_WIDTH = 3 * ATTN_WIDTH + SSM_WIDTH
D_FF = 4 * D_MODEL
N_MOD = 6
EPS = 1e-6
DT_MIN = 1e-3
DT_MAX = 1e-1

kernel_name = "hymba_dilated_attn_s5_sqrelu_adaln"


def rms_norm(x, g):
    xf = x.astype(jnp.float32)
    y = xf * lax.rsqrt(jnp.mean(xf * xf, axis=-1, keepdims=True) + EPS) * g.astype(jnp.float32)
    return y.astype(x.dtype)


def alibi_slopes(n_heads):
    return 2.0 ** (-8.0 * (jnp.arange(n_heads, dtype=jnp.float32) + 1.0) / n_heads)


def dilated_pattern(q, k, v, slopes, window, dilation):
    b, s, h, e = q.shape
    n = window // dilation
    L = s // dilation
    nb = -(-L // n)
    Lp = nb * n

    def to_blocks(t):
        t = t.reshape(b, L, dilation, h, e)
        t = jnp.pad(t, ((0, 0), (0, Lp - L), (0, 0), (0, 0), (0, 0)))
        return t.reshape(b, nb, n, dilation, h, e)

    def with_prev(t):
        prev = jnp.pad(t, ((0, 0), (1, 0), (0, 0), (0, 0), (0, 0), (0, 0)))[:, :-1]
        return jnp.concatenate([prev, t], axis=2)

    qb = to_blocks(q)
    kw = with_prev(to_blocks(k))
    vw = with_prev(to_blocks(v))
    scores = jnp.einsum('bnqrhe,bnkrhe->bnrhqk', qb, kw) * (HEAD_DIM ** -0.5)

    qi = jnp.arange(n)[:, None]
    ki = jnp.arange(2 * n)[None, :]
    steps = qi - ki + n
    key_idx = jnp.arange(nb)[:, None, None] * n - n + ki
    valid = (steps >= 0) & (steps <= n) & (key_idx >= 0)
    bias = -slopes[:, None, None] * (steps * dilation).astype(jnp.float32)
    scores = scores + bias[None, None, None]
    scores = jnp.where(valid[None, :, None, None], scores, -jnp.inf)

    m = jnp.max(scores, axis=-1, keepdims=True)
    p = jnp.exp(scores - m)
    denom = jnp.sum(p, axis=-1, keepdims=True)
    out = jnp.einsum('bnrhqk,bnkrhe->bnqrhe', p, vw)
    denom_q = jnp.moveaxis(denom[..., 0], -1, 2)
    lse_q = jnp.moveaxis(m[..., 0], -1, 2) + jnp.log(denom_q)
    out = out / denom_q[..., None]
    out = out.reshape(b, Lp, dilation, h, e)[:, :L].reshape(b, s, h, e)
    lse = lse_q.reshape(b, Lp, dilation, h)[:, :L].reshape(b, s, h)
    return out, lse


def dilated_attention(q, k, v):
    slopes = alibi_slopes(N_HEADS)
    outs, lses = [], []
    for window, dilation in DILATION_PATTERNS:
        o, l = dilated_pattern(q, k, v, slopes, window, dilation)
        outs.append(o)
        lses.append(l)
    w = jax.nn.softmax(jnp.stack(lses, axis=0), axis=0)
    return jnp.sum(w[..., None] * jnp.stack(outs, axis=0), axis=0)


def _ssm_combine(e_i, e_j):
    a_i, b_i = e_i
    a_j, b_j = e_j
    return a_j * a_i, a_j * b_i + b_j


def s5_mixer(u, lam_re, lam_im, log_step, b_re, b_im, c_re, c_im, d_skip):
    bsz, s, _ = u.shape
    uf = u.astype(jnp.float32).reshape(bsz, s, N_SSM_GROUPS, SSM_GROUP)
    lam = lax.complex(lam_re.astype(jnp.float32), lam_im.astype(jnp.float32))
    step = jnp.exp(log_step.astype(jnp.float32))[:, None]
    a_bar = jnp.exp(lam * step)
    b_mat = lax.complex(b_re.astype(jnp.float32), b_im.astype(jnp.float32))
    b_bar = ((a_bar - 1.0) / lam)[..., None] * b_mat
    c_mat = lax.complex(c_re.astype(jnp.float32), c_im.astype(jnp.float32))
    n_chunks = s // SSM_CHUNK
    u_chunks = uf.reshape(bsz, n_chunks, SSM_CHUNK, N_SSM_GROUPS, SSM_GROUP).transpose(1, 0, 2, 3, 4)
    a_full = jnp.broadcast_to(a_bar, (bsz, SSM_CHUNK, N_SSM_GROUPS, STATE_DIM))

    def segment(h, u_c):
        bu = jnp.einsum('blgi,gpi->blgp', u_c.astype(jnp.complex64), b_bar)
        bu = bu.at[:, 0].add(a_bar * h)
        _, hs = lax.associative_scan(_ssm_combine, (a_full, bu), axis=1)
        y = jnp.real(jnp.einsum('blgp,gip->blgi', hs, c_mat))
        return hs[:, -1], y

    h0 = jnp.zeros((bsz, N_SSM_GROUPS, STATE_DIM), jnp.complex64)
    _, ys = lax.scan(segment, h0, u_chunks)
    y = ys.transpose(1, 0, 2, 3, 4).reshape(bsz, s, N_SSM_GROUPS, SSM_GROUP)
    y = y + d_skip.astype(jnp.float32).reshape(N_SSM_GROUPS, SSM_GROUP) * uf
    return y.reshape(bsz, s, SSM_WIDTH)


def hybrid_layer(x, c, w_ada, b_ada, norm1_g, w_in, q_norm_g, k_norm_g, lam_re, lam_im,
                 log_step, b_re, b_im, c_re, c_im, d_skip, w_glu, b_glu, attn_out_g,
                 ssm_out_g, w_out, norm2_g, w_ff1, w_ff2):
    bsz, s, _ = x.shape
    mod = (jax.nn.silu(c) @ w_ada + b_ada)[:, None, :]
    sh1, sc1, g1, sh2, sc2, g2 = jnp.split(mod, N_MOD, axis=-1)

    h = rms_norm(x, norm1_g) * (1.0 + sc1) + sh1
    proj = h @ w_in
    q, k, v, u = jnp.split(proj, [ATTN_WIDTH, 2 * ATTN_WIDTH, 3 * ATTN_WIDTH], axis=-1)
    q = rms_norm(q.reshape(bsz, s, N_HEADS, HEAD_DIM), q_norm_g).astype(jnp.float32)
    k = rms_norm(k.reshape(bsz, s, N_HEADS, HEAD_DIM), k_norm_g).astype(jnp.float32)
    v = v.reshape(bsz, s, N_HEADS, HEAD_DIM).astype(jnp.float32)
    attn = dilated_attention(q, k, v).reshape(bsz, s, ATTN_WIDTH).astype(x.dtype)

    y = jax.nn.gelu(s5_mixer(u, lam_re, lam_im, log_step, b_re, b_im, c_re, c_im, d_skip)).astype(x.dtype)
    ssm = y * jax.nn.sigmoid(y @ w_glu + b_glu)

    mixed = jnp.concatenate([rms_norm(attn, attn_out_g), rms_norm(ssm, ssm_out_g)], axis=-1) @ w_out
    x = x + g1 * mixed

    h2 = rms_norm(x, norm2_g) * (1.0 + sc2) + sh2
    ff = jnp.square(jax.nn.relu(h2 @ w_ff1)) @ w_ff2
    return x + g2 * ff


def _fwd_setup_inputs(seed: int = 0) -> dict:
    key = jax.random.key(seed)
    ks = jax.random.split(key, 26)
    f32 = jnp.float32
    nrm = lambda k, shape, scale: jax.random.normal(k, shape, f32) * scale
    G, P = N_SSM_GROUPS, STATE_DIM
    lam_im_base = jnp.pi * jnp.arange(P, dtype=f32)
    return {
        "x": nrm(ks[0], (BATCH, SEQ, D_MODEL), 1.0),
        "c": nrm(ks[1], (BATCH, D_MODEL), 1.0),
        "w_ada": nrm(ks[2], (DEPTH, D_MODEL, N_MOD * D_MODEL), 0.5 * D_MODEL ** -0.5),
        "b_ada": nrm(ks[3], (DEPTH, N_MOD * D_MODEL), 0.01),
        "norm1_g": 1.0 + nrm(ks[4], (DEPTH, D_MODEL), 0.01),
        "w_in": nrm(ks[5], (DEPTH, D_MODEL, IN_WIDTH), D_MODEL ** -0.5),
        "q_norm_g": 1.0 + nrm(ks[6], (DEPTH, HEAD_DIM), 0.01),
        "k_norm_g": 1.0 + nrm(ks[7], (DEPTH, HEAD_DIM), 0.01),
        "lam_re": -0.5 + nrm(ks[8], (DEPTH, G, P), 0.01),
        "lam_im": lam_im_base + nrm(ks[9], (DEPTH, G, P), 0.01),
        "log_step": jax.random.uniform(ks[10], (DEPTH, G), f32, math.log(DT_MIN), math.log(DT_MAX)),
        "b_re": nrm(ks[11], (DEPTH, G, P, SSM_GROUP), (2.0 * SSM_GROUP) ** -0.5),
        "b_im": nrm(ks[12], (DEPTH, G, P, SSM_GROUP), (2.0 * SSM_GROUP) ** -0.5),
        "c_re": nrm(ks[13], (DEPTH, G, SSM_GROUP, P), (2.0 * P) ** -0.5),
        "c_im": nrm(ks[14], (DEPTH, G, SSM_GROUP, P), (2.0 * P) ** -0.5),
        "d_skip": nrm(ks[15], (DEPTH, SSM_WIDTH), 1.0),
        "w_glu": nrm(ks[16], (DEPTH, SSM_WIDTH, SSM_WIDTH), SSM_WIDTH ** -0.5),
        "b_glu": nrm(ks[17], (DEPTH, SSM_WIDTH), 0.01),
        "attn_out_g": 1.0 + nrm(ks[18], (DEPTH, ATTN_WIDTH), 0.01),
        "ssm_out_g": 1.0 + nrm(ks[19], (DEPTH, SSM_WIDTH), 0.01),
        "w_out": nrm(ks[20], (DEPTH, MIX_WIDTH, D_MODEL), MIX_WIDTH ** -0.5),
        "norm2_g": 1.0 + nrm(ks[21], (DEPTH, D_MODEL), 0.01),
        "w_ff1": nrm(ks[22], (DEPTH, D_MODEL, D_FF), D_MODEL ** -0.5),
        "w_ff2": nrm(ks[23], (DEPTH, D_FF, D_MODEL), D_FF ** -0.5),
    }


def _fwd_reference(x, c, w_ada, b_ada, norm1_g, w_in, q_norm_g, k_norm_g, lam_re, lam_im, log_step,
              b_re, b_im, c_re, c_im, d_skip, w_glu, b_glu, attn_out_g, ssm_out_g, w_out,
              norm2_g, w_ff1, w_ff2):
    for l in range(DEPTH):
        x = hybrid_layer(x, c, w_ada[l], b_ada[l], norm1_g[l], w_in[l], q_norm_g[l], k_norm_g[l],
                         lam_re[l], lam_im[l], log_step[l], b_re[l], b_im[l], c_re[l], c_im[l],
                         d_skip[l], w_glu[l], b_glu[l], attn_out_g[l], ssm_out_g[l], w_out[l],
                         norm2_g[l], w_ff1[l], w_ff2[l])
    return x


import jax as _jax
import jax.numpy as _jnp

TWIN_FORMAT = 'train_step'
FWD_PARAMS = ['x', 'c', 'w_ada', 'b_ada', 'norm1_g', 'w_in', 'q_norm_g', 'k_norm_g', 'lam_re', 'lam_im', 'log_step', 'b_re', 'b_im', 'c_re', 'c_im', 'd_skip', 'w_glu', 'b_glu', 'attn_out_g', 'ssm_out_g', 'w_out', 'norm2_g', 'w_ff1', 'w_ff2']
TWIN_WEIGHTS = ['w_ada', 'b_ada', 'norm1_g', 'w_in', 'q_norm_g', 'k_norm_g', 'lam_re', 'lam_im', 'log_step', 'b_re', 'b_im', 'c_re', 'c_im', 'd_skip', 'w_glu', 'b_glu', 'attn_out_g', 'ssm_out_g', 'w_out', 'norm2_g', 'w_ff1', 'w_ff2']
TWIN_DIFF_INPUT = 'x'
TWIN_INPUTS = ['x', 'c', 'w_ada', 'b_ada', 'norm1_g', 'w_in', 'q_norm_g', 'k_norm_g', 'lam_re', 'lam_im', 'log_step', 'b_re', 'b_im', 'c_re', 'c_im', 'd_skip', 'w_glu', 'b_glu', 'attn_out_g', 'ssm_out_g', 'w_out', 'norm2_g', 'w_ff1', 'w_ff2', 'loss_target', 'm_w_ada', 'm_b_ada', 'm_norm1_g', 'm_w_in', 'm_q_norm_g', 'm_k_norm_g', 'm_lam_re', 'm_lam_im', 'm_log_step', 'm_b_re', 'm_b_im', 'm_c_re', 'm_c_im', 'm_d_skip', 'm_w_glu', 'm_b_glu', 'm_attn_out_g', 'm_ssm_out_g', 'm_w_out', 'm_norm2_g', 'm_w_ff1', 'm_w_ff2', 'v_w_ada', 'v_b_ada', 'v_norm1_g', 'v_w_in', 'v_q_norm_g', 'v_k_norm_g', 'v_lam_re', 'v_lam_im', 'v_log_step', 'v_b_re', 'v_b_im', 'v_c_re', 'v_c_im', 'v_d_skip', 'v_w_glu', 'v_b_glu', 'v_attn_out_g', 'v_ssm_out_g', 'v_w_out', 'v_norm2_g', 'v_w_ff1', 'v_w_ff2']
TWIN_OUTPUTS = ['loss', 'grad_x', 'grad_w_ada', 'grad_b_ada', 'grad_norm1_g', 'grad_w_in', 'grad_q_norm_g', 'grad_k_norm_g', 'grad_lam_re', 'grad_lam_im', 'grad_log_step', 'grad_b_re', 'grad_b_im', 'grad_c_re', 'grad_c_im', 'grad_d_skip', 'grad_w_glu', 'grad_b_glu', 'grad_attn_out_g', 'grad_ssm_out_g', 'grad_w_out', 'grad_norm2_g', 'grad_w_ff1', 'grad_w_ff2', 'delta_w_ada', 'delta_b_ada', 'delta_norm1_g', 'delta_w_in', 'delta_q_norm_g', 'delta_k_norm_g', 'delta_lam_re', 'delta_lam_im', 'delta_log_step', 'delta_b_re', 'delta_b_im', 'delta_c_re', 'delta_c_im', 'delta_d_skip', 'delta_w_glu', 'delta_b_glu', 'delta_attn_out_g', 'delta_ssm_out_g', 'delta_w_out', 'delta_norm2_g', 'delta_w_ff1', 'delta_w_ff2', 'new_m_w_ada', 'new_m_b_ada', 'new_m_norm1_g', 'new_m_w_in', 'new_m_q_norm_g', 'new_m_k_norm_g', 'new_m_lam_re', 'new_m_lam_im', 'new_m_log_step', 'new_m_b_re', 'new_m_b_im', 'new_m_c_re', 'new_m_c_im', 'new_m_d_skip', 'new_m_w_glu', 'new_m_b_glu', 'new_m_attn_out_g', 'new_m_ssm_out_g', 'new_m_w_out', 'new_m_norm2_g', 'new_m_w_ff1', 'new_m_w_ff2', 'new_v_w_ada', 'new_v_b_ada', 'new_v_norm1_g', 'new_v_w_in', 'new_v_q_norm_g', 'new_v_k_norm_g', 'new_v_lam_re', 'new_v_lam_im', 'new_v_log_step', 'new_v_b_re', 'new_v_b_im', 'new_v_c_re', 'new_v_c_im', 'new_v_d_skip', 'new_v_w_glu', 'new_v_b_glu', 'new_v_attn_out_g', 'new_v_ssm_out_g', 'new_v_w_out', 'new_v_norm2_g', 'new_v_w_ff1', 'new_v_w_ff2']
TWIN_LEAF_KINDS = {'loss': 'loss', 'grad_x': 'grad_x', 'grad_w_ada': 'grad_w', 'grad_b_ada': 'grad_w', 'grad_norm1_g': 'grad_w', 'grad_w_in': 'grad_w', 'grad_q_norm_g': 'grad_w', 'grad_k_norm_g': 'grad_w', 'grad_lam_re': 'grad_w', 'grad_lam_im': 'grad_w', 'grad_log_step': 'grad_w', 'grad_b_re': 'grad_w', 'grad_b_im': 'grad_w', 'grad_c_re': 'grad_w', 'grad_c_im': 'grad_w', 'grad_d_skip': 'grad_w', 'grad_w_glu': 'grad_w', 'grad_b_glu': 'grad_w', 'grad_attn_out_g': 'grad_w', 'grad_ssm_out_g': 'grad_w', 'grad_w_out': 'grad_w', 'grad_norm2_g': 'grad_w', 'grad_w_ff1': 'grad_w', 'grad_w_ff2': 'grad_w', 'delta_w_ada': 'delta_w', 'delta_b_ada': 'delta_w', 'delta_norm1_g': 'delta_w', 'delta_w_in': 'delta_w', 'delta_q_norm_g': 'delta_w', 'delta_k_norm_g': 'delta_w', 'delta_lam_re': 'delta_w', 'delta_lam_im': 'delta_w', 'delta_log_step': 'delta_w', 'delta_b_re': 'delta_w', 'delta_b_im': 'delta_w', 'delta_c_re': 'delta_w', 'delta_c_im': 'delta_w', 'delta_d_skip': 'delta_w', 'delta_w_glu': 'delta_w', 'delta_b_glu': 'delta_w', 'delta_attn_out_g': 'delta_w', 'delta_ssm_out_g': 'delta_w', 'delta_w_out': 'delta_w', 'delta_norm2_g': 'delta_w', 'delta_w_ff1': 'delta_w', 'delta_w_ff2': 'delta_w', 'new_m_w_ada': 'new_m', 'new_m_b_ada': 'new_m', 'new_m_norm1_g': 'new_m', 'new_m_w_in': 'new_m', 'new_m_q_norm_g': 'new_m', 'new_m_k_norm_g': 'new_m', 'new_m_lam_re': 'new_m', 'new_m_lam_im': 'new_m', 'new_m_log_step': 'new_m', 'new_m_b_re': 'new_m', 'new_m_b_im': 'new_m', 'new_m_c_re': 'new_m', 'new_m_c_im': 'new_m', 'new_m_d_skip': 'new_m', 'new_m_w_glu': 'new_m', 'new_m_b_glu': 'new_m', 'new_m_attn_out_g': 'new_m', 'new_m_ssm_out_g': 'new_m', 'new_m_w_out': 'new_m', 'new_m_norm2_g': 'new_m', 'new_m_w_ff1': 'new_m', 'new_m_w_ff2': 'new_m', 'new_v_w_ada': 'new_v', 'new_v_b_ada': 'new_v', 'new_v_norm1_g': 'new_v', 'new_v_w_in': 'new_v', 'new_v_q_norm_g': 'new_v', 'new_v_k_norm_g': 'new_v', 'new_v_lam_re': 'new_v', 'new_v_lam_im': 'new_v', 'new_v_log_step': 'new_v', 'new_v_b_re': 'new_v', 'new_v_b_im': 'new_v', 'new_v_c_re': 'new_v', 'new_v_c_im': 'new_v', 'new_v_d_skip': 'new_v', 'new_v_w_glu': 'new_v', 'new_v_b_glu': 'new_v', 'new_v_attn_out_g': 'new_v', 'new_v_ssm_out_g': 'new_v', 'new_v_w_out': 'new_v', 'new_v_norm2_g': 'new_v', 'new_v_w_ff1': 'new_v', 'new_v_w_ff2': 'new_v'}


def _forward(args):
    return _fwd_reference(*[args[k] for k in FWD_PARAMS])


def _output_shape():
    def fwd():
        inp = _fwd_setup_inputs(0)
        return _fwd_reference(*[inp[k] for k in FWD_PARAMS])
    out = _jax.eval_shape(fwd)
    return out.shape, out.dtype

N_MICROBATCH = 1
ADAM_LR = 0.001
ADAM_B1 = 0.9
ADAM_B2 = 0.999
ADAM_EPS = 1e-08
ADAM_WD = 0.01
ADAM_STEP = 10
PER_EXAMPLE_BATCH_AXIS = {'x': 0, 'c': 0, 'loss_target': 0}
SHARED_INPUTS = []
_WEIGHT_DTYPES = {'w_ada': _jnp.float32, 'b_ada': _jnp.float32, 'norm1_g': _jnp.float32, 'w_in': _jnp.float32, 'q_norm_g': _jnp.float32, 'k_norm_g': _jnp.float32, 'lam_re': _jnp.float32, 'lam_im': _jnp.float32, 'log_step': _jnp.float32, 'b_re': _jnp.float32, 'b_im': _jnp.float32, 'c_re': _jnp.float32, 'c_im': _jnp.float32, 'd_skip': _jnp.float32, 'w_glu': _jnp.float32, 'b_glu': _jnp.float32, 'attn_out_g': _jnp.float32, 'ssm_out_g': _jnp.float32, 'w_out': _jnp.float32, 'norm2_g': _jnp.float32, 'w_ff1': _jnp.float32, 'w_ff2': _jnp.float32}
MOMENT_SCALE = {'w_ada': 2.654420e+00, 'b_ada': 6.718745e+00, 'norm1_g': 8.450747e-02, 'w_in': 4.261902e-01, 'q_norm_g': 1.099751e-01, 'k_norm_g': 1.101253e-01, 'lam_re': 2.962450e-02, 'lam_im': 2.738275e-02, 'log_step': 2.186707e+00, 'b_re': 2.025846e-02, 'b_im': 1.950542e-02, 'c_re': 4.406405e-02, 'c_im': 3.855573e-02, 'd_skip': 8.734522e-01, 'w_glu': 1.320716e-01, 'b_glu': 3.781547e-01, 'attn_out_g': 3.514786e+00, 'ssm_out_g': 5.530736e+00, 'w_out': 8.646240e-01, 'norm2_g': 1.174439e+01, 'w_ff1': 3.092429e-01, 'w_ff2': 1.282067e+00}


def _to_microbatches(a, axis):
    t = _jnp.moveaxis(a, axis, 0)
    t = t.reshape((N_MICROBATCH, t.shape[0] // N_MICROBATCH) + t.shape[1:])
    return _jnp.moveaxis(t, 1, axis + 1)


def setup_inputs(seed: int = 0) -> dict:
    inp = _fwd_setup_inputs(seed)
    key = _jax.random.fold_in(_jax.random.key(seed), 7919)
    shape, _ = _output_shape()
    out = dict(inp)
    out["loss_target"] = _jax.random.normal(_jax.random.fold_in(key, 0), shape, _jnp.float32)
    for i, name in enumerate(TWIN_WEIGHTS):
        w = inp[name].astype(_jnp.float32)
        if MOMENT_SCALE is None:
            s = _jnp.sqrt(_jnp.mean(_jnp.square(w)) + 1e-30)
        else:
            s = MOMENT_SCALE[name]
        km, kv = _jax.random.split(_jax.random.fold_in(key, i + 1))
        out[name] = w
        out["m_" + name] = s * _jax.random.normal(km, w.shape, _jnp.float32)
        out["v_" + name] = (s * s) * _jax.random.uniform(kv, w.shape, _jnp.float32, 0.5, 1.5)
    if N_MICROBATCH > 1:
        for name, axis in PER_EXAMPLE_BATCH_AXIS.items():
            out[name] = _to_microbatches(out[name], axis)
    return {'x': out['x'], 'c': out['c'], 'w_ada': out['w_ada'], 'b_ada': out['b_ada'], 'norm1_g': out['norm1_g'], 'w_in': out['w_in'], 'q_norm_g': out['q_norm_g'], 'k_norm_g': out['k_norm_g'], 'lam_re': out['lam_re'], 'lam_im': out['lam_im'], 'log_step': out['log_step'], 'b_re': out['b_re'], 'b_im': out['b_im'], 'c_re': out['c_re'], 'c_im': out['c_im'], 'd_skip': out['d_skip'], 'w_glu': out['w_glu'], 'b_glu': out['b_glu'], 'attn_out_g': out['attn_out_g'], 'ssm_out_g': out['ssm_out_g'], 'w_out': out['w_out'], 'norm2_g': out['norm2_g'], 'w_ff1': out['w_ff1'], 'w_ff2': out['w_ff2'], 'loss_target': out['loss_target'], 'm_w_ada': out['m_w_ada'], 'm_b_ada': out['m_b_ada'], 'm_norm1_g': out['m_norm1_g'], 'm_w_in': out['m_w_in'], 'm_q_norm_g': out['m_q_norm_g'], 'm_k_norm_g': out['m_k_norm_g'], 'm_lam_re': out['m_lam_re'], 'm_lam_im': out['m_lam_im'], 'm_log_step': out['m_log_step'], 'm_b_re': out['m_b_re'], 'm_b_im': out['m_b_im'], 'm_c_re': out['m_c_re'], 'm_c_im': out['m_c_im'], 'm_d_skip': out['m_d_skip'], 'm_w_glu': out['m_w_glu'], 'm_b_glu': out['m_b_glu'], 'm_attn_out_g': out['m_attn_out_g'], 'm_ssm_out_g': out['m_ssm_out_g'], 'm_w_out': out['m_w_out'], 'm_norm2_g': out['m_norm2_g'], 'm_w_ff1': out['m_w_ff1'], 'm_w_ff2': out['m_w_ff2'], 'v_w_ada': out['v_w_ada'], 'v_b_ada': out['v_b_ada'], 'v_norm1_g': out['v_norm1_g'], 'v_w_in': out['v_w_in'], 'v_q_norm_g': out['v_q_norm_g'], 'v_k_norm_g': out['v_k_norm_g'], 'v_lam_re': out['v_lam_re'], 'v_lam_im': out['v_lam_im'], 'v_log_step': out['v_log_step'], 'v_b_re': out['v_b_re'], 'v_b_im': out['v_b_im'], 'v_c_re': out['v_c_re'], 'v_c_im': out['v_c_im'], 'v_d_skip': out['v_d_skip'], 'v_w_glu': out['v_w_glu'], 'v_b_glu': out['v_b_glu'], 'v_attn_out_g': out['v_attn_out_g'], 'v_ssm_out_g': out['v_ssm_out_g'], 'v_w_out': out['v_w_out'], 'v_norm2_g': out['v_norm2_g'], 'v_w_ff1': out['v_w_ff1'], 'v_w_ff2': out['v_w_ff2']}


def _loss(weights, diff, rest, loss_target):
    with _jax.named_scope("forward"):
        args = {**rest, TWIN_DIFF_INPUT: diff, **{k: w.astype(_WEIGHT_DTYPES[k]) for k, w in weights.items()}}
        y = _forward(args)
    with _jax.named_scope("loss_head"):
        err = _jnp.square(y.astype(_jnp.float32) - loss_target)
        return 0.5 * _jnp.sum(_jnp.mean(err, axis=-1)) if err.ndim else 0.5 * err


def _adamw(w, g, m, v):
    m = ADAM_B1 * m + (1.0 - ADAM_B1) * g
    v = ADAM_B2 * v + (1.0 - ADAM_B2) * _jnp.square(g)
    m_hat = m / (1.0 - ADAM_B1 ** ADAM_STEP)
    v_hat = v / (1.0 - ADAM_B2 ** ADAM_STEP)
    delta = -ADAM_LR * (m_hat / (_jnp.sqrt(v_hat) + ADAM_EPS) + ADAM_WD * w)
    return delta, m, v


def reference(x, c, w_ada, b_ada, norm1_g, w_in, q_norm_g, k_norm_g, lam_re, lam_im, log_step, b_re, b_im, c_re, c_im, d_skip, w_glu, b_glu, attn_out_g, ssm_out_g, w_out, norm2_g, w_ff1, w_ff2, loss_target, m_w_ada, m_b_ada, m_norm1_g, m_w_in, m_q_norm_g, m_k_norm_g, m_lam_re, m_lam_im, m_log_step, m_b_re, m_b_im, m_c_re, m_c_im, m_d_skip, m_w_glu, m_b_glu, m_attn_out_g, m_ssm_out_g, m_w_out, m_norm2_g, m_w_ff1, m_w_ff2, v_w_ada, v_b_ada, v_norm1_g, v_w_in, v_q_norm_g, v_k_norm_g, v_lam_re, v_lam_im, v_log_step, v_b_re, v_b_im, v_c_re, v_c_im, v_d_skip, v_w_glu, v_b_glu, v_attn_out_g, v_ssm_out_g, v_w_out, v_norm2_g, v_w_ff1, v_w_ff2):
    given = dict(x=x, c=c, w_ada=w_ada, b_ada=b_ada, norm1_g=norm1_g, w_in=w_in, q_norm_g=q_norm_g, k_norm_g=k_norm_g, lam_re=lam_re, lam_im=lam_im, log_step=log_step, b_re=b_re, b_im=b_im, c_re=c_re, c_im=c_im, d_skip=d_skip, w_glu=w_glu, b_glu=b_glu, attn_out_g=attn_out_g, ssm_out_g=ssm_out_g, w_out=w_out, norm2_g=norm2_g, w_ff1=w_ff1, w_ff2=w_ff2, loss_target=loss_target, m_w_ada=m_w_ada, m_b_ada=m_b_ada, m_norm1_g=m_norm1_g, m_w_in=m_w_in, m_q_norm_g=m_q_norm_g, m_k_norm_g=m_k_norm_g, m_lam_re=m_lam_re, m_lam_im=m_lam_im, m_log_step=m_log_step, m_b_re=m_b_re, m_b_im=m_b_im, m_c_re=m_c_re, m_c_im=m_c_im, m_d_skip=m_d_skip, m_w_glu=m_w_glu, m_b_glu=m_b_glu, m_attn_out_g=m_attn_out_g, m_ssm_out_g=m_ssm_out_g, m_w_out=m_w_out, m_norm2_g=m_norm2_g, m_w_ff1=m_w_ff1, m_w_ff2=m_w_ff2, v_w_ada=v_w_ada, v_b_ada=v_b_ada, v_norm1_g=v_norm1_g, v_w_in=v_w_in, v_q_norm_g=v_q_norm_g, v_k_norm_g=v_k_norm_g, v_lam_re=v_lam_re, v_lam_im=v_lam_im, v_log_step=v_log_step, v_b_re=v_b_re, v_b_im=v_b_im, v_c_re=v_c_re, v_c_im=v_c_im, v_d_skip=v_d_skip, v_w_glu=v_w_glu, v_b_glu=v_b_glu, v_attn_out_g=v_attn_out_g, v_ssm_out_g=v_ssm_out_g, v_w_out=v_w_out, v_norm2_g=v_norm2_g, v_w_ff1=v_w_ff1, v_w_ff2=v_w_ff2)
    weights = {n: given[n] for n in TWIN_WEIGHTS}
    shared = {n: given[n] for n in SHARED_INPUTS}
    per_example = {n: given[n] for n in ['x', 'c']}
    grad_fn = _jax.value_and_grad(_loss, argnums=(0, 1))

    def one_microbatch(ex, loss_target):
        ex = dict(ex)
        diff = ex.pop(TWIN_DIFF_INPUT)
        return grad_fn(weights, diff, {**shared, **ex}, loss_target)

    if N_MICROBATCH == 1:
        loss, (grad_w, grad_x) = one_microbatch(per_example, given["loss_target"])
    else:
        def body(carry, xs):
            loss_sum, grad_sum = carry
            l_k, (gw_k, gx_k) = one_microbatch(xs[0], xs[1])
            with _jax.named_scope("update"):
                return (loss_sum + l_k, _jax.tree.map(_jnp.add, grad_sum, gw_k)), gx_k

        init = (_jnp.zeros((), _jnp.float32), _jax.tree.map(_jnp.zeros_like, weights))
        (loss, grad_w), grad_x = _jax.lax.scan(body, init, (per_example, given["loss_target"]))
    with _jax.named_scope("update"):
        delta_w, new_m, new_v = {}, {}, {}
        for n in TWIN_WEIGHTS:
            delta_w[n], new_m[n], new_v[n] = _adamw(weights[n], grad_w[n], given["m_" + n], given["v_" + n])
    return (loss, grad_x, *[grad_w[n] for n in TWIN_WEIGHTS], *[delta_w[n] for n in TWIN_WEIGHTS],
            *[new_m[n] for n in TWIN_WEIGHTS], *[new_v[n] for n in TWIN_WEIGHTS])
```

```python
import math

import jax
import jax.numpy as jnp
from jax import lax
from jax.experimental import pallas as pl
from jax.experimental.pallas import tpu as pltpu

f32, bf16 = jnp.float32, jnp.bfloat16

N_DEV = 8
LANE = 128
HEAD = 128
SSM_GROUP = 16
GROUPS_PER_BLOCK = LANE // SSM_GROUP
DILATIONS = ((128, 1), (512, 4), (2048, 16))
BAND = 128
EPS = 1e-6
ADAM_LR, ADAM_B1, ADAM_B2, ADAM_EPS, ADAM_WD, ADAM_STEP = 0.001, 0.9, 0.999, 1e-08, 0.01, 10
NEG = -1e30
VMEM_LIMIT = 60 * 1024 * 1024
HI = lax.Precision.HIGHEST
MESH = pl.DeviceIdType.MESH


def _pcall(body, **kw):
    sem = kw.pop("sem", None)
    kw["compiler_params"] = pltpu.CompilerParams(dimension_semantics=sem, vmem_limit_bytes=VMEM_LIMIT)
    return pl.pallas_call(body, **kw)


def _tile(n, pref):
    t = min(n, pref)
    while n % t:
        t //= 2
    return t


def _sds(shape, dtype):
    return jax.ShapeDtypeStruct(shape, dtype)


def _rowwise(name, fn, steps, ins, in_specs, outs, out_specs, acc):
    n_in = len(ins)

    def body(*refs):
        res = fn(*[r[...] for r in refs[:n_in]])
        res = res if isinstance(res, (tuple, list)) else (res,)
        for r, o, a in zip(refs[n_in:], res, acc):
            if a:
                @pl.when(pl.program_id(0) == 0)
                def _():
                    r[...] = jnp.zeros_like(r)
                r[...] += o
            else:
                r[...] = o.astype(r.dtype)

    return _pcall(body, name=name, grid=(steps,), in_specs=in_specs, out_specs=out_specs, out_shape=outs,
                  sem=("arbitrary",))(*ins)


def _row(tm, c, blk=0):
    return pl.BlockSpec((tm, c), lambda i: (i, blk))


def _vec(c, blk=0):
    return pl.BlockSpec((1, c), lambda i: (0, blk))


def _rms(x, g):
    return x * lax.rsqrt(jnp.mean(x * x, axis=-1, keepdims=True) + EPS) * g


def _norm_mod(x, g, sc, sh):
    return _rms(x, g) * (1.0 + sc) + sh


def _head_rms(t, g):
    return jnp.concatenate([_rms(t[:, h * HEAD:(h + 1) * HEAD], g) for h in range(t.shape[1] // HEAD)], axis=1)


def _mix_fn(attn, yg, z, bglu, ga, gs):
    ssm = yg * jax.nn.sigmoid(z + bglu)
    return jnp.concatenate([_rms(attn, ga), _rms(ssm, gs)], axis=1)


def _ypre_fn(ymm, u, dskip):
    return jax.nn.gelu(ymm + dskip * u)


def _matmul(name, a, b, *, dims, grid, a_spec, b_spec, acc_shape, outs, out_specs, extra=(), extra_specs=(),
            epilogue=None):
    gk = grid[2]
    n_x = len(extra)

    def body(a_ref, b_ref, *rest):
        acc = rest[-1]
        k = pl.program_id(2)

        @pl.when(k == 0)
        def _():
            acc[...] = jnp.zeros_like(acc)

        acc[...] += lax.dot_general(a_ref[...].astype(bf16), b_ref[...].astype(bf16), (dims, ((), ())),
                                    preferred_element_type=f32)

        @pl.when(k == gk - 1)
        def _():
            xs = [r[...] for r in rest[:n_x]]
            res = epilogue(acc[...], *xs) if epilogue is not None else (acc[...],)
            for r, o in zip(rest[n_x:-1], res):
                r[...] = o.astype(r.dtype)

    return _pcall(body, name=name, grid=grid, in_specs=[a_spec, b_spec, *extra_specs], out_specs=out_specs,
                  out_shape=outs, scratch_shapes=[pltpu.VMEM(acc_shape, f32)],
                  sem=("parallel", "parallel", "arbitrary"))(a, b, *extra)


NN = ((1,), (0,))
NT = ((1,), (1,))
TN = ((0,), (0,))


def _mm_nn(name, a, b, out_dtype=f32, tm=1024, tn=1024, tk=2048, epilogue=None, extra=(), outs=None):
    m, kd = a.shape
    n = b.shape[1]
    tm, tn, tk = _tile(m, tm), _tile(n, tn), _tile(kd, tk)
    o_spec = pl.BlockSpec((tm, tn), lambda i, j, k: (i, j))
    outs = outs if outs is not None else [_sds((m, n), out_dtype)]
    return _matmul(name, a, b, dims=NN, grid=(m // tm, n // tn, kd // tk),
                   a_spec=pl.BlockSpec((tm, tk), lambda i, j, k: (i, k)),
                   b_spec=pl.BlockSpec((tk, tn), lambda i, j, k: (k, j)),
                   acc_shape=(tm, tn), outs=outs, out_specs=[o_spec] * len(outs),
                   extra=extra, extra_specs=[o_spec] * len(extra), epilogue=epilogue)


def _mm_nn_sharded(name, a, b3, out_dtype=f32, tm=1024, tk=2048, epilogue=None, outs=None):
    m, kd = a.shape
    nsh, _, n = b3.shape
    tm, tk = _tile(m, tm), _tile(kd, tk)
    o_spec = pl.BlockSpec((tm, n), lambda i, j, k: (i, j))
    outs = outs if outs is not None else [_sds((m, nsh * n), out_dtype)]
    return _matmul(name, a, b3, dims=NN, grid=(m // tm, nsh, kd // tk),
                   a_spec=pl.BlockSpec((tm, tk), lambda i, j, k: (i, k)),
                   b_spec=pl.BlockSpec((None, tk, n), lambda i, j, k: (j, k, 0)),
                   acc_shape=(tm, n), outs=outs, out_specs=[o_spec] * len(outs), epilogue=epilogue)


def _mm_nt(name, a, b, out_dtype=f32, tm=1024, tn=1024, tk=1024, epilogue=None, extra=(), outs=None):
    m, kd = a.shape
    n = b.shape[0]
    tm, tn, tk = _tile(m, tm), _tile(n, tn), _tile(kd, tk)
    o_spec = pl.BlockSpec((tm, tn), lambda i, j, k: (i, j))
    outs = outs if outs is not None else [_sds((m, n), out_dtype)]
    return _matmul(name, a, b, dims=NT, grid=(m // tm, n // tn, kd // tk),
                   a_spec=pl.BlockSpec((tm, tk), lambda i, j, k: (i, k)),
                   b_spec=pl.BlockSpec((tn, tk), lambda i, j, k: (j, k)),
                   acc_shape=(tm, tn), outs=outs, out_specs=[o_spec] * len(outs),
                   extra=extra, extra_specs=[o_spec] * len(extra), epilogue=epilogue)


def _mm_nt_sharded(name, a, b3, out_dtype=f32, tm=512, tn=2048):
    m = a.shape[0]
    nsh, n_out, n = b3.shape
    tm, tn = _tile(m, tm), _tile(n_out, tn)
    return _matmul(name, a, b3, dims=NT, grid=(m // tm, n_out // tn, nsh),
                   a_spec=pl.BlockSpec((tm, n), lambda i, j, k: (i, k)),
                   b_spec=pl.BlockSpec((None, tn, n), lambda i, j, k: (k, j, 0)),
                   acc_shape=(tm, tn), outs=[_sds((m, n_out), out_dtype)],
                   out_specs=[pl.BlockSpec((tm, tn), lambda i, j, k: (i, j))])[0]


def _mm_tn(name, a, b, out_dtype=bf16, tm=1024, tn=1024, tk=1024):
    t, m = a.shape
    n = b.shape[1]
    tm, tn, tk = _tile(m, tm), _tile(n, tn), _tile(t, tk)
    return _matmul(name, a, b, dims=TN, grid=(m // tm, n // tn, t // tk),
                   a_spec=pl.BlockSpec((tk, tm), lambda i, j, k: (k, i)),
                   b_spec=pl.BlockSpec((tk, tn), lambda i, j, k: (k, j)),
                   acc_shape=(tm, tn), outs=[_sds((m, n), out_dtype)],
                   out_specs=[pl.BlockSpec((tm, tn), lambda i, j, k: (i, j))])[0]


def _mm_tn_sharded(name, a, b, nsh, out_dtype=bf16, tm=1024, tk=1024):
    t, m = a.shape
    n = b.shape[1] // nsh
    tm, tk = _tile(m, tm), _tile(t, tk)
    return _matmul(name, a, b, dims=TN, grid=(m // tm, nsh, t // tk),
                   a_spec=pl.BlockSpec((tk, tm), lambda i, j, k: (k, i)),
                   b_spec=pl.BlockSpec((tk, n), lambda i, j, k: (k, j)),
                   acc_shape=(tm, n), outs=[_sds((nsh, m, n), out_dtype)],
                   out_specs=[pl.BlockSpec((None, tm, n), lambda i, j, k: (j, i, 0))])[0]


def _bd_nn(name, a_list, w_list, ka, nb, out_dtype=f32, tm=1024):
    n_q = len(a_list)
    m = a_list[0].shape[0]
    ngs = w_list[0].shape[0]
    tm = _tile(m, tm)

    def body(*refs):
        o_ref = refs[-1]
        tot = None
        for q in range(n_q):
            p = jnp.dot(refs[q][...].astype(bf16), refs[n_q + q][...], preferred_element_type=f32)
            tot = p if tot is None else tot + p
        o_ref[...] = tot.astype(o_ref.dtype)

    return _pcall(body, name=name, grid=(m // tm, ngs),
                  in_specs=[pl.BlockSpec((tm, ka), lambda i, s: (i, s))] * n_q
                  + [pl.BlockSpec((None, ka, nb), lambda i, s: (s, 0, 0))] * n_q,
                  out_specs=pl.BlockSpec((tm, nb), lambda i, s: (i, s)),
                  out_shape=_sds((m, ngs * nb), out_dtype), sem=("parallel", "parallel"))(*a_list, *w_list)


def _bd_tn(name, a, b_list, ra, cb, tk=1024):
    n_q = len(b_list)
    t = a.shape[0]
    ngs = a.shape[1] // ra
    tk = _tile(t, tk)

    def body(*refs):
        a_t = refs[0][...].astype(bf16)
        for q in range(n_q):
            o_ref = refs[1 + n_q + q]

            @pl.when(pl.program_id(1) == 0)
            def _():
                o_ref[...] = jnp.zeros_like(o_ref)

            o_ref[...] += lax.dot_general(a_t, refs[1 + q][...].astype(bf16), (TN, ((), ())),
                                          preferred_element_type=f32)

    return _pcall(body, name=name, grid=(ngs, t // tk),
                  in_specs=[pl.BlockSpec((tk, ra), lambda s, k: (k, s))]
                  + [pl.BlockSpec((tk, cb), lambda s, k: (k, s))] * n_q,
                  out_specs=[pl.BlockSpec((None, ra, cb), lambda s, k: (s, 0, 0))] * n_q,
                  out_shape=[_sds((ngs, ra, cb), f32)] * n_q, sem=("parallel", "arbitrary"))(a, *b_list)


def _bd_weight(t4):
    ngs, gb, r, c = t4.shape
    eye = jnp.eye(gb, dtype=t4.dtype)
    return jnp.einsum("sgrc,gh->sgrhc", t4, eye).reshape(ngs, gb * r, gb * c).astype(bf16)


def _bd_diag(w, r, c):
    ngs = w.shape[0]
    gb = w.shape[1] // r
    w5 = w.reshape(ngs, gb, r, gb, c)
    return jnp.einsum("sgrhc,gh->sgrc", w5, jnp.eye(gb, dtype=w.dtype))


def _scan(name, xr, xi, a2, ends=None, *, reverse, write_h, da_from=None):
    s_len, gp = xr.shape
    seg = s_len // 8
    n_sq = int(math.log2(seg))
    assert 2 ** n_sq == seg
    with_da = da_from is not None

    def carry_in(e_ref, pr, pi, rev):
        zero = jnp.zeros((1, LANE), f32)
        rows_r, rows_i = [None] * 8, [None] * 8
        order = range(7, -1, -1) if rev else range(8)
        cr, ci = zero, zero
        for j in order:
            rows_r[j], rows_i[j] = cr, ci
            er, ei = e_ref[0, j:j + 1, :], e_ref[1, j:j + 1, :]
            cr, ci = er + pr * cr - pi * ci, ei + pr * ci + pi * cr
        return jnp.concatenate(rows_r, axis=0), jnp.concatenate(rows_i, axis=0)

    def seg_power(ar, ai):
        pr, pi = ar, ai
        for _ in range(n_sq):
            pr, pi = pr * pr - pi * pi, 2.0 * pr * pi
        return pr, pi

    def body(*refs):
        it = iter(refs)
        xr_ref, xi_ref, a_ref = next(it), next(it), next(it)
        e_ref = next(it) if ends is not None else None
        if with_da:
            hr_ref, hi_ref, ef_ref = next(it), next(it), next(it)
        if write_h:
            or_ref, oi_ref = next(it), next(it)
        end_ref = next(it)
        if with_da:
            da_ref = next(it)

        ar, ai_f = a_ref[0:1, :], a_ref[1:2, :]
        ai = -ai_f if reverse else ai_f
        arb, aib = jnp.broadcast_to(ar, (8, LANE)), jnp.broadcast_to(ai, (8, LANE))
        if ends is not None:
            pr, pi = seg_power(ar, ai)
            h0r, h0i = carry_in(e_ref, pr, pi, reverse)
        else:
            h0r = h0i = jnp.zeros((8, LANE), f32)

        def rows(k):
            return pl.ds(k, 8, stride=seg)

        def step(n, carry):
            k = seg - 1 - n if reverse else n
            if with_da:
                hr, hi, dr, di = carry
            else:
                hr, hi = carry
            nr = arb * hr - aib * hi + xr_ref[rows(k), :]
            ni = arb * hi + aib * hr + xi_ref[rows(k), :]
            if write_h:
                or_ref[rows(k), :] = nr
                oi_ref[rows(k), :] = ni
            if with_da:
                pr_, pi_ = hr_ref[rows(k - 1), :], hi_ref[rows(k - 1), :]
                return nr, ni, dr + nr * pr_ + ni * pi_, di + ni * pr_ - nr * pi_
            return nr, ni

        if with_da:
            z = jnp.zeros((8, LANE), f32)
            hr, hi, dr, di = lax.fori_loop(0, seg - 1, step, (h0r, h0i, z, z))
            nr = arb * hr - aib * hi + xr_ref[rows(0), :]
            ni = arb * hi + aib * hr + xi_ref[rows(0), :]
            or_ref[rows(0), :] = nr
            oi_ref[rows(0), :] = ni
            fpr, fpi = seg_power(ar, ai_f)
            pr_, pi_ = carry_in(ef_ref, fpr, fpi, False)
            dr = dr + nr * pr_ + ni * pi_
            di = di + ni * pr_ - nr * pi_
            da_ref[0] = jnp.sum(dr, axis=0, keepdims=True)
            da_ref[1] = jnp.sum(di, axis=0, keepdims=True)
            hr, hi = nr, ni
        else:
            hr, hi = lax.fori_loop(0, seg, step, (h0r, h0i))
        end_ref[0] = hr
        end_ref[1] = hi

    col = pl.BlockSpec((s_len, LANE), lambda l: (0, l))
    e_spec = pl.BlockSpec((2, 8, LANE), lambda l: (0, 0, l))
    ins, in_specs = [xr, xi, a2], [col, col, pl.BlockSpec((8, LANE), lambda l: (0, l))]
    if ends is not None:
        ins.append(ends)
        in_specs.append(e_spec)
    if with_da:
        ins += list(da_from)
        in_specs += [col, col, e_spec]
    outs, out_specs = [], []
    if write_h:
        outs += [_sds((s_len, gp), f32)] * 2
        out_specs += [col, col]
    outs.append(_sds((2, 8, gp), f32))
    out_specs.append(e_spec)
    if with_da:
        outs.append(_sds((2, 1, gp), f32))
        out_specs.append(pl.BlockSpec((2, 1, LANE), lambda l: (0, 0, l)))
    return _pcall(body, name=name, grid=(gp // LANE,), in_specs=in_specs, out_specs=out_specs, out_shape=outs,
                  sem=("parallel",))(*ins)


def _ssm_param_fn(lam_re, lam_im, log_step, b_re2, b_im2, expand):
    step = jnp.exp(log_step)
    xr, xi = lam_re * step, lam_im * step
    mag = jnp.exp(xr)
    ar, ai = mag * jnp.cos(xi), mag * jnp.sin(xi)
    nr, ni = ar - 1.0, ai
    den = lam_re * lam_re + lam_im * lam_im
    cr = (nr * lam_re + ni * lam_im) / den
    ci = (ni * lam_re - nr * lam_im) / den
    cre = jnp.dot(cr, expand, precision=HI, preferred_element_type=f32)
    cie = jnp.dot(ci, expand, precision=HI, preferred_element_type=f32)
    return ar, ai, cre * b_re2 - cie * b_im2, cre * b_im2 + cie * b_re2


def _ssm_params(lam_re, lam_im, log_step, b_re2, b_im2, expand):
    def body(*refs):
        res = _ssm_param_fn(*[r[...] for r in refs[:6]])
        for r, o in zip(refs[6:], res):
            r[...] = o

    g, p = lam_re.shape
    return _pcall(body, name="ssm_params", out_shape=[_sds((g, p), f32)] * 2 + [_sds(b_re2.shape, f32)] * 2)(
        lam_re, lam_im, log_step, b_re2, b_im2, expand)


def _ssm_params_bwd(lam_re, lam_im, log_step, b_re2, b_im2, expand, d_ar, d_ai, d_bbr, d_bbi):
    def body(*refs):
        prim = [r[...] for r in refs[:5]]
        ex = refs[5][...]
        cot = tuple(r[...] for r in refs[6:10])
        _, vjp = jax.vjp(lambda *p_: _ssm_param_fn(*p_, ex), *prim)
        for r, o in zip(refs[10:], vjp(cot)):
            r[...] = o

    shapes = [lam_re.shape, lam_im.shape, log_step.shape, b_re2.shape, b_im2.shape]
    return _pcall(body, name="ssm_params_bwd", out_shape=[_sds(s, f32) for s in shapes])(
        lam_re, lam_im, log_step, b_re2, b_im2, expand, d_ar, d_ai, d_bbr, d_bbi)


def _slope_table(n_heads):
    s = 2.0 ** (-8.0 * (jnp.arange(n_heads, dtype=f32) + 1.0) / n_heads)
    return jnp.broadcast_to(s[:, None, None], (n_heads, 1, LANE))


def _band_scores(q, k, slope_d, shift):
    s = lax.dot_general(q, k, (NT, ((), ())), preferred_element_type=f32) * (HEAD ** -0.5)
    qi = lax.broadcasted_iota(jnp.int32, (BAND, BAND), 0)
    ki = lax.broadcasted_iota(jnp.int32, (BAND, BAND), 1)
    steps = (qi - ki + shift).astype(f32)
    mask = (ki >= qi) if shift else (ki <= qi)
    return s - slope_d * steps, mask


def _attn_fwd(qn, kn, vb, dil, slopes):
    s_len, aw = qn.shape
    n_heads = aw // HEAD
    length = s_len // dil
    nb = length // BAND
    q2, k2, v2 = (t.reshape(length, dil * aw) for t in (qn, kn, vb))

    def body(q_ref, kp_ref, kc_ref, vp_ref, vc_ref, sl_ref, o_ref, lse_ref):
        j = pl.program_id(1)
        slope_d = sl_ref[:, 0:1] * float(dil)
        q = q_ref[...]
        s_c, m_c = _band_scores(q, kc_ref[...], slope_d, 0)
        s_p, m_p = _band_scores(q, kp_ref[...], slope_d, BAND)
        s_c = jnp.where(m_c, s_c, NEG)
        s_p = jnp.where(m_p & (j > 0), s_p, NEG)
        m = jnp.maximum(jnp.max(s_c, axis=1, keepdims=True), jnp.max(s_p, axis=1, keepdims=True))
        p_c, p_p = jnp.exp(s_c - m), jnp.exp(s_p - m)
        den = jnp.sum(p_c, axis=1, keepdims=True) + jnp.sum(p_p, axis=1, keepdims=True)
        o = jnp.dot(p_c.astype(bf16), vc_ref[...], preferred_element_type=f32)
        o += jnp.dot(p_p.astype(bf16), vp_ref[...], preferred_element_type=f32)
        o_ref[...] = o / den
        lse_ref[...] = jnp.broadcast_to(m + jnp.log(den), (BAND, HEAD))

    cur = pl.BlockSpec((BAND, HEAD), lambda s, j: (j, s))
    prev = pl.BlockSpec((BAND, HEAD), lambda s, j: (jnp.maximum(j - 1, 0), s))
    sl = pl.BlockSpec((None, 1, LANE), lambda s, j: (s % n_heads, 0, 0))
    o, lse = _pcall(body, name=f"attn_fwd_d{dil}", grid=(dil * n_heads, nb),
                    in_specs=[cur, prev, cur, prev, cur, sl], out_specs=[cur, cur],
                    out_shape=[_sds((length, dil * aw), f32)] * 2, sem=("parallel", "parallel"))(
        q2, k2, k2, v2, v2, slopes)
    return o.reshape(s_len, aw), lse.reshape(s_len, aw)


def _attn_bwd(qn, kn, vb, do, lse, dd, dil, slopes):
    s_len, aw = qn.shape
    n_heads = aw // HEAD
    length = s_len // dil
    nb = length // BAND
    q2, k2, v2, do2, lse2, dd2 = (t.reshape(length, dil * aw) for t in (qn, kn, vb, do, lse, dd))
    scale = HEAD ** -0.5

    def body(q_ref, qn_ref, kp_ref, kc_ref, vp_ref, vc_ref, do_ref, don_ref, l_ref, ln_ref, d_ref, dn_ref, sl_ref,
             dq_ref, dk_ref, dv_ref):
        j = pl.program_id(1)
        slope_d = sl_ref[:, 0:1] * float(dil)

        def block(q, k, v, do_, l_col, d_col, shift, live):
            s, msk = _band_scores(q, k, slope_d, shift)
            p = jnp.where(msk & live, jnp.exp(s - l_col), 0.0)
            dp = lax.dot_general(do_, v, (NT, ((), ())), preferred_element_type=f32)
            return p.astype(bf16), (p * (dp - d_col)).astype(bf16)

        q, qn_, kc, kp, vc, vp = q_ref[...], qn_ref[...], kc_ref[...], kp_ref[...], vc_ref[...], vp_ref[...]
        do_, don = do_ref[...], don_ref[...]
        l_c, l_n = l_ref[:, 0:1], ln_ref[:, 0:1]
        d_c, d_n = d_ref[:, 0:1], dn_ref[:, 0:1]
        p_c, ds_c = block(q, kc, vc, do_, l_c, d_c, 0, True)
        p_p, ds_p = block(q, kp, vp, do_, l_c, d_c, BAND, j > 0)
        p_n, ds_n = block(qn_, kc, vc, don, l_n, d_n, BAND, j < nb - 1)
        dq = jnp.dot(ds_c, kc, preferred_element_type=f32) + jnp.dot(ds_p, kp, preferred_element_type=f32)
        dk = lax.dot_general(ds_c, q, (TN, ((), ())), preferred_element_type=f32)
        dk += lax.dot_general(ds_n, qn_, (TN, ((), ())), preferred_element_type=f32)
        dv = lax.dot_general(p_c, do_, (TN, ((), ())), preferred_element_type=f32)
        dv += lax.dot_general(p_n, don, (TN, ((), ())), preferred_element_type=f32)
        dq_ref[...] = dq * scale
        dk_ref[...] = dk * scale
        dv_ref[...] = dv

    cur = pl.BlockSpec((BAND, HEAD), lambda s, j: (j, s))
    prev = pl.BlockSpec((BAND, HEAD), lambda s, j: (jnp.maximum(j - 1, 0), s))
    nxt = pl.BlockSpec((BAND, HEAD), lambda s, j: (jnp.minimum(j + 1, nb - 1), s))
    sl = pl.BlockSpec((None, 1, LANE), lambda s, j: (s % n_heads, 0, 0))
    outs = _pcall(body, name=f"attn_bwd_d{dil}", grid=(dil * n_heads, nb),
                  in_specs=[cur, nxt, prev, cur, prev, cur, cur, nxt, cur, nxt, cur, nxt, sl],
                  out_specs=[cur] * 3, out_shape=[_sds((length, dil * aw), f32)] * 3,
                  sem=("parallel", "parallel"))(q2, q2, k2, k2, v2, v2, do2, do2, lse2, lse2, dd2, dd2, slopes)
    return [t.reshape(s_len, aw) for t in outs]


def _exchange(name, srcs, scatter):
    n = len(srcs)

    def body(*refs):
        src, out = refs[:n], refs[n:2 * n]
        send_sems, recv_sems, local_sems = refs[2 * n:]
        x, y, c = lax.axis_index("x"), lax.axis_index("y"), lax.axis_index("c")
        me = 4 * x + 2 * y + c

        def peer(r):
            return ((1 - x) if r & 4 else x, (1 - y) if r & 2 else y, (1 - c) if r & 1 else c)

        def lin(p):
            return 4 * p[0] + 2 * p[1] + p[2]

        def piece(a, idx):
            return src[a].at[idx] if scatter[a] else src[a]

        local, sends = [], []
        for a in range(n):
            cp = pltpu.make_async_copy(piece(a, me), out[a].at[me], local_sems.at[a])
            cp.start()
            local.append(cp)
        for r in range(1, N_DEV):
            p = peer(r)
            for a in range(n):
                cp = pltpu.make_async_remote_copy(src_ref=piece(a, lin(p)), dst_ref=out[a].at[me],
                                                  send_sem=send_sems.at[a, r - 1], recv_sem=recv_sems.at[a, r - 1],
                                                  device_id=p, device_id_type=MESH)
                cp.start()
                sends.append(cp)
        for r in range(1, N_DEV):
            p = peer(r)
            for a in range(n):
                pltpu.make_async_remote_copy(src_ref=piece(a, lin(p)), dst_ref=out[a].at[lin(p)],
                                             send_sem=send_sems.at[a, r - 1], recv_sem=recv_sems.at[a, r - 1],
                                             device_id=p, device_id_type=MESH).wait_recv()
        for cp in sends:
            cp.wait_send()
        for cp in local:
            cp.wait()

    def piece_shape(a):
        return srcs[a].shape[1:] if scatter[a] else srcs[a].shape

    any_spec = pl.BlockSpec(memory_space=pl.ANY)
    return _pcall(body, name=name, in_specs=[any_spec] * n, out_specs=[any_spec] * n,
                  out_shape=[_sds((N_DEV, *piece_shape(a)), srcs[a].dtype) for a in range(n)],
                  scratch_shapes=[pltpu.SemaphoreType.DMA((n, N_DEV - 1)), pltpu.SemaphoreType.DMA((n, N_DEV - 1)),
                                  pltpu.SemaphoreType.DMA((n,))])(*srcs)


def _adamw(name, w, m, v, g_or_stack, stacked, rows=256):
    r, c = w.shape
    tr = _tile(r, rows)

    def fn(w_, m_, v_, g_):
        if stacked:
            g = g_[0].astype(f32)
            for j in range(1, N_DEV):
                g = g + g_[j].astype(f32)
        else:
            g = g_
        m_new = ADAM_B1 * m_ + (1.0 - ADAM_B1) * g
        v_new = ADAM_B2 * v_ + (1.0 - ADAM_B2) * (g * g)
        m_hat = m_new / (1.0 - ADAM_B1 ** ADAM_STEP)
        v_hat = v_new / (1.0 - ADAM_B2 ** ADAM_STEP)
        delta = -ADAM_LR * (m_hat / (jnp.sqrt(v_hat) + ADAM_EPS) + ADAM_WD * w_)
        return g, delta, m_new, v_new

    blk = _row(tr, c)
    g_spec = pl.BlockSpec((N_DEV, tr, c), lambda i: (0, i, 0)) if stacked else blk
    return _rowwise(name, fn, r // tr, [w, m, v, g_or_stack], [blk, blk, blk, g_spec],
                    [_sds((r, c), f32)] * 4, [blk] * 4, [False] * 4)


def _ada_fwd(c_all, w_shard, b_shard):
    nb_, d = c_all.shape
    n = w_shard.shape[1]
    tn = _tile(n, 512)

    def body(c_ref, w_ref, b_ref, o_ref):
        a = jax.nn.silu(c_ref[...]).astype(bf16)
        o_ref[...] = jnp.dot(a, w_ref[...].astype(bf16), preferred_element_type=f32) + b_ref[...]

    return _pcall(body, name="ada_fwd", grid=(n // tn,),
                  in_specs=[pl.BlockSpec((nb_, d), lambda j: (0, 0)), pl.BlockSpec((d, tn), lambda j: (0, j)),
                            pl.BlockSpec((1, tn), lambda j: (0, j))],
                  out_specs=pl.BlockSpec((nb_, tn), lambda j: (0, j)), out_shape=_sds((nb_, n), f32),
                  sem=("parallel",))(c_all, w_shard, b_shard)


def _ada_bwd(c_all, dmod_cols):
    nb_, d = c_all.shape
    n = dmod_cols.shape[1]
    tn = _tile(n, 512)

    def body(c_ref, g_ref, o_ref):
        a = jax.nn.silu(c_ref[...]).astype(bf16).astype(f32)
        g = g_ref[...].astype(bf16).astype(f32)
        o_ref[...] = lax.dot_general(a, g, (TN, ((), ())), precision=HI, preferred_element_type=f32)

    return _pcall(body, name="ada_bwd", grid=(n // tn,),
                  in_specs=[pl.BlockSpec((nb_, d), lambda j: (0, 0)), pl.BlockSpec((nb_, tn), lambda j: (0, j))],
                  out_specs=pl.BlockSpec((d, tn), lambda j: (0, j)), out_shape=_sds((d, n), f32),
                  sem=("parallel",))(c_all, dmod_cols)


SMALL = ("b_ada", "norm1_g", "q_norm_g", "k_norm_g", "lam_re", "lam_im", "log_step", "b_re", "b_im", "c_re", "c_im",
         "d_skip", "b_glu", "attn_out_g", "ssm_out_g", "norm2_g")
BIG = ("w_ada", "w_in", "w_glu", "w_out", "w_ff1", "w_ff2")
ORDER = ("w_ada", "b_ada", "norm1_g", "w_in", "q_norm_g", "k_norm_g", "lam_re", "lam_im", "log_step", "b_re", "b_im",
         "c_re", "c_im", "d_skip", "w_glu", "b_glu", "attn_out_g", "ssm_out_g", "w_out", "norm2_g", "w_ff1", "w_ff2")


def _pack(parts):
    flat = jnp.concatenate([p.reshape(-1) for p in parts])
    pad = (-flat.shape[0]) % (8 * LANE)
    return jnp.pad(flat, (0, pad)).reshape(-1, LANE)


def kernel(x, c, w_ada, b_ada, norm1_g, w_in, q_norm_g, k_norm_g, lam_re, lam_im, log_step, b_re, b_im, c_re, c_im, d_skip, w_glu, b_glu, attn_out_g, ssm_out_g, w_out, norm2_g, w_ff1, w_ff2, loss_target, m_w_ada, m_b_ada, m_norm1_g, m_w_in, m_q_norm_g, m_k_norm_g, m_lam_re, m_lam_im, m_log_step, m_b_re, m_b_im, m_c_re, m_c_im, m_d_skip, m_w_glu, m_b_glu, m_attn_out_g, m_ssm_out_g, m_w_out, m_norm2_g, m_w_ff1, m_w_ff2, v_w_ada, v_b_ada, v_norm1_g, v_w_in, v_q_norm_g, v_k_norm_g, v_lam_re, v_lam_im, v_log_step, v_b_re, v_b_im, v_c_re, v_c_im, v_d_skip, v_w_glu, v_b_glu, v_attn_out_g, v_ssm_out_g, v_w_out, v_norm2_g, v_w_ff1, v_w_ff2):
    env = dict(locals())
    wts = {n: env[n] for n in ORDER}
    mom = {n: env["m_" + n] for n in ORDER}
    var = {n: env["v_" + n] for n in ORDER}

    xs, tgt = x[0], loss_target[0]
    s_len, d = xs.shape
    aw = d // 2
    sw = d - aw
    n_heads = aw // HEAD
    n_groups = sw // SSM_GROUP
    n_state = lam_re.shape[-1]
    ngs = sw // LANE
    gp = n_groups * n_state
    sb = GROUPS_PER_BLOCK * n_state
    tm = _tile(s_len, 256)
    steps = s_len // tm
    me = 4 * lax.axis_index("x") + 2 * lax.axis_index("y") + lax.axis_index("c")

    win_g, wglu_g, wout_g, wff1_g, wff2_g, c_all = _exchange(
        "gather_weights",
        [w_in[0].astype(bf16), w_glu[0].astype(bf16), w_out[0].astype(bf16), w_ff1[0].astype(bf16),
         w_ff2[0].astype(bf16), c], [False] * 6)
    wglu_g = wglu_g.reshape(sw, sw)
    wout_g = wout_g.reshape(d, d)
    wff2_g = wff2_g.reshape(4 * d, d)
    c_all = c_all.reshape(N_DEV, d)

    n_ada = w_ada.shape[-1]
    b_ada_cols = lax.dynamic_slice_in_dim(b_ada, me * n_ada, n_ada, axis=1)
    mod_part = _ada_fwd(c_all, w_ada[0], b_ada_cols)
    (mod_all,) = _exchange("gather_mod", [mod_part], [False])
    mod = lax.dynamic_index_in_dim(mod_all, me, axis=1, keepdims=False).reshape(1, 6 * d)
    sh1, sc1, g1, sh2, sc2, g2 = (mod[:, i * d:(i + 1) * d] for i in range(6))

    (h,) = _rowwise("norm1", _norm_mod, steps, [xs, norm1_g, sc1, sh1],
                    [_row(tm, d), _vec(d), _vec(d), _vec(d)], [_sds((s_len, d), bf16)], [_row(tm, d)], [False])
    (proj,) = _mm_nn_sharded("in_proj", h, win_g)

    def qkv_fn(q, k, v, gq, gk):
        return _head_rms(q, gq), _head_rms(k, gk), v

    qn, kn, vb = _rowwise("qk_norm", qkv_fn, steps, [proj, proj, proj, q_norm_g, k_norm_g],
                          [_row(tm, aw, 0), _row(tm, aw, 1), _row(tm, aw, 2), _vec(HEAD), _vec(HEAD)],
                          [_sds((s_len, aw), bf16)] * 3, [_row(tm, aw)] * 3, [False] * 3)

    slopes = _slope_table(n_heads)
    pat = [_attn_fwd(qn, kn, vb, dil, slopes) for _, dil in DILATIONS]

    def attn_mix_fn(o1, l1, o2, l2, o3, l3):
        m = jnp.maximum(jnp.maximum(l1, l2), l3)
        e1, e2, e3 = jnp.exp(l1 - m), jnp.exp(l2 - m), jnp.exp(l3 - m)
        tot = e1 + e2 + e3
        return (e1 * o1 + e2 * o2 + e3 * o3) / tot, m + jnp.log(tot)

    attn, lse = _rowwise("attn_mix", attn_mix_fn, steps, [t for ol in pat for t in ol], [_row(tm, aw)] * 6,
                         [_sds((s_len, aw), f32)] * 2, [_row(tm, aw)] * 2, [False] * 2)

    lam_re2, lam_im2 = lam_re[0], lam_im[0]
    log_step2 = log_step[0].reshape(n_groups, 1)
    b_re2 = b_re[0].reshape(n_groups, n_state * SSM_GROUP)
    b_im2 = b_im[0].reshape(n_groups, n_state * SSM_GROUP)
    expand = jnp.repeat(jnp.eye(n_state, dtype=f32), SSM_GROUP, axis=1)
    a_re, a_im, bb_re2, bb_im2 = _ssm_params(lam_re2, lam_im2, log_step2, b_re2, b_im2, expand)
    a2 = jnp.zeros((8, gp), f32).at[0].set(a_re.reshape(gp)).at[1].set(a_im.reshape(gp))

    def by_block(t):
        return t.reshape(ngs, GROUPS_PER_BLOCK, *t.shape[1:])

    bb_re4 = by_block(bb_re2.reshape(n_groups, n_state, SSM_GROUP))
    bb_im4 = by_block(bb_im2.reshape(n_groups, n_state, SSM_GROUP))
    c_re4, c_im4 = by_block(c_re[0]), by_block(c_im[0])
    w_bu_re = _bd_weight(bb_re4.transpose(0, 1, 3, 2))
    w_bu_im = _bd_weight(bb_im4.transpose(0, 1, 3, 2))
    w_y_re = _bd_weight(c_re4.transpose(0, 1, 3, 2))
    w_y_im = _bd_weight(-c_im4.transpose(0, 1, 3, 2))
    w_g_re, w_g_im = _bd_weight(c_re4), _bd_weight(-c_im4)
    w_du_re, w_du_im = _bd_weight(bb_re4), _bd_weight(bb_im4)

    u_blk = 3 * aw // LANE

    def bd_u(name, w):
        ngs_ = w.shape[0]

        def body(u_ref, w_ref, o_ref):
            o_ref[...] = jnp.dot(u_ref[...].astype(bf16), w_ref[...], preferred_element_type=f32)

        tmm = _tile(s_len, 1024)
        return _pcall(body, name=name, grid=(s_len // tmm, ngs_),
                      in_specs=[pl.BlockSpec((tmm, LANE), lambda i, s: (i, u_blk + s)),
                                pl.BlockSpec((None, LANE, sb), lambda i, s: (s, 0, 0))],
                      out_specs=pl.BlockSpec((tmm, sb), lambda i, s: (i, s)), out_shape=_sds((s_len, gp), f32),
                      sem=("parallel", "parallel"))(proj, w)

    bu_re, bu_im = bd_u("ssm_bu_re", w_bu_re), bd_u("ssm_bu_im", w_bu_im)
    (ends_f,) = _scan("ssm_scan_ends", bu_re, bu_im, a2, reverse=False, write_h=False)
    h_re, h_im, _ = _scan("ssm_scan", bu_re, bu_im, a2, ends_f, reverse=False, write_h=True)
    ymm = _bd_nn("ssm_y", [h_re, h_im], [w_y_re, w_y_im], sb, LANE)

    u_spec = _row(tm, sw, 3 * aw // sw)
    (yg,) = _rowwise("ssm_gelu", _ypre_fn, steps, [ymm, proj, d_skip], [_row(tm, sw), u_spec, _vec(sw)],
                     [_sds((s_len, sw), f32)], [_row(tm, sw)], [False])
    (z,) = _mm_nn("glu_proj", yg, wglu_g)
    (cat,) = _rowwise("mix_norm", _mix_fn, steps, [attn, yg, z, b_glu, attn_out_g, ssm_out_g],
                      [_row(tm, aw), _row(tm, sw), _row(tm, sw), _vec(sw), _vec(aw), _vec(sw)],
                      [_sds((s_len, d), bf16)], [_row(tm, d)], [False])
    (mixed,) = _mm_nn("out_proj", cat, wout_g)

    def res_norm2_fn(x_, mixed_, g1_, gn, sc, sh):
        x1_ = x_ + g1_ * mixed_
        return x1_, _norm_mod(x1_, gn, sc, sh)

    x1, h2 = _rowwise("norm2", res_norm2_fn, steps, [xs, mixed, g1, norm2_g, sc2, sh2],
                      [_row(tm, d), _row(tm, d)] + [_vec(d)] * 4,
                      [_sds((s_len, d), f32), _sds((s_len, d), bf16)], [_row(tm, d)] * 2, [False] * 2)

    def act_epilogue(acc):
        r = jnp.maximum(acc, 0.0)
        return acc, r * r

    a_ff, act = _mm_nn_sharded("ff1", h2, wff1_g, epilogue=act_epilogue,
                               outs=[_sds((s_len, 4 * d), f32), _sds((s_len, 4 * d), bf16)])
    (ff,) = _mm_nn("ff2", act, wff2_g, tn=2048, tk=512)

    def loss_fn(x1_, ff_, tgt_, g2_):
        e = x1_ + g2_ * ff_ - tgt_
        dy_ = e * (1.0 / d)
        part = jnp.full((1, LANE), 0.5 / d, f32) * jnp.sum(e * e)
        return dy_, g2_ * dy_, part, jnp.sum(dy_ * ff_, axis=0, keepdims=True)

    dy, dff, loss_part, d_g2 = _rowwise(
        "loss", loss_fn, steps, [x1, ff, tgt, g2], [_row(tm, d)] * 3 + [_vec(d)],
        [_sds((s_len, d), f32), _sds((s_len, d), bf16), _sds((1, LANE), f32), _sds((1, d), f32)],
        [_row(tm, d), _row(tm, d), _vec(LANE), _vec(d)], [False, False, True, True])
    loss = lax.psum(loss_part[0, 0], ("x", "y", "c"))

    def dact_epilogue(acc, a_):
        return (acc * (2.0 * jnp.maximum(a_, 0.0)),)

    (da,) = _mm_nt("ff2_dx", dff, wff2_g, epilogue=dact_epilogue, extra=[a_ff], outs=[_sds((s_len, 4 * d), bf16)])
    g_wff2 = _mm_tn("ff2_dw", act, dff).reshape(N_DEV, 4 * d // N_DEV, d)
    dh2 = _mm_nt_sharded("ff1_dx", da, wff1_g)
    g_wff1 = _mm_tn_sharded("ff1_dw", h2, da, N_DEV)

    def norm2_bwd_fn(dh2_, x1_, dy_, mixed_, gn, sc, sh, g1_):
        _, vjp = jax.vjp(_norm_mod, x1_, gn, sc, sh)
        dx, dgn, dsc, dsh = vjp(dh2_)
        dx1_ = dy_ + dx
        return dx1_, g1_ * dx1_, dgn, dsc, dsh, jnp.sum(dx1_ * mixed_, axis=0, keepdims=True)

    dx1, dmixed, d_norm2_g, d_sc2, d_sh2, d_g1 = _rowwise(
        "norm2_bwd", norm2_bwd_fn, steps, [dh2, x1, dy, mixed, norm2_g, sc2, sh2, g1],
        [_row(tm, d)] * 4 + [_vec(d)] * 4,
        [_sds((s_len, d), f32), _sds((s_len, d), bf16)] + [_sds((1, d), f32)] * 4,
        [_row(tm, d)] * 2 + [_vec(d)] * 4, [False, False, True, True, True, True])

    (dcat,) = _mm_nt("out_dx", dmixed, wout_g)
    g_wout = _mm_tn("out_dw", cat, dmixed).reshape(N_DEV, d // N_DEV, d)

    def mix_bwd_fn(dcat_, attn_, yg_, z_, bglu, ga, gs):
        _, vjp = jax.vjp(_mix_fn, attn_, yg_, z_, bglu, ga, gs)
        dattn_, dyg_, dz_, dbglu, dga, dgs = vjp(dcat_)
        prod = dattn_ * attn_
        dd_ = jnp.concatenate([jnp.broadcast_to(jnp.sum(prod[:, i * HEAD:(i + 1) * HEAD], axis=1, keepdims=True),
                                                (prod.shape[0], HEAD)) for i in range(n_heads)], axis=1)
        return dattn_, dd_, dyg_, dz_, dbglu, dga, dgs

    dattn, dd, dyg1, dz, d_b_glu, d_attn_out_g, d_ssm_out_g = _rowwise(
        "mix_bwd", mix_bwd_fn, steps, [dcat, attn, yg, z, b_glu, attn_out_g, ssm_out_g],
        [_row(tm, d), _row(tm, aw), _row(tm, sw), _row(tm, sw), _vec(sw), _vec(aw), _vec(sw)],
        [_sds((s_len, aw), bf16), _sds((s_len, aw), f32), _sds((s_len, sw), f32), _sds((s_len, sw), bf16),
         _sds((1, sw), f32), _sds((1, aw), f32), _sds((1, sw), f32)],
        [_row(tm, aw), _row(tm, aw), _row(tm, sw), _row(tm, sw), _vec(sw), _vec(aw), _vec(sw)],
        [False] * 4 + [True] * 3)

    (dyg2,) = _mm_nt("glu_dx", dz, wglu_g)
    g_wglu = _mm_tn("glu_dw", yg, dz).reshape(N_DEV, sw // N_DEV, sw)

    def gelu_bwd_fn(dyg1_, dyg2_, ymm_, u_, dskip):
        _, vjp = jax.vjp(_ypre_fn, ymm_, u_, dskip)
        dymm, du_, ddskip = vjp(dyg1_ + dyg2_)
        return dymm, du_, ddskip

    dymm, du_skip, d_d_skip = _rowwise(
        "ssm_gelu_bwd", gelu_bwd_fn, steps, [dyg1, dyg2, ymm, proj, d_skip],
        [_row(tm, sw)] * 3 + [u_spec, _vec(sw)],
        [_sds((s_len, sw), f32), _sds((s_len, sw), f32), _sds((1, sw), f32)],
        [_row(tm, sw), _row(tm, sw), _vec(sw)], [False, False, True])

    gr = _bd_nn("ssm_dh_re", [dymm], [w_g_re], LANE, sb)
    gi = _bd_nn("ssm_dh_im", [dymm], [w_g_im], LANE, sb)
    (ends_b,) = _scan("ssm_adj_ends", gr, gi, a2, reverse=True, write_h=False)
    lr, li, _, da_seg = _scan("ssm_adj", gr, gi, a2, ends_b, reverse=True, write_h=True,
                              da_from=(h_re, h_im, ends_f))
    du_ssm = _bd_nn("ssm_du", [lr, li], [w_du_re, w_du_im], sb, LANE)
    dc_re_c, dc_im_c = _bd_tn("ssm_dc", dymm, [h_re, h_im], LANE, sb)

    def bd_tn_u(name, b_list):
        tk = _tile(s_len, 1024)

        def body(u_ref, b0_ref, b1_ref, o0_ref, o1_ref):
            u_t = u_ref[...].astype(bf16)
            for b_ref, o_ref in ((b0_ref, o0_ref), (b1_ref, o1_ref)):
                @pl.when(pl.program_id(1) == 0)
                def _():
                    o_ref[...] = jnp.zeros_like(o_ref)

                o_ref[...] += lax.dot_general(u_t, b_ref[...].astype(bf16), (TN, ((), ())),
                                              preferred_element_type=f32)

        return _pcall(body, name=name, grid=(ngs, s_len // tk),
                      in_specs=[pl.BlockSpec((tk, LANE), lambda s, k: (k, u_blk + s))]
                      + [pl.BlockSpec((tk, sb), lambda s, k: (k, s))] * 2,
                      out_specs=[pl.BlockSpec((None, LANE, sb), lambda s, k: (s, 0, 0))] * 2,
                      out_shape=[_sds((ngs, LANE, sb), f32)] * 2, sem=("parallel", "arbitrary"))(proj, *b_list)

    dbb_re_c, dbb_im_c = bd_tn_u("ssm_dbbar", [lr, li])

    def diag_to_gpi(w):
        return _bd_diag(w, SSM_GROUP, n_state).transpose(0, 1, 3, 2).reshape(n_groups, n_state * SSM_GROUP)

    d_lam_re, d_lam_im, d_log_step, d_b_re2, d_b_im2 = _ssm_params_bwd(
        lam_re2, lam_im2, log_step2, b_re2, b_im2, expand,
        da_seg[0, 0].reshape(n_groups, n_state), da_seg[1, 0].reshape(n_groups, n_state),
        diag_to_gpi(dbb_re_c), diag_to_gpi(dbb_im_c))
    d_c_re = _bd_diag(dc_re_c, SSM_GROUP, n_state).reshape(n_groups, SSM_GROUP, n_state)
    d_c_im = -_bd_diag(dc_im_c, SSM_GROUP, n_state).reshape(n_groups, SSM_GROUP, n_state)

    grads_qkv = [_attn_bwd(qn, kn, vb, dattn, lse, dd, dil, slopes) for _, dil in DILATIONS]

    def qkv_bwd_fn(q, k, gq, gk, dq1, dq2, dq3, dk1, dk2, dk3, dv1, dv2, dv3, du1, du2):
        _, vjp = jax.vjp(lambda q_, k_, gq_, gk_: (_head_rms(q_, gq_), _head_rms(k_, gk_)), q, k, gq, gk)
        dq, dk, dgq, dgk = vjp((dq1 + dq2 + dq3, dk1 + dk2 + dk3))
        return jnp.concatenate([dq, dk, dv1 + dv2 + dv3, du1 + du2], axis=1), dgq, dgk

    qkv_cots = [grads_qkv[p][i] for i in range(3) for p in range(3)]
    dproj, d_q_norm_g, d_k_norm_g = _rowwise(
        "qk_norm_bwd", qkv_bwd_fn, steps, [proj, proj, q_norm_g, k_norm_g, *qkv_cots, du_skip, du_ssm],
        [_row(tm, aw, 0), _row(tm, aw, 1), _vec(HEAD), _vec(HEAD)] + [_row(tm, aw)] * 9 + [_row(tm, sw)] * 2,
        [_sds((s_len, 3 * aw + sw), bf16), _sds((1, HEAD), f32), _sds((1, HEAD), f32)],
        [_row(tm, 3 * aw + sw), _vec(HEAD), _vec(HEAD)], [False, True, True])

    dh = _mm_nt_sharded("in_dx", dproj, win_g)
    g_win = _mm_tn_sharded("in_dw", h, dproj, N_DEV)

    def norm1_bwd_fn(dh_, x_, dx1_, gn, sc, sh):
        _, vjp = jax.vjp(_norm_mod, x_, gn, sc, sh)
        dx, dgn, dsc, dsh = vjp(dh_)
        return dx1_ + dx, dgn, dsc, dsh

    grad_x, d_norm1_g, d_sc1, d_sh1 = _rowwise(
        "norm1_bwd", norm1_bwd_fn, steps, [dh, xs, dx1, norm1_g, sc1, sh1], [_row(tm, d)] * 3 + [_vec(d)] * 3,
        [_sds((s_len, d), f32)] + [_sds((1, d), f32)] * 3, [_row(tm, d)] + [_vec(d)] * 3,
        [False, True, True, True])

    dmod = jnp.concatenate([d_sh1, d_sc1, d_g1, d_sh2, d_sc2, d_g2], axis=1)
    small_g = {"b_ada": dmod, "norm1_g": d_norm1_g, "q_norm_g": d_q_norm_g, "k_norm_g": d_k_norm_g,
               "lam_re": d_lam_re, "lam_im": d_lam_im, "log_step": d_log_step, "b_re": d_b_re2, "b_im": d_b_im2,
               "c_re": d_c_re, "c_im": d_c_im, "d_skip": d_d_skip, "b_glu": d_b_glu,
               "attn_out_g": d_attn_out_g, "ssm_out_g": d_ssm_out_g, "norm2_g": d_norm2_g}
    small_part = _pack([small_g[n] for n in SMALL])
    r_win, r_wglu, r_wout, r_wff1, r_wff2, r_small = _exchange(
        "scatter_grads", [g_win, g_wglu, g_wout, g_wff1, g_wff2, small_part], [True] * 5 + [False])

    res = {}
    dmod_all = r_small.reshape(N_DEV, -1)[:, :6 * d]
    g_wada = _ada_bwd(c_all, lax.dynamic_slice_in_dim(dmod_all, me * n_ada, n_ada, axis=1))
    res["w_ada"] = _adamw("adamw_w_ada", w_ada[0], m_w_ada[0], v_w_ada[0], g_wada, False)
    for name, stack in (("w_in", r_win), ("w_glu", r_wglu), ("w_out", r_wout), ("w_ff1", r_wff1), ("w_ff2", r_wff2)):
        res[name] = _adamw("adamw_" + name, wts[name][0], mom[name][0], var[name][0], stack, True)
    small_res = _adamw("adamw_small", _pack([wts[n] for n in SMALL]), _pack([mom[n] for n in SMALL]),
                       _pack([var[n] for n in SMALL]), r_small, True)
    off = 0
    for n in SMALL:
        size = wts[n].size
        res[n] = [t.reshape(-1)[off:off + size] for t in small_res]
        off += size

    out = [loss, grad_x[None]]
    for i in range(4):
        out += [res[n][i].reshape(wts[n].shape) for n in ORDER]
    return tuple(out)
```

```python
import math

import jax
import jax.numpy as jnp
from jax import lax
from jax.experimental import pallas as pl
from jax.experimental.pallas import tpu as pltpu

f32, bf16 = jnp.float32, jnp.bfloat16

N_DEV = 8
LANE = 128
HEAD = 128
SSM_GROUP = 16
GROUPS_PER_BLOCK = LANE // SSM_GROUP
DILATIONS = ((128, 1), (512, 4), (2048, 16))
BAND = 128
EPS = 1e-6
ADAM_LR, ADAM_B1, ADAM_B2, ADAM_EPS, ADAM_WD, ADAM_STEP = 0.001, 0.9, 0.999, 1e-08, 0.01, 10
NEG = -1e30
VMEM_LIMIT = 60 * 1024 * 1024
HI = lax.Precision.HIGHEST
MESH = pl.DeviceIdType.MESH


def _pcall(body, **kw):
    sem = kw.pop("sem", None)
    kw["compiler_params"] = pltpu.CompilerParams(dimension_semantics=sem, vmem_limit_bytes=VMEM_LIMIT)
    return pl.pallas_call(body, **kw)


def _tile(n, pref):
    t = min(n, pref)
    while n % t:
        t //= 2
    return t


def _sds(shape, dtype):
    return jax.ShapeDtypeStruct(shape, dtype)


def _rowwise(name, fn, steps, ins, in_specs, outs, out_specs, acc):
    n_in = len(ins)

    def body(*refs):
        res = fn(*[r[...] for r in refs[:n_in]])
        res = res if isinstance(res, (tuple, list)) else (res,)
        for r, o, a in zip(refs[n_in:], res, acc):
            if a:
                @pl.when(pl.program_id(0) == 0)
                def _():
                    r[...] = jnp.zeros_like(r)
                r[...] += o
            else:
                r[...] = o.astype(r.dtype)

    return _pcall(body, name=name, grid=(steps,), in_specs=in_specs, out_specs=out_specs, out_shape=outs,
                  sem=("arbitrary",))(*ins)


def _row(tm, c, blk=0):
    return pl.BlockSpec((tm, c), lambda i: (i, blk))


def _vec(c, blk=0):
    return pl.BlockSpec((1, c), lambda i: (0, blk))


def _rms(x, g):
    return x * lax.rsqrt(jnp.mean(x * x, axis=-1, keepdims=True) + EPS) * g


def _norm_mod(x, g, sc, sh):
    return _rms(x, g) * (1.0 + sc) + sh


def _head_rms(t, g):
    return jnp.concatenate([_rms(t[:, h * HEAD:(h + 1) * HEAD], g) for h in range(t.shape[1] // HEAD)], axis=1)


def _mix_fn(attn, yg, z, bglu, ga, gs):
    ssm = yg * jax.nn.sigmoid(z + bglu)
    return jnp.concatenate([_rms(attn, ga), _rms(ssm, gs)], axis=1)


def _ypre_fn(ymm, u, dskip):
    return jax.nn.gelu(ymm + dskip * u)


def _matmul(name, a, b, *, dims, grid, a_spec, b_spec, acc_shape, outs, out_specs, extra=(), extra_specs=(),
            epilogue=None):
    gk = grid[2]
    n_x = len(extra)

    def body(a_ref, b_ref, *rest):
        acc = rest[-1]
        k = pl.program_id(2)

        @pl.when(k == 0)
        def _():
            acc[...] = jnp.zeros_like(acc)

        acc[...] += lax.dot_general(a_ref[...].astype(bf16), b_ref[...].astype(bf16), (dims, ((), ())),
                                    preferred_element_type=f32)

        @pl.when(k == gk - 1)
        def _():
            xs = [r[...] for r in rest[:n_x]]
            res = epilogue(acc[...], *xs) if epilogue is not None else (acc[...],)
            for r, o in zip(rest[n_x:-1], res):
                r[...] = o.astype(r.dtype)

    return _pcall(body, name=name, grid=grid, in_specs=[a_spec, b_spec, *extra_specs], out_specs=out_specs,
                  out_shape=outs, scratch_shapes=[pltpu.VMEM(acc_shape, f32)],
                  sem=("parallel", "parallel", "arbitrary"))(a, b, *extra)


NN = ((1,), (0,))
NT = ((1,), (1,))
TN = ((0,), (0,))


def _mm_nn(name, a, b, out_dtype=f32, tm=1024, tn=1024, tk=2048, epilogue=None, extra=(), outs=None):
    m, kd = a.shape
    n = b.shape[1]
    tm, tn, tk = _tile(m, tm), _tile(n, tn), _tile(kd, tk)
    o_spec = pl.BlockSpec((tm, tn), lambda i, j, k: (i, j))
    outs = outs if outs is not None else [_sds((m, n), out_dtype)]
    return _matmul(name, a, b, dims=NN, grid=(m // tm, n // tn, kd // tk),
                   a_spec=pl.BlockSpec((tm, tk), lambda i, j, k: (i, k)),
                   b_spec=pl.BlockSpec((tk, tn), lambda i, j, k: (k, j)),
                   acc_shape=(tm, tn), outs=outs, out_specs=[o_spec] * len(outs),
                   extra=extra, extra_specs=[o_spec] * len(extra), epilogue=epilogue)


def _mm_nn_sharded(name, a, b3, out_dtype=f32, tm=1024, tk=2048, epilogue=None, outs=None):
    m, kd = a.shape
    nsh, _, n = b3.shape
    tm, tk = _tile(m, tm), _tile(kd, tk)
    o_spec = pl.BlockSpec((tm, n), lambda i, j, k: (i, j))
    outs = outs if outs is not None else [_sds((m, nsh * n), out_dtype)]
    return _matmul(name, a, b3, dims=NN, grid=(m // tm, nsh, kd // tk),
                   a_spec=pl.BlockSpec((tm, tk), lambda i, j, k: (i, k)),
                   b_spec=pl.BlockSpec((None, tk, n), lambda i, j, k: (j, k, 0)),
                   acc_shape=(tm, n), outs=outs, out_specs=[o_spec] * len(outs), epilogue=epilogue)


def _mm_nt(name, a, b, out_dtype=f32, tm=1024, tn=1024, tk=1024, epilogue=None, extra=(), outs=None):
    m, kd = a.shape
    n = b.shape[0]
    tm, tn, tk = _tile(m, tm), _tile(n, tn), _tile(kd, tk)
    o_spec = pl.BlockSpec((tm, tn), lambda i, j, k: (i, j))
    outs = outs if outs is not None else [_sds((m, n), out_dtype)]
    return _matmul(name, a, b, dims=NT, grid=(m // tm, n // tn, kd // tk),
                   a_spec=pl.BlockSpec((tm, tk), lambda i, j, k: (i, k)),
                   b_spec=pl.BlockSpec((tn, tk), lambda i, j, k: (j, k)),
                   acc_shape=(tm, tn), outs=outs, out_specs=[o_spec] * len(outs),
                   extra=extra, extra_specs=[o_spec] * len(extra), epilogue=epilogue)


def _mm_nt_sharded(name, a, b3, out_dtype=f32, tm=512, tn=2048):
    m = a.shape[0]
    nsh, n_out, n = b3.shape
    tm, tn = _tile(m, tm), _tile(n_out, tn)
    return _matmul(name, a, b3, dims=NT, grid=(m // tm, n_out // tn, nsh),
                   a_spec=pl.BlockSpec((tm, n), lambda i, j, k: (i, k)),
                   b_spec=pl.BlockSpec((None, tn, n), lambda i, j, k: (k, j, 0)),
                   acc_shape=(tm, tn), outs=[_sds((m, n_out), out_dtype)],
                   out_specs=[pl.BlockSpec((tm, tn), lambda i, j, k: (i, j))])[0]


def _mm_tn(name, a, b, out_dtype=bf16, tm=1024, tn=1024, tk=1024):
    t, m = a.shape
    n = b.shape[1]
    tm, tn, tk = _tile(m, tm), _tile(n, tn), _tile(t, tk)
    return _matmul(name, a, b, dims=TN, grid=(m // tm, n // tn, t // tk),
                   a_spec=pl.BlockSpec((tk, tm), lambda i, j, k: (k, i)),
                   b_spec=pl.BlockSpec((tk, tn), lambda i, j, k: (k, j)),
                   acc_shape=(tm, tn), outs=[_sds((m, n), out_dtype)],
                   out_specs=[pl.BlockSpec((tm, tn), lambda i, j, k: (i, j))])[0]


def _mm_tn_sharded(name, a, b, nsh, out_dtype=bf16, tm=1024, tk=1024):
    t, m = a.shape
    n = b.shape[1] // nsh
    tm, tk = _tile(m, tm), _tile(t, tk)
    return _matmul(name, a, b, dims=TN, grid=(m // tm, nsh, t // tk),
                   a_spec=pl.BlockSpec((tk, tm), lambda i, j, k: (k, i)),
                   b_spec=pl.BlockSpec((tk, n), lambda i, j, k: (k, j)),
                   acc_shape=(tm, n), outs=[_sds((nsh, m, n), out_dtype)],
                   out_specs=[pl.BlockSpec((None, tm, n), lambda i, j, k: (j, i, 0))])[0]


def _bd_nn(name, a_list, w_list, ka, nb, out_dtype=f32, tm=1024):
    n_q = len(a_list)
    m = a_list[0].shape[0]
    ngs = w_list[0].shape[0]
    tm = _tile(m, tm)

    def body(*refs):
        o_ref = refs[-1]
        tot = None
        for q in range(n_q):
            p = jnp.dot(refs[q][...].astype(bf16), refs[n_q + q][...], preferred_element_type=f32)
            tot = p if tot is None else tot + p
        o_ref[...] = tot.astype(o_ref.dtype)

    return _pcall(body, name=name, grid=(m // tm, ngs),
                  in_specs=[pl.BlockSpec((tm, ka), lambda i, s: (i, s))] * n_q
                  + [pl.BlockSpec((None, ka, nb), lambda i, s: (s, 0, 0))] * n_q,
                  out_specs=pl.BlockSpec((tm, nb), lambda i, s: (i, s)),
                  out_shape=_sds((m, ngs * nb), out_dtype), sem=("parallel", "parallel"))(*a_list, *w_list)


def _bd_tn(name, a, b_list, ra, cb, tk=1024):
    n_q = len(b_list)
    t = a.shape[0]
    ngs = a.shape[1] // ra
    tk = _tile(t, tk)

    def body(*refs):
        a_t = refs[0][...].astype(bf16)
        for q in range(n_q):
            o_ref = refs[1 + n_q + q]

            @pl.when(pl.program_id(1) == 0)
            def _():
                o_ref[...] = jnp.zeros_like(o_ref)

            o_ref[...] += lax.dot_general(a_t, refs[1 + q][...].astype(bf16), (TN, ((), ())),
                                          preferred_element_type=f32)

    return _pcall(body, name=name, grid=(ngs, t // tk),
                  in_specs=[pl.BlockSpec((tk, ra), lambda s, k: (k, s))]
                  + [pl.BlockSpec((tk, cb), lambda s, k: (k, s))] * n_q,
                  out_specs=[pl.BlockSpec((None, ra, cb), lambda s, k: (s, 0, 0))] * n_q,
                  out_shape=[_sds((ngs, ra, cb), f32)] * n_q, sem=("parallel", "arbitrary"))(a, *b_list)


def _bd_weight(t4):
    ngs, gb, r, c = t4.shape
    eye = jnp.eye(gb, dtype=t4.dtype)
    return jnp.einsum("sgrc,gh->sgrhc", t4, eye).reshape(ngs, gb * r, gb * c).astype(bf16)


def _bd_diag(w, r, c):
    ngs = w.shape[0]
    gb = w.shape[1] // r
    w5 = w.reshape(ngs, gb, r, gb, c)
    return jnp.einsum("sgrhc,gh->sgrc", w5, jnp.eye(gb, dtype=w.dtype))


SCAN_CHAINS = 8


def _scan(name, xr, xi, a2, *, reverse, da_from=None):
    s_len, gp = xr.shape
    nch = SCAN_CHAINS
    while s_len % (8 * nch) or (s_len // (8 * nch)) & (s_len // (8 * nch) - 1):
        nch //= 2
    nseg = 8 * nch
    seg = s_len // nseg
    n_sq = int(math.log2(seg))
    assert 2 ** n_sq == seg
    with_da = da_from is not None

    def body(*refs):
        it = iter(refs)
        xr_ref, xi_ref, a_ref = next(it), next(it), next(it)
        if with_da:
            hr_ref, hi_ref, hin_ref = next(it), next(it), next(it)
        or_ref, oi_ref, oin_ref = next(it), next(it), next(it)
        if with_da:
            da_ref = next(it)

        ar = a_ref[0:1, :]
        ai = -a_ref[1:2, :] if reverse else a_ref[1:2, :]
        arb, aib = jnp.broadcast_to(ar, (8, LANE)), jnp.broadcast_to(ai, (8, LANE))

        def rows(ch, k):
            return pl.ds(ch * 8 * seg + k, 8, stride=seg)

        def advance(h, ch, k):
            hr, hi = h
            return (arb * hr - aib * hi + xr_ref[rows(ch, k), :], arb * hi + aib * hr + xi_ref[rows(ch, k), :])

        def kk(n):
            return seg - 1 - n if reverse else n

        zero = jnp.zeros((8, LANE), f32)

        def sweep1(n, hs):
            return tuple(advance(hs[ch], ch, kk(n)) for ch in range(nch))

        ends = lax.fori_loop(0, seg, sweep1, tuple((zero, zero) for _ in range(nch)))

        pr, pi = ar, ai
        for _ in range(n_sq):
            pr, pi = pr * pr - pi * pi, 2.0 * pr * pi
        in_r, in_i = [None] * nseg, [None] * nseg
        cr = ci = jnp.zeros((1, LANE), f32)
        for j in (range(nseg - 1, -1, -1) if reverse else range(nseg)):
            in_r[j], in_i[j] = cr, ci
            er, ei = ends[j // 8][0][j % 8:j % 8 + 1, :], ends[j // 8][1][j % 8:j % 8 + 1, :]
            cr, ci = er + pr * cr - pi * ci, ei + pr * ci + pi * cr
        h0 = tuple((jnp.concatenate(in_r[8 * ch:8 * ch + 8], axis=0), jnp.concatenate(in_i[8 * ch:8 * ch + 8], axis=0))
                   for ch in range(nch))
        for ch in range(nch):
            oin_ref[0, 8 * ch:8 * ch + 8, :] = h0[ch][0]
            oin_ref[1, 8 * ch:8 * ch + 8, :] = h0[ch][1]

        def emit(ch, k, h):
            or_ref[rows(ch, k), :] = h[0]
            oi_ref[rows(ch, k), :] = h[1]

        def pair(h, p):
            return h[0] * p[0] + h[1] * p[1], h[1] * p[0] - h[0] * p[1]

        def sweep2(n, carry):
            k = kk(n)
            hs = carry[:nch]
            new = tuple(advance(hs[ch], ch, k) for ch in range(nch))
            for ch in range(nch):
                emit(ch, k, new[ch])
            if not with_da:
                return new
            dr, di = carry[nch]
            for ch in range(nch):
                qr, qi = pair(new[ch], (hr_ref[rows(ch, k - 1), :], hi_ref[rows(ch, k - 1), :]))
                dr, di = dr + qr, di + qi
            return new + ((dr, di),)

        if with_da:
            carry = lax.fori_loop(0, seg - 1, sweep2, h0 + ((zero, zero),))
            dr, di = carry[nch]
            for ch in range(nch):
                new = advance(carry[ch], ch, 0)
                emit(ch, 0, new)
                qr, qi = pair(new, (hin_ref[0, 8 * ch:8 * ch + 8, :], hin_ref[1, 8 * ch:8 * ch + 8, :]))
                dr, di = dr + qr, di + qi
            da_ref[0] = jnp.sum(dr, axis=0, keepdims=True)
            da_ref[1] = jnp.sum(di, axis=0, keepdims=True)
        else:
            lax.fori_loop(0, seg, sweep2, h0)

    col = pl.BlockSpec((s_len, LANE), lambda l: (0, l))
    in_spec = pl.BlockSpec((2, nseg, LANE), lambda l: (0, 0, l))
    ins, in_specs = [xr, xi, a2], [col, col, pl.BlockSpec((8, LANE), lambda l: (0, l))]
    if with_da:
        ins += list(da_from)
        in_specs += [col, col, in_spec]
    outs = [_sds((s_len, gp), f32)] * 2 + [_sds((2, nseg, gp), f32)]
    out_specs = [col, col, in_spec]
    if with_da:
        outs.append(_sds((2, 1, gp), f32))
        out_specs.append(pl.BlockSpec((2, 1, LANE), lambda l: (0, 0, l)))
    return _pcall(body, name=name, grid=(gp // LANE,), in_specs=in_specs, out_specs=out_specs, out_shape=outs,
                  sem=("parallel",))(*ins)


def _ssm_param_fn(lam_re, lam_im, log_step, b_re2, b_im2, expand):
    step = jnp.exp(log_step)
    xr, xi = lam_re * step, lam_im * step
    mag = jnp.exp(xr)
    ar, ai = mag * jnp.cos(xi), mag * jnp.sin(xi)
    nr, ni = ar - 1.0, ai
    den = lam_re * lam_re + lam_im * lam_im
    cr = (nr * lam_re + ni * lam_im) / den
    ci = (ni * lam_re - nr * lam_im) / den
    cre = jnp.dot(cr, expand, precision=HI, preferred_element_type=f32)
    cie = jnp.dot(ci, expand, precision=HI, preferred_element_type=f32)
    return ar, ai, cre * b_re2 - cie * b_im2, cre * b_im2 + cie * b_re2


def _ssm_params(lam_re, lam_im, log_step, b_re2, b_im2, expand):
    def body(*refs):
        res = _ssm_param_fn(*[r[...] for r in refs[:6]])
        for r, o in zip(refs[6:], res):
            r[...] = o

    g, p = lam_re.shape
    return _pcall(body, name="ssm_params", out_shape=[_sds((g, p), f32)] * 2 + [_sds(b_re2.shape, f32)] * 2)(
        lam_re, lam_im, log_step, b_re2, b_im2, expand)


def _ssm_params_bwd(lam_re, lam_im, log_step, b_re2, b_im2, expand, d_ar, d_ai, d_bbr, d_bbi):
    def body(*refs):
        prim = [r[...] for r in refs[:5]]
        ex = refs[5][...]
        cot = tuple(r[...] for r in refs[6:10])
        _, vjp = jax.vjp(lambda *p_: _ssm_param_fn(*p_, ex), *prim)
        for r, o in zip(refs[10:], vjp(cot)):
            r[...] = o

    shapes = [lam_re.shape, lam_im.shape, log_step.shape, b_re2.shape, b_im2.shape]
    return _pcall(body, name="ssm_params_bwd", out_shape=[_sds(s, f32) for s in shapes])(
        lam_re, lam_im, log_step, b_re2, b_im2, expand, d_ar, d_ai, d_bbr, d_bbi)


def _slope_table(n_heads):
    s = 2.0 ** (-8.0 * (jnp.arange(n_heads, dtype=f32) + 1.0) / n_heads)
    return jnp.broadcast_to(s[:, None, None], (n_heads, 1, LANE))


def _band_probs(q, k, slope_d, shift, live, ref_col):
    s = lax.dot_general(q, k, (NT, ((), ())), preferred_element_type=f32) * (HEAD ** -0.5)
    qi = lax.broadcasted_iota(jnp.int32, (BAND, BAND), 0)
    ki = lax.broadcasted_iota(jnp.int32, (BAND, BAND), 1)
    s = s - slope_d * (qi - ki + shift).astype(f32)
    mask = ((ki >= qi) if shift else (ki <= qi)) & live
    if ref_col is None:
        return jnp.where(mask, s, NEG)
    return jnp.where(mask, jnp.exp(s - ref_col), 0.0)


def _attn_geometry(s_len, dil):
    piece = BAND * dil
    m = max(1, 8 // dil)
    while s_len % (piece * m):
        m //= 2
    return m, piece


def _stream_rows(start, dil):
    return pl.ds(start, BAND, stride=dil) if dil > 1 else pl.ds(start, BAND)


def _attn_fwd(qn, kn, proj, v_blk, dil, slopes):
    s_len, aw = qn.shape
    n_heads = aw // HEAD
    m, piece = _attn_geometry(s_len, dil)
    rows = m * piece

    def body(q_ref, k_ref, kp_ref, v_ref, vp_ref, sl_ref, o_ref, lse_ref):
        t = pl.program_id(1)
        slope_d = sl_ref[:, 0:1] * float(dil)
        for b in range(m):
            for r in range(dil):
                idx = _stream_rows(b * piece + r, dil)
                q, kc, vc = (ref[idx, :].astype(bf16) for ref in (q_ref, k_ref, v_ref))
                if b:
                    pidx = _stream_rows((b - 1) * piece + r, dil)
                    kp, vp, live = k_ref[pidx, :].astype(bf16), v_ref[pidx, :].astype(bf16), True
                else:
                    pidx = _stream_rows(r, dil)
                    kp, vp, live = kp_ref[pidx, :].astype(bf16), vp_ref[pidx, :].astype(bf16), t > 0
                s_c = _band_probs(q, kc, slope_d, 0, True, None)
                s_p = _band_probs(q, kp, slope_d, BAND, live, None)
                mx = jnp.maximum(jnp.max(s_c, axis=1, keepdims=True), jnp.max(s_p, axis=1, keepdims=True))
                p_c, p_p = jnp.exp(s_c - mx), jnp.exp(s_p - mx)
                den = jnp.sum(p_c, axis=1, keepdims=True) + jnp.sum(p_p, axis=1, keepdims=True)
                o = jnp.dot(p_c.astype(bf16), vc, preferred_element_type=f32)
                o += jnp.dot(p_p.astype(bf16), vp, preferred_element_type=f32)
                o_ref[idx, :] = o / den
                lse_ref[idx, :] = jnp.broadcast_to(mx + jnp.log(den), (BAND, HEAD))

    def cur(blk0):
        return pl.BlockSpec((rows, HEAD), lambda h, t: (t, blk0 + h))

    def prev(blk0):
        return pl.BlockSpec((piece, HEAD), lambda h, t: (jnp.maximum(t * m - 1, 0), blk0 + h))

    sl = pl.BlockSpec((None, 1, LANE), lambda h, t: (h, 0, 0))
    return _pcall(body, name=f"attn_fwd_d{dil}", grid=(n_heads, s_len // rows),
                  in_specs=[cur(0), cur(0), prev(0), cur(v_blk), prev(v_blk), sl], out_specs=[cur(0), cur(0)],
                  out_shape=[_sds((s_len, aw), f32)] * 2, sem=("parallel", "parallel"))(
        qn, kn, kn, proj, proj, slopes)


def _attn_bwd(qn, kn, proj, v_blk, do, lse, dd, dil, slopes):
    s_len, aw = qn.shape
    n_heads = aw // HEAD
    m, piece = _attn_geometry(s_len, dil)
    rows = m * piece
    n_tiles = s_len // rows
    scale = HEAD ** -0.5

    def body(q_ref, qx_ref, k_ref, kp_ref, v_ref, vp_ref, do_ref, dox_ref, l_ref, lx_ref, d_ref, dx_ref, sl_ref,
             dq_ref, dk_ref, dv_ref):
        t = pl.program_id(1)
        slope_d = sl_ref[:, 0:1] * float(dil)

        def query_side(ref_q, ref_do, ref_l, ref_d, idx):
            return (ref_q[idx, :].astype(bf16), ref_do[idx, :].astype(bf16), ref_l[idx, :][:, 0:1],
                    ref_d[idx, :][:, 0:1])

        def block(qs, k, v, shift, live):
            q, do_, l_col, d_col = qs
            p = _band_probs(q, k, slope_d, shift, live, l_col)
            dp = lax.dot_general(do_, v, (NT, ((), ())), preferred_element_type=f32)
            return p.astype(bf16), (p * (dp - d_col)).astype(bf16)

        def tn(a_, b_):
            return lax.dot_general(a_, b_, (TN, ((), ())), preferred_element_type=f32)

        for r in range(dil):
            pend = None
            for b in range(m + 1):
                last = b == m
                if last:
                    qs = query_side(qx_ref, dox_ref, lx_ref, dx_ref, _stream_rows(r, dil))
                    live = t < n_tiles - 1
                else:
                    idx = _stream_rows(b * piece + r, dil)
                    qs = query_side(q_ref, do_ref, l_ref, d_ref, idx)
                    kc, vc = k_ref[idx, :].astype(bf16), v_ref[idx, :].astype(bf16)
                if b == 0:
                    pidx = _stream_rows(r, dil)
                    kp, vp, live = kp_ref[pidx, :].astype(bf16), vp_ref[pidx, :].astype(bf16), t > 0
                elif not last:
                    kp, vp, live = kc_prev, vc_prev, True
                else:
                    kp, vp = kc_prev, vc_prev
                p_p, ds_p = block(qs, kp, vp, BAND, live)
                if pend is not None:
                    pidx_, dk_, dv_ = pend
                    dk_ref[pidx_, :] = (dk_ + tn(ds_p, qs[0])) * scale
                    dv_ref[pidx_, :] = dv_ + tn(p_p, qs[1])
                if last:
                    break
                p_c, ds_c = block(qs, kc, vc, 0, True)
                dq = jnp.dot(ds_c, kc, preferred_element_type=f32) + jnp.dot(ds_p, kp, preferred_element_type=f32)
                dq_ref[idx, :] = dq * scale
                pend = (idx, tn(ds_c, qs[0]), tn(p_c, qs[1]))
                kc_prev, vc_prev = kc, vc

    def cur(blk0):
        return pl.BlockSpec((rows, HEAD), lambda h, t: (t, blk0 + h))

    def prev(blk0):
        return pl.BlockSpec((piece, HEAD), lambda h, t: (jnp.maximum(t * m - 1, 0), blk0 + h))

    def nxt(blk0):
        return pl.BlockSpec((piece, HEAD), lambda h, t: (jnp.minimum(t * m + m, n_tiles * m - 1), blk0 + h))

    sl = pl.BlockSpec((None, 1, LANE), lambda h, t: (h, 0, 0))
    return _pcall(body, name=f"attn_bwd_d{dil}", grid=(n_heads, n_tiles),
                  in_specs=[cur(0), nxt(0), cur(0), prev(0), cur(v_blk), prev(v_blk), cur(0), nxt(0), cur(0), nxt(0),
                            cur(0), nxt(0), sl],
                  out_specs=[cur(0)] * 3, out_shape=[_sds((s_len, aw), f32)] * 3,
                  sem=("parallel", "parallel"))(qn, qn, kn, kn, proj, proj, do, do, lse, lse, dd, dd, slopes)


def _exchange(name, srcs, scatter):
    n = len(srcs)

    def body(*refs):
        src, out = refs[:n], refs[n:2 * n]
        send_sems, recv_sems, local_sems = refs[2 * n:]
        x, y, c = lax.axis_index("x"), lax.axis_index("y"), lax.axis_index("c")
        me = 4 * x + 2 * y + c

        def peer(r):
            return ((1 - x) if r & 4 else x, (1 - y) if r & 2 else y, (1 - c) if r & 1 else c)

        def lin(p):
            return 4 * p[0] + 2 * p[1] + p[2]

        def piece(a, idx):
            return src[a].at[idx] if scatter[a] else src[a]

        local, sends = [], []
        for a in range(n):
            cp = pltpu.make_async_copy(piece(a, me), out[a].at[me], local_sems.at[a])
            cp.start()
            local.append(cp)
        for r in range(1, N_DEV):
            p = peer(r)
            for a in range(n):
                cp = pltpu.make_async_remote_copy(src_ref=piece(a, lin(p)), dst_ref=out[a].at[me],
                                                  send_sem=send_sems.at[a, r - 1], recv_sem=recv_sems.at[a, r - 1],
                                                  device_id=p, device_id_type=MESH)
                cp.start()
                sends.append(cp)
        for r in range(1, N_DEV):
            p = peer(r)
            for a in range(n):
                pltpu.make_async_remote_copy(src_ref=piece(a, lin(p)), dst_ref=out[a].at[lin(p)],
                                             send_sem=send_sems.at[a, r - 1], recv_sem=recv_sems.at[a, r - 1],
                                             device_id=p, device_id_type=MESH).wait_recv()
        for cp in sends:
            cp.wait_send()
        for cp in local:
            cp.wait()

    def piece_shape(a):
        return srcs[a].shape[1:] if scatter[a] else srcs[a].shape

    any_spec = pl.BlockSpec(memory_space=pl.ANY)
    return _pcall(body, name=name, in_specs=[any_spec] * n, out_specs=[any_spec] * n,
                  out_shape=[_sds((N_DEV, *piece_shape(a)), srcs[a].dtype) for a in range(n)],
                  scratch_shapes=[pltpu.SemaphoreType.DMA((n, N_DEV - 1)), pltpu.SemaphoreType.DMA((n, N_DEV - 1)),
                                  pltpu.SemaphoreType.DMA((n,))])(*srcs)


def _adamw(name, w, m, v, g_or_stack, stacked, rows=256):
    r, c = w.shape
    tr = _tile(r, rows)

    def fn(w_, m_, v_, g_):
        if stacked:
            g = g_[0].astype(f32)
            for j in range(1, N_DEV):
                g = g + g_[j].astype(f32)
        else:
            g = g_
        m_new = ADAM_B1 * m_ + (1.0 - ADAM_B1) * g
        v_new = ADAM_B2 * v_ + (1.0 - ADAM_B2) * (g * g)
        m_hat = m_new / (1.0 - ADAM_B1 ** ADAM_STEP)
        v_hat = v_new / (1.0 - ADAM_B2 ** ADAM_STEP)
        delta = -ADAM_LR * (m_hat / (jnp.sqrt(v_hat) + ADAM_EPS) + ADAM_WD * w_)
        return g, delta, m_new, v_new

    blk = _row(tr, c)
    g_spec = pl.BlockSpec((N_DEV, tr, c), lambda i: (0, i, 0)) if stacked else blk
    return _rowwise(name, fn, r // tr, [w, m, v, g_or_stack], [blk, blk, blk, g_spec],
                    [_sds((r, c), f32)] * 4, [blk] * 4, [False] * 4)


def _ada_fwd(c_all, w_shard, b_shard):
    nb_, d = c_all.shape
    n = w_shard.shape[1]
    tn = _tile(n, 512)

    def body(c_ref, w_ref, b_ref, o_ref):
        a = jax.nn.silu(c_ref[...]).astype(bf16)
        o_ref[...] = jnp.dot(a, w_ref[...].astype(bf16), preferred_element_type=f32) + b_ref[...]

    return _pcall(body, name="ada_fwd", grid=(n // tn,),
                  in_specs=[pl.BlockSpec((nb_, d), lambda j: (0, 0)), pl.BlockSpec((d, tn), lambda j: (0, j)),
                            pl.BlockSpec((1, tn), lambda j: (0, j))],
                  out_specs=pl.BlockSpec((nb_, tn), lambda j: (0, j)), out_shape=_sds((nb_, n), f32),
                  sem=("parallel",))(c_all, w_shard, b_shard)


def _ada_bwd(c_all, dmod_cols):
    nb_, d = c_all.shape
    n = dmod_cols.shape[1]
    tn = _tile(n, 512)

    def body(c_ref, g_ref, o_ref):
        a = jax.nn.silu(c_ref[...]).astype(bf16).astype(f32)
        g = g_ref[...].astype(bf16).astype(f32)
        o_ref[...] = lax.dot_general(a, g, (TN, ((), ())), precision=HI, preferred_element_type=f32)

    return _pcall(body, name="ada_bwd", grid=(n // tn,),
                  in_specs=[pl.BlockSpec((nb_, d), lambda j: (0, 0)), pl.BlockSpec((nb_, tn), lambda j: (0, j))],
                  out_specs=pl.BlockSpec((d, tn), lambda j: (0, j)), out_shape=_sds((d, n), f32),
                  sem=("parallel",))(c_all, dmod_cols)


SMALL = ("b_ada", "norm1_g", "q_norm_g", "k_norm_g", "lam_re", "lam_im", "log_step", "b_re", "b_im", "c_re", "c_im",
         "d_skip", "b_glu", "attn_out_g", "ssm_out_g", "norm2_g")
BIG = ("w_ada", "w_in", "w_glu", "w_out", "w_ff1", "w_ff2")
ORDER = ("w_ada", "b_ada", "norm1_g", "w_in", "q_norm_g", "k_norm_g", "lam_re", "lam_im", "log_step", "b_re", "b_im",
         "c_re", "c_im", "d_skip", "w_glu", "b_glu", "attn_out_g", "ssm_out_g", "w_out", "norm2_g", "w_ff1", "w_ff2")


def _pack(parts):
    flat = jnp.concatenate([p.reshape(-1) for p in parts])
    pad = (-flat.shape[0]) % (8 * LANE)
    return jnp.pad(flat, (0, pad)).reshape(-1, LANE)


def kernel(x, c, w_ada, b_ada, norm1_g, w_in, q_norm_g, k_norm_g, lam_re, lam_im, log_step, b_re, b_im, c_re, c_im, d_skip, w_glu, b_glu, attn_out_g, ssm_out_g, w_out, norm2_g, w_ff1, w_ff2, loss_target, m_w_ada, m_b_ada, m_norm1_g, m_w_in, m_q_norm_g, m_k_norm_g, m_lam_re, m_lam_im, m_log_step, m_b_re, m_b_im, m_c_re, m_c_im, m_d_skip, m_w_glu, m_b_glu, m_attn_out_g, m_ssm_out_g, m_w_out, m_norm2_g, m_w_ff1, m_w_ff2, v_w_ada, v_b_ada, v_norm1_g, v_w_in, v_q_norm_g, v_k_norm_g, v_lam_re, v_lam_im, v_log_step, v_b_re, v_b_im, v_c_re, v_c_im, v_d_skip, v_w_glu, v_b_glu, v_attn_out_g, v_ssm_out_g, v_w_out, v_norm2_g, v_w_ff1, v_w_ff2):
    env = dict(locals())
    wts = {n: env[n] for n in ORDER}
    mom = {n: env["m_" + n] for n in ORDER}
    var = {n: env["v_" + n] for n in ORDER}

    xs, tgt = x[0], loss_target[0]
    s_len, d = xs.shape
    aw = d // 2
    sw = d - aw
    n_heads = aw // HEAD
    n_groups = sw // SSM_GROUP
    n_state = lam_re.shape[-1]
    ngs = sw // LANE
    gp = n_groups * n_state
    sb = GROUPS_PER_BLOCK * n_state
    tm = _tile(s_len, 256)
    steps = s_len // tm
    me = 4 * lax.axis_index("x") + 2 * lax.axis_index("y") + lax.axis_index("c")

    win_g, wglu_g, wout_g, wff1_g, wff2_g, c_all = _exchange(
        "gather_weights",
        [w_in[0].astype(bf16), w_glu[0].astype(bf16), w_out[0].astype(bf16), w_ff1[0].astype(bf16),
         w_ff2[0].astype(bf16), c], [False] * 6)
    wglu_g = wglu_g.reshape(sw, sw)
    wout_g = wout_g.reshape(d, d)
    wff2_g = wff2_g.reshape(4 * d, d)
    c_all = c_all.reshape(N_DEV, d)

    n_ada = w_ada.shape[-1]
    b_ada_cols = lax.dynamic_slice_in_dim(b_ada, me * n_ada, n_ada, axis=1)
    mod_part = _ada_fwd(c_all, w_ada[0], b_ada_cols)
    (mod_all,) = _exchange("gather_mod", [mod_part], [False])
    mod = lax.dynamic_index_in_dim(mod_all, me, axis=1, keepdims=False).reshape(1, 6 * d)
    sh1, sc1, g1, sh2, sc2, g2 = (mod[:, i * d:(i + 1) * d] for i in range(6))

    (h,) = _rowwise("norm1", _norm_mod, steps, [xs, norm1_g, sc1, sh1],
                    [_row(tm, d), _vec(d), _vec(d), _vec(d)], [_sds((s_len, d), bf16)], [_row(tm, d)], [False])
    (proj,) = _mm_nn_sharded("in_proj", h, win_g)

    def qk_fn(q, k, gq, gk):
        return _head_rms(q, gq), _head_rms(k, gk)

    qn, kn = _rowwise("qk_norm", qk_fn, steps, [proj, proj, q_norm_g, k_norm_g],
                      [_row(tm, aw, 0), _row(tm, aw, 1), _vec(HEAD), _vec(HEAD)],
                      [_sds((s_len, aw), f32)] * 2, [_row(tm, aw)] * 2, [False] * 2)
    v_blk = 2 * aw // HEAD

    slopes = _slope_table(n_heads)
    pat = [_attn_fwd(qn, kn, proj, v_blk, dil, slopes) for _, dil in DILATIONS]

    def attn_mix_fn(o1, l1, o2, l2, o3, l3):
        m = jnp.maximum(jnp.maximum(l1, l2), l3)
        e1, e2, e3 = jnp.exp(l1 - m), jnp.exp(l2 - m), jnp.exp(l3 - m)
        tot = e1 + e2 + e3
        return (e1 * o1 + e2 * o2 + e3 * o3) / tot, m + jnp.log(tot)

    attn, lse = _rowwise("attn_mix", attn_mix_fn, steps, [t for ol in pat for t in ol], [_row(tm, aw)] * 6,
                         [_sds((s_len, aw), f32)] * 2, [_row(tm, aw)] * 2, [False] * 2)

    lam_re2, lam_im2 = lam_re[0], lam_im[0]
    log_step2 = log_step[0].reshape(n_groups, 1)
    b_re2 = b_re[0].reshape(n_groups, n_state * SSM_GROUP)
    b_im2 = b_im[0].reshape(n_groups, n_state * SSM_GROUP)
    expand = jnp.repeat(jnp.eye(n_state, dtype=f32), SSM_GROUP, axis=1)
    a_re, a_im, bb_re2, bb_im2 = _ssm_params(lam_re2, lam_im2, log_step2, b_re2, b_im2, expand)
    a2 = jnp.zeros((8, gp), f32).at[0].set(a_re.reshape(gp)).at[1].set(a_im.reshape(gp))

    def by_block(t):
        return t.reshape(ngs, GROUPS_PER_BLOCK, *t.shape[1:])

    bb_re4 = by_block(bb_re2.reshape(n_groups, n_state, SSM_GROUP))
    bb_im4 = by_block(bb_im2.reshape(n_groups, n_state, SSM_GROUP))
    c_re4, c_im4 = by_block(c_re[0]), by_block(c_im[0])
    w_bu_re = _bd_weight(bb_re4.transpose(0, 1, 3, 2))
    w_bu_im = _bd_weight(bb_im4.transpose(0, 1, 3, 2))
    w_y_re = _bd_weight(c_re4.transpose(0, 1, 3, 2))
    w_y_im = _bd_weight(-c_im4.transpose(0, 1, 3, 2))
    w_g_re, w_g_im = _bd_weight(c_re4), _bd_weight(-c_im4)
    w_du_re, w_du_im = _bd_weight(bb_re4), _bd_weight(bb_im4)

    u_blk = 3 * aw // LANE

    def bd_u(name, w):
        ngs_ = w.shape[0]

        def body(u_ref, w_ref, o_ref):
            o_ref[...] = jnp.dot(u_ref[...].astype(bf16), w_ref[...], preferred_element_type=f32)

        tmm = _tile(s_len, 1024)
        return _pcall(body, name=name, grid=(s_len // tmm, ngs_),
                      in_specs=[pl.BlockSpec((tmm, LANE), lambda i, s: (i, u_blk + s)),
                                pl.BlockSpec((None, LANE, sb), lambda i, s: (s, 0, 0))],
                      out_specs=pl.BlockSpec((tmm, sb), lambda i, s: (i, s)), out_shape=_sds((s_len, gp), f32),
                      sem=("parallel", "parallel"))(proj, w)

    bu_re, bu_im = bd_u("ssm_bu_re", w_bu_re), bd_u("ssm_bu_im", w_bu_im)
    h_re, h_im, hin_f = _scan("ssm_scan", bu_re, bu_im, a2, reverse=False)
    ymm = _bd_nn("ssm_y", [h_re, h_im], [w_y_re, w_y_im], sb, LANE)

    u_spec = _row(tm, sw, 3 * aw // sw)
    (yg,) = _rowwise("ssm_gelu", _ypre_fn, steps, [ymm, proj, d_skip], [_row(tm, sw), u_spec, _vec(sw)],
                     [_sds((s_len, sw), f32)], [_row(tm, sw)], [False])
    (z,) = _mm_nn("glu_proj", yg, wglu_g)
    (cat,) = _rowwise("mix_norm", _mix_fn, steps, [attn, yg, z, b_glu, attn_out_g, ssm_out_g],
                      [_row(tm, aw), _row(tm, sw), _row(tm, sw), _vec(sw), _vec(aw), _vec(sw)],
                      [_sds((s_len, d), bf16)], [_row(tm, d)], [False])
    (mixed,) = _mm_nn("out_proj", cat, wout_g)

    def res_norm2_fn(x_, mixed_, g1_, gn, sc, sh):
        x1_ = x_ + g1_ * mixed_
        return x1_, _norm_mod(x1_, gn, sc, sh)

    x1, h2 = _rowwise("norm2", res_norm2_fn, steps, [xs, mixed, g1, norm2_g, sc2, sh2],
                      [_row(tm, d), _row(tm, d)] + [_vec(d)] * 4,
                      [_sds((s_len, d), f32), _sds((s_len, d), bf16)], [_row(tm, d)] * 2, [False] * 2)

    def act_epilogue(acc):
        r = jnp.maximum(acc, 0.0)
        return acc, r * r

    a_ff, act = _mm_nn_sharded("ff1", h2, wff1_g, epilogue=act_epilogue,
                               outs=[_sds((s_len, 4 * d), f32), _sds((s_len, 4 * d), bf16)])
    (ff,) = _mm_nn("ff2", act, wff2_g, tn=2048, tk=512)

    def loss_fn(x1_, ff_, tgt_, g2_):
        e = x1_ + g2_ * ff_ - tgt_
        dy_ = e * (1.0 / d)
        part = jnp.full((1, LANE), 0.5 / d, f32) * jnp.sum(e * e)
        return dy_, g2_ * dy_, part, jnp.sum(dy_ * ff_, axis=0, keepdims=True)

    dy, dff, loss_part, d_g2 = _rowwise(
        "loss", loss_fn, steps, [x1, ff, tgt, g2], [_row(tm, d)] * 3 + [_vec(d)],
        [_sds((s_len, d), f32), _sds((s_len, d), bf16), _sds((1, LANE), f32), _sds((1, d), f32)],
        [_row(tm, d), _row(tm, d), _vec(LANE), _vec(d)], [False, False, True, True])
    loss = lax.psum(loss_part[0, 0], ("x", "y", "c"))

    def dact_epilogue(acc, a_):
        return (acc * (2.0 * jnp.maximum(a_, 0.0)),)

    (da,) = _mm_nt("ff2_dx", dff, wff2_g, epilogue=dact_epilogue, extra=[a_ff], outs=[_sds((s_len, 4 * d), bf16)])
    g_wff2 = _mm_tn("ff2_dw", act, dff).reshape(N_DEV, 4 * d // N_DEV, d)
    dh2 = _mm_nt_sharded("ff1_dx", da, wff1_g)
    g_wff1 = _mm_tn_sharded("ff1_dw", h2, da, N_DEV)

    def norm2_bwd_fn(dh2_, x1_, dy_, mixed_, gn, sc, sh, g1_):
        _, vjp = jax.vjp(_norm_mod, x1_, gn, sc, sh)
        dx, dgn, dsc, dsh = vjp(dh2_)
        dx1_ = dy_ + dx
        return dx1_, g1_ * dx1_, dgn, dsc, dsh, jnp.sum(dx1_ * mixed_, axis=0, keepdims=True)

    dx1, dmixed, d_norm2_g, d_sc2, d_sh2, d_g1 = _rowwise(
        "norm2_bwd", norm2_bwd_fn, steps, [dh2, x1, dy, mixed, norm2_g, sc2, sh2, g1],
        [_row(tm, d)] * 4 + [_vec(d)] * 4,
        [_sds((s_len, d), f32), _sds((s_len, d), bf16)] + [_sds((1, d), f32)] * 4,
        [_row(tm, d)] * 2 + [_vec(d)] * 4, [False, False, True, True, True, True])

    (dcat,) = _mm_nt("out_dx", dmixed, wout_g)
    g_wout = _mm_tn("out_dw", cat, dmixed).reshape(N_DEV, d // N_DEV, d)

    def mix_bwd_fn(dcat_, attn_, yg_, z_, bglu, ga, gs):
        _, vjp = jax.vjp(_mix_fn, attn_, yg_, z_, bglu, ga, gs)
        dattn_, dyg_, dz_, dbglu, dga, dgs = vjp(dcat_)
        prod = dattn_ * attn_
        dd_ = jnp.concatenate([jnp.broadcast_to(jnp.sum(prod[:, i * HEAD:(i + 1) * HEAD], axis=1, keepdims=True),
                                                (prod.shape[0], HEAD)) for i in range(n_heads)], axis=1)
        return dattn_, dd_, dyg_, dz_, dbglu, dga, dgs

    dattn, dd, dyg1, dz, d_b_glu, d_attn_out_g, d_ssm_out_g = _rowwise(
        "mix_bwd", mix_bwd_fn, steps, [dcat, attn, yg, z, b_glu, attn_out_g, ssm_out_g],
        [_row(tm, d), _row(tm, aw), _row(tm, sw), _row(tm, sw), _vec(sw), _vec(aw), _vec(sw)],
        [_sds((s_len, aw), f32), _sds((s_len, aw), f32), _sds((s_len, sw), f32), _sds((s_len, sw), bf16),
         _sds((1, sw), f32), _sds((1, aw), f32), _sds((1, sw), f32)],
        [_row(tm, aw), _row(tm, aw), _row(tm, sw), _row(tm, sw), _vec(sw), _vec(aw), _vec(sw)],
        [False] * 4 + [True] * 3)

    (dyg2,) = _mm_nt("glu_dx", dz, wglu_g)
    g_wglu = _mm_tn("glu_dw", yg, dz).reshape(N_DEV, sw // N_DEV, sw)

    def gelu_bwd_fn(dyg1_, dyg2_, ymm_, u_, dskip):
        _, vjp = jax.vjp(_ypre_fn, ymm_, u_, dskip)
        dymm, du_, ddskip = vjp(dyg1_ + dyg2_)
        return dymm, du_, ddskip

    dymm, du_skip, d_d_skip = _rowwise(
        "ssm_gelu_bwd", gelu_bwd_fn, steps, [dyg1, dyg2, ymm, proj, d_skip],
        [_row(tm, sw)] * 3 + [u_spec, _vec(sw)],
        [_sds((s_len, sw), f32), _sds((s_len, sw), f32), _sds((1, sw), f32)],
        [_row(tm, sw), _row(tm, sw), _vec(sw)], [False, False, True])

    gr = _bd_nn("ssm_dh_re", [dymm], [w_g_re], LANE, sb)
    gi = _bd_nn("ssm_dh_im", [dymm], [w_g_im], LANE, sb)
    lr, li, _, da_seg = _scan("ssm_adj", gr, gi, a2, reverse=True, da_from=(h_re, h_im, hin_f))
    du_ssm = _bd_nn("ssm_du", [lr, li], [w_du_re, w_du_im], sb, LANE)
    dc_re_c, dc_im_c = _bd_tn("ssm_dc", dymm, [h_re, h_im], LANE, sb)

    def bd_tn_u(name, b_list):
        tk = _tile(s_len, 1024)

        def body(u_ref, b0_ref, b1_ref, o0_ref, o1_ref):
            u_t = u_ref[...].astype(bf16)
            for b_ref, o_ref in ((b0_ref, o0_ref), (b1_ref, o1_ref)):
                @pl.when(pl.program_id(1) == 0)
                def _():
                    o_ref[...] = jnp.zeros_like(o_ref)

                o_ref[...] += lax.dot_general(u_t, b_ref[...].astype(bf16), (TN, ((), ())),
                                              preferred_element_type=f32)

        return _pcall(body, name=name, grid=(ngs, s_len // tk),
                      in_specs=[pl.BlockSpec((tk, LANE), lambda s, k: (k, u_blk + s))]
                      + [pl.BlockSpec((tk, sb), lambda s, k: (k, s))] * 2,
                      out_specs=[pl.BlockSpec((None, LANE, sb), lambda s, k: (s, 0, 0))] * 2,
                      out_shape=[_sds((ngs, LANE, sb), f32)] * 2, sem=("parallel", "arbitrary"))(proj, *b_list)

    dbb_re_c, dbb_im_c = bd_tn_u("ssm_dbbar", [lr, li])

    def diag_to_gpi(w):
        return _bd_diag(w, SSM_GROUP, n_state).transpose(0, 1, 3, 2).reshape(n_groups, n_state * SSM_GROUP)

    d_lam_re, d_lam_im, d_log_step, d_b_re2, d_b_im2 = _ssm_params_bwd(
        lam_re2, lam_im2, log_step2, b_re2, b_im2, expand,
        da_seg[0, 0].reshape(n_groups, n_state), da_seg[1, 0].reshape(n_groups, n_state),
        diag_to_gpi(dbb_re_c), diag_to_gpi(dbb_im_c))
    d_c_re = _bd_diag(dc_re_c, SSM_GROUP, n_state).reshape(n_groups, SSM_GROUP, n_state)
    d_c_im = -_bd_diag(dc_im_c, SSM_GROUP, n_state).reshape(n_groups, SSM_GROUP, n_state)

    grads_qkv = [_attn_bwd(qn, kn, proj, v_blk, dattn, lse, dd, dil, slopes) for _, dil in DILATIONS]

    def qkv_bwd_fn(q, k, gq, gk, dq1, dq2, dq3, dk1, dk2, dk3, dv1, dv2, dv3, du1, du2):
        _, vjp = jax.vjp(lambda q_, k_, gq_, gk_: (_head_rms(q_, gq_), _head_rms(k_, gk_)), q, k, gq, gk)
        dq, dk, dgq, dgk = vjp((dq1 + dq2 + dq3, dk1 + dk2 + dk3))
        return jnp.concatenate([dq, dk, dv1 + dv2 + dv3, du1 + du2], axis=1), dgq, dgk

    qkv_cots = [grads_qkv[p][i] for i in range(3) for p in range(3)]
    dproj, d_q_norm_g, d_k_norm_g = _rowwise(
        "qk_norm_bwd", qkv_bwd_fn, steps, [proj, proj, q_norm_g, k_norm_g, *qkv_cots, du_skip, du_ssm],
        [_row(tm, aw, 0), _row(tm, aw, 1), _vec(HEAD), _vec(HEAD)] + [_row(tm, aw)] * 9 + [_row(tm, sw)] * 2,
        [_sds((s_len, 3 * aw + sw), bf16), _sds((1, HEAD), f32), _sds((1, HEAD), f32)],
        [_row(tm, 3 * aw + sw), _vec(HEAD), _vec(HEAD)], [False, True, True])

    dh = _mm_nt_sharded("in_dx", dproj, win_g)
    g_win = _mm_tn_sharded("in_dw", h, dproj, N_DEV)

    def norm1_bwd_fn(dh_, x_, dx1_, gn, sc, sh):
        _, vjp = jax.vjp(_norm_mod, x_, gn, sc, sh)
        dx, dgn, dsc, dsh = vjp(dh_)
        return dx1_ + dx, dgn, dsc, dsh

    grad_x, d_norm1_g, d_sc1, d_sh1 = _rowwise(
        "norm1_bwd", norm1_bwd_fn, steps, [dh, xs, dx1, norm1_g, sc1, sh1], [_row(tm, d)] * 3 + [_vec(d)] * 3,
        [_sds((s_len, d), f32)] + [_sds((1, d), f32)] * 3, [_row(tm, d)] + [_vec(d)] * 3,
        [False, True, True, True])

    dmod = jnp.concatenate([d_sh1, d_sc1, d_g1, d_sh2, d_sc2, d_g2], axis=1)
    small_g = {"b_ada": dmod, "norm1_g": d_norm1_g, "q_norm_g": d_q_norm_g, "k_norm_g": d_k_norm_g,
               "lam_re": d_lam_re, "lam_im": d_lam_im, "log_step": d_log_step, "b_re": d_b_re2, "b_im": d_b_im2,
               "c_re": d_c_re, "c_im": d_c_im, "d_skip": d_d_skip, "b_glu": d_b_glu,
               "attn_out_g": d_attn_out_g, "ssm_out_g": d_ssm_out_g, "norm2_g": d_norm2_g}
    small_part = _pack([small_g[n] for n in SMALL])
    r_win, r_wglu, r_wout, r_wff1, r_wff2, r_small = _exchange(
        "scatter_grads", [g_win, g_wglu, g_wout, g_wff1, g_wff2, small_part], [True] * 5 + [False])

    res = {}
    dmod_all = r_small.reshape(N_DEV, -1)[:, :6 * d]
    g_wada = _ada_bwd(c_all, lax.dynamic_slice_in_dim(dmod_all, me * n_ada, n_ada, axis=1))
    res["w_ada"] = _adamw("adamw_w_ada", w_ada[0], m_w_ada[0], v_w_ada[0], g_wada, False)
    for name, stack in (("w_in", r_win), ("w_glu", r_wglu), ("w_out", r_wout), ("w_ff1", r_wff1), ("w_ff2", r_wff2)):
        res[name] = _adamw("adamw_" + name, wts[name][0], mom[name][0], var[name][0], stack, True)
    small_res = _adamw("adamw_small", _pack([wts[n] for n in SMALL]), _pack([mom[n] for n in SMALL]),
                       _pack([var[n] for n in SMALL]), r_small, True, rows=4096)
    off = 0
    for n in SMALL:
        size = wts[n].size
        res[n] = [t.reshape(-1)[off:off + size] for t in small_res]
        off += size

    out = [loss, grad_x[None]]
    for i in range(4):
        out += [res[n][i].reshape(wts[n].shape) for n in ORDER]
    return tuple(out)
```

```python
import math

import jax
import jax.numpy as jnp
from jax import lax
from jax.experimental import pallas as pl
from jax.experimental.pallas import tpu as pltpu

f32, bf16 = jnp.float32, jnp.bfloat16

N_DEV = 8
LANE = 128
HEAD = 128
SSM_GROUP = 16
GROUPS_PER_BLOCK = LANE // SSM_GROUP
DILATIONS = ((128, 1), (512, 4), (2048, 16))
BAND = 128
EPS = 1e-6
ADAM_LR, ADAM_B1, ADAM_B2, ADAM_EPS, ADAM_WD, ADAM_STEP = 0.001, 0.9, 0.999, 1e-08, 0.01, 10
NEG = -1e30
VMEM_LIMIT = 60 * 1024 * 1024
HI = lax.Precision.HIGHEST
MESH = pl.DeviceIdType.MESH


def _pcall(body, **kw):
    sem = kw.pop("sem", None)
    kw["compiler_params"] = pltpu.CompilerParams(dimension_semantics=sem, vmem_limit_bytes=VMEM_LIMIT)
    return pl.pallas_call(body, **kw)


def _tile(n, pref):
    t = min(n, pref)
    while n % t:
        t //= 2
    return t


def _sds(shape, dtype):
    return jax.ShapeDtypeStruct(shape, dtype)


def _rowwise(name, fn, steps, ins, in_specs, outs, out_specs, acc):
    n_in = len(ins)

    def body(*refs):
        res = fn(*[r[...] for r in refs[:n_in]])
        res = res if isinstance(res, (tuple, list)) else (res,)
        for r, o, a in zip(refs[n_in:], res, acc):
            if a:
                @pl.when(pl.program_id(0) == 0)
                def _():
                    r[...] = jnp.zeros_like(r)
                r[...] += o
            else:
                r[...] = o.astype(r.dtype)

    return _pcall(body, name=name, grid=(steps,), in_specs=in_specs, out_specs=out_specs, out_shape=outs,
                  sem=("arbitrary",))(*ins)


def _row(tm, c, blk=0):
    return pl.BlockSpec((tm, c), lambda i: (i, blk))


def _vec(c, blk=0):
    return pl.BlockSpec((1, c), lambda i: (0, blk))


def _rms(x, g):
    return x * lax.rsqrt(jnp.mean(x * x, axis=-1, keepdims=True) + EPS) * g


def _norm_mod(x, g, sc, sh):
    return _rms(x, g) * (1.0 + sc) + sh


def _head_rms(t, g):
    return jnp.concatenate([_rms(t[:, h * HEAD:(h + 1) * HEAD], g) for h in range(t.shape[1] // HEAD)], axis=1)


def _mix_fn(attn, yg, z, bglu, ga, gs):
    ssm = yg * jax.nn.sigmoid(z + bglu)
    return jnp.concatenate([_rms(attn, ga), _rms(ssm, gs)], axis=1)


def _ypre_fn(ymm, u, dskip):
    return jax.nn.gelu(ymm + dskip * u)


def _matmul(name, a, b, *, dims, grid, a_spec, b_spec, acc_shape, outs, out_specs, extra=(), extra_specs=(),
            epilogue=None):
    gk = grid[2]
    n_x = len(extra)

    def body(a_ref, b_ref, *rest):
        acc = rest[-1]
        k = pl.program_id(2)

        @pl.when(k == 0)
        def _():
            acc[...] = jnp.zeros_like(acc)

        acc[...] += lax.dot_general(a_ref[...].astype(bf16), b_ref[...].astype(bf16), (dims, ((), ())),
                                    preferred_element_type=f32)

        @pl.when(k == gk - 1)
        def _():
            xs = [r[...] for r in rest[:n_x]]
            res = epilogue(acc[...], *xs) if epilogue is not None else (acc[...],)
            for r, o in zip(rest[n_x:-1], res):
                r[...] = o.astype(r.dtype)

    return _pcall(body, name=name, grid=grid, in_specs=[a_spec, b_spec, *extra_specs], out_specs=out_specs,
                  out_shape=outs, scratch_shapes=[pltpu.VMEM(acc_shape, f32)],
                  sem=("parallel", "parallel", "arbitrary"))(a, b, *extra)


NN = ((1,), (0,))
NT = ((1,), (1,))
TN = ((0,), (0,))


def _mm_nn(name, a, b, out_dtype=f32, tm=1024, tn=1024, tk=2048, epilogue=None, extra=(), outs=None):
    m, kd = a.shape
    n = b.shape[1]
    tm, tn, tk = _tile(m, tm), _tile(n, tn), _tile(kd, tk)
    o_spec = pl.BlockSpec((tm, tn), lambda i, j, k: (i, j))
    outs = outs if outs is not None else [_sds((m, n), out_dtype)]
    return _matmul(name, a, b, dims=NN, grid=(m // tm, n // tn, kd // tk),
                   a_spec=pl.BlockSpec((tm, tk), lambda i, j, k: (i, k)),
                   b_spec=pl.BlockSpec((tk, tn), lambda i, j, k: (k, j)),
                   acc_shape=(tm, tn), outs=outs, out_specs=[o_spec] * len(outs),
                   extra=extra, extra_specs=[o_spec] * len(extra), epilogue=epilogue)


def _mm_nn_sharded(name, a, b3, out_dtype=f32, tm=1024, tk=2048, epilogue=None, outs=None):
    m, kd = a.shape
    nsh, _, n = b3.shape
    tm, tk = _tile(m, tm), _tile(kd, tk)
    o_spec = pl.BlockSpec((tm, n), lambda i, j, k: (i, j))
    outs = outs if outs is not None else [_sds((m, nsh * n), out_dtype)]
    return _matmul(name, a, b3, dims=NN, grid=(m // tm, nsh, kd // tk),
                   a_spec=pl.BlockSpec((tm, tk), lambda i, j, k: (i, k)),
                   b_spec=pl.BlockSpec((None, tk, n), lambda i, j, k: (j, k, 0)),
                   acc_shape=(tm, n), outs=outs, out_specs=[o_spec] * len(outs), epilogue=epilogue)


def _mm_nt(name, a, b, out_dtype=f32, tm=1024, tn=1024, tk=1024, epilogue=None, extra=(), outs=None):
    m, kd = a.shape
    n = b.shape[0]
    tm, tn, tk = _tile(m, tm), _tile(n, tn), _tile(kd, tk)
    o_spec = pl.BlockSpec((tm, tn), lambda i, j, k: (i, j))
    outs = outs if outs is not None else [_sds((m, n), out_dtype)]
    return _matmul(name, a, b, dims=NT, grid=(m // tm, n // tn, kd // tk),
                   a_spec=pl.BlockSpec((tm, tk), lambda i, j, k: (i, k)),
                   b_spec=pl.BlockSpec((tn, tk), lambda i, j, k: (j, k)),
                   acc_shape=(tm, tn), outs=outs, out_specs=[o_spec] * len(outs),
                   extra=extra, extra_specs=[o_spec] * len(extra), epilogue=epilogue)


def _mm_nt_sharded(name, a, b3, out_dtype=f32, tm=512, tn=2048):
    m = a.shape[0]
    nsh, n_out, n = b3.shape
    tm, tn = _tile(m, tm), _tile(n_out, tn)
    return _matmul(name, a, b3, dims=NT, grid=(m // tm, n_out // tn, nsh),
                   a_spec=pl.BlockSpec((tm, n), lambda i, j, k: (i, k)),
                   b_spec=pl.BlockSpec((None, tn, n), lambda i, j, k: (k, j, 0)),
                   acc_shape=(tm, tn), outs=[_sds((m, n_out), out_dtype)],
                   out_specs=[pl.BlockSpec((tm, tn), lambda i, j, k: (i, j))])[0]


def _mm_tn(name, a, b, out_dtype=bf16, tm=1024, tn=1024, tk=1024):
    t, m = a.shape
    n = b.shape[1]
    tm, tn, tk = _tile(m, tm), _tile(n, tn), _tile(t, tk)
    return _matmul(name, a, b, dims=TN, grid=(m // tm, n // tn, t // tk),
                   a_spec=pl.BlockSpec((tk, tm), lambda i, j, k: (k, i)),
                   b_spec=pl.BlockSpec((tk, tn), lambda i, j, k: (k, j)),
                   acc_shape=(tm, tn), outs=[_sds((m, n), out_dtype)],
                   out_specs=[pl.BlockSpec((tm, tn), lambda i, j, k: (i, j))])[0]


def _mm_tn_sharded(name, a, b, nsh, out_dtype=bf16, tm=1024, tk=1024):
    t, m = a.shape
    n = b.shape[1] // nsh
    tm, tk = _tile(m, tm), _tile(t, tk)
    return _matmul(name, a, b, dims=TN, grid=(m // tm, nsh, t // tk),
                   a_spec=pl.BlockSpec((tk, tm), lambda i, j, k: (k, i)),
                   b_spec=pl.BlockSpec((tk, n), lambda i, j, k: (k, j)),
                   acc_shape=(tm, n), outs=[_sds((nsh, m, n), out_dtype)],
                   out_specs=[pl.BlockSpec((None, tm, n), lambda i, j, k: (j, i, 0))])[0]


def _bd_nn(name, a_list, w_list, ka, nb, out_dtype=f32, tm=1024):
    n_q = len(a_list)
    m = a_list[0].shape[0]
    ngs = w_list[0].shape[0]
    tm = _tile(m, tm)

    def body(*refs):
        o_ref = refs[-1]
        tot = None
        for q in range(n_q):
            p = jnp.dot(refs[q][...].astype(bf16), refs[n_q + q][...], preferred_element_type=f32)
            tot = p if tot is None else tot + p
        o_ref[...] = tot.astype(o_ref.dtype)

    return _pcall(body, name=name, grid=(m // tm, ngs),
                  in_specs=[pl.BlockSpec((tm, ka), lambda i, s: (i, s))] * n_q
                  + [pl.BlockSpec((None, ka, nb), lambda i, s: (s, 0, 0))] * n_q,
                  out_specs=pl.BlockSpec((tm, nb), lambda i, s: (i, s)),
                  out_shape=_sds((m, ngs * nb), out_dtype), sem=("parallel", "parallel"))(*a_list, *w_list)


def _bd_tn(name, a, b_list, ra, cb, tk=1024):
    n_q = len(b_list)
    t = a.shape[0]
    ngs = a.shape[1] // ra
    tk = _tile(t, tk)

    def body(*refs):
        a_t = refs[0][...].astype(bf16)
        for q in range(n_q):
            o_ref = refs[1 + n_q + q]

            @pl.when(pl.program_id(1) == 0)
            def _():
                o_ref[...] = jnp.zeros_like(o_ref)

            o_ref[...] += lax.dot_general(a_t, refs[1 + q][...].astype(bf16), (TN, ((), ())),
                                          preferred_element_type=f32)

    return _pcall(body, name=name, grid=(ngs, t // tk),
                  in_specs=[pl.BlockSpec((tk, ra), lambda s, k: (k, s))]
                  + [pl.BlockSpec((tk, cb), lambda s, k: (k, s))] * n_q,
                  out_specs=[pl.BlockSpec((None, ra, cb), lambda s, k: (s, 0, 0))] * n_q,
                  out_shape=[_sds((ngs, ra, cb), f32)] * n_q, sem=("parallel", "arbitrary"))(a, *b_list)


def _bd_weight(t4):
    ngs, gb, r, c = t4.shape
    eye = jnp.eye(gb, dtype=t4.dtype)
    return jnp.einsum("sgrc,gh->sgrhc", t4, eye).reshape(ngs, gb * r, gb * c).astype(bf16)


def _bd_diag(w, r, c):
    ngs = w.shape[0]
    gb = w.shape[1] // r
    w5 = w.reshape(ngs, gb, r, gb, c)
    return jnp.einsum("sgrhc,gh->sgrc", w5, jnp.eye(gb, dtype=w.dtype))


SCAN_CHAINS = 8


def _scan_segments(s_len):
    nch = SCAN_CHAINS
    while s_len % (8 * nch) or (s_len // (8 * nch)) & (s_len // (8 * nch) - 1):
        nch //= 2
    return 8 * nch


def _to_segments(t, nseg):
    s_len, c = t.shape
    return t.reshape(nseg, s_len // nseg, c).transpose(1, 0, 2).reshape(s_len, c)


def _from_segments(t, nseg):
    s_len, c = t.shape
    return t.reshape(s_len // nseg, nseg, c).transpose(1, 0, 2).reshape(s_len, c)


def _scan(name, xr, xi, a2, *, reverse, da_from=None):
    s_len, gp = xr.shape
    nseg = _scan_segments(s_len)
    nch = nseg // 8
    seg = s_len // nseg
    n_sq = int(math.log2(seg))
    assert 2 ** n_sq == seg
    with_da = da_from is not None

    def body(*refs):
        it = iter(refs)
        xr_ref, xi_ref, a_ref = next(it), next(it), next(it)
        if with_da:
            hr_ref, hi_ref, hin_ref = next(it), next(it), next(it)
        or_ref, oi_ref, oin_ref = next(it), next(it), next(it)
        if with_da:
            da_ref = next(it)

        ar = a_ref[0:1, :]
        ai = -a_ref[1:2, :] if reverse else a_ref[1:2, :]
        arb, aib = jnp.broadcast_to(ar, (8, LANE)), jnp.broadcast_to(ai, (8, LANE))

        def rows(ch, k):
            return pl.ds(pl.multiple_of(k * nseg + ch * 8, 8), 8)

        def advance(h, ch, k):
            hr, hi = h
            return (arb * hr - aib * hi + xr_ref[rows(ch, k), :], arb * hi + aib * hr + xi_ref[rows(ch, k), :])

        def kk(n):
            return seg - 1 - n if reverse else n

        zero = jnp.zeros((8, LANE), f32)

        def sweep1(n, hs):
            return tuple(advance(hs[ch], ch, kk(n)) for ch in range(nch))

        ends = lax.fori_loop(0, seg, sweep1, tuple((zero, zero) for _ in range(nch)))

        pr, pi = ar, ai
        for _ in range(n_sq):
            pr, pi = pr * pr - pi * pi, 2.0 * pr * pi
        in_r, in_i = [None] * nseg, [None] * nseg
        cr = ci = jnp.zeros((1, LANE), f32)
        for j in (range(nseg - 1, -1, -1) if reverse else range(nseg)):
            in_r[j], in_i[j] = cr, ci
            er, ei = ends[j // 8][0][j % 8:j % 8 + 1, :], ends[j // 8][1][j % 8:j % 8 + 1, :]
            cr, ci = er + pr * cr - pi * ci, ei + pr * ci + pi * cr
        h0 = tuple((jnp.concatenate(in_r[8 * ch:8 * ch + 8], axis=0), jnp.concatenate(in_i[8 * ch:8 * ch + 8], axis=0))
                   for ch in range(nch))
        for ch in range(nch):
            oin_ref[0, 8 * ch:8 * ch + 8, :] = h0[ch][0]
            oin_ref[1, 8 * ch:8 * ch + 8, :] = h0[ch][1]

        def emit(ch, k, h):
            or_ref[rows(ch, k), :] = h[0]
            oi_ref[rows(ch, k), :] = h[1]

        def pair(h, p):
            return h[0] * p[0] + h[1] * p[1], h[1] * p[0] - h[0] * p[1]

        def sweep2(n, carry):
            k = kk(n)
            hs = carry[:nch]
            new = tuple(advance(hs[ch], ch, k) for ch in range(nch))
            for ch in range(nch):
                emit(ch, k, new[ch])
            if not with_da:
                return new
            dr, di = carry[nch]
            for ch in range(nch):
                qr, qi = pair(new[ch], (hr_ref[rows(ch, k - 1), :], hi_ref[rows(ch, k - 1), :]))
                dr, di = dr + qr, di + qi
            return new + ((dr, di),)

        if with_da:
            carry = lax.fori_loop(0, seg - 1, sweep2, h0 + ((zero, zero),))
            dr, di = carry[nch]
            for ch in range(nch):
                new = advance(carry[ch], ch, 0)
                emit(ch, 0, new)
                qr, qi = pair(new, (hin_ref[0, 8 * ch:8 * ch + 8, :], hin_ref[1, 8 * ch:8 * ch + 8, :]))
                dr, di = dr + qr, di + qi
            da_ref[0] = jnp.sum(dr, axis=0, keepdims=True)
            da_ref[1] = jnp.sum(di, axis=0, keepdims=True)
        else:
            lax.fori_loop(0, seg, sweep2, h0)

    col = pl.BlockSpec((s_len, LANE), lambda l: (0, l))
    in_spec = pl.BlockSpec((2, nseg, LANE), lambda l: (0, 0, l))
    ins, in_specs = [xr, xi, a2], [col, col, pl.BlockSpec((8, LANE), lambda l: (0, l))]
    if with_da:
        ins += list(da_from)
        in_specs += [col, col, in_spec]
    outs = [_sds((s_len, gp), f32)] * 2 + [_sds((2, nseg, gp), f32)]
    out_specs = [col, col, in_spec]
    if with_da:
        outs.append(_sds((2, 1, gp), f32))
        out_specs.append(pl.BlockSpec((2, 1, LANE), lambda l: (0, 0, l)))
    return _pcall(body, name=name, grid=(gp // LANE,), in_specs=in_specs, out_specs=out_specs, out_shape=outs,
                  sem=("parallel",))(*ins)


def _ssm_param_fn(lam_re, lam_im, log_step, b_re2, b_im2, expand):
    step = jnp.exp(log_step)
    xr, xi = lam_re * step, lam_im * step
    mag = jnp.exp(xr)
    ar, ai = mag * jnp.cos(xi), mag * jnp.sin(xi)
    nr, ni = ar - 1.0, ai
    den = lam_re * lam_re + lam_im * lam_im
    cr = (nr * lam_re + ni * lam_im) / den
    ci = (ni * lam_re - nr * lam_im) / den
    cre = jnp.dot(cr, expand, precision=HI, preferred_element_type=f32)
    cie = jnp.dot(ci, expand, precision=HI, preferred_element_type=f32)
    return ar, ai, cre * b_re2 - cie * b_im2, cre * b_im2 + cie * b_re2


def _ssm_params(lam_re, lam_im, log_step, b_re2, b_im2, expand):
    def body(*refs):
        res = _ssm_param_fn(*[r[...] for r in refs[:6]])
        for r, o in zip(refs[6:], res):
            r[...] = o

    g, p = lam_re.shape
    return _pcall(body, name="ssm_params", out_shape=[_sds((g, p), f32)] * 2 + [_sds(b_re2.shape, f32)] * 2)(
        lam_re, lam_im, log_step, b_re2, b_im2, expand)


def _ssm_params_bwd(lam_re, lam_im, log_step, b_re2, b_im2, expand, d_ar, d_ai, d_bbr, d_bbi):
    def body(*refs):
        prim = [r[...] for r in refs[:5]]
        ex = refs[5][...]
        cot = tuple(r[...] for r in refs[6:10])
        _, vjp = jax.vjp(lambda *p_: _ssm_param_fn(*p_, ex), *prim)
        for r, o in zip(refs[10:], vjp(cot)):
            r[...] = o

    shapes = [lam_re.shape, lam_im.shape, log_step.shape, b_re2.shape, b_im2.shape]
    return _pcall(body, name="ssm_params_bwd", out_shape=[_sds(s, f32) for s in shapes])(
        lam_re, lam_im, log_step, b_re2, b_im2, expand, d_ar, d_ai, d_bbr, d_bbi)


def _slope_table(n_heads):
    s = 2.0 ** (-8.0 * (jnp.arange(n_heads, dtype=f32) + 1.0) / n_heads)
    return jnp.broadcast_to(s[:, None, None], (n_heads, 1, LANE))


def _band_probs(q, k, slope_d, shift, live, ref_col):
    s = lax.dot_general(q, k, (NT, ((), ())), preferred_element_type=f32) * (HEAD ** -0.5)
    qi = lax.broadcasted_iota(jnp.int32, (BAND, BAND), 0)
    ki = lax.broadcasted_iota(jnp.int32, (BAND, BAND), 1)
    s = s - slope_d * (qi - ki + shift).astype(f32)
    mask = ((ki >= qi) if shift else (ki <= qi)) & live
    if ref_col is None:
        return jnp.where(mask, s, NEG)
    return jnp.where(mask, jnp.exp(s - ref_col), 0.0)


def _attn_geometry(s_len, dil):
    piece = BAND * dil
    m = max(1, 8 // dil)
    while s_len % (piece * m):
        m //= 2
    return m, piece


def _stream_rows(start, dil):
    return pl.ds(start, BAND, stride=dil) if dil > 1 else pl.ds(start, BAND)


def _attn_fwd(qn, kn, proj, v_blk, dil, slopes):
    s_len, aw = qn.shape
    n_heads = aw // HEAD
    m, piece = _attn_geometry(s_len, dil)
    rows = m * piece

    def body(q_ref, k_ref, kp_ref, v_ref, vp_ref, sl_ref, o_ref, lse_ref):
        t = pl.program_id(1)
        slope_d = sl_ref[:, 0:1] * float(dil)
        for b in range(m):
            for r in range(dil):
                idx = _stream_rows(b * piece + r, dil)
                q, kc, vc = (ref[idx, :].astype(bf16) for ref in (q_ref, k_ref, v_ref))
                if b:
                    pidx = _stream_rows((b - 1) * piece + r, dil)
                    kp, vp, live = k_ref[pidx, :].astype(bf16), v_ref[pidx, :].astype(bf16), True
                else:
                    pidx = _stream_rows(r, dil)
                    kp, vp, live = kp_ref[pidx, :].astype(bf16), vp_ref[pidx, :].astype(bf16), t > 0
                s_c = _band_probs(q, kc, slope_d, 0, True, None)
                s_p = _band_probs(q, kp, slope_d, BAND, live, None)
                mx = jnp.maximum(jnp.max(s_c, axis=1, keepdims=True), jnp.max(s_p, axis=1, keepdims=True))
                p_c, p_p = jnp.exp(s_c - mx), jnp.exp(s_p - mx)
                den = jnp.sum(p_c, axis=1, keepdims=True) + jnp.sum(p_p, axis=1, keepdims=True)
                o = jnp.dot(p_c.astype(bf16), vc, preferred_element_type=f32)
                o += jnp.dot(p_p.astype(bf16), vp, preferred_element_type=f32)
                o_ref[idx, :] = o / den
                lse_ref[idx, :] = jnp.broadcast_to(mx + jnp.log(den), (BAND, HEAD))

    def cur(blk0):
        return pl.BlockSpec((rows, HEAD), lambda h, t: (t, blk0 + h))

    def prev(blk0):
        return pl.BlockSpec((piece, HEAD), lambda h, t: (jnp.maximum(t * m - 1, 0), blk0 + h))

    sl = pl.BlockSpec((None, 1, LANE), lambda h, t: (h, 0, 0))
    return _pcall(body, name=f"attn_fwd_d{dil}", grid=(n_heads, s_len // rows),
                  in_specs=[cur(0), cur(0), prev(0), cur(v_blk), prev(v_blk), sl], out_specs=[cur(0), cur(0)],
                  out_shape=[_sds((s_len, aw), f32)] * 2, sem=("parallel", "parallel"))(
        qn, kn, kn, proj, proj, slopes)


def _attn_bwd(qn, kn, proj, v_blk, do, lse, dd, dil, slopes):
    s_len, aw = qn.shape
    n_heads = aw // HEAD
    m, piece = _attn_geometry(s_len, dil)
    rows = m * piece
    n_tiles = s_len // rows
    scale = HEAD ** -0.5

    def body(q_ref, qx_ref, k_ref, kp_ref, v_ref, vp_ref, do_ref, dox_ref, l_ref, lx_ref, d_ref, dx_ref, sl_ref,
             dq_ref, dk_ref, dv_ref):
        t = pl.program_id(1)
        slope_d = sl_ref[:, 0:1] * float(dil)

        def query_side(ref_q, ref_do, ref_l, ref_d, idx):
            return (ref_q[idx, :].astype(bf16), ref_do[idx, :].astype(bf16), ref_l[idx, :][:, 0:1],
                    ref_d[idx, :][:, 0:1])

        def block(qs, k, v, shift, live):
            q, do_, l_col, d_col = qs
            p = _band_probs(q, k, slope_d, shift, live, l_col)
            dp = lax.dot_general(do_, v, (NT, ((), ())), preferred_element_type=f32)
            return p.astype(bf16), (p * (dp - d_col)).astype(bf16)

        def tn(a_, b_):
            return lax.dot_general(a_, b_, (TN, ((), ())), preferred_element_type=f32)

        for r in range(dil):
            pend = None
            for b in range(m + 1):
                last = b == m
                if last:
                    qs = query_side(qx_ref, dox_ref, lx_ref, dx_ref, _stream_rows(r, dil))
                    live = t < n_tiles - 1
                else:
                    idx = _stream_rows(b * piece + r, dil)
                    qs = query_side(q_ref, do_ref, l_ref, d_ref, idx)
                    kc, vc = k_ref[idx, :].astype(bf16), v_ref[idx, :].astype(bf16)
                if b == 0:
                    pidx = _stream_rows(r, dil)
                    kp, vp, live = kp_ref[pidx, :].astype(bf16), vp_ref[pidx, :].astype(bf16), t > 0
                elif not last:
                    kp, vp, live = kc_prev, vc_prev, True
                else:
                    kp, vp = kc_prev, vc_prev
                p_p, ds_p = block(qs, kp, vp, BAND, live)
                if pend is not None:
                    pidx_, dk_, dv_ = pend
                    dk_ref[pidx_, :] = (dk_ + tn(ds_p, qs[0])) * scale
                    dv_ref[pidx_, :] = dv_ + tn(p_p, qs[1])
                if last:
                    break
                p_c, ds_c = block(qs, kc, vc, 0, True)
                dq = jnp.dot(ds_c, kc, preferred_element_type=f32) + jnp.dot(ds_p, kp, preferred_element_type=f32)
                dq_ref[idx, :] = dq * scale
                pend = (idx, tn(ds_c, qs[0]), tn(p_c, qs[1]))
                kc_prev, vc_prev = kc, vc

    def cur(blk0):
        return pl.BlockSpec((rows, HEAD), lambda h, t: (t, blk0 + h))

    def prev(blk0):
        return pl.BlockSpec((piece, HEAD), lambda h, t: (jnp.maximum(t * m - 1, 0), blk0 + h))

    def nxt(blk0):
        return pl.BlockSpec((piece, HEAD), lambda h, t: (jnp.minimum(t * m + m, n_tiles * m - 1), blk0 + h))

    sl = pl.BlockSpec((None, 1, LANE), lambda h, t: (h, 0, 0))
    return _pcall(body, name=f"attn_bwd_d{dil}", grid=(n_heads, n_tiles),
                  in_specs=[cur(0), nxt(0), cur(0), prev(0), cur(v_blk), prev(v_blk), cur(0), nxt(0), cur(0), nxt(0),
                            cur(0), nxt(0), sl],
                  out_specs=[cur(0)] * 3, out_shape=[_sds((s_len, aw), f32)] * 3,
                  sem=("parallel", "parallel"))(qn, qn, kn, kn, proj, proj, do, do, lse, lse, dd, dd, slopes)


def _exchange(name, srcs, scatter):
    n = len(srcs)

    def body(*refs):
        src, out = refs[:n], refs[n:2 * n]
        send_sems, recv_sems, local_sems = refs[2 * n:]
        x, y, c = lax.axis_index("x"), lax.axis_index("y"), lax.axis_index("c")
        me = 4 * x + 2 * y + c

        def peer(r):
            return ((1 - x) if r & 4 else x, (1 - y) if r & 2 else y, (1 - c) if r & 1 else c)

        def lin(p):
            return 4 * p[0] + 2 * p[1] + p[2]

        def piece(a, idx):
            return src[a].at[idx] if scatter[a] else src[a]

        local, sends = [], []
        for a in range(n):
            cp = pltpu.make_async_copy(piece(a, me), out[a].at[me], local_sems.at[a])
            cp.start()
            local.append(cp)
        for r in range(1, N_DEV):
            p = peer(r)
            for a in range(n):
                cp = pltpu.make_async_remote_copy(src_ref=piece(a, lin(p)), dst_ref=out[a].at[me],
                                                  send_sem=send_sems.at[a, r - 1], recv_sem=recv_sems.at[a, r - 1],
                                                  device_id=p, device_id_type=MESH)
                cp.start()
                sends.append(cp)
        for r in range(1, N_DEV):
            p = peer(r)
            for a in range(n):
                pltpu.make_async_remote_copy(src_ref=piece(a, lin(p)), dst_ref=out[a].at[lin(p)],
                                             send_sem=send_sems.at[a, r - 1], recv_sem=recv_sems.at[a, r - 1],
                                             device_id=p, device_id_type=MESH).wait_recv()
        for cp in sends:
            cp.wait_send()
        for cp in local:
            cp.wait()

    def piece_shape(a):
        return srcs[a].shape[1:] if scatter[a] else srcs[a].shape

    any_spec = pl.BlockSpec(memory_space=pl.ANY)
    return _pcall(body, name=name, in_specs=[any_spec] * n, out_specs=[any_spec] * n,
                  out_shape=[_sds((N_DEV, *piece_shape(a)), srcs[a].dtype) for a in range(n)],
                  scratch_shapes=[pltpu.SemaphoreType.DMA((n, N_DEV - 1)), pltpu.SemaphoreType.DMA((n, N_DEV - 1)),
                                  pltpu.SemaphoreType.DMA((n,))])(*srcs)


def _adamw(name, w, m, v, g_or_stack, stacked, rows=256):
    r, c = w.shape
    tr = _tile(r, rows)

    def fn(w_, m_, v_, g_):
        if stacked:
            g = g_[0].astype(f32)
            for j in range(1, N_DEV):
                g = g + g_[j].astype(f32)
        else:
            g = g_
        m_new = ADAM_B1 * m_ + (1.0 - ADAM_B1) * g
        v_new = ADAM_B2 * v_ + (1.0 - ADAM_B2) * (g * g)
        m_hat = m_new / (1.0 - ADAM_B1 ** ADAM_STEP)
        v_hat = v_new / (1.0 - ADAM_B2 ** ADAM_STEP)
        delta = -ADAM_LR * (m_hat / (jnp.sqrt(v_hat) + ADAM_EPS) + ADAM_WD * w_)
        return g, delta, m_new, v_new

    blk = _row(tr, c)
    g_spec = pl.BlockSpec((N_DEV, tr, c), lambda i: (0, i, 0)) if stacked else blk
    return _rowwise(name, fn, r // tr, [w, m, v, g_or_stack], [blk, blk, blk, g_spec],
                    [_sds((r, c), f32)] * 4, [blk] * 4, [False] * 4)


def _ada_fwd(c_all, w_shard, b_shard):
    nb_, d = c_all.shape
    n = w_shard.shape[1]
    tn = _tile(n, 512)

    def body(c_ref, w_ref, b_ref, o_ref):
        a = jax.nn.silu(c_ref[...]).astype(bf16)
        o_ref[...] = jnp.dot(a, w_ref[...].astype(bf16), preferred_element_type=f32) + b_ref[...]

    return _pcall(body, name="ada_fwd", grid=(n // tn,),
                  in_specs=[pl.BlockSpec((nb_, d), lambda j: (0, 0)), pl.BlockSpec((d, tn), lambda j: (0, j)),
                            pl.BlockSpec((1, tn), lambda j: (0, j))],
                  out_specs=pl.BlockSpec((nb_, tn), lambda j: (0, j)), out_shape=_sds((nb_, n), f32),
                  sem=("parallel",))(c_all, w_shard, b_shard)


def _ada_bwd(c_all, dmod_cols):
    nb_, d = c_all.shape
    n = dmod_cols.shape[1]
    tn = _tile(n, 512)

    def body(c_ref, g_ref, o_ref):
        a = jax.nn.silu(c_ref[...]).astype(bf16).astype(f32)
        g = g_ref[...].astype(bf16).astype(f32)
        o_ref[...] = lax.dot_general(a, g, (TN, ((), ())), precision=HI, preferred_element_type=f32)

    return _pcall(body, name="ada_bwd", grid=(n // tn,),
                  in_specs=[pl.BlockSpec((nb_, d), lambda j: (0, 0)), pl.BlockSpec((nb_, tn), lambda j: (0, j))],
                  out_specs=pl.BlockSpec((d, tn), lambda j: (0, j)), out_shape=_sds((d, n), f32),
                  sem=("parallel",))(c_all, dmod_cols)


SMALL = ("b_ada", "norm1_g", "q_norm_g", "k_norm_g", "lam_re", "lam_im", "log_step", "b_re", "b_im", "c_re", "c_im",
         "d_skip", "b_glu", "attn_out_g", "ssm_out_g", "norm2_g")
BIG = ("w_ada", "w_in", "w_glu", "w_out", "w_ff1", "w_ff2")
ORDER = ("w_ada", "b_ada", "norm1_g", "w_in", "q_norm_g", "k_norm_g", "lam_re", "lam_im", "log_step", "b_re", "b_im",
         "c_re", "c_im", "d_skip", "w_glu", "b_glu", "attn_out_g", "ssm_out_g", "w_out", "norm2_g", "w_ff1", "w_ff2")


def _pack(parts):
    flat = jnp.concatenate([p.reshape(-1) for p in parts])
    pad = (-flat.shape[0]) % (8 * LANE)
    return jnp.pad(flat, (0, pad)).reshape(-1, LANE)


def kernel(x, c, w_ada, b_ada, norm1_g, w_in, q_norm_g, k_norm_g, lam_re, lam_im, log_step, b_re, b_im, c_re, c_im, d_skip, w_glu, b_glu, attn_out_g, ssm_out_g, w_out, norm2_g, w_ff1, w_ff2, loss_target, m_w_ada, m_b_ada, m_norm1_g, m_w_in, m_q_norm_g, m_k_norm_g, m_lam_re, m_lam_im, m_log_step, m_b_re, m_b_im, m_c_re, m_c_im, m_d_skip, m_w_glu, m_b_glu, m_attn_out_g, m_ssm_out_g, m_w_out, m_norm2_g, m_w_ff1, m_w_ff2, v_w_ada, v_b_ada, v_norm1_g, v_w_in, v_q_norm_g, v_k_norm_g, v_lam_re, v_lam_im, v_log_step, v_b_re, v_b_im, v_c_re, v_c_im, v_d_skip, v_w_glu, v_b_glu, v_attn_out_g, v_ssm_out_g, v_w_out, v_norm2_g, v_w_ff1, v_w_ff2):
    env = dict(locals())
    wts = {n: env[n] for n in ORDER}
    mom = {n: env["m_" + n] for n in ORDER}
    var = {n: env["v_" + n] for n in ORDER}

    xs, tgt = x[0], loss_target[0]
    s_len, d = xs.shape
    aw = d // 2
    sw = d - aw
    n_heads = aw // HEAD
    n_groups = sw // SSM_GROUP
    n_state = lam_re.shape[-1]
    ngs = sw // LANE
    gp = n_groups * n_state
    sb = GROUPS_PER_BLOCK * n_state
    tm = _tile(s_len, 256)
    steps = s_len // tm
    me = 4 * lax.axis_index("x") + 2 * lax.axis_index("y") + lax.axis_index("c")

    win_g, wglu_g, wout_g, wff1_g, wff2_g, c_all = _exchange(
        "gather_weights",
        [w_in[0].astype(bf16), w_glu[0].astype(bf16), w_out[0].astype(bf16), w_ff1[0].astype(bf16),
         w_ff2[0].astype(bf16), c], [False] * 6)
    wglu_g = wglu_g.reshape(sw, sw)
    wout_g = wout_g.reshape(d, d)
    wff2_g = wff2_g.reshape(4 * d, d)
    c_all = c_all.reshape(N_DEV, d)

    n_ada = w_ada.shape[-1]
    b_ada_cols = lax.dynamic_slice_in_dim(b_ada, me * n_ada, n_ada, axis=1)
    mod_part = _ada_fwd(c_all, w_ada[0], b_ada_cols)
    (mod_all,) = _exchange("gather_mod", [mod_part], [False])
    mod = lax.dynamic_index_in_dim(mod_all, me, axis=1, keepdims=False).reshape(1, 6 * d)
    sh1, sc1, g1, sh2, sc2, g2 = (mod[:, i * d:(i + 1) * d] for i in range(6))

    (h,) = _rowwise("norm1", _norm_mod, steps, [xs, norm1_g, sc1, sh1],
                    [_row(tm, d), _vec(d), _vec(d), _vec(d)], [_sds((s_len, d), bf16)], [_row(tm, d)], [False])
    (proj,) = _mm_nn_sharded("in_proj", h, win_g)

    def qk_fn(q, k, gq, gk):
        return _head_rms(q, gq), _head_rms(k, gk)

    qn, kn = _rowwise("qk_norm", qk_fn, steps, [proj, proj, q_norm_g, k_norm_g],
                      [_row(tm, aw, 0), _row(tm, aw, 1), _vec(HEAD), _vec(HEAD)],
                      [_sds((s_len, aw), f32)] * 2, [_row(tm, aw)] * 2, [False] * 2)
    v_blk = 2 * aw // HEAD

    slopes = _slope_table(n_heads)
    pat = [_attn_fwd(qn, kn, proj, v_blk, dil, slopes) for _, dil in DILATIONS]

    def attn_mix_fn(o1, l1, o2, l2, o3, l3):
        m = jnp.maximum(jnp.maximum(l1, l2), l3)
        e1, e2, e3 = jnp.exp(l1 - m), jnp.exp(l2 - m), jnp.exp(l3 - m)
        tot = e1 + e2 + e3
        return (e1 * o1 + e2 * o2 + e3 * o3) / tot, m + jnp.log(tot)

    attn, lse = _rowwise("attn_mix", attn_mix_fn, steps, [t for ol in pat for t in ol], [_row(tm, aw)] * 6,
                         [_sds((s_len, aw), f32)] * 2, [_row(tm, aw)] * 2, [False] * 2)

    lam_re2, lam_im2 = lam_re[0], lam_im[0]
    log_step2 = log_step[0].reshape(n_groups, 1)
    b_re2 = b_re[0].reshape(n_groups, n_state * SSM_GROUP)
    b_im2 = b_im[0].reshape(n_groups, n_state * SSM_GROUP)
    expand = jnp.repeat(jnp.eye(n_state, dtype=f32), SSM_GROUP, axis=1)
    a_re, a_im, bb_re2, bb_im2 = _ssm_params(lam_re2, lam_im2, log_step2, b_re2, b_im2, expand)
    a2 = jnp.zeros((8, gp), f32).at[0].set(a_re.reshape(gp)).at[1].set(a_im.reshape(gp))

    def by_block(t):
        return t.reshape(ngs, GROUPS_PER_BLOCK, *t.shape[1:])

    bb_re4 = by_block(bb_re2.reshape(n_groups, n_state, SSM_GROUP))
    bb_im4 = by_block(bb_im2.reshape(n_groups, n_state, SSM_GROUP))
    c_re4, c_im4 = by_block(c_re[0]), by_block(c_im[0])
    w_bu_re = _bd_weight(bb_re4.transpose(0, 1, 3, 2))
    w_bu_im = _bd_weight(bb_im4.transpose(0, 1, 3, 2))
    w_y_re = _bd_weight(c_re4.transpose(0, 1, 3, 2))
    w_y_im = _bd_weight(-c_im4.transpose(0, 1, 3, 2))
    w_g_re, w_g_im = _bd_weight(c_re4), _bd_weight(-c_im4)
    w_du_re, w_du_im = _bd_weight(bb_re4), _bd_weight(bb_im4)

    nseg = _scan_segments(s_len)
    u_seg = _to_segments(proj[:, 3 * aw:], nseg).astype(bf16)
    bu_re = _bd_nn("ssm_bu_re", [u_seg], [w_bu_re], LANE, sb)
    bu_im = _bd_nn("ssm_bu_im", [u_seg], [w_bu_im], LANE, sb)
    h_re, h_im, hin_f = _scan("ssm_scan", bu_re, bu_im, a2, reverse=False)
    ymm = _from_segments(_bd_nn("ssm_y", [h_re, h_im], [w_y_re, w_y_im], sb, LANE), nseg)

    u_spec = _row(tm, sw, 3 * aw // sw)
    (yg,) = _rowwise("ssm_gelu", _ypre_fn, steps, [ymm, proj, d_skip], [_row(tm, sw), u_spec, _vec(sw)],
                     [_sds((s_len, sw), f32)], [_row(tm, sw)], [False])
    (z,) = _mm_nn("glu_proj", yg, wglu_g)
    (cat,) = _rowwise("mix_norm", _mix_fn, steps, [attn, yg, z, b_glu, attn_out_g, ssm_out_g],
                      [_row(tm, aw), _row(tm, sw), _row(tm, sw), _vec(sw), _vec(aw), _vec(sw)],
                      [_sds((s_len, d), bf16)], [_row(tm, d)], [False])
    (mixed,) = _mm_nn("out_proj", cat, wout_g)

    def res_norm2_fn(x_, mixed_, g1_, gn, sc, sh):
        x1_ = x_ + g1_ * mixed_
        return x1_, _norm_mod(x1_, gn, sc, sh)

    x1, h2 = _rowwise("norm2", res_norm2_fn, steps, [xs, mixed, g1, norm2_g, sc2, sh2],
                      [_row(tm, d), _row(tm, d)] + [_vec(d)] * 4,
                      [_sds((s_len, d), f32), _sds((s_len, d), bf16)], [_row(tm, d)] * 2, [False] * 2)

    def act_epilogue(acc):
        r = jnp.maximum(acc, 0.0)
        return acc, r * r

    a_ff, act = _mm_nn_sharded("ff1", h2, wff1_g, epilogue=act_epilogue,
                               outs=[_sds((s_len, 4 * d), f32), _sds((s_len, 4 * d), bf16)])
    (ff,) = _mm_nn("ff2", act, wff2_g, tn=2048, tk=512)

    def loss_fn(x1_, ff_, tgt_, g2_):
        e = x1_ + g2_ * ff_ - tgt_
        dy_ = e * (1.0 / d)
        part = jnp.full((1, LANE), 0.5 / d, f32) * jnp.sum(e * e)
        return dy_, g2_ * dy_, part, jnp.sum(dy_ * ff_, axis=0, keepdims=True)

    dy, dff, loss_part, d_g2 = _rowwise(
        "loss", loss_fn, steps, [x1, ff, tgt, g2], [_row(tm, d)] * 3 + [_vec(d)],
        [_sds((s_len, d), f32), _sds((s_len, d), bf16), _sds((1, LANE), f32), _sds((1, d), f32)],
        [_row(tm, d), _row(tm, d), _vec(LANE), _vec(d)], [False, False, True, True])
    loss = lax.psum(loss_part[0, 0], ("x", "y", "c"))

    def dact_epilogue(acc, a_):
        return (acc * (2.0 * jnp.maximum(a_, 0.0)),)

    (da,) = _mm_nt("ff2_dx", dff, wff2_g, epilogue=dact_epilogue, extra=[a_ff], outs=[_sds((s_len, 4 * d), bf16)])
    g_wff2 = _mm_tn("ff2_dw", act, dff).reshape(N_DEV, 4 * d // N_DEV, d)
    dh2 = _mm_nt_sharded("ff1_dx", da, wff1_g)
    g_wff1 = _mm_tn_sharded("ff1_dw", h2, da, N_DEV)

    def norm2_bwd_fn(dh2_, x1_, dy_, mixed_, gn, sc, sh, g1_):
        _, vjp = jax.vjp(_norm_mod, x1_, gn, sc, sh)
        dx, dgn, dsc, dsh = vjp(dh2_)
        dx1_ = dy_ + dx
        return dx1_, g1_ * dx1_, dgn, dsc, dsh, jnp.sum(dx1_ * mixed_, axis=0, keepdims=True)

    dx1, dmixed, d_norm2_g, d_sc2, d_sh2, d_g1 = _rowwise(
        "norm2_bwd", norm2_bwd_fn, steps, [dh2, x1, dy, mixed, norm2_g, sc2, sh2, g1],
        [_row(tm, d)] * 4 + [_vec(d)] * 4,
        [_sds((s_len, d), f32), _sds((s_len, d), bf16)] + [_sds((1, d), f32)] * 4,
        [_row(tm, d)] * 2 + [_vec(d)] * 4, [False, False, True, True, True, True])

    (dcat,) = _mm_nt("out_dx", dmixed, wout_g)
    g_wout = _mm_tn("out_dw", cat, dmixed).reshape(N_DEV, d // N_DEV, d)

    def mix_bwd_fn(dcat_, attn_, yg_, z_, bglu, ga, gs):
        _, vjp = jax.vjp(_mix_fn, attn_, yg_, z_, bglu, ga, gs)
        dattn_, dyg_, dz_, dbglu, dga, dgs = vjp(dcat_)
        prod = dattn_ * attn_
        dd_ = jnp.concatenate([jnp.broadcast_to(jnp.sum(prod[:, i * HEAD:(i + 1) * HEAD], axis=1, keepdims=True),
                                                (prod.shape[0], HEAD)) for i in range(n_heads)], axis=1)
        return dattn_, dd_, dyg_, dz_, dbglu, dga, dgs

    dattn, dd, dyg1, dz, d_b_glu, d_attn_out_g, d_ssm_out_g = _rowwise(
        "mix_bwd", mix_bwd_fn, steps, [dcat, attn, yg, z, b_glu, attn_out_g, ssm_out_g],
        [_row(tm, d), _row(tm, aw), _row(tm, sw), _row(tm, sw), _vec(sw), _vec(aw), _vec(sw)],
        [_sds((s_len, aw), f32), _sds((s_len, aw), f32), _sds((s_len, sw), f32), _sds((s_len, sw), bf16),
         _sds((1, sw), f32), _sds((1, aw), f32), _sds((1, sw), f32)],
        [_row(tm, aw), _row(tm, aw), _row(tm, sw), _row(tm, sw), _vec(sw), _vec(aw), _vec(sw)],
        [False] * 4 + [True] * 3)

    (dyg2,) = _mm_nt("glu_dx", dz, wglu_g)
    g_wglu = _mm_tn("glu_dw", yg, dz).reshape(N_DEV, sw // N_DEV, sw)

    def gelu_bwd_fn(dyg1_, dyg2_, ymm_, u_, dskip):
        _, vjp = jax.vjp(_ypre_fn, ymm_, u_, dskip)
        dymm, du_, ddskip = vjp(dyg1_ + dyg2_)
        return dymm, du_, ddskip

    dymm, du_skip, d_d_skip = _rowwise(
        "ssm_gelu_bwd", gelu_bwd_fn, steps, [dyg1, dyg2, ymm, proj, d_skip],
        [_row(tm, sw)] * 3 + [u_spec, _vec(sw)],
        [_sds((s_len, sw), f32), _sds((s_len, sw), f32), _sds((1, sw), f32)],
        [_row(tm, sw), _row(tm, sw), _vec(sw)], [False, False, True])

    dymm_seg = _to_segments(dymm, nseg).astype(bf16)
    gr = _bd_nn("ssm_dh_re", [dymm_seg], [w_g_re], LANE, sb)
    gi = _bd_nn("ssm_dh_im", [dymm_seg], [w_g_im], LANE, sb)
    lr, li, _, da_seg = _scan("ssm_adj", gr, gi, a2, reverse=True, da_from=(h_re, h_im, hin_f))
    du_ssm = _from_segments(_bd_nn("ssm_du", [lr, li], [w_du_re, w_du_im], sb, LANE), nseg)
    dc_re_c, dc_im_c = _bd_tn("ssm_dc", dymm_seg, [h_re, h_im], LANE, sb)
    dbb_re_c, dbb_im_c = _bd_tn("ssm_dbbar", u_seg, [lr, li], LANE, sb)

    def diag_to_gpi(w):
        return _bd_diag(w, SSM_GROUP, n_state).transpose(0, 1, 3, 2).reshape(n_groups, n_state * SSM_GROUP)

    d_lam_re, d_lam_im, d_log_step, d_b_re2, d_b_im2 = _ssm_params_bwd(
        lam_re2, lam_im2, log_step2, b_re2, b_im2, expand,
        da_seg[0, 0].reshape(n_groups, n_state), da_seg[1, 0].reshape(n_groups, n_state),
        diag_to_gpi(dbb_re_c), diag_to_gpi(dbb_im_c))
    d_c_re = _bd_diag(dc_re_c, SSM_GROUP, n_state).reshape(n_groups, SSM_GROUP, n_state)
    d_c_im = -_bd_diag(dc_im_c, SSM_GROUP, n_state).reshape(n_groups, SSM_GROUP, n_state)

    grads_qkv = [_attn_bwd(qn, kn, proj, v_blk, dattn, lse, dd, dil, slopes) for _, dil in DILATIONS]

    def qkv_bwd_fn(q, k, gq, gk, dq1, dq2, dq3, dk1, dk2, dk3, dv1, dv2, dv3, du1, du2):
        _, vjp = jax.vjp(lambda q_, k_, gq_, gk_: (_head_rms(q_, gq_), _head_rms(k_, gk_)), q, k, gq, gk)
        dq, dk, dgq, dgk = vjp((dq1 + dq2 + dq3, dk1 + dk2 + dk3))
        return jnp.concatenate([dq, dk, dv1 + dv2 + dv3, du1 + du2], axis=1), dgq, dgk

    qkv_cots = [grads_qkv[p][i] for i in range(3) for p in range(3)]
    dproj, d_q_norm_g, d_k_norm_g = _rowwise(
        "qk_norm_bwd", qkv_bwd_fn, steps, [proj, proj, q_norm_g, k_norm_g, *qkv_cots, du_skip, du_ssm],
        [_row(tm, aw, 0), _row(tm, aw, 1), _vec(HEAD), _vec(HEAD)] + [_row(tm, aw)] * 9 + [_row(tm, sw)] * 2,
        [_sds((s_len, 3 * aw + sw), bf16), _sds((1, HEAD), f32), _sds((1, HEAD), f32)],
        [_row(tm, 3 * aw + sw), _vec(HEAD), _vec(HEAD)], [False, True, True])

    dh = _mm_nt_sharded("in_dx", dproj, win_g)
    g_win = _mm_tn_sharded("in_dw", h, dproj, N_DEV)

    def norm1_bwd_fn(dh_, x_, dx1_, gn, sc, sh):
        _, vjp = jax.vjp(_norm_mod, x_, gn, sc, sh)
        dx, dgn, dsc, dsh = vjp(dh_)
        return dx1_ + dx, dgn, dsc, dsh

    grad_x, d_norm1_g, d_sc1, d_sh1 = _rowwise(
        "norm1_bwd", norm1_bwd_fn, steps, [dh, xs, dx1, norm1_g, sc1, sh1], [_row(tm, d)] * 3 + [_vec(d)] * 3,
        [_sds((s_len, d), f32)] + [_sds((1, d), f32)] * 3, [_row(tm, d)] + [_vec(d)] * 3,
        [False, True, True, True])

    dmod = jnp.concatenate([d_sh1, d_sc1, d_g1, d_sh2, d_sc2, d_g2], axis=1)
    small_g = {"b_ada": dmod, "norm1_g": d_norm1_g, "q_norm_g": d_q_norm_g, "k_norm_g": d_k_norm_g,
               "lam_re": d_lam_re, "lam_im": d_lam_im, "log_step": d_log_step, "b_re": d_b_re2, "b_im": d_b_im2,
               "c_re": d_c_re, "c_im": d_c_im, "d_skip": d_d_skip, "b_glu": d_b_glu,
               "attn_out_g": d_attn_out_g, "ssm_out_g": d_ssm_out_g, "norm2_g": d_norm2_g}
    small_part = _pack([small_g[n] for n in SMALL])
    r_win, r_wglu, r_wout, r_wff1, r_wff2, r_small = _exchange(
        "scatter_grads", [g_win, g_wglu, g_wout, g_wff1, g_wff2, small_part], [True] * 5 + [False])

    res = {}
    dmod_all = r_small.reshape(N_DEV, -1)[:, :6 * d]
    g_wada = _ada_bwd(c_all, lax.dynamic_slice_in_dim(dmod_all, me * n_ada, n_ada, axis=1))
    res["w_ada"] = _adamw("adamw_w_ada", w_ada[0], m_w_ada[0], v_w_ada[0], g_wada, False)
    for name, stack in (("w_in", r_win), ("w_glu", r_wglu), ("w_out", r_wout), ("w_ff1", r_wff1), ("w_ff2", r_wff2)):
        res[name] = _adamw("adamw_" + name, wts[name][0], mom[name][0], var[name][0], stack, True)
    small_res = _adamw("adamw_small", _pack([wts[n] for n in SMALL]), _pack([mom[n] for n in SMALL]),
                       _pack([var[n] for n in SMALL]), r_small, True, rows=4096)
    off = 0
    for n in SMALL:
        size = wts[n].size
        res[n] = [t.reshape(-1)[off:off + size] for t in small_res]
        off += size

    out = [loss, grad_x[None]]
    for i in range(4):
        out += [res[n][i].reshape(wts[n].shape) for n in ORDER]
    return tuple(out)
```

```python
import math

import jax
import jax.numpy as jnp
from jax import lax
from jax.experimental import pallas as pl
from jax.experimental.pallas import tpu as pltpu

f32, bf16 = jnp.float32, jnp.bfloat16

N_DEV = 8
LANE = 128
HEAD = 128
SSM_GROUP = 16
GROUPS_PER_BLOCK = LANE // SSM_GROUP
DILATIONS = ((128, 1), (512, 4), (2048, 16))
BAND = 128
EPS = 1e-6
ADAM_LR, ADAM_B1, ADAM_B2, ADAM_EPS, ADAM_WD, ADAM_STEP = 0.001, 0.9, 0.999, 1e-08, 0.01, 10
NEG = -1e30
VMEM_LIMIT = 60 * 1024 * 1024
HI = lax.Precision.HIGHEST
MESH = pl.DeviceIdType.MESH


def _pcall(body, **kw):
    sem = kw.pop("sem", None)
    kw["compiler_params"] = pltpu.CompilerParams(dimension_semantics=sem, vmem_limit_bytes=VMEM_LIMIT)
    return pl.pallas_call(body, **kw)


def _tile(n, pref):
    t = min(n, pref)
    while n % t:
        t //= 2
    return t


def _sds(shape, dtype):
    return jax.ShapeDtypeStruct(shape, dtype)


def _rowwise(name, fn, steps, ins, in_specs, outs, out_specs, acc):
    n_in = len(ins)

    def body(*refs):
        res = fn(*[r[...] for r in refs[:n_in]])
        res = res if isinstance(res, (tuple, list)) else (res,)
        for r, o, a in zip(refs[n_in:], res, acc):
            if a:
                @pl.when(pl.program_id(0) == 0)
                def _():
                    r[...] = jnp.zeros_like(r)
                r[...] += o
            else:
                r[...] = o.astype(r.dtype)

    return _pcall(body, name=name, grid=(steps,), in_specs=in_specs, out_specs=out_specs, out_shape=outs,
                  sem=("arbitrary",))(*ins)


def _row(tm, c, blk=0):
    return pl.BlockSpec((tm, c), lambda i: (i, blk))


def _vec(c, blk=0):
    return pl.BlockSpec((1, c), lambda i: (0, blk))


def _rms(x, g):
    return x * lax.rsqrt(jnp.mean(x * x, axis=-1, keepdims=True) + EPS) * g


def _norm_mod(x, g, sc, sh):
    return _rms(x, g) * (1.0 + sc) + sh


def _head_rms(t, g):
    return jnp.concatenate([_rms(t[:, h * HEAD:(h + 1) * HEAD], g) for h in range(t.shape[1] // HEAD)], axis=1)


def _mix_fn(attn, yg, z, bglu, ga, gs):
    ssm = yg * jax.nn.sigmoid(z + bglu)
    return jnp.concatenate([_rms(attn, ga), _rms(ssm, gs)], axis=1)


def _ypre_fn(ymm, u, dskip):
    return jax.nn.gelu(ymm + dskip * u)


def _matmul(name, a, b, *, dims, grid, a_spec, b_spec, acc_shape, outs, out_specs, extra=(), extra_specs=(),
            epilogue=None):
    gk = grid[2]
    n_x = len(extra)

    def body(a_ref, b_ref, *rest):
        acc = rest[-1]
        k = pl.program_id(2)

        @pl.when(k == 0)
        def _():
            acc[...] = jnp.zeros_like(acc)

        acc[...] += lax.dot_general(a_ref[...].astype(bf16), b_ref[...].astype(bf16), (dims, ((), ())),
                                    preferred_element_type=f32)

        @pl.when(k == gk - 1)
        def _():
            xs = [r[...] for r in rest[:n_x]]
            res = epilogue(acc[...], *xs) if epilogue is not None else (acc[...],)
            for r, o in zip(rest[n_x:-1], res):
                r[...] = o.astype(r.dtype)

    return _pcall(body, name=name, grid=grid, in_specs=[a_spec, b_spec, *extra_specs], out_specs=out_specs,
                  out_shape=outs, scratch_shapes=[pltpu.VMEM(acc_shape, f32)],
                  sem=("parallel", "parallel", "arbitrary"))(a, b, *extra)


NN = ((1,), (0,))
NT = ((1,), (1,))
TN = ((0,), (0,))


def _mm_nn(name, a, b, out_dtype=f32, tm=1024, tn=1024, tk=2048, epilogue=None, extra=(), outs=None):
    m, kd = a.shape
    n = b.shape[1]
    tm, tn, tk = _tile(m, tm), _tile(n, tn), _tile(kd, tk)
    o_spec = pl.BlockSpec((tm, tn), lambda i, j, k: (i, j))
    outs = outs if outs is not None else [_sds((m, n), out_dtype)]
    return _matmul(name, a, b, dims=NN, grid=(m // tm, n // tn, kd // tk),
                   a_spec=pl.BlockSpec((tm, tk), lambda i, j, k: (i, k)),
                   b_spec=pl.BlockSpec((tk, tn), lambda i, j, k: (k, j)),
                   acc_shape=(tm, tn), outs=outs, out_specs=[o_spec] * len(outs),
                   extra=extra, extra_specs=[o_spec] * len(extra), epilogue=epilogue)


def _mm_nn_sharded(name, a, b3, out_dtype=f32, tm=1024, tk=2048, epilogue=None, outs=None):
    m, kd = a.shape
    nsh, _, n = b3.shape
    tm, tk = _tile(m, tm), _tile(kd, tk)
    o_spec = pl.BlockSpec((tm, n), lambda i, j, k: (i, j))
    outs = outs if outs is not None else [_sds((m, nsh * n), out_dtype)]
    return _matmul(name, a, b3, dims=NN, grid=(m // tm, nsh, kd // tk),
                   a_spec=pl.BlockSpec((tm, tk), lambda i, j, k: (i, k)),
                   b_spec=pl.BlockSpec((None, tk, n), lambda i, j, k: (j, k, 0)),
                   acc_shape=(tm, n), outs=outs, out_specs=[o_spec] * len(outs), epilogue=epilogue)


def _mm_nt(name, a, b, out_dtype=f32, tm=1024, tn=1024, tk=1024, epilogue=None, extra=(), outs=None):
    m, kd = a.shape
    n = b.shape[0]
    tm, tn, tk = _tile(m, tm), _tile(n, tn), _tile(kd, tk)
    o_spec = pl.BlockSpec((tm, tn), lambda i, j, k: (i, j))
    outs = outs if outs is not None else [_sds((m, n), out_dtype)]
    return _matmul(name, a, b, dims=NT, grid=(m // tm, n // tn, kd // tk),
                   a_spec=pl.BlockSpec((tm, tk), lambda i, j, k: (i, k)),
                   b_spec=pl.BlockSpec((tn, tk), lambda i, j, k: (j, k)),
                   acc_shape=(tm, tn), outs=outs, out_specs=[o_spec] * len(outs),
                   extra=extra, extra_specs=[o_spec] * len(extra), epilogue=epilogue)


def _mm_nt_sharded(name, a, b3, out_dtype=f32, tm=512, tn=2048):
    m = a.shape[0]
    nsh, n_out, n = b3.shape
    tm, tn = _tile(m, tm), _tile(n_out, tn)
    return _matmul(name, a, b3, dims=NT, grid=(m // tm, n_out // tn, nsh),
                   a_spec=pl.BlockSpec((tm, n), lambda i, j, k: (i, k)),
                   b_spec=pl.BlockSpec((None, tn, n), lambda i, j, k: (k, j, 0)),
                   acc_shape=(tm, tn), outs=[_sds((m, n_out), out_dtype)],
                   out_specs=[pl.BlockSpec((tm, tn), lambda i, j, k: (i, j))])[0]


def _mm_tn(name, a, b, out_dtype=bf16, tm=1024, tn=1024, tk=1024):
    t, m = a.shape
    n = b.shape[1]
    tm, tn, tk = _tile(m, tm), _tile(n, tn), _tile(t, tk)
    return _matmul(name, a, b, dims=TN, grid=(m // tm, n // tn, t // tk),
                   a_spec=pl.BlockSpec((tk, tm), lambda i, j, k: (k, i)),
                   b_spec=pl.BlockSpec((tk, tn), lambda i, j, k: (k, j)),
                   acc_shape=(tm, tn), outs=[_sds((m, n), out_dtype)],
                   out_specs=[pl.BlockSpec((tm, tn), lambda i, j, k: (i, j))])[0]


def _mm_tn_sharded(name, a, b, nsh, out_dtype=bf16, tm=1024, tk=1024):
    t, m = a.shape
    n = b.shape[1] // nsh
    tm, tk = _tile(m, tm), _tile(t, tk)
    return _matmul(name, a, b, dims=TN, grid=(m // tm, nsh, t // tk),
                   a_spec=pl.BlockSpec((tk, tm), lambda i, j, k: (k, i)),
                   b_spec=pl.BlockSpec((tk, n), lambda i, j, k: (k, j)),
                   acc_shape=(tm, n), outs=[_sds((nsh, m, n), out_dtype)],
                   out_specs=[pl.BlockSpec((None, tm, n), lambda i, j, k: (j, i, 0))])[0]


def _bd_nn(name, a_list, w_list, ka, nb, out_dtype=f32, tm=1024):
    n_q = len(a_list)
    m = a_list[0].shape[0]
    ngs = w_list[0].shape[0]
    tm = _tile(m, tm)

    def body(*refs):
        o_ref = refs[-1]
        tot = None
        for q in range(n_q):
            p = jnp.dot(refs[q][...].astype(bf16), refs[n_q + q][...], preferred_element_type=f32)
            tot = p if tot is None else tot + p
        o_ref[...] = tot.astype(o_ref.dtype)

    return _pcall(body, name=name, grid=(m // tm, ngs),
                  in_specs=[pl.BlockSpec((tm, ka), lambda i, s: (i, s))] * n_q
                  + [pl.BlockSpec((None, ka, nb), lambda i, s: (s, 0, 0))] * n_q,
                  out_specs=pl.BlockSpec((tm, nb), lambda i, s: (i, s)),
                  out_shape=_sds((m, ngs * nb), out_dtype), sem=("parallel", "parallel"))(*a_list, *w_list)


def _bd_tn(name, a, b_list, ra, cb, tk=1024):
    n_q = len(b_list)
    t = a.shape[0]
    ngs = a.shape[1] // ra
    tk = _tile(t, tk)

    def body(*refs):
        a_t = refs[0][...].astype(bf16)
        for q in range(n_q):
            o_ref = refs[1 + n_q + q]

            @pl.when(pl.program_id(1) == 0)
            def _():
                o_ref[...] = jnp.zeros_like(o_ref)

            o_ref[...] += lax.dot_general(a_t, refs[1 + q][...].astype(bf16), (TN, ((), ())),
                                          preferred_element_type=f32)

    return _pcall(body, name=name, grid=(ngs, t // tk),
                  in_specs=[pl.BlockSpec((tk, ra), lambda s, k: (k, s))]
                  + [pl.BlockSpec((tk, cb), lambda s, k: (k, s))] * n_q,
                  out_specs=[pl.BlockSpec((None, ra, cb), lambda s, k: (s, 0, 0))] * n_q,
                  out_shape=[_sds((ngs, ra, cb), f32)] * n_q, sem=("parallel", "arbitrary"))(a, *b_list)


def _bd_weight(t4):
    ngs, gb, r, c = t4.shape
    eye = jnp.eye(gb, dtype=t4.dtype)
    return jnp.einsum("sgrc,gh->sgrhc", t4, eye).reshape(ngs, gb * r, gb * c).astype(bf16)


def _bd_diag(w, r, c):
    ngs = w.shape[0]
    gb = w.shape[1] // r
    w5 = w.reshape(ngs, gb, r, gb, c)
    return jnp.einsum("sgrhc,gh->sgrc", w5, jnp.eye(gb, dtype=w.dtype))


SCAN_CHAINS = 8


def _scan_segments(s_len):
    nch = SCAN_CHAINS
    while s_len % (8 * nch) or (s_len // (8 * nch)) & (s_len // (8 * nch) - 1):
        nch //= 2
    return 8 * nch


def _to_segments(t, nseg):
    s_len, c = t.shape
    return t.reshape(nseg, s_len // nseg, c).transpose(1, 0, 2).reshape(s_len, c)


def _from_segments(t, nseg):
    s_len, c = t.shape
    return t.reshape(s_len // nseg, nseg, c).transpose(1, 0, 2).reshape(s_len, c)


def _scan(name, xr, xi, a2, *, reverse, da_from=None):
    s_len, gp = xr.shape
    nseg = _scan_segments(s_len)
    nch = nseg // 8
    seg = s_len // nseg
    n_sq = int(math.log2(seg))
    assert 2 ** n_sq == seg
    with_da = da_from is not None

    def body(*refs):
        it = iter(refs)
        xr_ref, xi_ref, a_ref = next(it), next(it), next(it)
        if with_da:
            hr_ref, hi_ref, hin_ref = next(it), next(it), next(it)
        or_ref, oi_ref, oin_ref = next(it), next(it), next(it)
        if with_da:
            da_ref = next(it)

        ar = a_ref[0:1, :]
        ai = -a_ref[1:2, :] if reverse else a_ref[1:2, :]
        arb, aib = jnp.broadcast_to(ar, (8, LANE)), jnp.broadcast_to(ai, (8, LANE))

        def rows(ch, k):
            return pl.ds(pl.multiple_of(k * nseg + ch * 8, 8), 8)

        def advance(h, ch, k):
            hr, hi = h
            return (arb * hr - aib * hi + xr_ref[rows(ch, k), :], arb * hi + aib * hr + xi_ref[rows(ch, k), :])

        def kk(n):
            return seg - 1 - n if reverse else n

        zero = jnp.zeros((8, LANE), f32)

        def sweep1(n, hs):
            return tuple(advance(hs[ch], ch, kk(n)) for ch in range(nch))

        ends = lax.fori_loop(0, seg, sweep1, tuple((zero, zero) for _ in range(nch)))

        pr, pi = ar, ai
        for _ in range(n_sq):
            pr, pi = pr * pr - pi * pi, 2.0 * pr * pi
        in_r, in_i = [None] * nseg, [None] * nseg
        cr = ci = jnp.zeros((1, LANE), f32)
        for j in (range(nseg - 1, -1, -1) if reverse else range(nseg)):
            in_r[j], in_i[j] = cr, ci
            er, ei = ends[j // 8][0][j % 8:j % 8 + 1, :], ends[j // 8][1][j % 8:j % 8 + 1, :]
            cr, ci = er + pr * cr - pi * ci, ei + pr * ci + pi * cr
        h0 = tuple((jnp.concatenate(in_r[8 * ch:8 * ch + 8], axis=0), jnp.concatenate(in_i[8 * ch:8 * ch + 8], axis=0))
                   for ch in range(nch))
        for ch in range(nch):
            oin_ref[0, 8 * ch:8 * ch + 8, :] = h0[ch][0]
            oin_ref[1, 8 * ch:8 * ch + 8, :] = h0[ch][1]

        def emit(ch, k, h):
            or_ref[rows(ch, k), :] = h[0]
            oi_ref[rows(ch, k), :] = h[1]

        def pair(h, p):
            return h[0] * p[0] + h[1] * p[1], h[1] * p[0] - h[0] * p[1]

        def sweep2(n, carry):
            k = kk(n)
            hs = carry[:nch]
            new = tuple(advance(hs[ch], ch, k) for ch in range(nch))
            for ch in range(nch):
                emit(ch, k, new[ch])
            if not with_da:
                return new
            dr, di = carry[nch]
            for ch in range(nch):
                qr, qi = pair(new[ch], (hr_ref[rows(ch, k - 1), :], hi_ref[rows(ch, k - 1), :]))
                dr, di = dr + qr, di + qi
            return new + ((dr, di),)

        if with_da:
            carry = lax.fori_loop(0, seg - 1, sweep2, h0 + ((zero, zero),))
            dr, di = carry[nch]
            for ch in range(nch):
                new = advance(carry[ch], ch, 0)
                emit(ch, 0, new)
                qr, qi = pair(new, (hin_ref[0, 8 * ch:8 * ch + 8, :], hin_ref[1, 8 * ch:8 * ch + 8, :]))
                dr, di = dr + qr, di + qi
            da_ref[0] = jnp.sum(dr, axis=0, keepdims=True)
            da_ref[1] = jnp.sum(di, axis=0, keepdims=True)
        else:
            lax.fori_loop(0, seg, sweep2, h0)

    col = pl.BlockSpec((s_len, LANE), lambda l: (0, l))
    in_spec = pl.BlockSpec((2, nseg, LANE), lambda l: (0, 0, l))
    ins, in_specs = [xr, xi, a2], [col, col, pl.BlockSpec((8, LANE), lambda l: (0, l))]
    if with_da:
        ins += list(da_from)
        in_specs += [col, col, in_spec]
    outs = [_sds((s_len, gp), f32)] * 2 + [_sds((2, nseg, gp), f32)]
    out_specs = [col, col, in_spec]
    if with_da:
        outs.append(_sds((2, 1, gp), f32))
        out_specs.append(pl.BlockSpec((2, 1, LANE), lambda l: (0, 0, l)))
    return _pcall(body, name=name, grid=(gp // LANE,), in_specs=in_specs, out_specs=out_specs, out_shape=outs,
                  sem=("parallel",))(*ins)


def _ssm_param_fn(lam_re, lam_im, log_step, b_re2, b_im2, expand):
    step = jnp.exp(log_step)
    xr, xi = lam_re * step, lam_im * step
    mag = jnp.exp(xr)
    ar, ai = mag * jnp.cos(xi), mag * jnp.sin(xi)
    nr, ni = ar - 1.0, ai
    den = lam_re * lam_re + lam_im * lam_im
    cr = (nr * lam_re + ni * lam_im) / den
    ci = (ni * lam_re - nr * lam_im) / den
    cre = jnp.dot(cr, expand, precision=HI, preferred_element_type=f32)
    cie = jnp.dot(ci, expand, precision=HI, preferred_element_type=f32)
    return ar, ai, cre * b_re2 - cie * b_im2, cre * b_im2 + cie * b_re2


def _ssm_params(lam_re, lam_im, log_step, b_re2, b_im2, expand):
    def body(*refs):
        res = _ssm_param_fn(*[r[...] for r in refs[:6]])
        for r, o in zip(refs[6:], res):
            r[...] = o

    g, p = lam_re.shape
    return _pcall(body, name="ssm_params", out_shape=[_sds((g, p), f32)] * 2 + [_sds(b_re2.shape, f32)] * 2)(
        lam_re, lam_im, log_step, b_re2, b_im2, expand)


def _ssm_params_bwd(lam_re, lam_im, log_step, b_re2, b_im2, expand, d_ar, d_ai, d_bbr, d_bbi):
    def body(*refs):
        prim = [r[...] for r in refs[:5]]
        ex = refs[5][...]
        cot = tuple(r[...] for r in refs[6:10])
        _, vjp = jax.vjp(lambda *p_: _ssm_param_fn(*p_, ex), *prim)
        for r, o in zip(refs[10:], vjp(cot)):
            r[...] = o

    shapes = [lam_re.shape, lam_im.shape, log_step.shape, b_re2.shape, b_im2.shape]
    return _pcall(body, name="ssm_params_bwd", out_shape=[_sds(s, f32) for s in shapes])(
        lam_re, lam_im, log_step, b_re2, b_im2, expand, d_ar, d_ai, d_bbr, d_bbi)


def _slope_table(n_heads):
    s = 2.0 ** (-8.0 * (jnp.arange(n_heads, dtype=f32) + 1.0) / n_heads)
    return jnp.broadcast_to(s[:, None, None], (n_heads, 1, LANE))


def _band_probs(q, k, slope_d, shift, live, ref_col):
    s = lax.dot_general(q, k, (NT, ((), ())), preferred_element_type=f32) * (HEAD ** -0.5)
    qi = lax.broadcasted_iota(jnp.int32, (BAND, BAND), 0)
    ki = lax.broadcasted_iota(jnp.int32, (BAND, BAND), 1)
    s = s - slope_d * (qi - ki + shift).astype(f32)
    mask = ((ki >= qi) if shift else (ki <= qi)) & live
    if ref_col is None:
        return jnp.where(mask, s, NEG)
    return jnp.where(mask, jnp.exp(s - ref_col), 0.0)


def _attn_geometry(s_len, dil):
    piece = BAND * dil
    m = max(1, 8 // dil)
    while s_len % (piece * m):
        m //= 2
    return m, piece


def _stream_rows(start, dil):
    return pl.ds(start, BAND, stride=dil) if dil > 1 else pl.ds(start, BAND)


def _attn_fwd(qn, kn, proj, v_blk, dil, slopes):
    s_len, aw = qn.shape
    n_heads = aw // HEAD
    m, piece = _attn_geometry(s_len, dil)
    rows = m * piece

    def body(q_ref, k_ref, kp_ref, v_ref, vp_ref, sl_ref, o_ref, lse_ref):
        t = pl.program_id(1)
        slope_d = sl_ref[:, 0:1] * float(dil)
        for b in range(m):
            for r in range(dil):
                idx = _stream_rows(b * piece + r, dil)
                q, kc, vc = (ref[idx, :].astype(bf16) for ref in (q_ref, k_ref, v_ref))
                if b:
                    pidx = _stream_rows((b - 1) * piece + r, dil)
                    kp, vp, live = k_ref[pidx, :].astype(bf16), v_ref[pidx, :].astype(bf16), True
                else:
                    pidx = _stream_rows(r, dil)
                    kp, vp, live = kp_ref[pidx, :].astype(bf16), vp_ref[pidx, :].astype(bf16), t > 0
                s_c = _band_probs(q, kc, slope_d, 0, True, None)
                s_p = _band_probs(q, kp, slope_d, BAND, live, None)
                mx = jnp.maximum(jnp.max(s_c, axis=1, keepdims=True), jnp.max(s_p, axis=1, keepdims=True))
                p_c, p_p = jnp.exp(s_c - mx), jnp.exp(s_p - mx)
                den = jnp.sum(p_c, axis=1, keepdims=True) + jnp.sum(p_p, axis=1, keepdims=True)
                o = jnp.dot(p_c.astype(bf16), vc, preferred_element_type=f32)
                o += jnp.dot(p_p.astype(bf16), vp, preferred_element_type=f32)
                o_ref[idx, :] = o / den
                lse_ref[idx, :] = jnp.broadcast_to(mx + jnp.log(den), (BAND, HEAD))

    def cur(blk0):
        return pl.BlockSpec((rows, HEAD), lambda h, t: (t, blk0 + h))

    def prev(blk0):
        return pl.BlockSpec((piece, HEAD), lambda h, t: (jnp.maximum(t * m - 1, 0), blk0 + h))

    sl = pl.BlockSpec((None, 1, LANE), lambda h, t: (h, 0, 0))
    return _pcall(body, name=f"attn_fwd_d{dil}", grid=(n_heads, s_len // rows),
                  in_specs=[cur(0), cur(0), prev(0), cur(v_blk), prev(v_blk), sl], out_specs=[cur(0), cur(0)],
                  out_shape=[_sds((s_len, aw), f32)] * 2, sem=("parallel", "parallel"))(
        qn, kn, kn, proj, proj, slopes)


def _attn_bwd(qn, kn, proj, v_blk, do, lse, dd, dil, slopes):
    s_len, aw = qn.shape
    n_heads = aw // HEAD
    m, piece = _attn_geometry(s_len, dil)
    rows = m * piece
    n_tiles = s_len // rows
    scale = HEAD ** -0.5

    def body(q_ref, qx_ref, k_ref, kp_ref, v_ref, vp_ref, do_ref, dox_ref, l_ref, lx_ref, d_ref, dx_ref, sl_ref,
             dq_ref, dk_ref, dv_ref):
        t = pl.program_id(1)
        slope_d = sl_ref[:, 0:1] * float(dil)

        def query_side(ref_q, ref_do, ref_l, ref_d, idx):
            return (ref_q[idx, :].astype(bf16), ref_do[idx, :].astype(bf16), ref_l[idx, :][:, 0:1],
                    ref_d[idx, :][:, 0:1])

        def block(qs, k, v, shift, live):
            q, do_, l_col, d_col = qs
            p = _band_probs(q, k, slope_d, shift, live, l_col)
            dp = lax.dot_general(do_, v, (NT, ((), ())), preferred_element_type=f32)
            return p.astype(bf16), (p * (dp - d_col)).astype(bf16)

        def tn(a_, b_):
            return lax.dot_general(a_, b_, (TN, ((), ())), preferred_element_type=f32)

        for r in range(dil):
            pend = None
            for b in range(m + 1):
                last = b == m
                if last:
                    qs = query_side(qx_ref, dox_ref, lx_ref, dx_ref, _stream_rows(r, dil))
                    live = t < n_tiles - 1
                else:
                    idx = _stream_rows(b * piece + r, dil)
                    qs = query_side(q_ref, do_ref, l_ref, d_ref, idx)
                    kc, vc = k_ref[idx, :].astype(bf16), v_ref[idx, :].astype(bf16)
                if b == 0:
                    pidx = _stream_rows(r, dil)
                    kp, vp, live = kp_ref[pidx, :].astype(bf16), vp_ref[pidx, :].astype(bf16), t > 0
                elif not last:
                    kp, vp, live = kc_prev, vc_prev, True
                else:
                    kp, vp = kc_prev, vc_prev
                p_p, ds_p = block(qs, kp, vp, BAND, live)
                if pend is not None:
                    pidx_, dk_, dv_ = pend
                    dk_ref[pidx_, :] = (dk_ + tn(ds_p, qs[0])) * scale
                    dv_ref[pidx_, :] = dv_ + tn(p_p, qs[1])
                if last:
                    break
                p_c, ds_c = block(qs, kc, vc, 0, True)
                dq = jnp.dot(ds_c, kc, preferred_element_type=f32) + jnp.dot(ds_p, kp, preferred_element_type=f32)
                dq_ref[idx, :] = dq * scale
                pend = (idx, tn(ds_c, qs[0]), tn(p_c, qs[1]))
                kc_prev, vc_prev = kc, vc

    def cur(blk0):
        return pl.BlockSpec((rows, HEAD), lambda h, t: (t, blk0 + h))

    def prev(blk0):
        return pl.BlockSpec((piece, HEAD), lambda h, t: (jnp.maximum(t * m - 1, 0), blk0 + h))

    def nxt(blk0):
        return pl.BlockSpec((piece, HEAD), lambda h, t: (jnp.minimum(t * m + m, n_tiles * m - 1), blk0 + h))

    sl = pl.BlockSpec((None, 1, LANE), lambda h, t: (h, 0, 0))
    return _pcall(body, name=f"attn_bwd_d{dil}", grid=(n_heads, n_tiles),
                  in_specs=[cur(0), nxt(0), cur(0), prev(0), cur(v_blk), prev(v_blk), cur(0), nxt(0), cur(0), nxt(0),
                            cur(0), nxt(0), sl],
                  out_specs=[cur(0)] * 3, out_shape=[_sds((s_len, aw), f32)] * 3,
                  sem=("parallel", "parallel"))(qn, qn, kn, kn, proj, proj, do, do, lse, lse, dd, dd, slopes)


def _exchange(name, srcs, scatter):
    n = len(srcs)

    def body(*refs):
        src, out = refs[:n], refs[n:2 * n]
        send_sems, recv_sems, local_sems = refs[2 * n:]
        x, y, c = lax.axis_index("x"), lax.axis_index("y"), lax.axis_index("c")
        me = 4 * x + 2 * y + c

        def peer(r):
            return ((1 - x) if r & 4 else x, (1 - y) if r & 2 else y, (1 - c) if r & 1 else c)

        def lin(p):
            return 4 * p[0] + 2 * p[1] + p[2]

        def piece(a, idx):
            return src[a].at[idx] if scatter[a] else src[a]

        local, sends = [], []
        for a in range(n):
            cp = pltpu.make_async_copy(piece(a, me), out[a].at[me], local_sems.at[a])
            cp.start()
            local.append(cp)
        for r in range(1, N_DEV):
            p = peer(r)
            for a in range(n):
                cp = pltpu.make_async_remote_copy(src_ref=piece(a, lin(p)), dst_ref=out[a].at[me],
                                                  send_sem=send_sems.at[a, r - 1], recv_sem=recv_sems.at[a, r - 1],
                                                  device_id=p, device_id_type=MESH)
                cp.start()
                sends.append(cp)
        for r in range(1, N_DEV):
            p = peer(r)
            for a in range(n):
                pltpu.make_async_remote_copy(src_ref=piece(a, lin(p)), dst_ref=out[a].at[lin(p)],
                                             send_sem=send_sems.at[a, r - 1], recv_sem=recv_sems.at[a, r - 1],
                                             device_id=p, device_id_type=MESH).wait_recv()
        for cp in sends:
            cp.wait_send()
        for cp in local:
            cp.wait()

    def piece_shape(a):
        return srcs[a].shape[1:] if scatter[a] else srcs[a].shape

    any_spec = pl.BlockSpec(memory_space=pl.ANY)
    return _pcall(body, name=name, in_specs=[any_spec] * n, out_specs=[any_spec] * n,
                  out_shape=[_sds((N_DEV, *piece_shape(a)), srcs[a].dtype) for a in range(n)],
                  scratch_shapes=[pltpu.SemaphoreType.DMA((n, N_DEV - 1)), pltpu.SemaphoreType.DMA((n, N_DEV - 1)),
                                  pltpu.SemaphoreType.DMA((n,))])(*srcs)


_HBM = pl.BlockSpec(memory_space=pltpu.HBM)
_SEM = pl.BlockSpec(memory_space=pltpu.SEMAPHORE)
_EFFECT = pltpu.SideEffectType.DATAFLOW_SIDE_EFFECTING


def _peer_ids():
    x, y, c = lax.axis_index("x"), lax.axis_index("y"), lax.axis_index("c")
    peers = [((1 - x) if r & 4 else x, (1 - y) if r & 2 else y, (1 - c) if r & 1 else c) for r in range(1, N_DEV)]
    return 4 * x + 2 * y + c, peers, [4 * p[0] + 2 * p[1] + p[2] for p in peers]


def _exchange_start(name, src, scatter):
    piece_shape = src.shape[1:] if scatter else src.shape

    def body(src_ref, land_ref, send_sems, recv_sems, local_sem, src_thru, land_thru, token):
        me, peers, lins = _peer_ids()

        def piece(idx):
            return src_ref.at[idx] if scatter else src_ref

        pltpu.make_async_copy(piece(me), land_ref.at[me], local_sem).start()
        for r, (p, lp) in enumerate(zip(peers, lins)):
            pltpu.make_async_remote_copy(src_ref=piece(lp), dst_ref=land_ref.at[me], send_sem=send_sems.at[r],
                                         recv_sem=recv_sems.at[r], device_id=p, device_id_type=MESH).start()
        token[...] = jnp.zeros_like(token)

    land = pltpu.with_memory_space_constraint(lax.empty((N_DEV, *piece_shape), src.dtype), pltpu.HBM)
    send_sems, recv_sems, local_sem, src_thru, land_thru, token = pl.pallas_call(
        body, name=name,
        out_shape=(pltpu.SemaphoreType.DMA((N_DEV - 1,)), pltpu.SemaphoreType.DMA((N_DEV - 1,)),
                   pltpu.SemaphoreType.DMA(()), pltpu.HBM(src.shape, src.dtype),
                   pltpu.HBM((N_DEV, *piece_shape), src.dtype), _sds((8, LANE), f32)),
        in_specs=(_HBM, _HBM), out_specs=(_SEM, _SEM, _SEM, _HBM, _HBM, pl.BlockSpec(memory_space=pltpu.VMEM)),
        input_output_aliases={0: 3, 1: 4},
        compiler_params=pltpu.CompilerParams(has_side_effects=_EFFECT),
    )(pltpu.with_memory_space_constraint(src, pltpu.HBM), land)
    return (send_sems, recv_sems, local_sem, src_thru, land_thru, scatter), token


def _exchange_wait(name, handle, after):
    send_sems, recv_sems, local_sem, src_thru, land_thru, scatter = handle

    def body(src_ref, land_ref, send_sems_, recv_sems_, local_sem_, after_ref, src_dead, got_ref):
        me, peers, lins = _peer_ids()

        def piece(idx):
            return src_ref.at[idx] if scatter else src_ref

        pltpu.make_async_copy(piece(me), land_ref.at[me], local_sem_).wait()
        for r, (p, lp) in enumerate(zip(peers, lins)):
            pltpu.make_async_remote_copy(src_ref=piece(lp), dst_ref=land_ref.at[me], send_sem=send_sems_.at[r],
                                         recv_sem=recv_sems_.at[r], device_id=p, device_id_type=MESH).wait_send()
            pltpu.make_async_remote_copy(src_ref=piece(lp), dst_ref=land_ref.at[lp], send_sem=send_sems_.at[r],
                                         recv_sem=recv_sems_.at[r], device_id=p, device_id_type=MESH).wait_recv()

    return pl.pallas_call(
        body, name=name,
        out_shape=(pltpu.HBM(src_thru.shape, src_thru.dtype), pltpu.HBM(land_thru.shape, land_thru.dtype)),
        in_specs=(_HBM, _HBM, _SEM, _SEM, _SEM, pl.BlockSpec(memory_space=pl.ANY)), out_specs=(_HBM, _HBM),
        input_output_aliases={0: 0, 1: 1},
        compiler_params=pltpu.CompilerParams(has_side_effects=_EFFECT),
    )(src_thru, land_thru, send_sems, recv_sems, local_sem, after)[1]


def _adamw(name, w, m, v, g_or_stack, stacked, rows=256):
    r, c = w.shape
    tr = _tile(r, rows)

    def fn(w_, m_, v_, g_):
        if stacked:
            g = g_[0].astype(f32)
            for j in range(1, N_DEV):
                g = g + g_[j].astype(f32)
        else:
            g = g_
        m_new = ADAM_B1 * m_ + (1.0 - ADAM_B1) * g
        v_new = ADAM_B2 * v_ + (1.0 - ADAM_B2) * (g * g)
        m_hat = m_new / (1.0 - ADAM_B1 ** ADAM_STEP)
        v_hat = v_new / (1.0 - ADAM_B2 ** ADAM_STEP)
        delta = -ADAM_LR * (m_hat / (jnp.sqrt(v_hat) + ADAM_EPS) + ADAM_WD * w_)
        return g, delta, m_new, v_new

    blk = _row(tr, c)
    g_spec = pl.BlockSpec((N_DEV, tr, c), lambda i: (0, i, 0)) if stacked else blk
    return _rowwise(name, fn, r // tr, [w, m, v, g_or_stack], [blk, blk, blk, g_spec],
                    [_sds((r, c), f32)] * 4, [blk] * 4, [False] * 4)


def _ada_fwd(c_all, w_shard, b_shard):
    nb_, d = c_all.shape
    n = w_shard.shape[1]
    tn = _tile(n, 512)

    def body(c_ref, w_ref, b_ref, o_ref):
        a = jax.nn.silu(c_ref[...]).astype(bf16)
        o_ref[...] = jnp.dot(a, w_ref[...].astype(bf16), preferred_element_type=f32) + b_ref[...]

    return _pcall(body, name="ada_fwd", grid=(n // tn,),
                  in_specs=[pl.BlockSpec((nb_, d), lambda j: (0, 0)), pl.BlockSpec((d, tn), lambda j: (0, j)),
                            pl.BlockSpec((1, tn), lambda j: (0, j))],
                  out_specs=pl.BlockSpec((nb_, tn), lambda j: (0, j)), out_shape=_sds((nb_, n), f32),
                  sem=("parallel",))(c_all, w_shard, b_shard)


def _ada_bwd(c_all, dmod_cols):
    nb_, d = c_all.shape
    n = dmod_cols.shape[1]
    tn = _tile(n, 512)

    def body(c_ref, g_ref, o_ref):
        a = jax.nn.silu(c_ref[...]).astype(bf16).astype(f32)
        g = g_ref[...].astype(bf16).astype(f32)
        o_ref[...] = lax.dot_general(a, g, (TN, ((), ())), precision=HI, preferred_element_type=f32)

    return _pcall(body, name="ada_bwd", grid=(n // tn,),
                  in_specs=[pl.BlockSpec((nb_, d), lambda j: (0, 0)), pl.BlockSpec((nb_, tn), lambda j: (0, j))],
                  out_specs=pl.BlockSpec((d, tn), lambda j: (0, j)), out_shape=_sds((d, n), f32),
                  sem=("parallel",))(c_all, dmod_cols)


SMALL = ("b_ada", "norm1_g", "q_norm_g", "k_norm_g", "lam_re", "lam_im", "log_step", "b_re", "b_im", "c_re", "c_im",
         "d_skip", "b_glu", "attn_out_g", "ssm_out_g", "norm2_g")
BIG = ("w_ada", "w_in", "w_glu", "w_out", "w_ff1", "w_ff2")
ORDER = ("w_ada", "b_ada", "norm1_g", "w_in", "q_norm_g", "k_norm_g", "lam_re", "lam_im", "log_step", "b_re", "b_im",
         "c_re", "c_im", "d_skip", "w_glu", "b_glu", "attn_out_g", "ssm_out_g", "w_out", "norm2_g", "w_ff1", "w_ff2")


def _pack(parts):
    flat = jnp.concatenate([p.reshape(-1) for p in parts])
    pad = (-flat.shape[0]) % (8 * LANE)
    return jnp.pad(flat, (0, pad)).reshape(-1, LANE)


def kernel(x, c, w_ada, b_ada, norm1_g, w_in, q_norm_g, k_norm_g, lam_re, lam_im, log_step, b_re, b_im, c_re, c_im, d_skip, w_glu, b_glu, attn_out_g, ssm_out_g, w_out, norm2_g, w_ff1, w_ff2, loss_target, m_w_ada, m_b_ada, m_norm1_g, m_w_in, m_q_norm_g, m_k_norm_g, m_lam_re, m_lam_im, m_log_step, m_b_re, m_b_im, m_c_re, m_c_im, m_d_skip, m_w_glu, m_b_glu, m_attn_out_g, m_ssm_out_g, m_w_out, m_norm2_g, m_w_ff1, m_w_ff2, v_w_ada, v_b_ada, v_norm1_g, v_w_in, v_q_norm_g, v_k_norm_g, v_lam_re, v_lam_im, v_log_step, v_b_re, v_b_im, v_c_re, v_c_im, v_d_skip, v_w_glu, v_b_glu, v_attn_out_g, v_ssm_out_g, v_w_out, v_norm2_g, v_w_ff1, v_w_ff2):
    env = dict(locals())
    wts = {n: env[n] for n in ORDER}
    mom = {n: env["m_" + n] for n in ORDER}
    var = {n: env["v_" + n] for n in ORDER}

    xs, tgt = x[0], loss_target[0]
    s_len, d = xs.shape
    aw = d // 2
    sw = d - aw
    n_heads = aw // HEAD
    n_groups = sw // SSM_GROUP
    n_state = lam_re.shape[-1]
    ngs = sw // LANE
    gp = n_groups * n_state
    sb = GROUPS_PER_BLOCK * n_state
    tm = _tile(s_len, 256)
    steps = s_len // tm
    me = 4 * lax.axis_index("x") + 2 * lax.axis_index("y") + lax.axis_index("c")

    gather, started = {}, jnp.zeros((1, 1), f32)
    for name in ("w_in", "w_glu", "w_out", "w_ff1", "w_ff2"):
        gather[name], token = _exchange_start("gather_" + name, wts[name][0].astype(bf16), False)
        started = started + token[0:1, 0:1]
    (c_all,) = _exchange("gather_c", [c + started], [False])
    c_all = c_all.reshape(N_DEV, d)

    n_ada = w_ada.shape[-1]
    b_ada_cols = lax.dynamic_slice_in_dim(b_ada, me * n_ada, n_ada, axis=1)
    mod_part = _ada_fwd(c_all, w_ada[0], b_ada_cols)
    (mod_all,) = _exchange("gather_mod", [mod_part], [False])
    mod = lax.dynamic_index_in_dim(mod_all, me, axis=1, keepdims=False).reshape(1, 6 * d)
    sh1, sc1, g1, sh2, sc2, g2 = (mod[:, i * d:(i + 1) * d] for i in range(6))

    (h,) = _rowwise("norm1", _norm_mod, steps, [xs, norm1_g, sc1, sh1],
                    [_row(tm, d), _vec(d), _vec(d), _vec(d)], [_sds((s_len, d), bf16)], [_row(tm, d)], [False])
    win_g = _exchange_wait("gathered_w_in", gather["w_in"], h)
    (proj,) = _mm_nn_sharded("in_proj", h, win_g)

    def qk_fn(q, k, gq, gk):
        return _head_rms(q, gq), _head_rms(k, gk)

    qn, kn = _rowwise("qk_norm", qk_fn, steps, [proj, proj, q_norm_g, k_norm_g],
                      [_row(tm, aw, 0), _row(tm, aw, 1), _vec(HEAD), _vec(HEAD)],
                      [_sds((s_len, aw), f32)] * 2, [_row(tm, aw)] * 2, [False] * 2)
    v_blk = 2 * aw // HEAD

    slopes = _slope_table(n_heads)
    pat = [_attn_fwd(qn, kn, proj, v_blk, dil, slopes) for _, dil in DILATIONS]

    def attn_mix_fn(o1, l1, o2, l2, o3, l3):
        m = jnp.maximum(jnp.maximum(l1, l2), l3)
        e1, e2, e3 = jnp.exp(l1 - m), jnp.exp(l2 - m), jnp.exp(l3 - m)
        tot = e1 + e2 + e3
        return (e1 * o1 + e2 * o2 + e3 * o3) / tot, m + jnp.log(tot)

    attn, lse = _rowwise("attn_mix", attn_mix_fn, steps, [t for ol in pat for t in ol], [_row(tm, aw)] * 6,
                         [_sds((s_len, aw), f32)] * 2, [_row(tm, aw)] * 2, [False] * 2)

    lam_re2, lam_im2 = lam_re[0], lam_im[0]
    log_step2 = log_step[0].reshape(n_groups, 1)
    b_re2 = b_re[0].reshape(n_groups, n_state * SSM_GROUP)
    b_im2 = b_im[0].reshape(n_groups, n_state * SSM_GROUP)
    expand = jnp.repeat(jnp.eye(n_state, dtype=f32), SSM_GROUP, axis=1)
    a_re, a_im, bb_re2, bb_im2 = _ssm_params(lam_re2, lam_im2, log_step2, b_re2, b_im2, expand)
    a2 = jnp.zeros((8, gp), f32).at[0].set(a_re.reshape(gp)).at[1].set(a_im.reshape(gp))

    def by_block(t):
        return t.reshape(ngs, GROUPS_PER_BLOCK, *t.shape[1:])

    bb_re4 = by_block(bb_re2.reshape(n_groups, n_state, SSM_GROUP))
    bb_im4 = by_block(bb_im2.reshape(n_groups, n_state, SSM_GROUP))
    c_re4, c_im4 = by_block(c_re[0]), by_block(c_im[0])
    w_bu_re = _bd_weight(bb_re4.transpose(0, 1, 3, 2))
    w_bu_im = _bd_weight(bb_im4.transpose(0, 1, 3, 2))
    w_y_re = _bd_weight(c_re4.transpose(0, 1, 3, 2))
    w_y_im = _bd_weight(-c_im4.transpose(0, 1, 3, 2))
    w_g_re, w_g_im = _bd_weight(c_re4), _bd_weight(-c_im4)
    w_du_re, w_du_im = _bd_weight(bb_re4), _bd_weight(bb_im4)

    nseg = _scan_segments(s_len)
    u_seg = _to_segments(proj[:, 3 * aw:], nseg).astype(bf16)
    bu_re = _bd_nn("ssm_bu_re", [u_seg], [w_bu_re], LANE, sb)
    bu_im = _bd_nn("ssm_bu_im", [u_seg], [w_bu_im], LANE, sb)
    h_re, h_im, hin_f = _scan("ssm_scan", bu_re, bu_im, a2, reverse=False)
    ymm = _from_segments(_bd_nn("ssm_y", [h_re, h_im], [w_y_re, w_y_im], sb, LANE), nseg)

    u_spec = _row(tm, sw, 3 * aw // sw)
    (yg,) = _rowwise("ssm_gelu", _ypre_fn, steps, [ymm, proj, d_skip], [_row(tm, sw), u_spec, _vec(sw)],
                     [_sds((s_len, sw), f32)], [_row(tm, sw)], [False])
    wglu_g = _exchange_wait("gathered_w_glu", gather["w_glu"], yg).reshape(sw, sw)
    (z,) = _mm_nn("glu_proj", yg, wglu_g)
    (cat,) = _rowwise("mix_norm", _mix_fn, steps, [attn, yg, z, b_glu, attn_out_g, ssm_out_g],
                      [_row(tm, aw), _row(tm, sw), _row(tm, sw), _vec(sw), _vec(aw), _vec(sw)],
                      [_sds((s_len, d), bf16)], [_row(tm, d)], [False])
    wout_g = _exchange_wait("gathered_w_out", gather["w_out"], cat).reshape(d, d)
    (mixed,) = _mm_nn("out_proj", cat, wout_g)

    def res_norm2_fn(x_, mixed_, g1_, gn, sc, sh):
        x1_ = x_ + g1_ * mixed_
        return x1_, _norm_mod(x1_, gn, sc, sh)

    x1, h2 = _rowwise("norm2", res_norm2_fn, steps, [xs, mixed, g1, norm2_g, sc2, sh2],
                      [_row(tm, d), _row(tm, d)] + [_vec(d)] * 4,
                      [_sds((s_len, d), f32), _sds((s_len, d), bf16)], [_row(tm, d)] * 2, [False] * 2)

    def act_epilogue(acc):
        r = jnp.maximum(acc, 0.0)
        return acc, r * r

    wff1_g = _exchange_wait("gathered_w_ff1", gather["w_ff1"], h2)
    a_ff, act = _mm_nn_sharded("ff1", h2, wff1_g, epilogue=act_epilogue,
                               outs=[_sds((s_len, 4 * d), f32), _sds((s_len, 4 * d), bf16)])
    wff2_g = _exchange_wait("gathered_w_ff2", gather["w_ff2"], act).reshape(4 * d, d)
    (ff,) = _mm_nn("ff2", act, wff2_g, tn=2048, tk=512)

    def loss_fn(x1_, ff_, tgt_, g2_):
        e = x1_ + g2_ * ff_ - tgt_
        dy_ = e * (1.0 / d)
        part = jnp.full((1, LANE), 0.5 / d, f32) * jnp.sum(e * e)
        return dy_, g2_ * dy_, part, jnp.sum(dy_ * ff_, axis=0, keepdims=True)

    dy, dff, loss_part, d_g2 = _rowwise(
        "loss", loss_fn, steps, [x1, ff, tgt, g2], [_row(tm, d)] * 3 + [_vec(d)],
        [_sds((s_len, d), f32), _sds((s_len, d), bf16), _sds((1, LANE), f32), _sds((1, d), f32)],
        [_row(tm, d), _row(tm, d), _vec(LANE), _vec(d)], [False, False, True, True])
    loss = lax.psum(loss_part[0, 0], ("x", "y", "c"))

    def dact_epilogue(acc, a_):
        return (acc * (2.0 * jnp.maximum(a_, 0.0)),)

    (da,) = _mm_nt("ff2_dx", dff, wff2_g, epilogue=dact_epilogue, extra=[a_ff], outs=[_sds((s_len, 4 * d), bf16)])
    scatter = {}
    g_wff2 = _mm_tn("ff2_dw", act, dff).reshape(N_DEV, 4 * d // N_DEV, d)
    scatter["w_ff2"], tok_ff2 = _exchange_start("scatter_w_ff2", g_wff2, True)
    dh2 = _mm_nt_sharded("ff1_dx", da, wff1_g)
    g_wff1 = _mm_tn_sharded("ff1_dw", h2, da, N_DEV)
    scatter["w_ff1"], tok_ff1 = _exchange_start("scatter_w_ff1", g_wff1, True)
    norm2_g_t = norm2_g + (tok_ff2[0:1, 0:1] + tok_ff1[0:1, 0:1])

    def norm2_bwd_fn(dh2_, x1_, dy_, mixed_, gn, sc, sh, g1_):
        _, vjp = jax.vjp(_norm_mod, x1_, gn, sc, sh)
        dx, dgn, dsc, dsh = vjp(dh2_)
        dx1_ = dy_ + dx
        return dx1_, g1_ * dx1_, dgn, dsc, dsh, jnp.sum(dx1_ * mixed_, axis=0, keepdims=True)

    dx1, dmixed, d_norm2_g, d_sc2, d_sh2, d_g1 = _rowwise(
        "norm2_bwd", norm2_bwd_fn, steps, [dh2, x1, dy, mixed, norm2_g_t, sc2, sh2, g1],
        [_row(tm, d)] * 4 + [_vec(d)] * 4,
        [_sds((s_len, d), f32), _sds((s_len, d), bf16)] + [_sds((1, d), f32)] * 4,
        [_row(tm, d)] * 2 + [_vec(d)] * 4, [False, False, True, True, True, True])

    (dcat,) = _mm_nt("out_dx", dmixed, wout_g)
    g_wout = _mm_tn("out_dw", cat, dmixed).reshape(N_DEV, d // N_DEV, d)
    scatter["w_out"], tok_out = _exchange_start("scatter_w_out", g_wout, True)
    b_glu_t = b_glu + tok_out[0:1, 0:1]

    def mix_bwd_fn(dcat_, attn_, yg_, z_, bglu, ga, gs):
        _, vjp = jax.vjp(_mix_fn, attn_, yg_, z_, bglu, ga, gs)
        dattn_, dyg_, dz_, dbglu, dga, dgs = vjp(dcat_)
        prod = dattn_ * attn_
        dd_ = jnp.concatenate([jnp.broadcast_to(jnp.sum(prod[:, i * HEAD:(i + 1) * HEAD], axis=1, keepdims=True),
                                                (prod.shape[0], HEAD)) for i in range(n_heads)], axis=1)
        return dattn_, dd_, dyg_, dz_, dbglu, dga, dgs

    dattn, dd, dyg1, dz, d_b_glu, d_attn_out_g, d_ssm_out_g = _rowwise(
        "mix_bwd", mix_bwd_fn, steps, [dcat, attn, yg, z, b_glu_t, attn_out_g, ssm_out_g],
        [_row(tm, d), _row(tm, aw), _row(tm, sw), _row(tm, sw), _vec(sw), _vec(aw), _vec(sw)],
        [_sds((s_len, aw), f32), _sds((s_len, aw), f32), _sds((s_len, sw), f32), _sds((s_len, sw), bf16),
         _sds((1, sw), f32), _sds((1, aw), f32), _sds((1, sw), f32)],
        [_row(tm, aw), _row(tm, aw), _row(tm, sw), _row(tm, sw), _vec(sw), _vec(aw), _vec(sw)],
        [False] * 4 + [True] * 3)

    (dyg2,) = _mm_nt("glu_dx", dz, wglu_g)
    g_wglu = _mm_tn("glu_dw", yg, dz).reshape(N_DEV, sw // N_DEV, sw)
    scatter["w_glu"], tok_glu = _exchange_start("scatter_w_glu", g_wglu, True)
    d_skip_t = d_skip + tok_glu[0:1, 0:1]

    def gelu_bwd_fn(dyg1_, dyg2_, ymm_, u_, dskip):
        _, vjp = jax.vjp(_ypre_fn, ymm_, u_, dskip)
        dymm, du_, ddskip = vjp(dyg1_ + dyg2_)
        return dymm, du_, ddskip

    dymm, du_skip, d_d_skip = _rowwise(
        "ssm_gelu_bwd", gelu_bwd_fn, steps, [dyg1, dyg2, ymm, proj, d_skip_t],
        [_row(tm, sw)] * 3 + [u_spec, _vec(sw)],
        [_sds((s_len, sw), f32), _sds((s_len, sw), f32), _sds((1, sw), f32)],
        [_row(tm, sw), _row(tm, sw), _vec(sw)], [False, False, True])

    dymm_seg = _to_segments(dymm, nseg).astype(bf16)
    gr = _bd_nn("ssm_dh_re", [dymm_seg], [w_g_re], LANE, sb)
    gi = _bd_nn("ssm_dh_im", [dymm_seg], [w_g_im], LANE, sb)
    lr, li, _, da_seg = _scan("ssm_adj", gr, gi, a2, reverse=True, da_from=(h_re, h_im, hin_f))
    du_ssm = _from_segments(_bd_nn("ssm_du", [lr, li], [w_du_re, w_du_im], sb, LANE), nseg)
    dc_re_c, dc_im_c = _bd_tn("ssm_dc", dymm_seg, [h_re, h_im], LANE, sb)
    dbb_re_c, dbb_im_c = _bd_tn("ssm_dbbar", u_seg, [lr, li], LANE, sb)

    def diag_to_gpi(w):
        return _bd_diag(w, SSM_GROUP, n_state).transpose(0, 1, 3, 2).reshape(n_groups, n_state * SSM_GROUP)

    d_lam_re, d_lam_im, d_log_step, d_b_re2, d_b_im2 = _ssm_params_bwd(
        lam_re2, lam_im2, log_step2, b_re2, b_im2, expand,
        da_seg[0, 0].reshape(n_groups, n_state), da_seg[1, 0].reshape(n_groups, n_state),
        diag_to_gpi(dbb_re_c), diag_to_gpi(dbb_im_c))
    d_c_re = _bd_diag(dc_re_c, SSM_GROUP, n_state).reshape(n_groups, SSM_GROUP, n_state)
    d_c_im = -_bd_diag(dc_im_c, SSM_GROUP, n_state).reshape(n_groups, SSM_GROUP, n_state)

    grads_qkv = [_attn_bwd(qn, kn, proj, v_blk, dattn, lse, dd, dil, slopes) for _, dil in DILATIONS]

    def qkv_bwd_fn(q, k, gq, gk, dq1, dq2, dq3, dk1, dk2, dk3, dv1, dv2, dv3, du1, du2):
        _, vjp = jax.vjp(lambda q_, k_, gq_, gk_: (_head_rms(q_, gq_), _head_rms(k_, gk_)), q, k, gq, gk)
        dq, dk, dgq, dgk = vjp((dq1 + dq2 + dq3, dk1 + dk2 + dk3))
        return jnp.concatenate([dq, dk, dv1 + dv2 + dv3, du1 + du2], axis=1), dgq, dgk

    qkv_cots = [grads_qkv[p][i] for i in range(3) for p in range(3)]
    dproj, d_q_norm_g, d_k_norm_g = _rowwise(
        "qk_norm_bwd", qkv_bwd_fn, steps, [proj, proj, q_norm_g, k_norm_g, *qkv_cots, du_skip, du_ssm],
        [_row(tm, aw, 0), _row(tm, aw, 1), _vec(HEAD), _vec(HEAD)] + [_row(tm, aw)] * 9 + [_row(tm, sw)] * 2,
        [_sds((s_len, 3 * aw + sw), bf16), _sds((1, HEAD), f32), _sds((1, HEAD), f32)],
        [_row(tm, 3 * aw + sw), _vec(HEAD), _vec(HEAD)], [False, True, True])

    g_win = _mm_tn_sharded("in_dw", h, dproj, N_DEV)
    scatter["w_in"], tok_in = _exchange_start("scatter_w_in", g_win, True)
    dh = _mm_nt_sharded("in_dx", dproj, win_g)
    norm1_g_t = norm1_g + tok_in[0:1, 0:1]

    def norm1_bwd_fn(dh_, x_, dx1_, gn, sc, sh):
        _, vjp = jax.vjp(_norm_mod, x_, gn, sc, sh)
        dx, dgn, dsc, dsh = vjp(dh_)
        return dx1_ + dx, dgn, dsc, dsh

    grad_x, d_norm1_g, d_sc1, d_sh1 = _rowwise(
        "norm1_bwd", norm1_bwd_fn, steps, [dh, xs, dx1, norm1_g_t, sc1, sh1], [_row(tm, d)] * 3 + [_vec(d)] * 3,
        [_sds((s_len, d), f32)] + [_sds((1, d), f32)] * 3, [_row(tm, d)] + [_vec(d)] * 3,
        [False, True, True, True])

    dmod = jnp.concatenate([d_sh1, d_sc1, d_g1, d_sh2, d_sc2, d_g2], axis=1)
    small_g = {"b_ada": dmod, "norm1_g": d_norm1_g, "q_norm_g": d_q_norm_g, "k_norm_g": d_k_norm_g,
               "lam_re": d_lam_re, "lam_im": d_lam_im, "log_step": d_log_step, "b_re": d_b_re2, "b_im": d_b_im2,
               "c_re": d_c_re, "c_im": d_c_im, "d_skip": d_d_skip, "b_glu": d_b_glu,
               "attn_out_g": d_attn_out_g, "ssm_out_g": d_ssm_out_g, "norm2_g": d_norm2_g}
    small_part = _pack([small_g[n] for n in SMALL])
    (r_small,) = _exchange("gather_small_grads", [small_part], [False])

    res = {}
    dmod_all = r_small.reshape(N_DEV, -1)[:, :6 * d]
    g_wada = _ada_bwd(c_all, lax.dynamic_slice_in_dim(dmod_all, me * n_ada, n_ada, axis=1))
    res["w_ada"] = _adamw("adamw_w_ada", w_ada[0], m_w_ada[0], v_w_ada[0], g_wada, False)
    after = res["w_ada"][1]
    for name in ("w_ff2", "w_ff1", "w_out", "w_glu", "w_in"):
        stack = _exchange_wait("scattered_" + name, scatter[name], after)
        res[name] = _adamw("adamw_" + name, wts[name][0], mom[name][0], var[name][0], stack, True)
        after = res[name][1]
    small_res = _adamw("adamw_small", _pack([wts[n] for n in SMALL]), _pack([mom[n] for n in SMALL]),
                       _pack([var[n] for n in SMALL]), r_small, True, rows=4096)
    off = 0
    for n in SMALL:
        size = wts[n].size
        res[n] = [t.reshape(-1)[off:off + size] for t in small_res]
        off += size

    out = [loss, grad_x[None]]
    for i in range(4):
        out += [res[n][i].reshape(wts[n].shape) for n in ORDER]
    return tuple(out)
```

```python
import math

import jax
import jax.numpy as jnp
from jax import lax
from jax.experimental import pallas as pl
from jax.experimental.pallas import tpu as pltpu

f32, bf16 = jnp.float32, jnp.bfloat16

N_DEV = 8
LANE = 128
HEAD = 128
SSM_GROUP = 16
GROUPS_PER_BLOCK = LANE // SSM_GROUP
DILATIONS = ((128, 1), (512, 4), (2048, 16))
BAND = 128
EPS = 1e-6
ADAM_LR, ADAM_B1, ADAM_B2, ADAM_EPS, ADAM_WD, ADAM_STEP = 0.001, 0.9, 0.999, 1e-08, 0.01, 10
NEG = -1e30
VMEM_LIMIT = 60 * 1024 * 1024
HI = lax.Precision.HIGHEST
MESH = pl.DeviceIdType.MESH


def _pcall(body, **kw):
    sem = kw.pop("sem", None)
    kw["compiler_params"] = pltpu.CompilerParams(dimension_semantics=sem, vmem_limit_bytes=VMEM_LIMIT)
    return pl.pallas_call(body, **kw)


def _tile(n, pref):
    t = min(n, pref)
    while n % t:
        t //= 2
    return t


def _sds(shape, dtype):
    return jax.ShapeDtypeStruct(shape, dtype)


def _rowwise(name, fn, steps, ins, in_specs, outs, out_specs, acc):
    n_in = len(ins)

    def body(*refs):
        res = fn(*[r[...] for r in refs[:n_in]])
        res = res if isinstance(res, (tuple, list)) else (res,)
        for r, o, a in zip(refs[n_in:], res, acc):
            if a:
                @pl.when(pl.program_id(0) == 0)
                def _():
                    r[...] = jnp.zeros_like(r)
                r[...] += o
            else:
                r[...] = o.astype(r.dtype)

    return _pcall(body, name=name, grid=(steps,), in_specs=in_specs, out_specs=out_specs, out_shape=outs,
                  sem=("arbitrary",))(*ins)


def _row(tm, c, blk=0):
    return pl.BlockSpec((tm, c), lambda i: (i, blk))


def _vec(c, blk=0):
    return pl.BlockSpec((1, c), lambda i: (0, blk))


def _rms(x, g):
    return x * lax.rsqrt(jnp.mean(x * x, axis=-1, keepdims=True) + EPS) * g


def _norm_mod(x, g, sc, sh):
    return _rms(x, g) * (1.0 + sc) + sh


def _head_rms(t, g):
    return jnp.concatenate([_rms(t[:, h * HEAD:(h + 1) * HEAD], g) for h in range(t.shape[1] // HEAD)], axis=1)


def _mix_fn(attn, yg, z, bglu, ga, gs):
    ssm = yg * jax.nn.sigmoid(z + bglu)
    return jnp.concatenate([_rms(attn, ga), _rms(ssm, gs)], axis=1)


def _ypre_fn(ymm, u, dskip):
    return jax.nn.gelu(ymm + dskip * u)


def _matmul(name, a, b, *, dims, grid, a_spec, b_spec, acc_shape, outs, out_specs, extra=(), extra_specs=(),
            epilogue=None):
    gk = grid[2]
    n_x = len(extra)

    def body(a_ref, b_ref, *rest):
        acc = rest[-1]
        k = pl.program_id(2)

        @pl.when(k == 0)
        def _():
            acc[...] = jnp.zeros_like(acc)

        acc[...] += lax.dot_general(a_ref[...].astype(bf16), b_ref[...].astype(bf16), (dims, ((), ())),
                                    preferred_element_type=f32)

        @pl.when(k == gk - 1)
        def _():
            xs = [r[...] for r in rest[:n_x]]
            res = epilogue(acc[...], *xs) if epilogue is not None else (acc[...],)
            for r, o in zip(rest[n_x:-1], res):
                r[...] = o.astype(r.dtype)

    return _pcall(body, name=name, grid=grid, in_specs=[a_spec, b_spec, *extra_specs], out_specs=out_specs,
                  out_shape=outs, scratch_shapes=[pltpu.VMEM(acc_shape, f32)],
                  sem=("parallel", "parallel", "arbitrary"))(a, b, *extra)


NN = ((1,), (0,))
NT = ((1,), (1,))
TN = ((0,), (0,))


def _mm_nn(name, a, b, out_dtype=f32, tm=1024, tn=1024, tk=2048, epilogue=None, extra=(), outs=None):
    m, kd = a.shape
    n = b.shape[1]
    tm, tn, tk = _tile(m, tm), _tile(n, tn), _tile(kd, tk)
    o_spec = pl.BlockSpec((tm, tn), lambda i, j, k: (i, j))
    outs = outs if outs is not None else [_sds((m, n), out_dtype)]
    return _matmul(name, a, b, dims=NN, grid=(m // tm, n // tn, kd // tk),
                   a_spec=pl.BlockSpec((tm, tk), lambda i, j, k: (i, k)),
                   b_spec=pl.BlockSpec((tk, tn), lambda i, j, k: (k, j)),
                   acc_shape=(tm, tn), outs=outs, out_specs=[o_spec] * len(outs),
                   extra=extra, extra_specs=[o_spec] * len(extra), epilogue=epilogue)


def _mm_nn_sharded(name, a, b3, out_dtype=f32, tm=1024, tk=2048, epilogue=None, outs=None):
    m, kd = a.shape
    nsh, _, n = b3.shape
    tm, tk = _tile(m, tm), _tile(kd, tk)
    o_spec = pl.BlockSpec((tm, n), lambda i, j, k: (i, j))
    outs = outs if outs is not None else [_sds((m, nsh * n), out_dtype)]
    return _matmul(name, a, b3, dims=NN, grid=(m // tm, nsh, kd // tk),
                   a_spec=pl.BlockSpec((tm, tk), lambda i, j, k: (i, k)),
                   b_spec=pl.BlockSpec((None, tk, n), lambda i, j, k: (j, k, 0)),
                   acc_shape=(tm, n), outs=outs, out_specs=[o_spec] * len(outs), epilogue=epilogue)


def _mm_nt(name, a, b, out_dtype=f32, tm=1024, tn=1024, tk=1024, epilogue=None, extra=(), outs=None):
    m, kd = a.shape
    n = b.shape[0]
    tm, tn, tk = _tile(m, tm), _tile(n, tn), _tile(kd, tk)
    o_spec = pl.BlockSpec((tm, tn), lambda i, j, k: (i, j))
    outs = outs if outs is not None else [_sds((m, n), out_dtype)]
    return _matmul(name, a, b, dims=NT, grid=(m // tm, n // tn, kd // tk),
                   a_spec=pl.BlockSpec((tm, tk), lambda i, j, k: (i, k)),
                   b_spec=pl.BlockSpec((tn, tk), lambda i, j, k: (j, k)),
                   acc_shape=(tm, tn), outs=outs, out_specs=[o_spec] * len(outs),
                   extra=extra, extra_specs=[o_spec] * len(extra), epilogue=epilogue)


def _mm_nt_sharded(name, a, b3, out_dtype=f32, tm=512, tn=2048):
    m = a.shape[0]
    nsh, n_out, n = b3.shape
    tm, tn = _tile(m, tm), _tile(n_out, tn)
    return _matmul(name, a, b3, dims=NT, grid=(m // tm, n_out // tn, nsh),
                   a_spec=pl.BlockSpec((tm, n), lambda i, j, k: (i, k)),
                   b_spec=pl.BlockSpec((None, tn, n), lambda i, j, k: (k, j, 0)),
                   acc_shape=(tm, tn), outs=[_sds((m, n_out), out_dtype)],
                   out_specs=[pl.BlockSpec((tm, tn), lambda i, j, k: (i, j))])[0]


def _mm_tn(name, a, b, out_dtype=bf16, tm=1024, tn=1024, tk=1024):
    t, m = a.shape
    n = b.shape[1]
    tm, tn, tk = _tile(m, tm), _tile(n, tn), _tile(t, tk)
    return _matmul(name, a, b, dims=TN, grid=(m // tm, n // tn, t // tk),
                   a_spec=pl.BlockSpec((tk, tm), lambda i, j, k: (k, i)),
                   b_spec=pl.BlockSpec((tk, tn), lambda i, j, k: (k, j)),
                   acc_shape=(tm, tn), outs=[_sds((m, n), out_dtype)],
                   out_specs=[pl.BlockSpec((tm, tn), lambda i, j, k: (i, j))])[0]


def _mm_tn_sharded(name, a, b, nsh, out_dtype=bf16, tm=1024, tk=1024):
    t, m = a.shape
    n = b.shape[1] // nsh
    tm, tk = _tile(m, tm), _tile(t, tk)
    return _matmul(name, a, b, dims=TN, grid=(m // tm, nsh, t // tk),
                   a_spec=pl.BlockSpec((tk, tm), lambda i, j, k: (k, i)),
                   b_spec=pl.BlockSpec((tk, n), lambda i, j, k: (k, j)),
                   acc_shape=(tm, n), outs=[_sds((nsh, m, n), out_dtype)],
                   out_specs=[pl.BlockSpec((None, tm, n), lambda i, j, k: (j, i, 0))])[0]


def _bd_nn(name, a_list, w_list, ka, nb, out_dtype=f32, tm=1024):
    n_q = len(a_list)
    m = a_list[0].shape[0]
    ngs = w_list[0].shape[0]
    tm = _tile(m, tm)

    def body(*refs):
        o_ref = refs[-1]
        tot = None
        for q in range(n_q):
            p = jnp.dot(refs[q][...].astype(bf16), refs[n_q + q][...], preferred_element_type=f32)
            tot = p if tot is None else tot + p
        o_ref[...] = tot.astype(o_ref.dtype)

    return _pcall(body, name=name, grid=(m // tm, ngs),
                  in_specs=[pl.BlockSpec((tm, ka), lambda i, s: (i, s))] * n_q
                  + [pl.BlockSpec((None, ka, nb), lambda i, s: (s, 0, 0))] * n_q,
                  out_specs=pl.BlockSpec((tm, nb), lambda i, s: (i, s)),
                  out_shape=_sds((m, ngs * nb), out_dtype), sem=("parallel", "parallel"))(*a_list, *w_list)


def _bd_tn(name, a, b_list, ra, cb, tk=1024):
    n_q = len(b_list)
    t = a.shape[0]
    ngs = a.shape[1] // ra
    tk = _tile(t, tk)

    def body(*refs):
        a_t = refs[0][...].astype(bf16)
        for q in range(n_q):
            o_ref = refs[1 + n_q + q]

            @pl.when(pl.program_id(1) == 0)
            def _():
                o_ref[...] = jnp.zeros_like(o_ref)

            o_ref[...] += lax.dot_general(a_t, refs[1 + q][...].astype(bf16), (TN, ((), ())),
                                          preferred_element_type=f32)

    return _pcall(body, name=name, grid=(ngs, t // tk),
                  in_specs=[pl.BlockSpec((tk, ra), lambda s, k: (k, s))]
                  + [pl.BlockSpec((tk, cb), lambda s, k: (k, s))] * n_q,
                  out_specs=[pl.BlockSpec((None, ra, cb), lambda s, k: (s, 0, 0))] * n_q,
                  out_shape=[_sds((ngs, ra, cb), f32)] * n_q, sem=("parallel", "arbitrary"))(a, *b_list)


def _bd_weight(t4):
    ngs, gb, r, c = t4.shape
    eye = jnp.eye(gb, dtype=t4.dtype)
    return jnp.einsum("sgrc,gh->sgrhc", t4, eye).reshape(ngs, gb * r, gb * c).astype(bf16)


def _bd_diag(w, r, c):
    ngs = w.shape[0]
    gb = w.shape[1] // r
    w5 = w.reshape(ngs, gb, r, gb, c)
    return jnp.einsum("sgrhc,gh->sgrc", w5, jnp.eye(gb, dtype=w.dtype))


SCAN_CHAINS = 8


def _scan_segments(s_len):
    nch = SCAN_CHAINS
    while s_len % (8 * nch) or (s_len // (8 * nch)) & (s_len // (8 * nch) - 1):
        nch //= 2
    return 8 * nch


def _to_segments(t, nseg):
    s_len, c = t.shape
    return t.reshape(nseg, s_len // nseg, c).transpose(1, 0, 2).reshape(s_len, c)


def _from_segments(t, nseg):
    s_len, c = t.shape
    return t.reshape(s_len // nseg, nseg, c).transpose(1, 0, 2).reshape(s_len, c)


def _lane_block_weights(t3, n_state):
    n_groups = t3.shape[0]
    gpl = LANE // n_state
    per = LANE // (gpl * SSM_GROUP)
    n_lb = n_groups // gpl
    t5 = t3.reshape(n_lb // per, per, gpl, SSM_GROUP, n_state)
    w = jnp.einsum("aqgic,gh,qs->aqsgihc", t5, jnp.eye(gpl, dtype=t3.dtype), jnp.eye(per, dtype=t3.dtype))
    return w.reshape(n_lb, LANE, LANE).astype(bf16)


def _scan(name, src, w_re, w_im, a2, *, reverse, da_from=None):
    s_len = src.shape[0]
    gp = a2.shape[1]
    per = (gp // LANE) // (src.shape[1] // LANE)
    nseg = _scan_segments(s_len)
    nch = nseg // 8
    seg = s_len // nseg
    n_sq = int(math.log2(seg))
    assert 2 ** n_sq == seg
    with_da = da_from is not None
    chunk = _tile(s_len, 1024)

    def body(*refs):
        it = iter(refs)
        src_ref, wr_ref, wi_ref, a_ref = next(it), next(it), next(it), next(it)
        if with_da:
            hr_ref, hi_ref, hin_ref = next(it), next(it), next(it)
        or_ref, oi_ref, oin_ref = next(it), next(it), next(it)
        if with_da:
            da_ref = next(it)
        xr_ref, xi_ref = next(it), next(it)

        for i in range(s_len // chunk):
            part = src_ref[i * chunk:(i + 1) * chunk, :]
            xr_ref[i * chunk:(i + 1) * chunk, :] = jnp.dot(part, wr_ref[...], preferred_element_type=f32)
            xi_ref[i * chunk:(i + 1) * chunk, :] = jnp.dot(part, wi_ref[...], preferred_element_type=f32)

        ar = a_ref[0:1, :]
        ai = -a_ref[1:2, :] if reverse else a_ref[1:2, :]
        arb, aib = jnp.broadcast_to(ar, (8, LANE)), jnp.broadcast_to(ai, (8, LANE))

        def rows(ch, k):
            return pl.ds(pl.multiple_of(k * nseg + ch * 8, 8), 8)

        def advance(h, ch, k):
            hr, hi = h
            return (arb * hr - aib * hi + xr_ref[rows(ch, k), :], arb * hi + aib * hr + xi_ref[rows(ch, k), :])

        def kk(n):
            return seg - 1 - n if reverse else n

        zero = jnp.zeros((8, LANE), f32)

        def sweep1(n, hs):
            return tuple(advance(hs[ch], ch, kk(n)) for ch in range(nch))

        ends = lax.fori_loop(0, seg, sweep1, tuple((zero, zero) for _ in range(nch)))

        pr, pi = ar, ai
        for _ in range(n_sq):
            pr, pi = pr * pr - pi * pi, 2.0 * pr * pi
        in_r, in_i = [None] * nseg, [None] * nseg
        cr = ci = jnp.zeros((1, LANE), f32)
        for j in (range(nseg - 1, -1, -1) if reverse else range(nseg)):
            in_r[j], in_i[j] = cr, ci
            er, ei = ends[j // 8][0][j % 8:j % 8 + 1, :], ends[j // 8][1][j % 8:j % 8 + 1, :]
            cr, ci = er + pr * cr - pi * ci, ei + pr * ci + pi * cr
        h0 = tuple((jnp.concatenate(in_r[8 * ch:8 * ch + 8], axis=0), jnp.concatenate(in_i[8 * ch:8 * ch + 8], axis=0))
                   for ch in range(nch))
        for ch in range(nch):
            oin_ref[0, 8 * ch:8 * ch + 8, :] = h0[ch][0]
            oin_ref[1, 8 * ch:8 * ch + 8, :] = h0[ch][1]

        def emit(ch, k, h):
            or_ref[rows(ch, k), :] = h[0]
            oi_ref[rows(ch, k), :] = h[1]

        def pair(h, p):
            return h[0] * p[0] + h[1] * p[1], h[1] * p[0] - h[0] * p[1]

        def sweep2(n, carry):
            k = kk(n)
            hs = carry[:nch]
            new = tuple(advance(hs[ch], ch, k) for ch in range(nch))
            for ch in range(nch):
                emit(ch, k, new[ch])
            if not with_da:
                return new
            dr, di = carry[nch]
            for ch in range(nch):
                qr, qi = pair(new[ch], (hr_ref[rows(ch, k - 1), :], hi_ref[rows(ch, k - 1), :]))
                dr, di = dr + qr, di + qi
            return new + ((dr, di),)

        if with_da:
            carry = lax.fori_loop(0, seg - 1, sweep2, h0 + ((zero, zero),))
            dr, di = carry[nch]
            for ch in range(nch):
                new = advance(carry[ch], ch, 0)
                emit(ch, 0, new)
                qr, qi = pair(new, (hin_ref[0, 8 * ch:8 * ch + 8, :], hin_ref[1, 8 * ch:8 * ch + 8, :]))
                dr, di = dr + qr, di + qi
            da_ref[0] = jnp.sum(dr, axis=0, keepdims=True)
            da_ref[1] = jnp.sum(di, axis=0, keepdims=True)
        else:
            lax.fori_loop(0, seg, sweep2, h0)

    col = pl.BlockSpec((s_len, LANE), lambda l: (0, l))
    in_spec = pl.BlockSpec((2, nseg, LANE), lambda l: (0, 0, l))
    w_spec = pl.BlockSpec((None, LANE, LANE), lambda l: (l, 0, 0))
    ins = [src, w_re, w_im, a2]
    in_specs = [pl.BlockSpec((s_len, LANE), lambda l: (0, l // per)), w_spec, w_spec,
                pl.BlockSpec((8, LANE), lambda l: (0, l))]
    if with_da:
        ins += list(da_from)
        in_specs += [col, col, in_spec]
    outs = [_sds((s_len, gp), f32)] * 2 + [_sds((2, nseg, gp), f32)]
    out_specs = [col, col, in_spec]
    if with_da:
        outs.append(_sds((2, 1, gp), f32))
        out_specs.append(pl.BlockSpec((2, 1, LANE), lambda l: (0, 0, l)))
    return _pcall(body, name=name, grid=(gp // LANE,), in_specs=in_specs, out_specs=out_specs, out_shape=outs,
                  scratch_shapes=[pltpu.VMEM((s_len, LANE), f32)] * 2, sem=("arbitrary",))(*ins)


def _ssm_param_fn(lam_re, lam_im, log_step, b_re2, b_im2, expand):
    step = jnp.exp(log_step)
    xr, xi = lam_re * step, lam_im * step
    mag = jnp.exp(xr)
    ar, ai = mag * jnp.cos(xi), mag * jnp.sin(xi)
    nr, ni = ar - 1.0, ai
    den = lam_re * lam_re + lam_im * lam_im
    cr = (nr * lam_re + ni * lam_im) / den
    ci = (ni * lam_re - nr * lam_im) / den
    cre = jnp.dot(cr, expand, precision=HI, preferred_element_type=f32)
    cie = jnp.dot(ci, expand, precision=HI, preferred_element_type=f32)
    return ar, ai, cre * b_re2 - cie * b_im2, cre * b_im2 + cie * b_re2


def _ssm_params(lam_re, lam_im, log_step, b_re2, b_im2, expand):
    def body(*refs):
        res = _ssm_param_fn(*[r[...] for r in refs[:6]])
        for r, o in zip(refs[6:], res):
            r[...] = o

    g, p = lam_re.shape
    return _pcall(body, name="ssm_params", out_shape=[_sds((g, p), f32)] * 2 + [_sds(b_re2.shape, f32)] * 2)(
        lam_re, lam_im, log_step, b_re2, b_im2, expand)


def _ssm_params_bwd(lam_re, lam_im, log_step, b_re2, b_im2, expand, d_ar, d_ai, d_bbr, d_bbi):
    def body(*refs):
        prim = [r[...] for r in refs[:5]]
        ex = refs[5][...]
        cot = tuple(r[...] for r in refs[6:10])
        _, vjp = jax.vjp(lambda *p_: _ssm_param_fn(*p_, ex), *prim)
        for r, o in zip(refs[10:], vjp(cot)):
            r[...] = o

    shapes = [lam_re.shape, lam_im.shape, log_step.shape, b_re2.shape, b_im2.shape]
    return _pcall(body, name="ssm_params_bwd", out_shape=[_sds(s, f32) for s in shapes])(
        lam_re, lam_im, log_step, b_re2, b_im2, expand, d_ar, d_ai, d_bbr, d_bbi)


def _slope_table(n_heads):
    s = 2.0 ** (-8.0 * (jnp.arange(n_heads, dtype=f32) + 1.0) / n_heads)
    return jnp.broadcast_to(s[:, None, None], (n_heads, 1, LANE))


def _band_probs(q, k, slope_d, shift, live, ref_col):
    s = lax.dot_general(q, k, (NT, ((), ())), preferred_element_type=f32) * (HEAD ** -0.5)
    qi = lax.broadcasted_iota(jnp.int32, (BAND, BAND), 0)
    ki = lax.broadcasted_iota(jnp.int32, (BAND, BAND), 1)
    s = s - slope_d * (qi - ki + shift).astype(f32)
    mask = ((ki >= qi) if shift else (ki <= qi)) & live
    if ref_col is None:
        return jnp.where(mask, s, NEG)
    return jnp.where(mask, jnp.exp(s - ref_col), 0.0)


def _attn_geometry(s_len, dil):
    piece = BAND * dil
    m = max(1, 8 // dil)
    while s_len % (piece * m):
        m //= 2
    return m, piece


def _stream_rows(start, dil):
    return pl.ds(start, BAND, stride=dil) if dil > 1 else pl.ds(start, BAND)


def _attn_fwd(qn, kn, proj, v_blk, dil, slopes):
    s_len, aw = qn.shape
    n_heads = aw // HEAD
    m, piece = _attn_geometry(s_len, dil)
    rows = m * piece

    def body(q_ref, k_ref, kp_ref, v_ref, vp_ref, sl_ref, o_ref, lse_ref):
        t = pl.program_id(1)
        slope_d = sl_ref[:, 0:1] * float(dil)
        for b in range(m):
            for r in range(dil):
                idx = _stream_rows(b * piece + r, dil)
                q, kc, vc = (ref[idx, :].astype(bf16) for ref in (q_ref, k_ref, v_ref))
                if b:
                    pidx = _stream_rows((b - 1) * piece + r, dil)
                    kp, vp, live = k_ref[pidx, :].astype(bf16), v_ref[pidx, :].astype(bf16), True
                else:
                    pidx = _stream_rows(r, dil)
                    kp, vp, live = kp_ref[pidx, :].astype(bf16), vp_ref[pidx, :].astype(bf16), t > 0
                s_c = _band_probs(q, kc, slope_d, 0, True, None)
                s_p = _band_probs(q, kp, slope_d, BAND, live, None)
                mx = jnp.maximum(jnp.max(s_c, axis=1, keepdims=True), jnp.max(s_p, axis=1, keepdims=True))
                p_c, p_p = jnp.exp(s_c - mx), jnp.exp(s_p - mx)
                den = jnp.sum(p_c, axis=1, keepdims=True) + jnp.sum(p_p, axis=1, keepdims=True)
                o = jnp.dot(p_c.astype(bf16), vc, preferred_element_type=f32)
                o += jnp.dot(p_p.astype(bf16), vp, preferred_element_type=f32)
                o_ref[idx, :] = o / den
                lse_ref[idx, :] = jnp.broadcast_to(mx + jnp.log(den), (BAND, HEAD))

    def cur(blk0):
        return pl.BlockSpec((rows, HEAD), lambda h, t: (t, blk0 + h))

    def prev(blk0):
        return pl.BlockSpec((piece, HEAD), lambda h, t: (jnp.maximum(t * m - 1, 0), blk0 + h))

    sl = pl.BlockSpec((None, 1, LANE), lambda h, t: (h, 0, 0))
    return _pcall(body, name=f"attn_fwd_d{dil}", grid=(n_heads, s_len // rows),
                  in_specs=[cur(0), cur(0), prev(0), cur(v_blk), prev(v_blk), sl], out_specs=[cur(0), cur(0)],
                  out_shape=[_sds((s_len, aw), f32)] * 2, sem=("parallel", "parallel"))(
        qn, kn, kn, proj, proj, slopes)


def _attn_bwd(qn, kn, proj, v_blk, do, lse, dd, dil, slopes):
    s_len, aw = qn.shape
    n_heads = aw // HEAD
    m, piece = _attn_geometry(s_len, dil)
    rows = m * piece
    n_tiles = s_len // rows
    scale = HEAD ** -0.5

    def body(q_ref, qx_ref, k_ref, kp_ref, v_ref, vp_ref, do_ref, dox_ref, l_ref, lx_ref, d_ref, dx_ref, sl_ref,
             dq_ref, dk_ref, dv_ref):
        t = pl.program_id(1)
        slope_d = sl_ref[:, 0:1] * float(dil)

        def query_side(ref_q, ref_do, ref_l, ref_d, idx):
            return (ref_q[idx, :].astype(bf16), ref_do[idx, :].astype(bf16), ref_l[idx, :][:, 0:1],
                    ref_d[idx, :][:, 0:1])

        def block(qs, k, v, shift, live):
            q, do_, l_col, d_col = qs
            p = _band_probs(q, k, slope_d, shift, live, l_col)
            dp = lax.dot_general(do_, v, (NT, ((), ())), preferred_element_type=f32)
            return p.astype(bf16), (p * (dp - d_col)).astype(bf16)

        def tn(a_, b_):
            return lax.dot_general(a_, b_, (TN, ((), ())), preferred_element_type=f32)

        for r in range(dil):
            pend = None
            for b in range(m + 1):
                last = b == m
                if last:
                    qs = query_side(qx_ref, dox_ref, lx_ref, dx_ref, _stream_rows(r, dil))
                    live = t < n_tiles - 1
                else:
                    idx = _stream_rows(b * piece + r, dil)
                    qs = query_side(q_ref, do_ref, l_ref, d_ref, idx)
                    kc, vc = k_ref[idx, :].astype(bf16), v_ref[idx, :].astype(bf16)
                if b == 0:
                    pidx = _stream_rows(r, dil)
                    kp, vp, live = kp_ref[pidx, :].astype(bf16), vp_ref[pidx, :].astype(bf16), t > 0
                elif not last:
                    kp, vp, live = kc_prev, vc_prev, True
                else:
                    kp, vp = kc_prev, vc_prev
                p_p, ds_p = block(qs, kp, vp, BAND, live)
                if pend is not None:
                    pidx_, dk_, dv_ = pend
                    dk_ref[pidx_, :] = (dk_ + tn(ds_p, qs[0])) * scale
                    dv_ref[pidx_, :] = dv_ + tn(p_p, qs[1])
                if last:
                    break
                p_c, ds_c = block(qs, kc, vc, 0, True)
                dq = jnp.dot(ds_c, kc, preferred_element_type=f32) + jnp.dot(ds_p, kp, preferred_element_type=f32)
                dq_ref[idx, :] = dq * scale
                pend = (idx, tn(ds_c, qs[0]), tn(p_c, qs[1]))
                kc_prev, vc_prev = kc, vc

    def cur(blk0):
        return pl.BlockSpec((rows, HEAD), lambda h, t: (t, blk0 + h))

    def prev(blk0):
        return pl.BlockSpec((piece, HEAD), lambda h, t: (jnp.maximum(t * m - 1, 0), blk0 + h))

    def nxt(blk0):
        return pl.BlockSpec((piece, HEAD), lambda h, t: (jnp.minimum(t * m + m, n_tiles * m - 1), blk0 + h))

    sl = pl.BlockSpec((None, 1, LANE), lambda h, t: (h, 0, 0))
    return _pcall(body, name=f"attn_bwd_d{dil}", grid=(n_heads, n_tiles),
                  in_specs=[cur(0), nxt(0), cur(0), prev(0), cur(v_blk), prev(v_blk), cur(0), nxt(0), cur(0), nxt(0),
                            cur(0), nxt(0), sl],
                  out_specs=[cur(0)] * 3, out_shape=[_sds((s_len, aw), f32)] * 3,
                  sem=("parallel", "parallel"))(qn, qn, kn, kn, proj, proj, do, do, lse, lse, dd, dd, slopes)


def _exchange(name, srcs, scatter):
    n = len(srcs)

    def body(*refs):
        src, out = refs[:n], refs[n:2 * n]
        send_sems, recv_sems, local_sems = refs[2 * n:]
        x, y, c = lax.axis_index("x"), lax.axis_index("y"), lax.axis_index("c")
        me = 4 * x + 2 * y + c

        def peer(r):
            return ((1 - x) if r & 4 else x, (1 - y) if r & 2 else y, (1 - c) if r & 1 else c)

        def lin(p):
            return 4 * p[0] + 2 * p[1] + p[2]

        def piece(a, idx):
            return src[a].at[idx] if scatter[a] else src[a]

        local, sends = [], []
        for a in range(n):
            cp = pltpu.make_async_copy(piece(a, me), out[a].at[me], local_sems.at[a])
            cp.start()
            local.append(cp)
        for r in range(1, N_DEV):
            p = peer(r)
            for a in range(n):
                cp = pltpu.make_async_remote_copy(src_ref=piece(a, lin(p)), dst_ref=out[a].at[me],
                                                  send_sem=send_sems.at[a, r - 1], recv_sem=recv_sems.at[a, r - 1],
                                                  device_id=p, device_id_type=MESH)
                cp.start()
                sends.append(cp)
        for r in range(1, N_DEV):
            p = peer(r)
            for a in range(n):
                pltpu.make_async_remote_copy(src_ref=piece(a, lin(p)), dst_ref=out[a].at[lin(p)],
                                             send_sem=send_sems.at[a, r - 1], recv_sem=recv_sems.at[a, r - 1],
                                             device_id=p, device_id_type=MESH).wait_recv()
        for cp in sends:
            cp.wait_send()
        for cp in local:
            cp.wait()

    def piece_shape(a):
        return srcs[a].shape[1:] if scatter[a] else srcs[a].shape

    any_spec = pl.BlockSpec(memory_space=pl.ANY)
    return _pcall(body, name=name, in_specs=[any_spec] * n, out_specs=[any_spec] * n,
                  out_shape=[_sds((N_DEV, *piece_shape(a)), srcs[a].dtype) for a in range(n)],
                  scratch_shapes=[pltpu.SemaphoreType.DMA((n, N_DEV - 1)), pltpu.SemaphoreType.DMA((n, N_DEV - 1)),
                                  pltpu.SemaphoreType.DMA((n,))])(*srcs)


_HBM = pl.BlockSpec(memory_space=pltpu.HBM)
_SEM = pl.BlockSpec(memory_space=pltpu.SEMAPHORE)
_EFFECT = pltpu.SideEffectType.DATAFLOW_SIDE_EFFECTING


def _peer_ids():
    x, y, c = lax.axis_index("x"), lax.axis_index("y"), lax.axis_index("c")
    peers = [((1 - x) if r & 4 else x, (1 - y) if r & 2 else y, (1 - c) if r & 1 else c) for r in range(1, N_DEV)]
    return 4 * x + 2 * y + c, peers, [4 * p[0] + 2 * p[1] + p[2] for p in peers]


def _exchange_start(name, src, scatter, after):
    piece_shape = src.shape[1:] if scatter else src.shape

    def body(src_ref, land_ref, after_ref, send_sems, recv_sems, local_sem, src_thru, land_thru, token):
        me, peers, lins = _peer_ids()

        def piece(idx):
            return src_ref.at[idx] if scatter else src_ref

        pltpu.make_async_copy(piece(me), land_ref.at[me], local_sem).start()
        for r, (p, lp) in enumerate(zip(peers, lins)):
            pltpu.make_async_remote_copy(src_ref=piece(lp), dst_ref=land_ref.at[me], send_sem=send_sems.at[r],
                                         recv_sem=recv_sems.at[r], device_id=p, device_id_type=MESH).start()
        token[...] = jnp.zeros_like(token)

    land = pltpu.with_memory_space_constraint(lax.empty((N_DEV, *piece_shape), src.dtype), pltpu.HBM)
    send_sems, recv_sems, local_sem, src_thru, land_thru, token = pl.pallas_call(
        body, name=name,
        out_shape=(pltpu.SemaphoreType.DMA((N_DEV - 1,)), pltpu.SemaphoreType.DMA((N_DEV - 1,)),
                   pltpu.SemaphoreType.DMA(()), pltpu.HBM(src.shape, src.dtype),
                   pltpu.HBM((N_DEV, *piece_shape), src.dtype), _sds((8, LANE), f32)),
        in_specs=(_HBM, _HBM, pl.BlockSpec(memory_space=pl.ANY)),
        out_specs=(_SEM, _SEM, _SEM, _HBM, _HBM, pl.BlockSpec(memory_space=pltpu.VMEM)),
        input_output_aliases={0: 3, 1: 4},
        compiler_params=pltpu.CompilerParams(has_side_effects=_EFFECT),
    )(pltpu.with_memory_space_constraint(src, pltpu.HBM), land, after)
    return (send_sems, recv_sems, local_sem, src_thru, land_thru, scatter), token


def _exchange_wait(name, handle, after):
    send_sems, recv_sems, local_sem, src_thru, land_thru, scatter = handle

    def body(src_ref, land_ref, send_sems_, recv_sems_, local_sem_, after_ref, src_dead, got_ref):
        me, peers, lins = _peer_ids()

        def piece(idx):
            return src_ref.at[idx] if scatter else src_ref

        pltpu.make_async_copy(piece(me), land_ref.at[me], local_sem_).wait()
        for r, (p, lp) in enumerate(zip(peers, lins)):
            pltpu.make_async_remote_copy(src_ref=piece(lp), dst_ref=land_ref.at[me], send_sem=send_sems_.at[r],
                                         recv_sem=recv_sems_.at[r], device_id=p, device_id_type=MESH).wait_send()
            pltpu.make_async_remote_copy(src_ref=piece(lp), dst_ref=land_ref.at[lp], send_sem=send_sems_.at[r],
                                         recv_sem=recv_sems_.at[r], device_id=p, device_id_type=MESH).wait_recv()

    return pl.pallas_call(
        body, name=name,
        out_shape=(pltpu.HBM(src_thru.shape, src_thru.dtype), pltpu.HBM(land_thru.shape, land_thru.dtype)),
        in_specs=(_HBM, _HBM, _SEM, _SEM, _SEM, pl.BlockSpec(memory_space=pl.ANY)), out_specs=(_HBM, _HBM),
        input_output_aliases={0: 0, 1: 1},
        compiler_params=pltpu.CompilerParams(has_side_effects=_EFFECT),
    )(src_thru, land_thru, send_sems, recv_sems, local_sem, after)[1]


def _adamw(name, w, m, v, g_or_stack, stacked, rows=256):
    r, c = w.shape
    tr = _tile(r, rows)

    def fn(w_, m_, v_, g_):
        if stacked:
            g = g_[0].astype(f32)
            for j in range(1, N_DEV):
                g = g + g_[j].astype(f32)
        else:
            g = g_
        m_new = ADAM_B1 * m_ + (1.0 - ADAM_B1) * g
        v_new = ADAM_B2 * v_ + (1.0 - ADAM_B2) * (g * g)
        m_hat = m_new / (1.0 - ADAM_B1 ** ADAM_STEP)
        v_hat = v_new / (1.0 - ADAM_B2 ** ADAM_STEP)
        delta = -ADAM_LR * (m_hat / (jnp.sqrt(v_hat) + ADAM_EPS) + ADAM_WD * w_)
        return g, delta, m_new, v_new

    blk = _row(tr, c)
    g_spec = pl.BlockSpec((N_DEV, tr, c), lambda i: (0, i, 0)) if stacked else blk
    return _rowwise(name, fn, r // tr, [w, m, v, g_or_stack], [blk, blk, blk, g_spec],
                    [_sds((r, c), f32)] * 4, [blk] * 4, [False] * 4)


def _ada_fwd(c_all, w_shard, b_shard):
    nb_, d = c_all.shape
    n = w_shard.shape[1]
    tn = _tile(n, 512)

    def body(c_ref, w_ref, b_ref, o_ref):
        a = jax.nn.silu(c_ref[...]).astype(bf16)
        o_ref[...] = jnp.dot(a, w_ref[...].astype(bf16), preferred_element_type=f32) + b_ref[...]

    return _pcall(body, name="ada_fwd", grid=(n // tn,),
                  in_specs=[pl.BlockSpec((nb_, d), lambda j: (0, 0)), pl.BlockSpec((d, tn), lambda j: (0, j)),
                            pl.BlockSpec((1, tn), lambda j: (0, j))],
                  out_specs=pl.BlockSpec((nb_, tn), lambda j: (0, j)), out_shape=_sds((nb_, n), f32),
                  sem=("parallel",))(c_all, w_shard, b_shard)


def _ada_bwd(c_all, dmod_cols):
    nb_, d = c_all.shape
    n = dmod_cols.shape[1]
    tn = _tile(n, 512)

    def body(c_ref, g_ref, o_ref):
        a = jax.nn.silu(c_ref[...]).astype(bf16).astype(f32)
        g = g_ref[...].astype(bf16).astype(f32)
        o_ref[...] = lax.dot_general(a, g, (TN, ((), ())), precision=HI, preferred_element_type=f32)

    return _pcall(body, name="ada_bwd", grid=(n // tn,),
                  in_specs=[pl.BlockSpec((nb_, d), lambda j: (0, 0)), pl.BlockSpec((nb_, tn), lambda j: (0, j))],
                  out_specs=pl.BlockSpec((d, tn), lambda j: (0, j)), out_shape=_sds((d, n), f32),
                  sem=("parallel",))(c_all, dmod_cols)


SMALL = ("b_ada", "norm1_g", "q_norm_g", "k_norm_g", "lam_re", "lam_im", "log_step", "b_re", "b_im", "c_re", "c_im",
         "d_skip", "b_glu", "attn_out_g", "ssm_out_g", "norm2_g")
BIG = ("w_ada", "w_in", "w_glu", "w_out", "w_ff1", "w_ff2")
ORDER = ("w_ada", "b_ada", "norm1_g", "w_in", "q_norm_g", "k_norm_g", "lam_re", "lam_im", "log_step", "b_re", "b_im",
         "c_re", "c_im", "d_skip", "w_glu", "b_glu", "attn_out_g", "ssm_out_g", "w_out", "norm2_g", "w_ff1", "w_ff2")


def _pack(parts):
    flat = jnp.concatenate([p.reshape(-1) for p in parts])
    pad = (-flat.shape[0]) % (8 * LANE)
    return jnp.pad(flat, (0, pad)).reshape(-1, LANE)


def kernel(x, c, w_ada, b_ada, norm1_g, w_in, q_norm_g, k_norm_g, lam_re, lam_im, log_step, b_re, b_im, c_re, c_im, d_skip, w_glu, b_glu, attn_out_g, ssm_out_g, w_out, norm2_g, w_ff1, w_ff2, loss_target, m_w_ada, m_b_ada, m_norm1_g, m_w_in, m_q_norm_g, m_k_norm_g, m_lam_re, m_lam_im, m_log_step, m_b_re, m_b_im, m_c_re, m_c_im, m_d_skip, m_w_glu, m_b_glu, m_attn_out_g, m_ssm_out_g, m_w_out, m_norm2_g, m_w_ff1, m_w_ff2, v_w_ada, v_b_ada, v_norm1_g, v_w_in, v_q_norm_g, v_k_norm_g, v_lam_re, v_lam_im, v_log_step, v_b_re, v_b_im, v_c_re, v_c_im, v_d_skip, v_w_glu, v_b_glu, v_attn_out_g, v_ssm_out_g, v_w_out, v_norm2_g, v_w_ff1, v_w_ff2):
    env = dict(locals())
    wts = {n: env[n] for n in ORDER}
    mom = {n: env["m_" + n] for n in ORDER}
    var = {n: env["v_" + n] for n in ORDER}

    xs, tgt = x[0], loss_target[0]
    s_len, d = xs.shape
    aw = d // 2
    sw = d - aw
    n_heads = aw // HEAD
    n_groups = sw // SSM_GROUP
    n_state = lam_re.shape[-1]
    ngs = sw // LANE
    gp = n_groups * n_state
    sb = GROUPS_PER_BLOCK * n_state
    tm = _tile(s_len, 256)
    steps = s_len // tm
    me = 4 * lax.axis_index("x") + 2 * lax.axis_index("y") + lax.axis_index("c")

    (c_all,) = _exchange("gather_c", [c], [False])
    c_all = c_all.reshape(N_DEV, d)

    n_ada = w_ada.shape[-1]
    b_ada_cols = lax.dynamic_slice_in_dim(b_ada, me * n_ada, n_ada, axis=1)
    mod_part = _ada_fwd(c_all, w_ada[0], b_ada_cols)
    (mod_all,) = _exchange("gather_mod", [mod_part], [False])
    mod = lax.dynamic_index_in_dim(mod_all, me, axis=1, keepdims=False).reshape(1, 6 * d)
    sh1, sc1, g1, sh2, sc2, g2 = (mod[:, i * d:(i + 1) * d] for i in range(6))

    gather, started = {}, jnp.zeros((1, 1), f32)
    for name in ("w_in", "w_glu", "w_out", "w_ff1", "w_ff2"):
        gather[name], token = _exchange_start("gather_" + name, wts[name][0].astype(bf16), False, mod_all)
        started = started + token[0:1, 0:1]
    sc1 = sc1 + started

    (h,) = _rowwise("norm1", _norm_mod, steps, [xs, norm1_g, sc1, sh1],
                    [_row(tm, d), _vec(d), _vec(d), _vec(d)], [_sds((s_len, d), bf16)], [_row(tm, d)], [False])
    win_g = _exchange_wait("gathered_w_in", gather["w_in"], h)
    (proj,) = _mm_nn_sharded("in_proj", h, win_g)

    def qk_fn(q, k, gq, gk):
        return _head_rms(q, gq), _head_rms(k, gk)

    qn, kn = _rowwise("qk_norm", qk_fn, steps, [proj, proj, q_norm_g, k_norm_g],
                      [_row(tm, aw, 0), _row(tm, aw, 1), _vec(HEAD), _vec(HEAD)],
                      [_sds((s_len, aw), f32)] * 2, [_row(tm, aw)] * 2, [False] * 2)
    v_blk = 2 * aw // HEAD

    slopes = _slope_table(n_heads)
    pat = [_attn_fwd(qn, kn, proj, v_blk, dil, slopes) for _, dil in DILATIONS]

    def attn_mix_fn(o1, l1, o2, l2, o3, l3):
        m = jnp.maximum(jnp.maximum(l1, l2), l3)
        e1, e2, e3 = jnp.exp(l1 - m), jnp.exp(l2 - m), jnp.exp(l3 - m)
        tot = e1 + e2 + e3
        return (e1 * o1 + e2 * o2 + e3 * o3) / tot, m + jnp.log(tot)

    attn, lse = _rowwise("attn_mix", attn_mix_fn, steps, [t for ol in pat for t in ol], [_row(tm, aw)] * 6,
                         [_sds((s_len, aw), f32)] * 2, [_row(tm, aw)] * 2, [False] * 2)

    lam_re2, lam_im2 = lam_re[0], lam_im[0]
    log_step2 = log_step[0].reshape(n_groups, 1)
    b_re2 = b_re[0].reshape(n_groups, n_state * SSM_GROUP)
    b_im2 = b_im[0].reshape(n_groups, n_state * SSM_GROUP)
    expand = jnp.repeat(jnp.eye(n_state, dtype=f32), SSM_GROUP, axis=1)
    a_re, a_im, bb_re2, bb_im2 = _ssm_params(lam_re2, lam_im2, log_step2, b_re2, b_im2, expand)
    a2 = jnp.zeros((8, gp), f32).at[0].set(a_re.reshape(gp)).at[1].set(a_im.reshape(gp))

    def by_block(t):
        return t.reshape(ngs, GROUPS_PER_BLOCK, *t.shape[1:])

    bb_re4 = by_block(bb_re2.reshape(n_groups, n_state, SSM_GROUP))
    bb_im4 = by_block(bb_im2.reshape(n_groups, n_state, SSM_GROUP))
    c_re4, c_im4 = by_block(c_re[0]), by_block(c_im[0])
    w_y_re = _bd_weight(c_re4.transpose(0, 1, 3, 2))
    w_y_im = _bd_weight(-c_im4.transpose(0, 1, 3, 2))
    w_du_re, w_du_im = _bd_weight(bb_re4), _bd_weight(bb_im4)
    w_bu_re = _lane_block_weights(bb_re2.reshape(n_groups, n_state, SSM_GROUP).transpose(0, 2, 1), n_state)
    w_bu_im = _lane_block_weights(bb_im2.reshape(n_groups, n_state, SSM_GROUP).transpose(0, 2, 1), n_state)
    w_g_re, w_g_im = _lane_block_weights(c_re[0], n_state), _lane_block_weights(-c_im[0], n_state)

    nseg = _scan_segments(s_len)
    u_seg = _to_segments(proj[:, 3 * aw:], nseg).astype(bf16)
    h_re, h_im, hin_f = _scan("ssm_scan", u_seg, w_bu_re, w_bu_im, a2, reverse=False)
    ymm = _from_segments(_bd_nn("ssm_y", [h_re, h_im], [w_y_re, w_y_im], sb, LANE), nseg)

    u_spec = _row(tm, sw, 3 * aw // sw)
    (yg,) = _rowwise("ssm_gelu", _ypre_fn, steps, [ymm, proj, d_skip], [_row(tm, sw), u_spec, _vec(sw)],
                     [_sds((s_len, sw), f32)], [_row(tm, sw)], [False])
    wglu_g = _exchange_wait("gathered_w_glu", gather["w_glu"], yg).reshape(sw, sw)
    (z,) = _mm_nn("glu_proj", yg, wglu_g)
    (cat,) = _rowwise("mix_norm", _mix_fn, steps, [attn, yg, z, b_glu, attn_out_g, ssm_out_g],
                      [_row(tm, aw), _row(tm, sw), _row(tm, sw), _vec(sw), _vec(aw), _vec(sw)],
                      [_sds((s_len, d), bf16)], [_row(tm, d)], [False])
    wout_g = _exchange_wait("gathered_w_out", gather["w_out"], cat).reshape(d, d)
    (mixed,) = _mm_nn("out_proj", cat, wout_g)

    def res_norm2_fn(x_, mixed_, g1_, gn, sc, sh):
        x1_ = x_ + g1_ * mixed_
        return x1_, _norm_mod(x1_, gn, sc, sh)

    x1, h2 = _rowwise("norm2", res_norm2_fn, steps, [xs, mixed, g1, norm2_g, sc2, sh2],
                      [_row(tm, d), _row(tm, d)] + [_vec(d)] * 4,
                      [_sds((s_len, d), f32), _sds((s_len, d), bf16)], [_row(tm, d)] * 2, [False] * 2)

    def act_epilogue(acc):
        r = jnp.maximum(acc, 0.0)
        return acc, r * r

    wff1_g = _exchange_wait("gathered_w_ff1", gather["w_ff1"], h2)
    a_ff, act = _mm_nn_sharded("ff1", h2, wff1_g, epilogue=act_epilogue,
                               outs=[_sds((s_len, 4 * d), f32), _sds((s_len, 4 * d), bf16)])
    wff2_g = _exchange_wait("gathered_w_ff2", gather["w_ff2"], act).reshape(4 * d, d)
    (ff,) = _mm_nn("ff2", act, wff2_g, tn=2048, tk=512)

    def loss_fn(x1_, ff_, tgt_, g2_):
        e = x1_ + g2_ * ff_ - tgt_
        dy_ = e * (1.0 / d)
        part = jnp.full((1, LANE), 0.5 / d, f32) * jnp.sum(e * e)
        return dy_, g2_ * dy_, part, jnp.sum(dy_ * ff_, axis=0, keepdims=True)

    dy, dff, loss_part, d_g2 = _rowwise(
        "loss", loss_fn, steps, [x1, ff, tgt, g2], [_row(tm, d)] * 3 + [_vec(d)],
        [_sds((s_len, d), f32), _sds((s_len, d), bf16), _sds((1, LANE), f32), _sds((1, d), f32)],
        [_row(tm, d), _row(tm, d), _vec(LANE), _vec(d)], [False, False, True, True])
    loss = lax.psum(loss_part[0, 0], ("x", "y", "c"))

    def dact_epilogue(acc, a_):
        return (acc * (2.0 * jnp.maximum(a_, 0.0)),)

    (da,) = _mm_nt("ff2_dx", dff, wff2_g, epilogue=dact_epilogue, extra=[a_ff], outs=[_sds((s_len, 4 * d), bf16)])
    scatter = {}
    g_wff2 = _mm_tn("ff2_dw", act, dff).reshape(N_DEV, 4 * d // N_DEV, d)
    scatter["w_ff2"], tok_ff2 = _exchange_start("scatter_w_ff2", g_wff2, True, started)
    dh2 = _mm_nt_sharded("ff1_dx", da, wff1_g)
    g_wff1 = _mm_tn_sharded("ff1_dw", h2, da, N_DEV)
    scatter["w_ff1"], tok_ff1 = _exchange_start("scatter_w_ff1", g_wff1, True, started)
    norm2_g_t = norm2_g + (tok_ff2[0:1, 0:1] + tok_ff1[0:1, 0:1])

    def norm2_bwd_fn(dh2_, x1_, dy_, mixed_, gn, sc, sh, g1_):
        _, vjp = jax.vjp(_norm_mod, x1_, gn, sc, sh)
        dx, dgn, dsc, dsh = vjp(dh2_)
        dx1_ = dy_ + dx
        return dx1_, g1_ * dx1_, dgn, dsc, dsh, jnp.sum(dx1_ * mixed_, axis=0, keepdims=True)

    dx1, dmixed, d_norm2_g, d_sc2, d_sh2, d_g1 = _rowwise(
        "norm2_bwd", norm2_bwd_fn, steps, [dh2, x1, dy, mixed, norm2_g_t, sc2, sh2, g1],
        [_row(tm, d)] * 4 + [_vec(d)] * 4,
        [_sds((s_len, d), f32), _sds((s_len, d), bf16)] + [_sds((1, d), f32)] * 4,
        [_row(tm, d)] * 2 + [_vec(d)] * 4, [False, False, True, True, True, True])

    (dcat,) = _mm_nt("out_dx", dmixed, wout_g)
    g_wout = _mm_tn("out_dw", cat, dmixed).reshape(N_DEV, d // N_DEV, d)
    scatter["w_out"], tok_out = _exchange_start("scatter_w_out", g_wout, True, started)
    b_glu_t = b_glu + tok_out[0:1, 0:1]

    def mix_bwd_fn(dcat_, attn_, yg_, z_, bglu, ga, gs):
        _, vjp = jax.vjp(_mix_fn, attn_, yg_, z_, bglu, ga, gs)
        dattn_, dyg_, dz_, dbglu, dga, dgs = vjp(dcat_)
        prod = dattn_ * attn_
        dd_ = jnp.concatenate([jnp.broadcast_to(jnp.sum(prod[:, i * HEAD:(i + 1) * HEAD], axis=1, keepdims=True),
                                                (prod.shape[0], HEAD)) for i in range(n_heads)], axis=1)
        return dattn_, dd_, dyg_, dz_, dbglu, dga, dgs

    dattn, dd, dyg1, dz, d_b_glu, d_attn_out_g, d_ssm_out_g = _rowwise(
        "mix_bwd", mix_bwd_fn, steps, [dcat, attn, yg, z, b_glu_t, attn_out_g, ssm_out_g],
        [_row(tm, d), _row(tm, aw), _row(tm, sw), _row(tm, sw), _vec(sw), _vec(aw), _vec(sw)],
        [_sds((s_len, aw), f32), _sds((s_len, aw), f32), _sds((s_len, sw), f32), _sds((s_len, sw), bf16),
         _sds((1, sw), f32), _sds((1, aw), f32), _sds((1, sw), f32)],
        [_row(tm, aw), _row(tm, aw), _row(tm, sw), _row(tm, sw), _vec(sw), _vec(aw), _vec(sw)],
        [False] * 4 + [True] * 3)

    (dyg2,) = _mm_nt("glu_dx", dz, wglu_g)
    g_wglu = _mm_tn("glu_dw", yg, dz).reshape(N_DEV, sw // N_DEV, sw)
    scatter["w_glu"], tok_glu = _exchange_start("scatter_w_glu", g_wglu, True, started)
    d_skip_t = d_skip + tok_glu[0:1, 0:1]

    def gelu_bwd_fn(dyg1_, dyg2_, ymm_, u_, dskip):
        _, vjp = jax.vjp(_ypre_fn, ymm_, u_, dskip)
        dymm, du_, ddskip = vjp(dyg1_ + dyg2_)
        return dymm, du_, ddskip

    dymm, du_skip, d_d_skip = _rowwise(
        "ssm_gelu_bwd", gelu_bwd_fn, steps, [dyg1, dyg2, ymm, proj, d_skip_t],
        [_row(tm, sw)] * 3 + [u_spec, _vec(sw)],
        [_sds((s_len, sw), f32), _sds((s_len, sw), f32), _sds((1, sw), f32)],
        [_row(tm, sw), _row(tm, sw), _vec(sw)], [False, False, True])

    dymm_seg = _to_segments(dymm, nseg).astype(bf16)
    lr, li, _, da_seg = _scan("ssm_adj", dymm_seg, w_g_re, w_g_im, a2, reverse=True, da_from=(h_re, h_im, hin_f))
    du_ssm = _from_segments(_bd_nn("ssm_du", [lr, li], [w_du_re, w_du_im], sb, LANE), nseg)
    dc_re_c, dc_im_c = _bd_tn("ssm_dc", dymm_seg, [h_re, h_im], LANE, sb)
    dbb_re_c, dbb_im_c = _bd_tn("ssm_dbbar", u_seg, [lr, li], LANE, sb)

    def diag_to_gpi(w):
        return _bd_diag(w, SSM_GROUP, n_state).transpose(0, 1, 3, 2).reshape(n_groups, n_state * SSM_GROUP)

    d_lam_re, d_lam_im, d_log_step, d_b_re2, d_b_im2 = _ssm_params_bwd(
        lam_re2, lam_im2, log_step2, b_re2, b_im2, expand,
        da_seg[0, 0].reshape(n_groups, n_state), da_seg[1, 0].reshape(n_groups, n_state),
        diag_to_gpi(dbb_re_c), diag_to_gpi(dbb_im_c))
    d_c_re = _bd_diag(dc_re_c, SSM_GROUP, n_state).reshape(n_groups, SSM_GROUP, n_state)
    d_c_im = -_bd_diag(dc_im_c, SSM_GROUP, n_state).reshape(n_groups, SSM_GROUP, n_state)

    grads_qkv = [_attn_bwd(qn, kn, proj, v_blk, dattn, lse, dd, dil, slopes) for _, dil in DILATIONS]

    def qkv_bwd_fn(q, k, gq, gk, dq1, dq2, dq3, dk1, dk2, dk3, dv1, dv2, dv3, du1, du2):
        _, vjp = jax.vjp(lambda q_, k_, gq_, gk_: (_head_rms(q_, gq_), _head_rms(k_, gk_)), q, k, gq, gk)
        dq, dk, dgq, dgk = vjp((dq1 + dq2 + dq3, dk1 + dk2 + dk3))
        return jnp.concatenate([dq, dk, dv1 + dv2 + dv3, du1 + du2], axis=1), dgq, dgk

    qkv_cots = [grads_qkv[p][i] for i in range(3) for p in range(3)]
    dproj, d_q_norm_g, d_k_norm_g = _rowwise(
        "qk_norm_bwd", qkv_bwd_fn, steps, [proj, proj, q_norm_g, k_norm_g, *qkv_cots, du_skip, du_ssm],
        [_row(tm, aw, 0), _row(tm, aw, 1), _vec(HEAD), _vec(HEAD)] + [_row(tm, aw)] * 9 + [_row(tm, sw)] * 2,
        [_sds((s_len, 3 * aw + sw), bf16), _sds((1, HEAD), f32), _sds((1, HEAD), f32)],
        [_row(tm, 3 * aw + sw), _vec(HEAD), _vec(HEAD)], [False, True, True])

    g_win = _mm_tn_sharded("in_dw", h, dproj, N_DEV)
    scatter["w_in"], tok_in = _exchange_start("scatter_w_in", g_win, True, started)
    dh = _mm_nt_sharded("in_dx", dproj, win_g)
    norm1_g_t = norm1_g + tok_in[0:1, 0:1]

    def norm1_bwd_fn(dh_, x_, dx1_, gn, sc, sh):
        _, vjp = jax.vjp(_norm_mod, x_, gn, sc, sh)
        dx, dgn, dsc, dsh = vjp(dh_)
        return dx1_ + dx, dgn, dsc, dsh

    grad_x, d_norm1_g, d_sc1, d_sh1 = _rowwise(
        "norm1_bwd", norm1_bwd_fn, steps, [dh, xs, dx1, norm1_g_t, sc1, sh1], [_row(tm, d)] * 3 + [_vec(d)] * 3,
        [_sds((s_len, d), f32)] + [_sds((1, d), f32)] * 3, [_row(tm, d)] + [_vec(d)] * 3,
        [False, True, True, True])

    dmod = jnp.concatenate([d_sh1, d_sc1, d_g1, d_sh2, d_sc2, d_g2], axis=1)
    small_g = {"b_ada": dmod, "norm1_g": d_norm1_g, "q_norm_g": d_q_norm_g, "k_norm_g": d_k_norm_g,
               "lam_re": d_lam_re, "lam_im": d_lam_im, "log_step": d_log_step, "b_re": d_b_re2, "b_im": d_b_im2,
               "c_re": d_c_re, "c_im": d_c_im, "d_skip": d_d_skip, "b_glu": d_b_glu,
               "attn_out_g": d_attn_out_g, "ssm_out_g": d_ssm_out_g, "norm2_g": d_norm2_g}
    small_part = _pack([small_g[n] for n in SMALL])
    (r_small,) = _exchange("gather_small_grads", [small_part], [False])

    res = {}
    dmod_all = r_small.reshape(N_DEV, -1)[:, :6 * d]
    g_wada = _ada_bwd(c_all, lax.dynamic_slice_in_dim(dmod_all, me * n_ada, n_ada, axis=1))
    res["w_ada"] = _adamw("adamw_w_ada", w_ada[0], m_w_ada[0], v_w_ada[0], g_wada, False)
    after = res["w_ada"][1]
    for name in ("w_ff2", "w_ff1", "w_out", "w_glu", "w_in"):
        stack = _exchange_wait("scattered_" + name, scatter[name], after)
        res[name] = _adamw("adamw_" + name, wts[name][0], mom[name][0], var[name][0], stack, True)
        after = res[name][1]
    small_res = _adamw("adamw_small", _pack([wts[n] for n in SMALL]), _pack([mom[n] for n in SMALL]),
                       _pack([var[n] for n in SMALL]), r_small, True, rows=4096)
    off = 0
    for n in SMALL:
        size = wts[n].size
        res[n] = [t.reshape(-1)[off:off + size] for t in small_res]
        off += size

    out = [loss, grad_x[None]]
    for i in range(4):
        out += [res[n][i].reshape(wts[n].shape) for n in ORDER]
    return tuple(out)
```

```python
import math

import jax
import jax.numpy as jnp
from jax import lax
from jax.experimental import pallas as pl
from jax.experimental.pallas import tpu as pltpu

f32, bf16 = jnp.float32, jnp.bfloat16

N_DEV = 8
LANE = 128
HEAD = 128
SSM_GROUP = 16
GROUPS_PER_BLOCK = LANE // SSM_GROUP
DILATIONS = ((128, 1), (512, 4), (2048, 16))
BAND = 128
EPS = 1e-6
ADAM_LR, ADAM_B1, ADAM_B2, ADAM_EPS, ADAM_WD, ADAM_STEP = 0.001, 0.9, 0.999, 1e-08, 0.01, 10
NEG = -1e30
VMEM_LIMIT = 60 * 1024 * 1024
HI = lax.Precision.HIGHEST
MESH = pl.DeviceIdType.MESH


def _pcall(body, **kw):
    sem = kw.pop("sem", None)
    kw["compiler_params"] = pltpu.CompilerParams(dimension_semantics=sem, vmem_limit_bytes=VMEM_LIMIT)
    return pl.pallas_call(body, **kw)


def _tile(n, pref):
    t = min(n, pref)
    while n % t:
        t //= 2
    return t


def _sds(shape, dtype):
    return jax.ShapeDtypeStruct(shape, dtype)


def _rowwise(name, fn, steps, ins, in_specs, outs, out_specs, acc):
    n_in = len(ins)

    def body(*refs):
        res = fn(*[r[...] for r in refs[:n_in]])
        res = res if isinstance(res, (tuple, list)) else (res,)
        for r, o, a in zip(refs[n_in:], res, acc):
            if a:
                @pl.when(pl.program_id(0) == 0)
                def _():
                    r[...] = jnp.zeros_like(r)
                r[...] += o
            else:
                r[...] = o.astype(r.dtype)

    return _pcall(body, name=name, grid=(steps,), in_specs=in_specs, out_specs=out_specs, out_shape=outs,
                  sem=("arbitrary",))(*ins)


def _row(tm, c, blk=0):
    return pl.BlockSpec((tm, c), lambda i: (i, blk))


def _vec(c, blk=0):
    return pl.BlockSpec((1, c), lambda i: (0, blk))


def _rms(x, g):
    return x * lax.rsqrt(jnp.mean(x * x, axis=-1, keepdims=True) + EPS) * g


def _norm_mod(x, g, sc, sh):
    return _rms(x, g) * (1.0 + sc) + sh


def _head_rms(t, g):
    return jnp.concatenate([_rms(t[:, h * HEAD:(h + 1) * HEAD], g) for h in range(t.shape[1] // HEAD)], axis=1)


def _mix_fn(attn, yg, z, bglu, ga, gs):
    ssm = yg * jax.nn.sigmoid(z + bglu)
    return jnp.concatenate([_rms(attn, ga), _rms(ssm, gs)], axis=1)


def _ypre_fn(ymm, u, dskip):
    return jax.nn.gelu(ymm + dskip * u)


def _matmul(name, a, b, *, dims, grid, a_spec, b_spec, acc_shape, outs, out_specs, extra=(), extra_specs=(),
            epilogue=None):
    gk = grid[2]
    n_x = len(extra)

    def finish(res, x_refs, o_refs):
        res = epilogue(res, *[r[...] for r in x_refs]) if epilogue is not None else (res,)
        for r, o in zip(o_refs, res):
            r[...] = o.astype(r.dtype)

    def body_single(a_ref, b_ref, *rest):
        res = lax.dot_general(a_ref[...].astype(bf16), b_ref[...].astype(bf16), (dims, ((), ())),
                              preferred_element_type=f32)
        finish(res, rest[:n_x], rest[n_x:])

    if gk == 1:
        return _pcall(body_single, name=name, grid=grid, in_specs=[a_spec, b_spec, *extra_specs],
                      out_specs=out_specs, out_shape=outs, sem=("parallel", "parallel", "arbitrary"))(a, b, *extra)

    def body(a_ref, b_ref, *rest):
        acc = rest[-1]
        k = pl.program_id(2)

        prod = lax.dot_general(a_ref[...].astype(bf16), b_ref[...].astype(bf16), (dims, ((), ())),
                               preferred_element_type=f32)

        @pl.when(k == 0)
        def _():
            acc[...] = prod

        @pl.when(jnp.logical_and(k > 0, k < gk - 1))
        def _():
            acc[...] += prod

        @pl.when(k == gk - 1)
        def _():
            finish(acc[...] + prod, rest[:n_x], rest[n_x:-1])

    return _pcall(body, name=name, grid=grid, in_specs=[a_spec, b_spec, *extra_specs], out_specs=out_specs,
                  out_shape=outs, scratch_shapes=[pltpu.VMEM(acc_shape, f32)],
                  sem=("parallel", "parallel", "arbitrary"))(a, b, *extra)


NN = ((1,), (0,))
NT = ((1,), (1,))
TN = ((0,), (0,))


def _mm_nn(name, a, b, out_dtype=f32, tm=1024, tn=1024, tk=2048, epilogue=None, extra=(), outs=None):
    m, kd = a.shape
    n = b.shape[1]
    tm, tn, tk = _tile(m, tm), _tile(n, tn), _tile(kd, tk)
    o_spec = pl.BlockSpec((tm, tn), lambda i, j, k: (i, j))
    outs = outs if outs is not None else [_sds((m, n), out_dtype)]
    return _matmul(name, a, b, dims=NN, grid=(m // tm, n // tn, kd // tk),
                   a_spec=pl.BlockSpec((tm, tk), lambda i, j, k: (i, k)),
                   b_spec=pl.BlockSpec((tk, tn), lambda i, j, k: (k, j)),
                   acc_shape=(tm, tn), outs=outs, out_specs=[o_spec] * len(outs),
                   extra=extra, extra_specs=[o_spec] * len(extra), epilogue=epilogue)


def _mm_nn_sharded(name, a, b3, out_dtype=f32, tm=1024, tk=2048, epilogue=None, outs=None):
    m, kd = a.shape
    nsh, _, n = b3.shape
    tm, tk = _tile(m, tm), _tile(kd, tk)
    o_spec = pl.BlockSpec((tm, n), lambda i, j, k: (i, j))
    outs = outs if outs is not None else [_sds((m, nsh * n), out_dtype)]
    return _matmul(name, a, b3, dims=NN, grid=(m // tm, nsh, kd // tk),
                   a_spec=pl.BlockSpec((tm, tk), lambda i, j, k: (i, k)),
                   b_spec=pl.BlockSpec((None, tk, n), lambda i, j, k: (j, k, 0)),
                   acc_shape=(tm, n), outs=outs, out_specs=[o_spec] * len(outs), epilogue=epilogue)


def _mm_nt(name, a, b, out_dtype=f32, tm=1024, tn=1024, tk=2048, epilogue=None, extra=(), outs=None):
    m, kd = a.shape
    n = b.shape[0]
    tm, tn, tk = _tile(m, tm), _tile(n, tn), _tile(kd, tk)
    o_spec = pl.BlockSpec((tm, tn), lambda i, j, k: (i, j))
    outs = outs if outs is not None else [_sds((m, n), out_dtype)]
    return _matmul(name, a, b, dims=NT, grid=(m // tm, n // tn, kd // tk),
                   a_spec=pl.BlockSpec((tm, tk), lambda i, j, k: (i, k)),
                   b_spec=pl.BlockSpec((tn, tk), lambda i, j, k: (j, k)),
                   acc_shape=(tm, tn), outs=outs, out_specs=[o_spec] * len(outs),
                   extra=extra, extra_specs=[o_spec] * len(extra), epilogue=epilogue)


def _mm_nt_sharded(name, a, b3, out_dtype=f32, tm=512, tn=2048):
    m = a.shape[0]
    nsh, n_out, n = b3.shape
    tm, tn = _tile(m, tm), _tile(n_out, tn)
    return _matmul(name, a, b3, dims=NT, grid=(m // tm, n_out // tn, nsh),
                   a_spec=pl.BlockSpec((tm, n), lambda i, j, k: (i, k)),
                   b_spec=pl.BlockSpec((None, tn, n), lambda i, j, k: (k, j, 0)),
                   acc_shape=(tm, tn), outs=[_sds((m, n_out), out_dtype)],
                   out_specs=[pl.BlockSpec((tm, tn), lambda i, j, k: (i, j))])[0]


def _mm_tn(name, a, b, out_dtype=bf16, tm=1024, tn=1024, tk=2048):
    t, m = a.shape
    n = b.shape[1]
    tm, tn, tk = _tile(m, tm), _tile(n, tn), _tile(t, tk)
    return _matmul(name, a, b, dims=TN, grid=(m // tm, n // tn, t // tk),
                   a_spec=pl.BlockSpec((tk, tm), lambda i, j, k: (k, i)),
                   b_spec=pl.BlockSpec((tk, tn), lambda i, j, k: (k, j)),
                   acc_shape=(tm, tn), outs=[_sds((m, n), out_dtype)],
                   out_specs=[pl.BlockSpec((tm, tn), lambda i, j, k: (i, j))])[0]


def _mm_tn_sharded(name, a, b, nsh, out_dtype=bf16, tm=1024, tk=2048):
    t, m = a.shape
    n = b.shape[1] // nsh
    tm, tk = _tile(m, tm), _tile(t, tk)
    return _matmul(name, a, b, dims=TN, grid=(m // tm, nsh, t // tk),
                   a_spec=pl.BlockSpec((tk, tm), lambda i, j, k: (k, i)),
                   b_spec=pl.BlockSpec((tk, n), lambda i, j, k: (k, j)),
                   acc_shape=(tm, n), outs=[_sds((nsh, m, n), out_dtype)],
                   out_specs=[pl.BlockSpec((None, tm, n), lambda i, j, k: (j, i, 0))])[0]


def _bd_nn(name, a_list, w_list, ka, nb, out_dtype=f32, tm=1024):
    n_q = len(a_list)
    m = a_list[0].shape[0]
    ngs = w_list[0].shape[0]
    tm = _tile(m, tm)

    def body(*refs):
        o_ref = refs[-1]
        tot = None
        for q in range(n_q):
            p = jnp.dot(refs[q][...].astype(bf16), refs[n_q + q][...], preferred_element_type=f32)
            tot = p if tot is None else tot + p
        o_ref[...] = tot.astype(o_ref.dtype)

    return _pcall(body, name=name, grid=(m // tm, ngs),
                  in_specs=[pl.BlockSpec((tm, ka), lambda i, s: (i, s))] * n_q
                  + [pl.BlockSpec((None, ka, nb), lambda i, s: (s, 0, 0))] * n_q,
                  out_specs=pl.BlockSpec((tm, nb), lambda i, s: (i, s)),
                  out_shape=_sds((m, ngs * nb), out_dtype), sem=("parallel", "parallel"))(*a_list, *w_list)


def _bd_tn(name, a, b_list, ra, cb, tk=1024):
    n_q = len(b_list)
    t = a.shape[0]
    ngs = a.shape[1] // ra
    tk = _tile(t, tk)

    def body(*refs):
        a_t = refs[0][...].astype(bf16)
        for q in range(n_q):
            o_ref = refs[1 + n_q + q]

            @pl.when(pl.program_id(1) == 0)
            def _():
                o_ref[...] = jnp.zeros_like(o_ref)

            o_ref[...] += lax.dot_general(a_t, refs[1 + q][...].astype(bf16), (TN, ((), ())),
                                          preferred_element_type=f32)

    return _pcall(body, name=name, grid=(ngs, t // tk),
                  in_specs=[pl.BlockSpec((tk, ra), lambda s, k: (k, s))]
                  + [pl.BlockSpec((tk, cb), lambda s, k: (k, s))] * n_q,
                  out_specs=[pl.BlockSpec((None, ra, cb), lambda s, k: (s, 0, 0))] * n_q,
                  out_shape=[_sds((ngs, ra, cb), f32)] * n_q, sem=("parallel", "arbitrary"))(a, *b_list)


def _bd_weight(t4):
    ngs, gb, r, c = t4.shape
    eye = jnp.eye(gb, dtype=t4.dtype)
    return jnp.einsum("sgrc,gh->sgrhc", t4, eye).reshape(ngs, gb * r, gb * c).astype(bf16)


def _bd_diag(w, r, c):
    ngs = w.shape[0]
    gb = w.shape[1] // r
    w5 = w.reshape(ngs, gb, r, gb, c)
    return jnp.einsum("sgrhc,gh->sgrc", w5, jnp.eye(gb, dtype=w.dtype))


SCAN_CHAINS = 8


def _scan_segments(s_len):
    nch = SCAN_CHAINS
    while s_len % (8 * nch) or (s_len // (8 * nch)) & (s_len // (8 * nch) - 1):
        nch //= 2
    return 8 * nch


def _to_segments(t, nseg):
    s_len, c = t.shape
    return t.reshape(nseg, s_len // nseg, c).transpose(1, 0, 2).reshape(s_len, c)


def _from_segments(t, nseg):
    s_len, c = t.shape
    return t.reshape(s_len // nseg, nseg, c).transpose(1, 0, 2).reshape(s_len, c)


def _lane_block_weights(t3, n_state):
    n_groups = t3.shape[0]
    gpl = LANE // n_state
    per = LANE // (gpl * SSM_GROUP)
    n_lb = n_groups // gpl
    t5 = t3.reshape(n_lb // per, per, gpl, SSM_GROUP, n_state)
    w = jnp.einsum("aqgic,gh,qs->aqsgihc", t5, jnp.eye(gpl, dtype=t3.dtype), jnp.eye(per, dtype=t3.dtype))
    return w.reshape(n_lb, LANE, LANE).astype(bf16)


def _scan(name, src, w_re, w_im, a2, *, reverse, da_from=None):
    s_len = src.shape[0]
    gp = a2.shape[1]
    per = (gp // LANE) // (src.shape[1] // LANE)
    nseg = _scan_segments(s_len)
    nch = nseg // 8
    seg = s_len // nseg
    n_sq = int(math.log2(seg))
    assert 2 ** n_sq == seg
    with_da = da_from is not None
    chunk = _tile(s_len, 1024)

    def body(*refs):
        it = iter(refs)
        src_ref, wr_ref, wi_ref, a_ref = next(it), next(it), next(it), next(it)
        if with_da:
            hr_ref, hi_ref, hin_ref = next(it), next(it), next(it)
        or_ref, oi_ref, oin_ref = next(it), next(it), next(it)
        if with_da:
            da_ref = next(it)
        xr_ref, xi_ref = next(it), next(it)

        for i in range(s_len // chunk):
            part = src_ref[i * chunk:(i + 1) * chunk, :]
            xr_ref[i * chunk:(i + 1) * chunk, :] = jnp.dot(part, wr_ref[...], preferred_element_type=f32)
            xi_ref[i * chunk:(i + 1) * chunk, :] = jnp.dot(part, wi_ref[...], preferred_element_type=f32)

        ar = a_ref[0:1, :]
        ai = -a_ref[1:2, :] if reverse else a_ref[1:2, :]
        arb, aib = jnp.broadcast_to(ar, (8, LANE)), jnp.broadcast_to(ai, (8, LANE))

        def rows(ch, k):
            return pl.ds(pl.multiple_of(k * nseg + ch * 8, 8), 8)

        def advance(h, ch, k):
            hr, hi = h
            return (arb * hr - aib * hi + xr_ref[rows(ch, k), :], arb * hi + aib * hr + xi_ref[rows(ch, k), :])

        def kk(n):
            return seg - 1 - n if reverse else n

        zero = jnp.zeros((8, LANE), f32)

        def sweep1(n, hs):
            return tuple(advance(hs[ch], ch, kk(n)) for ch in range(nch))

        ends = lax.fori_loop(0, seg, sweep1, tuple((zero, zero) for _ in range(nch)))

        pr, pi = ar, ai
        for _ in range(n_sq):
            pr, pi = pr * pr - pi * pi, 2.0 * pr * pi
        in_r, in_i = [None] * nseg, [None] * nseg
        cr = ci = jnp.zeros((1, LANE), f32)
        for j in (range(nseg - 1, -1, -1) if reverse else range(nseg)):
            in_r[j], in_i[j] = cr, ci
            er, ei = ends[j // 8][0][j % 8:j % 8 + 1, :], ends[j // 8][1][j % 8:j % 8 + 1, :]
            cr, ci = er + pr * cr - pi * ci, ei + pr * ci + pi * cr
        h0 = tuple((jnp.concatenate(in_r[8 * ch:8 * ch + 8], axis=0), jnp.concatenate(in_i[8 * ch:8 * ch + 8], axis=0))
                   for ch in range(nch))
        for ch in range(nch):
            oin_ref[0, 8 * ch:8 * ch + 8, :] = h0[ch][0]
            oin_ref[1, 8 * ch:8 * ch + 8, :] = h0[ch][1]

        def emit(ch, k, h):
            or_ref[rows(ch, k), :] = h[0]
            oi_ref[rows(ch, k), :] = h[1]

        def pair(h, p):
            return h[0] * p[0] + h[1] * p[1], h[1] * p[0] - h[0] * p[1]

        def sweep2(n, carry):
            k = kk(n)
            hs = carry[:nch]
            new = tuple(advance(hs[ch], ch, k) for ch in range(nch))
            for ch in range(nch):
                emit(ch, k, new[ch])
            if not with_da:
                return new
            dr, di = carry[nch]
            for ch in range(nch):
                qr, qi = pair(new[ch], (hr_ref[rows(ch, k - 1), :], hi_ref[rows(ch, k - 1), :]))
                dr, di = dr + qr, di + qi
            return new + ((dr, di),)

        if with_da:
            carry = lax.fori_loop(0, seg - 1, sweep2, h0 + ((zero, zero),))
            dr, di = carry[nch]
            for ch in range(nch):
                new = advance(carry[ch], ch, 0)
                emit(ch, 0, new)
                qr, qi = pair(new, (hin_ref[0, 8 * ch:8 * ch + 8, :], hin_ref[1, 8 * ch:8 * ch + 8, :]))
                dr, di = dr + qr, di + qi
            da_ref[0] = jnp.sum(dr, axis=0, keepdims=True)
            da_ref[1] = jnp.sum(di, axis=0, keepdims=True)
        else:
            lax.fori_loop(0, seg, sweep2, h0)

    col = pl.BlockSpec((s_len, LANE), lambda l: (0, l))
    in_spec = pl.BlockSpec((2, nseg, LANE), lambda l: (0, 0, l))
    w_spec = pl.BlockSpec((None, LANE, LANE), lambda l: (l, 0, 0))
    ins = [src, w_re, w_im, a2]
    in_specs = [pl.BlockSpec((s_len, LANE), lambda l: (0, l // per)), w_spec, w_spec,
                pl.BlockSpec((8, LANE), lambda l: (0, l))]
    if with_da:
        ins += list(da_from)
        in_specs += [col, col, in_spec]
    outs = [_sds((s_len, gp), f32)] * 2 + [_sds((2, nseg, gp), f32)]
    out_specs = [col, col, in_spec]
    if with_da:
        outs.append(_sds((2, 1, gp), f32))
        out_specs.append(pl.BlockSpec((2, 1, LANE), lambda l: (0, 0, l)))
    return _pcall(body, name=name, grid=(gp // LANE,), in_specs=in_specs, out_specs=out_specs, out_shape=outs,
                  scratch_shapes=[pltpu.VMEM((s_len, LANE), f32)] * 2, sem=("arbitrary",))(*ins)


def _ssm_param_fn(lam_re, lam_im, log_step, b_re2, b_im2, expand):
    step = jnp.exp(log_step)
    xr, xi = lam_re * step, lam_im * step
    mag = jnp.exp(xr)
    ar, ai = mag * jnp.cos(xi), mag * jnp.sin(xi)
    nr, ni = ar - 1.0, ai
    den = lam_re * lam_re + lam_im * lam_im
    cr = (nr * lam_re + ni * lam_im) / den
    ci = (ni * lam_re - nr * lam_im) / den
    cre = jnp.dot(cr, expand, precision=HI, preferred_element_type=f32)
    cie = jnp.dot(ci, expand, precision=HI, preferred_element_type=f32)
    return ar, ai, cre * b_re2 - cie * b_im2, cre * b_im2 + cie * b_re2


def _ssm_params(lam_re, lam_im, log_step, b_re2, b_im2, expand):
    def body(*refs):
        res = _ssm_param_fn(*[r[...] for r in refs[:6]])
        for r, o in zip(refs[6:], res):
            r[...] = o

    g, p = lam_re.shape
    return _pcall(body, name="ssm_params", out_shape=[_sds((g, p), f32)] * 2 + [_sds(b_re2.shape, f32)] * 2)(
        lam_re, lam_im, log_step, b_re2, b_im2, expand)


def _ssm_params_bwd(lam_re, lam_im, log_step, b_re2, b_im2, expand, d_ar, d_ai, d_bbr, d_bbi):
    def body(*refs):
        prim = [r[...] for r in refs[:5]]
        ex = refs[5][...]
        cot = tuple(r[...] for r in refs[6:10])
        _, vjp = jax.vjp(lambda *p_: _ssm_param_fn(*p_, ex), *prim)
        for r, o in zip(refs[10:], vjp(cot)):
            r[...] = o

    shapes = [lam_re.shape, lam_im.shape, log_step.shape, b_re2.shape, b_im2.shape]
    return _pcall(body, name="ssm_params_bwd", out_shape=[_sds(s, f32) for s in shapes])(
        lam_re, lam_im, log_step, b_re2, b_im2, expand, d_ar, d_ai, d_bbr, d_bbi)


def _slope_table(n_heads):
    s = 2.0 ** (-8.0 * (jnp.arange(n_heads, dtype=f32) + 1.0) / n_heads)
    return jnp.broadcast_to(s[:, None, None], (n_heads, 1, LANE))


def _band_bias(slope_d, shift):
    qi = lax.broadcasted_iota(jnp.int32, (BAND, BAND), 0)
    ki = lax.broadcasted_iota(jnp.int32, (BAND, BAND), 1)
    mask = (ki >= qi) if shift else (ki <= qi)
    return jnp.where(mask, -slope_d * (qi - ki + shift).astype(f32), NEG)


def _band_probs(q, k, bias, live, ref_col):
    s = lax.dot_general(q, k, (NT, ((), ())), preferred_element_type=f32) * (HEAD ** -0.5) + bias
    if ref_col is None:
        return s if live is True else jnp.where(live, s, NEG)
    p = jnp.exp(s - ref_col)
    return p if live is True else jnp.where(live, p, 0.0)


def _attn_geometry(s_len, dil):
    piece = BAND * dil
    m = max(1, 8 // dil)
    while s_len % (piece * m):
        m //= 2
    return m, piece


def _stream_rows(start, dil):
    return pl.ds(start, BAND, stride=dil) if dil > 1 else pl.ds(start, BAND)


def _attn_fwd(qn, kn, proj, v_blk, dil, slopes):
    s_len, aw = qn.shape
    n_heads = aw // HEAD
    m, piece = _attn_geometry(s_len, dil)
    rows = m * piece

    def body(q_ref, k_ref, kp_ref, v_ref, vp_ref, sl_ref, o_ref, lse_ref):
        t = pl.program_id(1)
        slope_d = sl_ref[:, 0:1] * float(dil)
        bias_c, bias_p = _band_bias(slope_d, 0), _band_bias(slope_d, BAND)
        for b in range(m):
            for r in range(dil):
                idx = _stream_rows(b * piece + r, dil)
                q, kc, vc = (ref[idx, :].astype(bf16) for ref in (q_ref, k_ref, v_ref))
                if b:
                    pidx = _stream_rows((b - 1) * piece + r, dil)
                    kp, vp, live = k_ref[pidx, :].astype(bf16), v_ref[pidx, :].astype(bf16), True
                else:
                    pidx = _stream_rows(r, dil)
                    kp, vp, live = kp_ref[pidx, :].astype(bf16), vp_ref[pidx, :].astype(bf16), t > 0
                s_c = _band_probs(q, kc, bias_c, True, None)
                s_p = _band_probs(q, kp, bias_p, live, None)
                mx = jnp.maximum(jnp.max(s_c, axis=1, keepdims=True), jnp.max(s_p, axis=1, keepdims=True))
                p_c, p_p = jnp.exp(s_c - mx), jnp.exp(s_p - mx)
                den = jnp.sum(p_c, axis=1, keepdims=True) + jnp.sum(p_p, axis=1, keepdims=True)
                o = jnp.dot(p_c.astype(bf16), vc, preferred_element_type=f32)
                o += jnp.dot(p_p.astype(bf16), vp, preferred_element_type=f32)
                o_ref[idx, :] = o / den
                lse_ref[idx, :] = jnp.broadcast_to(mx + jnp.log(den), (BAND, HEAD))

    def cur(blk0):
        return pl.BlockSpec((rows, HEAD), lambda h, t: (t, blk0 + h))

    def prev(blk0):
        return pl.BlockSpec((piece, HEAD), lambda h, t: (jnp.maximum(t * m - 1, 0), blk0 + h))

    sl = pl.BlockSpec((None, 1, LANE), lambda h, t: (h, 0, 0))
    return _pcall(body, name=f"attn_fwd_d{dil}", grid=(n_heads, s_len // rows),
                  in_specs=[cur(0), cur(0), prev(0), cur(v_blk), prev(v_blk), sl], out_specs=[cur(0), cur(0)],
                  out_shape=[_sds((s_len, aw), f32)] * 2, sem=("parallel", "parallel"))(
        qn, kn, kn, proj, proj, slopes)


def _attn_bwd(qn, kn, proj, v_blk, do, lse, dd, dil, slopes):
    s_len, aw = qn.shape
    n_heads = aw // HEAD
    m, piece = _attn_geometry(s_len, dil)
    rows = m * piece
    n_tiles = s_len // rows
    scale = HEAD ** -0.5

    def body(q_ref, qx_ref, k_ref, kp_ref, v_ref, vp_ref, do_ref, dox_ref, l_ref, lx_ref, d_ref, dx_ref, sl_ref,
             dq_ref, dk_ref, dv_ref):
        t = pl.program_id(1)
        slope_d = sl_ref[:, 0:1] * float(dil)
        bias = {0: _band_bias(slope_d, 0), BAND: _band_bias(slope_d, BAND)}

        def query_side(ref_q, ref_do, ref_l, ref_d, idx):
            return (ref_q[idx, :].astype(bf16), ref_do[idx, :].astype(bf16), ref_l[idx, :][:, 0:1],
                    ref_d[idx, :][:, 0:1])

        def block(qs, k, v, shift, live):
            q, do_, l_col, d_col = qs
            p = _band_probs(q, k, bias[shift], live, l_col)
            dp = lax.dot_general(do_, v, (NT, ((), ())), preferred_element_type=f32)
            return p.astype(bf16), (p * (dp - d_col)).astype(bf16)

        def tn(a_, b_):
            return lax.dot_general(a_, b_, (TN, ((), ())), preferred_element_type=f32)

        for r in range(dil):
            pend = None
            for b in range(m + 1):
                last = b == m
                if last:
                    qs = query_side(qx_ref, dox_ref, lx_ref, dx_ref, _stream_rows(r, dil))
                    live = t < n_tiles - 1
                else:
                    idx = _stream_rows(b * piece + r, dil)
                    qs = query_side(q_ref, do_ref, l_ref, d_ref, idx)
                    kc, vc = k_ref[idx, :].astype(bf16), v_ref[idx, :].astype(bf16)
                if b == 0:
                    pidx = _stream_rows(r, dil)
                    kp, vp, live = kp_ref[pidx, :].astype(bf16), vp_ref[pidx, :].astype(bf16), t > 0
                elif not last:
                    kp, vp, live = kc_prev, vc_prev, True
                else:
                    kp, vp = kc_prev, vc_prev
                p_p, ds_p = block(qs, kp, vp, BAND, live)
                if pend is not None:
                    pidx_, dk_, dv_ = pend
                    dk_ref[pidx_, :] = (dk_ + tn(ds_p, qs[0])) * scale
                    dv_ref[pidx_, :] = dv_ + tn(p_p, qs[1])
                if last:
                    break
                p_c, ds_c = block(qs, kc, vc, 0, True)
                dq = jnp.dot(ds_c, kc, preferred_element_type=f32) + jnp.dot(ds_p, kp, preferred_element_type=f32)
                dq_ref[idx, :] = dq * scale
                pend = (idx, tn(ds_c, qs[0]), tn(p_c, qs[1]))
                kc_prev, vc_prev = kc, vc

    def cur(blk0):
        return pl.BlockSpec((rows, HEAD), lambda h, t: (t, blk0 + h))

    def prev(blk0):
        return pl.BlockSpec((piece, HEAD), lambda h, t: (jnp.maximum(t * m - 1, 0), blk0 + h))

    def nxt(blk0):
        return pl.BlockSpec((piece, HEAD), lambda h, t: (jnp.minimum(t * m + m, n_tiles * m - 1), blk0 + h))

    sl = pl.BlockSpec((None, 1, LANE), lambda h, t: (h, 0, 0))
    return _pcall(body, name=f"attn_bwd_d{dil}", grid=(n_heads, n_tiles),
                  in_specs=[cur(0), nxt(0), cur(0), prev(0), cur(v_blk), prev(v_blk), cur(0), nxt(0), cur(0), nxt(0),
                            cur(0), nxt(0), sl],
                  out_specs=[cur(0)] * 3, out_shape=[_sds((s_len, aw), f32)] * 3,
                  sem=("parallel", "parallel"))(qn, qn, kn, kn, proj, proj, do, do, lse, lse, dd, dd, slopes)


def _exchange(name, srcs, scatter):
    n = len(srcs)

    def body(*refs):
        src, out = refs[:n], refs[n:2 * n]
        send_sems, recv_sems, local_sems = refs[2 * n:]
        x, y, c = lax.axis_index("x"), lax.axis_index("y"), lax.axis_index("c")
        me = 4 * x + 2 * y + c

        def peer(r):
            return ((1 - x) if r & 4 else x, (1 - y) if r & 2 else y, (1 - c) if r & 1 else c)

        def lin(p):
            return 4 * p[0] + 2 * p[1] + p[2]

        def piece(a, idx):
            return src[a].at[idx] if scatter[a] else src[a]

        local, sends = [], []
        for a in range(n):
            cp = pltpu.make_async_copy(piece(a, me), out[a].at[me], local_sems.at[a])
            cp.start()
            local.append(cp)
        for r in range(1, N_DEV):
            p = peer(r)
            for a in range(n):
                cp = pltpu.make_async_remote_copy(src_ref=piece(a, lin(p)), dst_ref=out[a].at[me],
                                                  send_sem=send_sems.at[a, r - 1], recv_sem=recv_sems.at[a, r - 1],
                                                  device_id=p, device_id_type=MESH)
                cp.start()
                sends.append(cp)
        for r in range(1, N_DEV):
            p = peer(r)
            for a in range(n):
                pltpu.make_async_remote_copy(src_ref=piece(a, lin(p)), dst_ref=out[a].at[lin(p)],
                                             send_sem=send_sems.at[a, r - 1], recv_sem=recv_sems.at[a, r - 1],
                                             device_id=p, device_id_type=MESH).wait_recv()
        for cp in sends:
            cp.wait_send()
        for cp in local:
            cp.wait()

    def piece_shape(a):
        return srcs[a].shape[1:] if scatter[a] else srcs[a].shape

    any_spec = pl.BlockSpec(memory_space=pl.ANY)
    return _pcall(body, name=name, in_specs=[any_spec] * n, out_specs=[any_spec] * n,
                  out_shape=[_sds((N_DEV, *piece_shape(a)), srcs[a].dtype) for a in range(n)],
                  scratch_shapes=[pltpu.SemaphoreType.DMA((n, N_DEV - 1)), pltpu.SemaphoreType.DMA((n, N_DEV - 1)),
                                  pltpu.SemaphoreType.DMA((n,))])(*srcs)


_HBM = pl.BlockSpec(memory_space=pltpu.HBM)
_SEM = pl.BlockSpec(memory_space=pltpu.SEMAPHORE)
_EFFECT = pltpu.SideEffectType.DATAFLOW_SIDE_EFFECTING


def _peer_ids():
    x, y, c = lax.axis_index("x"), lax.axis_index("y"), lax.axis_index("c")
    peers = [((1 - x) if r & 4 else x, (1 - y) if r & 2 else y, (1 - c) if r & 1 else c) for r in range(1, N_DEV)]
    return 4 * x + 2 * y + c, peers, [4 * p[0] + 2 * p[1] + p[2] for p in peers]


def _exchange_start(name, src, scatter, after):
    piece_shape = src.shape[1:] if scatter else src.shape

    def body(src_ref, land_ref, after_ref, send_sems, recv_sems, local_sem, src_thru, land_thru, token):
        me, peers, lins = _peer_ids()

        def piece(idx):
            return src_ref.at[idx] if scatter else src_ref

        pltpu.make_async_copy(piece(me), land_ref.at[me], local_sem).start()
        for r, (p, lp) in enumerate(zip(peers, lins)):
            pltpu.make_async_remote_copy(src_ref=piece(lp), dst_ref=land_ref.at[me], send_sem=send_sems.at[r],
                                         recv_sem=recv_sems.at[r], device_id=p, device_id_type=MESH).start()
        token[...] = jnp.zeros_like(token)

    land = pltpu.with_memory_space_constraint(lax.empty((N_DEV, *piece_shape), src.dtype), pltpu.HBM)
    send_sems, recv_sems, local_sem, src_thru, land_thru, token = pl.pallas_call(
        body, name=name,
        out_shape=(pltpu.SemaphoreType.DMA((N_DEV - 1,)), pltpu.SemaphoreType.DMA((N_DEV - 1,)),
                   pltpu.SemaphoreType.DMA(()), pltpu.HBM(src.shape, src.dtype),
                   pltpu.HBM((N_DEV, *piece_shape), src.dtype), _sds((8, LANE), f32)),
        in_specs=(_HBM, _HBM, pl.BlockSpec(memory_space=pl.ANY)),
        out_specs=(_SEM, _SEM, _SEM, _HBM, _HBM, pl.BlockSpec(memory_space=pltpu.VMEM)),
        input_output_aliases={0: 3, 1: 4},
        compiler_params=pltpu.CompilerParams(has_side_effects=_EFFECT),
    )(pltpu.with_memory_space_constraint(src, pltpu.HBM), land, after)
    return (send_sems, recv_sems, local_sem, src_thru, land_thru, scatter), token


def _exchange_wait(name, handle, after):
    send_sems, recv_sems, local_sem, src_thru, land_thru, scatter = handle

    def body(src_ref, land_ref, send_sems_, recv_sems_, local_sem_, after_ref, src_dead, got_ref):
        me, peers, lins = _peer_ids()

        def piece(idx):
            return src_ref.at[idx] if scatter else src_ref

        pltpu.make_async_copy(piece(me), land_ref.at[me], local_sem_).wait()
        for r, (p, lp) in enumerate(zip(peers, lins)):
            pltpu.make_async_remote_copy(src_ref=piece(lp), dst_ref=land_ref.at[me], send_sem=send_sems_.at[r],
                                         recv_sem=recv_sems_.at[r], device_id=p, device_id_type=MESH).wait_send()
            pltpu.make_async_remote_copy(src_ref=piece(lp), dst_ref=land_ref.at[lp], send_sem=send_sems_.at[r],
                                         recv_sem=recv_sems_.at[r], device_id=p, device_id_type=MESH).wait_recv()

    return pl.pallas_call(
        body, name=name,
        out_shape=(pltpu.HBM(src_thru.shape, src_thru.dtype), pltpu.HBM(land_thru.shape, land_thru.dtype)),
        in_specs=(_HBM, _HBM, _SEM, _SEM, _SEM, pl.BlockSpec(memory_space=pl.ANY)), out_specs=(_HBM, _HBM),
        input_output_aliases={0: 0, 1: 1},
        compiler_params=pltpu.CompilerParams(has_side_effects=_EFFECT),
    )(src_thru, land_thru, send_sems, recv_sems, local_sem, after)[1]


def _adamw(name, w, m, v, g_or_stack, stacked, rows=256):
    r, c = w.shape
    tr = _tile(r, rows)

    def fn(w_, m_, v_, g_):
        if stacked:
            g = g_[0].astype(f32)
            for j in range(1, N_DEV):
                g = g + g_[j].astype(f32)
        else:
            g = g_
        m_new = ADAM_B1 * m_ + (1.0 - ADAM_B1) * g
        v_new = ADAM_B2 * v_ + (1.0 - ADAM_B2) * (g * g)
        m_hat = m_new / (1.0 - ADAM_B1 ** ADAM_STEP)
        v_hat = v_new / (1.0 - ADAM_B2 ** ADAM_STEP)
        delta = -ADAM_LR * (m_hat / (jnp.sqrt(v_hat) + ADAM_EPS) + ADAM_WD * w_)
        return g, delta, m_new, v_new

    blk = _row(tr, c)
    g_spec = pl.BlockSpec((N_DEV, tr, c), lambda i: (0, i, 0)) if stacked else blk
    return _rowwise(name, fn, r // tr, [w, m, v, g_or_stack], [blk, blk, blk, g_spec],
                    [_sds((r, c), f32)] * 4, [blk] * 4, [False] * 4)


def _ada_fwd(c_all, w_shard, b_shard):
    nb_, d = c_all.shape
    n = w_shard.shape[1]
    tn = _tile(n, 512)

    def body(c_ref, w_ref, b_ref, o_ref):
        a = jax.nn.silu(c_ref[...]).astype(bf16)
        o_ref[...] = jnp.dot(a, w_ref[...].astype(bf16), preferred_element_type=f32) + b_ref[...]

    return _pcall(body, name="ada_fwd", grid=(n // tn,),
                  in_specs=[pl.BlockSpec((nb_, d), lambda j: (0, 0)), pl.BlockSpec((d, tn), lambda j: (0, j)),
                            pl.BlockSpec((1, tn), lambda j: (0, j))],
                  out_specs=pl.BlockSpec((nb_, tn), lambda j: (0, j)), out_shape=_sds((nb_, n), f32),
                  sem=("parallel",))(c_all, w_shard, b_shard)


def _ada_bwd(c_all, dmod_cols):
    nb_, d = c_all.shape
    n = dmod_cols.shape[1]
    tn = _tile(n, 512)

    def body(c_ref, g_ref, o_ref):
        a = jax.nn.silu(c_ref[...]).astype(bf16).astype(f32)
        g = g_ref[...].astype(bf16).astype(f32)
        o_ref[...] = lax.dot_general(a, g, (TN, ((), ())), precision=HI, preferred_element_type=f32)

    return _pcall(body, name="ada_bwd", grid=(n // tn,),
                  in_specs=[pl.BlockSpec((nb_, d), lambda j: (0, 0)), pl.BlockSpec((nb_, tn), lambda j: (0, j))],
                  out_specs=pl.BlockSpec((d, tn), lambda j: (0, j)), out_shape=_sds((d, n), f32),
                  sem=("parallel",))(c_all, dmod_cols)


SMALL_LATE = ("b_ada", "norm1_g")
SMALL_EARLY = ("q_norm_g", "k_norm_g", "lam_re", "lam_im", "log_step", "b_re", "b_im", "c_re", "c_im", "d_skip",
               "b_glu", "attn_out_g", "ssm_out_g", "norm2_g")
ORDER = ("w_ada", "b_ada", "norm1_g", "w_in", "q_norm_g", "k_norm_g", "lam_re", "lam_im", "log_step", "b_re", "b_im",
         "c_re", "c_im", "d_skip", "w_glu", "b_glu", "attn_out_g", "ssm_out_g", "w_out", "norm2_g", "w_ff1", "w_ff2")


def _pack(parts):
    flat = jnp.concatenate([p.reshape(-1) for p in parts])
    pad = (-flat.shape[0]) % (8 * LANE)
    return jnp.pad(flat, (0, pad)).reshape(-1, LANE)


def kernel(x, c, w_ada, b_ada, norm1_g, w_in, q_norm_g, k_norm_g, lam_re, lam_im, log_step, b_re, b_im, c_re, c_im, d_skip, w_glu, b_glu, attn_out_g, ssm_out_g, w_out, norm2_g, w_ff1, w_ff2, loss_target, m_w_ada, m_b_ada, m_norm1_g, m_w_in, m_q_norm_g, m_k_norm_g, m_lam_re, m_lam_im, m_log_step, m_b_re, m_b_im, m_c_re, m_c_im, m_d_skip, m_w_glu, m_b_glu, m_attn_out_g, m_ssm_out_g, m_w_out, m_norm2_g, m_w_ff1, m_w_ff2, v_w_ada, v_b_ada, v_norm1_g, v_w_in, v_q_norm_g, v_k_norm_g, v_lam_re, v_lam_im, v_log_step, v_b_re, v_b_im, v_c_re, v_c_im, v_d_skip, v_w_glu, v_b_glu, v_attn_out_g, v_ssm_out_g, v_w_out, v_norm2_g, v_w_ff1, v_w_ff2):
    env = dict(locals())
    wts = {n: env[n] for n in ORDER}
    mom = {n: env["m_" + n] for n in ORDER}
    var = {n: env["v_" + n] for n in ORDER}

    xs, tgt = x[0], loss_target[0]
    s_len, d = xs.shape
    aw = d // 2
    sw = d - aw
    n_heads = aw // HEAD
    n_groups = sw // SSM_GROUP
    n_state = lam_re.shape[-1]
    ngs = sw // LANE
    gp = n_groups * n_state
    sb = GROUPS_PER_BLOCK * n_state
    tm = _tile(s_len, 256)
    steps = s_len // tm
    me = 4 * lax.axis_index("x") + 2 * lax.axis_index("y") + lax.axis_index("c")

    (c_all,) = _exchange("gather_c", [c], [False])
    c_all = c_all.reshape(N_DEV, d)

    n_ada = w_ada.shape[-1]
    b_ada_cols = lax.dynamic_slice_in_dim(b_ada, me * n_ada, n_ada, axis=1)
    mod_part = _ada_fwd(c_all, w_ada[0], b_ada_cols)
    (mod_all,) = _exchange("gather_mod", [mod_part], [False])
    mod = lax.dynamic_index_in_dim(mod_all, me, axis=1, keepdims=False).reshape(1, 6 * d)
    sh1, sc1, g1, sh2, sc2, g2 = (mod[:, i * d:(i + 1) * d] for i in range(6))

    gather, started = {}, jnp.zeros((1, 1), f32)
    for name in ("w_in", "w_glu", "w_out", "w_ff1", "w_ff2"):
        gather[name], token = _exchange_start("gather_" + name, wts[name][0].astype(bf16), False, mod_all)
        started = started + token[0:1, 0:1]
    sc1 = sc1 + started

    (h,) = _rowwise("norm1", _norm_mod, steps, [xs, norm1_g, sc1, sh1],
                    [_row(tm, d), _vec(d), _vec(d), _vec(d)], [_sds((s_len, d), bf16)], [_row(tm, d)], [False])
    win_g = _exchange_wait("gathered_w_in", gather["w_in"], h)
    (proj,) = _mm_nn_sharded("in_proj", h, win_g)

    def qk_fn(q, k, gq, gk):
        return _head_rms(q, gq), _head_rms(k, gk)

    qn, kn = _rowwise("qk_norm", qk_fn, steps, [proj, proj, q_norm_g, k_norm_g],
                      [_row(tm, aw, 0), _row(tm, aw, 1), _vec(HEAD), _vec(HEAD)],
                      [_sds((s_len, aw), f32)] * 2, [_row(tm, aw)] * 2, [False] * 2)
    v_blk = 2 * aw // HEAD

    slopes = _slope_table(n_heads)
    pat = [_attn_fwd(qn, kn, proj, v_blk, dil, slopes) for _, dil in DILATIONS]

    def attn_mix_fn(o1, l1, o2, l2, o3, l3):
        m = jnp.maximum(jnp.maximum(l1, l2), l3)
        e1, e2, e3 = jnp.exp(l1 - m), jnp.exp(l2 - m), jnp.exp(l3 - m)
        tot = e1 + e2 + e3
        return (e1 * o1 + e2 * o2 + e3 * o3) / tot, m + jnp.log(tot)

    attn, lse = _rowwise("attn_mix", attn_mix_fn, steps, [t for ol in pat for t in ol], [_row(tm, aw)] * 6,
                         [_sds((s_len, aw), f32)] * 2, [_row(tm, aw)] * 2, [False] * 2)

    lam_re2, lam_im2 = lam_re[0], lam_im[0]
    log_step2 = log_step[0].reshape(n_groups, 1)
    b_re2 = b_re[0].reshape(n_groups, n_state * SSM_GROUP)
    b_im2 = b_im[0].reshape(n_groups, n_state * SSM_GROUP)
    expand = jnp.repeat(jnp.eye(n_state, dtype=f32), SSM_GROUP, axis=1)
    a_re, a_im, bb_re2, bb_im2 = _ssm_params(lam_re2, lam_im2, log_step2, b_re2, b_im2, expand)
    a2 = jnp.zeros((8, gp), f32).at[0].set(a_re.reshape(gp)).at[1].set(a_im.reshape(gp))

    def by_block(t):
        return t.reshape(ngs, GROUPS_PER_BLOCK, *t.shape[1:])

    bb_re4 = by_block(bb_re2.reshape(n_groups, n_state, SSM_GROUP))
    bb_im4 = by_block(bb_im2.reshape(n_groups, n_state, SSM_GROUP))
    c_re4, c_im4 = by_block(c_re[0]), by_block(c_im[0])
    w_y_re = _bd_weight(c_re4.transpose(0, 1, 3, 2))
    w_y_im = _bd_weight(-c_im4.transpose(0, 1, 3, 2))
    w_du_re, w_du_im = _bd_weight(bb_re4), _bd_weight(bb_im4)
    w_bu_re = _lane_block_weights(bb_re2.reshape(n_groups, n_state, SSM_GROUP).transpose(0, 2, 1), n_state)
    w_bu_im = _lane_block_weights(bb_im2.reshape(n_groups, n_state, SSM_GROUP).transpose(0, 2, 1), n_state)
    w_g_re, w_g_im = _lane_block_weights(c_re[0], n_state), _lane_block_weights(-c_im[0], n_state)

    nseg = _scan_segments(s_len)
    u_seg = _to_segments(proj[:, 3 * aw:], nseg).astype(bf16)
    h_re, h_im, hin_f = _scan("ssm_scan", u_seg, w_bu_re, w_bu_im, a2, reverse=False)
    ymm = _from_segments(_bd_nn("ssm_y", [h_re, h_im], [w_y_re, w_y_im], sb, LANE), nseg)

    u_spec = _row(tm, sw, 3 * aw // sw)
    (yg,) = _rowwise("ssm_gelu", _ypre_fn, steps, [ymm, proj, d_skip], [_row(tm, sw), u_spec, _vec(sw)],
                     [_sds((s_len, sw), f32)], [_row(tm, sw)], [False])
    wglu_g = _exchange_wait("gathered_w_glu", gather["w_glu"], yg).reshape(sw, sw)
    (z,) = _mm_nn("glu_proj", yg, wglu_g)
    (cat,) = _rowwise("mix_norm", _mix_fn, steps, [attn, yg, z, b_glu, attn_out_g, ssm_out_g],
                      [_row(tm, aw), _row(tm, sw), _row(tm, sw), _vec(sw), _vec(aw), _vec(sw)],
                      [_sds((s_len, d), bf16)], [_row(tm, d)], [False])
    wout_g = _exchange_wait("gathered_w_out", gather["w_out"], cat).reshape(d, d)
    (mixed,) = _mm_nn("out_proj", cat, wout_g)

    def res_norm2_fn(x_, mixed_, g1_, gn, sc, sh):
        x1_ = x_ + g1_ * mixed_
        return x1_, _norm_mod(x1_, gn, sc, sh)

    x1, h2 = _rowwise("norm2", res_norm2_fn, steps, [xs, mixed, g1, norm2_g, sc2, sh2],
                      [_row(tm, d), _row(tm, d)] + [_vec(d)] * 4,
                      [_sds((s_len, d), f32), _sds((s_len, d), bf16)], [_row(tm, d)] * 2, [False] * 2)

    def act_epilogue(acc):
        r = jnp.maximum(acc, 0.0)
        return acc, r * r

    wff1_g = _exchange_wait("gathered_w_ff1", gather["w_ff1"], h2)
    a_ff, act = _mm_nn_sharded("ff1", h2, wff1_g, epilogue=act_epilogue,
                               outs=[_sds((s_len, 4 * d), f32), _sds((s_len, 4 * d), bf16)])
    wff2_g = _exchange_wait("gathered_w_ff2", gather["w_ff2"], act).reshape(4 * d, d)
    (ff,) = _mm_nn("ff2", act, wff2_g)

    def loss_fn(x1_, ff_, tgt_, g2_):
        e = x1_ + g2_ * ff_ - tgt_
        dy_ = e * (1.0 / d)
        part = jnp.full((1, LANE), 0.5 / d, f32) * jnp.sum(e * e)
        return dy_, g2_ * dy_, part, jnp.sum(dy_ * ff_, axis=0, keepdims=True)

    dy, dff, loss_part, d_g2 = _rowwise(
        "loss", loss_fn, steps, [x1, ff, tgt, g2], [_row(tm, d)] * 3 + [_vec(d)],
        [_sds((s_len, d), f32), _sds((s_len, d), bf16), _sds((1, LANE), f32), _sds((1, d), f32)],
        [_row(tm, d), _row(tm, d), _vec(LANE), _vec(d)], [False, False, True, True])
    loss = lax.psum(loss_part[0, 0], ("x", "y", "c"))

    def dact_epilogue(acc, a_):
        return (acc * (2.0 * jnp.maximum(a_, 0.0)),)

    (da,) = _mm_nt("ff2_dx", dff, wff2_g, epilogue=dact_epilogue, extra=[a_ff], outs=[_sds((s_len, 4 * d), bf16)])
    scatter = {}
    g_wff2 = _mm_tn("ff2_dw", act, dff).reshape(N_DEV, 4 * d // N_DEV, d)
    scatter["w_ff2"], tok_ff2 = _exchange_start("scatter_w_ff2", g_wff2, True, started)
    dh2 = _mm_nt_sharded("ff1_dx", da, wff1_g)
    g_wff1 = _mm_tn_sharded("ff1_dw", h2, da, N_DEV)
    scatter["w_ff1"], tok_ff1 = _exchange_start("scatter_w_ff1", g_wff1, True, started)
    norm2_g_t = norm2_g + (tok_ff2[0:1, 0:1] + tok_ff1[0:1, 0:1])

    def norm2_bwd_fn(dh2_, x1_, dy_, mixed_, gn, sc, sh, g1_):
        _, vjp = jax.vjp(_norm_mod, x1_, gn, sc, sh)
        dx, dgn, dsc, dsh = vjp(dh2_)
        dx1_ = dy_ + dx
        return dx1_, g1_ * dx1_, dgn, dsc, dsh, jnp.sum(dx1_ * mixed_, axis=0, keepdims=True)

    dx1, dmixed, d_norm2_g, d_sc2, d_sh2, d_g1 = _rowwise(
        "norm2_bwd", norm2_bwd_fn, steps, [dh2, x1, dy, mixed, norm2_g_t, sc2, sh2, g1],
        [_row(tm, d)] * 4 + [_vec(d)] * 4,
        [_sds((s_len, d), f32), _sds((s_len, d), bf16)] + [_sds((1, d), f32)] * 4,
        [_row(tm, d)] * 2 + [_vec(d)] * 4, [False, False, True, True, True, True])

    (dcat,) = _mm_nt("out_dx", dmixed, wout_g)
    g_wout = _mm_tn("out_dw", cat, dmixed).reshape(N_DEV, d // N_DEV, d)
    scatter["w_out"], tok_out = _exchange_start("scatter_w_out", g_wout, True, started)
    b_glu_t = b_glu + tok_out[0:1, 0:1]

    def mix_bwd_fn(dcat_, attn_, yg_, z_, bglu, ga, gs):
        _, vjp = jax.vjp(_mix_fn, attn_, yg_, z_, bglu, ga, gs)
        dattn_, dyg_, dz_, dbglu, dga, dgs = vjp(dcat_)
        prod = dattn_ * attn_
        dd_ = jnp.concatenate([jnp.broadcast_to(jnp.sum(prod[:, i * HEAD:(i + 1) * HEAD], axis=1, keepdims=True),
                                                (prod.shape[0], HEAD)) for i in range(n_heads)], axis=1)
        return dattn_, dd_, dyg_, dz_, dbglu, dga, dgs

    dattn, dd, dyg1, dz, d_b_glu, d_attn_out_g, d_ssm_out_g = _rowwise(
        "mix_bwd", mix_bwd_fn, steps, [dcat, attn, yg, z, b_glu_t, attn_out_g, ssm_out_g],
        [_row(tm, d), _row(tm, aw), _row(tm, sw), _row(tm, sw), _vec(sw), _vec(aw), _vec(sw)],
        [_sds((s_len, aw), f32), _sds((s_len, aw), f32), _sds((s_len, sw), f32), _sds((s_len, sw), bf16),
         _sds((1, sw), f32), _sds((1, aw), f32), _sds((1, sw), f32)],
        [_row(tm, aw), _row(tm, aw), _row(tm, sw), _row(tm, sw), _vec(sw), _vec(aw), _vec(sw)],
        [False] * 4 + [True] * 3)

    (dyg2,) = _mm_nt("glu_dx", dz, wglu_g)
    g_wglu = _mm_tn("glu_dw", yg, dz).reshape(N_DEV, sw // N_DEV, sw)
    scatter["w_glu"], tok_glu = _exchange_start("scatter_w_glu", g_wglu, True, started)
    d_skip_t = d_skip + tok_glu[0:1, 0:1]

    def gelu_bwd_fn(dyg1_, dyg2_, ymm_, u_, dskip):
        _, vjp = jax.vjp(_ypre_fn, ymm_, u_, dskip)
        dymm, du_, ddskip = vjp(dyg1_ + dyg2_)
        return dymm, du_, ddskip

    dymm, du_skip, d_d_skip = _rowwise(
        "ssm_gelu_bwd", gelu_bwd_fn, steps, [dyg1, dyg2, ymm, proj, d_skip_t],
        [_row(tm, sw)] * 3 + [u_spec, _vec(sw)],
        [_sds((s_len, sw), f32), _sds((s_len, sw), f32), _sds((1, sw), f32)],
        [_row(tm, sw), _row(tm, sw), _vec(sw)], [False, False, True])

    dymm_seg = _to_segments(dymm, nseg).astype(bf16)
    lr, li, _, da_seg = _scan("ssm_adj", dymm_seg, w_g_re, w_g_im, a2, reverse=True, da_from=(h_re, h_im, hin_f))
    du_ssm = _from_segments(_bd_nn("ssm_du", [lr, li], [w_du_re, w_du_im], sb, LANE), nseg)
    dc_re_c, dc_im_c = _bd_tn("ssm_dc", dymm_seg, [h_re, h_im], LANE, sb)
    dbb_re_c, dbb_im_c = _bd_tn("ssm_dbbar", u_seg, [lr, li], LANE, sb)

    def diag_to_gpi(w):
        return _bd_diag(w, SSM_GROUP, n_state).transpose(0, 1, 3, 2).reshape(n_groups, n_state * SSM_GROUP)

    d_lam_re, d_lam_im, d_log_step, d_b_re2, d_b_im2 = _ssm_params_bwd(
        lam_re2, lam_im2, log_step2, b_re2, b_im2, expand,
        da_seg[0, 0].reshape(n_groups, n_state), da_seg[1, 0].reshape(n_groups, n_state),
        diag_to_gpi(dbb_re_c), diag_to_gpi(dbb_im_c))
    d_c_re = _bd_diag(dc_re_c, SSM_GROUP, n_state).reshape(n_groups, SSM_GROUP, n_state)
    d_c_im = -_bd_diag(dc_im_c, SSM_GROUP, n_state).reshape(n_groups, SSM_GROUP, n_state)

    grads_qkv = [_attn_bwd(qn, kn, proj, v_blk, dattn, lse, dd, dil, slopes) for _, dil in DILATIONS]

    def qkv_bwd_fn(q, k, gq, gk, dq1, dq2, dq3, dk1, dk2, dk3, dv1, dv2, dv3, du1, du2):
        _, vjp = jax.vjp(lambda q_, k_, gq_, gk_: (_head_rms(q_, gq_), _head_rms(k_, gk_)), q, k, gq, gk)
        dq, dk, dgq, dgk = vjp((dq1 + dq2 + dq3, dk1 + dk2 + dk3))
        return jnp.concatenate([dq, dk, dv1 + dv2 + dv3, du1 + du2], axis=1), dgq, dgk

    qkv_cots = [grads_qkv[p][i] for i in range(3) for p in range(3)]
    dproj, d_q_norm_g, d_k_norm_g = _rowwise(
        "qk_norm_bwd", qkv_bwd_fn, steps, [proj, proj, q_norm_g, k_norm_g, *qkv_cots, du_skip, du_ssm],
        [_row(tm, aw, 0), _row(tm, aw, 1), _vec(HEAD), _vec(HEAD)] + [_row(tm, aw)] * 9 + [_row(tm, sw)] * 2,
        [_sds((s_len, 3 * aw + sw), bf16), _sds((1, HEAD), f32), _sds((1, HEAD), f32)],
        [_row(tm, 3 * aw + sw), _vec(HEAD), _vec(HEAD)], [False, True, True])

    small_g = {"q_norm_g": d_q_norm_g, "k_norm_g": d_k_norm_g, "lam_re": d_lam_re, "lam_im": d_lam_im,
               "log_step": d_log_step, "b_re": d_b_re2, "b_im": d_b_im2, "c_re": d_c_re, "c_im": d_c_im,
               "d_skip": d_d_skip, "b_glu": d_b_glu, "attn_out_g": d_attn_out_g, "ssm_out_g": d_ssm_out_g,
               "norm2_g": d_norm2_g}
    early, tok_early = _exchange_start("gather_early_grads", _pack([small_g[n] for n in SMALL_EARLY]), False, started)

    g_win = _mm_tn_sharded("in_dw", h, dproj, N_DEV)
    scatter["w_in"], tok_in = _exchange_start("scatter_w_in", g_win, True, tok_early)
    dh = _mm_nt_sharded("in_dx", dproj, win_g)
    norm1_g_t = norm1_g + tok_in[0:1, 0:1]

    def norm1_bwd_fn(dh_, x_, dx1_, gn, sc, sh):
        _, vjp = jax.vjp(_norm_mod, x_, gn, sc, sh)
        dx, dgn, dsc, dsh = vjp(dh_)
        return dx1_ + dx, dgn, dsc, dsh

    grad_x, d_norm1_g, d_sc1, d_sh1 = _rowwise(
        "norm1_bwd", norm1_bwd_fn, steps, [dh, xs, dx1, norm1_g_t, sc1, sh1], [_row(tm, d)] * 3 + [_vec(d)] * 3,
        [_sds((s_len, d), f32)] + [_sds((1, d), f32)] * 3, [_row(tm, d)] + [_vec(d)] * 3,
        [False, True, True, True])

    small_g["b_ada"] = jnp.concatenate([d_sh1, d_sc1, d_g1, d_sh2, d_sc2, d_g2], axis=1)
    small_g["norm1_g"] = d_norm1_g
    (r_late,) = _exchange("gather_late_grads", [_pack([small_g[n] for n in SMALL_LATE])], [False])

    res = {}
    dmod_all = r_late.reshape(N_DEV, -1)[:, :6 * d]
    g_wada = _ada_bwd(c_all, lax.dynamic_slice_in_dim(dmod_all, me * n_ada, n_ada, axis=1))
    res["w_ada"] = _adamw("adamw_w_ada", w_ada[0], m_w_ada[0], v_w_ada[0], g_wada, False)
    after = res["w_ada"][1]
    for name in ("w_ff2", "w_ff1", "w_out", "w_glu", "w_in"):
        stack = _exchange_wait("scattered_" + name, scatter[name], after)
        res[name] = _adamw("adamw_" + name, wts[name][0], mom[name][0], var[name][0], stack, True)
        after = res[name][1]
    r_early = _exchange_wait("gathered_early_grads", early, after)
    for label, names, stack in (("late", SMALL_LATE, r_late), ("early", SMALL_EARLY, r_early)):
        small_res = _adamw("adamw_small_" + label, _pack([wts[n] for n in names]), _pack([mom[n] for n in names]),
                           _pack([var[n] for n in names]), stack, True, rows=4096)
        off = 0
        for n in names:
            size = wts[n].size
            res[n] = [t.reshape(-1)[off:off + size] for t in small_res]
            off += size

    out = [loss, grad_x[None]]
    for i in range(4):
        out += [res[n][i].reshape(wts[n].shape) for n in ORDER]
    return tuple(out)
```

```python
import math

import jax
import jax.numpy as jnp
from jax import lax
from jax.experimental import pallas as pl
from jax.experimental.pallas import tpu as pltpu

f32, bf16 = jnp.float32, jnp.bfloat16

N_DEV = 8
LANE = 128
HEAD = 128
SSM_GROUP = 16
GROUPS_PER_BLOCK = LANE // SSM_GROUP
DILATIONS = ((128, 1), (512, 4), (2048, 16))
BAND = 128
EPS = 1e-6
ADAM_LR, ADAM_B1, ADAM_B2, ADAM_EPS, ADAM_WD, ADAM_STEP = 0.001, 0.9, 0.999, 1e-08, 0.01, 10
NEG = -1e30
VMEM_LIMIT = 60 * 1024 * 1024
HI = lax.Precision.HIGHEST
MESH = pl.DeviceIdType.MESH


def _pcall(body, **kw):
    sem = kw.pop("sem", None)
    kw["compiler_params"] = pltpu.CompilerParams(dimension_semantics=sem, vmem_limit_bytes=VMEM_LIMIT)
    return pl.pallas_call(body, **kw)


def _tile(n, pref):
    t = min(n, pref)
    while n % t:
        t //= 2
    return t


def _sds(shape, dtype):
    return jax.ShapeDtypeStruct(shape, dtype)


def _rowwise(name, fn, steps, ins, in_specs, outs, out_specs, acc):
    n_in = len(ins)

    def body(*refs):
        res = fn(*[r[...] for r in refs[:n_in]])
        res = res if isinstance(res, (tuple, list)) else (res,)
        for r, o, a in zip(refs[n_in:], res, acc):
            if a:
                @pl.when(pl.program_id(0) == 0)
                def _():
                    r[...] = jnp.zeros_like(r)
                r[...] += o
            else:
                r[...] = o.astype(r.dtype)

    return _pcall(body, name=name, grid=(steps,), in_specs=in_specs, out_specs=out_specs, out_shape=outs,
                  sem=("arbitrary",))(*ins)


def _row(tm, c, blk=0):
    return pl.BlockSpec((tm, c), lambda i: (i, blk))


def _vec(c, blk=0):
    return pl.BlockSpec((1, c), lambda i: (0, blk))


def _rms(x, g):
    return x * lax.rsqrt(jnp.mean(x * x, axis=-1, keepdims=True) + EPS) * g


def _norm_mod(x, g, sc, sh):
    return _rms(x, g) * (1.0 + sc) + sh


def _head_rms(t, g):
    return jnp.concatenate([_rms(t[:, h * HEAD:(h + 1) * HEAD], g) for h in range(t.shape[1] // HEAD)], axis=1)


def _mix_fn(attn, yg, z, bglu, ga, gs):
    ssm = yg * jax.nn.sigmoid(z + bglu)
    return jnp.concatenate([_rms(attn, ga), _rms(ssm, gs)], axis=1)


def _ypre_fn(ymm, u, dskip):
    return jax.nn.gelu(ymm + dskip * u)


def _matmul(name, a, b, *, dims, grid, a_spec, b_spec, acc_shape, outs, out_specs, extra=(), extra_specs=(),
            epilogue=None, after=None):
    gk = grid[2]
    n_x = len(extra)
    placed = [] if after is None else [after]
    first_out = n_x + len(placed)
    ins = [a, b, *extra, *placed]
    in_specs = [a_spec, b_spec, *extra_specs] + [pl.BlockSpec(memory_space=pl.ANY)] * len(placed)

    def product(a_ref, b_ref):
        return lax.dot_general(a_ref[...].astype(bf16), b_ref[...].astype(bf16), (dims, ((), ())),
                               preferred_element_type=f32)

    def finish(res, x_refs, o_refs):
        res = epilogue(res, *[r[...] for r in x_refs]) if epilogue is not None else (res,)
        for r, o in zip(o_refs, res):
            r[...] = o.astype(r.dtype)

    def body_single(a_ref, b_ref, *rest):
        finish(product(a_ref, b_ref), rest[:n_x], rest[first_out:])

    def body_pair(a_ref, b_ref, *rest):
        acc = rest[-1]
        prod = product(a_ref, b_ref)

        @pl.when(pl.program_id(2) == 0)
        def _():
            acc[...] = prod

        @pl.when(pl.program_id(2) == 1)
        def _():
            finish(acc[...] + prod, rest[:n_x], rest[first_out:-1])

    def body(a_ref, b_ref, *rest):
        acc = rest[-1]
        k = pl.program_id(2)

        @pl.when(k == 0)
        def _():
            acc[...] = jnp.zeros_like(acc)

        acc[...] += product(a_ref, b_ref)

        @pl.when(k == gk - 1)
        def _():
            finish(acc[...], rest[:n_x], rest[first_out:-1])

    sem = ("parallel", "parallel", "arbitrary")
    if gk == 1:
        return _pcall(body_single, name=name, grid=grid, in_specs=in_specs, out_specs=out_specs, out_shape=outs,
                      sem=sem)(*ins)
    return _pcall(body_pair if gk == 2 else body, name=name, grid=grid, in_specs=in_specs, out_specs=out_specs,
                  out_shape=outs, scratch_shapes=[pltpu.VMEM(acc_shape, f32)], sem=sem)(*ins)


NN = ((1,), (0,))
NT = ((1,), (1,))
TN = ((0,), (0,))


def _mm_nn(name, a, b, out_dtype=f32, tm=1024, tn=1024, tk=2048, epilogue=None, extra=(), outs=None):
    m, kd = a.shape
    n = b.shape[1]
    tm, tn, tk = _tile(m, tm), _tile(n, tn), _tile(kd, tk)
    o_spec = pl.BlockSpec((tm, tn), lambda i, j, k: (i, j))
    outs = outs if outs is not None else [_sds((m, n), out_dtype)]
    return _matmul(name, a, b, dims=NN, grid=(m // tm, n // tn, kd // tk),
                   a_spec=pl.BlockSpec((tm, tk), lambda i, j, k: (i, k)),
                   b_spec=pl.BlockSpec((tk, tn), lambda i, j, k: (k, j)),
                   acc_shape=(tm, tn), outs=outs, out_specs=[o_spec] * len(outs),
                   extra=extra, extra_specs=[o_spec] * len(extra), epilogue=epilogue)


def _mm_nn_sharded(name, a, b3, out_dtype=f32, tm=1024, tk=2048, epilogue=None, outs=None):
    m, kd = a.shape
    nsh, _, n = b3.shape
    tm, tk = _tile(m, tm), _tile(kd, tk)
    o_spec = pl.BlockSpec((tm, n), lambda i, j, k: (i, j))
    outs = outs if outs is not None else [_sds((m, nsh * n), out_dtype)]
    return _matmul(name, a, b3, dims=NN, grid=(m // tm, nsh, kd // tk),
                   a_spec=pl.BlockSpec((tm, tk), lambda i, j, k: (i, k)),
                   b_spec=pl.BlockSpec((None, tk, n), lambda i, j, k: (j, k, 0)),
                   acc_shape=(tm, n), outs=outs, out_specs=[o_spec] * len(outs), epilogue=epilogue)


def _mm_nt(name, a, b, out_dtype=f32, tm=1024, tn=1024, tk=2048, epilogue=None, extra=(), outs=None):
    m, kd = a.shape
    n = b.shape[0]
    tm, tn, tk = _tile(m, tm), _tile(n, tn), _tile(kd, tk)
    o_spec = pl.BlockSpec((tm, tn), lambda i, j, k: (i, j))
    outs = outs if outs is not None else [_sds((m, n), out_dtype)]
    return _matmul(name, a, b, dims=NT, grid=(m // tm, n // tn, kd // tk),
                   a_spec=pl.BlockSpec((tm, tk), lambda i, j, k: (i, k)),
                   b_spec=pl.BlockSpec((tn, tk), lambda i, j, k: (j, k)),
                   acc_shape=(tm, tn), outs=outs, out_specs=[o_spec] * len(outs),
                   extra=extra, extra_specs=[o_spec] * len(extra), epilogue=epilogue)


def _mm_nt_sharded(name, a, b3, out_dtype=f32, tm=512, tn=2048, after=None):
    m = a.shape[0]
    nsh, n_out, n = b3.shape
    tm, tn = _tile(m, tm), _tile(n_out, tn)
    return _matmul(name, a, b3, dims=NT, grid=(m // tm, n_out // tn, nsh),
                   a_spec=pl.BlockSpec((tm, n), lambda i, j, k: (i, k)),
                   b_spec=pl.BlockSpec((None, tn, n), lambda i, j, k: (k, j, 0)),
                   acc_shape=(tm, tn), outs=[_sds((m, n_out), out_dtype)],
                   out_specs=[pl.BlockSpec((tm, tn), lambda i, j, k: (i, j))], after=after)[0]


def _mm_tn(name, a, b, out_dtype=bf16, tm=1024, tn=1024, tk=2048):
    t, m = a.shape
    n = b.shape[1]
    tm, tn, tk = _tile(m, tm), _tile(n, tn), _tile(t, tk)
    return _matmul(name, a, b, dims=TN, grid=(m // tm, n // tn, t // tk),
                   a_spec=pl.BlockSpec((tk, tm), lambda i, j, k: (k, i)),
                   b_spec=pl.BlockSpec((tk, tn), lambda i, j, k: (k, j)),
                   acc_shape=(tm, tn), outs=[_sds((m, n), out_dtype)],
                   out_specs=[pl.BlockSpec((tm, tn), lambda i, j, k: (i, j))])[0]


def _mm_tn_sharded(name, a, b, nsh, out_dtype=bf16, tm=1024, tk=2048):
    t, m = a.shape
    n = b.shape[1] // nsh
    tm, tk = _tile(m, tm), _tile(t, tk)
    return _matmul(name, a, b, dims=TN, grid=(m // tm, nsh, t // tk),
                   a_spec=pl.BlockSpec((tk, tm), lambda i, j, k: (k, i)),
                   b_spec=pl.BlockSpec((tk, n), lambda i, j, k: (k, j)),
                   acc_shape=(tm, n), outs=[_sds((nsh, m, n), out_dtype)],
                   out_specs=[pl.BlockSpec((None, tm, n), lambda i, j, k: (j, i, 0))])[0]


def _bd_nn(name, a_list, w_list, ka, nb, out_dtype=f32, tm=1024):
    n_q = len(a_list)
    m = a_list[0].shape[0]
    ngs = w_list[0].shape[0]
    tm = _tile(m, tm)

    def body(*refs):
        o_ref = refs[-1]
        tot = None
        for q in range(n_q):
            p = jnp.dot(refs[q][...].astype(bf16), refs[n_q + q][...], preferred_element_type=f32)
            tot = p if tot is None else tot + p
        o_ref[...] = tot.astype(o_ref.dtype)

    return _pcall(body, name=name, grid=(m // tm, ngs),
                  in_specs=[pl.BlockSpec((tm, ka), lambda i, s: (i, s))] * n_q
                  + [pl.BlockSpec((None, ka, nb), lambda i, s: (s, 0, 0))] * n_q,
                  out_specs=pl.BlockSpec((tm, nb), lambda i, s: (i, s)),
                  out_shape=_sds((m, ngs * nb), out_dtype), sem=("parallel", "parallel"))(*a_list, *w_list)


def _bd_tn(name, a, b_list, ra, cb, tk=1024):
    n_q = len(b_list)
    t = a.shape[0]
    ngs = a.shape[1] // ra
    tk = _tile(t, tk)

    def body(*refs):
        a_t = refs[0][...].astype(bf16)
        for q in range(n_q):
            o_ref = refs[1 + n_q + q]

            @pl.when(pl.program_id(1) == 0)
            def _():
                o_ref[...] = jnp.zeros_like(o_ref)

            o_ref[...] += lax.dot_general(a_t, refs[1 + q][...].astype(bf16), (TN, ((), ())),
                                          preferred_element_type=f32)

    return _pcall(body, name=name, grid=(ngs, t // tk),
                  in_specs=[pl.BlockSpec((tk, ra), lambda s, k: (k, s))]
                  + [pl.BlockSpec((tk, cb), lambda s, k: (k, s))] * n_q,
                  out_specs=[pl.BlockSpec((None, ra, cb), lambda s, k: (s, 0, 0))] * n_q,
                  out_shape=[_sds((ngs, ra, cb), f32)] * n_q, sem=("parallel", "arbitrary"))(a, *b_list)


def _bd_weight(t4):
    ngs, gb, r, c = t4.shape
    eye = jnp.eye(gb, dtype=t4.dtype)
    return jnp.einsum("sgrc,gh->sgrhc", t4, eye).reshape(ngs, gb * r, gb * c).astype(bf16)


def _bd_diag(w, r, c):
    ngs = w.shape[0]
    gb = w.shape[1] // r
    w5 = w.reshape(ngs, gb, r, gb, c)
    return jnp.einsum("sgrhc,gh->sgrc", w5, jnp.eye(gb, dtype=w.dtype))


SCAN_CHAINS = 8


def _scan_segments(s_len):
    nch = SCAN_CHAINS
    while s_len % (8 * nch) or (s_len // (8 * nch)) & (s_len // (8 * nch) - 1):
        nch //= 2
    return 8 * nch


def _to_segments(t, nseg):
    s_len, c = t.shape
    return t.reshape(nseg, s_len // nseg, c).transpose(1, 0, 2).reshape(s_len, c)


def _from_segments(t, nseg):
    s_len, c = t.shape
    return t.reshape(s_len // nseg, nseg, c).transpose(1, 0, 2).reshape(s_len, c)


def _lane_block_weights(t3, n_state):
    n_groups = t3.shape[0]
    gpl = LANE // n_state
    per = LANE // (gpl * SSM_GROUP)
    n_lb = n_groups // gpl
    t5 = t3.reshape(n_lb // per, per, gpl, SSM_GROUP, n_state)
    w = jnp.einsum("aqgic,gh,qs->aqsgihc", t5, jnp.eye(gpl, dtype=t3.dtype), jnp.eye(per, dtype=t3.dtype))
    return w.reshape(n_lb, LANE, LANE).astype(bf16)


def _scan(name, src, w_re, w_im, a2, *, reverse, da_from=None):
    s_len = src.shape[0]
    gp = a2.shape[1]
    per = (gp // LANE) // (src.shape[1] // LANE)
    nseg = _scan_segments(s_len)
    nch = nseg // 8
    seg = s_len // nseg
    n_sq = int(math.log2(seg))
    assert 2 ** n_sq == seg
    with_da = da_from is not None
    chunk = _tile(s_len, 1024)

    def body(*refs):
        it = iter(refs)
        src_ref, wr_ref, wi_ref, a_ref = next(it), next(it), next(it), next(it)
        if with_da:
            hr_ref, hi_ref, hin_ref = next(it), next(it), next(it)
        or_ref, oi_ref, oin_ref = next(it), next(it), next(it)
        if with_da:
            da_ref = next(it)
        xr_ref, xi_ref = next(it), next(it)

        for i in range(s_len // chunk):
            part = src_ref[i * chunk:(i + 1) * chunk, :]
            xr_ref[i * chunk:(i + 1) * chunk, :] = jnp.dot(part, wr_ref[...], preferred_element_type=f32)
            xi_ref[i * chunk:(i + 1) * chunk, :] = jnp.dot(part, wi_ref[...], preferred_element_type=f32)

        ar = a_ref[0:1, :]
        ai = -a_ref[1:2, :] if reverse else a_ref[1:2, :]
        arb, aib = jnp.broadcast_to(ar, (8, LANE)), jnp.broadcast_to(ai, (8, LANE))

        def rows(ch, k):
            return pl.ds(pl.multiple_of(k * nseg + ch * 8, 8), 8)

        def advance(h, ch, k):
            hr, hi = h
            return (arb * hr - aib * hi + xr_ref[rows(ch, k), :], arb * hi + aib * hr + xi_ref[rows(ch, k), :])

        def kk(n):
            return seg - 1 - n if reverse else n

        zero = jnp.zeros((8, LANE), f32)

        def sweep1(n, hs):
            return tuple(advance(hs[ch], ch, kk(n)) for ch in range(nch))

        ends = lax.fori_loop(0, seg, sweep1, tuple((zero, zero) for _ in range(nch)))

        pr, pi = ar, ai
        for _ in range(n_sq):
            pr, pi = pr * pr - pi * pi, 2.0 * pr * pi
        in_r, in_i = [None] * nseg, [None] * nseg
        cr = ci = jnp.zeros((1, LANE), f32)
        for j in (range(nseg - 1, -1, -1) if reverse else range(nseg)):
            in_r[j], in_i[j] = cr, ci
            er, ei = ends[j // 8][0][j % 8:j % 8 + 1, :], ends[j // 8][1][j % 8:j % 8 + 1, :]
            cr, ci = er + pr * cr - pi * ci, ei + pr * ci + pi * cr
        h0 = tuple((jnp.concatenate(in_r[8 * ch:8 * ch + 8], axis=0), jnp.concatenate(in_i[8 * ch:8 * ch + 8], axis=0))
                   for ch in range(nch))
        for ch in range(nch):
            oin_ref[0, 8 * ch:8 * ch + 8, :] = h0[ch][0]
            oin_ref[1, 8 * ch:8 * ch + 8, :] = h0[ch][1]

        def emit(ch, k, h):
            or_ref[rows(ch, k), :] = h[0]
            oi_ref[rows(ch, k), :] = h[1]

        def pair(h, p):
            return h[0] * p[0] + h[1] * p[1], h[1] * p[0] - h[0] * p[1]

        def sweep2(n, carry):
            k = kk(n)
            hs = carry[:nch]
            new = tuple(advance(hs[ch], ch, k) for ch in range(nch))
            for ch in range(nch):
                emit(ch, k, new[ch])
            if not with_da:
                return new
            dr, di = carry[nch]
            for ch in range(nch):
                qr, qi = pair(new[ch], (hr_ref[rows(ch, k - 1), :], hi_ref[rows(ch, k - 1), :]))
                dr, di = dr + qr, di + qi
            return new + ((dr, di),)

        if with_da:
            carry = lax.fori_loop(0, seg - 1, sweep2, h0 + ((zero, zero),))
            dr, di = carry[nch]
            for ch in range(nch):
                new = advance(carry[ch], ch, 0)
                emit(ch, 0, new)
                qr, qi = pair(new, (hin_ref[0, 8 * ch:8 * ch + 8, :], hin_ref[1, 8 * ch:8 * ch + 8, :]))
                dr, di = dr + qr, di + qi
            da_ref[0] = jnp.sum(dr, axis=0, keepdims=True)
            da_ref[1] = jnp.sum(di, axis=0, keepdims=True)
        else:
            lax.fori_loop(0, seg, sweep2, h0)

    col = pl.BlockSpec((s_len, LANE), lambda l: (0, l))
    in_spec = pl.BlockSpec((2, nseg, LANE), lambda l: (0, 0, l))
    w_spec = pl.BlockSpec((None, LANE, LANE), lambda l: (l, 0, 0))
    ins = [src, w_re, w_im, a2]
    in_specs = [pl.BlockSpec((s_len, LANE), lambda l: (0, l // per)), w_spec, w_spec,
                pl.BlockSpec((8, LANE), lambda l: (0, l))]
    if with_da:
        ins += list(da_from)
        in_specs += [col, col, in_spec]
    outs = [_sds((s_len, gp), f32)] * 2 + [_sds((2, nseg, gp), f32)]
    out_specs = [col, col, in_spec]
    if with_da:
        outs.append(_sds((2, 1, gp), f32))
        out_specs.append(pl.BlockSpec((2, 1, LANE), lambda l: (0, 0, l)))
    return _pcall(body, name=name, grid=(gp // LANE,), in_specs=in_specs, out_specs=out_specs, out_shape=outs,
                  scratch_shapes=[pltpu.VMEM((s_len, LANE), f32)] * 2, sem=("arbitrary",))(*ins)


def _ssm_param_fn(lam_re, lam_im, log_step, b_re2, b_im2, expand):
    step = jnp.exp(log_step)
    xr, xi = lam_re * step, lam_im * step
    mag = jnp.exp(xr)
    ar, ai = mag * jnp.cos(xi), mag * jnp.sin(xi)
    nr, ni = ar - 1.0, ai
    den = lam_re * lam_re + lam_im * lam_im
    cr = (nr * lam_re + ni * lam_im) / den
    ci = (ni * lam_re - nr * lam_im) / den
    cre = jnp.dot(cr, expand, precision=HI, preferred_element_type=f32)
    cie = jnp.dot(ci, expand, precision=HI, preferred_element_type=f32)
    return ar, ai, cre * b_re2 - cie * b_im2, cre * b_im2 + cie * b_re2


def _ssm_params(lam_re, lam_im, log_step, b_re2, b_im2, expand):
    def body(*refs):
        res = _ssm_param_fn(*[r[...] for r in refs[:6]])
        for r, o in zip(refs[6:], res):
            r[...] = o

    g, p = lam_re.shape
    return _pcall(body, name="ssm_params", out_shape=[_sds((g, p), f32)] * 2 + [_sds(b_re2.shape, f32)] * 2)(
        lam_re, lam_im, log_step, b_re2, b_im2, expand)


def _ssm_params_bwd(lam_re, lam_im, log_step, b_re2, b_im2, expand, d_ar, d_ai, d_bbr, d_bbi):
    def body(*refs):
        prim = [r[...] for r in refs[:5]]
        ex = refs[5][...]
        cot = tuple(r[...] for r in refs[6:10])
        _, vjp = jax.vjp(lambda *p_: _ssm_param_fn(*p_, ex), *prim)
        for r, o in zip(refs[10:], vjp(cot)):
            r[...] = o

    shapes = [lam_re.shape, lam_im.shape, log_step.shape, b_re2.shape, b_im2.shape]
    return _pcall(body, name="ssm_params_bwd", out_shape=[_sds(s, f32) for s in shapes])(
        lam_re, lam_im, log_step, b_re2, b_im2, expand, d_ar, d_ai, d_bbr, d_bbi)


def _slope_table(n_heads):
    s = 2.0 ** (-8.0 * (jnp.arange(n_heads, dtype=f32) + 1.0) / n_heads)
    return jnp.broadcast_to(s[:, None, None], (n_heads, 1, LANE))


def _band_bias(slope_d, shift):
    qi = lax.broadcasted_iota(jnp.int32, (BAND, BAND), 0)
    ki = lax.broadcasted_iota(jnp.int32, (BAND, BAND), 1)
    mask = (ki >= qi) if shift else (ki <= qi)
    return jnp.where(mask, -slope_d * (qi - ki + shift).astype(f32), NEG)


def _window_bias(slope_d, has_prev):
    own = _band_bias(slope_d, 0)
    mid = jnp.concatenate([_band_bias(slope_d, BAND), own], axis=1)
    none = jnp.concatenate([jnp.full((BAND, BAND), NEG, f32), own], axis=1)
    return mid, jnp.where(has_prev, mid, none)


def _window_scores(q, k2, bias):
    return lax.dot_general(q, k2, (NT, ((), ())), preferred_element_type=f32) * (HEAD ** -0.5) + bias


def _attn_geometry(s_len, dil):
    piece = BAND * dil
    m = max(1, 8 // dil)
    while s_len % (piece * m):
        m //= 2
    return m, piece


def _stream_rows(start, dil):
    return pl.ds(start, BAND, stride=dil) if dil > 1 else pl.ds(start, BAND)


def _attn_fwd(qn, kn, proj, v_blk, dil, slopes):
    s_len, aw = qn.shape
    n_heads = aw // HEAD
    m, piece = _attn_geometry(s_len, dil)
    rows = m * piece

    def body(q_ref, k_ref, kp_ref, v_ref, vp_ref, sl_ref, o_ref, lse_ref):
        bias_mid, bias_first = _window_bias(sl_ref[:, 0:1] * float(dil), pl.program_id(1) > 0)
        for b in range(m):
            for r in range(dil):
                idx = _stream_rows(b * piece + r, dil)
                q, kc, vc = (ref[idx, :].astype(bf16) for ref in (q_ref, k_ref, v_ref))
                if b:
                    pidx = _stream_rows((b - 1) * piece + r, dil)
                    kp, vp = k_ref[pidx, :].astype(bf16), v_ref[pidx, :].astype(bf16)
                else:
                    pidx = _stream_rows(r, dil)
                    kp, vp = kp_ref[pidx, :].astype(bf16), vp_ref[pidx, :].astype(bf16)
                k2, v2 = jnp.concatenate([kp, kc], axis=0), jnp.concatenate([vp, vc], axis=0)
                s = _window_scores(q, k2, bias_mid if b else bias_first)
                mx = jnp.max(s, axis=1, keepdims=True)
                p = jnp.exp(s - mx)
                den = jnp.sum(p, axis=1, keepdims=True)
                o_ref[idx, :] = jnp.dot(p.astype(bf16), v2, preferred_element_type=f32) / den
                lse_ref[idx, :] = jnp.broadcast_to(mx + jnp.log(den), (BAND, HEAD))

    def cur(blk0):
        return pl.BlockSpec((rows, HEAD), lambda h, t: (t, blk0 + h))

    def prev(blk0):
        return pl.BlockSpec((piece, HEAD), lambda h, t: (jnp.maximum(t * m - 1, 0), blk0 + h))

    sl = pl.BlockSpec((None, 1, LANE), lambda h, t: (h, 0, 0))
    return _pcall(body, name=f"attn_fwd_d{dil}", grid=(n_heads, s_len // rows),
                  in_specs=[cur(0), cur(0), prev(0), cur(v_blk), prev(v_blk), sl], out_specs=[cur(0), cur(0)],
                  out_shape=[_sds((s_len, aw), f32)] * 2, sem=("parallel", "parallel"))(
        qn, kn, kn, proj, proj, slopes)


def _attn_bwd(qn, kn, proj, v_blk, do, lse, dd, dil, slopes):
    s_len, aw = qn.shape
    n_heads = aw // HEAD
    m, piece = _attn_geometry(s_len, dil)
    rows = m * piece
    n_tiles = s_len // rows
    scale = HEAD ** -0.5

    def body(q_ref, qx_ref, k_ref, kp_ref, v_ref, vp_ref, do_ref, dox_ref, l_ref, lx_ref, d_ref, dx_ref, sl_ref,
             dq_ref, dk_ref, dv_ref):
        t = pl.program_id(1)
        slope_d = sl_ref[:, 0:1] * float(dil)
        bias_mid, bias_first = _window_bias(slope_d, t > 0)
        bias_next = _band_bias(slope_d, BAND)

        def query_side(ref_q, ref_do, ref_l, ref_d, idx):
            return (ref_q[idx, :].astype(bf16), ref_do[idx, :].astype(bf16), ref_l[idx, :][:, 0:1],
                    ref_d[idx, :][:, 0:1])

        def probs(qs, keys, values, bias):
            q, do_, l_col, d_col = qs
            p = jnp.exp(_window_scores(q, keys, bias) - l_col)
            dp = lax.dot_general(do_, values, (NT, ((), ())), preferred_element_type=f32)
            return p.astype(bf16), (p * (dp - d_col)).astype(bf16)

        def tn(a_, b_):
            return lax.dot_general(a_, b_, (TN, ((), ())), preferred_element_type=f32)

        for r in range(dil):
            pend = None
            for b in range(m):
                idx = _stream_rows(b * piece + r, dil)
                qs = query_side(q_ref, do_ref, l_ref, d_ref, idx)
                kc, vc = k_ref[idx, :].astype(bf16), v_ref[idx, :].astype(bf16)
                if b:
                    kp, vp = kc_prev, vc_prev
                else:
                    pidx = _stream_rows(r, dil)
                    kp, vp = kp_ref[pidx, :].astype(bf16), vp_ref[pidx, :].astype(bf16)
                k2, v2 = jnp.concatenate([kp, kc], axis=0), jnp.concatenate([vp, vc], axis=0)
                p, ds = probs(qs, k2, v2, bias_mid if b else bias_first)
                dq_ref[idx, :] = jnp.dot(ds, k2, preferred_element_type=f32) * scale
                dk2, dv2 = tn(ds, qs[0]), tn(p, qs[1])
                if pend is not None:
                    dk_ref[pend[0], :] = (pend[1] + dk2[:BAND]) * scale
                    dv_ref[pend[0], :] = pend[2] + dv2[:BAND]
                pend = (idx, dk2[BAND:], dv2[BAND:])
                kc_prev, vc_prev = kc, vc
            qs = query_side(qx_ref, dox_ref, lx_ref, dx_ref, _stream_rows(r, dil))
            p, ds = probs(qs, kc_prev, vc_prev, bias_next)
            live = t < n_tiles - 1
            dk_ref[pend[0], :] = (pend[1] + jnp.where(live, tn(ds, qs[0]), 0.0)) * scale
            dv_ref[pend[0], :] = pend[2] + jnp.where(live, tn(p, qs[1]), 0.0)

    def cur(blk0):
        return pl.BlockSpec((rows, HEAD), lambda h, t: (t, blk0 + h))

    def prev(blk0):
        return pl.BlockSpec((piece, HEAD), lambda h, t: (jnp.maximum(t * m - 1, 0), blk0 + h))

    def nxt(blk0):
        return pl.BlockSpec((piece, HEAD), lambda h, t: (jnp.minimum(t * m + m, n_tiles * m - 1), blk0 + h))

    sl = pl.BlockSpec((None, 1, LANE), lambda h, t: (h, 0, 0))
    return _pcall(body, name=f"attn_bwd_d{dil}", grid=(n_heads, n_tiles),
                  in_specs=[cur(0), nxt(0), cur(0), prev(0), cur(v_blk), prev(v_blk), cur(0), nxt(0), cur(0), nxt(0),
                            cur(0), nxt(0), sl],
                  out_specs=[cur(0)] * 3, out_shape=[_sds((s_len, aw), f32)] * 3,
                  sem=("parallel", "parallel"))(qn, qn, kn, kn, proj, proj, do, do, lse, lse, dd, dd, slopes)


def _exchange(name, srcs, scatter):
    n = len(srcs)

    def body(*refs):
        src, out = refs[:n], refs[n:2 * n]
        send_sems, recv_sems, local_sems = refs[2 * n:]
        x, y, c = lax.axis_index("x"), lax.axis_index("y"), lax.axis_index("c")
        me = 4 * x + 2 * y + c

        def peer(r):
            return ((1 - x) if r & 4 else x, (1 - y) if r & 2 else y, (1 - c) if r & 1 else c)

        def lin(p):
            return 4 * p[0] + 2 * p[1] + p[2]

        def piece(a, idx):
            return src[a].at[idx] if scatter[a] else src[a]

        local, sends = [], []
        for a in range(n):
            cp = pltpu.make_async_copy(piece(a, me), out[a].at[me], local_sems.at[a])
            cp.start()
            local.append(cp)
        for r in range(1, N_DEV):
            p = peer(r)
            for a in range(n):
                cp = pltpu.make_async_remote_copy(src_ref=piece(a, lin(p)), dst_ref=out[a].at[me],
                                                  send_sem=send_sems.at[a, r - 1], recv_sem=recv_sems.at[a, r - 1],
                                                  device_id=p, device_id_type=MESH)
                cp.start()
                sends.append(cp)
        for r in range(1, N_DEV):
            p = peer(r)
            for a in range(n):
                pltpu.make_async_remote_copy(src_ref=piece(a, lin(p)), dst_ref=out[a].at[lin(p)],
                                             send_sem=send_sems.at[a, r - 1], recv_sem=recv_sems.at[a, r - 1],
                                             device_id=p, device_id_type=MESH).wait_recv()
        for cp in sends:
            cp.wait_send()
        for cp in local:
            cp.wait()

    def piece_shape(a):
        return srcs[a].shape[1:] if scatter[a] else srcs[a].shape

    any_spec = pl.BlockSpec(memory_space=pl.ANY)
    return _pcall(body, name=name, in_specs=[any_spec] * n, out_specs=[any_spec] * n,
                  out_shape=[_sds((N_DEV, *piece_shape(a)), srcs[a].dtype) for a in range(n)],
                  scratch_shapes=[pltpu.SemaphoreType.DMA((n, N_DEV - 1)), pltpu.SemaphoreType.DMA((n, N_DEV - 1)),
                                  pltpu.SemaphoreType.DMA((n,))])(*srcs)


_HBM = pl.BlockSpec(memory_space=pltpu.HBM)
_SEM = pl.BlockSpec(memory_space=pltpu.SEMAPHORE)
_EFFECT = pltpu.SideEffectType.DATAFLOW_SIDE_EFFECTING


def _peer_ids():
    x, y, c = lax.axis_index("x"), lax.axis_index("y"), lax.axis_index("c")
    peers = [((1 - x) if r & 4 else x, (1 - y) if r & 2 else y, (1 - c) if r & 1 else c) for r in range(1, N_DEV)]
    return 4 * x + 2 * y + c, peers, [4 * p[0] + 2 * p[1] + p[2] for p in peers]


def _exchange_start(name, src, scatter, after):
    piece_shape = src.shape[1:] if scatter else src.shape

    def body(src_ref, land_ref, after_ref, send_sems, recv_sems, local_sem, src_thru, land_thru, token):
        me, peers, lins = _peer_ids()

        def piece(idx):
            return src_ref.at[idx] if scatter else src_ref

        pltpu.make_async_copy(piece(me), land_ref.at[me], local_sem).start()
        for r, (p, lp) in enumerate(zip(peers, lins)):
            pltpu.make_async_remote_copy(src_ref=piece(lp), dst_ref=land_ref.at[me], send_sem=send_sems.at[r],
                                         recv_sem=recv_sems.at[r], device_id=p, device_id_type=MESH).start()
        token[...] = jnp.zeros_like(token)

    land = pltpu.with_memory_space_constraint(lax.empty((N_DEV, *piece_shape), src.dtype), pltpu.HBM)
    send_sems, recv_sems, local_sem, src_thru, land_thru, token = pl.pallas_call(
        body, name=name,
        out_shape=(pltpu.SemaphoreType.DMA((N_DEV - 1,)), pltpu.SemaphoreType.DMA((N_DEV - 1,)),
                   pltpu.SemaphoreType.DMA(()), pltpu.HBM(src.shape, src.dtype),
                   pltpu.HBM((N_DEV, *piece_shape), src.dtype), _sds((8, LANE), f32)),
        in_specs=(_HBM, _HBM, pl.BlockSpec(memory_space=pl.ANY)),
        out_specs=(_SEM, _SEM, _SEM, _HBM, _HBM, pl.BlockSpec(memory_space=pltpu.VMEM)),
        input_output_aliases={0: 3, 1: 4},
        compiler_params=pltpu.CompilerParams(has_side_effects=_EFFECT),
    )(pltpu.with_memory_space_constraint(src, pltpu.HBM), land, after)
    return (send_sems, recv_sems, local_sem, src_thru, land_thru, scatter), token


def _exchange_wait(name, handle, after):
    send_sems, recv_sems, local_sem, src_thru, land_thru, scatter = handle

    def body(src_ref, land_ref, send_sems_, recv_sems_, local_sem_, after_ref, src_dead, got_ref):
        me, peers, lins = _peer_ids()

        def piece(idx):
            return src_ref.at[idx] if scatter else src_ref

        pltpu.make_async_copy(piece(me), land_ref.at[me], local_sem_).wait()
        for r, (p, lp) in enumerate(zip(peers, lins)):
            pltpu.make_async_remote_copy(src_ref=piece(lp), dst_ref=land_ref.at[me], send_sem=send_sems_.at[r],
                                         recv_sem=recv_sems_.at[r], device_id=p, device_id_type=MESH).wait_send()
            pltpu.make_async_remote_copy(src_ref=piece(lp), dst_ref=land_ref.at[lp], send_sem=send_sems_.at[r],
                                         recv_sem=recv_sems_.at[r], device_id=p, device_id_type=MESH).wait_recv()

    return pl.pallas_call(
        body, name=name,
        out_shape=(pltpu.HBM(src_thru.shape, src_thru.dtype), pltpu.HBM(land_thru.shape, land_thru.dtype)),
        in_specs=(_HBM, _HBM, _SEM, _SEM, _SEM, pl.BlockSpec(memory_space=pl.ANY)), out_specs=(_HBM, _HBM),
        input_output_aliases={0: 0, 1: 1},
        compiler_params=pltpu.CompilerParams(has_side_effects=_EFFECT),
    )(src_thru, land_thru, send_sems, recv_sems, local_sem, after)[1]


def _adamw(name, w, m, v, g_or_stack, stacked, rows=256):
    r, c = w.shape
    tr = _tile(r, rows)

    def fn(w_, m_, v_, g_):
        if stacked:
            g = g_[0].astype(f32)
            for j in range(1, N_DEV):
                g = g + g_[j].astype(f32)
        else:
            g = g_
        m_new = ADAM_B1 * m_ + (1.0 - ADAM_B1) * g
        v_new = ADAM_B2 * v_ + (1.0 - ADAM_B2) * (g * g)
        m_hat = m_new / (1.0 - ADAM_B1 ** ADAM_STEP)
        v_hat = v_new / (1.0 - ADAM_B2 ** ADAM_STEP)
        delta = -ADAM_LR * (m_hat / (jnp.sqrt(v_hat) + ADAM_EPS) + ADAM_WD * w_)
        return g, delta, m_new, v_new

    blk = _row(tr, c)
    g_spec = pl.BlockSpec((N_DEV, tr, c), lambda i: (0, i, 0)) if stacked else blk
    return _rowwise(name, fn, r // tr, [w, m, v, g_or_stack], [blk, blk, blk, g_spec],
                    [_sds((r, c), f32)] * 4, [blk] * 4, [False] * 4)


def _ada_fwd(c_all, w_shard, b_shard):
    nb_, d = c_all.shape
    n = w_shard.shape[1]
    tn = _tile(n, 512)

    def body(c_ref, w_ref, b_ref, o_ref):
        a = jax.nn.silu(c_ref[...]).astype(bf16)
        o_ref[...] = jnp.dot(a, w_ref[...].astype(bf16), preferred_element_type=f32) + b_ref[...]

    return _pcall(body, name="ada_fwd", grid=(n // tn,),
                  in_specs=[pl.BlockSpec((nb_, d), lambda j: (0, 0)), pl.BlockSpec((d, tn), lambda j: (0, j)),
                            pl.BlockSpec((1, tn), lambda j: (0, j))],
                  out_specs=pl.BlockSpec((nb_, tn), lambda j: (0, j)), out_shape=_sds((nb_, n), f32),
                  sem=("parallel",))(c_all, w_shard, b_shard)


def _ada_bwd(c_all, dmod_cols):
    nb_, d = c_all.shape
    n = dmod_cols.shape[1]
    tn = _tile(n, 512)

    def body(c_ref, g_ref, o_ref):
        a = jax.nn.silu(c_ref[...]).astype(bf16).astype(f32)
        g = g_ref[...].astype(bf16).astype(f32)
        o_ref[...] = lax.dot_general(a, g, (TN, ((), ())), precision=HI, preferred_element_type=f32)

    return _pcall(body, name="ada_bwd", grid=(n // tn,),
                  in_specs=[pl.BlockSpec((nb_, d), lambda j: (0, 0)), pl.BlockSpec((nb_, tn), lambda j: (0, j))],
                  out_specs=pl.BlockSpec((d, tn), lambda j: (0, j)), out_shape=_sds((d, n), f32),
                  sem=("parallel",))(c_all, dmod_cols)


SMALL_LATE = ("b_ada", "norm1_g", "q_norm_g", "k_norm_g")
SMALL_EARLY = ("lam_re", "lam_im", "log_step", "b_re", "b_im", "c_re", "c_im", "d_skip", "b_glu", "attn_out_g",
               "ssm_out_g", "norm2_g")
ORDER = ("w_ada", "b_ada", "norm1_g", "w_in", "q_norm_g", "k_norm_g", "lam_re", "lam_im", "log_step", "b_re", "b_im",
         "c_re", "c_im", "d_skip", "w_glu", "b_glu", "attn_out_g", "ssm_out_g", "w_out", "norm2_g", "w_ff1", "w_ff2")


def _pack(parts):
    flat = jnp.concatenate([p.reshape(-1) for p in parts])
    pad = (-flat.shape[0]) % (8 * LANE)
    return jnp.pad(flat, (0, pad)).reshape(-1, LANE)


def kernel(x, c, w_ada, b_ada, norm1_g, w_in, q_norm_g, k_norm_g, lam_re, lam_im, log_step, b_re, b_im, c_re, c_im, d_skip, w_glu, b_glu, attn_out_g, ssm_out_g, w_out, norm2_g, w_ff1, w_ff2, loss_target, m_w_ada, m_b_ada, m_norm1_g, m_w_in, m_q_norm_g, m_k_norm_g, m_lam_re, m_lam_im, m_log_step, m_b_re, m_b_im, m_c_re, m_c_im, m_d_skip, m_w_glu, m_b_glu, m_attn_out_g, m_ssm_out_g, m_w_out, m_norm2_g, m_w_ff1, m_w_ff2, v_w_ada, v_b_ada, v_norm1_g, v_w_in, v_q_norm_g, v_k_norm_g, v_lam_re, v_lam_im, v_log_step, v_b_re, v_b_im, v_c_re, v_c_im, v_d_skip, v_w_glu, v_b_glu, v_attn_out_g, v_ssm_out_g, v_w_out, v_norm2_g, v_w_ff1, v_w_ff2):
    env = dict(locals())
    wts = {n: env[n] for n in ORDER}
    mom = {n: env["m_" + n] for n in ORDER}
    var = {n: env["v_" + n] for n in ORDER}

    xs, tgt = x[0], loss_target[0]
    s_len, d = xs.shape
    aw = d // 2
    sw = d - aw
    n_heads = aw // HEAD
    n_groups = sw // SSM_GROUP
    n_state = lam_re.shape[-1]
    ngs = sw // LANE
    gp = n_groups * n_state
    sb = GROUPS_PER_BLOCK * n_state
    tm = _tile(s_len, 256)
    steps = s_len // tm
    me = 4 * lax.axis_index("x") + 2 * lax.axis_index("y") + lax.axis_index("c")

    (c_all,) = _exchange("gather_c", [c], [False])
    c_all = c_all.reshape(N_DEV, d)

    n_ada = w_ada.shape[-1]
    b_ada_cols = lax.dynamic_slice_in_dim(b_ada, me * n_ada, n_ada, axis=1)
    mod_part = _ada_fwd(c_all, w_ada[0], b_ada_cols)
    (mod_all,) = _exchange("gather_mod", [mod_part], [False])
    mod = lax.dynamic_index_in_dim(mod_all, me, axis=1, keepdims=False).reshape(1, 6 * d)
    sh1, sc1, g1, sh2, sc2, g2 = (mod[:, i * d:(i + 1) * d] for i in range(6))

    gather, started = {}, jnp.zeros((1, 1), f32)
    for name in ("w_in", "w_glu", "w_out", "w_ff1", "w_ff2"):
        gather[name], token = _exchange_start("gather_" + name, wts[name][0].astype(bf16), False, mod_all)
        started = started + token[0:1, 0:1]
    sc1 = sc1 + started

    (h,) = _rowwise("norm1", _norm_mod, steps, [xs, norm1_g, sc1, sh1],
                    [_row(tm, d), _vec(d), _vec(d), _vec(d)], [_sds((s_len, d), bf16)], [_row(tm, d)], [False])
    win_g = _exchange_wait("gathered_w_in", gather["w_in"], h)
    (proj,) = _mm_nn_sharded("in_proj", h, win_g)

    def qk_fn(q, k, gq, gk):
        return _head_rms(q, gq), _head_rms(k, gk)

    qn, kn = _rowwise("qk_norm", qk_fn, steps, [proj, proj, q_norm_g, k_norm_g],
                      [_row(tm, aw, 0), _row(tm, aw, 1), _vec(HEAD), _vec(HEAD)],
                      [_sds((s_len, aw), f32)] * 2, [_row(tm, aw)] * 2, [False] * 2)
    v_blk = 2 * aw // HEAD

    slopes = _slope_table(n_heads)
    pat = [_attn_fwd(qn, kn, proj, v_blk, dil, slopes) for _, dil in DILATIONS]

    def attn_mix_fn(o1, l1, o2, l2, o3, l3):
        m = jnp.maximum(jnp.maximum(l1, l2), l3)
        e1, e2, e3 = jnp.exp(l1 - m), jnp.exp(l2 - m), jnp.exp(l3 - m)
        tot = e1 + e2 + e3
        return (e1 * o1 + e2 * o2 + e3 * o3) / tot, m + jnp.log(tot)

    attn, lse = _rowwise("attn_mix", attn_mix_fn, steps, [t for ol in pat for t in ol], [_row(tm, aw)] * 6,
                         [_sds((s_len, aw), f32)] * 2, [_row(tm, aw)] * 2, [False] * 2)

    lam_re2, lam_im2 = lam_re[0], lam_im[0]
    log_step2 = log_step[0].reshape(n_groups, 1)
    b_re2 = b_re[0].reshape(n_groups, n_state * SSM_GROUP)
    b_im2 = b_im[0].reshape(n_groups, n_state * SSM_GROUP)
    expand = jnp.repeat(jnp.eye(n_state, dtype=f32), SSM_GROUP, axis=1)
    a_re, a_im, bb_re2, bb_im2 = _ssm_params(lam_re2, lam_im2, log_step2, b_re2, b_im2, expand)
    a2 = jnp.zeros((8, gp), f32).at[0].set(a_re.reshape(gp)).at[1].set(a_im.reshape(gp))

    def by_block(t):
        return t.reshape(ngs, GROUPS_PER_BLOCK, *t.shape[1:])

    bb_re4 = by_block(bb_re2.reshape(n_groups, n_state, SSM_GROUP))
    bb_im4 = by_block(bb_im2.reshape(n_groups, n_state, SSM_GROUP))
    c_re4, c_im4 = by_block(c_re[0]), by_block(c_im[0])
    w_y_re = _bd_weight(c_re4.transpose(0, 1, 3, 2))
    w_y_im = _bd_weight(-c_im4.transpose(0, 1, 3, 2))
    w_du_re, w_du_im = _bd_weight(bb_re4), _bd_weight(bb_im4)
    w_bu_re = _lane_block_weights(bb_re2.reshape(n_groups, n_state, SSM_GROUP).transpose(0, 2, 1), n_state)
    w_bu_im = _lane_block_weights(bb_im2.reshape(n_groups, n_state, SSM_GROUP).transpose(0, 2, 1), n_state)
    w_g_re, w_g_im = _lane_block_weights(c_re[0], n_state), _lane_block_weights(-c_im[0], n_state)

    nseg = _scan_segments(s_len)
    u_seg = _to_segments(proj[:, 3 * aw:], nseg).astype(bf16)
    h_re, h_im, hin_f = _scan("ssm_scan", u_seg, w_bu_re, w_bu_im, a2, reverse=False)
    ymm = _from_segments(_bd_nn("ssm_y", [h_re, h_im], [w_y_re, w_y_im], sb, LANE), nseg)

    u_spec = _row(tm, sw, 3 * aw // sw)
    (yg,) = _rowwise("ssm_gelu", _ypre_fn, steps, [ymm, proj, d_skip], [_row(tm, sw), u_spec, _vec(sw)],
                     [_sds((s_len, sw), f32)], [_row(tm, sw)], [False])
    wglu_g = _exchange_wait("gathered_w_glu", gather["w_glu"], yg).reshape(sw, sw)
    (z,) = _mm_nn("glu_proj", yg, wglu_g)
    (cat,) = _rowwise("mix_norm", _mix_fn, steps, [attn, yg, z, b_glu, attn_out_g, ssm_out_g],
                      [_row(tm, aw), _row(tm, sw), _row(tm, sw), _vec(sw), _vec(aw), _vec(sw)],
                      [_sds((s_len, d), bf16)], [_row(tm, d)], [False])
    wout_g = _exchange_wait("gathered_w_out", gather["w_out"], cat).reshape(d, d)
    (mixed,) = _mm_nn("out_proj", cat, wout_g)

    def res_norm2_fn(x_, mixed_, g1_, gn, sc, sh):
        x1_ = x_ + g1_ * mixed_
        return x1_, _norm_mod(x1_, gn, sc, sh)

    x1, h2 = _rowwise("norm2", res_norm2_fn, steps, [xs, mixed, g1, norm2_g, sc2, sh2],
                      [_row(tm, d), _row(tm, d)] + [_vec(d)] * 4,
                      [_sds((s_len, d), f32), _sds((s_len, d), bf16)], [_row(tm, d)] * 2, [False] * 2)

    def act_epilogue(acc):
        r = jnp.maximum(acc, 0.0)
        return r, r * r

    wff1_g = _exchange_wait("gathered_w_ff1", gather["w_ff1"], h2)
    r_ff, act = _mm_nn_sharded("ff1", h2, wff1_g, epilogue=act_epilogue,
                               outs=[_sds((s_len, 4 * d), bf16), _sds((s_len, 4 * d), bf16)])
    wff2_g = _exchange_wait("gathered_w_ff2", gather["w_ff2"], act).reshape(4 * d, d)
    (ff,) = _mm_nn("ff2", act, wff2_g)

    def loss_fn(x1_, ff_, tgt_, g2_):
        e = x1_ + g2_ * ff_ - tgt_
        dy_ = e * (1.0 / d)
        part = jnp.full((1, LANE), 0.5 / d, f32) * jnp.sum(e * e)
        return dy_, g2_ * dy_, part, jnp.sum(dy_ * ff_, axis=0, keepdims=True)

    dy, dff, loss_part, d_g2 = _rowwise(
        "loss", loss_fn, steps, [x1, ff, tgt, g2], [_row(tm, d)] * 3 + [_vec(d)],
        [_sds((s_len, d), f32), _sds((s_len, d), bf16), _sds((1, LANE), f32), _sds((1, d), f32)],
        [_row(tm, d), _row(tm, d), _vec(LANE), _vec(d)], [False, False, True, True])
    loss = lax.psum(loss_part[0, 0], ("x", "y", "c"))

    def dact_epilogue(acc, r_):
        return (acc * (2.0 * r_.astype(f32)),)

    (da,) = _mm_nt("ff2_dx", dff, wff2_g, epilogue=dact_epilogue, extra=[r_ff], outs=[_sds((s_len, 4 * d), bf16)])
    scatter = {}
    g_wff2 = _mm_tn("ff2_dw", act, dff).reshape(N_DEV, 4 * d // N_DEV, d)
    scatter["w_ff2"], tok_ff2 = _exchange_start("scatter_w_ff2", g_wff2, True, started)
    dh2 = _mm_nt_sharded("ff1_dx", da, wff1_g)
    g_wff1 = _mm_tn_sharded("ff1_dw", h2, da, N_DEV)
    scatter["w_ff1"], tok_ff1 = _exchange_start("scatter_w_ff1", g_wff1, True, started)
    norm2_g_t = norm2_g + (tok_ff2[0:1, 0:1] + tok_ff1[0:1, 0:1])

    def norm2_bwd_fn(dh2_, x1_, dy_, mixed_, gn, sc, sh, g1_):
        _, vjp = jax.vjp(_norm_mod, x1_, gn, sc, sh)
        dx, dgn, dsc, dsh = vjp(dh2_)
        dx1_ = dy_ + dx
        return dx1_, g1_ * dx1_, dgn, dsc, dsh, jnp.sum(dx1_ * mixed_, axis=0, keepdims=True)

    dx1, dmixed, d_norm2_g, d_sc2, d_sh2, d_g1 = _rowwise(
        "norm2_bwd", norm2_bwd_fn, steps, [dh2, x1, dy, mixed, norm2_g_t, sc2, sh2, g1],
        [_row(tm, d)] * 4 + [_vec(d)] * 4,
        [_sds((s_len, d), f32), _sds((s_len, d), bf16)] + [_sds((1, d), f32)] * 4,
        [_row(tm, d)] * 2 + [_vec(d)] * 4, [False, False, True, True, True, True])

    (dcat,) = _mm_nt("out_dx", dmixed, wout_g)
    g_wout = _mm_tn("out_dw", cat, dmixed).reshape(N_DEV, d // N_DEV, d)
    scatter["w_out"], tok_out = _exchange_start("scatter_w_out", g_wout, True, started)
    b_glu_t = b_glu + tok_out[0:1, 0:1]

    def mix_bwd_fn(dcat_, attn_, yg_, z_, bglu, ga, gs):
        _, vjp = jax.vjp(_mix_fn, attn_, yg_, z_, bglu, ga, gs)
        dattn_, dyg_, dz_, dbglu, dga, dgs = vjp(dcat_)
        prod = dattn_ * attn_
        dd_ = jnp.concatenate([jnp.broadcast_to(jnp.sum(prod[:, i * HEAD:(i + 1) * HEAD], axis=1, keepdims=True),
                                                (prod.shape[0], HEAD)) for i in range(n_heads)], axis=1)
        return dattn_, dd_, dyg_, dz_, dbglu, dga, dgs

    dattn, dd, dyg1, dz, d_b_glu, d_attn_out_g, d_ssm_out_g = _rowwise(
        "mix_bwd", mix_bwd_fn, steps, [dcat, attn, yg, z, b_glu_t, attn_out_g, ssm_out_g],
        [_row(tm, d), _row(tm, aw), _row(tm, sw), _row(tm, sw), _vec(sw), _vec(aw), _vec(sw)],
        [_sds((s_len, aw), f32), _sds((s_len, aw), f32), _sds((s_len, sw), f32), _sds((s_len, sw), bf16),
         _sds((1, sw), f32), _sds((1, aw), f32), _sds((1, sw), f32)],
        [_row(tm, aw), _row(tm, aw), _row(tm, sw), _row(tm, sw), _vec(sw), _vec(aw), _vec(sw)],
        [False] * 4 + [True] * 3)

    (dyg2,) = _mm_nt("glu_dx", dz, wglu_g)
    g_wglu = _mm_tn("glu_dw", yg, dz).reshape(N_DEV, sw // N_DEV, sw)
    scatter["w_glu"], tok_glu = _exchange_start("scatter_w_glu", g_wglu, True, started)
    d_skip_t = d_skip + tok_glu[0:1, 0:1]

    def gelu_bwd_fn(dyg1_, dyg2_, ymm_, u_, dskip):
        _, vjp = jax.vjp(_ypre_fn, ymm_, u_, dskip)
        dymm, du_, ddskip = vjp(dyg1_ + dyg2_)
        return dymm, du_, ddskip

    dymm, du_skip, d_d_skip = _rowwise(
        "ssm_gelu_bwd", gelu_bwd_fn, steps, [dyg1, dyg2, ymm, proj, d_skip_t],
        [_row(tm, sw)] * 3 + [u_spec, _vec(sw)],
        [_sds((s_len, sw), f32), _sds((s_len, sw), f32), _sds((1, sw), f32)],
        [_row(tm, sw), _row(tm, sw), _vec(sw)], [False, False, True])

    dymm_seg = _to_segments(dymm, nseg).astype(bf16)
    lr, li, _, da_seg = _scan("ssm_adj", dymm_seg, w_g_re, w_g_im, a2, reverse=True, da_from=(h_re, h_im, hin_f))
    du_ssm = _from_segments(_bd_nn("ssm_du", [lr, li], [w_du_re, w_du_im], sb, LANE), nseg)
    dc_re_c, dc_im_c = _bd_tn("ssm_dc", dymm_seg, [h_re, h_im], LANE, sb)
    dbb_re_c, dbb_im_c = _bd_tn("ssm_dbbar", u_seg, [lr, li], LANE, sb)

    def diag_to_gpi(w):
        return _bd_diag(w, SSM_GROUP, n_state).transpose(0, 1, 3, 2).reshape(n_groups, n_state * SSM_GROUP)

    d_lam_re, d_lam_im, d_log_step, d_b_re2, d_b_im2 = _ssm_params_bwd(
        lam_re2, lam_im2, log_step2, b_re2, b_im2, expand,
        da_seg[0, 0].reshape(n_groups, n_state), da_seg[1, 0].reshape(n_groups, n_state),
        diag_to_gpi(dbb_re_c), diag_to_gpi(dbb_im_c))
    d_c_re = _bd_diag(dc_re_c, SSM_GROUP, n_state).reshape(n_groups, SSM_GROUP, n_state)
    d_c_im = -_bd_diag(dc_im_c, SSM_GROUP, n_state).reshape(n_groups, SSM_GROUP, n_state)

    grads_qkv = [_attn_bwd(qn, kn, proj, v_blk, dattn, lse, dd, dil, slopes) for _, dil in DILATIONS]

    def qkv_bwd_fn(q, k, gq, gk, dq1, dq2, dq3, dk1, dk2, dk3, dv1, dv2, dv3, du1, du2):
        _, vjp = jax.vjp(lambda q_, k_, gq_, gk_: (_head_rms(q_, gq_), _head_rms(k_, gk_)), q, k, gq, gk)
        dq, dk, dgq, dgk = vjp((dq1 + dq2 + dq3, dk1 + dk2 + dk3))
        return jnp.concatenate([dq, dk, dv1 + dv2 + dv3, du1 + du2], axis=1), dgq, dgk

    small_g = {"lam_re": d_lam_re, "lam_im": d_lam_im, "log_step": d_log_step, "b_re": d_b_re2, "b_im": d_b_im2,
               "c_re": d_c_re, "c_im": d_c_im, "d_skip": d_d_skip, "b_glu": d_b_glu,
               "attn_out_g": d_attn_out_g, "ssm_out_g": d_ssm_out_g, "norm2_g": d_norm2_g}
    early, tok_early = _exchange_start("gather_early_grads", _pack([small_g[n] for n in SMALL_EARLY]), False, started)

    qkv_cots = [grads_qkv[p][i] for i in range(3) for p in range(3)]
    dproj, small_g["q_norm_g"], small_g["k_norm_g"] = _rowwise(
        "qk_norm_bwd", qkv_bwd_fn, steps,
        [proj, proj, q_norm_g + tok_early[0:1, 0:1], k_norm_g, *qkv_cots, du_skip, du_ssm],
        [_row(tm, aw, 0), _row(tm, aw, 1), _vec(HEAD), _vec(HEAD)] + [_row(tm, aw)] * 9 + [_row(tm, sw)] * 2,
        [_sds((s_len, 3 * aw + sw), bf16), _sds((1, HEAD), f32), _sds((1, HEAD), f32)],
        [_row(tm, 3 * aw + sw), _vec(HEAD), _vec(HEAD)], [False, True, True])

    g_win = _mm_tn_sharded("in_dw", h, dproj, N_DEV)
    scatter["w_in"], tok_in = _exchange_start("scatter_w_in", g_win, True, tok_early)
    dh = _mm_nt_sharded("in_dx", dproj, win_g, after=tok_in)
    norm1_g_t = norm1_g + tok_in[0:1, 0:1]

    def norm1_bwd_fn(dh_, x_, dx1_, gn, sc, sh):
        _, vjp = jax.vjp(_norm_mod, x_, gn, sc, sh)
        dx, dgn, dsc, dsh = vjp(dh_)
        return dx1_ + dx, dgn, dsc, dsh

    grad_x, d_norm1_g, d_sc1, d_sh1 = _rowwise(
        "norm1_bwd", norm1_bwd_fn, steps, [dh, xs, dx1, norm1_g_t, sc1, sh1], [_row(tm, d)] * 3 + [_vec(d)] * 3,
        [_sds((s_len, d), f32)] + [_sds((1, d), f32)] * 3, [_row(tm, d)] + [_vec(d)] * 3,
        [False, True, True, True])

    small_g["b_ada"] = jnp.concatenate([d_sh1, d_sc1, d_g1, d_sh2, d_sc2, d_g2], axis=1)
    small_g["norm1_g"] = d_norm1_g
    (r_late,) = _exchange("gather_late_grads", [_pack([small_g[n] for n in SMALL_LATE])], [False])

    res = {}
    dmod_all = r_late.reshape(N_DEV, -1)[:, :6 * d]
    g_wada = _ada_bwd(c_all, lax.dynamic_slice_in_dim(dmod_all, me * n_ada, n_ada, axis=1))
    res["w_ada"] = _adamw("adamw_w_ada", w_ada[0], m_w_ada[0], v_w_ada[0], g_wada, False)
    after = res["w_ada"][1]
    for name in ("w_ff2", "w_ff1", "w_out", "w_glu", "w_in"):
        stack = _exchange_wait("scattered_" + name, scatter[name], after)
        res[name] = _adamw("adamw_" + name, wts[name][0], mom[name][0], var[name][0], stack, True)
        after = res[name][1]
    r_early = _exchange_wait("gathered_early_grads", early, after)
    for label, names, stack in (("late", SMALL_LATE, r_late), ("early", SMALL_EARLY, r_early)):
        small_res = _adamw("adamw_small_" + label, _pack([wts[n] for n in names]), _pack([mom[n] for n in names]),
                           _pack([var[n] for n in names]), stack, True, rows=4096)
        off = 0
        for n in names:
            size = wts[n].size
            res[n] = [t.reshape(-1)[off:off + size] for t in small_res]
            off += size

    out = [loss, grad_x[None]]
    for i in range(4):
        out += [res[n][i].reshape(wts[n].shape) for n in ORDER]
    return tuple(out)
```

```python
import math

import jax
import jax.numpy as jnp
from jax import lax
from jax.experimental import pallas as pl
from jax.experimental.pallas import tpu as pltpu

f32, bf16 = jnp.float32, jnp.bfloat16

N_DEV = 8
LANE = 128
HEAD = 128
SSM_GROUP = 16
DILATIONS = ((128, 1), (512, 4), (2048, 16))
BAND = 128
EPS = 1e-6
ADAM_LR, ADAM_B1, ADAM_B2, ADAM_EPS, ADAM_WD, ADAM_STEP = 0.001, 0.9, 0.999, 1e-08, 0.01, 10
NEG = -1e30
VMEM_LIMIT = 60 * 1024 * 1024
HI = lax.Precision.HIGHEST
MESH = pl.DeviceIdType.MESH


def _pcall(body, **kw):
    sem = kw.pop("sem", None)
    kw["compiler_params"] = pltpu.CompilerParams(dimension_semantics=sem, vmem_limit_bytes=VMEM_LIMIT)
    return pl.pallas_call(body, **kw)


def _tile(n, pref):
    t = min(n, pref)
    while n % t:
        t //= 2
    return t


def _sds(shape, dtype):
    return jax.ShapeDtypeStruct(shape, dtype)


def _rowwise(name, fn, steps, ins, in_specs, outs, out_specs, acc):
    n_in = len(ins)

    def body(*refs):
        res = fn(*[r[...] for r in refs[:n_in]])
        res = res if isinstance(res, (tuple, list)) else (res,)
        for r, o, a in zip(refs[n_in:], res, acc):
            if a:
                @pl.when(pl.program_id(0) == 0)
                def _():
                    r[...] = jnp.zeros_like(r)
                r[...] += o
            else:
                r[...] = o.astype(r.dtype)

    return _pcall(body, name=name, grid=(steps,), in_specs=in_specs, out_specs=out_specs, out_shape=outs,
                  sem=("arbitrary",))(*ins)


def _row(tm, c, blk=0):
    return pl.BlockSpec((tm, c), lambda i: (i, blk))


def _vec(c, blk=0):
    return pl.BlockSpec((1, c), lambda i: (0, blk))


def _rms(x, g):
    return x * lax.rsqrt(jnp.mean(x * x, axis=-1, keepdims=True) + EPS) * g


def _norm_mod(x, g, sc, sh):
    return _rms(x, g) * (1.0 + sc) + sh


def _head_rms(t, g):
    return jnp.concatenate([_rms(t[:, h * HEAD:(h + 1) * HEAD], g) for h in range(t.shape[1] // HEAD)], axis=1)


def _mix_fn(attn, yg, z, bglu, ga, gs):
    ssm = yg * jax.nn.sigmoid(z + bglu)
    return jnp.concatenate([_rms(attn, ga), _rms(ssm, gs)], axis=1)


def _ypre_fn(ymm, u, dskip):
    return jax.nn.gelu(ymm + dskip * u)


def _matmul(name, a, b, *, dims, grid, a_spec, b_spec, acc_shape, outs, out_specs, extra=(), extra_specs=(),
            epilogue=None, after=None):
    gk = grid[2]
    n_x = len(extra)
    placed = [] if after is None else [after]
    first_out = n_x + len(placed)
    ins = [a, b, *extra, *placed]
    in_specs = [a_spec, b_spec, *extra_specs] + [pl.BlockSpec(memory_space=pl.ANY)] * len(placed)

    def product(a_ref, b_ref):
        return lax.dot_general(a_ref[...].astype(bf16), b_ref[...].astype(bf16), (dims, ((), ())),
                               preferred_element_type=f32)

    def finish(res, x_refs, o_refs):
        res = epilogue(res, *[r[...] for r in x_refs]) if epilogue is not None else (res,)
        for r, o in zip(o_refs, res):
            r[...] = o.astype(r.dtype)

    def body_single(a_ref, b_ref, *rest):
        finish(product(a_ref, b_ref), rest[:n_x], rest[first_out:])

    def body_pair(a_ref, b_ref, *rest):
        acc = rest[-1]
        prod = product(a_ref, b_ref)

        @pl.when(pl.program_id(2) == 0)
        def _():
            acc[...] = prod

        @pl.when(pl.program_id(2) == 1)
        def _():
            finish(acc[...] + prod, rest[:n_x], rest[first_out:-1])

    def body(a_ref, b_ref, *rest):
        acc = rest[-1]
        k = pl.program_id(2)

        @pl.when(k == 0)
        def _():
            acc[...] = jnp.zeros_like(acc)

        acc[...] += product(a_ref, b_ref)

        @pl.when(k == gk - 1)
        def _():
            finish(acc[...], rest[:n_x], rest[first_out:-1])

    sem = ("parallel", "parallel", "arbitrary")
    if gk == 1:
        return _pcall(body_single, name=name, grid=grid, in_specs=in_specs, out_specs=out_specs, out_shape=outs,
                      sem=sem)(*ins)
    return _pcall(body_pair if gk == 2 else body, name=name, grid=grid, in_specs=in_specs, out_specs=out_specs,
                  out_shape=outs, scratch_shapes=[pltpu.VMEM(acc_shape, f32)], sem=sem)(*ins)


NN = ((1,), (0,))
NT = ((1,), (1,))
TN = ((0,), (0,))


def _mm_nn(name, a, b, out_dtype=f32, tm=1024, tn=1024, tk=2048, epilogue=None, extra=(), outs=None):
    m, kd = a.shape
    n = b.shape[1]
    tm, tn, tk = _tile(m, tm), _tile(n, tn), _tile(kd, tk)
    o_spec = pl.BlockSpec((tm, tn), lambda i, j, k: (i, j))
    outs = outs if outs is not None else [_sds((m, n), out_dtype)]
    return _matmul(name, a, b, dims=NN, grid=(m // tm, n // tn, kd // tk),
                   a_spec=pl.BlockSpec((tm, tk), lambda i, j, k: (i, k)),
                   b_spec=pl.BlockSpec((tk, tn), lambda i, j, k: (k, j)),
                   acc_shape=(tm, tn), outs=outs, out_specs=[o_spec] * len(outs),
                   extra=extra, extra_specs=[o_spec] * len(extra), epilogue=epilogue)


def _mm_nn_sharded(name, a, b3, out_dtype=f32, tm=1024, tk=2048, epilogue=None, outs=None):
    m, kd = a.shape
    nsh, _, n = b3.shape
    tm, tk = _tile(m, tm), _tile(kd, tk)
    o_spec = pl.BlockSpec((tm, n), lambda i, j, k: (i, j))
    outs = outs if outs is not None else [_sds((m, nsh * n), out_dtype)]
    return _matmul(name, a, b3, dims=NN, grid=(m // tm, nsh, kd // tk),
                   a_spec=pl.BlockSpec((tm, tk), lambda i, j, k: (i, k)),
                   b_spec=pl.BlockSpec((None, tk, n), lambda i, j, k: (j, k, 0)),
                   acc_shape=(tm, n), outs=outs, out_specs=[o_spec] * len(outs), epilogue=epilogue)


def _mm_nt(name, a, b, out_dtype=f32, tm=1024, tn=1024, tk=2048, epilogue=None, extra=(), outs=None):
    m, kd = a.shape
    n = b.shape[0]
    tm, tn, tk = _tile(m, tm), _tile(n, tn), _tile(kd, tk)
    o_spec = pl.BlockSpec((tm, tn), lambda i, j, k: (i, j))
    outs = outs if outs is not None else [_sds((m, n), out_dtype)]
    return _matmul(name, a, b, dims=NT, grid=(m // tm, n // tn, kd // tk),
                   a_spec=pl.BlockSpec((tm, tk), lambda i, j, k: (i, k)),
                   b_spec=pl.BlockSpec((tn, tk), lambda i, j, k: (j, k)),
                   acc_shape=(tm, tn), outs=outs, out_specs=[o_spec] * len(outs),
                   extra=extra, extra_specs=[o_spec] * len(extra), epilogue=epilogue)


def _mm_nt_sharded(name, a, b3, out_dtype=f32, tm=512, tn=2048, after=None):
    m = a.shape[0]
    nsh, n_out, n = b3.shape
    tm, tn = _tile(m, tm), _tile(n_out, tn)
    return _matmul(name, a, b3, dims=NT, grid=(m // tm, n_out // tn, nsh),
                   a_spec=pl.BlockSpec((tm, n), lambda i, j, k: (i, k)),
                   b_spec=pl.BlockSpec((None, tn, n), lambda i, j, k: (k, j, 0)),
                   acc_shape=(tm, tn), outs=[_sds((m, n_out), out_dtype)],
                   out_specs=[pl.BlockSpec((tm, tn), lambda i, j, k: (i, j))], after=after)[0]


def _mm_tn(name, a, b, out_dtype=bf16, tm=1024, tn=1024, tk=2048):
    t, m = a.shape
    n = b.shape[1]
    tm, tn, tk = _tile(m, tm), _tile(n, tn), _tile(t, tk)
    return _matmul(name, a, b, dims=TN, grid=(m // tm, n // tn, t // tk),
                   a_spec=pl.BlockSpec((tk, tm), lambda i, j, k: (k, i)),
                   b_spec=pl.BlockSpec((tk, tn), lambda i, j, k: (k, j)),
                   acc_shape=(tm, tn), outs=[_sds((m, n), out_dtype)],
                   out_specs=[pl.BlockSpec((tm, tn), lambda i, j, k: (i, j))])[0]


def _mm_tn_sharded(name, a, b, nsh, out_dtype=bf16, tm=1024, tk=2048):
    t, m = a.shape
    n = b.shape[1] // nsh
    tm, tk = _tile(m, tm), _tile(t, tk)
    return _matmul(name, a, b, dims=TN, grid=(m // tm, nsh, t // tk),
                   a_spec=pl.BlockSpec((tk, tm), lambda i, j, k: (k, i)),
                   b_spec=pl.BlockSpec((tk, n), lambda i, j, k: (k, j)),
                   acc_shape=(tm, n), outs=[_sds((nsh, m, n), out_dtype)],
                   out_specs=[pl.BlockSpec((None, tm, n), lambda i, j, k: (j, i, 0))])[0]


SCAN_CHAINS = 8


def _scan_segments(s_len):
    nch = SCAN_CHAINS
    while s_len % (8 * nch) or (s_len // (8 * nch)) & (s_len // (8 * nch) - 1):
        nch //= 2
    return 8 * nch


def _to_segments(t, nseg):
    s_len, c = t.shape
    return t.reshape(nseg, s_len // nseg, c).transpose(1, 0, 2).reshape(s_len, c)


def _from_segments(t, nseg):
    s_len, c = t.shape
    return t.reshape(s_len // nseg, nseg, c).transpose(1, 0, 2).reshape(s_len, c)


def _lane_block_weights(t3, n_state):
    n_groups = t3.shape[0]
    gpl = LANE // n_state
    per = LANE // (gpl * SSM_GROUP)
    n_lb = n_groups // gpl
    t5 = t3.reshape(n_lb // per, per, gpl, SSM_GROUP, n_state)
    w = jnp.einsum("aqgic,gh,qs->aqsgihc", t5, jnp.eye(gpl, dtype=t3.dtype), jnp.eye(per, dtype=t3.dtype))
    return w.reshape(n_lb, LANE, LANE).astype(bf16)


def _lane_block_diag(w, n_state):
    gpl = LANE // n_state
    per = LANE // (gpl * SSM_GROUP)
    n_lb = w.shape[0]
    w7 = w.reshape(n_lb // per, per, per, gpl, SSM_GROUP, gpl, n_state)
    t5 = jnp.einsum("aqsgihc,gh,qs->aqgic", w7, jnp.eye(gpl, dtype=w.dtype), jnp.eye(per, dtype=w.dtype))
    return t5.reshape(n_lb * gpl, SSM_GROUP, n_state)


def _scan(name, src, w_in, w_out, a2, *, reverse, adjoint_of=None):
    s_len, n_ch = src.shape
    gp = a2.shape[1]
    per = (gp // LANE) // (n_ch // LANE)
    nseg = _scan_segments(s_len)
    nch = nseg // 8
    seg = s_len // nseg
    n_sq = int(math.log2(seg))
    assert 2 ** n_sq == seg
    adj = adjoint_of is not None
    chunk = _tile(s_len, 1024)
    n_chunks = s_len // chunk

    def body(*refs):
        it = iter(refs)
        src_ref, wir_ref, wii_ref, wor_ref, woi_ref, a_ref = (next(it) for _ in range(6))
        if adj:
            u_ref, hr_ref, hi_ref, hin_ref = (next(it) for _ in range(4))
        res_ref = next(it)
        if adj:
            da_ref, dbr_ref, dbi_ref, dcr_ref, dci_ref = (next(it) for _ in range(5))
        else:
            or_ref, oi_ref, oin_ref = (next(it) for _ in range(3))
        if adj:
            or_ref, oi_ref = next(it), next(it)

        for i in range(n_chunks):
            part = src_ref[i * chunk:(i + 1) * chunk, :]
            or_ref[i * chunk:(i + 1) * chunk, :] = jnp.dot(part, wir_ref[...], preferred_element_type=f32)
            oi_ref[i * chunk:(i + 1) * chunk, :] = jnp.dot(part, wii_ref[...], preferred_element_type=f32)

        ar = a_ref[0:1, :]
        ai = -a_ref[1:2, :] if reverse else a_ref[1:2, :]
        arb, aib = jnp.broadcast_to(ar, (8, LANE)), jnp.broadcast_to(ai, (8, LANE))

        def rows(ch, k):
            return pl.ds(pl.multiple_of(k * nseg + ch * 8, 8), 8)

        def advance(h, ch, k):
            hr, hi = h
            return (arb * hr - aib * hi + or_ref[rows(ch, k), :], arb * hi + aib * hr + oi_ref[rows(ch, k), :])

        def kk(n):
            return seg - 1 - n if reverse else n

        zero = jnp.zeros((8, LANE), f32)

        def sweep1(n, hs):
            return tuple(advance(hs[ch], ch, kk(n)) for ch in range(nch))

        ends = lax.fori_loop(0, seg, sweep1, tuple((zero, zero) for _ in range(nch)))

        pr, pi = ar, ai
        for _ in range(n_sq):
            pr, pi = pr * pr - pi * pi, 2.0 * pr * pi
        in_r, in_i = [None] * nseg, [None] * nseg
        cr = ci = jnp.zeros((1, LANE), f32)
        for j in (range(nseg - 1, -1, -1) if reverse else range(nseg)):
            in_r[j], in_i[j] = cr, ci
            er, ei = ends[j // 8][0][j % 8:j % 8 + 1, :], ends[j // 8][1][j % 8:j % 8 + 1, :]
            cr, ci = er + pr * cr - pi * ci, ei + pr * ci + pi * cr
        h0 = tuple((jnp.concatenate(in_r[8 * ch:8 * ch + 8], axis=0), jnp.concatenate(in_i[8 * ch:8 * ch + 8], axis=0))
                   for ch in range(nch))
        if not adj:
            for ch in range(nch):
                oin_ref[0, 8 * ch:8 * ch + 8, :] = h0[ch][0]
                oin_ref[1, 8 * ch:8 * ch + 8, :] = h0[ch][1]

        def emit(ch, k, h):
            or_ref[rows(ch, k), :] = h[0]
            oi_ref[rows(ch, k), :] = h[1]

        def pair(h, p):
            return h[0] * p[0] + h[1] * p[1], h[1] * p[0] - h[0] * p[1]

        def sweep2(n, carry):
            k = kk(n)
            new = tuple(advance(carry[ch], ch, k) for ch in range(nch))
            for ch in range(nch):
                emit(ch, k, new[ch])
            if not adj:
                return new
            dr, di = carry[nch]
            for ch in range(nch):
                qr, qi = pair(new[ch], (hr_ref[rows(ch, k - 1), :], hi_ref[rows(ch, k - 1), :]))
                dr, di = dr + qr, di + qi
            return new + ((dr, di),)

        if adj:
            carry = lax.fori_loop(0, seg - 1, sweep2, h0 + ((zero, zero),))
            dr, di = carry[nch]
            for ch in range(nch):
                new = advance(carry[ch], ch, 0)
                emit(ch, 0, new)
                qr, qi = pair(new, (hin_ref[0, 8 * ch:8 * ch + 8, :], hin_ref[1, 8 * ch:8 * ch + 8, :]))
                dr, di = dr + qr, di + qi
            da_ref[0] = jnp.sum(dr, axis=0, keepdims=True)
            da_ref[1] = jnp.sum(di, axis=0, keepdims=True)
        else:
            lax.fori_loop(0, seg, sweep2, h0)

        first = pl.program_id(0) % per == 0
        for i in range(n_chunks):
            sl = slice(i * chunk, (i + 1) * chunk)
            part = lax.dot_general(or_ref[sl, :].astype(bf16), wor_ref[...], (NT, ((), ())), preferred_element_type=f32)
            part += lax.dot_general(oi_ref[sl, :].astype(bf16), woi_ref[...], (NT, ((), ())),
                                    preferred_element_type=f32)

            @pl.when(first)
            def _():
                res_ref[sl, :] = part

            @pl.when(jnp.logical_not(first))
            def _():
                res_ref[sl, :] += part

        if adj:
            def tn_all(x_ref, y_ref):
                tot = jnp.zeros((LANE, LANE), f32)
                for i in range(n_chunks):
                    sl = slice(i * chunk, (i + 1) * chunk)
                    tot += lax.dot_general(x_ref[sl, :].astype(bf16), y_ref[sl, :].astype(bf16), (TN, ((), ())),
                                           preferred_element_type=f32)
                return tot

            dbr_ref[...] = tn_all(u_ref, or_ref)
            dbi_ref[...] = tn_all(u_ref, oi_ref)
            dcr_ref[...] = tn_all(src_ref, hr_ref)
            dci_ref[...] = tn_all(src_ref, hi_ref)

    col = pl.BlockSpec((s_len, LANE), lambda l: (0, l))
    chan = pl.BlockSpec((s_len, LANE), lambda l: (0, l // per))
    in_spec = pl.BlockSpec((2, nseg, LANE), lambda l: (0, 0, l))
    w_spec = pl.BlockSpec((None, LANE, LANE), lambda l: (l, 0, 0))
    ins = [src, *w_in, *w_out, a2]
    in_specs = [chan, w_spec, w_spec, w_spec, w_spec, pl.BlockSpec((8, LANE), lambda l: (0, l))]
    outs, out_specs = [_sds((s_len, n_ch), f32)], [chan]
    scratch = []
    if adj:
        ins += list(adjoint_of)
        in_specs += [chan, col, col, in_spec]
        outs += [_sds((2, 1, gp), f32)] + [_sds((gp // LANE, LANE, LANE), f32)] * 4
        out_specs += [pl.BlockSpec((2, 1, LANE), lambda l: (0, 0, l))] + [w_spec] * 4
        scratch = [pltpu.VMEM((s_len, LANE), f32)] * 2
    else:
        outs += [_sds((s_len, gp), f32)] * 2 + [_sds((2, nseg, gp), f32)]
        out_specs += [col, col, in_spec]
    res = _pcall(body, name=name, grid=(gp // LANE,), in_specs=in_specs, out_specs=out_specs, out_shape=outs,
                 scratch_shapes=scratch, sem=("arbitrary",))(*ins)
    if adj:
        return res[0], res[1], (res[2], res[3]), (res[4], res[5])
    return res


def _ssm_param_fn(lam_re, lam_im, log_step, b_re2, b_im2, expand):
    step = jnp.exp(log_step)
    xr, xi = lam_re * step, lam_im * step
    mag = jnp.exp(xr)
    ar, ai = mag * jnp.cos(xi), mag * jnp.sin(xi)
    nr, ni = ar - 1.0, ai
    den = lam_re * lam_re + lam_im * lam_im
    cr = (nr * lam_re + ni * lam_im) / den
    ci = (ni * lam_re - nr * lam_im) / den
    cre = jnp.dot(cr, expand, precision=HI, preferred_element_type=f32)
    cie = jnp.dot(ci, expand, precision=HI, preferred_element_type=f32)
    return ar, ai, cre * b_re2 - cie * b_im2, cre * b_im2 + cie * b_re2


def _ssm_params(lam_re, lam_im, log_step, b_re2, b_im2, expand):
    def body(*refs):
        res = _ssm_param_fn(*[r[...] for r in refs[:6]])
        for r, o in zip(refs[6:], res):
            r[...] = o

    g, p = lam_re.shape
    return _pcall(body, name="ssm_params", out_shape=[_sds((g, p), f32)] * 2 + [_sds(b_re2.shape, f32)] * 2)(
        lam_re, lam_im, log_step, b_re2, b_im2, expand)


def _ssm_params_bwd(lam_re, lam_im, log_step, b_re2, b_im2, expand, d_ar, d_ai, d_bbr, d_bbi):
    def body(*refs):
        prim = [r[...] for r in refs[:5]]
        ex = refs[5][...]
        cot = tuple(r[...] for r in refs[6:10])
        _, vjp = jax.vjp(lambda *p_: _ssm_param_fn(*p_, ex), *prim)
        for r, o in zip(refs[10:], vjp(cot)):
            r[...] = o

    shapes = [lam_re.shape, lam_im.shape, log_step.shape, b_re2.shape, b_im2.shape]
    return _pcall(body, name="ssm_params_bwd", out_shape=[_sds(s, f32) for s in shapes])(
        lam_re, lam_im, log_step, b_re2, b_im2, expand, d_ar, d_ai, d_bbr, d_bbi)


def _slope_table(n_heads):
    s = 2.0 ** (-8.0 * (jnp.arange(n_heads, dtype=f32) + 1.0) / n_heads)
    return jnp.broadcast_to(s[:, None, None], (n_heads, 1, LANE))


def _band_bias(slope_d, shift):
    qi = lax.broadcasted_iota(jnp.int32, (BAND, BAND), 0)
    ki = lax.broadcasted_iota(jnp.int32, (BAND, BAND), 1)
    mask = (ki >= qi) if shift else (ki <= qi)
    return jnp.where(mask, -slope_d * (qi - ki + shift).astype(f32), NEG)


def _window_bias(slope_d, has_prev):
    own = _band_bias(slope_d, 0)
    mid = jnp.concatenate([_band_bias(slope_d, BAND), own], axis=1)
    none = jnp.concatenate([jnp.full((BAND, BAND), NEG, f32), own], axis=1)
    return mid, jnp.where(has_prev, mid, none)


def _window_scores(q, k2, bias):
    return lax.dot_general(q, k2, (NT, ((), ())), preferred_element_type=f32) * (HEAD ** -0.5) + bias


def _attn_geometry(s_len, dil, rows=1024):
    piece = BAND * dil
    m = max(1, rows // piece)
    while s_len % (piece * m):
        m //= 2
    return m, piece


def _stream_rows(start, dil):
    return pl.ds(start, BAND, stride=dil) if dil > 1 else pl.ds(start, BAND)


def _attn_fwd(qn, kn, proj, v_blk, dil, slopes):
    s_len, aw = qn.shape
    n_heads = aw // HEAD
    m, piece = _attn_geometry(s_len, dil)
    rows = m * piece

    def body(q_ref, k_ref, kp_ref, v_ref, vp_ref, sl_ref, o_ref, lse_ref):
        bias_mid, bias_first = _window_bias(sl_ref[:, 0:1] * float(dil), pl.program_id(1) > 0)
        for b in range(m):
            for r in range(dil):
                idx = _stream_rows(b * piece + r, dil)
                q, kc, vc = (ref[idx, :].astype(bf16) for ref in (q_ref, k_ref, v_ref))
                if b:
                    pidx = _stream_rows((b - 1) * piece + r, dil)
                    kp, vp = k_ref[pidx, :].astype(bf16), v_ref[pidx, :].astype(bf16)
                else:
                    pidx = _stream_rows(r, dil)
                    kp, vp = kp_ref[pidx, :].astype(bf16), vp_ref[pidx, :].astype(bf16)
                k2, v2 = jnp.concatenate([kp, kc], axis=0), jnp.concatenate([vp, vc], axis=0)
                s = _window_scores(q, k2, bias_mid if b else bias_first)
                mx = jnp.max(s, axis=1, keepdims=True)
                p = jnp.exp(s - mx)
                den = jnp.sum(p, axis=1, keepdims=True)
                o_ref[idx, :] = jnp.dot(p.astype(bf16), v2, preferred_element_type=f32) / den
                lse_ref[idx, :] = jnp.broadcast_to(mx + jnp.log(den), (BAND, HEAD))

    def cur(blk0):
        return pl.BlockSpec((rows, HEAD), lambda h, t: (t, blk0 + h))

    def prev(blk0):
        return pl.BlockSpec((piece, HEAD), lambda h, t: (jnp.maximum(t * m - 1, 0), blk0 + h))

    sl = pl.BlockSpec((None, 1, LANE), lambda h, t: (h, 0, 0))
    return _pcall(body, name=f"attn_fwd_d{dil}", grid=(n_heads, s_len // rows),
                  in_specs=[cur(0), cur(0), prev(0), cur(v_blk), prev(v_blk), sl], out_specs=[cur(0), cur(0)],
                  out_shape=[_sds((s_len, aw), f32)] * 2, sem=("parallel", "parallel"))(
        qn, kn, kn, proj, proj, slopes)


def _attn_bwd(qn, kn, proj, v_blk, do, lse, dd, dil, slopes):
    s_len, aw = qn.shape
    n_heads = aw // HEAD
    m, piece = _attn_geometry(s_len, dil, max(1024, 2 * BAND * dil))
    rows = m * piece
    n_tiles = s_len // rows
    scale = HEAD ** -0.5

    def body(q_ref, qx_ref, k_ref, kp_ref, v_ref, vp_ref, do_ref, dox_ref, l_ref, lx_ref, d_ref, dx_ref, sl_ref,
             dq_ref, dk_ref, dv_ref):
        t = pl.program_id(1)
        slope_d = sl_ref[:, 0:1] * float(dil)
        bias_mid, bias_first = _window_bias(slope_d, t > 0)
        bias_next = _band_bias(slope_d, BAND)

        def query_side(ref_q, ref_do, ref_l, ref_d, idx):
            return (ref_q[idx, :].astype(bf16), ref_do[idx, :].astype(bf16), ref_l[idx, :][:, 0:1],
                    ref_d[idx, :][:, 0:1])

        def probs(qs, keys, values, bias):
            q, do_, l_col, d_col = qs
            p = jnp.exp(_window_scores(q, keys, bias) - l_col)
            dp = lax.dot_general(do_, values, (NT, ((), ())), preferred_element_type=f32)
            return p.astype(bf16), (p * (dp - d_col)).astype(bf16)

        def tn(a_, b_):
            return lax.dot_general(a_, b_, (TN, ((), ())), preferred_element_type=f32)

        for r in range(dil):
            pend = None
            for b in range(m):
                idx = _stream_rows(b * piece + r, dil)
                qs = query_side(q_ref, do_ref, l_ref, d_ref, idx)
                kc, vc = k_ref[idx, :].astype(bf16), v_ref[idx, :].astype(bf16)
                if b:
                    kp, vp = kc_prev, vc_prev
                else:
                    pidx = _stream_rows(r, dil)
                    kp, vp = kp_ref[pidx, :].astype(bf16), vp_ref[pidx, :].astype(bf16)
                k2, v2 = jnp.concatenate([kp, kc], axis=0), jnp.concatenate([vp, vc], axis=0)
                p, ds = probs(qs, k2, v2, bias_mid if b else bias_first)
                dq_ref[idx, :] = jnp.dot(ds, k2, preferred_element_type=f32) * scale
                dk2, dv2 = tn(ds, qs[0]), tn(p, qs[1])
                if pend is not None:
                    dk_ref[pend[0], :] = (pend[1] + dk2[:BAND]) * scale
                    dv_ref[pend[0], :] = pend[2] + dv2[:BAND]
                pend = (idx, dk2[BAND:], dv2[BAND:])
                kc_prev, vc_prev = kc, vc
            qs = query_side(qx_ref, dox_ref, lx_ref, dx_ref, _stream_rows(r, dil))
            p, ds = probs(qs, kc_prev, vc_prev, bias_next)
            live = t < n_tiles - 1
            dk_ref[pend[0], :] = (pend[1] + jnp.where(live, tn(ds, qs[0]), 0.0)) * scale
            dv_ref[pend[0], :] = pend[2] + jnp.where(live, tn(p, qs[1]), 0.0)

    def cur(blk0):
        return pl.BlockSpec((rows, HEAD), lambda h, t: (t, blk0 + h))

    def prev(blk0):
        return pl.BlockSpec((piece, HEAD), lambda h, t: (jnp.maximum(t * m - 1, 0), blk0 + h))

    def nxt(blk0):
        return pl.BlockSpec((piece, HEAD), lambda h, t: (jnp.minimum(t * m + m, n_tiles * m - 1), blk0 + h))

    sl = pl.BlockSpec((None, 1, LANE), lambda h, t: (h, 0, 0))
    return _pcall(body, name=f"attn_bwd_d{dil}", grid=(n_heads, n_tiles),
                  in_specs=[cur(0), nxt(0), cur(0), prev(0), cur(v_blk), prev(v_blk), cur(0), nxt(0), cur(0), nxt(0),
                            cur(0), nxt(0), sl],
                  out_specs=[cur(0)] * 3, out_shape=[_sds((s_len, aw), f32)] * 3,
                  sem=("parallel", "parallel"))(qn, qn, kn, kn, proj, proj, do, do, lse, lse, dd, dd, slopes)


def _exchange(name, srcs, scatter):
    n = len(srcs)

    def body(*refs):
        src, out = refs[:n], refs[n:2 * n]
        send_sems, recv_sems, local_sems = refs[2 * n:]
        x, y, c = lax.axis_index("x"), lax.axis_index("y"), lax.axis_index("c")
        me = 4 * x + 2 * y + c

        def peer(r):
            return ((1 - x) if r & 4 else x, (1 - y) if r & 2 else y, (1 - c) if r & 1 else c)

        def lin(p):
            return 4 * p[0] + 2 * p[1] + p[2]

        def piece(a, idx):
            return src[a].at[idx] if scatter[a] else src[a]

        local, sends = [], []
        for a in range(n):
            cp = pltpu.make_async_copy(piece(a, me), out[a].at[me], local_sems.at[a])
            cp.start()
            local.append(cp)
        for r in range(1, N_DEV):
            p = peer(r)
            for a in range(n):
                cp = pltpu.make_async_remote_copy(src_ref=piece(a, lin(p)), dst_ref=out[a].at[me],
                                                  send_sem=send_sems.at[a, r - 1], recv_sem=recv_sems.at[a, r - 1],
                                                  device_id=p, device_id_type=MESH)
                cp.start()
                sends.append(cp)
        for r in range(1, N_DEV):
            p = peer(r)
            for a in range(n):
                pltpu.make_async_remote_copy(src_ref=piece(a, lin(p)), dst_ref=out[a].at[lin(p)],
                                             send_sem=send_sems.at[a, r - 1], recv_sem=recv_sems.at[a, r - 1],
                                             device_id=p, device_id_type=MESH).wait_recv()
        for cp in sends:
            cp.wait_send()
        for cp in local:
            cp.wait()

    def piece_shape(a):
        return srcs[a].shape[1:] if scatter[a] else srcs[a].shape

    any_spec = pl.BlockSpec(memory_space=pl.ANY)
    return _pcall(body, name=name, in_specs=[any_spec] * n, out_specs=[any_spec] * n,
                  out_shape=[_sds((N_DEV, *piece_shape(a)), srcs[a].dtype) for a in range(n)],
                  scratch_shapes=[pltpu.SemaphoreType.DMA((n, N_DEV - 1)), pltpu.SemaphoreType.DMA((n, N_DEV - 1)),
                                  pltpu.SemaphoreType.DMA((n,))])(*srcs)


_HBM = pl.BlockSpec(memory_space=pltpu.HBM)
_SEM = pl.BlockSpec(memory_space=pltpu.SEMAPHORE)
_EFFECT = pltpu.SideEffectType.DATAFLOW_SIDE_EFFECTING


def _peer_ids():
    x, y, c = lax.axis_index("x"), lax.axis_index("y"), lax.axis_index("c")
    peers = [((1 - x) if r & 4 else x, (1 - y) if r & 2 else y, (1 - c) if r & 1 else c) for r in range(1, N_DEV)]
    return 4 * x + 2 * y + c, peers, [4 * p[0] + 2 * p[1] + p[2] for p in peers]


def _exchange_start(name, src, scatter, after):
    piece_shape = src.shape[1:] if scatter else src.shape

    def body(src_ref, land_ref, after_ref, send_sems, recv_sems, local_sem, src_thru, land_thru, token):
        me, peers, lins = _peer_ids()

        def piece(idx):
            return src_ref.at[idx] if scatter else src_ref

        pltpu.make_async_copy(piece(me), land_ref.at[me], local_sem).start()
        for r, (p, lp) in enumerate(zip(peers, lins)):
            pltpu.make_async_remote_copy(src_ref=piece(lp), dst_ref=land_ref.at[me], send_sem=send_sems.at[r],
                                         recv_sem=recv_sems.at[r], device_id=p, device_id_type=MESH).start()
        token[...] = jnp.zeros_like(token)

    land = pltpu.with_memory_space_constraint(lax.empty((N_DEV, *piece_shape), src.dtype), pltpu.HBM)
    send_sems, recv_sems, local_sem, src_thru, land_thru, token = pl.pallas_call(
        body, name=name,
        out_shape=(pltpu.SemaphoreType.DMA((N_DEV - 1,)), pltpu.SemaphoreType.DMA((N_DEV - 1,)),
                   pltpu.SemaphoreType.DMA(()), pltpu.HBM(src.shape, src.dtype),
                   pltpu.HBM((N_DEV, *piece_shape), src.dtype), _sds((8, LANE), f32)),
        in_specs=(_HBM, _HBM, pl.BlockSpec(memory_space=pl.ANY)),
        out_specs=(_SEM, _SEM, _SEM, _HBM, _HBM, pl.BlockSpec(memory_space=pltpu.VMEM)),
        input_output_aliases={0: 3, 1: 4},
        compiler_params=pltpu.CompilerParams(has_side_effects=_EFFECT),
    )(pltpu.with_memory_space_constraint(src, pltpu.HBM), land, after)
    return (send_sems, recv_sems, local_sem, src_thru, land_thru, scatter), token


def _exchange_wait(name, handle, after):
    send_sems, recv_sems, local_sem, src_thru, land_thru, scatter = handle

    def body(src_ref, land_ref, send_sems_, recv_sems_, local_sem_, after_ref, src_dead, got_ref):
        me, peers, lins = _peer_ids()

        def piece(idx):
            return src_ref.at[idx] if scatter else src_ref

        pltpu.make_async_copy(piece(me), land_ref.at[me], local_sem_).wait()
        for r, (p, lp) in enumerate(zip(peers, lins)):
            pltpu.make_async_remote_copy(src_ref=piece(lp), dst_ref=land_ref.at[me], send_sem=send_sems_.at[r],
                                         recv_sem=recv_sems_.at[r], device_id=p, device_id_type=MESH).wait_send()
            pltpu.make_async_remote_copy(src_ref=piece(lp), dst_ref=land_ref.at[lp], send_sem=send_sems_.at[r],
                                         recv_sem=recv_sems_.at[r], device_id=p, device_id_type=MESH).wait_recv()

    return pl.pallas_call(
        body, name=name,
        out_shape=(pltpu.HBM(src_thru.shape, src_thru.dtype), pltpu.HBM(land_thru.shape, land_thru.dtype)),
        in_specs=(_HBM, _HBM, _SEM, _SEM, _SEM, pl.BlockSpec(memory_space=pl.ANY)), out_specs=(_HBM, _HBM),
        input_output_aliases={0: 0, 1: 1},
        compiler_params=pltpu.CompilerParams(has_side_effects=_EFFECT),
    )(src_thru, land_thru, send_sems, recv_sems, local_sem, after)[1]


def _adamw(name, w, m, v, g_or_stack, stacked, rows=256):
    r, c = w.shape
    tr = _tile(r, rows)

    def fn(w_, m_, v_, g_):
        if stacked:
            g = g_[0].astype(f32)
            for j in range(1, N_DEV):
                g = g + g_[j].astype(f32)
        else:
            g = g_
        m_new = ADAM_B1 * m_ + (1.0 - ADAM_B1) * g
        v_new = ADAM_B2 * v_ + (1.0 - ADAM_B2) * (g * g)
        m_hat = m_new / (1.0 - ADAM_B1 ** ADAM_STEP)
        v_hat = v_new / (1.0 - ADAM_B2 ** ADAM_STEP)
        delta = -ADAM_LR * (m_hat / (jnp.sqrt(v_hat) + ADAM_EPS) + ADAM_WD * w_)
        return g, delta, m_new, v_new

    blk = _row(tr, c)
    g_spec = pl.BlockSpec((N_DEV, tr, c), lambda i: (0, i, 0)) if stacked else blk
    return _rowwise(name, fn, r // tr, [w, m, v, g_or_stack], [blk, blk, blk, g_spec],
                    [_sds((r, c), f32)] * 4, [blk] * 4, [False] * 4)


def _ada_fwd(c_all, w_shard, b_shard):
    nb_, d = c_all.shape
    n = w_shard.shape[1]
    tn = _tile(n, 512)

    def body(c_ref, w_ref, b_ref, o_ref):
        a = jax.nn.silu(c_ref[...]).astype(bf16)
        o_ref[...] = jnp.dot(a, w_ref[...].astype(bf16), preferred_element_type=f32) + b_ref[...]

    return _pcall(body, name="ada_fwd", grid=(n // tn,),
                  in_specs=[pl.BlockSpec((nb_, d), lambda j: (0, 0)), pl.BlockSpec((d, tn), lambda j: (0, j)),
                            pl.BlockSpec((1, tn), lambda j: (0, j))],
                  out_specs=pl.BlockSpec((nb_, tn), lambda j: (0, j)), out_shape=_sds((nb_, n), f32),
                  sem=("parallel",))(c_all, w_shard, b_shard)


def _ada_bwd(c_all, dmod_cols):
    nb_, d = c_all.shape
    n = dmod_cols.shape[1]
    tn = _tile(n, 512)

    def body(c_ref, g_ref, o_ref):
        a = jax.nn.silu(c_ref[...]).astype(bf16).astype(f32)
        g = g_ref[...].astype(bf16).astype(f32)
        o_ref[...] = lax.dot_general(a, g, (TN, ((), ())), precision=HI, preferred_element_type=f32)

    return _pcall(body, name="ada_bwd", grid=(n // tn,),
                  in_specs=[pl.BlockSpec((nb_, d), lambda j: (0, 0)), pl.BlockSpec((nb_, tn), lambda j: (0, j))],
                  out_specs=pl.BlockSpec((d, tn), lambda j: (0, j)), out_shape=_sds((d, n), f32),
                  sem=("parallel",))(c_all, dmod_cols)


SMALL_LATE = ("b_ada", "norm1_g", "q_norm_g", "k_norm_g")
SMALL_EARLY = ("lam_re", "lam_im", "log_step", "b_re", "b_im", "c_re", "c_im", "d_skip", "b_glu", "attn_out_g",
               "ssm_out_g", "norm2_g")
ORDER = ("w_ada", "b_ada", "norm1_g", "w_in", "q_norm_g", "k_norm_g", "lam_re", "lam_im", "log_step", "b_re", "b_im",
         "c_re", "c_im", "d_skip", "w_glu", "b_glu", "attn_out_g", "ssm_out_g", "w_out", "norm2_g", "w_ff1", "w_ff2")


def _pack(parts):
    flat = jnp.concatenate([p.reshape(-1) for p in parts])
    pad = (-flat.shape[0]) % (8 * LANE)
    return jnp.pad(flat, (0, pad)).reshape(-1, LANE)


def kernel(x, c, w_ada, b_ada, norm1_g, w_in, q_norm_g, k_norm_g, lam_re, lam_im, log_step, b_re, b_im, c_re, c_im, d_skip, w_glu, b_glu, attn_out_g, ssm_out_g, w_out, norm2_g, w_ff1, w_ff2, loss_target, m_w_ada, m_b_ada, m_norm1_g, m_w_in, m_q_norm_g, m_k_norm_g, m_lam_re, m_lam_im, m_log_step, m_b_re, m_b_im, m_c_re, m_c_im, m_d_skip, m_w_glu, m_b_glu, m_attn_out_g, m_ssm_out_g, m_w_out, m_norm2_g, m_w_ff1, m_w_ff2, v_w_ada, v_b_ada, v_norm1_g, v_w_in, v_q_norm_g, v_k_norm_g, v_lam_re, v_lam_im, v_log_step, v_b_re, v_b_im, v_c_re, v_c_im, v_d_skip, v_w_glu, v_b_glu, v_attn_out_g, v_ssm_out_g, v_w_out, v_norm2_g, v_w_ff1, v_w_ff2):
    env = dict(locals())
    wts = {n: env[n] for n in ORDER}
    mom = {n: env["m_" + n] for n in ORDER}
    var = {n: env["v_" + n] for n in ORDER}

    xs, tgt = x[0], loss_target[0]
    s_len, d = xs.shape
    aw = d // 2
    sw = d - aw
    n_heads = aw // HEAD
    n_groups = sw // SSM_GROUP
    n_state = lam_re.shape[-1]
    gp = n_groups * n_state
    tm = _tile(s_len, 256)
    steps = s_len // tm
    me = 4 * lax.axis_index("x") + 2 * lax.axis_index("y") + lax.axis_index("c")

    (c_all,) = _exchange("gather_c", [c], [False])
    c_all = c_all.reshape(N_DEV, d)

    n_ada = w_ada.shape[-1]
    b_ada_cols = lax.dynamic_slice_in_dim(b_ada, me * n_ada, n_ada, axis=1)
    mod_part = _ada_fwd(c_all, w_ada[0], b_ada_cols)
    (mod_all,) = _exchange("gather_mod", [mod_part], [False])
    mod = lax.dynamic_index_in_dim(mod_all, me, axis=1, keepdims=False).reshape(1, 6 * d)
    sh1, sc1, g1, sh2, sc2, g2 = (mod[:, i * d:(i + 1) * d] for i in range(6))

    gather, started = {}, jnp.zeros((1, 1), f32)
    for name in ("w_in", "w_glu", "w_out", "w_ff1", "w_ff2"):
        gather[name], token = _exchange_start("gather_" + name, wts[name][0].astype(bf16), False, mod_all)
        started = started + token[0:1, 0:1]
    sc1 = sc1 + started

    (h,) = _rowwise("norm1", _norm_mod, steps, [xs, norm1_g, sc1, sh1],
                    [_row(tm, d), _vec(d), _vec(d), _vec(d)], [_sds((s_len, d), bf16)], [_row(tm, d)], [False])
    win_g = _exchange_wait("gathered_w_in", gather["w_in"], h)
    (proj,) = _mm_nn_sharded("in_proj", h, win_g)

    def qk_fn(q, k, gq, gk):
        return _head_rms(q, gq), _head_rms(k, gk)

    qn, kn = _rowwise("qk_norm", qk_fn, steps, [proj, proj, q_norm_g, k_norm_g],
                      [_row(tm, aw, 0), _row(tm, aw, 1), _vec(HEAD), _vec(HEAD)],
                      [_sds((s_len, aw), f32)] * 2, [_row(tm, aw)] * 2, [False] * 2)
    v_blk = 2 * aw // HEAD

    slopes = _slope_table(n_heads)
    pat = [_attn_fwd(qn, kn, proj, v_blk, dil, slopes) for _, dil in DILATIONS]

    def attn_mix_fn(o1, l1, o2, l2, o3, l3):
        m = jnp.maximum(jnp.maximum(l1, l2), l3)
        e1, e2, e3 = jnp.exp(l1 - m), jnp.exp(l2 - m), jnp.exp(l3 - m)
        tot = e1 + e2 + e3
        return (e1 * o1 + e2 * o2 + e3 * o3) / tot, m + jnp.log(tot)

    attn, lse = _rowwise("attn_mix", attn_mix_fn, steps, [t for ol in pat for t in ol], [_row(tm, aw)] * 6,
                         [_sds((s_len, aw), f32)] * 2, [_row(tm, aw)] * 2, [False] * 2)

    lam_re2, lam_im2 = lam_re[0], lam_im[0]
    log_step2 = log_step[0].reshape(n_groups, 1)
    b_re2 = b_re[0].reshape(n_groups, n_state * SSM_GROUP)
    b_im2 = b_im[0].reshape(n_groups, n_state * SSM_GROUP)
    expand = jnp.repeat(jnp.eye(n_state, dtype=f32), SSM_GROUP, axis=1)
    a_re, a_im, bb_re2, bb_im2 = _ssm_params(lam_re2, lam_im2, log_step2, b_re2, b_im2, expand)
    a2 = jnp.zeros((8, gp), f32).at[0].set(a_re.reshape(gp)).at[1].set(a_im.reshape(gp))

    w_bu = tuple(_lane_block_weights(t.reshape(n_groups, n_state, SSM_GROUP).transpose(0, 2, 1), n_state)
                 for t in (bb_re2, bb_im2))
    w_c = (_lane_block_weights(c_re[0], n_state), _lane_block_weights(-c_im[0], n_state))

    nseg = _scan_segments(s_len)
    u_seg = _to_segments(proj[:, 3 * aw:], nseg).astype(bf16)
    y_seg, h_re, h_im, hin_f = _scan("ssm_scan", u_seg, w_bu, w_c, a2, reverse=False)
    ymm = _from_segments(y_seg, nseg)

    u_spec = _row(tm, sw, 3 * aw // sw)
    (yg,) = _rowwise("ssm_gelu", _ypre_fn, steps, [ymm, proj, d_skip], [_row(tm, sw), u_spec, _vec(sw)],
                     [_sds((s_len, sw), f32)], [_row(tm, sw)], [False])
    wglu_g = _exchange_wait("gathered_w_glu", gather["w_glu"], yg).reshape(sw, sw)
    (z,) = _mm_nn("glu_proj", yg, wglu_g)
    (cat,) = _rowwise("mix_norm", _mix_fn, steps, [attn, yg, z, b_glu, attn_out_g, ssm_out_g],
                      [_row(tm, aw), _row(tm, sw), _row(tm, sw), _vec(sw), _vec(aw), _vec(sw)],
                      [_sds((s_len, d), bf16)], [_row(tm, d)], [False])
    wout_g = _exchange_wait("gathered_w_out", gather["w_out"], cat).reshape(d, d)
    (mixed,) = _mm_nn("out_proj", cat, wout_g)

    def res_norm2_fn(x_, mixed_, g1_, gn, sc, sh):
        x1_ = x_ + g1_ * mixed_
        return x1_, _norm_mod(x1_, gn, sc, sh)

    x1, h2 = _rowwise("norm2", res_norm2_fn, steps, [xs, mixed, g1, norm2_g, sc2, sh2],
                      [_row(tm, d), _row(tm, d)] + [_vec(d)] * 4,
                      [_sds((s_len, d), f32), _sds((s_len, d), bf16)], [_row(tm, d)] * 2, [False] * 2)

    def act_epilogue(acc):
        r = jnp.maximum(acc, 0.0)
        return r, r * r

    wff1_g = _exchange_wait("gathered_w_ff1", gather["w_ff1"], h2)
    r_ff, act = _mm_nn_sharded("ff1", h2, wff1_g, epilogue=act_epilogue,
                               outs=[_sds((s_len, 4 * d), bf16), _sds((s_len, 4 * d), bf16)])
    wff2_g = _exchange_wait("gathered_w_ff2", gather["w_ff2"], act).reshape(4 * d, d)
    (ff,) = _mm_nn("ff2", act, wff2_g)

    def loss_fn(x1_, ff_, tgt_, g2_):
        e = x1_ + g2_ * ff_ - tgt_
        dy_ = e * (1.0 / d)
        part = jnp.full((1, LANE), 0.5 / d, f32) * jnp.sum(e * e)
        return dy_, g2_ * dy_, part, jnp.sum(dy_ * ff_, axis=0, keepdims=True)

    dy, dff, loss_part, d_g2 = _rowwise(
        "loss", loss_fn, steps, [x1, ff, tgt, g2], [_row(tm, d)] * 3 + [_vec(d)],
        [_sds((s_len, d), f32), _sds((s_len, d), bf16), _sds((1, LANE), f32), _sds((1, d), f32)],
        [_row(tm, d), _row(tm, d), _vec(LANE), _vec(d)], [False, False, True, True])
    loss = lax.psum(loss_part[0, 0], ("x", "y", "c"))

    def dact_epilogue(acc, r_):
        return (acc * (2.0 * r_.astype(f32)),)

    (da,) = _mm_nt("ff2_dx", dff, wff2_g, epilogue=dact_epilogue, extra=[r_ff], outs=[_sds((s_len, 4 * d), bf16)])
    scatter = {}
    g_wff2 = _mm_tn("ff2_dw", act, dff).reshape(N_DEV, 4 * d // N_DEV, d)
    scatter["w_ff2"], tok_ff2 = _exchange_start("scatter_w_ff2", g_wff2, True, loss.reshape(1, 1))
    dh2 = _mm_nt_sharded("ff1_dx", da, wff1_g)
    g_wff1 = _mm_tn_sharded("ff1_dw", h2, da, N_DEV)
    scatter["w_ff1"], tok_ff1 = _exchange_start("scatter_w_ff1", g_wff1, True, started)
    norm2_g_t = norm2_g + (tok_ff2[0:1, 0:1] + tok_ff1[0:1, 0:1])

    def norm2_bwd_fn(dh2_, x1_, dy_, mixed_, gn, sc, sh, g1_):
        _, vjp = jax.vjp(_norm_mod, x1_, gn, sc, sh)
        dx, dgn, dsc, dsh = vjp(dh2_)
        dx1_ = dy_ + dx
        return dx1_, g1_ * dx1_, dgn, dsc, dsh, jnp.sum(dx1_ * mixed_, axis=0, keepdims=True)

    dx1, dmixed, d_norm2_g, d_sc2, d_sh2, d_g1 = _rowwise(
        "norm2_bwd", norm2_bwd_fn, steps, [dh2, x1, dy, mixed, norm2_g_t, sc2, sh2, g1],
        [_row(tm, d)] * 4 + [_vec(d)] * 4,
        [_sds((s_len, d), f32), _sds((s_len, d), bf16)] + [_sds((1, d), f32)] * 4,
        [_row(tm, d)] * 2 + [_vec(d)] * 4, [False, False, True, True, True, True])

    (dcat,) = _mm_nt("out_dx", dmixed, wout_g)
    g_wout = _mm_tn("out_dw", cat, dmixed).reshape(N_DEV, d // N_DEV, d)
    scatter["w_out"], tok_out = _exchange_start("scatter_w_out", g_wout, True, started)
    b_glu_t = b_glu + tok_out[0:1, 0:1]

    def mix_bwd_fn(dcat_, attn_, yg_, z_, bglu, ga, gs):
        _, vjp = jax.vjp(_mix_fn, attn_, yg_, z_, bglu, ga, gs)
        dattn_, dyg_, dz_, dbglu, dga, dgs = vjp(dcat_)
        prod = dattn_ * attn_
        dd_ = jnp.concatenate([jnp.broadcast_to(jnp.sum(prod[:, i * HEAD:(i + 1) * HEAD], axis=1, keepdims=True),
                                                (prod.shape[0], HEAD)) for i in range(n_heads)], axis=1)
        return dattn_, dd_, dyg_, dz_, dbglu, dga, dgs

    dattn, dd, dyg1, dz, d_b_glu, d_attn_out_g, d_ssm_out_g = _rowwise(
        "mix_bwd", mix_bwd_fn, steps, [dcat, attn, yg, z, b_glu_t, attn_out_g, ssm_out_g],
        [_row(tm, d), _row(tm, aw), _row(tm, sw), _row(tm, sw), _vec(sw), _vec(aw), _vec(sw)],
        [_sds((s_len, aw), f32), _sds((s_len, aw), f32), _sds((s_len, sw), f32), _sds((s_len, sw), bf16),
         _sds((1, sw), f32), _sds((1, aw), f32), _sds((1, sw), f32)],
        [_row(tm, aw), _row(tm, aw), _row(tm, sw), _row(tm, sw), _vec(sw), _vec(aw), _vec(sw)],
        [False] * 4 + [True] * 3)

    (dyg2,) = _mm_nt("glu_dx", dz, wglu_g)
    g_wglu = _mm_tn("glu_dw", yg, dz).reshape(N_DEV, sw // N_DEV, sw)
    scatter["w_glu"], tok_glu = _exchange_start("scatter_w_glu", g_wglu, True, started)
    d_skip_t = d_skip + tok_glu[0:1, 0:1]

    def gelu_bwd_fn(dyg1_, dyg2_, ymm_, u_, dskip):
        _, vjp = jax.vjp(_ypre_fn, ymm_, u_, dskip)
        dymm, du_, ddskip = vjp(dyg1_ + dyg2_)
        return dymm, du_, ddskip

    dymm, du_skip, d_d_skip = _rowwise(
        "ssm_gelu_bwd", gelu_bwd_fn, steps, [dyg1, dyg2, ymm, proj, d_skip_t],
        [_row(tm, sw)] * 3 + [u_spec, _vec(sw)],
        [_sds((s_len, sw), f32), _sds((s_len, sw), f32), _sds((1, sw), f32)],
        [_row(tm, sw), _row(tm, sw), _vec(sw)], [False, False, True])

    dymm_seg = _to_segments(dymm, nseg).astype(bf16)
    du_seg, da_seg, dbb_c, dc_c = _scan("ssm_adj", dymm_seg, w_c, w_bu, a2, reverse=True,
                                        adjoint_of=(u_seg, h_re, h_im, hin_f))
    du_ssm = _from_segments(du_seg, nseg)

    def to_gpi(w):
        return _lane_block_diag(w, n_state).transpose(0, 2, 1).reshape(n_groups, n_state * SSM_GROUP)

    d_lam_re, d_lam_im, d_log_step, d_b_re2, d_b_im2 = _ssm_params_bwd(
        lam_re2, lam_im2, log_step2, b_re2, b_im2, expand,
        da_seg[0, 0].reshape(n_groups, n_state), da_seg[1, 0].reshape(n_groups, n_state),
        to_gpi(dbb_c[0]), to_gpi(dbb_c[1]))
    d_c_re = _lane_block_diag(dc_c[0], n_state)
    d_c_im = -_lane_block_diag(dc_c[1], n_state)

    grads_qkv = [_attn_bwd(qn, kn, proj, v_blk, dattn, lse, dd, dil, slopes) for _, dil in DILATIONS]

    def qkv_bwd_fn(q, k, gq, gk, dq1, dq2, dq3, dk1, dk2, dk3, dv1, dv2, dv3, du1, du2):
        _, vjp = jax.vjp(lambda q_, k_, gq_, gk_: (_head_rms(q_, gq_), _head_rms(k_, gk_)), q, k, gq, gk)
        dq, dk, dgq, dgk = vjp((dq1 + dq2 + dq3, dk1 + dk2 + dk3))
        return jnp.concatenate([dq, dk, dv1 + dv2 + dv3, du1 + du2], axis=1), dgq, dgk

    small_g = {"lam_re": d_lam_re, "lam_im": d_lam_im, "log_step": d_log_step, "b_re": d_b_re2, "b_im": d_b_im2,
               "c_re": d_c_re, "c_im": d_c_im, "d_skip": d_d_skip, "b_glu": d_b_glu,
               "attn_out_g": d_attn_out_g, "ssm_out_g": d_ssm_out_g, "norm2_g": d_norm2_g}
    early, tok_early = _exchange_start("gather_early_grads", _pack([small_g[n] for n in SMALL_EARLY]), False, started)

    qkv_cots = [grads_qkv[p][i] for i in range(3) for p in range(3)]
    dproj, small_g["q_norm_g"], small_g["k_norm_g"] = _rowwise(
        "qk_norm_bwd", qkv_bwd_fn, steps,
        [proj, proj, q_norm_g + tok_early[0:1, 0:1], k_norm_g, *qkv_cots, du_skip, du_ssm],
        [_row(tm, aw, 0), _row(tm, aw, 1), _vec(HEAD), _vec(HEAD)] + [_row(tm, aw)] * 9 + [_row(tm, sw)] * 2,
        [_sds((s_len, 3 * aw + sw), bf16), _sds((1, HEAD), f32), _sds((1, HEAD), f32)],
        [_row(tm, 3 * aw + sw), _vec(HEAD), _vec(HEAD)], [False, True, True])

    g_win = _mm_tn_sharded("in_dw", h, dproj, N_DEV)
    scatter["w_in"], tok_in = _exchange_start("scatter_w_in", g_win, True, tok_early)
    dh = _mm_nt_sharded("in_dx", dproj, win_g, after=tok_in)
    norm1_g_t = norm1_g + tok_in[0:1, 0:1]

    def norm1_bwd_fn(dh_, x_, dx1_, gn, sc, sh):
        _, vjp = jax.vjp(_norm_mod, x_, gn, sc, sh)
        dx, dgn, dsc, dsh = vjp(dh_)
        return dx1_ + dx, dgn, dsc, dsh

    grad_x, d_norm1_g, d_sc1, d_sh1 = _rowwise(
        "norm1_bwd", norm1_bwd_fn, steps, [dh, xs, dx1, norm1_g_t, sc1, sh1], [_row(tm, d)] * 3 + [_vec(d)] * 3,
        [_sds((s_len, d), f32)] + [_sds((1, d), f32)] * 3, [_row(tm, d)] + [_vec(d)] * 3,
        [False, True, True, True])

    small_g["b_ada"] = jnp.concatenate([d_sh1, d_sc1, d_g1, d_sh2, d_sc2, d_g2], axis=1)
    small_g["norm1_g"] = d_norm1_g
    (r_late,) = _exchange("gather_late_grads", [_pack([small_g[n] for n in SMALL_LATE])], [False])

    res = {}
    dmod_all = r_late.reshape(N_DEV, -1)[:, :6 * d]
    g_wada = _ada_bwd(c_all, lax.dynamic_slice_in_dim(dmod_all, me * n_ada, n_ada, axis=1))
    res["w_ada"] = _adamw("adamw_w_ada", w_ada[0], m_w_ada[0], v_w_ada[0], g_wada, False)
    after = res["w_ada"][1]
    for name in ("w_ff2", "w_ff1", "w_out", "w_glu", "w_in"):
        stack = _exchange_wait("scattered_" + name, scatter[name], after)
        res[name] = _adamw("adamw_" + name, wts[name][0], mom[name][0], var[name][0], stack, True)
        after = res[name][1]
    r_early = _exchange_wait("gathered_early_grads", early, after)
    for label, names, stack in (("late", SMALL_LATE, r_late), ("early", SMALL_EARLY, r_early)):
        small_res = _adamw("adamw_small_" + label, _pack([wts[n] for n in names]), _pack([mom[n] for n in names]),
                           _pack([var[n] for n in names]), stack, True, rows=4096)
        off = 0
        for n in names:
            size = wts[n].size
            res[n] = [t.reshape(-1)[off:off + size] for t in small_res]
            off += size

    out = [loss, grad_x[None]]
    for i in range(4):
        out += [res[n][i].reshape(wts[n].shape) for n in ORDER]
    return tuple(out)
```

```python
import math

import jax
import jax.numpy as jnp
from jax import lax
from jax.experimental import pallas as pl
from jax.experimental.pallas import tpu as pltpu

f32, bf16 = jnp.float32, jnp.bfloat16

N_DEV = 8
LANE = 128
HEAD = 128
SSM_GROUP = 16
DILATIONS = ((128, 1), (512, 4), (2048, 16))
BAND = 128
EPS = 1e-6
ADAM_LR, ADAM_B1, ADAM_B2, ADAM_EPS, ADAM_WD, ADAM_STEP = 0.001, 0.9, 0.999, 1e-08, 0.01, 10
NEG = -1e30
VMEM_LIMIT = 60 * 1024 * 1024
HI = lax.Precision.HIGHEST
MESH = pl.DeviceIdType.MESH


def _pcall(body, **kw):
    sem = kw.pop("sem", None)
    kw["compiler_params"] = pltpu.CompilerParams(dimension_semantics=sem, vmem_limit_bytes=VMEM_LIMIT)
    return pl.pallas_call(body, **kw)


def _tile(n, pref):
    t = min(n, pref)
    while n % t:
        t //= 2
    return t


def _sds(shape, dtype):
    return jax.ShapeDtypeStruct(shape, dtype)


def _rowwise(name, fn, steps, ins, in_specs, outs, out_specs, acc):
    n_in = len(ins)

    def body(*refs):
        res = fn(*[r[...] for r in refs[:n_in]])
        res = res if isinstance(res, (tuple, list)) else (res,)
        for r, o, a in zip(refs[n_in:], res, acc):
            if a:
                @pl.when(pl.program_id(0) == 0)
                def _():
                    r[...] = jnp.zeros_like(r)
                r[...] += o
            else:
                r[...] = o.astype(r.dtype)

    return _pcall(body, name=name, grid=(steps,), in_specs=in_specs, out_specs=out_specs, out_shape=outs,
                  sem=("arbitrary",))(*ins)


def _row(tm, c, blk=0):
    return pl.BlockSpec((tm, c), lambda i: (i, blk))


def _vec(c, blk=0):
    return pl.BlockSpec((1, c), lambda i: (0, blk))


def _rms(x, g):
    return x * lax.rsqrt(jnp.mean(x * x, axis=-1, keepdims=True) + EPS) * g


def _norm_mod(x, g, sc, sh):
    return _rms(x, g) * (1.0 + sc) + sh


def _head_rms(t, g):
    return jnp.concatenate([_rms(t[:, h * HEAD:(h + 1) * HEAD], g) for h in range(t.shape[1] // HEAD)], axis=1)


def _mix_fn(attn, yg, z, bglu, ga, gs):
    ssm = yg * jax.nn.sigmoid(z + bglu)
    return jnp.concatenate([_rms(attn, ga), _rms(ssm, gs)], axis=1)


def _ypre_fn(ymm, u, dskip):
    return jax.nn.gelu(ymm + dskip * u)


def _matmul(name, a, b, *, dims, grid, a_spec, b_spec, acc_shape, outs, out_specs, extra=(), extra_specs=(),
            epilogue=None, after=None):
    gk = grid[2]
    n_x = len(extra)
    placed = [] if after is None else [after]
    first_out = n_x + len(placed)
    ins = [a, b, *extra, *placed]
    in_specs = [a_spec, b_spec, *extra_specs] + [pl.BlockSpec(memory_space=pl.ANY)] * len(placed)

    def product(a_ref, b_ref):
        return lax.dot_general(a_ref[...].astype(bf16), b_ref[...].astype(bf16), (dims, ((), ())),
                               preferred_element_type=f32)

    def finish(res, x_refs, o_refs):
        res = epilogue(res, *[r[...] for r in x_refs]) if epilogue is not None else (res,)
        for r, o in zip(o_refs, res):
            r[...] = o.astype(r.dtype)

    def body_single(a_ref, b_ref, *rest):
        finish(product(a_ref, b_ref), rest[:n_x], rest[first_out:])

    def body_pair(a_ref, b_ref, *rest):
        acc = rest[-1]
        prod = product(a_ref, b_ref)

        @pl.when(pl.program_id(2) == 0)
        def _():
            acc[...] = prod

        @pl.when(pl.program_id(2) == 1)
        def _():
            finish(acc[...] + prod, rest[:n_x], rest[first_out:-1])

    def body(a_ref, b_ref, *rest):
        acc = rest[-1]
        k = pl.program_id(2)

        @pl.when(k == 0)
        def _():
            acc[...] = jnp.zeros_like(acc)

        acc[...] += product(a_ref, b_ref)

        @pl.when(k == gk - 1)
        def _():
            finish(acc[...], rest[:n_x], rest[first_out:-1])

    sem = ("parallel", "parallel", "arbitrary")
    if gk == 1:
        return _pcall(body_single, name=name, grid=grid, in_specs=in_specs, out_specs=out_specs, out_shape=outs,
                      sem=sem)(*ins)
    return _pcall(body_pair if gk == 2 else body, name=name, grid=grid, in_specs=in_specs, out_specs=out_specs,
                  out_shape=outs, scratch_shapes=[pltpu.VMEM(acc_shape, f32)], sem=sem)(*ins)


NN = ((1,), (0,))
NT = ((1,), (1,))
TN = ((0,), (0,))


def _mm_nn(name, a, b, out_dtype=f32, tm=1024, tn=1024, tk=2048, epilogue=None, extra=(), outs=None):
    m, kd = a.shape
    n = b.shape[1]
    tm, tn, tk = _tile(m, tm), _tile(n, tn), _tile(kd, tk)
    o_spec = pl.BlockSpec((tm, tn), lambda i, j, k: (i, j))
    outs = outs if outs is not None else [_sds((m, n), out_dtype)]
    return _matmul(name, a, b, dims=NN, grid=(m // tm, n // tn, kd // tk),
                   a_spec=pl.BlockSpec((tm, tk), lambda i, j, k: (i, k)),
                   b_spec=pl.BlockSpec((tk, tn), lambda i, j, k: (k, j)),
                   acc_shape=(tm, tn), outs=outs, out_specs=[o_spec] * len(outs),
                   extra=extra, extra_specs=[o_spec] * len(extra), epilogue=epilogue)


def _mm_nn_sharded(name, a, b3, out_dtype=f32, tm=1024, tk=2048, epilogue=None, outs=None):
    m, kd = a.shape
    nsh, _, n = b3.shape
    tm, tk = _tile(m, tm), _tile(kd, tk)
    o_spec = pl.BlockSpec((tm, n), lambda i, j, k: (i, j))
    outs = outs if outs is not None else [_sds((m, nsh * n), out_dtype)]
    return _matmul(name, a, b3, dims=NN, grid=(m // tm, nsh, kd // tk),
                   a_spec=pl.BlockSpec((tm, tk), lambda i, j, k: (i, k)),
                   b_spec=pl.BlockSpec((None, tk, n), lambda i, j, k: (j, k, 0)),
                   acc_shape=(tm, n), outs=outs, out_specs=[o_spec] * len(outs), epilogue=epilogue)


def _mm_nt(name, a, b, out_dtype=f32, tm=1024, tn=1024, tk=2048, epilogue=None, extra=(), outs=None, after=None):
    m, kd = a.shape
    n = b.shape[0]
    tm, tn, tk = _tile(m, tm), _tile(n, tn), _tile(kd, tk)
    o_spec = pl.BlockSpec((tm, tn), lambda i, j, k: (i, j))
    outs = outs if outs is not None else [_sds((m, n), out_dtype)]
    return _matmul(name, a, b, dims=NT, grid=(m // tm, n // tn, kd // tk),
                   a_spec=pl.BlockSpec((tm, tk), lambda i, j, k: (i, k)),
                   b_spec=pl.BlockSpec((tn, tk), lambda i, j, k: (j, k)),
                   acc_shape=(tm, tn), outs=outs, out_specs=[o_spec] * len(outs),
                   extra=extra, extra_specs=[o_spec] * len(extra), epilogue=epilogue, after=after)


def _mm_tn(name, a, b, out_dtype=bf16, tm=1024, tn=1024, tk=2048, after=None):
    t, m = a.shape
    n = b.shape[1]
    tm, tn, tk = _tile(m, tm), _tile(n, tn), _tile(t, tk)
    return _matmul(name, a, b, dims=TN, grid=(m // tm, n // tn, t // tk),
                   a_spec=pl.BlockSpec((tk, tm), lambda i, j, k: (k, i)),
                   b_spec=pl.BlockSpec((tk, tn), lambda i, j, k: (k, j)),
                   acc_shape=(tm, tn), outs=[_sds((m, n), out_dtype)],
                   out_specs=[pl.BlockSpec((tm, tn), lambda i, j, k: (i, j))], after=after)[0]


def _mm_tn_sharded(name, a, b, nsh, out_dtype=bf16, tm=1024, tk=2048):
    t, m = a.shape
    n = b.shape[1] // nsh
    tm, tk = _tile(m, tm), _tile(t, tk)
    return _matmul(name, a, b, dims=TN, grid=(m // tm, nsh, t // tk),
                   a_spec=pl.BlockSpec((tk, tm), lambda i, j, k: (k, i)),
                   b_spec=pl.BlockSpec((tk, n), lambda i, j, k: (k, j)),
                   acc_shape=(tm, n), outs=[_sds((nsh, m, n), out_dtype)],
                   out_specs=[pl.BlockSpec((None, tm, n), lambda i, j, k: (j, i, 0))])[0]


SCAN_CHAINS = 8


def _scan_segments(s_len):
    nch = SCAN_CHAINS
    while s_len % (8 * nch) or (s_len // (8 * nch)) & (s_len // (8 * nch) - 1):
        nch //= 2
    return 8 * nch


def _to_segments(t, nseg):
    s_len, c = t.shape
    return t.reshape(nseg, s_len // nseg, c).transpose(1, 0, 2).reshape(s_len, c)


def _from_segments(t, nseg):
    s_len, c = t.shape
    return t.reshape(s_len // nseg, nseg, c).transpose(1, 0, 2).reshape(s_len, c)


def _lane_block_weights(t3, n_state):
    n_groups = t3.shape[0]
    gpl = LANE // n_state
    per = LANE // (gpl * SSM_GROUP)
    n_lb = n_groups // gpl
    t5 = t3.reshape(n_lb // per, per, gpl, SSM_GROUP, n_state)
    w = jnp.einsum("aqgic,gh,qs->aqsgihc", t5, jnp.eye(gpl, dtype=t3.dtype), jnp.eye(per, dtype=t3.dtype))
    return w.reshape(n_lb, LANE, LANE).astype(bf16)


def _lane_block_diag(w, n_state):
    gpl = LANE // n_state
    per = LANE // (gpl * SSM_GROUP)
    n_lb = w.shape[0]
    w7 = w.reshape(n_lb // per, per, per, gpl, SSM_GROUP, gpl, n_state)
    t5 = jnp.einsum("aqsgihc,gh,qs->aqgic", w7, jnp.eye(gpl, dtype=w.dtype), jnp.eye(per, dtype=w.dtype))
    return t5.reshape(n_lb * gpl, SSM_GROUP, n_state)


def _scan(name, src, w_in, w_out, a2, *, reverse, adjoint_of=None):
    s_len, n_ch = src.shape
    gp = a2.shape[1]
    per = (gp // LANE) // (n_ch // LANE)
    nseg = _scan_segments(s_len)
    nch = nseg // 8
    seg = s_len // nseg
    n_sq = int(math.log2(seg))
    assert 2 ** n_sq == seg
    adj = adjoint_of is not None
    chunk = _tile(s_len, 1024)
    n_chunks = s_len // chunk

    def body(*refs):
        it = iter(refs)
        src_ref, wir_ref, wii_ref, wor_ref, woi_ref, a_ref = (next(it) for _ in range(6))
        if adj:
            u_ref, hr_ref, hi_ref, hin_ref = (next(it) for _ in range(4))
        res_ref = next(it)
        if adj:
            da_ref, dbr_ref, dbi_ref, dcr_ref, dci_ref = (next(it) for _ in range(5))
        else:
            or_ref, oi_ref, oin_ref = (next(it) for _ in range(3))
        if adj:
            or_ref, oi_ref = next(it), next(it)

        for i in range(n_chunks):
            part = src_ref[i * chunk:(i + 1) * chunk, :]
            or_ref[i * chunk:(i + 1) * chunk, :] = jnp.dot(part, wir_ref[...], preferred_element_type=f32)
            oi_ref[i * chunk:(i + 1) * chunk, :] = jnp.dot(part, wii_ref[...], preferred_element_type=f32)

        ar = a_ref[0:1, :]
        ai = -a_ref[1:2, :] if reverse else a_ref[1:2, :]
        arb, aib = jnp.broadcast_to(ar, (8, LANE)), jnp.broadcast_to(ai, (8, LANE))

        def rows(ch, k):
            return pl.ds(pl.multiple_of(k * nseg + ch * 8, 8), 8)

        def advance(h, ch, k):
            hr, hi = h
            return (arb * hr - aib * hi + or_ref[rows(ch, k), :], arb * hi + aib * hr + oi_ref[rows(ch, k), :])

        def kk(n):
            return seg - 1 - n if reverse else n

        zero = jnp.zeros((8, LANE), f32)

        def sweep1(n, hs):
            return tuple(advance(hs[ch], ch, kk(n)) for ch in range(nch))

        ends = lax.fori_loop(0, seg, sweep1, tuple((zero, zero) for _ in range(nch)))

        pr, pi = ar, ai
        for _ in range(n_sq):
            pr, pi = pr * pr - pi * pi, 2.0 * pr * pi
        in_r, in_i = [None] * nseg, [None] * nseg
        cr = ci = jnp.zeros((1, LANE), f32)
        for j in (range(nseg - 1, -1, -1) if reverse else range(nseg)):
            in_r[j], in_i[j] = cr, ci
            er, ei = ends[j // 8][0][j % 8:j % 8 + 1, :], ends[j // 8][1][j % 8:j % 8 + 1, :]
            cr, ci = er + pr * cr - pi * ci, ei + pr * ci + pi * cr
        h0 = tuple((jnp.concatenate(in_r[8 * ch:8 * ch + 8], axis=0), jnp.concatenate(in_i[8 * ch:8 * ch + 8], axis=0))
                   for ch in range(nch))
        if not adj:
            for ch in range(nch):
                oin_ref[0, 8 * ch:8 * ch + 8, :] = h0[ch][0]
                oin_ref[1, 8 * ch:8 * ch + 8, :] = h0[ch][1]

        def emit(ch, k, h):
            or_ref[rows(ch, k), :] = h[0]
            oi_ref[rows(ch, k), :] = h[1]

        def pair(h, p):
            return h[0] * p[0] + h[1] * p[1], h[1] * p[0] - h[0] * p[1]

        def sweep2(n, carry):
            k = kk(n)
            new = tuple(advance(carry[ch], ch, k) for ch in range(nch))
            for ch in range(nch):
                emit(ch, k, new[ch])
            if not adj:
                return new
            dr, di = carry[nch]
            for ch in range(nch):
                qr, qi = pair(new[ch], (hr_ref[rows(ch, k - 1), :], hi_ref[rows(ch, k - 1), :]))
                dr, di = dr + qr, di + qi
            return new + ((dr, di),)

        if adj:
            carry = lax.fori_loop(0, seg - 1, sweep2, h0 + ((zero, zero),))
            dr, di = carry[nch]
            for ch in range(nch):
                new = advance(carry[ch], ch, 0)
                emit(ch, 0, new)
                qr, qi = pair(new, (hin_ref[0, 8 * ch:8 * ch + 8, :], hin_ref[1, 8 * ch:8 * ch + 8, :]))
                dr, di = dr + qr, di + qi
            da_ref[0] = jnp.sum(dr, axis=0, keepdims=True)
            da_ref[1] = jnp.sum(di, axis=0, keepdims=True)
        else:
            lax.fori_loop(0, seg, sweep2, h0)

        first = pl.program_id(0) % per == 0
        for i in range(n_chunks):
            sl = slice(i * chunk, (i + 1) * chunk)
            part = lax.dot_general(or_ref[sl, :].astype(bf16), wor_ref[...], (NT, ((), ())), preferred_element_type=f32)
            part += lax.dot_general(oi_ref[sl, :].astype(bf16), woi_ref[...], (NT, ((), ())),
                                    preferred_element_type=f32)

            @pl.when(first)
            def _():
                res_ref[sl, :] = part

            @pl.when(jnp.logical_not(first))
            def _():
                res_ref[sl, :] += part

        if adj:
            def tn_all(x_ref, y_ref):
                tot = jnp.zeros((LANE, LANE), f32)
                for i in range(n_chunks):
                    sl = slice(i * chunk, (i + 1) * chunk)
                    tot += lax.dot_general(x_ref[sl, :].astype(bf16), y_ref[sl, :].astype(bf16), (TN, ((), ())),
                                           preferred_element_type=f32)
                return tot

            dbr_ref[...] = tn_all(u_ref, or_ref)
            dbi_ref[...] = tn_all(u_ref, oi_ref)
            dcr_ref[...] = tn_all(src_ref, hr_ref)
            dci_ref[...] = tn_all(src_ref, hi_ref)

    col = pl.BlockSpec((s_len, LANE), lambda l: (0, l))
    chan = pl.BlockSpec((s_len, LANE), lambda l: (0, l // per))
    in_spec = pl.BlockSpec((2, nseg, LANE), lambda l: (0, 0, l))
    w_spec = pl.BlockSpec((None, LANE, LANE), lambda l: (l, 0, 0))
    ins = [src, *w_in, *w_out, a2]
    in_specs = [chan, w_spec, w_spec, w_spec, w_spec, pl.BlockSpec((8, LANE), lambda l: (0, l))]
    outs, out_specs = [_sds((s_len, n_ch), f32)], [chan]
    scratch = []
    if adj:
        ins += list(adjoint_of)
        in_specs += [chan, col, col, in_spec]
        outs += [_sds((2, 1, gp), f32)] + [_sds((gp // LANE, LANE, LANE), f32)] * 4
        out_specs += [pl.BlockSpec((2, 1, LANE), lambda l: (0, 0, l))] + [w_spec] * 4
        scratch = [pltpu.VMEM((s_len, LANE), f32)] * 2
    else:
        outs += [_sds((s_len, gp), f32)] * 2 + [_sds((2, nseg, gp), f32)]
        out_specs += [col, col, in_spec]
    res = _pcall(body, name=name, grid=(gp // LANE,), in_specs=in_specs, out_specs=out_specs, out_shape=outs,
                 scratch_shapes=scratch, sem=("arbitrary",))(*ins)
    if adj:
        return res[0], res[1], (res[2], res[3]), (res[4], res[5])
    return res


def _ssm_param_fn(lam_re, lam_im, log_step, b_re2, b_im2, expand):
    step = jnp.exp(log_step)
    xr, xi = lam_re * step, lam_im * step
    mag = jnp.exp(xr)
    ar, ai = mag * jnp.cos(xi), mag * jnp.sin(xi)
    nr, ni = ar - 1.0, ai
    den = lam_re * lam_re + lam_im * lam_im
    cr = (nr * lam_re + ni * lam_im) / den
    ci = (ni * lam_re - nr * lam_im) / den
    cre = jnp.dot(cr, expand, precision=HI, preferred_element_type=f32)
    cie = jnp.dot(ci, expand, precision=HI, preferred_element_type=f32)
    return ar, ai, cre * b_re2 - cie * b_im2, cre * b_im2 + cie * b_re2


def _ssm_params(lam_re, lam_im, log_step, b_re2, b_im2, expand):
    def body(*refs):
        res = _ssm_param_fn(*[r[...] for r in refs[:6]])
        for r, o in zip(refs[6:], res):
            r[...] = o

    g, p = lam_re.shape
    return _pcall(body, name="ssm_params", out_shape=[_sds((g, p), f32)] * 2 + [_sds(b_re2.shape, f32)] * 2)(
        lam_re, lam_im, log_step, b_re2, b_im2, expand)


def _ssm_params_bwd(lam_re, lam_im, log_step, b_re2, b_im2, expand, d_ar, d_ai, d_bbr, d_bbi):
    def body(*refs):
        prim = [r[...] for r in refs[:5]]
        ex = refs[5][...]
        cot = tuple(r[...] for r in refs[6:10])
        _, vjp = jax.vjp(lambda *p_: _ssm_param_fn(*p_, ex), *prim)
        for r, o in zip(refs[10:], vjp(cot)):
            r[...] = o

    shapes = [lam_re.shape, lam_im.shape, log_step.shape, b_re2.shape, b_im2.shape]
    return _pcall(body, name="ssm_params_bwd", out_shape=[_sds(s, f32) for s in shapes])(
        lam_re, lam_im, log_step, b_re2, b_im2, expand, d_ar, d_ai, d_bbr, d_bbi)


def _slope_table(n_heads):
    s = 2.0 ** (-8.0 * (jnp.arange(n_heads, dtype=f32) + 1.0) / n_heads)
    return jnp.broadcast_to(s[:, None, None], (n_heads, 1, LANE))


def _band_bias(slope_d, shift):
    qi = lax.broadcasted_iota(jnp.int32, (BAND, BAND), 0)
    ki = lax.broadcasted_iota(jnp.int32, (BAND, BAND), 1)
    mask = (ki >= qi) if shift else (ki <= qi)
    return jnp.where(mask, -slope_d * (qi - ki + shift).astype(f32), NEG)


def _window_bias(slope_d, has_prev):
    own = _band_bias(slope_d, 0)
    mid = jnp.concatenate([_band_bias(slope_d, BAND), own], axis=1)
    none = jnp.concatenate([jnp.full((BAND, BAND), NEG, f32), own], axis=1)
    return mid, jnp.where(has_prev, mid, none)


def _window_scores(q, k2, bias):
    return lax.dot_general(q, k2, (NT, ((), ())), preferred_element_type=f32) * (HEAD ** -0.5) + bias


def _attn_geometry(s_len, dil, rows=1024):
    piece = BAND * dil
    m = max(1, rows // piece)
    while s_len % (piece * m):
        m //= 2
    return m, piece


def _stream_rows(start, dil):
    return pl.ds(start, BAND, stride=dil) if dil > 1 else pl.ds(start, BAND)


def _attn_fwd(qn, kn, proj, v_blk, dil, slopes):
    s_len, aw = qn.shape
    n_heads = aw // HEAD
    m, piece = _attn_geometry(s_len, dil)
    rows = m * piece

    def body(q_ref, k_ref, kp_ref, v_ref, vp_ref, sl_ref, o_ref, lse_ref):
        bias_mid, bias_first = _window_bias(sl_ref[:, 0:1] * float(dil), pl.program_id(1) > 0)
        for b in range(m):
            for r in range(dil):
                idx = _stream_rows(b * piece + r, dil)
                q, kc, vc = (ref[idx, :].astype(bf16) for ref in (q_ref, k_ref, v_ref))
                if b:
                    pidx = _stream_rows((b - 1) * piece + r, dil)
                    kp, vp = k_ref[pidx, :].astype(bf16), v_ref[pidx, :].astype(bf16)
                else:
                    pidx = _stream_rows(r, dil)
                    kp, vp = kp_ref[pidx, :].astype(bf16), vp_ref[pidx, :].astype(bf16)
                k2, v2 = jnp.concatenate([kp, kc], axis=0), jnp.concatenate([vp, vc], axis=0)
                s = _window_scores(q, k2, bias_mid if b else bias_first)
                mx = jnp.max(s, axis=1, keepdims=True)
                p = jnp.exp(s - mx)
                den = jnp.sum(p, axis=1, keepdims=True)
                o_ref[idx, :] = jnp.dot(p.astype(bf16), v2, preferred_element_type=f32) / den
                lse_ref[idx, :] = jnp.broadcast_to(mx + jnp.log(den), (BAND, HEAD))

    def cur(blk0):
        return pl.BlockSpec((rows, HEAD), lambda h, t: (t, blk0 + h))

    def prev(blk0):
        return pl.BlockSpec((piece, HEAD), lambda h, t: (jnp.maximum(t * m - 1, 0), blk0 + h))

    sl = pl.BlockSpec((None, 1, LANE), lambda h, t: (h, 0, 0))
    return _pcall(body, name=f"attn_fwd_d{dil}", grid=(n_heads, s_len // rows),
                  in_specs=[cur(0), cur(0), prev(0), cur(v_blk), prev(v_blk), sl], out_specs=[cur(0), cur(0)],
                  out_shape=[_sds((s_len, aw), f32)] * 2, sem=("parallel", "parallel"))(
        qn, kn, kn, proj, proj, slopes)


def _attn_bwd(qn, kn, proj, v_blk, do, lse, dd, dil, slopes):
    s_len, aw = qn.shape
    n_heads = aw // HEAD
    m, piece = _attn_geometry(s_len, dil, max(1024, 2 * BAND * dil))
    rows = m * piece
    n_tiles = s_len // rows
    scale = HEAD ** -0.5

    def body(q_ref, qx_ref, k_ref, kp_ref, v_ref, vp_ref, do_ref, dox_ref, l_ref, lx_ref, d_ref, dx_ref, sl_ref,
             dq_ref, dk_ref, dv_ref):
        t = pl.program_id(1)
        slope_d = sl_ref[:, 0:1] * float(dil)
        bias_mid, bias_first = _window_bias(slope_d, t > 0)
        bias_next = _band_bias(slope_d, BAND)

        def query_side(ref_q, ref_do, ref_l, ref_d, idx):
            return (ref_q[idx, :].astype(bf16), ref_do[idx, :].astype(bf16), ref_l[idx, :][:, 0:1],
                    ref_d[idx, :][:, 0:1])

        def probs(qs, keys, values, bias):
            q, do_, l_col, d_col = qs
            p = jnp.exp(_window_scores(q, keys, bias) - l_col)
            dp = lax.dot_general(do_, values, (NT, ((), ())), preferred_element_type=f32)
            return p.astype(bf16), (p * (dp - d_col)).astype(bf16)

        def tn(a_, b_):
            return lax.dot_general(a_, b_, (TN, ((), ())), preferred_element_type=f32)

        for r in range(dil):
            pend = None
            for b in range(m):
                idx = _stream_rows(b * piece + r, dil)
                qs = query_side(q_ref, do_ref, l_ref, d_ref, idx)
                kc, vc = k_ref[idx, :].astype(bf16), v_ref[idx, :].astype(bf16)
                if b:
                    kp, vp = kc_prev, vc_prev
                else:
                    pidx = _stream_rows(r, dil)
                    kp, vp = kp_ref[pidx, :].astype(bf16), vp_ref[pidx, :].astype(bf16)
                k2, v2 = jnp.concatenate([kp, kc], axis=0), jnp.concatenate([vp, vc], axis=0)
                p, ds = probs(qs, k2, v2, bias_mid if b else bias_first)
                dq_ref[idx, :] = jnp.dot(ds, k2, preferred_element_type=f32) * scale
                dk2, dv2 = tn(ds, qs[0]), tn(p, qs[1])
                if pend is not None:
                    dk_ref[pend[0], :] = (pend[1] + dk2[:BAND]) * scale
                    dv_ref[pend[0], :] = pend[2] + dv2[:BAND]
                pend = (idx, dk2[BAND:], dv2[BAND:])
                kc_prev, vc_prev = kc, vc
            qs = query_side(qx_ref, dox_ref, lx_ref, dx_ref, _stream_rows(r, dil))
            p, ds = probs(qs, kc_prev, vc_prev, bias_next)
            live = t < n_tiles - 1
            dk_ref[pend[0], :] = (pend[1] + jnp.where(live, tn(ds, qs[0]), 0.0)) * scale
            dv_ref[pend[0], :] = pend[2] + jnp.where(live, tn(p, qs[1]), 0.0)

    def cur(blk0):
        return pl.BlockSpec((rows, HEAD), lambda h, t: (t, blk0 + h))

    def prev(blk0):
        return pl.BlockSpec((piece, HEAD), lambda h, t: (jnp.maximum(t * m - 1, 0), blk0 + h))

    def nxt(blk0):
        return pl.BlockSpec((piece, HEAD), lambda h, t: (jnp.minimum(t * m + m, n_tiles * m - 1), blk0 + h))

    sl = pl.BlockSpec((None, 1, LANE), lambda h, t: (h, 0, 0))
    return _pcall(body, name=f"attn_bwd_d{dil}", grid=(n_heads, n_tiles),
                  in_specs=[cur(0), nxt(0), cur(0), prev(0), cur(v_blk), prev(v_blk), cur(0), nxt(0), cur(0), nxt(0),
                            cur(0), nxt(0), sl],
                  out_specs=[cur(0)] * 3, out_shape=[_sds((s_len, aw), f32)] * 3,
                  sem=("parallel", "parallel"))(qn, qn, kn, kn, proj, proj, do, do, lse, lse, dd, dd, slopes)


def _exchange(name, srcs, scatter):
    n = len(srcs)

    def body(*refs):
        src, out = refs[:n], refs[n:2 * n]
        send_sems, recv_sems, local_sems = refs[2 * n:]
        x, y, c = lax.axis_index("x"), lax.axis_index("y"), lax.axis_index("c")
        me = 4 * x + 2 * y + c

        def peer(r):
            return ((1 - x) if r & 4 else x, (1 - y) if r & 2 else y, (1 - c) if r & 1 else c)

        def lin(p):
            return 4 * p[0] + 2 * p[1] + p[2]

        def piece(a, idx):
            return src[a].at[idx] if scatter[a] else src[a]

        local, sends = [], []
        for a in range(n):
            cp = pltpu.make_async_copy(piece(a, me), out[a].at[me], local_sems.at[a])
            cp.start()
            local.append(cp)
        for r in range(1, N_DEV):
            p = peer(r)
            for a in range(n):
                cp = pltpu.make_async_remote_copy(src_ref=piece(a, lin(p)), dst_ref=out[a].at[me],
                                                  send_sem=send_sems.at[a, r - 1], recv_sem=recv_sems.at[a, r - 1],
                                                  device_id=p, device_id_type=MESH)
                cp.start()
                sends.append(cp)
        for r in range(1, N_DEV):
            p = peer(r)
            for a in range(n):
                pltpu.make_async_remote_copy(src_ref=piece(a, lin(p)), dst_ref=out[a].at[lin(p)],
                                             send_sem=send_sems.at[a, r - 1], recv_sem=recv_sems.at[a, r - 1],
                                             device_id=p, device_id_type=MESH).wait_recv()
        for cp in sends:
            cp.wait_send()
        for cp in local:
            cp.wait()

    def piece_shape(a):
        return srcs[a].shape[1:] if scatter[a] else srcs[a].shape

    any_spec = pl.BlockSpec(memory_space=pl.ANY)
    return _pcall(body, name=name, in_specs=[any_spec] * n, out_specs=[any_spec] * n,
                  out_shape=[_sds((N_DEV, *piece_shape(a)), srcs[a].dtype) for a in range(n)],
                  scratch_shapes=[pltpu.SemaphoreType.DMA((n, N_DEV - 1)), pltpu.SemaphoreType.DMA((n, N_DEV - 1)),
                                  pltpu.SemaphoreType.DMA((n,))])(*srcs)


_HBM = pl.BlockSpec(memory_space=pltpu.HBM)
_SEM = pl.BlockSpec(memory_space=pltpu.SEMAPHORE)
_EFFECT = pltpu.SideEffectType.DATAFLOW_SIDE_EFFECTING


def _peer_ids():
    x, y, c = lax.axis_index("x"), lax.axis_index("y"), lax.axis_index("c")
    peers = [((1 - x) if r & 4 else x, (1 - y) if r & 2 else y, (1 - c) if r & 1 else c) for r in range(1, N_DEV)]
    return 4 * x + 2 * y + c, peers, [4 * p[0] + 2 * p[1] + p[2] for p in peers]


def _exchange_start(name, src, scatter, after):
    piece_shape = src.shape[1:] if scatter else src.shape

    def body(src_ref, land_ref, after_ref, send_sems, recv_sems, local_sem, src_thru, land_thru, token):
        me, peers, lins = _peer_ids()

        def piece(idx):
            return src_ref.at[idx] if scatter else src_ref

        pltpu.make_async_copy(piece(me), land_ref.at[me], local_sem).start()
        for r, (p, lp) in enumerate(zip(peers, lins)):
            pltpu.make_async_remote_copy(src_ref=piece(lp), dst_ref=land_ref.at[me], send_sem=send_sems.at[r],
                                         recv_sem=recv_sems.at[r], device_id=p, device_id_type=MESH).start()
        token[...] = jnp.zeros_like(token)

    land = pltpu.with_memory_space_constraint(lax.empty((N_DEV, *piece_shape), src.dtype), pltpu.HBM)
    send_sems, recv_sems, local_sem, src_thru, land_thru, token = pl.pallas_call(
        body, name=name,
        out_shape=(pltpu.SemaphoreType.DMA((N_DEV - 1,)), pltpu.SemaphoreType.DMA((N_DEV - 1,)),
                   pltpu.SemaphoreType.DMA(()), pltpu.HBM(src.shape, src.dtype),
                   pltpu.HBM((N_DEV, *piece_shape), src.dtype), _sds((8, LANE), f32)),
        in_specs=(_HBM, _HBM, pl.BlockSpec(memory_space=pl.ANY)),
        out_specs=(_SEM, _SEM, _SEM, _HBM, _HBM, pl.BlockSpec(memory_space=pltpu.VMEM)),
        input_output_aliases={0: 3, 1: 4},
        compiler_params=pltpu.CompilerParams(has_side_effects=_EFFECT),
    )(pltpu.with_memory_space_constraint(src, pltpu.HBM), land, after)
    return (send_sems, recv_sems, local_sem, src_thru, land_thru, scatter), token


def _exchange_wait(name, handle, after):
    send_sems, recv_sems, local_sem, src_thru, land_thru, scatter = handle

    def body(src_ref, land_ref, send_sems_, recv_sems_, local_sem_, after_ref, src_dead, got_ref):
        me, peers, lins = _peer_ids()

        def piece(idx):
            return src_ref.at[idx] if scatter else src_ref

        pltpu.make_async_copy(piece(me), land_ref.at[me], local_sem_).wait()
        for r, (p, lp) in enumerate(zip(peers, lins)):
            pltpu.make_async_remote_copy(src_ref=piece(lp), dst_ref=land_ref.at[me], send_sem=send_sems_.at[r],
                                         recv_sem=recv_sems_.at[r], device_id=p, device_id_type=MESH).wait_send()
            pltpu.make_async_remote_copy(src_ref=piece(lp), dst_ref=land_ref.at[lp], send_sem=send_sems_.at[r],
                                         recv_sem=recv_sems_.at[r], device_id=p, device_id_type=MESH).wait_recv()

    return pl.pallas_call(
        body, name=name,
        out_shape=(pltpu.HBM(src_thru.shape, src_thru.dtype), pltpu.HBM(land_thru.shape, land_thru.dtype)),
        in_specs=(_HBM, _HBM, _SEM, _SEM, _SEM, pl.BlockSpec(memory_space=pl.ANY)), out_specs=(_HBM, _HBM),
        input_output_aliases={0: 0, 1: 1},
        compiler_params=pltpu.CompilerParams(has_side_effects=_EFFECT),
    )(src_thru, land_thru, send_sems, recv_sems, local_sem, after)[1]


def _adamw(name, w, m, v, g_or_stack, stacked, rows=256):
    r, c = w.shape
    tr = _tile(r, rows)

    def fn(w_, m_, v_, g_):
        if stacked:
            g = g_[0].astype(f32)
            for j in range(1, N_DEV):
                g = g + g_[j].astype(f32)
        else:
            g = g_
        m_new = ADAM_B1 * m_ + (1.0 - ADAM_B1) * g
        v_new = ADAM_B2 * v_ + (1.0 - ADAM_B2) * (g * g)
        m_hat = m_new / (1.0 - ADAM_B1 ** ADAM_STEP)
        v_hat = v_new / (1.0 - ADAM_B2 ** ADAM_STEP)
        delta = -ADAM_LR * (m_hat / (jnp.sqrt(v_hat) + ADAM_EPS) + ADAM_WD * w_)
        return g, delta, m_new, v_new

    blk = _row(tr, c)
    g_spec = pl.BlockSpec((N_DEV, tr, c), lambda i: (0, i, 0)) if stacked else blk
    return _rowwise(name, fn, r // tr, [w, m, v, g_or_stack], [blk, blk, blk, g_spec],
                    [_sds((r, c), f32)] * 4, [blk] * 4, [False] * 4)


def _ada_fwd(c_all, w_shard, b_shard):
    nb_, d = c_all.shape
    n = w_shard.shape[1]
    tn = _tile(n, 512)

    def body(c_ref, w_ref, b_ref, o_ref):
        a = jax.nn.silu(c_ref[...]).astype(bf16)
        o_ref[...] = jnp.dot(a, w_ref[...].astype(bf16), preferred_element_type=f32) + b_ref[...]

    return _pcall(body, name="ada_fwd", grid=(n // tn,),
                  in_specs=[pl.BlockSpec((nb_, d), lambda j: (0, 0)), pl.BlockSpec((d, tn), lambda j: (0, j)),
                            pl.BlockSpec((1, tn), lambda j: (0, j))],
                  out_specs=pl.BlockSpec((nb_, tn), lambda j: (0, j)), out_shape=_sds((nb_, n), f32),
                  sem=("parallel",))(c_all, w_shard, b_shard)


def _ada_bwd(c_all, dmod_cols):
    nb_, d = c_all.shape
    n = dmod_cols.shape[1]
    tn = _tile(n, 512)

    def body(c_ref, g_ref, o_ref):
        a = jax.nn.silu(c_ref[...]).astype(bf16).astype(f32)
        g = g_ref[...].astype(bf16).astype(f32)
        o_ref[...] = lax.dot_general(a, g, (TN, ((), ())), precision=HI, preferred_element_type=f32)

    return _pcall(body, name="ada_bwd", grid=(n // tn,),
                  in_specs=[pl.BlockSpec((nb_, d), lambda j: (0, 0)), pl.BlockSpec((nb_, tn), lambda j: (0, j))],
                  out_specs=pl.BlockSpec((d, tn), lambda j: (0, j)), out_shape=_sds((d, n), f32),
                  sem=("parallel",))(c_all, dmod_cols)


SMALL_LATE = ("b_ada", "norm1_g", "q_norm_g", "k_norm_g")
SMALL_EARLY = ("lam_re", "lam_im", "log_step", "b_re", "b_im", "c_re", "c_im", "d_skip", "b_glu", "attn_out_g",
               "ssm_out_g", "norm2_g")
ORDER = ("w_ada", "b_ada", "norm1_g", "w_in", "q_norm_g", "k_norm_g", "lam_re", "lam_im", "log_step", "b_re", "b_im",
         "c_re", "c_im", "d_skip", "w_glu", "b_glu", "attn_out_g", "ssm_out_g", "w_out", "norm2_g", "w_ff1", "w_ff2")


def _pack(parts):
    flat = jnp.concatenate([p.reshape(-1) for p in parts])
    pad = (-flat.shape[0]) % (8 * LANE)
    return jnp.pad(flat, (0, pad)).reshape(-1, LANE)


def kernel(x, c, w_ada, b_ada, norm1_g, w_in, q_norm_g, k_norm_g, lam_re, lam_im, log_step, b_re, b_im, c_re, c_im, d_skip, w_glu, b_glu, attn_out_g, ssm_out_g, w_out, norm2_g, w_ff1, w_ff2, loss_target, m_w_ada, m_b_ada, m_norm1_g, m_w_in, m_q_norm_g, m_k_norm_g, m_lam_re, m_lam_im, m_log_step, m_b_re, m_b_im, m_c_re, m_c_im, m_d_skip, m_w_glu, m_b_glu, m_attn_out_g, m_ssm_out_g, m_w_out, m_norm2_g, m_w_ff1, m_w_ff2, v_w_ada, v_b_ada, v_norm1_g, v_w_in, v_q_norm_g, v_k_norm_g, v_lam_re, v_lam_im, v_log_step, v_b_re, v_b_im, v_c_re, v_c_im, v_d_skip, v_w_glu, v_b_glu, v_attn_out_g, v_ssm_out_g, v_w_out, v_norm2_g, v_w_ff1, v_w_ff2):
    env = dict(locals())
    wts = {n: env[n] for n in ORDER}
    mom = {n: env["m_" + n] for n in ORDER}
    var = {n: env["v_" + n] for n in ORDER}

    xs, tgt = x[0], loss_target[0]
    s_len, d = xs.shape
    aw = d // 2
    sw = d - aw
    n_heads = aw // HEAD
    n_groups = sw // SSM_GROUP
    n_state = lam_re.shape[-1]
    gp = n_groups * n_state
    tm = _tile(s_len, 256)
    steps = s_len // tm
    me = 4 * lax.axis_index("x") + 2 * lax.axis_index("y") + lax.axis_index("c")

    (c_all,) = _exchange("gather_c", [c], [False])
    c_all = c_all.reshape(N_DEV, d)

    n_ada = w_ada.shape[-1]
    b_ada_cols = lax.dynamic_slice_in_dim(b_ada, me * n_ada, n_ada, axis=1)
    mod_part = _ada_fwd(c_all, w_ada[0], b_ada_cols)
    (mod_all,) = _exchange("gather_mod", [mod_part], [False])
    mod = lax.dynamic_index_in_dim(mod_all, me, axis=1, keepdims=False).reshape(1, 6 * d)
    sh1, sc1, g1, sh2, sc2, g2 = (mod[:, i * d:(i + 1) * d] for i in range(6))

    gather, started = {}, jnp.zeros((1, 1), f32)
    for name in ("w_in", "w_glu", "w_out", "w_ff1", "w_ff2"):
        gather[name], token = _exchange_start("gather_" + name, wts[name][0].astype(bf16), False, mod_all)
        started = started + token[0:1, 0:1]
    sc1 = sc1 + started

    (h,) = _rowwise("norm1", _norm_mod, steps, [xs, norm1_g, sc1, sh1],
                    [_row(tm, d), _vec(d), _vec(d), _vec(d)], [_sds((s_len, d), bf16)], [_row(tm, d)], [False])
    win_g = _exchange_wait("gathered_w_in", gather["w_in"], h)
    (proj,) = _mm_nn_sharded("in_proj", h, win_g)

    def qk_fn(q, k, gq, gk):
        return _head_rms(q, gq), _head_rms(k, gk)

    qn, kn = _rowwise("qk_norm", qk_fn, steps, [proj, proj, q_norm_g, k_norm_g],
                      [_row(tm, aw, 0), _row(tm, aw, 1), _vec(HEAD), _vec(HEAD)],
                      [_sds((s_len, aw), f32)] * 2, [_row(tm, aw)] * 2, [False] * 2)
    v_blk = 2 * aw // HEAD

    slopes = _slope_table(n_heads)
    pat = [_attn_fwd(qn, kn, proj, v_blk, dil, slopes) for _, dil in DILATIONS]

    def attn_mix_fn(o1, l1, o2, l2, o3, l3):
        m = jnp.maximum(jnp.maximum(l1, l2), l3)
        e1, e2, e3 = jnp.exp(l1 - m), jnp.exp(l2 - m), jnp.exp(l3 - m)
        tot = e1 + e2 + e3
        return (e1 * o1 + e2 * o2 + e3 * o3) / tot, m + jnp.log(tot)

    attn, lse = _rowwise("attn_mix", attn_mix_fn, steps, [t for ol in pat for t in ol], [_row(tm, aw)] * 6,
                         [_sds((s_len, aw), f32)] * 2, [_row(tm, aw)] * 2, [False] * 2)

    lam_re2, lam_im2 = lam_re[0], lam_im[0]
    log_step2 = log_step[0].reshape(n_groups, 1)
    b_re2 = b_re[0].reshape(n_groups, n_state * SSM_GROUP)
    b_im2 = b_im[0].reshape(n_groups, n_state * SSM_GROUP)
    expand = jnp.repeat(jnp.eye(n_state, dtype=f32), SSM_GROUP, axis=1)
    a_re, a_im, bb_re2, bb_im2 = _ssm_params(lam_re2, lam_im2, log_step2, b_re2, b_im2, expand)
    a2 = jnp.zeros((8, gp), f32).at[0].set(a_re.reshape(gp)).at[1].set(a_im.reshape(gp))

    w_bu = tuple(_lane_block_weights(t.reshape(n_groups, n_state, SSM_GROUP).transpose(0, 2, 1), n_state)
                 for t in (bb_re2, bb_im2))
    w_c = (_lane_block_weights(c_re[0], n_state), _lane_block_weights(-c_im[0], n_state))

    nseg = _scan_segments(s_len)
    u_seg = _to_segments(proj[:, 3 * aw:], nseg).astype(bf16)
    y_seg, h_re, h_im, hin_f = _scan("ssm_scan", u_seg, w_bu, w_c, a2, reverse=False)
    ymm = _from_segments(y_seg, nseg)

    u_spec = _row(tm, sw, 3 * aw // sw)
    (yg,) = _rowwise("ssm_gelu", _ypre_fn, steps, [ymm, proj, d_skip], [_row(tm, sw), u_spec, _vec(sw)],
                     [_sds((s_len, sw), f32)], [_row(tm, sw)], [False])
    wglu_g = _exchange_wait("gathered_w_glu", gather["w_glu"], yg).reshape(sw, sw)
    (z,) = _mm_nn("glu_proj", yg, wglu_g)
    (cat,) = _rowwise("mix_norm", _mix_fn, steps, [attn, yg, z, b_glu, attn_out_g, ssm_out_g],
                      [_row(tm, aw), _row(tm, sw), _row(tm, sw), _vec(sw), _vec(aw), _vec(sw)],
                      [_sds((s_len, d), bf16)], [_row(tm, d)], [False])
    wout_g = _exchange_wait("gathered_w_out", gather["w_out"], cat).reshape(d, d)
    (mixed,) = _mm_nn("out_proj", cat, wout_g)

    def res_norm2_fn(x_, mixed_, g1_, gn, sc, sh):
        x1_ = x_ + g1_ * mixed_
        return x1_, _norm_mod(x1_, gn, sc, sh)

    x1, h2 = _rowwise("norm2", res_norm2_fn, steps, [xs, mixed, g1, norm2_g, sc2, sh2],
                      [_row(tm, d), _row(tm, d)] + [_vec(d)] * 4,
                      [_sds((s_len, d), f32), _sds((s_len, d), bf16)], [_row(tm, d)] * 2, [False] * 2)

    def act_epilogue(acc):
        r = jnp.maximum(acc, 0.0)
        return r, r * r

    wff1_g = _exchange_wait("gathered_w_ff1", gather["w_ff1"], h2)
    r_ff, act = _mm_nn_sharded("ff1", h2, wff1_g, epilogue=act_epilogue,
                               outs=[_sds((s_len, 4 * d), bf16), _sds((s_len, 4 * d), bf16)])
    wff2_g = _exchange_wait("gathered_w_ff2", gather["w_ff2"], act).reshape(4 * d, d)
    (ff,) = _mm_nn("ff2", act, wff2_g)

    def loss_fn(x1_, ff_, tgt_, g2_):
        e = x1_ + g2_ * ff_ - tgt_
        dy_ = e * (1.0 / d)
        part = jnp.full((1, LANE), 0.5 / d, f32) * jnp.sum(e * e)
        return dy_, g2_ * dy_, part, jnp.sum(dy_ * ff_, axis=0, keepdims=True)

    dy, dff, loss_part, d_g2 = _rowwise(
        "loss", loss_fn, steps, [x1, ff, tgt, g2], [_row(tm, d)] * 3 + [_vec(d)],
        [_sds((s_len, d), f32), _sds((s_len, d), bf16), _sds((1, LANE), f32), _sds((1, d), f32)],
        [_row(tm, d), _row(tm, d), _vec(LANE), _vec(d)], [False, False, True, True])
    loss = lax.psum(loss_part[0, 0], ("x", "y", "c"))

    def dact_epilogue(acc, r_):
        return (acc * (2.0 * r_.astype(f32)),)

    (da,) = _mm_nt("ff2_dx", dff, wff2_g, epilogue=dact_epilogue, extra=[r_ff], outs=[_sds((s_len, 4 * d), bf16)])
    scatter = {}
    g_wff2 = _mm_tn("ff2_dw", act, dff, after=loss.reshape(1, 1)).reshape(N_DEV, 4 * d // N_DEV, d)
    scatter["w_ff2"], tok_ff2 = _exchange_start("scatter_w_ff2", g_wff2, True, loss.reshape(1, 1))
    dh2 = _mm_nt("ff1_dx", da, wff1_g.transpose(1, 0, 2).reshape(d, 4 * d))[0]
    g_wff1 = _mm_tn_sharded("ff1_dw", h2, da, N_DEV)
    scatter["w_ff1"], tok_ff1 = _exchange_start("scatter_w_ff1", g_wff1, True, started)
    norm2_g_t = norm2_g + (tok_ff2[0:1, 0:1] + tok_ff1[0:1, 0:1])

    def norm2_bwd_fn(dh2_, x1_, dy_, mixed_, gn, sc, sh, g1_):
        _, vjp = jax.vjp(_norm_mod, x1_, gn, sc, sh)
        dx, dgn, dsc, dsh = vjp(dh2_)
        dx1_ = dy_ + dx
        return dx1_, g1_ * dx1_, dgn, dsc, dsh, jnp.sum(dx1_ * mixed_, axis=0, keepdims=True)

    dx1, dmixed, d_norm2_g, d_sc2, d_sh2, d_g1 = _rowwise(
        "norm2_bwd", norm2_bwd_fn, steps, [dh2, x1, dy, mixed, norm2_g_t, sc2, sh2, g1],
        [_row(tm, d)] * 4 + [_vec(d)] * 4,
        [_sds((s_len, d), f32), _sds((s_len, d), bf16)] + [_sds((1, d), f32)] * 4,
        [_row(tm, d)] * 2 + [_vec(d)] * 4, [False, False, True, True, True, True])

    (dcat,) = _mm_nt("out_dx", dmixed, wout_g)
    g_wout = _mm_tn("out_dw", cat, dmixed).reshape(N_DEV, d // N_DEV, d)
    scatter["w_out"], tok_out = _exchange_start("scatter_w_out", g_wout, True, started)
    b_glu_t = b_glu + tok_out[0:1, 0:1]

    def mix_bwd_fn(dcat_, attn_, yg_, z_, bglu, ga, gs):
        _, vjp = jax.vjp(_mix_fn, attn_, yg_, z_, bglu, ga, gs)
        dattn_, dyg_, dz_, dbglu, dga, dgs = vjp(dcat_)
        prod = dattn_ * attn_
        dd_ = jnp.concatenate([jnp.broadcast_to(jnp.sum(prod[:, i * HEAD:(i + 1) * HEAD], axis=1, keepdims=True),
                                                (prod.shape[0], HEAD)) for i in range(n_heads)], axis=1)
        return dattn_, dd_, dyg_, dz_, dbglu, dga, dgs

    dattn, dd, dyg1, dz, d_b_glu, d_attn_out_g, d_ssm_out_g = _rowwise(
        "mix_bwd", mix_bwd_fn, steps, [dcat, attn, yg, z, b_glu_t, attn_out_g, ssm_out_g],
        [_row(tm, d), _row(tm, aw), _row(tm, sw), _row(tm, sw), _vec(sw), _vec(aw), _vec(sw)],
        [_sds((s_len, aw), f32), _sds((s_len, aw), f32), _sds((s_len, sw), f32), _sds((s_len, sw), bf16),
         _sds((1, sw), f32), _sds((1, aw), f32), _sds((1, sw), f32)],
        [_row(tm, aw), _row(tm, aw), _row(tm, sw), _row(tm, sw), _vec(sw), _vec(aw), _vec(sw)],
        [False] * 4 + [True] * 3)

    (dyg2,) = _mm_nt("glu_dx", dz, wglu_g)
    g_wglu = _mm_tn("glu_dw", yg, dz).reshape(N_DEV, sw // N_DEV, sw)
    scatter["w_glu"], tok_glu = _exchange_start("scatter_w_glu", g_wglu, True, started)
    d_skip_t = d_skip + tok_glu[0:1, 0:1]

    def gelu_bwd_fn(dyg1_, dyg2_, ymm_, u_, dskip):
        _, vjp = jax.vjp(_ypre_fn, ymm_, u_, dskip)
        dymm, du_, ddskip = vjp(dyg1_ + dyg2_)
        return dymm, du_, ddskip

    dymm, du_skip, d_d_skip = _rowwise(
        "ssm_gelu_bwd", gelu_bwd_fn, steps, [dyg1, dyg2, ymm, proj, d_skip_t],
        [_row(tm, sw)] * 3 + [u_spec, _vec(sw)],
        [_sds((s_len, sw), f32), _sds((s_len, sw), f32), _sds((1, sw), f32)],
        [_row(tm, sw), _row(tm, sw), _vec(sw)], [False, False, True])

    dymm_seg = _to_segments(dymm, nseg).astype(bf16)
    du_seg, da_seg, dbb_c, dc_c = _scan("ssm_adj", dymm_seg, w_c, w_bu, a2, reverse=True,
                                        adjoint_of=(u_seg, h_re, h_im, hin_f))
    du_ssm = _from_segments(du_seg, nseg)

    def to_gpi(w):
        return _lane_block_diag(w, n_state).transpose(0, 2, 1).reshape(n_groups, n_state * SSM_GROUP)

    d_lam_re, d_lam_im, d_log_step, d_b_re2, d_b_im2 = _ssm_params_bwd(
        lam_re2, lam_im2, log_step2, b_re2, b_im2, expand,
        da_seg[0, 0].reshape(n_groups, n_state), da_seg[1, 0].reshape(n_groups, n_state),
        to_gpi(dbb_c[0]), to_gpi(dbb_c[1]))
    d_c_re = _lane_block_diag(dc_c[0], n_state)
    d_c_im = -_lane_block_diag(dc_c[1], n_state)

    grads_qkv = [_attn_bwd(qn, kn, proj, v_blk, dattn, lse, dd, dil, slopes) for _, dil in DILATIONS]

    def qkv_bwd_fn(q, k, gq, gk, dq1, dq2, dq3, dk1, dk2, dk3, dv1, dv2, dv3, du1, du2):
        _, vjp = jax.vjp(lambda q_, k_, gq_, gk_: (_head_rms(q_, gq_), _head_rms(k_, gk_)), q, k, gq, gk)
        dq, dk, dgq, dgk = vjp((dq1 + dq2 + dq3, dk1 + dk2 + dk3))
        return jnp.concatenate([dq, dk, dv1 + dv2 + dv3, du1 + du2], axis=1), dgq, dgk

    small_g = {"lam_re": d_lam_re, "lam_im": d_lam_im, "log_step": d_log_step, "b_re": d_b_re2, "b_im": d_b_im2,
               "c_re": d_c_re, "c_im": d_c_im, "d_skip": d_d_skip, "b_glu": d_b_glu,
               "attn_out_g": d_attn_out_g, "ssm_out_g": d_ssm_out_g, "norm2_g": d_norm2_g}
    early, tok_early = _exchange_start("gather_early_grads", _pack([small_g[n] for n in SMALL_EARLY]), False, started)

    qkv_cots = [grads_qkv[p][i] for i in range(3) for p in range(3)]
    dproj, small_g["q_norm_g"], small_g["k_norm_g"] = _rowwise(
        "qk_norm_bwd", qkv_bwd_fn, steps,
        [proj, proj, q_norm_g + tok_early[0:1, 0:1], k_norm_g, *qkv_cots, du_skip, du_ssm],
        [_row(tm, aw, 0), _row(tm, aw, 1), _vec(HEAD), _vec(HEAD)] + [_row(tm, aw)] * 9 + [_row(tm, sw)] * 2,
        [_sds((s_len, 3 * aw + sw), bf16), _sds((1, HEAD), f32), _sds((1, HEAD), f32)],
        [_row(tm, 3 * aw + sw), _vec(HEAD), _vec(HEAD)], [False, True, True])

    g_win = _mm_tn_sharded("in_dw", h, dproj, N_DEV)
    scatter["w_in"], tok_in = _exchange_start("scatter_w_in", g_win, True, tok_early)
    dh = _mm_nt("in_dx", dproj, win_g.transpose(1, 0, 2).reshape(d, 3 * aw + sw), after=tok_in)[0]
    norm1_g_t = norm1_g + tok_in[0:1, 0:1]

    def norm1_bwd_fn(dh_, x_, dx1_, gn, sc, sh):
        _, vjp = jax.vjp(_norm_mod, x_, gn, sc, sh)
        dx, dgn, dsc, dsh = vjp(dh_)
        return dx1_ + dx, dgn, dsc, dsh

    grad_x, d_norm1_g, d_sc1, d_sh1 = _rowwise(
        "norm1_bwd", norm1_bwd_fn, steps, [dh, xs, dx1, norm1_g_t, sc1, sh1], [_row(tm, d)] * 3 + [_vec(d)] * 3,
        [_sds((s_len, d), f32)] + [_sds((1, d), f32)] * 3, [_row(tm, d)] + [_vec(d)] * 3,
        [False, True, True, True])

    small_g["b_ada"] = jnp.concatenate([d_sh1, d_sc1, d_g1, d_sh2, d_sc2, d_g2], axis=1)
    small_g["norm1_g"] = d_norm1_g
    (r_late,) = _exchange("gather_late_grads", [_pack([small_g[n] for n in SMALL_LATE])], [False])

    res = {}
    dmod_all = r_late.reshape(N_DEV, -1)[:, :6 * d]
    g_wada = _ada_bwd(c_all, lax.dynamic_slice_in_dim(dmod_all, me * n_ada, n_ada, axis=1))
    res["w_ada"] = _adamw("adamw_w_ada", w_ada[0], m_w_ada[0], v_w_ada[0], g_wada, False)
    after = res["w_ada"][1]
    for name in ("w_ff2", "w_ff1", "w_out", "w_glu", "w_in"):
        stack = _exchange_wait("scattered_" + name, scatter[name], after)
        res[name] = _adamw("adamw_" + name, wts[name][0], mom[name][0], var[name][0], stack, True)
        after = res[name][1]
    r_early = _exchange_wait("gathered_early_grads", early, after)
    for label, names, stack in (("late", SMALL_LATE, r_late), ("early", SMALL_EARLY, r_early)):
        small_res = _adamw("adamw_small_" + label, _pack([wts[n] for n in names]), _pack([mom[n] for n in names]),
                           _pack([var[n] for n in names]), stack, True, rows=4096)
        off = 0
        for n in names:
            size = wts[n].size
            res[n] = [t.reshape(-1)[off:off + size] for t in small_res]
            off += size

    out = [loss, grad_x[None]]
    for i in range(4):
        out += [res[n][i].reshape(wts[n].shape) for n in ORDER]
    return tuple(out)
```

```python
import math

import jax
import jax.numpy as jnp
from jax import lax
from jax.experimental import pallas as pl
from jax.experimental.pallas import tpu as pltpu

f32, bf16 = jnp.float32, jnp.bfloat16

N_DEV = 8
LANE = 128
HEAD = 128
SSM_GROUP = 16
DILATIONS = ((128, 1), (512, 4), (2048, 16))
BAND = 128
EPS = 1e-6
ADAM_LR, ADAM_B1, ADAM_B2, ADAM_EPS, ADAM_WD, ADAM_STEP = 0.001, 0.9, 0.999, 1e-08, 0.01, 10
NEG = -1e30
VMEM_LIMIT = 60 * 1024 * 1024
HI = lax.Precision.HIGHEST
MESH = pl.DeviceIdType.MESH


def _pcall(body, **kw):
    sem = kw.pop("sem", None)
    kw["compiler_params"] = pltpu.CompilerParams(dimension_semantics=sem, vmem_limit_bytes=VMEM_LIMIT)
    return pl.pallas_call(body, **kw)


def _tile(n, pref):
    t = min(n, pref)
    while n % t:
        t //= 2
    return t


def _sds(shape, dtype):
    return jax.ShapeDtypeStruct(shape, dtype)


def _rowwise(name, fn, steps, ins, in_specs, outs, out_specs, acc):
    n_in = len(ins)

    def body(*refs):
        res = fn(*[r[...] for r in refs[:n_in]])
        res = res if isinstance(res, (tuple, list)) else (res,)
        for r, o, a in zip(refs[n_in:], res, acc):
            if a:
                @pl.when(pl.program_id(0) == 0)
                def _():
                    r[...] = jnp.zeros_like(r)
                r[...] += o
            else:
                r[...] = o.astype(r.dtype)

    return _pcall(body, name=name, grid=(steps,), in_specs=in_specs, out_specs=out_specs, out_shape=outs,
                  sem=("arbitrary",))(*ins)


def _row(tm, c, blk=0):
    return pl.BlockSpec((tm, c), lambda i: (i, blk))


def _vec(c, blk=0):
    return pl.BlockSpec((1, c), lambda i: (0, blk))


def _rms(x, g):
    return x * lax.rsqrt(jnp.mean(x * x, axis=-1, keepdims=True) + EPS) * g


def _norm_mod(x, g, sc, sh):
    return _rms(x, g) * (1.0 + sc) + sh


def _head_rms(t, g):
    return jnp.concatenate([_rms(t[:, h * HEAD:(h + 1) * HEAD], g) for h in range(t.shape[1] // HEAD)], axis=1)


def _mix_fn(attn, yg, z, bglu, ga, gs):
    ssm = yg * jax.nn.sigmoid(z + bglu)
    return jnp.concatenate([_rms(attn, ga), _rms(ssm, gs)], axis=1)


def _ypre_fn(ymm, u, dskip):
    return jax.nn.gelu(ymm + dskip * u)


def _matmul(name, a, b, *, dims, grid, a_spec, b_spec, acc_shape, outs, out_specs, extra=(), extra_specs=(),
            epilogue=None, after=None):
    gk = grid[2]
    n_x = len(extra)
    placed = [] if after is None else [after]
    first_out = n_x + len(placed)
    ins = [a, b, *extra, *placed]
    in_specs = [a_spec, b_spec, *extra_specs] + [pl.BlockSpec(memory_space=pl.ANY)] * len(placed)

    def product(a_ref, b_ref):
        return lax.dot_general(a_ref[...].astype(bf16), b_ref[...].astype(bf16), (dims, ((), ())),
                               preferred_element_type=f32)

    def finish(res, x_refs, o_refs):
        res = epilogue(res, *[r[...] for r in x_refs]) if epilogue is not None else (res,)
        for r, o in zip(o_refs, res):
            r[...] = o.astype(r.dtype)

    def body_single(a_ref, b_ref, *rest):
        finish(product(a_ref, b_ref), rest[:n_x], rest[first_out:])

    def body_pair(a_ref, b_ref, *rest):
        acc = rest[-1]
        prod = product(a_ref, b_ref)

        @pl.when(pl.program_id(2) == 0)
        def _():
            acc[...] = prod

        @pl.when(pl.program_id(2) == 1)
        def _():
            finish(acc[...] + prod, rest[:n_x], rest[first_out:-1])

    def body(a_ref, b_ref, *rest):
        acc = rest[-1]
        k = pl.program_id(2)

        @pl.when(k == 0)
        def _():
            acc[...] = jnp.zeros_like(acc)

        acc[...] += product(a_ref, b_ref)

        @pl.when(k == gk - 1)
        def _():
            finish(acc[...], rest[:n_x], rest[first_out:-1])

    sem = ("parallel", "parallel", "arbitrary")
    if gk == 1:
        return _pcall(body_single, name=name, grid=grid, in_specs=in_specs, out_specs=out_specs, out_shape=outs,
                      sem=sem)(*ins)
    return _pcall(body_pair if gk == 2 else body, name=name, grid=grid, in_specs=in_specs, out_specs=out_specs,
                  out_shape=outs, scratch_shapes=[pltpu.VMEM(acc_shape, f32)], sem=sem)(*ins)


NN = ((1,), (0,))
NT = ((1,), (1,))
TN = ((0,), (0,))


def _mm_nn(name, a, b, out_dtype=f32, tm=1024, tn=1024, tk=2048, epilogue=None, extra=(), outs=None):
    m, kd = a.shape
    n = b.shape[1]
    tm, tn, tk = _tile(m, tm), _tile(n, tn), _tile(kd, tk)
    o_spec = pl.BlockSpec((tm, tn), lambda i, j, k: (i, j))
    outs = outs if outs is not None else [_sds((m, n), out_dtype)]
    return _matmul(name, a, b, dims=NN, grid=(m // tm, n // tn, kd // tk),
                   a_spec=pl.BlockSpec((tm, tk), lambda i, j, k: (i, k)),
                   b_spec=pl.BlockSpec((tk, tn), lambda i, j, k: (k, j)),
                   acc_shape=(tm, tn), outs=outs, out_specs=[o_spec] * len(outs),
                   extra=extra, extra_specs=[o_spec] * len(extra), epilogue=epilogue)


def _mm_nn_sharded(name, a, b3, out_dtype=f32, tm=1024, tk=2048, epilogue=None, outs=None):
    m, kd = a.shape
    nsh, _, n = b3.shape
    tm, tk = _tile(m, tm), _tile(kd, tk)
    o_spec = pl.BlockSpec((tm, n), lambda i, j, k: (i, j))
    outs = outs if outs is not None else [_sds((m, nsh * n), out_dtype)]
    return _matmul(name, a, b3, dims=NN, grid=(m // tm, nsh, kd // tk),
                   a_spec=pl.BlockSpec((tm, tk), lambda i, j, k: (i, k)),
                   b_spec=pl.BlockSpec((None, tk, n), lambda i, j, k: (j, k, 0)),
                   acc_shape=(tm, n), outs=outs, out_specs=[o_spec] * len(outs), epilogue=epilogue)


def _mm_nt(name, a, b, out_dtype=f32, tm=1024, tn=1024, tk=2048, epilogue=None, extra=(), outs=None, after=None):
    m, kd = a.shape
    n = b.shape[0]
    tm, tn, tk = _tile(m, tm), _tile(n, tn), _tile(kd, tk)
    o_spec = pl.BlockSpec((tm, tn), lambda i, j, k: (i, j))
    outs = outs if outs is not None else [_sds((m, n), out_dtype)]
    return _matmul(name, a, b, dims=NT, grid=(m // tm, n // tn, kd // tk),
                   a_spec=pl.BlockSpec((tm, tk), lambda i, j, k: (i, k)),
                   b_spec=pl.BlockSpec((tn, tk), lambda i, j, k: (j, k)),
                   acc_shape=(tm, tn), outs=outs, out_specs=[o_spec] * len(outs),
                   extra=extra, extra_specs=[o_spec] * len(extra), epilogue=epilogue, after=after)


def _mm_tn(name, a, b, out_dtype=bf16, tm=1024, tn=1024, tk=2048, after=None):
    t, m = a.shape
    n = b.shape[1]
    tm, tn, tk = _tile(m, tm), _tile(n, tn), _tile(t, tk)
    return _matmul(name, a, b, dims=TN, grid=(m // tm, n // tn, t // tk),
                   a_spec=pl.BlockSpec((tk, tm), lambda i, j, k: (k, i)),
                   b_spec=pl.BlockSpec((tk, tn), lambda i, j, k: (k, j)),
                   acc_shape=(tm, tn), outs=[_sds((m, n), out_dtype)],
                   out_specs=[pl.BlockSpec((tm, tn), lambda i, j, k: (i, j))], after=after)[0]


def _mm_tn_sharded(name, a, b, nsh, out_dtype=bf16, tm=1024, tk=2048):
    t, m = a.shape
    n = b.shape[1] // nsh
    tm, tk = _tile(m, tm), _tile(t, tk)
    return _matmul(name, a, b, dims=TN, grid=(m // tm, nsh, t // tk),
                   a_spec=pl.BlockSpec((tk, tm), lambda i, j, k: (k, i)),
                   b_spec=pl.BlockSpec((tk, n), lambda i, j, k: (k, j)),
                   acc_shape=(tm, n), outs=[_sds((nsh, m, n), out_dtype)],
                   out_specs=[pl.BlockSpec((None, tm, n), lambda i, j, k: (j, i, 0))])[0]


SCAN_CHAINS = 8


def _scan_segments(s_len):
    nch = SCAN_CHAINS
    while s_len % (8 * nch) or (s_len // (8 * nch)) & (s_len // (8 * nch) - 1):
        nch //= 2
    return 8 * nch


def _to_segments(t, nseg):
    s_len, c = t.shape
    return t.reshape(nseg, s_len // nseg, c).transpose(1, 0, 2).reshape(s_len, c)


def _from_segments(t, nseg):
    s_len, c = t.shape
    return t.reshape(s_len // nseg, nseg, c).transpose(1, 0, 2).reshape(s_len, c)


def _lane_block_weights(t3, n_state):
    n_groups = t3.shape[0]
    gpl = LANE // n_state
    per = LANE // (gpl * SSM_GROUP)
    n_lb = n_groups // gpl
    t5 = t3.reshape(n_lb // per, per, gpl, SSM_GROUP, n_state)
    w = jnp.einsum("aqgic,gh,qs->aqsgihc", t5, jnp.eye(gpl, dtype=t3.dtype), jnp.eye(per, dtype=t3.dtype))
    return w.reshape(n_lb, LANE, LANE).astype(bf16)


def _lane_block_diag(w, n_state):
    gpl = LANE // n_state
    per = LANE // (gpl * SSM_GROUP)
    n_lb = w.shape[0]
    w7 = w.reshape(n_lb // per, per, per, gpl, SSM_GROUP, gpl, n_state)
    t5 = jnp.einsum("aqsgihc,gh,qs->aqgic", w7, jnp.eye(gpl, dtype=w.dtype), jnp.eye(per, dtype=w.dtype))
    return t5.reshape(n_lb * gpl, SSM_GROUP, n_state)


def _scan(name, src, w_in, w_out, a2, *, reverse, adjoint_of=None):
    s_len, n_ch = src.shape
    gp = a2.shape[1]
    per = (gp // LANE) // (n_ch // LANE)
    nseg = _scan_segments(s_len)
    nch = nseg // 8
    seg = s_len // nseg
    n_sq = int(math.log2(seg))
    assert 2 ** n_sq == seg
    adj = adjoint_of is not None
    chunk = _tile(s_len, 1024)
    n_chunks = s_len // chunk

    def body(*refs):
        it = iter(refs)
        src_ref, wir_ref, wii_ref, wor_ref, woi_ref, a_ref = (next(it) for _ in range(6))
        if adj:
            ut_ref, dyt_ref, hr_ref, hi_ref, hin_ref = (next(it) for _ in range(5))
        res_ref = next(it)
        if adj:
            da_ref, dbr_ref, dbi_ref, dcr_ref, dci_ref = (next(it) for _ in range(5))
        else:
            or_ref, oi_ref, oin_ref = (next(it) for _ in range(3))
        if adj:
            or_ref, oi_ref = next(it), next(it)

        for i in range(n_chunks):
            part = src_ref[i * chunk:(i + 1) * chunk, :]
            or_ref[i * chunk:(i + 1) * chunk, :] = jnp.dot(part, wir_ref[...], preferred_element_type=f32)
            oi_ref[i * chunk:(i + 1) * chunk, :] = jnp.dot(part, wii_ref[...], preferred_element_type=f32)

        ar = a_ref[0:1, :]
        ai = -a_ref[1:2, :] if reverse else a_ref[1:2, :]
        arb, aib = jnp.broadcast_to(ar, (8, LANE)), jnp.broadcast_to(ai, (8, LANE))

        def rows(ch, k):
            return pl.ds(pl.multiple_of(k * nseg + ch * 8, 8), 8)

        def advance(h, ch, k):
            hr, hi = h
            return (arb * hr - aib * hi + or_ref[rows(ch, k), :], arb * hi + aib * hr + oi_ref[rows(ch, k), :])

        def kk(n):
            return seg - 1 - n if reverse else n

        zero = jnp.zeros((8, LANE), f32)

        def sweep1(n, hs):
            return tuple(advance(hs[ch], ch, kk(n)) for ch in range(nch))

        ends = lax.fori_loop(0, seg, sweep1, tuple((zero, zero) for _ in range(nch)))

        pr, pi = ar, ai
        for _ in range(n_sq):
            pr, pi = pr * pr - pi * pi, 2.0 * pr * pi
        in_r, in_i = [None] * nseg, [None] * nseg
        cr = ci = jnp.zeros((1, LANE), f32)
        for j in (range(nseg - 1, -1, -1) if reverse else range(nseg)):
            in_r[j], in_i[j] = cr, ci
            er, ei = ends[j // 8][0][j % 8:j % 8 + 1, :], ends[j // 8][1][j % 8:j % 8 + 1, :]
            cr, ci = er + pr * cr - pi * ci, ei + pr * ci + pi * cr
        h0 = tuple((jnp.concatenate(in_r[8 * ch:8 * ch + 8], axis=0), jnp.concatenate(in_i[8 * ch:8 * ch + 8], axis=0))
                   for ch in range(nch))
        if not adj:
            for ch in range(nch):
                oin_ref[0, 8 * ch:8 * ch + 8, :] = h0[ch][0]
                oin_ref[1, 8 * ch:8 * ch + 8, :] = h0[ch][1]

        def emit(ch, k, h):
            or_ref[rows(ch, k), :] = h[0]
            oi_ref[rows(ch, k), :] = h[1]

        def pair(h, p):
            return h[0] * p[0] + h[1] * p[1], h[1] * p[0] - h[0] * p[1]

        def sweep2(n, carry):
            k = kk(n)
            new = tuple(advance(carry[ch], ch, k) for ch in range(nch))
            for ch in range(nch):
                emit(ch, k, new[ch])
            if not adj:
                return new
            dr, di = carry[nch]
            for ch in range(nch):
                qr, qi = pair(new[ch], (hr_ref[rows(ch, k - 1), :], hi_ref[rows(ch, k - 1), :]))
                dr, di = dr + qr, di + qi
            return new + ((dr, di),)

        if adj:
            carry = lax.fori_loop(0, seg - 1, sweep2, h0 + ((zero, zero),))
            dr, di = carry[nch]
            for ch in range(nch):
                new = advance(carry[ch], ch, 0)
                emit(ch, 0, new)
                qr, qi = pair(new, (hin_ref[0, 8 * ch:8 * ch + 8, :], hin_ref[1, 8 * ch:8 * ch + 8, :]))
                dr, di = dr + qr, di + qi
            da_ref[0] = jnp.sum(dr, axis=0, keepdims=True)
            da_ref[1] = jnp.sum(di, axis=0, keepdims=True)
        else:
            lax.fori_loop(0, seg, sweep2, h0)

        first = pl.program_id(0) % per == 0
        for i in range(n_chunks):
            sl = slice(i * chunk, (i + 1) * chunk)
            part = lax.dot_general(or_ref[sl, :].astype(bf16), wor_ref[...], (NT, ((), ())), preferred_element_type=f32)
            part += lax.dot_general(oi_ref[sl, :].astype(bf16), woi_ref[...], (NT, ((), ())),
                                    preferred_element_type=f32)

            @pl.when(first)
            def _():
                res_ref[sl, :] = part

            @pl.when(jnp.logical_not(first))
            def _():
                res_ref[sl, :] += part

        if adj:
            def over_time(xt_ref, y_ref):
                tot = jnp.zeros((LANE, LANE), f32)
                for i in range(n_chunks):
                    sl = slice(i * chunk, (i + 1) * chunk)
                    tot += jnp.dot(xt_ref[:, sl], y_ref[sl, :].astype(bf16), preferred_element_type=f32)
                return tot

            dbr_ref[...] = over_time(ut_ref, or_ref)
            dbi_ref[...] = over_time(ut_ref, oi_ref)
            dcr_ref[...] = over_time(dyt_ref, hr_ref)
            dci_ref[...] = over_time(dyt_ref, hi_ref)

    col = pl.BlockSpec((s_len, LANE), lambda l: (0, l))
    chan = pl.BlockSpec((s_len, LANE), lambda l: (0, l // per))
    in_spec = pl.BlockSpec((2, nseg, LANE), lambda l: (0, 0, l))
    w_spec = pl.BlockSpec((None, LANE, LANE), lambda l: (l, 0, 0))
    ins = [src, *w_in, *w_out, a2]
    in_specs = [chan, w_spec, w_spec, w_spec, w_spec, pl.BlockSpec((8, LANE), lambda l: (0, l))]
    outs, out_specs = [_sds((s_len, n_ch), f32)], [chan]
    scratch = []
    if adj:
        ins += list(adjoint_of)
        chan_t = pl.BlockSpec((LANE, s_len), lambda l: (l // per, 0))
        in_specs += [chan_t, chan_t, col, col, in_spec]
        outs += [_sds((2, 1, gp), f32)] + [_sds((gp // LANE, LANE, LANE), f32)] * 4
        out_specs += [pl.BlockSpec((2, 1, LANE), lambda l: (0, 0, l))] + [w_spec] * 4
        scratch = [pltpu.VMEM((s_len, LANE), f32)] * 2
    else:
        outs += [_sds((s_len, gp), f32)] * 2 + [_sds((2, nseg, gp), f32)]
        out_specs += [col, col, in_spec]
    res = _pcall(body, name=name, grid=(gp // LANE,), in_specs=in_specs, out_specs=out_specs, out_shape=outs,
                 scratch_shapes=scratch, sem=("arbitrary",))(*ins)
    if adj:
        return res[0], res[1], (res[2], res[3]), (res[4], res[5])
    return res


def _ssm_param_fn(lam_re, lam_im, log_step, b_re2, b_im2, expand):
    step = jnp.exp(log_step)
    xr, xi = lam_re * step, lam_im * step
    mag = jnp.exp(xr)
    ar, ai = mag * jnp.cos(xi), mag * jnp.sin(xi)
    nr, ni = ar - 1.0, ai
    den = lam_re * lam_re + lam_im * lam_im
    cr = (nr * lam_re + ni * lam_im) / den
    ci = (ni * lam_re - nr * lam_im) / den
    cre = jnp.dot(cr, expand, precision=HI, preferred_element_type=f32)
    cie = jnp.dot(ci, expand, precision=HI, preferred_element_type=f32)
    return ar, ai, cre * b_re2 - cie * b_im2, cre * b_im2 + cie * b_re2


def _ssm_params(lam_re, lam_im, log_step, b_re2, b_im2, expand):
    def body(*refs):
        res = _ssm_param_fn(*[r[...] for r in refs[:6]])
        for r, o in zip(refs[6:], res):
            r[...] = o

    g, p = lam_re.shape
    return _pcall(body, name="ssm_params", out_shape=[_sds((g, p), f32)] * 2 + [_sds(b_re2.shape, f32)] * 2)(
        lam_re, lam_im, log_step, b_re2, b_im2, expand)


def _ssm_params_bwd(lam_re, lam_im, log_step, b_re2, b_im2, expand, d_ar, d_ai, d_bbr, d_bbi):
    def body(*refs):
        prim = [r[...] for r in refs[:5]]
        ex = refs[5][...]
        cot = tuple(r[...] for r in refs[6:10])
        _, vjp = jax.vjp(lambda *p_: _ssm_param_fn(*p_, ex), *prim)
        for r, o in zip(refs[10:], vjp(cot)):
            r[...] = o

    shapes = [lam_re.shape, lam_im.shape, log_step.shape, b_re2.shape, b_im2.shape]
    return _pcall(body, name="ssm_params_bwd", out_shape=[_sds(s, f32) for s in shapes])(
        lam_re, lam_im, log_step, b_re2, b_im2, expand, d_ar, d_ai, d_bbr, d_bbi)


def _slope_table(n_heads):
    s = 2.0 ** (-8.0 * (jnp.arange(n_heads, dtype=f32) + 1.0) / n_heads)
    return jnp.broadcast_to(s[:, None, None], (n_heads, 1, LANE))


def _band_bias(slope_d, shift):
    qi = lax.broadcasted_iota(jnp.int32, (BAND, BAND), 0)
    ki = lax.broadcasted_iota(jnp.int32, (BAND, BAND), 1)
    mask = (ki >= qi) if shift else (ki <= qi)
    return jnp.where(mask, -slope_d * (qi - ki + shift).astype(f32), NEG)


def _window_bias(slope_d, has_prev):
    own = _band_bias(slope_d, 0)
    mid = jnp.concatenate([_band_bias(slope_d, BAND), own], axis=1)
    none = jnp.concatenate([jnp.full((BAND, BAND), NEG, f32), own], axis=1)
    return mid, jnp.where(has_prev, mid, none)


def _window_scores(q, k2, bias):
    return lax.dot_general(q, k2, (NT, ((), ())), preferred_element_type=f32) * (HEAD ** -0.5) + bias


def _attn_geometry(s_len, dil, rows=1024):
    piece = BAND * dil
    m = max(1, rows // piece)
    while s_len % (piece * m):
        m //= 2
    return m, piece


def _stream_rows(start, dil):
    return pl.ds(start, BAND, stride=dil) if dil > 1 else pl.ds(start, BAND)


def _attn_fwd(qn, kn, proj, v_blk, dil, slopes):
    s_len, aw = qn.shape
    n_heads = aw // HEAD
    m, piece = _attn_geometry(s_len, dil)
    rows = m * piece

    def body(q_ref, k_ref, kp_ref, v_ref, vp_ref, sl_ref, o_ref, lse_ref):
        bias_mid, bias_first = _window_bias(sl_ref[:, 0:1] * float(dil), pl.program_id(1) > 0)
        for b in range(m):
            for r in range(dil):
                idx = _stream_rows(b * piece + r, dil)
                q, kc, vc = (ref[idx, :].astype(bf16) for ref in (q_ref, k_ref, v_ref))
                if b:
                    pidx = _stream_rows((b - 1) * piece + r, dil)
                    kp, vp = k_ref[pidx, :].astype(bf16), v_ref[pidx, :].astype(bf16)
                else:
                    pidx = _stream_rows(r, dil)
                    kp, vp = kp_ref[pidx, :].astype(bf16), vp_ref[pidx, :].astype(bf16)
                k2, v2 = jnp.concatenate([kp, kc], axis=0), jnp.concatenate([vp, vc], axis=0)
                s = _window_scores(q, k2, bias_mid if b else bias_first)
                mx = jnp.max(s, axis=1, keepdims=True)
                p = jnp.exp(s - mx)
                den = jnp.sum(p, axis=1, keepdims=True)
                o_ref[idx, :] = jnp.dot(p.astype(bf16), v2, preferred_element_type=f32) / den
                lse_ref[idx, :] = jnp.broadcast_to(mx + jnp.log(den), (BAND, HEAD))

    def cur(blk0):
        return pl.BlockSpec((rows, HEAD), lambda h, t: (t, blk0 + h))

    def prev(blk0):
        return pl.BlockSpec((piece, HEAD), lambda h, t: (jnp.maximum(t * m - 1, 0), blk0 + h))

    sl = pl.BlockSpec((None, 1, LANE), lambda h, t: (h, 0, 0))
    return _pcall(body, name=f"attn_fwd_d{dil}", grid=(n_heads, s_len // rows),
                  in_specs=[cur(0), cur(0), prev(0), cur(v_blk), prev(v_blk), sl], out_specs=[cur(0), cur(0)],
                  out_shape=[_sds((s_len, aw), f32)] * 2, sem=("parallel", "parallel"))(
        qn, kn, kn, proj, proj, slopes)


def _attn_bwd(qn, kn, proj, v_blk, do, lse, dd, dil, slopes):
    s_len, aw = qn.shape
    n_heads = aw // HEAD
    m, piece = _attn_geometry(s_len, dil, max(1024, 2 * BAND * dil))
    rows = m * piece
    n_tiles = s_len // rows
    scale = HEAD ** -0.5

    def body(q_ref, qx_ref, k_ref, kp_ref, v_ref, vp_ref, do_ref, dox_ref, l_ref, lx_ref, d_ref, dx_ref, sl_ref,
             dq_ref, dk_ref, dv_ref):
        t = pl.program_id(1)
        slope_d = sl_ref[:, 0:1] * float(dil)
        bias_mid, bias_first = _window_bias(slope_d, t > 0)
        bias_next = _band_bias(slope_d, BAND)

        def query_side(ref_q, ref_do, ref_l, ref_d, idx):
            return (ref_q[idx, :].astype(bf16), ref_do[idx, :].astype(bf16), ref_l[idx, :][:, 0:1],
                    ref_d[idx, :][:, 0:1])

        def probs(qs, keys, values, bias):
            q, do_, l_col, d_col = qs
            p = jnp.exp(_window_scores(q, keys, bias) - l_col)
            dp = lax.dot_general(do_, values, (NT, ((), ())), preferred_element_type=f32)
            return p.astype(bf16), (p * (dp - d_col)).astype(bf16)

        def tn(a_, b_):
            return lax.dot_general(a_, b_, (TN, ((), ())), preferred_element_type=f32)

        for r in range(dil):
            pend = None
            for b in range(m):
                idx = _stream_rows(b * piece + r, dil)
                qs = query_side(q_ref, do_ref, l_ref, d_ref, idx)
                kc, vc = k_ref[idx, :].astype(bf16), v_ref[idx, :].astype(bf16)
                if b:
                    kp, vp = kc_prev, vc_prev
                else:
                    pidx = _stream_rows(r, dil)
                    kp, vp = kp_ref[pidx, :].astype(bf16), vp_ref[pidx, :].astype(bf16)
                k2, v2 = jnp.concatenate([kp, kc], axis=0), jnp.concatenate([vp, vc], axis=0)
                p, ds = probs(qs, k2, v2, bias_mid if b else bias_first)
                dq_ref[idx, :] = jnp.dot(ds, k2, preferred_element_type=f32) * scale
                dk2, dv2 = tn(ds, qs[0]), tn(p, qs[1])
                if pend is not None:
                    dk_ref[pend[0], :] = (pend[1] + dk2[:BAND]) * scale
                    dv_ref[pend[0], :] = pend[2] + dv2[:BAND]
                pend = (idx, dk2[BAND:], dv2[BAND:])
                kc_prev, vc_prev = kc, vc
            qs = query_side(qx_ref, dox_ref, lx_ref, dx_ref, _stream_rows(r, dil))
            p, ds = probs(qs, kc_prev, vc_prev, bias_next)
            live = t < n_tiles - 1
            dk_ref[pend[0], :] = (pend[1] + jnp.where(live, tn(ds, qs[0]), 0.0)) * scale
            dv_ref[pend[0], :] = pend[2] + jnp.where(live, tn(p, qs[1]), 0.0)

    def cur(blk0):
        return pl.BlockSpec((rows, HEAD), lambda h, t: (t, blk0 + h))

    def prev(blk0):
        return pl.BlockSpec((piece, HEAD), lambda h, t: (jnp.maximum(t * m - 1, 0), blk0 + h))

    def nxt(blk0):
        return pl.BlockSpec((piece, HEAD), lambda h, t: (jnp.minimum(t * m + m, n_tiles * m - 1), blk0 + h))

    sl = pl.BlockSpec((None, 1, LANE), lambda h, t: (h, 0, 0))
    return _pcall(body, name=f"attn_bwd_d{dil}", grid=(n_heads, n_tiles),
                  in_specs=[cur(0), nxt(0), cur(0), prev(0), cur(v_blk), prev(v_blk), cur(0), nxt(0), cur(0), nxt(0),
                            cur(0), nxt(0), sl],
                  out_specs=[cur(0)] * 3, out_shape=[_sds((s_len, aw), f32)] * 3,
                  sem=("parallel", "parallel"))(qn, qn, kn, kn, proj, proj, do, do, lse, lse, dd, dd, slopes)


def _exchange(name, srcs, scatter):
    n = len(srcs)

    def body(*refs):
        src, out = refs[:n], refs[n:2 * n]
        send_sems, recv_sems, local_sems = refs[2 * n:]
        x, y, c = lax.axis_index("x"), lax.axis_index("y"), lax.axis_index("c")
        me = 4 * x + 2 * y + c

        def peer(r):
            return ((1 - x) if r & 4 else x, (1 - y) if r & 2 else y, (1 - c) if r & 1 else c)

        def lin(p):
            return 4 * p[0] + 2 * p[1] + p[2]

        def piece(a, idx):
            return src[a].at[idx] if scatter[a] else src[a]

        local, sends = [], []
        for a in range(n):
            cp = pltpu.make_async_copy(piece(a, me), out[a].at[me], local_sems.at[a])
            cp.start()
            local.append(cp)
        for r in range(1, N_DEV):
            p = peer(r)
            for a in range(n):
                cp = pltpu.make_async_remote_copy(src_ref=piece(a, lin(p)), dst_ref=out[a].at[me],
                                                  send_sem=send_sems.at[a, r - 1], recv_sem=recv_sems.at[a, r - 1],
                                                  device_id=p, device_id_type=MESH)
                cp.start()
                sends.append(cp)
        for r in range(1, N_DEV):
            p = peer(r)
            for a in range(n):
                pltpu.make_async_remote_copy(src_ref=piece(a, lin(p)), dst_ref=out[a].at[lin(p)],
                                             send_sem=send_sems.at[a, r - 1], recv_sem=recv_sems.at[a, r - 1],
                                             device_id=p, device_id_type=MESH).wait_recv()
        for cp in sends:
            cp.wait_send()
        for cp in local:
            cp.wait()

    def piece_shape(a):
        return srcs[a].shape[1:] if scatter[a] else srcs[a].shape

    any_spec = pl.BlockSpec(memory_space=pl.ANY)
    return _pcall(body, name=name, in_specs=[any_spec] * n, out_specs=[any_spec] * n,
                  out_shape=[_sds((N_DEV, *piece_shape(a)), srcs[a].dtype) for a in range(n)],
                  scratch_shapes=[pltpu.SemaphoreType.DMA((n, N_DEV - 1)), pltpu.SemaphoreType.DMA((n, N_DEV - 1)),
                                  pltpu.SemaphoreType.DMA((n,))])(*srcs)


_HBM = pl.BlockSpec(memory_space=pltpu.HBM)
_SEM = pl.BlockSpec(memory_space=pltpu.SEMAPHORE)
_EFFECT = pltpu.SideEffectType.DATAFLOW_SIDE_EFFECTING


def _peer_ids():
    x, y, c = lax.axis_index("x"), lax.axis_index("y"), lax.axis_index("c")
    peers = [((1 - x) if r & 4 else x, (1 - y) if r & 2 else y, (1 - c) if r & 1 else c) for r in range(1, N_DEV)]
    return 4 * x + 2 * y + c, peers, [4 * p[0] + 2 * p[1] + p[2] for p in peers]


def _exchange_start(name, src, scatter, after):
    piece_shape = src.shape[1:] if scatter else src.shape

    def body(src_ref, land_ref, after_ref, send_sems, recv_sems, local_sem, src_thru, land_thru, token):
        me, peers, lins = _peer_ids()

        def piece(idx):
            return src_ref.at[idx] if scatter else src_ref

        pltpu.make_async_copy(piece(me), land_ref.at[me], local_sem).start()
        for r, (p, lp) in enumerate(zip(peers, lins)):
            pltpu.make_async_remote_copy(src_ref=piece(lp), dst_ref=land_ref.at[me], send_sem=send_sems.at[r],
                                         recv_sem=recv_sems.at[r], device_id=p, device_id_type=MESH).start()
        token[...] = jnp.zeros_like(token)

    land = pltpu.with_memory_space_constraint(lax.empty((N_DEV, *piece_shape), src.dtype), pltpu.HBM)
    send_sems, recv_sems, local_sem, src_thru, land_thru, token = pl.pallas_call(
        body, name=name,
        out_shape=(pltpu.SemaphoreType.DMA((N_DEV - 1,)), pltpu.SemaphoreType.DMA((N_DEV - 1,)),
                   pltpu.SemaphoreType.DMA(()), pltpu.HBM(src.shape, src.dtype),
                   pltpu.HBM((N_DEV, *piece_shape), src.dtype), _sds((8, LANE), f32)),
        in_specs=(_HBM, _HBM, pl.BlockSpec(memory_space=pl.ANY)),
        out_specs=(_SEM, _SEM, _SEM, _HBM, _HBM, pl.BlockSpec(memory_space=pltpu.VMEM)),
        input_output_aliases={0: 3, 1: 4},
        compiler_params=pltpu.CompilerParams(has_side_effects=_EFFECT),
    )(pltpu.with_memory_space_constraint(src, pltpu.HBM), land, after)
    return (send_sems, recv_sems, local_sem, src_thru, land_thru, scatter), token


def _exchange_wait(name, handle, *after):
    send_sems, recv_sems, local_sem, src_thru, land_thru, scatter = handle

    def body(src_ref, land_ref, send_sems_, recv_sems_, local_sem_, *rest):
        me, peers, lins = _peer_ids()

        def piece(idx):
            return src_ref.at[idx] if scatter else src_ref

        pltpu.make_async_copy(piece(me), land_ref.at[me], local_sem_).wait()
        for r, (p, lp) in enumerate(zip(peers, lins)):
            pltpu.make_async_remote_copy(src_ref=piece(lp), dst_ref=land_ref.at[me], send_sem=send_sems_.at[r],
                                         recv_sem=recv_sems_.at[r], device_id=p, device_id_type=MESH).wait_send()
            pltpu.make_async_remote_copy(src_ref=piece(lp), dst_ref=land_ref.at[lp], send_sem=send_sems_.at[r],
                                         recv_sem=recv_sems_.at[r], device_id=p, device_id_type=MESH).wait_recv()

    return pl.pallas_call(
        body, name=name,
        out_shape=(pltpu.HBM(src_thru.shape, src_thru.dtype), pltpu.HBM(land_thru.shape, land_thru.dtype)),
        in_specs=(_HBM, _HBM, _SEM, _SEM, _SEM, *[pl.BlockSpec(memory_space=pl.ANY)] * len(after)),
        out_specs=(_HBM, _HBM), input_output_aliases={0: 0, 1: 1},
        compiler_params=pltpu.CompilerParams(has_side_effects=_EFFECT),
    )(src_thru, land_thru, send_sems, recv_sems, local_sem, *after)[1]


def _adamw(name, w, m, v, g_or_stack, stacked, rows=256):
    r, c = w.shape
    tr = _tile(r, rows)

    def fn(w_, m_, v_, g_):
        if stacked:
            g = g_[0].astype(f32)
            for j in range(1, N_DEV):
                g = g + g_[j].astype(f32)
        else:
            g = g_
        m_new = ADAM_B1 * m_ + (1.0 - ADAM_B1) * g
        v_new = ADAM_B2 * v_ + (1.0 - ADAM_B2) * (g * g)
        m_hat = m_new / (1.0 - ADAM_B1 ** ADAM_STEP)
        v_hat = v_new / (1.0 - ADAM_B2 ** ADAM_STEP)
        delta = -ADAM_LR * (m_hat / (jnp.sqrt(v_hat) + ADAM_EPS) + ADAM_WD * w_)
        return g, delta, m_new, v_new

    blk = _row(tr, c)
    g_spec = pl.BlockSpec((N_DEV, tr, c), lambda i: (0, i, 0)) if stacked else blk
    return _rowwise(name, fn, r // tr, [w, m, v, g_or_stack], [blk, blk, blk, g_spec],
                    [_sds((r, c), f32)] * 4, [blk] * 4, [False] * 4)


def _ada_fwd(c_all, w_shard, b_shard):
    nb_, d = c_all.shape
    n = w_shard.shape[1]
    tn = _tile(n, 512)

    def body(c_ref, w_ref, b_ref, o_ref):
        a = jax.nn.silu(c_ref[...]).astype(bf16)
        o_ref[...] = jnp.dot(a, w_ref[...].astype(bf16), preferred_element_type=f32) + b_ref[...]

    return _pcall(body, name="ada_fwd", grid=(n // tn,),
                  in_specs=[pl.BlockSpec((nb_, d), lambda j: (0, 0)), pl.BlockSpec((d, tn), lambda j: (0, j)),
                            pl.BlockSpec((1, tn), lambda j: (0, j))],
                  out_specs=pl.BlockSpec((nb_, tn), lambda j: (0, j)), out_shape=_sds((nb_, n), f32),
                  sem=("parallel",))(c_all, w_shard, b_shard)


def _ada_bwd(c_all, dmod_cols):
    nb_, d = c_all.shape
    n = dmod_cols.shape[1]
    tn = _tile(n, 512)

    def body(c_ref, g_ref, o_ref):
        a = jax.nn.silu(c_ref[...]).astype(bf16).astype(f32)
        g = g_ref[...].astype(bf16).astype(f32)
        o_ref[...] = lax.dot_general(a, g, (TN, ((), ())), precision=HI, preferred_element_type=f32)

    return _pcall(body, name="ada_bwd", grid=(n // tn,),
                  in_specs=[pl.BlockSpec((nb_, d), lambda j: (0, 0)), pl.BlockSpec((nb_, tn), lambda j: (0, j))],
                  out_specs=pl.BlockSpec((d, tn), lambda j: (0, j)), out_shape=_sds((d, n), f32),
                  sem=("parallel",))(c_all, dmod_cols)


SMALL_LATE = ("b_ada", "norm1_g", "q_norm_g", "k_norm_g")
SMALL_EARLY = ("lam_re", "lam_im", "log_step", "b_re", "b_im", "c_re", "c_im", "d_skip", "b_glu", "attn_out_g",
               "ssm_out_g", "norm2_g")
ORDER = ("w_ada", "b_ada", "norm1_g", "w_in", "q_norm_g", "k_norm_g", "lam_re", "lam_im", "log_step", "b_re", "b_im",
         "c_re", "c_im", "d_skip", "w_glu", "b_glu", "attn_out_g", "ssm_out_g", "w_out", "norm2_g", "w_ff1", "w_ff2")


def _pack(parts):
    flat = jnp.concatenate([p.reshape(-1) for p in parts])
    pad = (-flat.shape[0]) % (8 * LANE)
    return jnp.pad(flat, (0, pad)).reshape(-1, LANE)


def kernel(x, c, w_ada, b_ada, norm1_g, w_in, q_norm_g, k_norm_g, lam_re, lam_im, log_step, b_re, b_im, c_re, c_im, d_skip, w_glu, b_glu, attn_out_g, ssm_out_g, w_out, norm2_g, w_ff1, w_ff2, loss_target, m_w_ada, m_b_ada, m_norm1_g, m_w_in, m_q_norm_g, m_k_norm_g, m_lam_re, m_lam_im, m_log_step, m_b_re, m_b_im, m_c_re, m_c_im, m_d_skip, m_w_glu, m_b_glu, m_attn_out_g, m_ssm_out_g, m_w_out, m_norm2_g, m_w_ff1, m_w_ff2, v_w_ada, v_b_ada, v_norm1_g, v_w_in, v_q_norm_g, v_k_norm_g, v_lam_re, v_lam_im, v_log_step, v_b_re, v_b_im, v_c_re, v_c_im, v_d_skip, v_w_glu, v_b_glu, v_attn_out_g, v_ssm_out_g, v_w_out, v_norm2_g, v_w_ff1, v_w_ff2):
    env = dict(locals())
    wts = {n: env[n] for n in ORDER}
    mom = {n: env["m_" + n] for n in ORDER}
    var = {n: env["v_" + n] for n in ORDER}

    xs, tgt = x[0], loss_target[0]
    s_len, d = xs.shape
    aw = d // 2
    sw = d - aw
    n_heads = aw // HEAD
    n_groups = sw // SSM_GROUP
    n_state = lam_re.shape[-1]
    gp = n_groups * n_state
    tm = _tile(s_len, 256)
    steps = s_len // tm
    me = 4 * lax.axis_index("x") + 2 * lax.axis_index("y") + lax.axis_index("c")

    (c_all,) = _exchange("gather_c", [c], [False])
    c_all = c_all.reshape(N_DEV, d)

    n_ada = w_ada.shape[-1]
    b_ada_cols = lax.dynamic_slice_in_dim(b_ada, me * n_ada, n_ada, axis=1)
    mod_part = _ada_fwd(c_all, w_ada[0], b_ada_cols)
    (mod_all,) = _exchange("gather_mod", [mod_part], [False])
    mod = lax.dynamic_index_in_dim(mod_all, me, axis=1, keepdims=False).reshape(1, 6 * d)
    sh1, sc1, g1, sh2, sc2, g2 = (mod[:, i * d:(i + 1) * d] for i in range(6))

    gather, started = {}, jnp.zeros((1, 1), f32)
    for name in ("w_in", "w_glu", "w_out", "w_ff1", "w_ff2"):
        gather[name], token = _exchange_start("gather_" + name, wts[name][0].astype(bf16), False, mod_all)
        started = started + token[0:1, 0:1]
    sc1 = sc1 + started

    (h,) = _rowwise("norm1", _norm_mod, steps, [xs, norm1_g, sc1, sh1],
                    [_row(tm, d), _vec(d), _vec(d), _vec(d)], [_sds((s_len, d), bf16)], [_row(tm, d)], [False])
    lam_re2, lam_im2 = lam_re[0], lam_im[0]
    log_step2 = log_step[0].reshape(n_groups, 1)
    b_re2 = b_re[0].reshape(n_groups, n_state * SSM_GROUP)
    b_im2 = b_im[0].reshape(n_groups, n_state * SSM_GROUP)
    expand = jnp.repeat(jnp.eye(n_state, dtype=f32), SSM_GROUP, axis=1)
    a_re, a_im, bb_re2, bb_im2 = _ssm_params(lam_re2, lam_im2, log_step2, b_re2, b_im2, expand)
    a2 = jnp.zeros((8, gp), f32).at[0].set(a_re.reshape(gp)).at[1].set(a_im.reshape(gp))
    w_bu = tuple(_lane_block_weights(t.reshape(n_groups, n_state, SSM_GROUP).transpose(0, 2, 1), n_state)
                 for t in (bb_re2, bb_im2))
    w_c = (_lane_block_weights(c_re[0], n_state), _lane_block_weights(-c_im[0], n_state))

    packed = {names: tuple(_pack([t[n] for n in names]) for t in (wts, mom, var))
              for names in (SMALL_LATE, SMALL_EARLY)}

    win_g = _exchange_wait("gathered_w_in", gather["w_in"], h, a2, *w_bu, *w_c, *packed[SMALL_LATE],
                           *packed[SMALL_EARLY])
    (proj,) = _mm_nn_sharded("in_proj", h, win_g)

    def qk_fn(q, k, gq, gk):
        return _head_rms(q, gq), _head_rms(k, gk)

    qn, kn = _rowwise("qk_norm", qk_fn, steps, [proj, proj, q_norm_g, k_norm_g],
                      [_row(tm, aw, 0), _row(tm, aw, 1), _vec(HEAD), _vec(HEAD)],
                      [_sds((s_len, aw), f32)] * 2, [_row(tm, aw)] * 2, [False] * 2)
    v_blk = 2 * aw // HEAD

    slopes = _slope_table(n_heads)
    pat = [_attn_fwd(qn, kn, proj, v_blk, dil, slopes) for _, dil in DILATIONS]

    def attn_mix_fn(o1, l1, o2, l2, o3, l3):
        m = jnp.maximum(jnp.maximum(l1, l2), l3)
        e1, e2, e3 = jnp.exp(l1 - m), jnp.exp(l2 - m), jnp.exp(l3 - m)
        tot = e1 + e2 + e3
        return (e1 * o1 + e2 * o2 + e3 * o3) / tot, m + jnp.log(tot)

    attn, lse = _rowwise("attn_mix", attn_mix_fn, steps, [t for ol in pat for t in ol], [_row(tm, aw)] * 6,
                         [_sds((s_len, aw), f32)] * 2, [_row(tm, aw)] * 2, [False] * 2)

    nseg = _scan_segments(s_len)
    u_seg = _to_segments(proj[:, 3 * aw:], nseg).astype(bf16)
    y_seg, h_re, h_im, hin_f = _scan("ssm_scan", u_seg, w_bu, w_c, a2, reverse=False)
    ymm = _from_segments(y_seg, nseg)

    u_spec = _row(tm, sw, 3 * aw // sw)
    (yg,) = _rowwise("ssm_gelu", _ypre_fn, steps, [ymm, proj, d_skip], [_row(tm, sw), u_spec, _vec(sw)],
                     [_sds((s_len, sw), f32)], [_row(tm, sw)], [False])
    wglu_g = _exchange_wait("gathered_w_glu", gather["w_glu"], yg).reshape(sw, sw)
    (z,) = _mm_nn("glu_proj", yg, wglu_g)
    (cat,) = _rowwise("mix_norm", _mix_fn, steps, [attn, yg, z, b_glu, attn_out_g, ssm_out_g],
                      [_row(tm, aw), _row(tm, sw), _row(tm, sw), _vec(sw), _vec(aw), _vec(sw)],
                      [_sds((s_len, d), bf16)], [_row(tm, d)], [False])
    wout_g = _exchange_wait("gathered_w_out", gather["w_out"], cat).reshape(d, d)
    (mixed,) = _mm_nn("out_proj", cat, wout_g)

    def res_norm2_fn(x_, mixed_, g1_, gn, sc, sh):
        x1_ = x_ + g1_ * mixed_
        return x1_, _norm_mod(x1_, gn, sc, sh)

    x1, h2 = _rowwise("norm2", res_norm2_fn, steps, [xs, mixed, g1, norm2_g, sc2, sh2],
                      [_row(tm, d), _row(tm, d)] + [_vec(d)] * 4,
                      [_sds((s_len, d), f32), _sds((s_len, d), bf16)], [_row(tm, d)] * 2, [False] * 2)

    def act_epilogue(acc):
        r = jnp.maximum(acc, 0.0)
        return r, r * r

    wff1_g = _exchange_wait("gathered_w_ff1", gather["w_ff1"], h2)
    r_ff, act = _mm_nn_sharded("ff1", h2, wff1_g, epilogue=act_epilogue,
                               outs=[_sds((s_len, 4 * d), bf16), _sds((s_len, 4 * d), bf16)])
    wff2_g = _exchange_wait("gathered_w_ff2", gather["w_ff2"], act).reshape(4 * d, d)
    (ff,) = _mm_nn("ff2", act, wff2_g)

    def loss_fn(x1_, ff_, tgt_, g2_):
        e = x1_ + g2_ * ff_ - tgt_
        dy_ = e * (1.0 / d)
        part = jnp.full((1, LANE), 0.5 / d, f32) * jnp.sum(e * e)
        return dy_, g2_ * dy_, part, jnp.sum(dy_ * ff_, axis=0, keepdims=True)

    dy, dff, loss_part, d_g2 = _rowwise(
        "loss", loss_fn, steps, [x1, ff, tgt, g2], [_row(tm, d)] * 3 + [_vec(d)],
        [_sds((s_len, d), f32), _sds((s_len, d), bf16), _sds((1, LANE), f32), _sds((1, d), f32)],
        [_row(tm, d), _row(tm, d), _vec(LANE), _vec(d)], [False, False, True, True])
    loss = lax.psum(loss_part[0, 0], ("x", "y", "c"))

    def dact_epilogue(acc, r_):
        return (acc * (2.0 * r_.astype(f32)),)

    (da,) = _mm_nt("ff2_dx", dff, wff2_g, epilogue=dact_epilogue, extra=[r_ff], outs=[_sds((s_len, 4 * d), bf16)])
    scatter = {}
    g_wff2 = _mm_tn("ff2_dw", act, dff, after=loss.reshape(1, 1)).reshape(N_DEV, 4 * d // N_DEV, d)
    scatter["w_ff2"], tok_ff2 = _exchange_start("scatter_w_ff2", g_wff2, True, loss.reshape(1, 1))
    dh2 = _mm_nt("ff1_dx", da, wff1_g.transpose(1, 0, 2).reshape(d, 4 * d))[0]
    g_wff1 = _mm_tn_sharded("ff1_dw", h2, da, N_DEV)
    scatter["w_ff1"], tok_ff1 = _exchange_start("scatter_w_ff1", g_wff1, True, started)
    norm2_g_t = norm2_g + (tok_ff2[0:1, 0:1] + tok_ff1[0:1, 0:1])

    def norm2_bwd_fn(dh2_, x1_, dy_, mixed_, gn, sc, sh, g1_):
        _, vjp = jax.vjp(_norm_mod, x1_, gn, sc, sh)
        dx, dgn, dsc, dsh = vjp(dh2_)
        dx1_ = dy_ + dx
        return dx1_, g1_ * dx1_, dgn, dsc, dsh, jnp.sum(dx1_ * mixed_, axis=0, keepdims=True)

    dx1, dmixed, d_norm2_g, d_sc2, d_sh2, d_g1 = _rowwise(
        "norm2_bwd", norm2_bwd_fn, steps, [dh2, x1, dy, mixed, norm2_g_t, sc2, sh2, g1],
        [_row(tm, d)] * 4 + [_vec(d)] * 4,
        [_sds((s_len, d), f32), _sds((s_len, d), bf16)] + [_sds((1, d), f32)] * 4,
        [_row(tm, d)] * 2 + [_vec(d)] * 4, [False, False, True, True, True, True])

    (dcat,) = _mm_nt("out_dx", dmixed, wout_g)
    g_wout = _mm_tn("out_dw", cat, dmixed).reshape(N_DEV, d // N_DEV, d)
    scatter["w_out"], tok_out = _exchange_start("scatter_w_out", g_wout, True, started)
    b_glu_t = b_glu + tok_out[0:1, 0:1]

    def mix_bwd_fn(dcat_, attn_, yg_, z_, bglu, ga, gs):
        _, vjp = jax.vjp(_mix_fn, attn_, yg_, z_, bglu, ga, gs)
        dattn_, dyg_, dz_, dbglu, dga, dgs = vjp(dcat_)
        prod = dattn_ * attn_
        dd_ = jnp.concatenate([jnp.broadcast_to(jnp.sum(prod[:, i * HEAD:(i + 1) * HEAD], axis=1, keepdims=True),
                                                (prod.shape[0], HEAD)) for i in range(n_heads)], axis=1)
        return dattn_, dd_, dyg_, dz_, dbglu, dga, dgs

    dattn, dd, dyg1, dz, d_b_glu, d_attn_out_g, d_ssm_out_g = _rowwise(
        "mix_bwd", mix_bwd_fn, steps, [dcat, attn, yg, z, b_glu_t, attn_out_g, ssm_out_g],
        [_row(tm, d), _row(tm, aw), _row(tm, sw), _row(tm, sw), _vec(sw), _vec(aw), _vec(sw)],
        [_sds((s_len, aw), f32), _sds((s_len, aw), f32), _sds((s_len, sw), f32), _sds((s_len, sw), bf16),
         _sds((1, sw), f32), _sds((1, aw), f32), _sds((1, sw), f32)],
        [_row(tm, aw), _row(tm, aw), _row(tm, sw), _row(tm, sw), _vec(sw), _vec(aw), _vec(sw)],
        [False] * 4 + [True] * 3)

    (dyg2,) = _mm_nt("glu_dx", dz, wglu_g)
    g_wglu = _mm_tn("glu_dw", yg, dz).reshape(N_DEV, sw // N_DEV, sw)
    scatter["w_glu"], tok_glu = _exchange_start("scatter_w_glu", g_wglu, True, started)
    d_skip_t = d_skip + tok_glu[0:1, 0:1]

    def gelu_bwd_fn(dyg1_, dyg2_, ymm_, u_, dskip):
        _, vjp = jax.vjp(_ypre_fn, ymm_, u_, dskip)
        dymm, du_, ddskip = vjp(dyg1_ + dyg2_)
        return dymm, du_, ddskip

    dymm, du_skip, d_d_skip = _rowwise(
        "ssm_gelu_bwd", gelu_bwd_fn, steps, [dyg1, dyg2, ymm, proj, d_skip_t],
        [_row(tm, sw)] * 3 + [u_spec, _vec(sw)],
        [_sds((s_len, sw), f32), _sds((s_len, sw), f32), _sds((1, sw), f32)],
        [_row(tm, sw), _row(tm, sw), _vec(sw)], [False, False, True])

    dymm_seg = _to_segments(dymm, nseg).astype(bf16)
    du_seg, da_seg, dbb_c, dc_c = _scan("ssm_adj", dymm_seg, w_c, w_bu, a2, reverse=True,
                                        adjoint_of=(u_seg.T, dymm_seg.T, h_re, h_im, hin_f))
    du_ssm = _from_segments(du_seg, nseg)

    def to_gpi(w):
        return _lane_block_diag(w, n_state).transpose(0, 2, 1).reshape(n_groups, n_state * SSM_GROUP)

    d_lam_re, d_lam_im, d_log_step, d_b_re2, d_b_im2 = _ssm_params_bwd(
        lam_re2, lam_im2, log_step2, b_re2, b_im2, expand,
        da_seg[0, 0].reshape(n_groups, n_state), da_seg[1, 0].reshape(n_groups, n_state),
        to_gpi(dbb_c[0]), to_gpi(dbb_c[1]))
    d_c_re = _lane_block_diag(dc_c[0], n_state)
    d_c_im = -_lane_block_diag(dc_c[1], n_state)

    grads_qkv = [_attn_bwd(qn, kn, proj, v_blk, dattn, lse, dd, dil, slopes) for _, dil in DILATIONS]

    def qkv_bwd_fn(q, k, gq, gk, dq1, dq2, dq3, dk1, dk2, dk3, dv1, dv2, dv3, du1, du2):
        _, vjp = jax.vjp(lambda q_, k_, gq_, gk_: (_head_rms(q_, gq_), _head_rms(k_, gk_)), q, k, gq, gk)
        dq, dk, dgq, dgk = vjp((dq1 + dq2 + dq3, dk1 + dk2 + dk3))
        return jnp.concatenate([dq, dk, dv1 + dv2 + dv3, du1 + du2], axis=1), dgq, dgk

    small_g = {"lam_re": d_lam_re, "lam_im": d_lam_im, "log_step": d_log_step, "b_re": d_b_re2, "b_im": d_b_im2,
               "c_re": d_c_re, "c_im": d_c_im, "d_skip": d_d_skip, "b_glu": d_b_glu,
               "attn_out_g": d_attn_out_g, "ssm_out_g": d_ssm_out_g, "norm2_g": d_norm2_g}
    early, tok_early = _exchange_start("gather_early_grads", _pack([small_g[n] for n in SMALL_EARLY]), False, started)

    qkv_cots = [grads_qkv[p][i] for i in range(3) for p in range(3)]
    dproj, small_g["q_norm_g"], small_g["k_norm_g"] = _rowwise(
        "qk_norm_bwd", qkv_bwd_fn, steps,
        [proj, proj, q_norm_g + tok_early[0:1, 0:1], k_norm_g, *qkv_cots, du_skip, du_ssm],
        [_row(tm, aw, 0), _row(tm, aw, 1), _vec(HEAD), _vec(HEAD)] + [_row(tm, aw)] * 9 + [_row(tm, sw)] * 2,
        [_sds((s_len, 3 * aw + sw), bf16), _sds((1, HEAD), f32), _sds((1, HEAD), f32)],
        [_row(tm, 3 * aw + sw), _vec(HEAD), _vec(HEAD)], [False, True, True])

    g_win = _mm_tn_sharded("in_dw", h, dproj, N_DEV)
    scatter["w_in"], tok_in = _exchange_start("scatter_w_in", g_win, True, tok_early)
    dh = _mm_nt("in_dx", dproj, win_g.transpose(1, 0, 2).reshape(d, 3 * aw + sw), after=tok_in)[0]
    norm1_g_t = norm1_g + tok_in[0:1, 0:1]

    def norm1_bwd_fn(dh_, x_, dx1_, gn, sc, sh):
        _, vjp = jax.vjp(_norm_mod, x_, gn, sc, sh)
        dx, dgn, dsc, dsh = vjp(dh_)
        return dx1_ + dx, dgn, dsc, dsh

    grad_x, d_norm1_g, d_sc1, d_sh1 = _rowwise(
        "norm1_bwd", norm1_bwd_fn, steps, [dh, xs, dx1, norm1_g_t, sc1, sh1], [_row(tm, d)] * 3 + [_vec(d)] * 3,
        [_sds((s_len, d), f32)] + [_sds((1, d), f32)] * 3, [_row(tm, d)] + [_vec(d)] * 3,
        [False, True, True, True])

    small_g["b_ada"] = jnp.concatenate([d_sh1, d_sc1, d_g1, d_sh2, d_sc2, d_g2], axis=1)
    small_g["norm1_g"] = d_norm1_g
    (r_late,) = _exchange("gather_late_grads", [_pack([small_g[n] for n in SMALL_LATE])], [False])

    res = {}
    dmod_all = r_late.reshape(N_DEV, -1)[:, :6 * d]
    g_wada = _ada_bwd(c_all, lax.dynamic_slice_in_dim(dmod_all, me * n_ada, n_ada, axis=1))
    res["w_ada"] = _adamw("adamw_w_ada", w_ada[0], m_w_ada[0], v_w_ada[0], g_wada, False)
    after = res["w_ada"][1]
    for name in ("w_ff2", "w_ff1", "w_out", "w_glu", "w_in"):
        stack = _exchange_wait("scattered_" + name, scatter[name], after)
        res[name] = _adamw("adamw_" + name, wts[name][0], mom[name][0], var[name][0], stack, True)
        after = res[name][1]
    r_early = _exchange_wait("gathered_early_grads", early, after)
    for label, names, stack in (("late", SMALL_LATE, r_late), ("early", SMALL_EARLY, r_early)):
        small_res = _adamw("adamw_small_" + label, *packed[names], stack, True, rows=4096)
        off = 0
        for n in names:
            size = wts[n].size
            res[n] = [t.reshape(-1)[off:off + size] for t in small_res]
            off += size

    out = [loss, grad_x[None]]
    for i in range(4):
        out += [res[n][i].reshape(wts[n].shape) for n in ORDER]
    return tuple(out)
```

```python
import math

import jax
import jax.numpy as jnp
from jax import lax
from jax.experimental import pallas as pl
from jax.experimental.pallas import tpu as pltpu

f32, bf16 = jnp.float32, jnp.bfloat16

N_DEV = 8
LANE = 128
HEAD = 128
SSM_GROUP = 16
DILATIONS = ((128, 1), (512, 4), (2048, 16))
BAND = 128
EPS = 1e-6
ADAM_LR, ADAM_B1, ADAM_B2, ADAM_EPS, ADAM_WD, ADAM_STEP = 0.001, 0.9, 0.999, 1e-08, 0.01, 10
NEG = -1e30
VMEM_LIMIT = 60 * 1024 * 1024
HI = lax.Precision.HIGHEST
MESH = pl.DeviceIdType.MESH


def _pcall(body, **kw):
    sem = kw.pop("sem", None)
    kw["compiler_params"] = pltpu.CompilerParams(dimension_semantics=sem, vmem_limit_bytes=VMEM_LIMIT)
    return pl.pallas_call(body, **kw)


def _tile(n, pref):
    t = min(n, pref)
    while n % t:
        t //= 2
    return t


def _sds(shape, dtype):
    return jax.ShapeDtypeStruct(shape, dtype)


def _rowwise(name, fn, steps, ins, in_specs, outs, out_specs, acc):
    n_in = len(ins)

    def body(*refs):
        res = fn(*[r[...] for r in refs[:n_in]])
        res = res if isinstance(res, (tuple, list)) else (res,)
        for r, o, a in zip(refs[n_in:], res, acc):
            if a:
                @pl.when(pl.program_id(0) == 0)
                def _():
                    r[...] = jnp.zeros_like(r)
                r[...] += o
            else:
                r[...] = o.astype(r.dtype)

    return _pcall(body, name=name, grid=(steps,), in_specs=in_specs, out_specs=out_specs, out_shape=outs,
                  sem=("arbitrary",))(*ins)


def _row(tm, c, blk=0):
    return pl.BlockSpec((tm, c), lambda i: (i, blk))


def _vec(c, blk=0):
    return pl.BlockSpec((1, c), lambda i: (0, blk))


def _rms(x, g):
    return x * lax.rsqrt(jnp.mean(x * x, axis=-1, keepdims=True) + EPS) * g


def _norm_mod(x, g, sc, sh):
    return _rms(x, g) * (1.0 + sc) + sh


def _head_rms(t, g):
    return jnp.concatenate([_rms(t[:, h * HEAD:(h + 1) * HEAD], g) for h in range(t.shape[1] // HEAD)], axis=1)


def _mix_fn(attn, yg, z, bglu, ga, gs):
    ssm = yg * jax.nn.sigmoid(z + bglu)
    return jnp.concatenate([_rms(attn, ga), _rms(ssm, gs)], axis=1)


def _ypre_fn(ymm, u, dskip):
    return jax.nn.gelu(ymm + dskip * u)


def _matmul(name, a, b, *, dims, grid, a_spec, b_spec, acc_shape, outs, out_specs, extra=(), extra_specs=(),
            epilogue=None, after=None):
    gk = grid[2]
    n_x = len(extra)
    placed = [] if after is None else [after]
    first_out = n_x + len(placed)
    ins = [a, b, *extra, *placed]
    in_specs = [a_spec, b_spec, *extra_specs] + [pl.BlockSpec(memory_space=pl.ANY)] * len(placed)

    def product(a_ref, b_ref):
        return lax.dot_general(a_ref[...].astype(bf16), b_ref[...].astype(bf16), (dims, ((), ())),
                               preferred_element_type=f32)

    def finish(res, x_refs, o_refs):
        res = epilogue(res, *[r[...] for r in x_refs]) if epilogue is not None else (res,)
        for r, o in zip(o_refs, res):
            r[...] = o.astype(r.dtype)

    def body_single(a_ref, b_ref, *rest):
        finish(product(a_ref, b_ref), rest[:n_x], rest[first_out:])

    def body_pair(a_ref, b_ref, *rest):
        acc = rest[-1]
        prod = product(a_ref, b_ref)

        @pl.when(pl.program_id(2) == 0)
        def _():
            acc[...] = prod

        @pl.when(pl.program_id(2) == 1)
        def _():
            finish(acc[...] + prod, rest[:n_x], rest[first_out:-1])

    def body(a_ref, b_ref, *rest):
        acc = rest[-1]
        k = pl.program_id(2)

        @pl.when(k == 0)
        def _():
            acc[...] = jnp.zeros_like(acc)

        acc[...] += product(a_ref, b_ref)

        @pl.when(k == gk - 1)
        def _():
            finish(acc[...], rest[:n_x], rest[first_out:-1])

    sem = ("parallel", "parallel", "arbitrary")
    if gk == 1:
        return _pcall(body_single, name=name, grid=grid, in_specs=in_specs, out_specs=out_specs, out_shape=outs,
                      sem=sem)(*ins)
    return _pcall(body_pair if gk == 2 else body, name=name, grid=grid, in_specs=in_specs, out_specs=out_specs,
                  out_shape=outs, scratch_shapes=[pltpu.VMEM(acc_shape, f32)], sem=sem)(*ins)


NN = ((1,), (0,))
NT = ((1,), (1,))
TN = ((0,), (0,))


def _mm_nn(name, a, b, out_dtype=f32, tm=1024, tn=1024, tk=2048, epilogue=None, extra=(), outs=None):
    m, kd = a.shape
    n = b.shape[1]
    tm, tn, tk = _tile(m, tm), _tile(n, tn), _tile(kd, tk)
    o_spec = pl.BlockSpec((tm, tn), lambda i, j, k: (i, j))
    outs = outs if outs is not None else [_sds((m, n), out_dtype)]
    return _matmul(name, a, b, dims=NN, grid=(m // tm, n // tn, kd // tk),
                   a_spec=pl.BlockSpec((tm, tk), lambda i, j, k: (i, k)),
                   b_spec=pl.BlockSpec((tk, tn), lambda i, j, k: (k, j)),
                   acc_shape=(tm, tn), outs=outs, out_specs=[o_spec] * len(outs),
                   extra=extra, extra_specs=[o_spec] * len(extra), epilogue=epilogue)


def _mm_nn_sharded(name, a, b3, out_dtype=f32, tm=1024, tk=2048, epilogue=None, outs=None):
    m, kd = a.shape
    nsh, _, n = b3.shape
    tm, tk = _tile(m, tm), _tile(kd, tk)
    o_spec = pl.BlockSpec((tm, n), lambda i, j, k: (i, j))
    outs = outs if outs is not None else [_sds((m, nsh * n), out_dtype)]
    return _matmul(name, a, b3, dims=NN, grid=(m // tm, nsh, kd // tk),
                   a_spec=pl.BlockSpec((tm, tk), lambda i, j, k: (i, k)),
                   b_spec=pl.BlockSpec((None, tk, n), lambda i, j, k: (j, k, 0)),
                   acc_shape=(tm, n), outs=outs, out_specs=[o_spec] * len(outs), epilogue=epilogue)


def _mm_nt(name, a, b, out_dtype=f32, tm=1024, tn=1024, tk=2048, epilogue=None, extra=(), outs=None, after=None):
    m, kd = a.shape
    n = b.shape[0]
    tm, tn, tk = _tile(m, tm), _tile(n, tn), _tile(kd, tk)
    o_spec = pl.BlockSpec((tm, tn), lambda i, j, k: (i, j))
    outs = outs if outs is not None else [_sds((m, n), out_dtype)]
    return _matmul(name, a, b, dims=NT, grid=(m // tm, n // tn, kd // tk),
                   a_spec=pl.BlockSpec((tm, tk), lambda i, j, k: (i, k)),
                   b_spec=pl.BlockSpec((tn, tk), lambda i, j, k: (j, k)),
                   acc_shape=(tm, tn), outs=outs, out_specs=[o_spec] * len(outs),
                   extra=extra, extra_specs=[o_spec] * len(extra), epilogue=epilogue, after=after)


def _mm_tn(name, a, b, out_dtype=bf16, tm=1024, tn=1024, tk=2048, after=None):
    t, m = a.shape
    n = b.shape[1]
    tm, tn, tk = _tile(m, tm), _tile(n, tn), _tile(t, tk)
    return _matmul(name, a, b, dims=TN, grid=(m // tm, n // tn, t // tk),
                   a_spec=pl.BlockSpec((tk, tm), lambda i, j, k: (k, i)),
                   b_spec=pl.BlockSpec((tk, tn), lambda i, j, k: (k, j)),
                   acc_shape=(tm, tn), outs=[_sds((m, n), out_dtype)],
                   out_specs=[pl.BlockSpec((tm, tn), lambda i, j, k: (i, j))], after=after)[0]


def _mm_tn_sharded(name, a, b, nsh, out_dtype=bf16, tm=1024, tk=2048):
    t, m = a.shape
    n = b.shape[1] // nsh
    tm, tk = _tile(m, tm), _tile(t, tk)
    return _matmul(name, a, b, dims=TN, grid=(m // tm, nsh, t // tk),
                   a_spec=pl.BlockSpec((tk, tm), lambda i, j, k: (k, i)),
                   b_spec=pl.BlockSpec((tk, n), lambda i, j, k: (k, j)),
                   acc_shape=(tm, n), outs=[_sds((nsh, m, n), out_dtype)],
                   out_specs=[pl.BlockSpec((None, tm, n), lambda i, j, k: (j, i, 0))])[0]


SCAN_CHAINS = 8


def _scan_segments(s_len):
    nch = SCAN_CHAINS
    while s_len % (8 * nch) or (s_len // (8 * nch)) & (s_len // (8 * nch) - 1):
        nch //= 2
    return 8 * nch


def _to_segments(t, nseg):
    s_len, c = t.shape
    return t.reshape(nseg, s_len // nseg, c).transpose(1, 0, 2).reshape(s_len, c)


def _from_segments(t, nseg):
    s_len, c = t.shape
    return t.reshape(s_len // nseg, nseg, c).transpose(1, 0, 2).reshape(s_len, c)


def _lane_block_weights(t3, n_state):
    n_groups = t3.shape[0]
    gpl = LANE // n_state
    per = LANE // (gpl * SSM_GROUP)
    n_lb = n_groups // gpl
    t5 = t3.reshape(n_lb // per, per, gpl, SSM_GROUP, n_state)
    w = jnp.einsum("aqgic,gh,qs->aqsgihc", t5, jnp.eye(gpl, dtype=t3.dtype), jnp.eye(per, dtype=t3.dtype))
    return w.reshape(n_lb, LANE, LANE).astype(bf16)


def _lane_block_diag(w, n_state):
    gpl = LANE // n_state
    per = LANE // (gpl * SSM_GROUP)
    n_lb = w.shape[0]
    w7 = w.reshape(n_lb // per, per, per, gpl, SSM_GROUP, gpl, n_state)
    t5 = jnp.einsum("aqsgihc,gh,qs->aqgic", w7, jnp.eye(gpl, dtype=w.dtype), jnp.eye(per, dtype=w.dtype))
    return t5.reshape(n_lb * gpl, SSM_GROUP, n_state)


def _scan(name, src, w_in, w_out, a2, *, reverse, adjoint_of=None):
    s_len, n_ch = src.shape
    gp = a2.shape[1]
    per = (gp // LANE) // (n_ch // LANE)
    nseg = _scan_segments(s_len)
    nch = nseg // 8
    seg = s_len // nseg
    n_sq = int(math.log2(seg))
    assert 2 ** n_sq == seg
    adj = adjoint_of is not None
    chunk = _tile(s_len, 1024)
    n_chunks = s_len // chunk

    def body(*refs):
        it = iter(refs)
        src_ref, wi_ref, wo_ref, a_ref = (next(it) for _ in range(4))
        if adj:
            ut_ref, dyt_ref, hr_ref, hi_ref, hin_ref = (next(it) for _ in range(5))
        res_ref = next(it)
        if adj:
            da_ref, dbr_ref, dbi_ref, dcr_ref, dci_ref = (next(it) for _ in range(5))
        else:
            or_ref, oi_ref, oin_ref = (next(it) for _ in range(3))
        if adj:
            or_ref, oi_ref = next(it), next(it)

        for i in range(n_chunks):
            x2 = jnp.dot(src_ref[i * chunk:(i + 1) * chunk, :], wi_ref[...], preferred_element_type=f32)
            or_ref[i * chunk:(i + 1) * chunk, :] = x2[:, :LANE]
            oi_ref[i * chunk:(i + 1) * chunk, :] = x2[:, LANE:]

        ar = a_ref[0:1, :]
        ai = -a_ref[1:2, :] if reverse else a_ref[1:2, :]
        arb, aib = jnp.broadcast_to(ar, (8, LANE)), jnp.broadcast_to(ai, (8, LANE))

        def rows(ch, k):
            return pl.ds(pl.multiple_of(k * nseg + ch * 8, 8), 8)

        def advance(h, ch, k):
            hr, hi = h
            return (arb * hr - aib * hi + or_ref[rows(ch, k), :], arb * hi + aib * hr + oi_ref[rows(ch, k), :])

        def kk(n):
            return seg - 1 - n if reverse else n

        zero = jnp.zeros((8, LANE), f32)

        def sweep1(n, hs):
            return tuple(advance(hs[ch], ch, kk(n)) for ch in range(nch))

        ends = lax.fori_loop(0, seg, sweep1, tuple((zero, zero) for _ in range(nch)))

        pr, pi = ar, ai
        for _ in range(n_sq):
            pr, pi = pr * pr - pi * pi, 2.0 * pr * pi
        in_r, in_i = [None] * nseg, [None] * nseg
        cr = ci = jnp.zeros((1, LANE), f32)
        for j in (range(nseg - 1, -1, -1) if reverse else range(nseg)):
            in_r[j], in_i[j] = cr, ci
            er, ei = ends[j // 8][0][j % 8:j % 8 + 1, :], ends[j // 8][1][j % 8:j % 8 + 1, :]
            cr, ci = er + pr * cr - pi * ci, ei + pr * ci + pi * cr
        h0 = tuple((jnp.concatenate(in_r[8 * ch:8 * ch + 8], axis=0), jnp.concatenate(in_i[8 * ch:8 * ch + 8], axis=0))
                   for ch in range(nch))
        if not adj:
            for ch in range(nch):
                oin_ref[0, 8 * ch:8 * ch + 8, :] = h0[ch][0]
                oin_ref[1, 8 * ch:8 * ch + 8, :] = h0[ch][1]

        def emit(ch, k, h):
            or_ref[rows(ch, k), :] = h[0]
            oi_ref[rows(ch, k), :] = h[1]

        def pair(h, p):
            return h[0] * p[0] + h[1] * p[1], h[1] * p[0] - h[0] * p[1]

        def sweep2(n, carry):
            k = kk(n)
            new = tuple(advance(carry[ch], ch, k) for ch in range(nch))
            for ch in range(nch):
                emit(ch, k, new[ch])
            if not adj:
                return new
            dr, di = carry[nch]
            for ch in range(nch):
                qr, qi = pair(new[ch], (hr_ref[rows(ch, k - 1), :], hi_ref[rows(ch, k - 1), :]))
                dr, di = dr + qr, di + qi
            return new + ((dr, di),)

        if adj:
            carry = lax.fori_loop(0, seg - 1, sweep2, h0 + ((zero, zero),))
            dr, di = carry[nch]
            for ch in range(nch):
                new = advance(carry[ch], ch, 0)
                emit(ch, 0, new)
                qr, qi = pair(new, (hin_ref[0, 8 * ch:8 * ch + 8, :], hin_ref[1, 8 * ch:8 * ch + 8, :]))
                dr, di = dr + qr, di + qi
            da_ref[0] = jnp.sum(dr, axis=0, keepdims=True)
            da_ref[1] = jnp.sum(di, axis=0, keepdims=True)
        else:
            lax.fori_loop(0, seg, sweep2, h0)

        first = pl.program_id(0) % per == 0
        for i in range(n_chunks):
            sl = slice(i * chunk, (i + 1) * chunk)
            h2 = jnp.concatenate([or_ref[sl, :], oi_ref[sl, :]], axis=1).astype(bf16)
            part = lax.dot_general(h2, wo_ref[...], (NT, ((), ())), preferred_element_type=f32)

            @pl.when(first)
            def _():
                res_ref[sl, :] = part

            @pl.when(jnp.logical_not(first))
            def _():
                res_ref[sl, :] += part

        if adj:
            def over_time(xt_ref, yr_ref, yi_ref):
                tot = jnp.zeros((LANE, 2 * LANE), f32)
                for i in range(n_chunks):
                    sl = slice(i * chunk, (i + 1) * chunk)
                    y2 = jnp.concatenate([yr_ref[sl, :], yi_ref[sl, :]], axis=1).astype(bf16)
                    tot += jnp.dot(xt_ref[:, sl], y2, preferred_element_type=f32)
                return tot[:, :LANE], tot[:, LANE:]

            dbr_ref[...], dbi_ref[...] = over_time(ut_ref, or_ref, oi_ref)
            dcr_ref[...], dci_ref[...] = over_time(dyt_ref, hr_ref, hi_ref)

    col = pl.BlockSpec((s_len, LANE), lambda l: (0, l))
    chan = pl.BlockSpec((s_len, LANE), lambda l: (0, l // per))
    in_spec = pl.BlockSpec((2, nseg, LANE), lambda l: (0, 0, l))
    w_spec = pl.BlockSpec((None, LANE, LANE), lambda l: (l, 0, 0))
    w2_spec = pl.BlockSpec((None, LANE, 2 * LANE), lambda l: (l, 0, 0))
    ins = [src, jnp.concatenate(w_in, axis=2), jnp.concatenate(w_out, axis=2), a2]
    in_specs = [chan, w2_spec, w2_spec, pl.BlockSpec((8, LANE), lambda l: (0, l))]
    outs, out_specs = [_sds((s_len, n_ch), f32)], [chan]
    scratch = []
    if adj:
        ins += list(adjoint_of)
        chan_t = pl.BlockSpec((LANE, s_len), lambda l: (l // per, 0))
        in_specs += [chan_t, chan_t, col, col, in_spec]
        outs += [_sds((2, 1, gp), f32)] + [_sds((gp // LANE, LANE, LANE), f32)] * 4
        out_specs += [pl.BlockSpec((2, 1, LANE), lambda l: (0, 0, l))] + [w_spec] * 4
        scratch = [pltpu.VMEM((s_len, LANE), f32)] * 2
    else:
        outs += [_sds((s_len, gp), f32)] * 2 + [_sds((2, nseg, gp), f32)]
        out_specs += [col, col, in_spec]
    res = _pcall(body, name=name, grid=(gp // LANE,), in_specs=in_specs, out_specs=out_specs, out_shape=outs,
                 scratch_shapes=scratch, sem=("arbitrary",))(*ins)
    if adj:
        return res[0], res[1], (res[2], res[3]), (res[4], res[5])
    return res


def _ssm_param_fn(lam_re, lam_im, log_step, b_re2, b_im2, expand):
    step = jnp.exp(log_step)
    xr, xi = lam_re * step, lam_im * step
    mag = jnp.exp(xr)
    ar, ai = mag * jnp.cos(xi), mag * jnp.sin(xi)
    nr, ni = ar - 1.0, ai
    den = lam_re * lam_re + lam_im * lam_im
    cr = (nr * lam_re + ni * lam_im) / den
    ci = (ni * lam_re - nr * lam_im) / den
    cre = jnp.dot(cr, expand, precision=HI, preferred_element_type=f32)
    cie = jnp.dot(ci, expand, precision=HI, preferred_element_type=f32)
    return ar, ai, cre * b_re2 - cie * b_im2, cre * b_im2 + cie * b_re2


def _ssm_params(lam_re, lam_im, log_step, b_re2, b_im2, expand):
    def body(*refs):
        res = _ssm_param_fn(*[r[...] for r in refs[:6]])
        for r, o in zip(refs[6:], res):
            r[...] = o

    g, p = lam_re.shape
    return _pcall(body, name="ssm_params", out_shape=[_sds((g, p), f32)] * 2 + [_sds(b_re2.shape, f32)] * 2)(
        lam_re, lam_im, log_step, b_re2, b_im2, expand)


def _ssm_params_bwd(lam_re, lam_im, log_step, b_re2, b_im2, expand, d_ar, d_ai, d_bbr, d_bbi):
    def body(*refs):
        prim = [r[...] for r in refs[:5]]
        ex = refs[5][...]
        cot = tuple(r[...] for r in refs[6:10])
        _, vjp = jax.vjp(lambda *p_: _ssm_param_fn(*p_, ex), *prim)
        for r, o in zip(refs[10:], vjp(cot)):
            r[...] = o

    shapes = [lam_re.shape, lam_im.shape, log_step.shape, b_re2.shape, b_im2.shape]
    return _pcall(body, name="ssm_params_bwd", out_shape=[_sds(s, f32) for s in shapes])(
        lam_re, lam_im, log_step, b_re2, b_im2, expand, d_ar, d_ai, d_bbr, d_bbi)


def _slope_table(n_heads):
    s = 2.0 ** (-8.0 * (jnp.arange(n_heads, dtype=f32) + 1.0) / n_heads)
    return jnp.broadcast_to(s[:, None, None], (n_heads, 1, LANE))


def _band_bias(slope_d, shift):
    qi = lax.broadcasted_iota(jnp.int32, (BAND, BAND), 0)
    ki = lax.broadcasted_iota(jnp.int32, (BAND, BAND), 1)
    mask = (ki >= qi) if shift else (ki <= qi)
    return jnp.where(mask, -slope_d * (qi - ki + shift).astype(f32), NEG)


def _window_bias(slope_d, has_prev):
    own = _band_bias(slope_d, 0)
    mid = jnp.concatenate([_band_bias(slope_d, BAND), own], axis=1)
    none = jnp.concatenate([jnp.full((BAND, BAND), NEG, f32), own], axis=1)
    return mid, jnp.where(has_prev, mid, none)


def _window_scores(q, k2, bias):
    return lax.dot_general(q, k2, (NT, ((), ())), preferred_element_type=f32) * (HEAD ** -0.5) + bias


def _attn_geometry(s_len, dil, rows=1024):
    piece = BAND * dil
    m = max(1, rows // piece)
    while s_len % (piece * m):
        m //= 2
    return m, piece


def _stream_rows(start, dil):
    return pl.ds(start, BAND, stride=dil) if dil > 1 else pl.ds(start, BAND)


def _attn_fwd(qn, kn, proj, v_blk, dil, slopes):
    s_len, aw = qn.shape
    n_heads = aw // HEAD
    m, piece = _attn_geometry(s_len, dil)
    rows = m * piece

    def body(q_ref, k_ref, kp_ref, v_ref, vp_ref, sl_ref, o_ref, lse_ref):
        bias_mid, bias_first = _window_bias(sl_ref[:, 0:1] * float(dil), pl.program_id(1) > 0)
        for b in range(m):
            for r in range(dil):
                idx = _stream_rows(b * piece + r, dil)
                q, kc, vc = (ref[idx, :].astype(bf16) for ref in (q_ref, k_ref, v_ref))
                if b:
                    pidx = _stream_rows((b - 1) * piece + r, dil)
                    kp, vp = k_ref[pidx, :].astype(bf16), v_ref[pidx, :].astype(bf16)
                else:
                    pidx = _stream_rows(r, dil)
                    kp, vp = kp_ref[pidx, :].astype(bf16), vp_ref[pidx, :].astype(bf16)
                k2, v2 = jnp.concatenate([kp, kc], axis=0), jnp.concatenate([vp, vc], axis=0)
                s = _window_scores(q, k2, bias_mid if b else bias_first)
                mx = jnp.max(s, axis=1, keepdims=True)
                p = jnp.exp(s - mx)
                den = jnp.sum(p, axis=1, keepdims=True)
                o_ref[idx, :] = jnp.dot(p.astype(bf16), v2, preferred_element_type=f32) / den
                lse_ref[idx, :] = jnp.broadcast_to(mx + jnp.log(den), (BAND, HEAD))

    def cur(blk0):
        return pl.BlockSpec((rows, HEAD), lambda h, t: (t, blk0 + h))

    def prev(blk0):
        return pl.BlockSpec((piece, HEAD), lambda h, t: (jnp.maximum(t * m - 1, 0), blk0 + h))

    sl = pl.BlockSpec((None, 1, LANE), lambda h, t: (h, 0, 0))
    return _pcall(body, name=f"attn_fwd_d{dil}", grid=(n_heads, s_len // rows),
                  in_specs=[cur(0), cur(0), prev(0), cur(v_blk), prev(v_blk), sl], out_specs=[cur(0), cur(0)],
                  out_shape=[_sds((s_len, aw), f32)] * 2, sem=("parallel", "parallel"))(
        qn, kn, kn, proj, proj, slopes)


def _attn_bwd(qn, kn, proj, v_blk, do, lse, dd, dil, slopes):
    s_len, aw = qn.shape
    n_heads = aw // HEAD
    m, piece = _attn_geometry(s_len, dil, max(1024, 2 * BAND * dil))
    rows = m * piece
    n_tiles = s_len // rows
    scale = HEAD ** -0.5

    def body(q_ref, qx_ref, k_ref, kp_ref, v_ref, vp_ref, do_ref, dox_ref, l_ref, lx_ref, d_ref, dx_ref, sl_ref,
             dq_ref, dk_ref, dv_ref):
        t = pl.program_id(1)
        slope_d = sl_ref[:, 0:1] * float(dil)
        bias_mid, bias_first = _window_bias(slope_d, t > 0)
        bias_next = _band_bias(slope_d, BAND)

        def query_side(ref_q, ref_do, ref_l, ref_d, idx):
            return (ref_q[idx, :].astype(bf16), ref_do[idx, :].astype(bf16), ref_l[idx, :][:, 0:1],
                    ref_d[idx, :][:, 0:1])

        def probs(qs, keys, values, bias):
            q, do_, l_col, d_col = qs
            p = jnp.exp(_window_scores(q, keys, bias) - l_col)
            dp = lax.dot_general(do_, values, (NT, ((), ())), preferred_element_type=f32)
            return p.astype(bf16), (p * (dp - d_col)).astype(bf16)

        def tn(a_, b_):
            return lax.dot_general(a_, b_, (TN, ((), ())), preferred_element_type=f32)

        for r in range(dil):
            pend = None
            for b in range(m):
                idx = _stream_rows(b * piece + r, dil)
                qs = query_side(q_ref, do_ref, l_ref, d_ref, idx)
                kc, vc = k_ref[idx, :].astype(bf16), v_ref[idx, :].astype(bf16)
                if b:
                    kp, vp = kc_prev, vc_prev
                else:
                    pidx = _stream_rows(r, dil)
                    kp, vp = kp_ref[pidx, :].astype(bf16), vp_ref[pidx, :].astype(bf16)
                k2, v2 = jnp.concatenate([kp, kc], axis=0), jnp.concatenate([vp, vc], axis=0)
                p, ds = probs(qs, k2, v2, bias_mid if b else bias_first)
                dq_ref[idx, :] = jnp.dot(ds, k2, preferred_element_type=f32) * scale
                dk2, dv2 = tn(ds, qs[0]), tn(p, qs[1])
                if pend is not None:
                    dk_ref[pend[0], :] = (pend[1] + dk2[:BAND]) * scale
                    dv_ref[pend[0], :] = pend[2] + dv2[:BAND]
                pend = (idx, dk2[BAND:], dv2[BAND:])
                kc_prev, vc_prev = kc, vc
            qs = query_side(qx_ref, dox_ref, lx_ref, dx_ref, _stream_rows(r, dil))
            p, ds = probs(qs, kc_prev, vc_prev, bias_next)
            live = t < n_tiles - 1
            dk_ref[pend[0], :] = (pend[1] + jnp.where(live, tn(ds, qs[0]), 0.0)) * scale
            dv_ref[pend[0], :] = pend[2] + jnp.where(live, tn(p, qs[1]), 0.0)

    def cur(blk0):
        return pl.BlockSpec((rows, HEAD), lambda h, t: (t, blk0 + h))

    def prev(blk0):
        return pl.BlockSpec((piece, HEAD), lambda h, t: (jnp.maximum(t * m - 1, 0), blk0 + h))

    def nxt(blk0):
        return pl.BlockSpec((piece, HEAD), lambda h, t: (jnp.minimum(t * m + m, n_tiles * m - 1), blk0 + h))

    sl = pl.BlockSpec((None, 1, LANE), lambda h, t: (h, 0, 0))
    return _pcall(body, name=f"attn_bwd_d{dil}", grid=(n_heads, n_tiles),
                  in_specs=[cur(0), nxt(0), cur(0), prev(0), cur(v_blk), prev(v_blk), cur(0), nxt(0), cur(0), nxt(0),
                            cur(0), nxt(0), sl],
                  out_specs=[cur(0)] * 3, out_shape=[_sds((s_len, aw), f32)] * 3,
                  sem=("parallel", "parallel"))(qn, qn, kn, kn, proj, proj, do, do, lse, lse, dd, dd, slopes)


def _exchange(name, srcs, scatter):
    n = len(srcs)

    def body(*refs):
        src, out = refs[:n], refs[n:2 * n]
        send_sems, recv_sems, local_sems = refs[2 * n:]
        x, y, c = lax.axis_index("x"), lax.axis_index("y"), lax.axis_index("c")
        me = 4 * x + 2 * y + c

        def peer(r):
            return ((1 - x) if r & 4 else x, (1 - y) if r & 2 else y, (1 - c) if r & 1 else c)

        def lin(p):
            return 4 * p[0] + 2 * p[1] + p[2]

        def piece(a, idx):
            return src[a].at[idx] if scatter[a] else src[a]

        local, sends = [], []
        for a in range(n):
            cp = pltpu.make_async_copy(piece(a, me), out[a].at[me], local_sems.at[a])
            cp.start()
            local.append(cp)
        for r in range(1, N_DEV):
            p = peer(r)
            for a in range(n):
                cp = pltpu.make_async_remote_copy(src_ref=piece(a, lin(p)), dst_ref=out[a].at[me],
                                                  send_sem=send_sems.at[a, r - 1], recv_sem=recv_sems.at[a, r - 1],
                                                  device_id=p, device_id_type=MESH)
                cp.start()
                sends.append(cp)
        for r in range(1, N_DEV):
            p = peer(r)
            for a in range(n):
                pltpu.make_async_remote_copy(src_ref=piece(a, lin(p)), dst_ref=out[a].at[lin(p)],
                                             send_sem=send_sems.at[a, r - 1], recv_sem=recv_sems.at[a, r - 1],
                                             device_id=p, device_id_type=MESH).wait_recv()
        for cp in sends:
            cp.wait_send()
        for cp in local:
            cp.wait()

    def piece_shape(a):
        return srcs[a].shape[1:] if scatter[a] else srcs[a].shape

    any_spec = pl.BlockSpec(memory_space=pl.ANY)
    return _pcall(body, name=name, in_specs=[any_spec] * n, out_specs=[any_spec] * n,
                  out_shape=[_sds((N_DEV, *piece_shape(a)), srcs[a].dtype) for a in range(n)],
                  scratch_shapes=[pltpu.SemaphoreType.DMA((n, N_DEV - 1)), pltpu.SemaphoreType.DMA((n, N_DEV - 1)),
                                  pltpu.SemaphoreType.DMA((n,))])(*srcs)


_HBM = pl.BlockSpec(memory_space=pltpu.HBM)
_SEM = pl.BlockSpec(memory_space=pltpu.SEMAPHORE)
_EFFECT = pltpu.SideEffectType.DATAFLOW_SIDE_EFFECTING


def _peer_ids():
    x, y, c = lax.axis_index("x"), lax.axis_index("y"), lax.axis_index("c")
    peers = [((1 - x) if r & 4 else x, (1 - y) if r & 2 else y, (1 - c) if r & 1 else c) for r in range(1, N_DEV)]
    return 4 * x + 2 * y + c, peers, [4 * p[0] + 2 * p[1] + p[2] for p in peers]


def _exchange_start(name, src, scatter, after):
    piece_shape = src.shape[1:] if scatter else src.shape

    def body(src_ref, land_ref, after_ref, send_sems, recv_sems, local_sem, src_thru, land_thru, token):
        me, peers, lins = _peer_ids()

        def piece(idx):
            return src_ref.at[idx] if scatter else src_ref

        pltpu.make_async_copy(piece(me), land_ref.at[me], local_sem).start()
        for r, (p, lp) in enumerate(zip(peers, lins)):
            pltpu.make_async_remote_copy(src_ref=piece(lp), dst_ref=land_ref.at[me], send_sem=send_sems.at[r],
                                         recv_sem=recv_sems.at[r], device_id=p, device_id_type=MESH).start()
        token[...] = jnp.zeros_like(token)

    land = pltpu.with_memory_space_constraint(lax.empty((N_DEV, *piece_shape), src.dtype), pltpu.HBM)
    send_sems, recv_sems, local_sem, src_thru, land_thru, token = pl.pallas_call(
        body, name=name,
        out_shape=(pltpu.SemaphoreType.DMA((N_DEV - 1,)), pltpu.SemaphoreType.DMA((N_DEV - 1,)),
                   pltpu.SemaphoreType.DMA(()), pltpu.HBM(src.shape, src.dtype),
                   pltpu.HBM((N_DEV, *piece_shape), src.dtype), _sds((8, LANE), f32)),
        in_specs=(_HBM, _HBM, pl.BlockSpec(memory_space=pl.ANY)),
        out_specs=(_SEM, _SEM, _SEM, _HBM, _HBM, pl.BlockSpec(memory_space=pltpu.VMEM)),
        input_output_aliases={0: 3, 1: 4},
        compiler_params=pltpu.CompilerParams(has_side_effects=_EFFECT),
    )(pltpu.with_memory_space_constraint(src, pltpu.HBM), land, after)
    return (send_sems, recv_sems, local_sem, src_thru, land_thru, scatter), token


def _exchange_wait(name, handle, *after):
    send_sems, recv_sems, local_sem, src_thru, land_thru, scatter = handle

    def body(src_ref, land_ref, send_sems_, recv_sems_, local_sem_, *rest):
        me, peers, lins = _peer_ids()

        def piece(idx):
            return src_ref.at[idx] if scatter else src_ref

        pltpu.make_async_copy(piece(me), land_ref.at[me], local_sem_).wait()
        for r, (p, lp) in enumerate(zip(peers, lins)):
            pltpu.make_async_remote_copy(src_ref=piece(lp), dst_ref=land_ref.at[me], send_sem=send_sems_.at[r],
                                         recv_sem=recv_sems_.at[r], device_id=p, device_id_type=MESH).wait_send()
            pltpu.make_async_remote_copy(src_ref=piece(lp), dst_ref=land_ref.at[lp], send_sem=send_sems_.at[r],
                                         recv_sem=recv_sems_.at[r], device_id=p, device_id_type=MESH).wait_recv()

    return pl.pallas_call(
        body, name=name,
        out_shape=(pltpu.HBM(src_thru.shape, src_thru.dtype), pltpu.HBM(land_thru.shape, land_thru.dtype)),
        in_specs=(_HBM, _HBM, _SEM, _SEM, _SEM, *[pl.BlockSpec(memory_space=pl.ANY)] * len(after)),
        out_specs=(_HBM, _HBM), input_output_aliases={0: 0, 1: 1},
        compiler_params=pltpu.CompilerParams(has_side_effects=_EFFECT),
    )(src_thru, land_thru, send_sems, recv_sems, local_sem, *after)[1]


def _adamw(name, w, m, v, g_or_stack, stacked, rows=256):
    r, c = w.shape
    tr = _tile(r, rows)

    def fn(w_, m_, v_, g_):
        if stacked:
            g = g_[0].astype(f32)
            for j in range(1, N_DEV):
                g = g + g_[j].astype(f32)
        else:
            g = g_
        m_new = ADAM_B1 * m_ + (1.0 - ADAM_B1) * g
        v_new = ADAM_B2 * v_ + (1.0 - ADAM_B2) * (g * g)
        m_hat = m_new / (1.0 - ADAM_B1 ** ADAM_STEP)
        v_hat = v_new / (1.0 - ADAM_B2 ** ADAM_STEP)
        delta = -ADAM_LR * (m_hat / (jnp.sqrt(v_hat) + ADAM_EPS) + ADAM_WD * w_)
        return g, delta, m_new, v_new

    blk = _row(tr, c)
    g_spec = pl.BlockSpec((N_DEV, tr, c), lambda i: (0, i, 0)) if stacked else blk
    return _rowwise(name, fn, r // tr, [w, m, v, g_or_stack], [blk, blk, blk, g_spec],
                    [_sds((r, c), f32)] * 4, [blk] * 4, [False] * 4)


def _ada_fwd(c_all, w_shard, b_shard):
    nb_, d = c_all.shape
    n = w_shard.shape[1]
    tn = _tile(n, 512)

    def body(c_ref, w_ref, b_ref, o_ref):
        a = jax.nn.silu(c_ref[...]).astype(bf16)
        o_ref[...] = jnp.dot(a, w_ref[...].astype(bf16), preferred_element_type=f32) + b_ref[...]

    return _pcall(body, name="ada_fwd", grid=(n // tn,),
                  in_specs=[pl.BlockSpec((nb_, d), lambda j: (0, 0)), pl.BlockSpec((d, tn), lambda j: (0, j)),
                            pl.BlockSpec((1, tn), lambda j: (0, j))],
                  out_specs=pl.BlockSpec((nb_, tn), lambda j: (0, j)), out_shape=_sds((nb_, n), f32),
                  sem=("parallel",))(c_all, w_shard, b_shard)


def _ada_bwd(c_all, dmod_cols):
    nb_, d = c_all.shape
    n = dmod_cols.shape[1]
    tn = _tile(n, 512)

    def body(c_ref, g_ref, o_ref):
        a = jax.nn.silu(c_ref[...]).astype(bf16).astype(f32)
        g = g_ref[...].astype(bf16).astype(f32)
        o_ref[...] = lax.dot_general(a, g, (TN, ((), ())), precision=HI, preferred_element_type=f32)

    return _pcall(body, name="ada_bwd", grid=(n // tn,),
                  in_specs=[pl.BlockSpec((nb_, d), lambda j: (0, 0)), pl.BlockSpec((nb_, tn), lambda j: (0, j))],
                  out_specs=pl.BlockSpec((d, tn), lambda j: (0, j)), out_shape=_sds((d, n), f32),
                  sem=("parallel",))(c_all, dmod_cols)


SMALL_LATE = ("b_ada", "norm1_g", "q_norm_g", "k_norm_g")
SMALL_EARLY = ("lam_re", "lam_im", "log_step", "b_re", "b_im", "c_re", "c_im", "d_skip", "b_glu", "attn_out_g",
               "ssm_out_g", "norm2_g")
ORDER = ("w_ada", "b_ada", "norm1_g", "w_in", "q_norm_g", "k_norm_g", "lam_re", "lam_im", "log_step", "b_re", "b_im",
         "c_re", "c_im", "d_skip", "w_glu", "b_glu", "attn_out_g", "ssm_out_g", "w_out", "norm2_g", "w_ff1", "w_ff2")


def _pack(parts):
    flat = jnp.concatenate([p.reshape(-1) for p in parts])
    pad = (-flat.shape[0]) % (8 * LANE)
    return jnp.pad(flat, (0, pad)).reshape(-1, LANE)


def kernel(x, c, w_ada, b_ada, norm1_g, w_in, q_norm_g, k_norm_g, lam_re, lam_im, log_step, b_re, b_im, c_re, c_im, d_skip, w_glu, b_glu, attn_out_g, ssm_out_g, w_out, norm2_g, w_ff1, w_ff2, loss_target, m_w_ada, m_b_ada, m_norm1_g, m_w_in, m_q_norm_g, m_k_norm_g, m_lam_re, m_lam_im, m_log_step, m_b_re, m_b_im, m_c_re, m_c_im, m_d_skip, m_w_glu, m_b_glu, m_attn_out_g, m_ssm_out_g, m_w_out, m_norm2_g, m_w_ff1, m_w_ff2, v_w_ada, v_b_ada, v_norm1_g, v_w_in, v_q_norm_g, v_k_norm_g, v_lam_re, v_lam_im, v_log_step, v_b_re, v_b_im, v_c_re, v_c_im, v_d_skip, v_w_glu, v_b_glu, v_attn_out_g, v_ssm_out_g, v_w_out, v_norm2_g, v_w_ff1, v_w_ff2):
    env = dict(locals())
    wts = {n: env[n] for n in ORDER}
    mom = {n: env["m_" + n] for n in ORDER}
    var = {n: env["v_" + n] for n in ORDER}

    xs, tgt = x[0], loss_target[0]
    s_len, d = xs.shape
    aw = d // 2
    sw = d - aw
    n_heads = aw // HEAD
    n_groups = sw // SSM_GROUP
    n_state = lam_re.shape[-1]
    gp = n_groups * n_state
    tm = _tile(s_len, 256)
    steps = s_len // tm
    me = 4 * lax.axis_index("x") + 2 * lax.axis_index("y") + lax.axis_index("c")

    (c_all,) = _exchange("gather_c", [c], [False])
    c_all = c_all.reshape(N_DEV, d)

    n_ada = w_ada.shape[-1]
    b_ada_cols = lax.dynamic_slice_in_dim(b_ada, me * n_ada, n_ada, axis=1)
    mod_part = _ada_fwd(c_all, w_ada[0], b_ada_cols)
    (mod_all,) = _exchange("gather_mod", [mod_part], [False])
    mod = lax.dynamic_index_in_dim(mod_all, me, axis=1, keepdims=False).reshape(1, 6 * d)
    sh1, sc1, g1, sh2, sc2, g2 = (mod[:, i * d:(i + 1) * d] for i in range(6))

    gather, started = {}, jnp.zeros((1, 1), f32)
    for name in ("w_in", "w_glu", "w_out", "w_ff1", "w_ff2"):
        gather[name], token = _exchange_start("gather_" + name, wts[name][0].astype(bf16), False, mod_all)
        started = started + token[0:1, 0:1]
    sc1 = sc1 + started

    (h,) = _rowwise("norm1", _norm_mod, steps, [xs, norm1_g, sc1, sh1],
                    [_row(tm, d), _vec(d), _vec(d), _vec(d)], [_sds((s_len, d), bf16)], [_row(tm, d)], [False])
    lam_re2, lam_im2 = lam_re[0], lam_im[0]
    log_step2 = log_step[0].reshape(n_groups, 1)
    b_re2 = b_re[0].reshape(n_groups, n_state * SSM_GROUP)
    b_im2 = b_im[0].reshape(n_groups, n_state * SSM_GROUP)
    expand = jnp.repeat(jnp.eye(n_state, dtype=f32), SSM_GROUP, axis=1)
    a_re, a_im, bb_re2, bb_im2 = _ssm_params(lam_re2, lam_im2, log_step2, b_re2, b_im2, expand)
    a2 = jnp.zeros((8, gp), f32).at[0].set(a_re.reshape(gp)).at[1].set(a_im.reshape(gp))
    w_bu = tuple(_lane_block_weights(t.reshape(n_groups, n_state, SSM_GROUP).transpose(0, 2, 1), n_state)
                 for t in (bb_re2, bb_im2))
    w_c = (_lane_block_weights(c_re[0], n_state), _lane_block_weights(-c_im[0], n_state))

    packed = {names: tuple(_pack([t[n] for n in names]) for t in (wts, mom, var))
              for names in (SMALL_LATE, SMALL_EARLY)}

    win_g = _exchange_wait("gathered_w_in", gather["w_in"], h, a2, *w_bu, *w_c, *packed[SMALL_LATE],
                           *packed[SMALL_EARLY])
    (proj,) = _mm_nn_sharded("in_proj", h, win_g)

    def qk_fn(q, k, gq, gk):
        return _head_rms(q, gq), _head_rms(k, gk)

    qn, kn = _rowwise("qk_norm", qk_fn, steps, [proj, proj, q_norm_g, k_norm_g],
                      [_row(tm, aw, 0), _row(tm, aw, 1), _vec(HEAD), _vec(HEAD)],
                      [_sds((s_len, aw), f32)] * 2, [_row(tm, aw)] * 2, [False] * 2)
    v_blk = 2 * aw // HEAD

    slopes = _slope_table(n_heads)
    pat = [_attn_fwd(qn, kn, proj, v_blk, dil, slopes) for _, dil in DILATIONS]

    def attn_mix_fn(o1, l1, o2, l2, o3, l3):
        m = jnp.maximum(jnp.maximum(l1, l2), l3)
        e1, e2, e3 = jnp.exp(l1 - m), jnp.exp(l2 - m), jnp.exp(l3 - m)
        tot = e1 + e2 + e3
        return (e1 * o1 + e2 * o2 + e3 * o3) / tot, m + jnp.log(tot)

    attn, lse = _rowwise("attn_mix", attn_mix_fn, steps, [t for ol in pat for t in ol], [_row(tm, aw)] * 6,
                         [_sds((s_len, aw), f32)] * 2, [_row(tm, aw)] * 2, [False] * 2)

    nseg = _scan_segments(s_len)
    u_seg = _to_segments(proj[:, 3 * aw:], nseg).astype(bf16)
    y_seg, h_re, h_im, hin_f = _scan("ssm_scan", u_seg, w_bu, w_c, a2, reverse=False)
    ymm = _from_segments(y_seg, nseg)

    u_spec = _row(tm, sw, 3 * aw // sw)
    (yg,) = _rowwise("ssm_gelu", _ypre_fn, steps, [ymm, proj, d_skip], [_row(tm, sw), u_spec, _vec(sw)],
                     [_sds((s_len, sw), f32)], [_row(tm, sw)], [False])
    wglu_g = _exchange_wait("gathered_w_glu", gather["w_glu"], yg).reshape(sw, sw)
    (z,) = _mm_nn("glu_proj", yg, wglu_g)
    (cat,) = _rowwise("mix_norm", _mix_fn, steps, [attn, yg, z, b_glu, attn_out_g, ssm_out_g],
                      [_row(tm, aw), _row(tm, sw), _row(tm, sw), _vec(sw), _vec(aw), _vec(sw)],
                      [_sds((s_len, d), bf16)], [_row(tm, d)], [False])
    wout_g = _exchange_wait("gathered_w_out", gather["w_out"], cat).reshape(d, d)
    (mixed,) = _mm_nn("out_proj", cat, wout_g)

    def res_norm2_fn(x_, mixed_, g1_, gn, sc, sh):
        x1_ = x_ + g1_ * mixed_
        return x1_, _norm_mod(x1_, gn, sc, sh)

    x1, h2 = _rowwise("norm2", res_norm2_fn, steps, [xs, mixed, g1, norm2_g, sc2, sh2],
                      [_row(tm, d), _row(tm, d)] + [_vec(d)] * 4,
                      [_sds((s_len, d), f32), _sds((s_len, d), bf16)], [_row(tm, d)] * 2, [False] * 2)

    def act_epilogue(acc):
        r = jnp.maximum(acc, 0.0)
        return r, r * r

    wff1_g = _exchange_wait("gathered_w_ff1", gather["w_ff1"], h2)
    r_ff, act = _mm_nn_sharded("ff1", h2, wff1_g, epilogue=act_epilogue,
                               outs=[_sds((s_len, 4 * d), bf16), _sds((s_len, 4 * d), bf16)])
    wff2_g = _exchange_wait("gathered_w_ff2", gather["w_ff2"], act).reshape(4 * d, d)
    (ff,) = _mm_nn("ff2", act, wff2_g)

    def loss_fn(x1_, ff_, tgt_, g2_):
        e = x1_ + g2_ * ff_ - tgt_
        dy_ = e * (1.0 / d)
        part = jnp.full((1, LANE), 0.5 / d, f32) * jnp.sum(e * e)
        return dy_, g2_ * dy_, part, jnp.sum(dy_ * ff_, axis=0, keepdims=True)

    dy, dff, loss_part, d_g2 = _rowwise(
        "loss", loss_fn, steps, [x1, ff, tgt, g2], [_row(tm, d)] * 3 + [_vec(d)],
        [_sds((s_len, d), f32), _sds((s_len, d), bf16), _sds((1, LANE), f32), _sds((1, d), f32)],
        [_row(tm, d), _row(tm, d), _vec(LANE), _vec(d)], [False, False, True, True])
    loss = lax.psum(loss_part[0, 0], ("x", "y", "c"))

    def dact_epilogue(acc, r_):
        return (acc * (2.0 * r_.astype(f32)),)

    (da,) = _mm_nt("ff2_dx", dff, wff2_g, epilogue=dact_epilogue, extra=[r_ff], outs=[_sds((s_len, 4 * d), bf16)])
    scatter = {}
    g_wff2 = _mm_tn("ff2_dw", act, dff, after=loss.reshape(1, 1)).reshape(N_DEV, 4 * d // N_DEV, d)
    scatter["w_ff2"], tok_ff2 = _exchange_start("scatter_w_ff2", g_wff2, True, loss.reshape(1, 1))
    dh2 = _mm_nt("ff1_dx", da, wff1_g.transpose(1, 0, 2).reshape(d, 4 * d))[0]
    g_wff1 = _mm_tn_sharded("ff1_dw", h2, da, N_DEV)
    scatter["w_ff1"], tok_ff1 = _exchange_start("scatter_w_ff1", g_wff1, True, started)
    norm2_g_t = norm2_g + (tok_ff2[0:1, 0:1] + tok_ff1[0:1, 0:1])

    def norm2_bwd_fn(dh2_, x1_, dy_, mixed_, gn, sc, sh, g1_):
        _, vjp = jax.vjp(_norm_mod, x1_, gn, sc, sh)
        dx, dgn, dsc, dsh = vjp(dh2_)
        dx1_ = dy_ + dx
        return dx1_, g1_ * dx1_, dgn, dsc, dsh, jnp.sum(dx1_ * mixed_, axis=0, keepdims=True)

    dx1, dmixed, d_norm2_g, d_sc2, d_sh2, d_g1 = _rowwise(
        "norm2_bwd", norm2_bwd_fn, steps, [dh2, x1, dy, mixed, norm2_g_t, sc2, sh2, g1],
        [_row(tm, d)] * 4 + [_vec(d)] * 4,
        [_sds((s_len, d), f32), _sds((s_len, d), bf16)] + [_sds((1, d), f32)] * 4,
        [_row(tm, d)] * 2 + [_vec(d)] * 4, [False, False, True, True, True, True])

    (dcat,) = _mm_nt("out_dx", dmixed, wout_g)
    g_wout = _mm_tn("out_dw", cat, dmixed).reshape(N_DEV, d // N_DEV, d)
    scatter["w_out"], tok_out = _exchange_start("scatter_w_out", g_wout, True, started)
    b_glu_t = b_glu + tok_out[0:1, 0:1]

    def mix_bwd_fn(dcat_, attn_, yg_, z_, bglu, ga, gs):
        _, vjp = jax.vjp(_mix_fn, attn_, yg_, z_, bglu, ga, gs)
        dattn_, dyg_, dz_, dbglu, dga, dgs = vjp(dcat_)
        prod = dattn_ * attn_
        dd_ = jnp.concatenate([jnp.broadcast_to(jnp.sum(prod[:, i * HEAD:(i + 1) * HEAD], axis=1, keepdims=True),
                                                (prod.shape[0], HEAD)) for i in range(n_heads)], axis=1)
        return dattn_, dd_, dyg_, dz_, dbglu, dga, dgs

    dattn, dd, dyg1, dz, d_b_glu, d_attn_out_g, d_ssm_out_g = _rowwise(
        "mix_bwd", mix_bwd_fn, steps, [dcat, attn, yg, z, b_glu_t, attn_out_g, ssm_out_g],
        [_row(tm, d), _row(tm, aw), _row(tm, sw), _row(tm, sw), _vec(sw), _vec(aw), _vec(sw)],
        [_sds((s_len, aw), f32), _sds((s_len, aw), f32), _sds((s_len, sw), f32), _sds((s_len, sw), bf16),
         _sds((1, sw), f32), _sds((1, aw), f32), _sds((1, sw), f32)],
        [_row(tm, aw), _row(tm, aw), _row(tm, sw), _row(tm, sw), _vec(sw), _vec(aw), _vec(sw)],
        [False] * 4 + [True] * 3)

    (dyg2,) = _mm_nt("glu_dx", dz, wglu_g)
    g_wglu = _mm_tn("glu_dw", yg, dz).reshape(N_DEV, sw // N_DEV, sw)
    scatter["w_glu"], tok_glu = _exchange_start("scatter_w_glu", g_wglu, True, started)
    d_skip_t = d_skip + tok_glu[0:1, 0:1]

    def gelu_bwd_fn(dyg1_, dyg2_, ymm_, u_, dskip):
        _, vjp = jax.vjp(_ypre_fn, ymm_, u_, dskip)
        dymm, du_, ddskip = vjp(dyg1_ + dyg2_)
        return dymm, du_, ddskip

    dymm, du_skip, d_d_skip = _rowwise(
        "ssm_gelu_bwd", gelu_bwd_fn, steps, [dyg1, dyg2, ymm, proj, d_skip_t],
        [_row(tm, sw)] * 3 + [u_spec, _vec(sw)],
        [_sds((s_len, sw), f32), _sds((s_len, sw), f32), _sds((1, sw), f32)],
        [_row(tm, sw), _row(tm, sw), _vec(sw)], [False, False, True])

    dymm_seg = _to_segments(dymm, nseg).astype(bf16)
    du_seg, da_seg, dbb_c, dc_c = _scan("ssm_adj", dymm_seg, w_c, w_bu, a2, reverse=True,
                                        adjoint_of=(u_seg.T, dymm_seg.T, h_re, h_im, hin_f))
    du_ssm = _from_segments(du_seg, nseg)

    def to_gpi(w):
        return _lane_block_diag(w, n_state).transpose(0, 2, 1).reshape(n_groups, n_state * SSM_GROUP)

    d_lam_re, d_lam_im, d_log_step, d_b_re2, d_b_im2 = _ssm_params_bwd(
        lam_re2, lam_im2, log_step2, b_re2, b_im2, expand,
        da_seg[0, 0].reshape(n_groups, n_state), da_seg[1, 0].reshape(n_groups, n_state),
        to_gpi(dbb_c[0]), to_gpi(dbb_c[1]))
    d_c_re = _lane_block_diag(dc_c[0], n_state)
    d_c_im = -_lane_block_diag(dc_c[1], n_state)

    grads_qkv = [_attn_bwd(qn, kn, proj, v_blk, dattn, lse, dd, dil, slopes) for _, dil in DILATIONS]

    def qkv_bwd_fn(q, k, gq, gk, dq1, dq2, dq3, dk1, dk2, dk3, dv1, dv2, dv3, du1, du2):
        _, vjp = jax.vjp(lambda q_, k_, gq_, gk_: (_head_rms(q_, gq_), _head_rms(k_, gk_)), q, k, gq, gk)
        dq, dk, dgq, dgk = vjp((dq1 + dq2 + dq3, dk1 + dk2 + dk3))
        return jnp.concatenate([dq, dk, dv1 + dv2 + dv3, du1 + du2], axis=1), dgq, dgk

    small_g = {"lam_re": d_lam_re, "lam_im": d_lam_im, "log_step": d_log_step, "b_re": d_b_re2, "b_im": d_b_im2,
               "c_re": d_c_re, "c_im": d_c_im, "d_skip": d_d_skip, "b_glu": d_b_glu,
               "attn_out_g": d_attn_out_g, "ssm_out_g": d_ssm_out_g, "norm2_g": d_norm2_g}
    early, tok_early = _exchange_start("gather_early_grads", _pack([small_g[n] for n in SMALL_EARLY]), False, started)

    qkv_cots = [grads_qkv[p][i] for i in range(3) for p in range(3)]
    dproj, small_g["q_norm_g"], small_g["k_norm_g"] = _rowwise(
        "qk_norm_bwd", qkv_bwd_fn, steps,
        [proj, proj, q_norm_g + tok_early[0:1, 0:1], k_norm_g, *qkv_cots, du_skip, du_ssm],
        [_row(tm, aw, 0), _row(tm, aw, 1), _vec(HEAD), _vec(HEAD)] + [_row(tm, aw)] * 9 + [_row(tm, sw)] * 2,
        [_sds((s_len, 3 * aw + sw), bf16), _sds((1, HEAD), f32), _sds((1, HEAD), f32)],
        [_row(tm, 3 * aw + sw), _vec(HEAD), _vec(HEAD)], [False, True, True])

    g_win = _mm_tn_sharded("in_dw", h, dproj, N_DEV)
    scatter["w_in"], tok_in = _exchange_start("scatter_w_in", g_win, True, tok_early)
    dh = _mm_nt("in_dx", dproj, win_g.transpose(1, 0, 2).reshape(d, 3 * aw + sw), after=tok_in)[0]
    norm1_g_t = norm1_g + tok_in[0:1, 0:1]

    def norm1_bwd_fn(dh_, x_, dx1_, gn, sc, sh):
        _, vjp = jax.vjp(_norm_mod, x_, gn, sc, sh)
        dx, dgn, dsc, dsh = vjp(dh_)
        return dx1_ + dx, dgn, dsc, dsh

    grad_x, d_norm1_g, d_sc1, d_sh1 = _rowwise(
        "norm1_bwd", norm1_bwd_fn, steps, [dh, xs, dx1, norm1_g_t, sc1, sh1], [_row(tm, d)] * 3 + [_vec(d)] * 3,
        [_sds((s_len, d), f32)] + [_sds((1, d), f32)] * 3, [_row(tm, d)] + [_vec(d)] * 3,
        [False, True, True, True])

    small_g["b_ada"] = jnp.concatenate([d_sh1, d_sc1, d_g1, d_sh2, d_sc2, d_g2], axis=1)
    small_g["norm1_g"] = d_norm1_g
    (r_late,) = _exchange("gather_late_grads", [_pack([small_g[n] for n in SMALL_LATE])], [False])

    res = {}
    dmod_all = r_late.reshape(N_DEV, -1)[:, :6 * d]
    g_wada = _ada_bwd(c_all, lax.dynamic_slice_in_dim(dmod_all, me * n_ada, n_ada, axis=1))
    res["w_ada"] = _adamw("adamw_w_ada", w_ada[0], m_w_ada[0], v_w_ada[0], g_wada, False)
    after = res["w_ada"][1]
    for name in ("w_ff2", "w_ff1", "w_out", "w_glu", "w_in"):
        stack = _exchange_wait("scattered_" + name, scatter[name], after)
        res[name] = _adamw("adamw_" + name, wts[name][0], mom[name][0], var[name][0], stack, True)
        after = res[name][1]
    r_early = _exchange_wait("gathered_early_grads", early, after)
    for label, names, stack in (("late", SMALL_LATE, r_late), ("early", SMALL_EARLY, r_early)):
        small_res = _adamw("adamw_small_" + label, *packed[names], stack, True, rows=4096)
        off = 0
        for n in names:
            size = wts[n].size
            res[n] = [t.reshape(-1)[off:off + size] for t in small_res]
            off += size

    out = [loss, grad_x[None]]
    for i in range(4):
        out += [res[n][i].reshape(wts[n].shape) for n in ORDER]
    return tuple(out)
```

```python
import math

import jax
import jax.numpy as jnp
from jax import lax
from jax.experimental import pallas as pl
from jax.experimental.pallas import tpu as pltpu

f32, bf16 = jnp.float32, jnp.bfloat16

N_DEV = 8
LANE = 128
HEAD = 128
SSM_GROUP = 16
DILATIONS = ((128, 1), (512, 4), (2048, 16))
BAND = 128
EPS = 1e-6
ADAM_LR, ADAM_B1, ADAM_B2, ADAM_EPS, ADAM_WD, ADAM_STEP = 0.001, 0.9, 0.999, 1e-08, 0.01, 10
NEG = -1e30
VMEM_LIMIT = 60 * 1024 * 1024
HI = lax.Precision.HIGHEST
MESH = pl.DeviceIdType.MESH


def _pcall(body, **kw):
    sem = kw.pop("sem", None)
    kw["compiler_params"] = pltpu.CompilerParams(dimension_semantics=sem, vmem_limit_bytes=VMEM_LIMIT)
    return pl.pallas_call(body, **kw)


def _tile(n, pref):
    t = min(n, pref)
    while n % t:
        t //= 2
    return t


def _sds(shape, dtype):
    return jax.ShapeDtypeStruct(shape, dtype)


def _rowwise(name, fn, steps, ins, in_specs, outs, out_specs, acc):
    n_in = len(ins)

    def body(*refs):
        res = fn(*[r[...] for r in refs[:n_in]])
        res = res if isinstance(res, (tuple, list)) else (res,)
        for r, o, a in zip(refs[n_in:], res, acc):
            if a:
                @pl.when(pl.program_id(0) == 0)
                def _():
                    r[...] = jnp.zeros_like(r)
                r[...] += o
            else:
                r[...] = o.astype(r.dtype)

    return _pcall(body, name=name, grid=(steps,), in_specs=in_specs, out_specs=out_specs, out_shape=outs,
                  sem=("arbitrary",))(*ins)


def _row(tm, c, blk=0):
    return pl.BlockSpec((tm, c), lambda i: (i, blk))


def _vec(c, blk=0):
    return pl.BlockSpec((1, c), lambda i: (0, blk))


def _rms(x, g):
    return x * lax.rsqrt(jnp.mean(x * x, axis=-1, keepdims=True) + EPS) * g


def _norm_mod(x, g, sc, sh):
    return _rms(x, g) * (1.0 + sc) + sh


def _head_rms(t, g):
    return jnp.concatenate([_rms(t[:, h * HEAD:(h + 1) * HEAD], g) for h in range(t.shape[1] // HEAD)], axis=1)


def _mix_fn(attn, yg, z, bglu, ga, gs):
    ssm = yg * jax.nn.sigmoid(z + bglu)
    return jnp.concatenate([_rms(attn, ga), _rms(ssm, gs)], axis=1)


def _ypre_fn(ymm, u, dskip):
    return jax.nn.gelu(ymm + dskip * u)


def _matmul(name, a, b, *, dims, grid, a_spec, b_spec, acc_shape, outs, out_specs, extra=(), extra_specs=(),
            epilogue=None, after=None):
    gk = grid[2]
    n_x = len(extra)
    placed = [] if after is None else [after]
    first_out = n_x + len(placed)
    ins = [a, b, *extra, *placed]
    in_specs = [a_spec, b_spec, *extra_specs] + [pl.BlockSpec(memory_space=pl.ANY)] * len(placed)

    def product(a_ref, b_ref):
        return lax.dot_general(a_ref[...].astype(bf16), b_ref[...].astype(bf16), (dims, ((), ())),
                               preferred_element_type=f32)

    def finish(res, x_refs, o_refs):
        res = epilogue(res, *[r[...] for r in x_refs]) if epilogue is not None else (res,)
        for r, o in zip(o_refs, res):
            r[...] = o.astype(r.dtype)

    def body_single(a_ref, b_ref, *rest):
        finish(product(a_ref, b_ref), rest[:n_x], rest[first_out:])

    def body_pair(a_ref, b_ref, *rest):
        acc = rest[-1]
        prod = product(a_ref, b_ref)

        @pl.when(pl.program_id(2) == 0)
        def _():
            acc[...] = prod

        @pl.when(pl.program_id(2) == 1)
        def _():
            finish(acc[...] + prod, rest[:n_x], rest[first_out:-1])

    def body(a_ref, b_ref, *rest):
        acc = rest[-1]
        k = pl.program_id(2)

        @pl.when(k == 0)
        def _():
            acc[...] = jnp.zeros_like(acc)

        acc[...] += product(a_ref, b_ref)

        @pl.when(k == gk - 1)
        def _():
            finish(acc[...], rest[:n_x], rest[first_out:-1])

    sem = ("parallel", "parallel", "arbitrary")
    if gk == 1:
        return _pcall(body_single, name=name, grid=grid, in_specs=in_specs, out_specs=out_specs, out_shape=outs,
                      sem=sem)(*ins)
    return _pcall(body_pair if gk == 2 else body, name=name, grid=grid, in_specs=in_specs, out_specs=out_specs,
                  out_shape=outs, scratch_shapes=[pltpu.VMEM(acc_shape, f32)], sem=sem)(*ins)


NN = ((1,), (0,))
NT = ((1,), (1,))
TN = ((0,), (0,))


def _mm_nn(name, a, b, out_dtype=f32, tm=1024, tn=1024, tk=2048, epilogue=None, extra=(), outs=None):
    m, kd = a.shape
    n = b.shape[1]
    tm, tn, tk = _tile(m, tm), _tile(n, tn), _tile(kd, tk)
    o_spec = pl.BlockSpec((tm, tn), lambda i, j, k: (i, j))
    outs = outs if outs is not None else [_sds((m, n), out_dtype)]
    return _matmul(name, a, b, dims=NN, grid=(m // tm, n // tn, kd // tk),
                   a_spec=pl.BlockSpec((tm, tk), lambda i, j, k: (i, k)),
                   b_spec=pl.BlockSpec((tk, tn), lambda i, j, k: (k, j)),
                   acc_shape=(tm, tn), outs=outs, out_specs=[o_spec] * len(outs),
                   extra=extra, extra_specs=[o_spec] * len(extra), epilogue=epilogue)


def _mm_nn_sharded(name, a, b3, out_dtype=f32, tm=1024, tk=2048, epilogue=None, outs=None):
    m, kd = a.shape
    nsh, _, n = b3.shape
    tm, tk = _tile(m, tm), _tile(kd, tk)
    o_spec = pl.BlockSpec((tm, n), lambda i, j, k: (i, j))
    outs = outs if outs is not None else [_sds((m, nsh * n), out_dtype)]
    return _matmul(name, a, b3, dims=NN, grid=(m // tm, nsh, kd // tk),
                   a_spec=pl.BlockSpec((tm, tk), lambda i, j, k: (i, k)),
                   b_spec=pl.BlockSpec((None, tk, n), lambda i, j, k: (j, k, 0)),
                   acc_shape=(tm, n), outs=outs, out_specs=[o_spec] * len(outs), epilogue=epilogue)


def _mm_nt(name, a, b, out_dtype=f32, tm=1024, tn=1024, tk=2048, epilogue=None, extra=(), outs=None, after=None):
    m, kd = a.shape
    n = b.shape[0]
    tm, tn, tk = _tile(m, tm), _tile(n, tn), _tile(kd, tk)
    o_spec = pl.BlockSpec((tm, tn), lambda i, j, k: (i, j))
    outs = outs if outs is not None else [_sds((m, n), out_dtype)]
    return _matmul(name, a, b, dims=NT, grid=(m // tm, n // tn, kd // tk),
                   a_spec=pl.BlockSpec((tm, tk), lambda i, j, k: (i, k)),
                   b_spec=pl.BlockSpec((tn, tk), lambda i, j, k: (j, k)),
                   acc_shape=(tm, tn), outs=outs, out_specs=[o_spec] * len(outs),
                   extra=extra, extra_specs=[o_spec] * len(extra), epilogue=epilogue, after=after)


def _mm_tn(name, a, b, out_dtype=bf16, tm=1024, tn=1024, tk=2048, after=None):
    t, m = a.shape
    n = b.shape[1]
    tm, tn, tk = _tile(m, tm), _tile(n, tn), _tile(t, tk)
    return _matmul(name, a, b, dims=TN, grid=(m // tm, n // tn, t // tk),
                   a_spec=pl.BlockSpec((tk, tm), lambda i, j, k: (k, i)),
                   b_spec=pl.BlockSpec((tk, tn), lambda i, j, k: (k, j)),
                   acc_shape=(tm, tn), outs=[_sds((m, n), out_dtype)],
                   out_specs=[pl.BlockSpec((tm, tn), lambda i, j, k: (i, j))], after=after)[0]


def _mm_tn_sharded(name, a, b, nsh, out_dtype=bf16, tm=1024, tk=2048):
    t, m = a.shape
    n = b.shape[1] // nsh
    tm, tk = _tile(m, tm), _tile(t, tk)
    return _matmul(name, a, b, dims=TN, grid=(m // tm, nsh, t // tk),
                   a_spec=pl.BlockSpec((tk, tm), lambda i, j, k: (k, i)),
                   b_spec=pl.BlockSpec((tk, n), lambda i, j, k: (k, j)),
                   acc_shape=(tm, n), outs=[_sds((nsh, m, n), out_dtype)],
                   out_specs=[pl.BlockSpec((None, tm, n), lambda i, j, k: (j, i, 0))])[0]


SCAN_CHAINS = 8


def _scan_segments(s_len):
    nch = SCAN_CHAINS
    while s_len % (8 * nch) or (s_len // (8 * nch)) & (s_len // (8 * nch) - 1):
        nch //= 2
    return 8 * nch


def _to_segments(t, nseg):
    s_len, c = t.shape
    return t.reshape(nseg, s_len // nseg, c).transpose(1, 0, 2).reshape(s_len, c)


def _from_segments(t, nseg):
    s_len, c = t.shape
    return t.reshape(s_len // nseg, nseg, c).transpose(1, 0, 2).reshape(s_len, c)


def _lane_block_weights(t3, n_state):
    n_groups = t3.shape[0]
    gpl = LANE // n_state
    per = LANE // (gpl * SSM_GROUP)
    n_lb = n_groups // gpl
    t5 = t3.reshape(n_lb // per, per, gpl, SSM_GROUP, n_state)
    w = jnp.einsum("aqgic,gh,qs->aqsgihc", t5, jnp.eye(gpl, dtype=t3.dtype), jnp.eye(per, dtype=t3.dtype))
    return w.reshape(n_lb, LANE, LANE).astype(bf16)


def _lane_block_diag(w, n_state):
    gpl = LANE // n_state
    per = LANE // (gpl * SSM_GROUP)
    n_lb = w.shape[0]
    w7 = w.reshape(n_lb // per, per, per, gpl, SSM_GROUP, gpl, n_state)
    t5 = jnp.einsum("aqsgihc,gh,qs->aqgic", w7, jnp.eye(gpl, dtype=w.dtype), jnp.eye(per, dtype=w.dtype))
    return t5.reshape(n_lb * gpl, SSM_GROUP, n_state)


def _scan(name, src, w_in, w_out, a2, *, reverse, adjoint_of=None):
    s_len, n_ch = src.shape
    gp = a2.shape[1]
    per = (gp // LANE) // (n_ch // LANE)
    nseg = _scan_segments(s_len)
    nch = nseg // 8
    seg = s_len // nseg
    n_sq = int(math.log2(seg))
    assert 2 ** n_sq == seg
    adj = adjoint_of is not None
    chunk = _tile(s_len, 1024)
    n_chunks = s_len // chunk

    def body(*refs):
        it = iter(refs)
        src_ref, wi_ref, wo_ref, a_ref = (next(it) for _ in range(4))
        if adj:
            ut_ref, dyt_ref, hr_ref, hi_ref, hin_ref = (next(it) for _ in range(5))
        res_ref = next(it)
        if adj:
            da_ref, dbr_ref, dbi_ref, dcr_ref, dci_ref = (next(it) for _ in range(5))
        else:
            or_ref, oi_ref, oin_ref = (next(it) for _ in range(3))
        if adj:
            or_ref, oi_ref = next(it), next(it)

        for i in range(n_chunks):
            x2 = jnp.dot(src_ref[i * chunk:(i + 1) * chunk, :], wi_ref[...], preferred_element_type=f32)
            or_ref[i * chunk:(i + 1) * chunk, :] = x2[:, :LANE]
            oi_ref[i * chunk:(i + 1) * chunk, :] = x2[:, LANE:]

        ar = a_ref[0:1, :]
        ai = -a_ref[1:2, :] if reverse else a_ref[1:2, :]
        arb, aib = jnp.broadcast_to(ar, (8, LANE)), jnp.broadcast_to(ai, (8, LANE))

        def rows(ch, k):
            return pl.ds(pl.multiple_of(k * nseg + ch * 8, 8), 8)

        def advance(h, ch, k):
            hr, hi = h
            return (arb * hr - aib * hi + or_ref[rows(ch, k), :], arb * hi + aib * hr + oi_ref[rows(ch, k), :])

        def kk(n):
            return seg - 1 - n if reverse else n

        zero = jnp.zeros((8, LANE), f32)

        def sweep1(n, hs):
            return tuple(advance(hs[ch], ch, kk(n)) for ch in range(nch))

        ends = lax.fori_loop(0, seg, sweep1, tuple((zero, zero) for _ in range(nch)))

        pr, pi = ar, ai
        for _ in range(n_sq):
            pr, pi = pr * pr - pi * pi, 2.0 * pr * pi
        in_r, in_i = [None] * nseg, [None] * nseg
        cr = ci = jnp.zeros((1, LANE), f32)
        for j in (range(nseg - 1, -1, -1) if reverse else range(nseg)):
            in_r[j], in_i[j] = cr, ci
            er, ei = ends[j // 8][0][j % 8:j % 8 + 1, :], ends[j // 8][1][j % 8:j % 8 + 1, :]
            cr, ci = er + pr * cr - pi * ci, ei + pr * ci + pi * cr
        h0 = tuple((jnp.concatenate(in_r[8 * ch:8 * ch + 8], axis=0), jnp.concatenate(in_i[8 * ch:8 * ch + 8], axis=0))
                   for ch in range(nch))
        if not adj:
            for ch in range(nch):
                oin_ref[0, 8 * ch:8 * ch + 8, :] = h0[ch][0]
                oin_ref[1, 8 * ch:8 * ch + 8, :] = h0[ch][1]

        def emit(ch, k, h):
            or_ref[rows(ch, k), :] = h[0]
            oi_ref[rows(ch, k), :] = h[1]

        def pair(h, p):
            return h[0] * p[0] + h[1] * p[1], h[1] * p[0] - h[0] * p[1]

        def sweep2(n, carry):
            k = kk(n)
            new = tuple(advance(carry[ch], ch, k) for ch in range(nch))
            for ch in range(nch):
                emit(ch, k, new[ch])
            if not adj:
                return new
            dr, di = carry[nch]
            for ch in range(nch):
                qr, qi = pair(new[ch], (hr_ref[rows(ch, k - 1), :], hi_ref[rows(ch, k - 1), :]))
                dr, di = dr + qr, di + qi
            return new + ((dr, di),)

        if adj:
            carry = lax.fori_loop(0, seg - 1, sweep2, h0 + ((zero, zero),))
            dr, di = carry[nch]
            for ch in range(nch):
                new = advance(carry[ch], ch, 0)
                emit(ch, 0, new)
                qr, qi = pair(new, (hin_ref[0, 8 * ch:8 * ch + 8, :], hin_ref[1, 8 * ch:8 * ch + 8, :]))
                dr, di = dr + qr, di + qi
            da_ref[0] = jnp.sum(dr, axis=0, keepdims=True)
            da_ref[1] = jnp.sum(di, axis=0, keepdims=True)
        else:
            lax.fori_loop(0, seg, sweep2, h0)

        first = pl.program_id(0) % per == 0
        for i in range(n_chunks):
            sl = slice(i * chunk, (i + 1) * chunk)
            h2 = jnp.concatenate([or_ref[sl, :], oi_ref[sl, :]], axis=1).astype(bf16)
            part = lax.dot_general(h2, wo_ref[...], (NT, ((), ())), preferred_element_type=f32)

            @pl.when(first)
            def _():
                res_ref[sl, :] = part

            @pl.when(jnp.logical_not(first))
            def _():
                res_ref[sl, :] += part

        if adj:
            def over_time(xt_ref, yr_ref, yi_ref):
                tot = jnp.zeros((LANE, 2 * LANE), f32)
                for i in range(n_chunks):
                    sl = slice(i * chunk, (i + 1) * chunk)
                    y2 = jnp.concatenate([yr_ref[sl, :], yi_ref[sl, :]], axis=1).astype(bf16)
                    tot += jnp.dot(xt_ref[:, sl], y2, preferred_element_type=f32)
                return tot[:, :LANE], tot[:, LANE:]

            dbr_ref[...], dbi_ref[...] = over_time(ut_ref, or_ref, oi_ref)
            dcr_ref[...], dci_ref[...] = over_time(dyt_ref, hr_ref, hi_ref)

    col = pl.BlockSpec((s_len, LANE), lambda l: (0, l))
    chan = pl.BlockSpec((s_len, LANE), lambda l: (0, l // per))
    in_spec = pl.BlockSpec((2, nseg, LANE), lambda l: (0, 0, l))
    w_spec = pl.BlockSpec((None, LANE, LANE), lambda l: (l, 0, 0))
    w2_spec = pl.BlockSpec((None, LANE, 2 * LANE), lambda l: (l, 0, 0))
    ins = [src, jnp.concatenate(w_in, axis=2), jnp.concatenate(w_out, axis=2), a2]
    in_specs = [chan, w2_spec, w2_spec, pl.BlockSpec((8, LANE), lambda l: (0, l))]
    outs, out_specs = [_sds((s_len, n_ch), f32)], [chan]
    scratch = []
    if adj:
        ins += list(adjoint_of)
        chan_t = pl.BlockSpec((LANE, s_len), lambda l: (l // per, 0))
        in_specs += [chan_t, chan_t, col, col, in_spec]
        outs += [_sds((2, 1, gp), f32)] + [_sds((gp // LANE, LANE, LANE), f32)] * 4
        out_specs += [pl.BlockSpec((2, 1, LANE), lambda l: (0, 0, l))] + [w_spec] * 4
        scratch = [pltpu.VMEM((s_len, LANE), f32)] * 2
    else:
        outs += [_sds((s_len, gp), f32)] * 2 + [_sds((2, nseg, gp), f32)]
        out_specs += [col, col, in_spec]
    res = _pcall(body, name=name, grid=(gp // LANE,), in_specs=in_specs, out_specs=out_specs, out_shape=outs,
                 scratch_shapes=scratch, sem=("arbitrary",))(*ins)
    if adj:
        return res[0], res[1], (res[2], res[3]), (res[4], res[5])
    return res


def _ssm_param_fn(lam_re, lam_im, log_step, b_re2, b_im2, expand):
    step = jnp.exp(log_step)
    xr, xi = lam_re * step, lam_im * step
    mag = jnp.exp(xr)
    ar, ai = mag * jnp.cos(xi), mag * jnp.sin(xi)
    nr, ni = ar - 1.0, ai
    den = lam_re * lam_re + lam_im * lam_im
    cr = (nr * lam_re + ni * lam_im) / den
    ci = (ni * lam_re - nr * lam_im) / den
    cre = jnp.dot(cr, expand, precision=HI, preferred_element_type=f32)
    cie = jnp.dot(ci, expand, precision=HI, preferred_element_type=f32)
    return ar, ai, cre * b_re2 - cie * b_im2, cre * b_im2 + cie * b_re2


def _ssm_params(lam_re, lam_im, log_step, b_re2, b_im2, expand):
    def body(*refs):
        res = _ssm_param_fn(*[r[...] for r in refs[:6]])
        for r, o in zip(refs[6:], res):
            r[...] = o

    g, p = lam_re.shape
    return _pcall(body, name="ssm_params", out_shape=[_sds((g, p), f32)] * 2 + [_sds(b_re2.shape, f32)] * 2)(
        lam_re, lam_im, log_step, b_re2, b_im2, expand)


def _ssm_params_bwd(lam_re, lam_im, log_step, b_re2, b_im2, expand, d_ar, d_ai, d_bbr, d_bbi):
    def body(*refs):
        prim = [r[...] for r in refs[:5]]
        ex = refs[5][...]
        cot = tuple(r[...] for r in refs[6:10])
        _, vjp = jax.vjp(lambda *p_: _ssm_param_fn(*p_, ex), *prim)
        for r, o in zip(refs[10:], vjp(cot)):
            r[...] = o

    shapes = [lam_re.shape, lam_im.shape, log_step.shape, b_re2.shape, b_im2.shape]
    return _pcall(body, name="ssm_params_bwd", out_shape=[_sds(s, f32) for s in shapes])(
        lam_re, lam_im, log_step, b_re2, b_im2, expand, d_ar, d_ai, d_bbr, d_bbi)


def _slope_table(n_heads):
    s = 2.0 ** (-8.0 * (jnp.arange(n_heads, dtype=f32) + 1.0) / n_heads)
    return jnp.broadcast_to(s[:, None, None], (n_heads, 1, LANE))


def _band_bias(slope_d, shift):
    qi = lax.broadcasted_iota(jnp.int32, (BAND, BAND), 0)
    ki = lax.broadcasted_iota(jnp.int32, (BAND, BAND), 1)
    mask = (ki >= qi) if shift else (ki <= qi)
    return jnp.where(mask, -slope_d * (qi - ki + shift).astype(f32), NEG)


def _window_bias(slope_d, has_prev):
    own = _band_bias(slope_d, 0)
    mid = jnp.concatenate([_band_bias(slope_d, BAND), own], axis=1)
    none = jnp.concatenate([jnp.full((BAND, BAND), NEG, f32), own], axis=1)
    return mid, jnp.where(has_prev, mid, none)


def _window_scores(q, k2, bias):
    return lax.dot_general(q, k2, (NT, ((), ())), preferred_element_type=f32) * (HEAD ** -0.5) + bias


def _attn_geometry(s_len, dil, rows=1024):
    piece = BAND * dil
    m = max(1, rows // piece)
    while s_len % (piece * m):
        m //= 2
    return m, piece


def _stream_rows(start, dil):
    return pl.ds(start, BAND, stride=dil) if dil > 1 else pl.ds(start, BAND)


def _attn_fwd(qn, kn, proj, v_blk, dil, slopes, acc=None):
    s_len, aw = qn.shape
    n_heads = aw // HEAD
    m, piece = _attn_geometry(s_len, dil)
    rows = m * piece

    def body(q_ref, k_ref, kp_ref, v_ref, vp_ref, sl_ref, *rest):
        o_ref, lse_ref = rest[-2:]
        bias_mid, bias_first = _window_bias(sl_ref[:, 0:1] * float(dil), pl.program_id(1) > 0)
        for b in range(m):
            for r in range(dil):
                idx = _stream_rows(b * piece + r, dil)
                q, kc, vc = (ref[idx, :].astype(bf16) for ref in (q_ref, k_ref, v_ref))
                if b:
                    pidx = _stream_rows((b - 1) * piece + r, dil)
                    kp, vp = k_ref[pidx, :].astype(bf16), v_ref[pidx, :].astype(bf16)
                else:
                    pidx = _stream_rows(r, dil)
                    kp, vp = kp_ref[pidx, :].astype(bf16), vp_ref[pidx, :].astype(bf16)
                k2, v2 = jnp.concatenate([kp, kc], axis=0), jnp.concatenate([vp, vc], axis=0)
                s = _window_scores(q, k2, bias_mid if b else bias_first)
                mx = jnp.max(s, axis=1, keepdims=True)
                if acc is not None:
                    l_run = rest[1][idx, :][:, 0:1]
                    mx = jnp.maximum(mx, l_run)
                p = jnp.exp(s - mx)
                den = jnp.sum(p, axis=1, keepdims=True)
                o = jnp.dot(p.astype(bf16), v2, preferred_element_type=f32)
                if acc is not None:
                    w_run = jnp.exp(l_run - mx)
                    den, o = den + w_run, o + w_run * rest[0][idx, :]
                o_ref[idx, :] = o / den
                lse_ref[idx, :] = jnp.broadcast_to(mx + jnp.log(den), (BAND, HEAD))

    def cur(blk0):
        return pl.BlockSpec((rows, HEAD), lambda h, t: (t, blk0 + h))

    def prev(blk0):
        return pl.BlockSpec((piece, HEAD), lambda h, t: (jnp.maximum(t * m - 1, 0), blk0 + h))

    sl = pl.BlockSpec((None, 1, LANE), lambda h, t: (h, 0, 0))
    in_specs = [cur(0), cur(0), prev(0), cur(v_blk), prev(v_blk), sl]
    ins = [qn, kn, kn, proj, proj, slopes]
    extra = {}
    if acc is not None:
        extra = dict(input_output_aliases={len(ins) + i: i for i in range(2)})
        in_specs += [cur(0)] * 2
        ins += list(acc)
    return _pcall(body, name=f"attn_fwd_d{dil}", grid=(n_heads, s_len // rows), in_specs=in_specs,
                  out_specs=[cur(0), cur(0)], out_shape=[_sds((s_len, aw), f32)] * 2,
                  sem=("parallel", "parallel"), **extra)(*ins)


def _attn_bwd(qn, kn, proj, v_blk, do, lse, dd, dil, slopes, acc=None):
    s_len, aw = qn.shape
    n_heads = aw // HEAD
    m, piece = _attn_geometry(s_len, dil, max(1024, 2 * BAND * dil))
    rows = m * piece
    n_tiles = s_len // rows
    scale = HEAD ** -0.5

    def body(q_ref, qx_ref, k_ref, kp_ref, v_ref, vp_ref, do_ref, dox_ref, l_ref, lx_ref, d_ref, dx_ref, sl_ref,
             *rest):
        dq_ref, dk_ref, dv_ref = rest[-3:]

        def put(ref, which, idx, val):
            ref[idx, :] = val if acc is None else val + rest[which][idx, :]

        t = pl.program_id(1)
        slope_d = sl_ref[:, 0:1] * float(dil)
        bias_mid, bias_first = _window_bias(slope_d, t > 0)
        bias_next = _band_bias(slope_d, BAND)

        def query_side(ref_q, ref_do, ref_l, ref_d, idx):
            return (ref_q[idx, :].astype(bf16), ref_do[idx, :].astype(bf16), ref_l[idx, :][:, 0:1],
                    ref_d[idx, :][:, 0:1])

        def probs(qs, keys, values, bias):
            q, do_, l_col, d_col = qs
            p = jnp.exp(_window_scores(q, keys, bias) - l_col)
            dp = lax.dot_general(do_, values, (NT, ((), ())), preferred_element_type=f32)
            return p.astype(bf16), (p * (dp - d_col)).astype(bf16)

        def tn(a_, b_):
            return lax.dot_general(a_, b_, (TN, ((), ())), preferred_element_type=f32)

        for r in range(dil):
            pend = None
            for b in range(m):
                idx = _stream_rows(b * piece + r, dil)
                qs = query_side(q_ref, do_ref, l_ref, d_ref, idx)
                kc, vc = k_ref[idx, :].astype(bf16), v_ref[idx, :].astype(bf16)
                if b:
                    kp, vp = kc_prev, vc_prev
                else:
                    pidx = _stream_rows(r, dil)
                    kp, vp = kp_ref[pidx, :].astype(bf16), vp_ref[pidx, :].astype(bf16)
                k2, v2 = jnp.concatenate([kp, kc], axis=0), jnp.concatenate([vp, vc], axis=0)
                p, ds = probs(qs, k2, v2, bias_mid if b else bias_first)
                put(dq_ref, 0, idx, jnp.dot(ds, k2, preferred_element_type=f32) * scale)
                dk2, dv2 = tn(ds, qs[0]), tn(p, qs[1])
                if pend is not None:
                    put(dk_ref, 1, pend[0], (pend[1] + dk2[:BAND]) * scale)
                    put(dv_ref, 2, pend[0], pend[2] + dv2[:BAND])
                pend = (idx, dk2[BAND:], dv2[BAND:])
                kc_prev, vc_prev = kc, vc
            qs = query_side(qx_ref, dox_ref, lx_ref, dx_ref, _stream_rows(r, dil))
            p, ds = probs(qs, kc_prev, vc_prev, bias_next)
            live = t < n_tiles - 1
            put(dk_ref, 1, pend[0], (pend[1] + jnp.where(live, tn(ds, qs[0]), 0.0)) * scale)
            put(dv_ref, 2, pend[0], pend[2] + jnp.where(live, tn(p, qs[1]), 0.0))

    def cur(blk0):
        return pl.BlockSpec((rows, HEAD), lambda h, t: (t, blk0 + h))

    def prev(blk0):
        return pl.BlockSpec((piece, HEAD), lambda h, t: (jnp.maximum(t * m - 1, 0), blk0 + h))

    def nxt(blk0):
        return pl.BlockSpec((piece, HEAD), lambda h, t: (jnp.minimum(t * m + m, n_tiles * m - 1), blk0 + h))

    sl = pl.BlockSpec((None, 1, LANE), lambda h, t: (h, 0, 0))
    in_specs = [cur(0), nxt(0), cur(0), prev(0), cur(v_blk), prev(v_blk), cur(0), nxt(0), cur(0), nxt(0), cur(0),
                nxt(0), sl]
    ins = [qn, qn, kn, kn, proj, proj, do, do, lse, lse, dd, dd, slopes]
    extra = {}
    if acc is not None:
        extra = dict(input_output_aliases={len(ins) + i: i for i in range(3)})
        in_specs += [cur(0)] * 3
        ins += list(acc)
    return _pcall(body, name=f"attn_bwd_d{dil}", grid=(n_heads, n_tiles), in_specs=in_specs,
                  out_specs=[cur(0)] * 3, out_shape=[_sds((s_len, aw), f32)] * 3,
                  sem=("parallel", "parallel"), **extra)(*ins)


def _exchange(name, srcs, scatter):
    n = len(srcs)

    def body(*refs):
        src, out = refs[:n], refs[n:2 * n]
        send_sems, recv_sems, local_sems = refs[2 * n:]
        x, y, c = lax.axis_index("x"), lax.axis_index("y"), lax.axis_index("c")
        me = 4 * x + 2 * y + c

        def peer(r):
            return ((1 - x) if r & 4 else x, (1 - y) if r & 2 else y, (1 - c) if r & 1 else c)

        def lin(p):
            return 4 * p[0] + 2 * p[1] + p[2]

        def piece(a, idx):
            return src[a].at[idx] if scatter[a] else src[a]

        local, sends = [], []
        for a in range(n):
            cp = pltpu.make_async_copy(piece(a, me), out[a].at[me], local_sems.at[a])
            cp.start()
            local.append(cp)
        for r in range(1, N_DEV):
            p = peer(r)
            for a in range(n):
                cp = pltpu.make_async_remote_copy(src_ref=piece(a, lin(p)), dst_ref=out[a].at[me],
                                                  send_sem=send_sems.at[a, r - 1], recv_sem=recv_sems.at[a, r - 1],
                                                  device_id=p, device_id_type=MESH)
                cp.start()
                sends.append(cp)
        for r in range(1, N_DEV):
            p = peer(r)
            for a in range(n):
                pltpu.make_async_remote_copy(src_ref=piece(a, lin(p)), dst_ref=out[a].at[lin(p)],
                                             send_sem=send_sems.at[a, r - 1], recv_sem=recv_sems.at[a, r - 1],
                                             device_id=p, device_id_type=MESH).wait_recv()
        for cp in sends:
            cp.wait_send()
        for cp in local:
            cp.wait()

    def piece_shape(a):
        return srcs[a].shape[1:] if scatter[a] else srcs[a].shape

    any_spec = pl.BlockSpec(memory_space=pl.ANY)
    return _pcall(body, name=name, in_specs=[any_spec] * n, out_specs=[any_spec] * n,
                  out_shape=[_sds((N_DEV, *piece_shape(a)), srcs[a].dtype) for a in range(n)],
                  scratch_shapes=[pltpu.SemaphoreType.DMA((n, N_DEV - 1)), pltpu.SemaphoreType.DMA((n, N_DEV - 1)),
                                  pltpu.SemaphoreType.DMA((n,))])(*srcs)


_HBM = pl.BlockSpec(memory_space=pltpu.HBM)
_SEM = pl.BlockSpec(memory_space=pltpu.SEMAPHORE)
_EFFECT = pltpu.SideEffectType.DATAFLOW_SIDE_EFFECTING


def _peer_ids():
    x, y, c = lax.axis_index("x"), lax.axis_index("y"), lax.axis_index("c")
    peers = [((1 - x) if r & 4 else x, (1 - y) if r & 2 else y, (1 - c) if r & 1 else c) for r in range(1, N_DEV)]
    return 4 * x + 2 * y + c, peers, [4 * p[0] + 2 * p[1] + p[2] for p in peers]


def _exchange_start(name, src, scatter, after):
    piece_shape = src.shape[1:] if scatter else src.shape

    def body(src_ref, land_ref, after_ref, send_sems, recv_sems, local_sem, src_thru, land_thru, token):
        me, peers, lins = _peer_ids()

        def piece(idx):
            return src_ref.at[idx] if scatter else src_ref

        pltpu.make_async_copy(piece(me), land_ref.at[me], local_sem).start()
        for r, (p, lp) in enumerate(zip(peers, lins)):
            pltpu.make_async_remote_copy(src_ref=piece(lp), dst_ref=land_ref.at[me], send_sem=send_sems.at[r],
                                         recv_sem=recv_sems.at[r], device_id=p, device_id_type=MESH).start()
        token[...] = jnp.zeros_like(token)

    land = pltpu.with_memory_space_constraint(lax.empty((N_DEV, *piece_shape), src.dtype), pltpu.HBM)
    send_sems, recv_sems, local_sem, src_thru, land_thru, token = pl.pallas_call(
        body, name=name,
        out_shape=(pltpu.SemaphoreType.DMA((N_DEV - 1,)), pltpu.SemaphoreType.DMA((N_DEV - 1,)),
                   pltpu.SemaphoreType.DMA(()), pltpu.HBM(src.shape, src.dtype),
                   pltpu.HBM((N_DEV, *piece_shape), src.dtype), _sds((8, LANE), f32)),
        in_specs=(_HBM, _HBM, pl.BlockSpec(memory_space=pl.ANY)),
        out_specs=(_SEM, _SEM, _SEM, _HBM, _HBM, pl.BlockSpec(memory_space=pltpu.VMEM)),
        input_output_aliases={0: 3, 1: 4},
        compiler_params=pltpu.CompilerParams(has_side_effects=_EFFECT),
    )(pltpu.with_memory_space_constraint(src, pltpu.HBM), land, after)
    return (send_sems, recv_sems, local_sem, src_thru, land_thru, scatter), token


def _exchange_wait(name, handle, *after):
    send_sems, recv_sems, local_sem, src_thru, land_thru, scatter = handle

    def body(src_ref, land_ref, send_sems_, recv_sems_, local_sem_, *rest):
        me, peers, lins = _peer_ids()

        def piece(idx):
            return src_ref.at[idx] if scatter else src_ref

        pltpu.make_async_copy(piece(me), land_ref.at[me], local_sem_).wait()
        for r, (p, lp) in enumerate(zip(peers, lins)):
            pltpu.make_async_remote_copy(src_ref=piece(lp), dst_ref=land_ref.at[me], send_sem=send_sems_.at[r],
                                         recv_sem=recv_sems_.at[r], device_id=p, device_id_type=MESH).wait_send()
            pltpu.make_async_remote_copy(src_ref=piece(lp), dst_ref=land_ref.at[lp], send_sem=send_sems_.at[r],
                                         recv_sem=recv_sems_.at[r], device_id=p, device_id_type=MESH).wait_recv()

    return pl.pallas_call(
        body, name=name,
        out_shape=(pltpu.HBM(src_thru.shape, src_thru.dtype), pltpu.HBM(land_thru.shape, land_thru.dtype)),
        in_specs=(_HBM, _HBM, _SEM, _SEM, _SEM, *[pl.BlockSpec(memory_space=pl.ANY)] * len(after)),
        out_specs=(_HBM, _HBM), input_output_aliases={0: 0, 1: 1},
        compiler_params=pltpu.CompilerParams(has_side_effects=_EFFECT),
    )(src_thru, land_thru, send_sems, recv_sems, local_sem, *after)[1]


def _adamw(name, w, m, v, g_or_stack, stacked, rows=256):
    r, c = w.shape
    tr = _tile(r, rows)

    def fn(w_, m_, v_, g_):
        if stacked:
            g = g_[0].astype(f32)
            for j in range(1, N_DEV):
                g = g + g_[j].astype(f32)
        else:
            g = g_
        m_new = ADAM_B1 * m_ + (1.0 - ADAM_B1) * g
        v_new = ADAM_B2 * v_ + (1.0 - ADAM_B2) * (g * g)
        m_hat = m_new / (1.0 - ADAM_B1 ** ADAM_STEP)
        v_hat = v_new / (1.0 - ADAM_B2 ** ADAM_STEP)
        delta = -ADAM_LR * (m_hat / (jnp.sqrt(v_hat) + ADAM_EPS) + ADAM_WD * w_)
        return g, delta, m_new, v_new

    blk = _row(tr, c)
    g_spec = pl.BlockSpec((N_DEV, tr, c), lambda i: (0, i, 0)) if stacked else blk
    return _rowwise(name, fn, r // tr, [w, m, v, g_or_stack], [blk, blk, blk, g_spec],
                    [_sds((r, c), f32)] * 4, [blk] * 4, [False] * 4)


def _ada_fwd(c_all, w_shard, b_shard):
    nb_, d = c_all.shape
    n = w_shard.shape[1]
    tn = _tile(n, 512)

    def body(c_ref, w_ref, b_ref, o_ref):
        a = jax.nn.silu(c_ref[...]).astype(bf16)
        o_ref[...] = jnp.dot(a, w_ref[...].astype(bf16), preferred_element_type=f32) + b_ref[...]

    return _pcall(body, name="ada_fwd", grid=(n // tn,),
                  in_specs=[pl.BlockSpec((nb_, d), lambda j: (0, 0)), pl.BlockSpec((d, tn), lambda j: (0, j)),
                            pl.BlockSpec((1, tn), lambda j: (0, j))],
                  out_specs=pl.BlockSpec((nb_, tn), lambda j: (0, j)), out_shape=_sds((nb_, n), f32),
                  sem=("parallel",))(c_all, w_shard, b_shard)


def _ada_bwd(c_all, dmod_cols):
    nb_, d = c_all.shape
    n = dmod_cols.shape[1]
    tn = _tile(n, 512)

    def body(c_ref, g_ref, o_ref):
        a = jax.nn.silu(c_ref[...]).astype(bf16).astype(f32)
        g = g_ref[...].astype(bf16).astype(f32)
        o_ref[...] = lax.dot_general(a, g, (TN, ((), ())), precision=HI, preferred_element_type=f32)

    return _pcall(body, name="ada_bwd", grid=(n // tn,),
                  in_specs=[pl.BlockSpec((nb_, d), lambda j: (0, 0)), pl.BlockSpec((nb_, tn), lambda j: (0, j))],
                  out_specs=pl.BlockSpec((d, tn), lambda j: (0, j)), out_shape=_sds((d, n), f32),
                  sem=("parallel",))(c_all, dmod_cols)


SMALL_LATE = ("b_ada", "norm1_g", "q_norm_g", "k_norm_g")
SMALL_EARLY = ("lam_re", "lam_im", "log_step", "b_re", "b_im", "c_re", "c_im", "d_skip", "b_glu", "attn_out_g",
               "ssm_out_g", "norm2_g")
ORDER = ("w_ada", "b_ada", "norm1_g", "w_in", "q_norm_g", "k_norm_g", "lam_re", "lam_im", "log_step", "b_re", "b_im",
         "c_re", "c_im", "d_skip", "w_glu", "b_glu", "attn_out_g", "ssm_out_g", "w_out", "norm2_g", "w_ff1", "w_ff2")


def _pack(parts):
    flat = jnp.concatenate([p.reshape(-1) for p in parts])
    pad = (-flat.shape[0]) % (8 * LANE)
    return jnp.pad(flat, (0, pad)).reshape(-1, LANE)


def kernel(x, c, w_ada, b_ada, norm1_g, w_in, q_norm_g, k_norm_g, lam_re, lam_im, log_step, b_re, b_im, c_re, c_im, d_skip, w_glu, b_glu, attn_out_g, ssm_out_g, w_out, norm2_g, w_ff1, w_ff2, loss_target, m_w_ada, m_b_ada, m_norm1_g, m_w_in, m_q_norm_g, m_k_norm_g, m_lam_re, m_lam_im, m_log_step, m_b_re, m_b_im, m_c_re, m_c_im, m_d_skip, m_w_glu, m_b_glu, m_attn_out_g, m_ssm_out_g, m_w_out, m_norm2_g, m_w_ff1, m_w_ff2, v_w_ada, v_b_ada, v_norm1_g, v_w_in, v_q_norm_g, v_k_norm_g, v_lam_re, v_lam_im, v_log_step, v_b_re, v_b_im, v_c_re, v_c_im, v_d_skip, v_w_glu, v_b_glu, v_attn_out_g, v_ssm_out_g, v_w_out, v_norm2_g, v_w_ff1, v_w_ff2):
    env = dict(locals())
    wts = {n: env[n] for n in ORDER}
    mom = {n: env["m_" + n] for n in ORDER}
    var = {n: env["v_" + n] for n in ORDER}

    xs, tgt = x[0], loss_target[0]
    s_len, d = xs.shape
    aw = d // 2
    sw = d - aw
    n_heads = aw // HEAD
    n_groups = sw // SSM_GROUP
    n_state = lam_re.shape[-1]
    gp = n_groups * n_state
    tm = _tile(s_len, 256)
    steps = s_len // tm
    me = 4 * lax.axis_index("x") + 2 * lax.axis_index("y") + lax.axis_index("c")

    (c_all,) = _exchange("gather_c", [c], [False])
    c_all = c_all.reshape(N_DEV, d)

    n_ada = w_ada.shape[-1]
    b_ada_cols = lax.dynamic_slice_in_dim(b_ada, me * n_ada, n_ada, axis=1)
    mod_part = _ada_fwd(c_all, w_ada[0], b_ada_cols)
    (mod_all,) = _exchange("gather_mod", [mod_part], [False])
    mod = lax.dynamic_index_in_dim(mod_all, me, axis=1, keepdims=False).reshape(1, 6 * d)
    sh1, sc1, g1, sh2, sc2, g2 = (mod[:, i * d:(i + 1) * d] for i in range(6))

    gather, started = {}, jnp.zeros((1, 1), f32)
    for name in ("w_in", "w_glu", "w_out", "w_ff1", "w_ff2"):
        gather[name], token = _exchange_start("gather_" + name, wts[name][0].astype(bf16), False, mod_all)
        started = started + token[0:1, 0:1]
    sc1 = sc1 + started

    (h,) = _rowwise("norm1", _norm_mod, steps, [xs, norm1_g, sc1, sh1],
                    [_row(tm, d), _vec(d), _vec(d), _vec(d)], [_sds((s_len, d), bf16)], [_row(tm, d)], [False])
    lam_re2, lam_im2 = lam_re[0], lam_im[0]
    log_step2 = log_step[0].reshape(n_groups, 1)
    b_re2 = b_re[0].reshape(n_groups, n_state * SSM_GROUP)
    b_im2 = b_im[0].reshape(n_groups, n_state * SSM_GROUP)
    expand = jnp.repeat(jnp.eye(n_state, dtype=f32), SSM_GROUP, axis=1)
    a_re, a_im, bb_re2, bb_im2 = _ssm_params(lam_re2, lam_im2, log_step2, b_re2, b_im2, expand)
    a2 = jnp.zeros((8, gp), f32).at[0].set(a_re.reshape(gp)).at[1].set(a_im.reshape(gp))
    w_bu = tuple(_lane_block_weights(t.reshape(n_groups, n_state, SSM_GROUP).transpose(0, 2, 1), n_state)
                 for t in (bb_re2, bb_im2))
    w_c = (_lane_block_weights(c_re[0], n_state), _lane_block_weights(-c_im[0], n_state))

    packed = {names: tuple(_pack([t[n] for n in names]) for t in (wts, mom, var))
              for names in (SMALL_LATE, SMALL_EARLY)}

    win_g = _exchange_wait("gathered_w_in", gather["w_in"], h, a2, *w_bu, *w_c, *packed[SMALL_LATE],
                           *packed[SMALL_EARLY])
    (proj,) = _mm_nn_sharded("in_proj", h, win_g)

    def qk_fn(q, k, gq, gk):
        return _head_rms(q, gq), _head_rms(k, gk)

    qn, kn = _rowwise("qk_norm", qk_fn, steps, [proj, proj, q_norm_g, k_norm_g],
                      [_row(tm, aw, 0), _row(tm, aw, 1), _vec(HEAD), _vec(HEAD)],
                      [_sds((s_len, aw), f32)] * 2, [_row(tm, aw)] * 2, [False] * 2)
    v_blk = 2 * aw // HEAD

    slopes = _slope_table(n_heads)
    mixture = None
    for _, dil in reversed(DILATIONS):
        mixture = _attn_fwd(qn, kn, proj, v_blk, dil, slopes, acc=mixture)
    attn, lse = mixture

    nseg = _scan_segments(s_len)
    u_seg = _to_segments(proj[:, 3 * aw:], nseg).astype(bf16)
    y_seg, h_re, h_im, hin_f = _scan("ssm_scan", u_seg, w_bu, w_c, a2, reverse=False)
    ymm = _from_segments(y_seg, nseg)

    u_spec = _row(tm, sw, 3 * aw // sw)
    (yg,) = _rowwise("ssm_gelu", _ypre_fn, steps, [ymm, proj, d_skip], [_row(tm, sw), u_spec, _vec(sw)],
                     [_sds((s_len, sw), f32)], [_row(tm, sw)], [False])
    wglu_g = _exchange_wait("gathered_w_glu", gather["w_glu"], yg).reshape(sw, sw)
    (z,) = _mm_nn("glu_proj", yg, wglu_g)
    (cat,) = _rowwise("mix_norm", _mix_fn, steps, [attn, yg, z, b_glu, attn_out_g, ssm_out_g],
                      [_row(tm, aw), _row(tm, sw), _row(tm, sw), _vec(sw), _vec(aw), _vec(sw)],
                      [_sds((s_len, d), bf16)], [_row(tm, d)], [False])
    wout_g = _exchange_wait("gathered_w_out", gather["w_out"], cat).reshape(d, d)
    (mixed,) = _mm_nn("out_proj", cat, wout_g)

    def res_norm2_fn(x_, mixed_, g1_, gn, sc, sh):
        x1_ = x_ + g1_ * mixed_
        return x1_, _norm_mod(x1_, gn, sc, sh)

    x1, h2 = _rowwise("norm2", res_norm2_fn, steps, [xs, mixed, g1, norm2_g, sc2, sh2],
                      [_row(tm, d), _row(tm, d)] + [_vec(d)] * 4,
                      [_sds((s_len, d), f32), _sds((s_len, d), bf16)], [_row(tm, d)] * 2, [False] * 2)

    def act_epilogue(acc):
        r = jnp.maximum(acc, 0.0)
        return r, r * r

    wff1_g = _exchange_wait("gathered_w_ff1", gather["w_ff1"], h2)
    r_ff, act = _mm_nn_sharded("ff1", h2, wff1_g, epilogue=act_epilogue,
                               outs=[_sds((s_len, 4 * d), bf16), _sds((s_len, 4 * d), bf16)])
    wff2_g = _exchange_wait("gathered_w_ff2", gather["w_ff2"], act).reshape(4 * d, d)
    (ff,) = _mm_nn("ff2", act, wff2_g)

    def loss_fn(x1_, ff_, tgt_, g2_):
        e = x1_ + g2_ * ff_ - tgt_
        dy_ = e * (1.0 / d)
        part = jnp.full((1, LANE), 0.5 / d, f32) * jnp.sum(e * e)
        return dy_, g2_ * dy_, part, jnp.sum(dy_ * ff_, axis=0, keepdims=True)

    dy, dff, loss_part, d_g2 = _rowwise(
        "loss", loss_fn, steps, [x1, ff, tgt, g2], [_row(tm, d)] * 3 + [_vec(d)],
        [_sds((s_len, d), f32), _sds((s_len, d), bf16), _sds((1, LANE), f32), _sds((1, d), f32)],
        [_row(tm, d), _row(tm, d), _vec(LANE), _vec(d)], [False, False, True, True])
    loss = lax.psum(loss_part[0, 0], ("x", "y", "c"))

    def dact_epilogue(acc, r_):
        return (acc * (2.0 * r_.astype(f32)),)

    (da,) = _mm_nt("ff2_dx", dff, wff2_g, epilogue=dact_epilogue, extra=[r_ff], outs=[_sds((s_len, 4 * d), bf16)])
    scatter = {}
    g_wff2 = _mm_tn("ff2_dw", act, dff, after=loss.reshape(1, 1)).reshape(N_DEV, 4 * d // N_DEV, d)
    scatter["w_ff2"], tok_ff2 = _exchange_start("scatter_w_ff2", g_wff2, True, loss.reshape(1, 1))
    dh2 = _mm_nt("ff1_dx", da, wff1_g.transpose(1, 0, 2).reshape(d, 4 * d))[0]
    g_wff1 = _mm_tn_sharded("ff1_dw", h2, da, N_DEV)
    scatter["w_ff1"], tok_ff1 = _exchange_start("scatter_w_ff1", g_wff1, True, started)
    norm2_g_t = norm2_g + (tok_ff2[0:1, 0:1] + tok_ff1[0:1, 0:1])

    def norm2_bwd_fn(dh2_, x1_, dy_, mixed_, gn, sc, sh, g1_):
        _, vjp = jax.vjp(_norm_mod, x1_, gn, sc, sh)
        dx, dgn, dsc, dsh = vjp(dh2_)
        dx1_ = dy_ + dx
        return dx1_, g1_ * dx1_, dgn, dsc, dsh, jnp.sum(dx1_ * mixed_, axis=0, keepdims=True)

    dx1, dmixed, d_norm2_g, d_sc2, d_sh2, d_g1 = _rowwise(
        "norm2_bwd", norm2_bwd_fn, steps, [dh2, x1, dy, mixed, norm2_g_t, sc2, sh2, g1],
        [_row(tm, d)] * 4 + [_vec(d)] * 4,
        [_sds((s_len, d), f32), _sds((s_len, d), bf16)] + [_sds((1, d), f32)] * 4,
        [_row(tm, d)] * 2 + [_vec(d)] * 4, [False, False, True, True, True, True])

    (dcat,) = _mm_nt("out_dx", dmixed, wout_g)
    g_wout = _mm_tn("out_dw", cat, dmixed).reshape(N_DEV, d // N_DEV, d)
    scatter["w_out"], tok_out = _exchange_start("scatter_w_out", g_wout, True, started)
    b_glu_t = b_glu + tok_out[0:1, 0:1]

    def mix_bwd_fn(dcat_, attn_, yg_, z_, bglu, ga, gs):
        _, vjp = jax.vjp(_mix_fn, attn_, yg_, z_, bglu, ga, gs)
        dattn_, dyg_, dz_, dbglu, dga, dgs = vjp(dcat_)
        prod = dattn_ * attn_
        dd_ = jnp.concatenate([jnp.broadcast_to(jnp.sum(prod[:, i * HEAD:(i + 1) * HEAD], axis=1, keepdims=True),
                                                (prod.shape[0], HEAD)) for i in range(n_heads)], axis=1)
        return dattn_, dd_, dyg_, dz_, dbglu, dga, dgs

    dattn, dd, dyg1, dz, d_b_glu, d_attn_out_g, d_ssm_out_g = _rowwise(
        "mix_bwd", mix_bwd_fn, steps, [dcat, attn, yg, z, b_glu_t, attn_out_g, ssm_out_g],
        [_row(tm, d), _row(tm, aw), _row(tm, sw), _row(tm, sw), _vec(sw), _vec(aw), _vec(sw)],
        [_sds((s_len, aw), f32), _sds((s_len, aw), f32), _sds((s_len, sw), f32), _sds((s_len, sw), bf16),
         _sds((1, sw), f32), _sds((1, aw), f32), _sds((1, sw), f32)],
        [_row(tm, aw), _row(tm, aw), _row(tm, sw), _row(tm, sw), _vec(sw), _vec(aw), _vec(sw)],
        [False] * 4 + [True] * 3)

    (dyg2,) = _mm_nt("glu_dx", dz, wglu_g)
    g_wglu = _mm_tn("glu_dw", yg, dz).reshape(N_DEV, sw // N_DEV, sw)
    scatter["w_glu"], tok_glu = _exchange_start("scatter_w_glu", g_wglu, True, started)
    d_skip_t = d_skip + tok_glu[0:1, 0:1]

    def gelu_bwd_fn(dyg1_, dyg2_, ymm_, u_, dskip):
        _, vjp = jax.vjp(_ypre_fn, ymm_, u_, dskip)
        dymm, du_, ddskip = vjp(dyg1_ + dyg2_)
        return dymm, du_, ddskip

    dymm, du_skip, d_d_skip = _rowwise(
        "ssm_gelu_bwd", gelu_bwd_fn, steps, [dyg1, dyg2, ymm, proj, d_skip_t],
        [_row(tm, sw)] * 3 + [u_spec, _vec(sw)],
        [_sds((s_len, sw), f32), _sds((s_len, sw), f32), _sds((1, sw), f32)],
        [_row(tm, sw), _row(tm, sw), _vec(sw)], [False, False, True])

    dymm_seg = _to_segments(dymm, nseg).astype(bf16)
    du_seg, da_seg, dbb_c, dc_c = _scan("ssm_adj", dymm_seg, w_c, w_bu, a2, reverse=True,
                                        adjoint_of=(u_seg.T, dymm_seg.T, h_re, h_im, hin_f))
    du_ssm = _from_segments(du_seg, nseg)

    def to_gpi(w):
        return _lane_block_diag(w, n_state).transpose(0, 2, 1).reshape(n_groups, n_state * SSM_GROUP)

    d_lam_re, d_lam_im, d_log_step, d_b_re2, d_b_im2 = _ssm_params_bwd(
        lam_re2, lam_im2, log_step2, b_re2, b_im2, expand,
        da_seg[0, 0].reshape(n_groups, n_state), da_seg[1, 0].reshape(n_groups, n_state),
        to_gpi(dbb_c[0]), to_gpi(dbb_c[1]))
    d_c_re = _lane_block_diag(dc_c[0], n_state)
    d_c_im = -_lane_block_diag(dc_c[1], n_state)

    grads_qkv = None
    for _, dil in reversed(DILATIONS):
        grads_qkv = _attn_bwd(qn, kn, proj, v_blk, dattn, lse, dd, dil, slopes, acc=grads_qkv)

    def qkv_bwd_fn(q, k, gq, gk, dqn, dkn, dv, du1, du2):
        _, vjp = jax.vjp(lambda q_, k_, gq_, gk_: (_head_rms(q_, gq_), _head_rms(k_, gk_)), q, k, gq, gk)
        dq, dk, dgq, dgk = vjp((dqn, dkn))
        return jnp.concatenate([dq, dk, dv, du1 + du2], axis=1), dgq, dgk

    small_g = {"lam_re": d_lam_re, "lam_im": d_lam_im, "log_step": d_log_step, "b_re": d_b_re2, "b_im": d_b_im2,
               "c_re": d_c_re, "c_im": d_c_im, "d_skip": d_d_skip, "b_glu": d_b_glu,
               "attn_out_g": d_attn_out_g, "ssm_out_g": d_ssm_out_g, "norm2_g": d_norm2_g}
    early, tok_early = _exchange_start("gather_early_grads", _pack([small_g[n] for n in SMALL_EARLY]), False, started)

    dproj, small_g["q_norm_g"], small_g["k_norm_g"] = _rowwise(
        "qk_norm_bwd", qkv_bwd_fn, steps,
        [proj, proj, q_norm_g + tok_early[0:1, 0:1], k_norm_g, *grads_qkv, du_skip, du_ssm],
        [_row(tm, aw, 0), _row(tm, aw, 1), _vec(HEAD), _vec(HEAD)] + [_row(tm, aw)] * 3 + [_row(tm, sw)] * 2,
        [_sds((s_len, 3 * aw + sw), bf16), _sds((1, HEAD), f32), _sds((1, HEAD), f32)],
        [_row(tm, 3 * aw + sw), _vec(HEAD), _vec(HEAD)], [False, True, True])

    g_win = _mm_tn_sharded("in_dw", h, dproj, N_DEV)
    scatter["w_in"], tok_in = _exchange_start("scatter_w_in", g_win, True, tok_early)
    dh = _mm_nt("in_dx", dproj, win_g.transpose(1, 0, 2).reshape(d, 3 * aw + sw), after=tok_in)[0]
    norm1_g_t = norm1_g + tok_in[0:1, 0:1]

    def norm1_bwd_fn(dh_, x_, dx1_, gn, sc, sh):
        _, vjp = jax.vjp(_norm_mod, x_, gn, sc, sh)
        dx, dgn, dsc, dsh = vjp(dh_)
        return dx1_ + dx, dgn, dsc, dsh

    grad_x, d_norm1_g, d_sc1, d_sh1 = _rowwise(
        "norm1_bwd", norm1_bwd_fn, steps, [dh, xs, dx1, norm1_g_t, sc1, sh1], [_row(tm, d)] * 3 + [_vec(d)] * 3,
        [_sds((s_len, d), f32)] + [_sds((1, d), f32)] * 3, [_row(tm, d)] + [_vec(d)] * 3,
        [False, True, True, True])

    small_g["b_ada"] = jnp.concatenate([d_sh1, d_sc1, d_g1, d_sh2, d_sc2, d_g2], axis=1)
    small_g["norm1_g"] = d_norm1_g
    (r_late,) = _exchange("gather_late_grads", [_pack([small_g[n] for n in SMALL_LATE])], [False])

    res = {}
    dmod_all = r_late.reshape(N_DEV, -1)[:, :6 * d]
    g_wada = _ada_bwd(c_all, lax.dynamic_slice_in_dim(dmod_all, me * n_ada, n_ada, axis=1))
    res["w_ada"] = _adamw("adamw_w_ada", w_ada[0], m_w_ada[0], v_w_ada[0], g_wada, False)
    after = res["w_ada"][1]
    for name in ("w_ff2", "w_ff1", "w_out", "w_glu", "w_in"):
        stack = _exchange_wait("scattered_" + name, scatter[name], after)
        res[name] = _adamw("adamw_" + name, wts[name][0], mom[name][0], var[name][0], stack, True)
        after = res[name][1]
    r_early = _exchange_wait("gathered_early_grads", early, after)
    for label, names, stack in (("late", SMALL_LATE, r_late), ("early", SMALL_EARLY, r_early)):
        small_res = _adamw("adamw_small_" + label, *packed[names], stack, True, rows=4096)
        off = 0
        for n in names:
            size = wts[n].size
            res[n] = [t.reshape(-1)[off:off + size] for t in small_res]
            off += size

    out = [loss, grad_x[None]]
    for i in range(4):
        out += [res[n][i].reshape(wts[n].shape) for n in ORDER]
    return tuple(out)
```

```python
import math

import jax
import jax.numpy as jnp
from jax import lax
from jax.experimental import pallas as pl
from jax.experimental.pallas import tpu as pltpu

f32, bf16 = jnp.float32, jnp.bfloat16

N_DEV = 8
LANE = 128
HEAD = 128
SSM_GROUP = 16
DILATIONS = ((128, 1), (512, 4), (2048, 16))
BAND = 128
EPS = 1e-6
ADAM_LR, ADAM_B1, ADAM_B2, ADAM_EPS, ADAM_WD, ADAM_STEP = 0.001, 0.9, 0.999, 1e-08, 0.01, 10
NEG = -1e30
VMEM_LIMIT = 60 * 1024 * 1024
HI = lax.Precision.HIGHEST
MESH = pl.DeviceIdType.MESH


def _pcall(body, **kw):
    sem = kw.pop("sem", None)
    kw["compiler_params"] = pltpu.CompilerParams(dimension_semantics=sem, vmem_limit_bytes=VMEM_LIMIT)
    return pl.pallas_call(body, **kw)


def _tile(n, pref):
    t = min(n, pref)
    while n % t:
        t //= 2
    return t


def _sds(shape, dtype):
    return jax.ShapeDtypeStruct(shape, dtype)


def _rowwise(name, fn, steps, ins, in_specs, outs, out_specs, acc):
    n_in = len(ins)

    def body(*refs):
        res = fn(*[r[...] for r in refs[:n_in]])
        res = res if isinstance(res, (tuple, list)) else (res,)
        for r, o, a in zip(refs[n_in:], res, acc):
            if a:
                @pl.when(pl.program_id(0) == 0)
                def _():
                    r[...] = jnp.zeros_like(r)
                r[...] += o
            else:
                r[...] = o.astype(r.dtype)

    return _pcall(body, name=name, grid=(steps,), in_specs=in_specs, out_specs=out_specs, out_shape=outs,
                  sem=("arbitrary",))(*ins)


def _row(tm, c, blk=0):
    return pl.BlockSpec((tm, c), lambda i: (i, blk))


def _vec(c, blk=0):
    return pl.BlockSpec((1, c), lambda i: (0, blk))


def _rms(x, g):
    return x * lax.rsqrt(jnp.mean(x * x, axis=-1, keepdims=True) + EPS) * g


def _norm_mod(x, g, sc, sh):
    return _rms(x, g) * (1.0 + sc) + sh


def _head_rms(t, g):
    return jnp.concatenate([_rms(t[:, h * HEAD:(h + 1) * HEAD], g) for h in range(t.shape[1] // HEAD)], axis=1)


def _mix_fn(attn, yg, z, bglu, ga, gs):
    ssm = yg * jax.nn.sigmoid(z + bglu)
    return jnp.concatenate([_rms(attn, ga), _rms(ssm, gs)], axis=1)


def _ypre_fn(ymm, u, dskip):
    return jax.nn.gelu(ymm + dskip * u)


def _matmul(name, a, b, *, dims, grid, a_spec, b_spec, acc_shape, outs, out_specs, extra=(), extra_specs=(),
            epilogue=None, after=None):
    gk = grid[2]
    n_x = len(extra)
    placed = [] if after is None else [after]
    first_out = n_x + len(placed)
    ins = [a, b, *extra, *placed]
    in_specs = [a_spec, b_spec, *extra_specs] + [pl.BlockSpec(memory_space=pl.ANY)] * len(placed)

    def product(a_ref, b_ref):
        return lax.dot_general(a_ref[...].astype(bf16), b_ref[...].astype(bf16), (dims, ((), ())),
                               preferred_element_type=f32)

    def finish(res, x_refs, o_refs):
        res = epilogue(res, *[r[...] for r in x_refs]) if epilogue is not None else (res,)
        for r, o in zip(o_refs, res):
            r[...] = o.astype(r.dtype)

    def body_single(a_ref, b_ref, *rest):
        finish(product(a_ref, b_ref), rest[:n_x], rest[first_out:])

    def body_pair(a_ref, b_ref, *rest):
        acc = rest[-1]
        prod = product(a_ref, b_ref)

        @pl.when(pl.program_id(2) == 0)
        def _():
            acc[...] = prod

        @pl.when(pl.program_id(2) == 1)
        def _():
            finish(acc[...] + prod, rest[:n_x], rest[first_out:-1])

    def body(a_ref, b_ref, *rest):
        acc = rest[-1]
        k = pl.program_id(2)

        @pl.when(k == 0)
        def _():
            acc[...] = jnp.zeros_like(acc)

        acc[...] += product(a_ref, b_ref)

        @pl.when(k == gk - 1)
        def _():
            finish(acc[...], rest[:n_x], rest[first_out:-1])

    sem = ("parallel", "parallel", "arbitrary")
    if gk == 1:
        return _pcall(body_single, name=name, grid=grid, in_specs=in_specs, out_specs=out_specs, out_shape=outs,
                      sem=sem)(*ins)
    return _pcall(body_pair if gk == 2 else body, name=name, grid=grid, in_specs=in_specs, out_specs=out_specs,
                  out_shape=outs, scratch_shapes=[pltpu.VMEM(acc_shape, f32)], sem=sem)(*ins)


NN = ((1,), (0,))
NT = ((1,), (1,))
TN = ((0,), (0,))


def _mm_nn(name, a, b, out_dtype=f32, tm=1024, tn=1024, tk=2048, epilogue=None, extra=(), outs=None):
    m, kd = a.shape
    n = b.shape[1]
    tm, tn, tk = _tile(m, tm), _tile(n, tn), _tile(kd, tk)
    o_spec = pl.BlockSpec((tm, tn), lambda i, j, k: (i, j))
    outs = outs if outs is not None else [_sds((m, n), out_dtype)]
    return _matmul(name, a, b, dims=NN, grid=(m // tm, n // tn, kd // tk),
                   a_spec=pl.BlockSpec((tm, tk), lambda i, j, k: (i, k)),
                   b_spec=pl.BlockSpec((tk, tn), lambda i, j, k: (k, j)),
                   acc_shape=(tm, tn), outs=outs, out_specs=[o_spec] * len(outs),
                   extra=extra, extra_specs=[o_spec] * len(extra), epilogue=epilogue)


def _mm_nn_sharded(name, a, b3, out_dtype=f32, tm=1024, tk=2048, epilogue=None, outs=None):
    m, kd = a.shape
    nsh, _, n = b3.shape
    tm, tk = _tile(m, tm), _tile(kd, tk)
    o_spec = pl.BlockSpec((tm, n), lambda i, j, k: (i, j))
    outs = outs if outs is not None else [_sds((m, nsh * n), out_dtype)]
    return _matmul(name, a, b3, dims=NN, grid=(m // tm, nsh, kd // tk),
                   a_spec=pl.BlockSpec((tm, tk), lambda i, j, k: (i, k)),
                   b_spec=pl.BlockSpec((None, tk, n), lambda i, j, k: (j, k, 0)),
                   acc_shape=(tm, n), outs=outs, out_specs=[o_spec] * len(outs), epilogue=epilogue)


def _mm_nt(name, a, b, out_dtype=f32, tm=1024, tn=1024, tk=2048, epilogue=None, extra=(), outs=None, after=None):
    m, kd = a.shape
    n = b.shape[0]
    tm, tn, tk = _tile(m, tm), _tile(n, tn), _tile(kd, tk)
    o_spec = pl.BlockSpec((tm, tn), lambda i, j, k: (i, j))
    outs = outs if outs is not None else [_sds((m, n), out_dtype)]
    return _matmul(name, a, b, dims=NT, grid=(m // tm, n // tn, kd // tk),
                   a_spec=pl.BlockSpec((tm, tk), lambda i, j, k: (i, k)),
                   b_spec=pl.BlockSpec((tn, tk), lambda i, j, k: (j, k)),
                   acc_shape=(tm, tn), outs=outs, out_specs=[o_spec] * len(outs),
                   extra=extra, extra_specs=[o_spec] * len(extra), epilogue=epilogue, after=after)


def _mm_tn(name, a, b, out_dtype=bf16, tm=1024, tn=1024, tk=2048, after=None):
    t, m = a.shape
    n = b.shape[1]
    tm, tn, tk = _tile(m, tm), _tile(n, tn), _tile(t, tk)
    return _matmul(name, a, b, dims=TN, grid=(m // tm, n // tn, t // tk),
                   a_spec=pl.BlockSpec((tk, tm), lambda i, j, k: (k, i)),
                   b_spec=pl.BlockSpec((tk, tn), lambda i, j, k: (k, j)),
                   acc_shape=(tm, tn), outs=[_sds((m, n), out_dtype)],
                   out_specs=[pl.BlockSpec((tm, tn), lambda i, j, k: (i, j))], after=after)[0]


def _mm_tn_sharded(name, a, b, nsh, out_dtype=bf16, tm=1024, tk=2048):
    t, m = a.shape
    n = b.shape[1] // nsh
    tm, tk = _tile(m, tm), _tile(t, tk)
    return _matmul(name, a, b, dims=TN, grid=(m // tm, nsh, t // tk),
                   a_spec=pl.BlockSpec((tk, tm), lambda i, j, k: (k, i)),
                   b_spec=pl.BlockSpec((tk, n), lambda i, j, k: (k, j)),
                   acc_shape=(tm, n), outs=[_sds((nsh, m, n), out_dtype)],
                   out_specs=[pl.BlockSpec((None, tm, n), lambda i, j, k: (j, i, 0))])[0]


SCAN_CHAINS = 8


def _scan_segments(s_len):
    nch = SCAN_CHAINS
    while s_len % (8 * nch) or (s_len // (8 * nch)) & (s_len // (8 * nch) - 1):
        nch //= 2
    return 8 * nch


def _to_segments(t, nseg):
    s_len, c = t.shape
    return t.reshape(nseg, s_len // nseg, c).transpose(1, 0, 2).reshape(s_len, c)


def _from_segments(t, nseg):
    s_len, c = t.shape
    return t.reshape(s_len // nseg, nseg, c).transpose(1, 0, 2).reshape(s_len, c)


def _lane_block_weights(t3, n_state):
    n_groups = t3.shape[0]
    gpl = LANE // n_state
    per = LANE // (gpl * SSM_GROUP)
    n_lb = n_groups // gpl
    t5 = t3.reshape(n_lb // per, per, gpl, SSM_GROUP, n_state)
    w = jnp.einsum("aqgic,gh,qs->aqsgihc", t5, jnp.eye(gpl, dtype=t3.dtype), jnp.eye(per, dtype=t3.dtype))
    return w.reshape(n_lb, LANE, LANE).astype(bf16)


def _lane_block_diag(w, n_state):
    gpl = LANE // n_state
    per = LANE // (gpl * SSM_GROUP)
    n_lb = w.shape[0]
    w7 = w.reshape(n_lb // per, per, per, gpl, SSM_GROUP, gpl, n_state)
    t5 = jnp.einsum("aqsgihc,gh,qs->aqgic", w7, jnp.eye(gpl, dtype=w.dtype), jnp.eye(per, dtype=w.dtype))
    return t5.reshape(n_lb * gpl, SSM_GROUP, n_state)


def _scan(name, src, w_in, w_out, a2, *, reverse, adjoint_of=None):
    s_len, n_ch = src.shape
    gp = a2.shape[1]
    per = (gp // LANE) // (n_ch // LANE)
    nseg = _scan_segments(s_len)
    nch = nseg // 8
    seg = s_len // nseg
    n_sq = int(math.log2(seg))
    assert 2 ** n_sq == seg
    adj = adjoint_of is not None
    chunk = _tile(s_len, 1024)
    n_chunks = s_len // chunk

    def body(*refs):
        it = iter(refs)
        src_ref, wi_ref, wo_ref, a_ref = (next(it) for _ in range(4))
        if adj:
            ut_ref, dyt_ref, hr_ref, hi_ref, hin_ref = (next(it) for _ in range(5))
        res_ref = next(it)
        if adj:
            da_ref, dbr_ref, dbi_ref, dcr_ref, dci_ref = (next(it) for _ in range(5))
        else:
            or_ref, oi_ref, oin_ref = (next(it) for _ in range(3))
        if adj:
            or_ref, oi_ref = next(it), next(it)

        for i in range(n_chunks):
            x2 = jnp.dot(src_ref[i * chunk:(i + 1) * chunk, :], wi_ref[...], preferred_element_type=f32)
            or_ref[i * chunk:(i + 1) * chunk, :] = x2[:, :LANE]
            oi_ref[i * chunk:(i + 1) * chunk, :] = x2[:, LANE:]

        ar = a_ref[0:1, :]
        ai = -a_ref[1:2, :] if reverse else a_ref[1:2, :]
        arb, aib = jnp.broadcast_to(ar, (8, LANE)), jnp.broadcast_to(ai, (8, LANE))

        def rows(ch, k):
            return pl.ds(pl.multiple_of(k * nseg + ch * 8, 8), 8)

        def advance(h, ch, k):
            hr, hi = h
            return (arb * hr - aib * hi + or_ref[rows(ch, k), :], arb * hi + aib * hr + oi_ref[rows(ch, k), :])

        def kk(n):
            return seg - 1 - n if reverse else n

        zero = jnp.zeros((8, LANE), f32)

        def sweep1(n, hs):
            return tuple(advance(hs[ch], ch, kk(n)) for ch in range(nch))

        ends = lax.fori_loop(0, seg, sweep1, tuple((zero, zero) for _ in range(nch)))

        pr, pi = ar, ai
        for _ in range(n_sq):
            pr, pi = pr * pr - pi * pi, 2.0 * pr * pi
        in_r, in_i = [None] * nseg, [None] * nseg
        cr = ci = jnp.zeros((1, LANE), f32)
        for j in (range(nseg - 1, -1, -1) if reverse else range(nseg)):
            in_r[j], in_i[j] = cr, ci
            er, ei = ends[j // 8][0][j % 8:j % 8 + 1, :], ends[j // 8][1][j % 8:j % 8 + 1, :]
            cr, ci = er + pr * cr - pi * ci, ei + pr * ci + pi * cr
        h0 = tuple((jnp.concatenate(in_r[8 * ch:8 * ch + 8], axis=0), jnp.concatenate(in_i[8 * ch:8 * ch + 8], axis=0))
                   for ch in range(nch))
        if not adj:
            for ch in range(nch):
                oin_ref[0, 8 * ch:8 * ch + 8, :] = h0[ch][0]
                oin_ref[1, 8 * ch:8 * ch + 8, :] = h0[ch][1]

        def emit(ch, k, h):
            or_ref[rows(ch, k), :] = h[0]
            oi_ref[rows(ch, k), :] = h[1]

        def pair(h, p):
            return h[0] * p[0] + h[1] * p[1], h[1] * p[0] - h[0] * p[1]

        def sweep2(n, carry):
            k = kk(n)
            new = tuple(advance(carry[ch], ch, k) for ch in range(nch))
            for ch in range(nch):
                emit(ch, k, new[ch])
            if not adj:
                return new
            dr, di = carry[nch]
            for ch in range(nch):
                qr, qi = pair(new[ch], (hr_ref[rows(ch, k - 1), :], hi_ref[rows(ch, k - 1), :]))
                dr, di = dr + qr, di + qi
            return new + ((dr, di),)

        if adj:
            carry = lax.fori_loop(0, seg - 1, sweep2, h0 + ((zero, zero),))
            dr, di = carry[nch]
            for ch in range(nch):
                new = advance(carry[ch], ch, 0)
                emit(ch, 0, new)
                qr, qi = pair(new, (hin_ref[0, 8 * ch:8 * ch + 8, :], hin_ref[1, 8 * ch:8 * ch + 8, :]))
                dr, di = dr + qr, di + qi
            da_ref[0] = jnp.sum(dr, axis=0, keepdims=True)
            da_ref[1] = jnp.sum(di, axis=0, keepdims=True)
        else:
            lax.fori_loop(0, seg, sweep2, h0)

        first = pl.program_id(0) % per == 0
        for i in range(n_chunks):
            sl = slice(i * chunk, (i + 1) * chunk)
            h2 = jnp.concatenate([or_ref[sl, :], oi_ref[sl, :]], axis=1).astype(bf16)
            part = lax.dot_general(h2, wo_ref[...], (NT, ((), ())), preferred_element_type=f32)

            @pl.when(first)
            def _():
                res_ref[sl, :] = part

            @pl.when(jnp.logical_not(first))
            def _():
                res_ref[sl, :] += part

        if adj:
            def over_time(xt_ref, yr_ref, yi_ref):
                tot = jnp.zeros((LANE, 2 * LANE), f32)
                for i in range(n_chunks):
                    sl = slice(i * chunk, (i + 1) * chunk)
                    y2 = jnp.concatenate([yr_ref[sl, :], yi_ref[sl, :]], axis=1).astype(bf16)
                    tot += jnp.dot(xt_ref[:, sl], y2, preferred_element_type=f32)
                return tot[:, :LANE], tot[:, LANE:]

            dbr_ref[...], dbi_ref[...] = over_time(ut_ref, or_ref, oi_ref)
            dcr_ref[...], dci_ref[...] = over_time(dyt_ref, hr_ref, hi_ref)

    col = pl.BlockSpec((s_len, LANE), lambda l: (0, l))
    chan = pl.BlockSpec((s_len, LANE), lambda l: (0, l // per))
    in_spec = pl.BlockSpec((2, nseg, LANE), lambda l: (0, 0, l))
    w_spec = pl.BlockSpec((None, LANE, LANE), lambda l: (l, 0, 0))
    w2_spec = pl.BlockSpec((None, LANE, 2 * LANE), lambda l: (l, 0, 0))
    ins = [src, jnp.concatenate(w_in, axis=2), jnp.concatenate(w_out, axis=2), a2]
    in_specs = [chan, w2_spec, w2_spec, pl.BlockSpec((8, LANE), lambda l: (0, l))]
    outs, out_specs = [_sds((s_len, n_ch), f32)], [chan]
    scratch = []
    if adj:
        ins += list(adjoint_of)
        chan_t = pl.BlockSpec((LANE, s_len), lambda l: (l // per, 0))
        in_specs += [chan_t, chan_t, col, col, in_spec]
        outs += [_sds((2, 1, gp), f32)] + [_sds((gp // LANE, LANE, LANE), f32)] * 4
        out_specs += [pl.BlockSpec((2, 1, LANE), lambda l: (0, 0, l))] + [w_spec] * 4
        scratch = [pltpu.VMEM((s_len, LANE), f32)] * 2
    else:
        outs += [_sds((s_len, gp), f32)] * 2 + [_sds((2, nseg, gp), f32)]
        out_specs += [col, col, in_spec]
    res = _pcall(body, name=name, grid=(gp // LANE,), in_specs=in_specs, out_specs=out_specs, out_shape=outs,
                 scratch_shapes=scratch, sem=("arbitrary",))(*ins)
    if adj:
        return res[0], res[1], (res[2], res[3]), (res[4], res[5])
    return res


def _ssm_param_fn(lam_re, lam_im, log_step, b_re2, b_im2, expand):
    step = jnp.exp(log_step)
    xr, xi = lam_re * step, lam_im * step
    mag = jnp.exp(xr)
    ar, ai = mag * jnp.cos(xi), mag * jnp.sin(xi)
    nr, ni = ar - 1.0, ai
    den = lam_re * lam_re + lam_im * lam_im
    cr = (nr * lam_re + ni * lam_im) / den
    ci = (ni * lam_re - nr * lam_im) / den
    cre = jnp.dot(cr, expand, precision=HI, preferred_element_type=f32)
    cie = jnp.dot(ci, expand, precision=HI, preferred_element_type=f32)
    return ar, ai, cre * b_re2 - cie * b_im2, cre * b_im2 + cie * b_re2


def _ssm_params(lam_re, lam_im, log_step, b_re2, b_im2, expand):
    def body(*refs):
        res = _ssm_param_fn(*[r[...] for r in refs[:6]])
        for r, o in zip(refs[6:], res):
            r[...] = o

    g, p = lam_re.shape
    return _pcall(body, name="ssm_params", out_shape=[_sds((g, p), f32)] * 2 + [_sds(b_re2.shape, f32)] * 2)(
        lam_re, lam_im, log_step, b_re2, b_im2, expand)


def _ssm_params_bwd(lam_re, lam_im, log_step, b_re2, b_im2, expand, d_ar, d_ai, d_bbr, d_bbi):
    def body(*refs):
        prim = [r[...] for r in refs[:5]]
        ex = refs[5][...]
        cot = tuple(r[...] for r in refs[6:10])
        _, vjp = jax.vjp(lambda *p_: _ssm_param_fn(*p_, ex), *prim)
        for r, o in zip(refs[10:], vjp(cot)):
            r[...] = o

    shapes = [lam_re.shape, lam_im.shape, log_step.shape, b_re2.shape, b_im2.shape]
    return _pcall(body, name="ssm_params_bwd", out_shape=[_sds(s, f32) for s in shapes])(
        lam_re, lam_im, log_step, b_re2, b_im2, expand, d_ar, d_ai, d_bbr, d_bbi)


def _slope_table(n_heads):
    s = 2.0 ** (-8.0 * (jnp.arange(n_heads, dtype=f32) + 1.0) / n_heads)
    return jnp.broadcast_to(s[:, None, None], (n_heads, 1, LANE))


def _band_bias(slope_d, shift):
    qi = lax.broadcasted_iota(jnp.int32, (BAND, BAND), 0)
    ki = lax.broadcasted_iota(jnp.int32, (BAND, BAND), 1)
    mask = (ki >= qi) if shift else (ki <= qi)
    return jnp.where(mask, -slope_d * (qi - ki + shift).astype(f32), NEG)


def _window_bias(slope_d, has_prev):
    own = _band_bias(slope_d, 0)
    mid = jnp.concatenate([_band_bias(slope_d, BAND), own], axis=1)
    none = jnp.concatenate([jnp.full((BAND, BAND), NEG, f32), own], axis=1)
    return mid, jnp.where(has_prev, mid, none)


def _window_scores(q, k2, bias):
    return lax.dot_general(q, k2, (NT, ((), ())), preferred_element_type=f32) * (HEAD ** -0.5) + bias


def _attn_geometry(s_len, dil, rows=1024):
    piece = BAND * dil
    m = max(1, rows // piece)
    while s_len % (piece * m):
        m //= 2
    return m, piece


def _stream_rows(start, dil):
    return pl.ds(start, BAND, stride=dil) if dil > 1 else pl.ds(start, BAND)


def _attn_fwd(qn, kn, proj, v_blk, dil, slopes):
    s_len, aw = qn.shape
    n_heads = aw // HEAD
    m, piece = _attn_geometry(s_len, dil)
    rows = m * piece

    def body(q_ref, k_ref, kp_ref, v_ref, vp_ref, sl_ref, o_ref, lse_ref):
        bias_mid, bias_first = _window_bias(sl_ref[:, 0:1] * float(dil), pl.program_id(1) > 0)
        for b in range(m):
            for r in range(dil):
                idx = _stream_rows(b * piece + r, dil)
                q, kc, vc = (ref[idx, :].astype(bf16) for ref in (q_ref, k_ref, v_ref))
                if b:
                    pidx = _stream_rows((b - 1) * piece + r, dil)
                    kp, vp = k_ref[pidx, :].astype(bf16), v_ref[pidx, :].astype(bf16)
                else:
                    pidx = _stream_rows(r, dil)
                    kp, vp = kp_ref[pidx, :].astype(bf16), vp_ref[pidx, :].astype(bf16)
                k2, v2 = jnp.concatenate([kp, kc], axis=0), jnp.concatenate([vp, vc], axis=0)
                s = _window_scores(q, k2, bias_mid if b else bias_first)
                mx = jnp.max(s, axis=1, keepdims=True)
                p = jnp.exp(s - mx)
                den = jnp.sum(p, axis=1, keepdims=True)
                o_ref[idx, :] = jnp.dot(p.astype(bf16), v2, preferred_element_type=f32) / den
                lse_ref[idx, :] = jnp.broadcast_to(mx + jnp.log(den), (BAND, HEAD))

    def cur(blk0):
        return pl.BlockSpec((rows, HEAD), lambda h, t: (t, blk0 + h))

    def prev(blk0):
        return pl.BlockSpec((piece, HEAD), lambda h, t: (jnp.maximum(t * m - 1, 0), blk0 + h))

    sl = pl.BlockSpec((None, 1, LANE), lambda h, t: (h, 0, 0))
    return _pcall(body, name=f"attn_fwd_d{dil}", grid=(n_heads, s_len // rows),
                  in_specs=[cur(0), cur(0), prev(0), cur(v_blk), prev(v_blk), sl], out_specs=[cur(0), cur(0)],
                  out_shape=[_sds((s_len, aw), f32)] * 2, sem=("parallel", "parallel"))(
        qn, kn, kn, proj, proj, slopes)


def _attn_bwd(qn, kn, proj, v_blk, do, lse, dd, dil, slopes, acc=None):
    s_len, aw = qn.shape
    n_heads = aw // HEAD
    m, piece = _attn_geometry(s_len, dil, max(1024, 2 * BAND * dil))
    rows = m * piece
    n_tiles = s_len // rows
    scale = HEAD ** -0.5

    def body(q_ref, qx_ref, k_ref, kp_ref, v_ref, vp_ref, do_ref, dox_ref, l_ref, lx_ref, d_ref, dx_ref, sl_ref,
             *rest):
        dq_ref, dk_ref, dv_ref = rest[-3:]

        def put(ref, which, idx, val):
            ref[idx, :] = val if acc is None else val + rest[which][idx, :]

        t = pl.program_id(1)
        slope_d = sl_ref[:, 0:1] * float(dil)
        bias_mid, bias_first = _window_bias(slope_d, t > 0)
        bias_next = _band_bias(slope_d, BAND)

        def query_side(ref_q, ref_do, ref_l, ref_d, idx):
            return (ref_q[idx, :].astype(bf16), ref_do[idx, :].astype(bf16), ref_l[idx, :][:, 0:1],
                    ref_d[idx, :][:, 0:1])

        def probs(qs, keys, values, bias):
            q, do_, l_col, d_col = qs
            p = jnp.exp(_window_scores(q, keys, bias) - l_col)
            dp = lax.dot_general(do_, values, (NT, ((), ())), preferred_element_type=f32)
            return p.astype(bf16), (p * (dp - d_col)).astype(bf16)

        def tn(a_, b_):
            return lax.dot_general(a_, b_, (TN, ((), ())), preferred_element_type=f32)

        for r in range(dil):
            pend = None
            for b in range(m):
                idx = _stream_rows(b * piece + r, dil)
                qs = query_side(q_ref, do_ref, l_ref, d_ref, idx)
                kc, vc = k_ref[idx, :].astype(bf16), v_ref[idx, :].astype(bf16)
                if b:
                    kp, vp = kc_prev, vc_prev
                else:
                    pidx = _stream_rows(r, dil)
                    kp, vp = kp_ref[pidx, :].astype(bf16), vp_ref[pidx, :].astype(bf16)
                k2, v2 = jnp.concatenate([kp, kc], axis=0), jnp.concatenate([vp, vc], axis=0)
                p, ds = probs(qs, k2, v2, bias_mid if b else bias_first)
                put(dq_ref, 0, idx, jnp.dot(ds, k2, preferred_element_type=f32) * scale)
                dk2, dv2 = tn(ds, qs[0]), tn(p, qs[1])
                if pend is not None:
                    put(dk_ref, 1, pend[0], (pend[1] + dk2[:BAND]) * scale)
                    put(dv_ref, 2, pend[0], pend[2] + dv2[:BAND])
                pend = (idx, dk2[BAND:], dv2[BAND:])
                kc_prev, vc_prev = kc, vc
            qs = query_side(qx_ref, dox_ref, lx_ref, dx_ref, _stream_rows(r, dil))
            p, ds = probs(qs, kc_prev, vc_prev, bias_next)
            live = t < n_tiles - 1
            put(dk_ref, 1, pend[0], (pend[1] + jnp.where(live, tn(ds, qs[0]), 0.0)) * scale)
            put(dv_ref, 2, pend[0], pend[2] + jnp.where(live, tn(p, qs[1]), 0.0))

    def cur(blk0):
        return pl.BlockSpec((rows, HEAD), lambda h, t: (t, blk0 + h))

    def prev(blk0):
        return pl.BlockSpec((piece, HEAD), lambda h, t: (jnp.maximum(t * m - 1, 0), blk0 + h))

    def nxt(blk0):
        return pl.BlockSpec((piece, HEAD), lambda h, t: (jnp.minimum(t * m + m, n_tiles * m - 1), blk0 + h))

    sl = pl.BlockSpec((None, 1, LANE), lambda h, t: (h, 0, 0))
    in_specs = [cur(0), nxt(0), cur(0), prev(0), cur(v_blk), prev(v_blk), cur(0), nxt(0), cur(0), nxt(0), cur(0),
                nxt(0), sl]
    ins = [qn, qn, kn, kn, proj, proj, do, do, lse, lse, dd, dd, slopes]
    extra = {}
    if acc is not None:
        extra = dict(input_output_aliases={len(ins) + i: i for i in range(3)})
        in_specs += [cur(0)] * 3
        ins += list(acc)
    return _pcall(body, name=f"attn_bwd_d{dil}", grid=(n_heads, n_tiles), in_specs=in_specs,
                  out_specs=[cur(0)] * 3, out_shape=[_sds((s_len, aw), f32)] * 3,
                  sem=("parallel", "parallel"), **extra)(*ins)


def _exchange(name, srcs, scatter):
    n = len(srcs)

    def body(*refs):
        src, out = refs[:n], refs[n:2 * n]
        send_sems, recv_sems, local_sems = refs[2 * n:]
        x, y, c = lax.axis_index("x"), lax.axis_index("y"), lax.axis_index("c")
        me = 4 * x + 2 * y + c

        def peer(r):
            return ((1 - x) if r & 4 else x, (1 - y) if r & 2 else y, (1 - c) if r & 1 else c)

        def lin(p):
            return 4 * p[0] + 2 * p[1] + p[2]

        def piece(a, idx):
            return src[a].at[idx] if scatter[a] else src[a]

        local, sends = [], []
        for a in range(n):
            cp = pltpu.make_async_copy(piece(a, me), out[a].at[me], local_sems.at[a])
            cp.start()
            local.append(cp)
        for r in range(1, N_DEV):
            p = peer(r)
            for a in range(n):
                cp = pltpu.make_async_remote_copy(src_ref=piece(a, lin(p)), dst_ref=out[a].at[me],
                                                  send_sem=send_sems.at[a, r - 1], recv_sem=recv_sems.at[a, r - 1],
                                                  device_id=p, device_id_type=MESH)
                cp.start()
                sends.append(cp)
        for r in range(1, N_DEV):
            p = peer(r)
            for a in range(n):
                pltpu.make_async_remote_copy(src_ref=piece(a, lin(p)), dst_ref=out[a].at[lin(p)],
                                             send_sem=send_sems.at[a, r - 1], recv_sem=recv_sems.at[a, r - 1],
                                             device_id=p, device_id_type=MESH).wait_recv()
        for cp in sends:
            cp.wait_send()
        for cp in local:
            cp.wait()

    def piece_shape(a):
        return srcs[a].shape[1:] if scatter[a] else srcs[a].shape

    any_spec = pl.BlockSpec(memory_space=pl.ANY)
    return _pcall(body, name=name, in_specs=[any_spec] * n, out_specs=[any_spec] * n,
                  out_shape=[_sds((N_DEV, *piece_shape(a)), srcs[a].dtype) for a in range(n)],
                  scratch_shapes=[pltpu.SemaphoreType.DMA((n, N_DEV - 1)), pltpu.SemaphoreType.DMA((n, N_DEV - 1)),
                                  pltpu.SemaphoreType.DMA((n,))])(*srcs)


_HBM = pl.BlockSpec(memory_space=pltpu.HBM)
_SEM = pl.BlockSpec(memory_space=pltpu.SEMAPHORE)
_EFFECT = pltpu.SideEffectType.DATAFLOW_SIDE_EFFECTING


def _peer_ids():
    x, y, c = lax.axis_index("x"), lax.axis_index("y"), lax.axis_index("c")
    peers = [((1 - x) if r & 4 else x, (1 - y) if r & 2 else y, (1 - c) if r & 1 else c) for r in range(1, N_DEV)]
    return 4 * x + 2 * y + c, peers, [4 * p[0] + 2 * p[1] + p[2] for p in peers]


def _exchange_start(name, src, scatter, after):
    piece_shape = src.shape[1:] if scatter else src.shape

    def body(src_ref, land_ref, after_ref, send_sems, recv_sems, local_sem, src_thru, land_thru, token):
        me, peers, lins = _peer_ids()

        def piece(idx):
            return src_ref.at[idx] if scatter else src_ref

        pltpu.make_async_copy(piece(me), land_ref.at[me], local_sem).start()
        for r, (p, lp) in enumerate(zip(peers, lins)):
            pltpu.make_async_remote_copy(src_ref=piece(lp), dst_ref=land_ref.at[me], send_sem=send_sems.at[r],
                                         recv_sem=recv_sems.at[r], device_id=p, device_id_type=MESH).start()
        token[...] = jnp.zeros_like(token)

    land = pltpu.with_memory_space_constraint(lax.empty((N_DEV, *piece_shape), src.dtype), pltpu.HBM)
    send_sems, recv_sems, local_sem, src_thru, land_thru, token = pl.pallas_call(
        body, name=name,
        out_shape=(pltpu.SemaphoreType.DMA((N_DEV - 1,)), pltpu.SemaphoreType.DMA((N_DEV - 1,)),
                   pltpu.SemaphoreType.DMA(()), pltpu.HBM(src.shape, src.dtype),
                   pltpu.HBM((N_DEV, *piece_shape), src.dtype), _sds((8, LANE), f32)),
        in_specs=(_HBM, _HBM, pl.BlockSpec(memory_space=pl.ANY)),
        out_specs=(_SEM, _SEM, _SEM, _HBM, _HBM, pl.BlockSpec(memory_space=pltpu.VMEM)),
        input_output_aliases={0: 3, 1: 4},
        compiler_params=pltpu.CompilerParams(has_side_effects=_EFFECT),
    )(pltpu.with_memory_space_constraint(src, pltpu.HBM), land, after)
    return (send_sems, recv_sems, local_sem, src_thru, land_thru, scatter), token


def _exchange_wait(name, handle, *after):
    send_sems, recv_sems, local_sem, src_thru, land_thru, scatter = handle

    def body(src_ref, land_ref, send_sems_, recv_sems_, local_sem_, *rest):
        me, peers, lins = _peer_ids()

        def piece(idx):
            return src_ref.at[idx] if scatter else src_ref

        pltpu.make_async_copy(piece(me), land_ref.at[me], local_sem_).wait()
        for r, (p, lp) in enumerate(zip(peers, lins)):
            pltpu.make_async_remote_copy(src_ref=piece(lp), dst_ref=land_ref.at[me], send_sem=send_sems_.at[r],
                                         recv_sem=recv_sems_.at[r], device_id=p, device_id_type=MESH).wait_send()
            pltpu.make_async_remote_copy(src_ref=piece(lp), dst_ref=land_ref.at[lp], send_sem=send_sems_.at[r],
                                         recv_sem=recv_sems_.at[r], device_id=p, device_id_type=MESH).wait_recv()

    return pl.pallas_call(
        body, name=name,
        out_shape=(pltpu.HBM(src_thru.shape, src_thru.dtype), pltpu.HBM(land_thru.shape, land_thru.dtype)),
        in_specs=(_HBM, _HBM, _SEM, _SEM, _SEM, *[pl.BlockSpec(memory_space=pl.ANY)] * len(after)),
        out_specs=(_HBM, _HBM), input_output_aliases={0: 0, 1: 1},
        compiler_params=pltpu.CompilerParams(has_side_effects=_EFFECT),
    )(src_thru, land_thru, send_sems, recv_sems, local_sem, *after)[1]


def _adamw(name, w, m, v, g_or_stack, stacked, rows=256):
    r, c = w.shape
    tr = _tile(r, rows)

    def fn(w_, m_, v_, g_):
        if stacked:
            g = g_[0].astype(f32)
            for j in range(1, N_DEV):
                g = g + g_[j].astype(f32)
        else:
            g = g_
        m_new = ADAM_B1 * m_ + (1.0 - ADAM_B1) * g
        v_new = ADAM_B2 * v_ + (1.0 - ADAM_B2) * (g * g)
        m_hat = m_new / (1.0 - ADAM_B1 ** ADAM_STEP)
        v_hat = v_new / (1.0 - ADAM_B2 ** ADAM_STEP)
        delta = -ADAM_LR * (m_hat / (jnp.sqrt(v_hat) + ADAM_EPS) + ADAM_WD * w_)
        return g, delta, m_new, v_new

    blk = _row(tr, c)
    g_spec = pl.BlockSpec((N_DEV, tr, c), lambda i: (0, i, 0)) if stacked else blk
    return _rowwise(name, fn, r // tr, [w, m, v, g_or_stack], [blk, blk, blk, g_spec],
                    [_sds((r, c), f32)] * 4, [blk] * 4, [False] * 4)


def _ada_fwd(c_all, w_shard, b_shard):
    nb_, d = c_all.shape
    n = w_shard.shape[1]
    tn = _tile(n, 512)

    def body(c_ref, w_ref, b_ref, o_ref):
        a = jax.nn.silu(c_ref[...]).astype(bf16)
        o_ref[...] = jnp.dot(a, w_ref[...].astype(bf16), preferred_element_type=f32) + b_ref[...]

    return _pcall(body, name="ada_fwd", grid=(n // tn,),
                  in_specs=[pl.BlockSpec((nb_, d), lambda j: (0, 0)), pl.BlockSpec((d, tn), lambda j: (0, j)),
                            pl.BlockSpec((1, tn), lambda j: (0, j))],
                  out_specs=pl.BlockSpec((nb_, tn), lambda j: (0, j)), out_shape=_sds((nb_, n), f32),
                  sem=("parallel",))(c_all, w_shard, b_shard)


def _ada_bwd(c_all, dmod_cols):
    nb_, d = c_all.shape
    n = dmod_cols.shape[1]
    tn = _tile(n, 512)

    def body(c_ref, g_ref, o_ref):
        a = jax.nn.silu(c_ref[...]).astype(bf16).astype(f32)
        g = g_ref[...].astype(bf16).astype(f32)
        o_ref[...] = lax.dot_general(a, g, (TN, ((), ())), precision=HI, preferred_element_type=f32)

    return _pcall(body, name="ada_bwd", grid=(n // tn,),
                  in_specs=[pl.BlockSpec((nb_, d), lambda j: (0, 0)), pl.BlockSpec((nb_, tn), lambda j: (0, j))],
                  out_specs=pl.BlockSpec((d, tn), lambda j: (0, j)), out_shape=_sds((d, n), f32),
                  sem=("parallel",))(c_all, dmod_cols)


SMALL_LATE = ("b_ada", "norm1_g", "q_norm_g", "k_norm_g")
SMALL_EARLY = ("lam_re", "lam_im", "log_step", "b_re", "b_im", "c_re", "c_im", "d_skip", "b_glu", "attn_out_g",
               "ssm_out_g", "norm2_g")
ORDER = ("w_ada", "b_ada", "norm1_g", "w_in", "q_norm_g", "k_norm_g", "lam_re", "lam_im", "log_step", "b_re", "b_im",
         "c_re", "c_im", "d_skip", "w_glu", "b_glu", "attn_out_g", "ssm_out_g", "w_out", "norm2_g", "w_ff1", "w_ff2")


def _pack(parts):
    flat = jnp.concatenate([p.reshape(-1) for p in parts])
    pad = (-flat.shape[0]) % (8 * LANE)
    return jnp.pad(flat, (0, pad)).reshape(-1, LANE)


def kernel(x, c, w_ada, b_ada, norm1_g, w_in, q_norm_g, k_norm_g, lam_re, lam_im, log_step, b_re, b_im, c_re, c_im, d_skip, w_glu, b_glu, attn_out_g, ssm_out_g, w_out, norm2_g, w_ff1, w_ff2, loss_target, m_w_ada, m_b_ada, m_norm1_g, m_w_in, m_q_norm_g, m_k_norm_g, m_lam_re, m_lam_im, m_log_step, m_b_re, m_b_im, m_c_re, m_c_im, m_d_skip, m_w_glu, m_b_glu, m_attn_out_g, m_ssm_out_g, m_w_out, m_norm2_g, m_w_ff1, m_w_ff2, v_w_ada, v_b_ada, v_norm1_g, v_w_in, v_q_norm_g, v_k_norm_g, v_lam_re, v_lam_im, v_log_step, v_b_re, v_b_im, v_c_re, v_c_im, v_d_skip, v_w_glu, v_b_glu, v_attn_out_g, v_ssm_out_g, v_w_out, v_norm2_g, v_w_ff1, v_w_ff2):
    env = dict(locals())
    wts = {n: env[n] for n in ORDER}
    mom = {n: env["m_" + n] for n in ORDER}
    var = {n: env["v_" + n] for n in ORDER}

    xs, tgt = x[0], loss_target[0]
    s_len, d = xs.shape
    aw = d // 2
    sw = d - aw
    n_heads = aw // HEAD
    n_groups = sw // SSM_GROUP
    n_state = lam_re.shape[-1]
    gp = n_groups * n_state
    tm = _tile(s_len, 256)
    steps = s_len // tm
    me = 4 * lax.axis_index("x") + 2 * lax.axis_index("y") + lax.axis_index("c")

    (c_all,) = _exchange("gather_c", [c], [False])
    c_all = c_all.reshape(N_DEV, d)

    n_ada = w_ada.shape[-1]
    b_ada_cols = lax.dynamic_slice_in_dim(b_ada, me * n_ada, n_ada, axis=1)
    mod_part = _ada_fwd(c_all, w_ada[0], b_ada_cols)
    (mod_all,) = _exchange("gather_mod", [mod_part], [False])
    mod = lax.dynamic_index_in_dim(mod_all, me, axis=1, keepdims=False).reshape(1, 6 * d)
    sh1, sc1, g1, sh2, sc2, g2 = (mod[:, i * d:(i + 1) * d] for i in range(6))

    gather, started = {}, jnp.zeros((1, 1), f32)
    for name in ("w_in", "w_glu", "w_out", "w_ff1", "w_ff2"):
        gather[name], token = _exchange_start("gather_" + name, wts[name][0].astype(bf16), False, mod_all)
        started = started + token[0:1, 0:1]
    sc1 = sc1 + started

    (h,) = _rowwise("norm1", _norm_mod, steps, [xs, norm1_g, sc1, sh1],
                    [_row(tm, d), _vec(d), _vec(d), _vec(d)], [_sds((s_len, d), bf16)], [_row(tm, d)], [False])
    lam_re2, lam_im2 = lam_re[0], lam_im[0]
    log_step2 = log_step[0].reshape(n_groups, 1)
    b_re2 = b_re[0].reshape(n_groups, n_state * SSM_GROUP)
    b_im2 = b_im[0].reshape(n_groups, n_state * SSM_GROUP)
    expand = jnp.repeat(jnp.eye(n_state, dtype=f32), SSM_GROUP, axis=1)
    a_re, a_im, bb_re2, bb_im2 = _ssm_params(lam_re2, lam_im2, log_step2, b_re2, b_im2, expand)
    a2 = jnp.zeros((8, gp), f32).at[0].set(a_re.reshape(gp)).at[1].set(a_im.reshape(gp))
    w_bu = tuple(_lane_block_weights(t.reshape(n_groups, n_state, SSM_GROUP).transpose(0, 2, 1), n_state)
                 for t in (bb_re2, bb_im2))
    w_c = (_lane_block_weights(c_re[0], n_state), _lane_block_weights(-c_im[0], n_state))

    packed = {names: tuple(_pack([t[n] for n in names]) for t in (wts, mom, var))
              for names in (SMALL_LATE, SMALL_EARLY)}

    win_g = _exchange_wait("gathered_w_in", gather["w_in"], h, a2, *w_bu, *w_c, *packed[SMALL_LATE],
                           *packed[SMALL_EARLY])
    (proj,) = _mm_nn_sharded("in_proj", h, win_g)

    def qk_fn(q, k, gq, gk):
        return _head_rms(q, gq), _head_rms(k, gk)

    qn, kn = _rowwise("qk_norm", qk_fn, steps, [proj, proj, q_norm_g, k_norm_g],
                      [_row(tm, aw, 0), _row(tm, aw, 1), _vec(HEAD), _vec(HEAD)],
                      [_sds((s_len, aw), f32)] * 2, [_row(tm, aw)] * 2, [False] * 2)
    v_blk = 2 * aw // HEAD

    slopes = _slope_table(n_heads)
    pat = [_attn_fwd(qn, kn, proj, v_blk, dil, slopes) for _, dil in DILATIONS]

    def attn_mix_fn(o1, l1, o2, l2, o3, l3):
        m = jnp.maximum(jnp.maximum(l1, l2), l3)
        e1, e2, e3 = jnp.exp(l1 - m), jnp.exp(l2 - m), jnp.exp(l3 - m)
        tot = e1 + e2 + e3
        return (e1 * o1 + e2 * o2 + e3 * o3) / tot, m + jnp.log(tot)

    attn, lse = _rowwise("attn_mix", attn_mix_fn, steps, [t for ol in pat for t in ol], [_row(tm, aw)] * 6,
                         [_sds((s_len, aw), f32)] * 2, [_row(tm, aw)] * 2, [False] * 2)

    nseg = _scan_segments(s_len)
    u_seg = _to_segments(proj[:, 3 * aw:], nseg).astype(bf16)
    y_seg, h_re, h_im, hin_f = _scan("ssm_scan", u_seg, w_bu, w_c, a2, reverse=False)
    ymm = _from_segments(y_seg, nseg)

    u_spec = _row(tm, sw, 3 * aw // sw)
    (yg,) = _rowwise("ssm_gelu", _ypre_fn, steps, [ymm, proj, d_skip], [_row(tm, sw), u_spec, _vec(sw)],
                     [_sds((s_len, sw), f32)], [_row(tm, sw)], [False])
    wglu_g = _exchange_wait("gathered_w_glu", gather["w_glu"], yg).reshape(sw, sw)
    (z,) = _mm_nn("glu_proj", yg, wglu_g)
    (cat,) = _rowwise("mix_norm", _mix_fn, steps, [attn, yg, z, b_glu, attn_out_g, ssm_out_g],
                      [_row(tm, aw), _row(tm, sw), _row(tm, sw), _vec(sw), _vec(aw), _vec(sw)],
                      [_sds((s_len, d), bf16)], [_row(tm, d)], [False])
    wout_g = _exchange_wait("gathered_w_out", gather["w_out"], cat).reshape(d, d)
    (mixed,) = _mm_nn("out_proj", cat, wout_g)

    def res_norm2_fn(x_, mixed_, g1_, gn, sc, sh):
        x1_ = x_ + g1_ * mixed_
        return x1_, _norm_mod(x1_, gn, sc, sh)

    x1, h2 = _rowwise("norm2", res_norm2_fn, steps, [xs, mixed, g1, norm2_g, sc2, sh2],
                      [_row(tm, d), _row(tm, d)] + [_vec(d)] * 4,
                      [_sds((s_len, d), f32), _sds((s_len, d), bf16)], [_row(tm, d)] * 2, [False] * 2)

    def act_epilogue(acc):
        r = jnp.maximum(acc, 0.0)
        return r, r * r

    wff1_g = _exchange_wait("gathered_w_ff1", gather["w_ff1"], h2)
    r_ff, act = _mm_nn_sharded("ff1", h2, wff1_g, epilogue=act_epilogue,
                               outs=[_sds((s_len, 4 * d), bf16), _sds((s_len, 4 * d), bf16)])
    wff2_g = _exchange_wait("gathered_w_ff2", gather["w_ff2"], act).reshape(4 * d, d)
    (ff,) = _mm_nn("ff2", act, wff2_g)

    def loss_fn(x1_, ff_, tgt_, g2_):
        e = x1_ + g2_ * ff_ - tgt_
        dy_ = e * (1.0 / d)
        part = jnp.full((1, LANE), 0.5 / d, f32) * jnp.sum(e * e)
        return dy_, g2_ * dy_, part, jnp.sum(dy_ * ff_, axis=0, keepdims=True)

    dy, dff, loss_part, d_g2 = _rowwise(
        "loss", loss_fn, steps, [x1, ff, tgt, g2], [_row(tm, d)] * 3 + [_vec(d)],
        [_sds((s_len, d), f32), _sds((s_len, d), bf16), _sds((1, LANE), f32), _sds((1, d), f32)],
        [_row(tm, d), _row(tm, d), _vec(LANE), _vec(d)], [False, False, True, True])
    loss = lax.psum(loss_part[0, 0], ("x", "y", "c"))

    def dact_epilogue(acc, r_):
        return (acc * (2.0 * r_.astype(f32)),)

    (da,) = _mm_nt("ff2_dx", dff, wff2_g, epilogue=dact_epilogue, extra=[r_ff], outs=[_sds((s_len, 4 * d), bf16)])
    scatter = {}
    g_wff2 = _mm_tn("ff2_dw", act, dff, after=loss.reshape(1, 1)).reshape(N_DEV, 4 * d // N_DEV, d)
    scatter["w_ff2"], tok_ff2 = _exchange_start("scatter_w_ff2", g_wff2, True, loss.reshape(1, 1))
    dh2 = _mm_nt("ff1_dx", da, wff1_g.transpose(1, 0, 2).reshape(d, 4 * d))[0]
    g_wff1 = _mm_tn_sharded("ff1_dw", h2, da, N_DEV)
    scatter["w_ff1"], tok_ff1 = _exchange_start("scatter_w_ff1", g_wff1, True, started)
    norm2_g_t = norm2_g + (tok_ff2[0:1, 0:1] + tok_ff1[0:1, 0:1])

    def norm2_bwd_fn(dh2_, x1_, dy_, mixed_, gn, sc, sh, g1_):
        _, vjp = jax.vjp(_norm_mod, x1_, gn, sc, sh)
        dx, dgn, dsc, dsh = vjp(dh2_)
        dx1_ = dy_ + dx
        return dx1_, g1_ * dx1_, dgn, dsc, dsh, jnp.sum(dx1_ * mixed_, axis=0, keepdims=True)

    dx1, dmixed, d_norm2_g, d_sc2, d_sh2, d_g1 = _rowwise(
        "norm2_bwd", norm2_bwd_fn, steps, [dh2, x1, dy, mixed, norm2_g_t, sc2, sh2, g1],
        [_row(tm, d)] * 4 + [_vec(d)] * 4,
        [_sds((s_len, d), f32), _sds((s_len, d), bf16)] + [_sds((1, d), f32)] * 4,
        [_row(tm, d)] * 2 + [_vec(d)] * 4, [False, False, True, True, True, True])

    (dcat,) = _mm_nt("out_dx", dmixed, wout_g)
    g_wout = _mm_tn("out_dw", cat, dmixed).reshape(N_DEV, d // N_DEV, d)
    scatter["w_out"], tok_out = _exchange_start("scatter_w_out", g_wout, True, started)
    b_glu_t = b_glu + tok_out[0:1, 0:1]

    def mix_bwd_fn(dcat_, attn_, yg_, z_, bglu, ga, gs):
        _, vjp = jax.vjp(_mix_fn, attn_, yg_, z_, bglu, ga, gs)
        dattn_, dyg_, dz_, dbglu, dga, dgs = vjp(dcat_)
        prod = dattn_ * attn_
        dd_ = jnp.concatenate([jnp.broadcast_to(jnp.sum(prod[:, i * HEAD:(i + 1) * HEAD], axis=1, keepdims=True),
                                                (prod.shape[0], HEAD)) for i in range(n_heads)], axis=1)
        return dattn_, dd_, dyg_, dz_, dbglu, dga, dgs

    dattn, dd, dyg1, dz, d_b_glu, d_attn_out_g, d_ssm_out_g = _rowwise(
        "mix_bwd", mix_bwd_fn, steps, [dcat, attn, yg, z, b_glu_t, attn_out_g, ssm_out_g],
        [_row(tm, d), _row(tm, aw), _row(tm, sw), _row(tm, sw), _vec(sw), _vec(aw), _vec(sw)],
        [_sds((s_len, aw), f32), _sds((s_len, aw), f32), _sds((s_len, sw), f32), _sds((s_len, sw), bf16),
         _sds((1, sw), f32), _sds((1, aw), f32), _sds((1, sw), f32)],
        [_row(tm, aw), _row(tm, aw), _row(tm, sw), _row(tm, sw), _vec(sw), _vec(aw), _vec(sw)],
        [False] * 4 + [True] * 3)

    (dyg2,) = _mm_nt("glu_dx", dz, wglu_g)
    g_wglu = _mm_tn("glu_dw", yg, dz).reshape(N_DEV, sw // N_DEV, sw)
    scatter["w_glu"], tok_glu = _exchange_start("scatter_w_glu", g_wglu, True, started)
    d_skip_t = d_skip + tok_glu[0:1, 0:1]

    def gelu_bwd_fn(dyg1_, dyg2_, ymm_, u_, dskip):
        _, vjp = jax.vjp(_ypre_fn, ymm_, u_, dskip)
        dymm, du_, ddskip = vjp(dyg1_ + dyg2_)
        return dymm, du_, ddskip

    dymm, du_skip, d_d_skip = _rowwise(
        "ssm_gelu_bwd", gelu_bwd_fn, steps, [dyg1, dyg2, ymm, proj, d_skip_t],
        [_row(tm, sw)] * 3 + [u_spec, _vec(sw)],
        [_sds((s_len, sw), f32), _sds((s_len, sw), f32), _sds((1, sw), f32)],
        [_row(tm, sw), _row(tm, sw), _vec(sw)], [False, False, True])

    dymm_seg = _to_segments(dymm, nseg).astype(bf16)
    du_seg, da_seg, dbb_c, dc_c = _scan("ssm_adj", dymm_seg, w_c, w_bu, a2, reverse=True,
                                        adjoint_of=(u_seg.T, dymm_seg.T, h_re, h_im, hin_f))
    du_ssm = _from_segments(du_seg, nseg)

    def to_gpi(w):
        return _lane_block_diag(w, n_state).transpose(0, 2, 1).reshape(n_groups, n_state * SSM_GROUP)

    d_lam_re, d_lam_im, d_log_step, d_b_re2, d_b_im2 = _ssm_params_bwd(
        lam_re2, lam_im2, log_step2, b_re2, b_im2, expand,
        da_seg[0, 0].reshape(n_groups, n_state), da_seg[1, 0].reshape(n_groups, n_state),
        to_gpi(dbb_c[0]), to_gpi(dbb_c[1]))
    d_c_re = _lane_block_diag(dc_c[0], n_state)
    d_c_im = -_lane_block_diag(dc_c[1], n_state)

    grads_qkv = None
    for _, dil in reversed(DILATIONS):
        grads_qkv = _attn_bwd(qn, kn, proj, v_blk, dattn, lse, dd, dil, slopes, acc=grads_qkv)

    def qkv_bwd_fn(q, k, gq, gk, dqn, dkn, dv, du1, du2):
        _, vjp = jax.vjp(lambda q_, k_, gq_, gk_: (_head_rms(q_, gq_), _head_rms(k_, gk_)), q, k, gq, gk)
        dq, dk, dgq, dgk = vjp((dqn, dkn))
        return jnp.concatenate([dq, dk, dv, du1 + du2], axis=1), dgq, dgk

    small_g = {"lam_re": d_lam_re, "lam_im": d_lam_im, "log_step": d_log_step, "b_re": d_b_re2, "b_im": d_b_im2,
               "c_re": d_c_re, "c_im": d_c_im, "d_skip": d_d_skip, "b_glu": d_b_glu,
               "attn_out_g": d_attn_out_g, "ssm_out_g": d_ssm_out_g, "norm2_g": d_norm2_g}
    early, tok_early = _exchange_start("gather_early_grads", _pack([small_g[n] for n in SMALL_EARLY]), False, started)

    dproj, small_g["q_norm_g"], small_g["k_norm_g"] = _rowwise(
        "qk_norm_bwd", qkv_bwd_fn, steps,
        [proj, proj, q_norm_g + tok_early[0:1, 0:1], k_norm_g, *grads_qkv, du_skip, du_ssm],
        [_row(tm, aw, 0), _row(tm, aw, 1), _vec(HEAD), _vec(HEAD)] + [_row(tm, aw)] * 3 + [_row(tm, sw)] * 2,
        [_sds((s_len, 3 * aw + sw), bf16), _sds((1, HEAD), f32), _sds((1, HEAD), f32)],
        [_row(tm, 3 * aw + sw), _vec(HEAD), _vec(HEAD)], [False, True, True])

    g_win = _mm_tn_sharded("in_dw", h, dproj, N_DEV)
    scatter["w_in"], tok_in = _exchange_start("scatter_w_in", g_win, True, tok_early)
    dh = _mm_nt("in_dx", dproj, win_g.transpose(1, 0, 2).reshape(d, 3 * aw + sw), after=tok_in)[0]
    norm1_g_t = norm1_g + tok_in[0:1, 0:1]

    def norm1_bwd_fn(dh_, x_, dx1_, gn, sc, sh):
        _, vjp = jax.vjp(_norm_mod, x_, gn, sc, sh)
        dx, dgn, dsc, dsh = vjp(dh_)
        return dx1_ + dx, dgn, dsc, dsh

    grad_x, d_norm1_g, d_sc1, d_sh1 = _rowwise(
        "norm1_bwd", norm1_bwd_fn, steps, [dh, xs, dx1, norm1_g_t, sc1, sh1], [_row(tm, d)] * 3 + [_vec(d)] * 3,
        [_sds((s_len, d), f32)] + [_sds((1, d), f32)] * 3, [_row(tm, d)] + [_vec(d)] * 3,
        [False, True, True, True])

    small_g["b_ada"] = jnp.concatenate([d_sh1, d_sc1, d_g1, d_sh2, d_sc2, d_g2], axis=1)
    small_g["norm1_g"] = d_norm1_g
    (r_late,) = _exchange("gather_late_grads", [_pack([small_g[n] for n in SMALL_LATE])], [False])

    res = {}
    dmod_all = r_late.reshape(N_DEV, -1)[:, :6 * d]
    g_wada = _ada_bwd(c_all, lax.dynamic_slice_in_dim(dmod_all, me * n_ada, n_ada, axis=1))
    res["w_ada"] = _adamw("adamw_w_ada", w_ada[0], m_w_ada[0], v_w_ada[0], g_wada, False)
    after = res["w_ada"][1]
    for name in ("w_ff2", "w_ff1", "w_out", "w_glu", "w_in"):
        stack = _exchange_wait("scattered_" + name, scatter[name], after)
        res[name] = _adamw("adamw_" + name, wts[name][0], mom[name][0], var[name][0], stack, True)
        after = res[name][1]
    r_early = _exchange_wait("gathered_early_grads", early, after)
    for label, names, stack in (("late", SMALL_LATE, r_late), ("early", SMALL_EARLY, r_early)):
        small_res = _adamw("adamw_small_" + label, *packed[names], stack, True, rows=4096)
        off = 0
        for n in names:
            size = wts[n].size
            res[n] = [t.reshape(-1)[off:off + size] for t in small_res]
            off += size

    out = [loss, grad_x[None]]
    for i in range(4):
        out += [res[n][i].reshape(wts[n].shape) for n in ORDER]
    return tuple(out)
```

```python
import math

import jax
import jax.numpy as jnp
from jax import lax
from jax.experimental import pallas as pl
from jax.experimental.pallas import tpu as pltpu

f32, bf16 = jnp.float32, jnp.bfloat16

N_DEV = 8
LANE = 128
HEAD = 128
SSM_GROUP = 16
DILATIONS = ((128, 1), (512, 4), (2048, 16))
BAND = 128
EPS = 1e-6
ADAM_LR, ADAM_B1, ADAM_B2, ADAM_EPS, ADAM_WD, ADAM_STEP = 0.001, 0.9, 0.999, 1e-08, 0.01, 10
NEG = -1e30
VMEM_LIMIT = 60 * 1024 * 1024
HI = lax.Precision.HIGHEST
MESH = pl.DeviceIdType.MESH


def _pcall(body, **kw):
    sem = kw.pop("sem", None)
    kw["compiler_params"] = pltpu.CompilerParams(dimension_semantics=sem, vmem_limit_bytes=VMEM_LIMIT)
    return pl.pallas_call(body, **kw)


def _tile(n, pref):
    t = min(n, pref)
    while n % t:
        t //= 2
    return t


def _sds(shape, dtype):
    return jax.ShapeDtypeStruct(shape, dtype)


def _rowwise(name, fn, steps, ins, in_specs, outs, out_specs, acc):
    n_in = len(ins)

    def body(*refs):
        res = fn(*[r[...] for r in refs[:n_in]])
        res = res if isinstance(res, (tuple, list)) else (res,)
        for r, o, a in zip(refs[n_in:], res, acc):
            if a:
                @pl.when(pl.program_id(0) == 0)
                def _():
                    r[...] = jnp.zeros_like(r)
                r[...] += o
            else:
                r[...] = o.astype(r.dtype)

    return _pcall(body, name=name, grid=(steps,), in_specs=in_specs, out_specs=out_specs, out_shape=outs,
                  sem=("arbitrary",))(*ins)


def _row(tm, c, blk=0):
    return pl.BlockSpec((tm, c), lambda i: (i, blk))


def _vec(c, blk=0):
    return pl.BlockSpec((1, c), lambda i: (0, blk))


def _rms(x, g):
    return x * lax.rsqrt(jnp.mean(x * x, axis=-1, keepdims=True) + EPS) * g


def _norm_mod(x, g, sc, sh):
    return _rms(x, g) * (1.0 + sc) + sh


def _head_rms(t, g):
    return jnp.concatenate([_rms(t[:, h * HEAD:(h + 1) * HEAD], g) for h in range(t.shape[1] // HEAD)], axis=1)


def _mix_fn(attn, yg, z, bglu, ga, gs):
    ssm = yg * jax.nn.sigmoid(z + bglu)
    return jnp.concatenate([_rms(attn, ga), _rms(ssm, gs)], axis=1)


def _ypre_fn(ymm, u, dskip):
    return jax.nn.gelu(ymm + dskip * u)


def _matmul(name, a, b, *, dims, grid, a_spec, b_spec, acc_shape, outs, out_specs, extra=(), extra_specs=(),
            epilogue=None, after=None):
    gk = grid[2]
    n_x = len(extra)
    placed = [] if after is None else [after]
    first_out = n_x + len(placed)
    ins = [a, b, *extra, *placed]
    in_specs = [a_spec, b_spec, *extra_specs] + [pl.BlockSpec(memory_space=pl.ANY)] * len(placed)

    def product(a_ref, b_ref):
        return lax.dot_general(a_ref[...].astype(bf16), b_ref[...].astype(bf16), (dims, ((), ())),
                               preferred_element_type=f32)

    def finish(res, x_refs, o_refs):
        res = epilogue(res, *[r[...] for r in x_refs]) if epilogue is not None else (res,)
        for r, o in zip(o_refs, res):
            r[...] = o.astype(r.dtype)

    def body_single(a_ref, b_ref, *rest):
        finish(product(a_ref, b_ref), rest[:n_x], rest[first_out:])

    def body_pair(a_ref, b_ref, *rest):
        acc = rest[-1]
        prod = product(a_ref, b_ref)

        @pl.when(pl.program_id(2) == 0)
        def _():
            acc[...] = prod

        @pl.when(pl.program_id(2) == 1)
        def _():
            finish(acc[...] + prod, rest[:n_x], rest[first_out:-1])

    def body(a_ref, b_ref, *rest):
        acc = rest[-1]
        k = pl.program_id(2)

        @pl.when(k == 0)
        def _():
            acc[...] = jnp.zeros_like(acc)

        acc[...] += product(a_ref, b_ref)

        @pl.when(k == gk - 1)
        def _():
            finish(acc[...], rest[:n_x], rest[first_out:-1])

    sem = ("parallel", "parallel", "arbitrary")
    if gk == 1:
        return _pcall(body_single, name=name, grid=grid, in_specs=in_specs, out_specs=out_specs, out_shape=outs,
                      sem=sem)(*ins)
    return _pcall(body_pair if gk == 2 else body, name=name, grid=grid, in_specs=in_specs, out_specs=out_specs,
                  out_shape=outs, scratch_shapes=[pltpu.VMEM(acc_shape, f32)], sem=sem)(*ins)


NN = ((1,), (0,))
NT = ((1,), (1,))
TN = ((0,), (0,))


def _mm_nn(name, a, b, out_dtype=f32, tm=1024, tn=1024, tk=2048, epilogue=None, extra=(), outs=None):
    m, kd = a.shape
    n = b.shape[1]
    tm, tn, tk = _tile(m, tm), _tile(n, tn), _tile(kd, tk)
    o_spec = pl.BlockSpec((tm, tn), lambda i, j, k: (i, j))
    outs = outs if outs is not None else [_sds((m, n), out_dtype)]
    return _matmul(name, a, b, dims=NN, grid=(m // tm, n // tn, kd // tk),
                   a_spec=pl.BlockSpec((tm, tk), lambda i, j, k: (i, k)),
                   b_spec=pl.BlockSpec((tk, tn), lambda i, j, k: (k, j)),
                   acc_shape=(tm, tn), outs=outs, out_specs=[o_spec] * len(outs),
                   extra=extra, extra_specs=[o_spec] * len(extra), epilogue=epilogue)


def _mm_nn_sharded(name, a, b3, out_dtype=f32, tm=1024, tk=2048, epilogue=None, outs=None):
    m, kd = a.shape
    nsh, _, n = b3.shape
    tm, tk = _tile(m, tm), _tile(kd, tk)
    o_spec = pl.BlockSpec((tm, n), lambda i, j, k: (i, j))
    outs = outs if outs is not None else [_sds((m, nsh * n), out_dtype)]
    return _matmul(name, a, b3, dims=NN, grid=(m // tm, nsh, kd // tk),
                   a_spec=pl.BlockSpec((tm, tk), lambda i, j, k: (i, k)),
                   b_spec=pl.BlockSpec((None, tk, n), lambda i, j, k: (j, k, 0)),
                   acc_shape=(tm, n), outs=outs, out_specs=[o_spec] * len(outs), epilogue=epilogue)


def _mm_nt(name, a, b, out_dtype=f32, tm=1024, tn=1024, tk=2048, epilogue=None, extra=(), outs=None, after=None):
    m, kd = a.shape
    n = b.shape[0]
    tm, tn, tk = _tile(m, tm), _tile(n, tn), _tile(kd, tk)
    o_spec = pl.BlockSpec((tm, tn), lambda i, j, k: (i, j))
    outs = outs if outs is not None else [_sds((m, n), out_dtype)]
    return _matmul(name, a, b, dims=NT, grid=(m // tm, n // tn, kd // tk),
                   a_spec=pl.BlockSpec((tm, tk), lambda i, j, k: (i, k)),
                   b_spec=pl.BlockSpec((tn, tk), lambda i, j, k: (j, k)),
                   acc_shape=(tm, tn), outs=outs, out_specs=[o_spec] * len(outs),
                   extra=extra, extra_specs=[o_spec] * len(extra), epilogue=epilogue, after=after)


def _mm_tn(name, a, b, out_dtype=bf16, tm=1024, tn=1024, tk=2048, after=None):
    t, m = a.shape
    n = b.shape[1]
    tm, tn, tk = _tile(m, tm), _tile(n, tn), _tile(t, tk)
    return _matmul(name, a, b, dims=TN, grid=(m // tm, n // tn, t // tk),
                   a_spec=pl.BlockSpec((tk, tm), lambda i, j, k: (k, i)),
                   b_spec=pl.BlockSpec((tk, tn), lambda i, j, k: (k, j)),
                   acc_shape=(tm, tn), outs=[_sds((m, n), out_dtype)],
                   out_specs=[pl.BlockSpec((tm, tn), lambda i, j, k: (i, j))], after=after)[0]


def _mm_tn_sharded(name, a, b, nsh, out_dtype=bf16, tm=1024, tk=2048):
    t, m = a.shape
    n = b.shape[1] // nsh
    tm, tk = _tile(m, tm), _tile(t, tk)
    return _matmul(name, a, b, dims=TN, grid=(m // tm, nsh, t // tk),
                   a_spec=pl.BlockSpec((tk, tm), lambda i, j, k: (k, i)),
                   b_spec=pl.BlockSpec((tk, n), lambda i, j, k: (k, j)),
                   acc_shape=(tm, n), outs=[_sds((nsh, m, n), out_dtype)],
                   out_specs=[pl.BlockSpec((None, tm, n), lambda i, j, k: (j, i, 0))])[0]


SCAN_CHAINS = 8


def _scan_segments(s_len):
    nch = SCAN_CHAINS
    while s_len % (8 * nch) or (s_len // (8 * nch)) & (s_len // (8 * nch) - 1):
        nch //= 2
    return 8 * nch


def _to_segments(t, nseg):
    s_len, c = t.shape
    return t.reshape(nseg, s_len // nseg, c).transpose(1, 0, 2).reshape(s_len, c)


def _from_segments(t, nseg):
    s_len, c = t.shape
    return t.reshape(s_len // nseg, nseg, c).transpose(1, 0, 2).reshape(s_len, c)


def _lane_block_weights(t3, n_state):
    n_groups = t3.shape[0]
    gpl = LANE // n_state
    per = LANE // (gpl * SSM_GROUP)
    n_lb = n_groups // gpl
    t5 = t3.reshape(n_lb // per, per, gpl, SSM_GROUP, n_state)
    w = jnp.einsum("aqgic,gh,qs->aqsgihc", t5, jnp.eye(gpl, dtype=t3.dtype), jnp.eye(per, dtype=t3.dtype))
    return w.reshape(n_lb, LANE, LANE).astype(bf16)


def _lane_block_diag(w, n_state):
    gpl = LANE // n_state
    per = LANE // (gpl * SSM_GROUP)
    n_lb = w.shape[0]
    w7 = w.reshape(n_lb // per, per, per, gpl, SSM_GROUP, gpl, n_state)
    t5 = jnp.einsum("aqsgihc,gh,qs->aqgic", w7, jnp.eye(gpl, dtype=w.dtype), jnp.eye(per, dtype=w.dtype))
    return t5.reshape(n_lb * gpl, SSM_GROUP, n_state)


def _scan(name, src, w_in, w_out, a2, *, reverse, adjoint_of=None):
    s_len, n_ch = src.shape
    gp = a2.shape[1]
    per = (gp // LANE) // (n_ch // LANE)
    nseg = _scan_segments(s_len)
    nch = nseg // 8
    seg = s_len // nseg
    n_sq = int(math.log2(seg))
    assert 2 ** n_sq == seg
    adj = adjoint_of is not None
    chunk = _tile(s_len, 1024)
    n_chunks = s_len // chunk

    def body(*refs):
        it = iter(refs)
        src_ref, wi_ref, wo_ref, a_ref = (next(it) for _ in range(4))
        if adj:
            ut_ref, dyt_ref, hr_ref, hi_ref, hin_ref = (next(it) for _ in range(5))
        res_ref = next(it)
        if adj:
            da_ref, dbr_ref, dbi_ref, dcr_ref, dci_ref = (next(it) for _ in range(5))
        else:
            or_ref, oi_ref, oin_ref = (next(it) for _ in range(3))
        if adj:
            or_ref, oi_ref = next(it), next(it)

        for i in range(n_chunks):
            x2 = jnp.dot(src_ref[i * chunk:(i + 1) * chunk, :], wi_ref[...], preferred_element_type=f32)
            or_ref[i * chunk:(i + 1) * chunk, :] = x2[:, :LANE]
            oi_ref[i * chunk:(i + 1) * chunk, :] = x2[:, LANE:]

        ar = a_ref[0:1, :]
        ai = -a_ref[1:2, :] if reverse else a_ref[1:2, :]
        arb, aib = jnp.broadcast_to(ar, (8, LANE)), jnp.broadcast_to(ai, (8, LANE))

        def rows(ch, k):
            return pl.ds(pl.multiple_of(k * nseg + ch * 8, 8), 8)

        def advance(h, ch, k):
            hr, hi = h
            return (arb * hr - aib * hi + or_ref[rows(ch, k), :], arb * hi + aib * hr + oi_ref[rows(ch, k), :])

        def kk(n):
            return seg - 1 - n if reverse else n

        zero = jnp.zeros((8, LANE), f32)

        def sweep1(n, hs):
            return tuple(advance(hs[ch], ch, kk(n)) for ch in range(nch))

        ends = lax.fori_loop(0, seg, sweep1, tuple((zero, zero) for _ in range(nch)))

        pr, pi = ar, ai
        for _ in range(n_sq):
            pr, pi = pr * pr - pi * pi, 2.0 * pr * pi
        in_r, in_i = [None] * nseg, [None] * nseg
        cr = ci = jnp.zeros((1, LANE), f32)
        for j in (range(nseg - 1, -1, -1) if reverse else range(nseg)):
            in_r[j], in_i[j] = cr, ci
            er, ei = ends[j // 8][0][j % 8:j % 8 + 1, :], ends[j // 8][1][j % 8:j % 8 + 1, :]
            cr, ci = er + pr * cr - pi * ci, ei + pr * ci + pi * cr
        h0 = tuple((jnp.concatenate(in_r[8 * ch:8 * ch + 8], axis=0), jnp.concatenate(in_i[8 * ch:8 * ch + 8], axis=0))
                   for ch in range(nch))
        if not adj:
            for ch in range(nch):
                oin_ref[0, 8 * ch:8 * ch + 8, :] = h0[ch][0]
                oin_ref[1, 8 * ch:8 * ch + 8, :] = h0[ch][1]

        def emit(ch, k, h):
            or_ref[rows(ch, k), :] = h[0]
            oi_ref[rows(ch, k), :] = h[1]

        def pair(h, p):
            return h[0] * p[0] + h[1] * p[1], h[1] * p[0] - h[0] * p[1]

        def sweep2(n, carry):
            k = kk(n)
            new = tuple(advance(carry[ch], ch, k) for ch in range(nch))
            for ch in range(nch):
                emit(ch, k, new[ch])
            if not adj:
                return new
            dr, di = carry[nch]
            for ch in range(nch):
                qr, qi = pair(new[ch], (hr_ref[rows(ch, k - 1), :], hi_ref[rows(ch, k - 1), :]))
                dr, di = dr + qr, di + qi
            return new + ((dr, di),)

        if adj:
            carry = lax.fori_loop(0, seg - 1, sweep2, h0 + ((zero, zero),))
            dr, di = carry[nch]
            for ch in range(nch):
                new = advance(carry[ch], ch, 0)
                emit(ch, 0, new)
                qr, qi = pair(new, (hin_ref[0, 8 * ch:8 * ch + 8, :], hin_ref[1, 8 * ch:8 * ch + 8, :]))
                dr, di = dr + qr, di + qi
            da_ref[0] = jnp.sum(dr, axis=0, keepdims=True)
            da_ref[1] = jnp.sum(di, axis=0, keepdims=True)
        else:
            lax.fori_loop(0, seg, sweep2, h0)

        first = pl.program_id(0) % per == 0
        for i in range(n_chunks):
            sl = slice(i * chunk, (i + 1) * chunk)
            h2 = jnp.concatenate([or_ref[sl, :], oi_ref[sl, :]], axis=1).astype(bf16)
            part = lax.dot_general(h2, wo_ref[...], (NT, ((), ())), preferred_element_type=f32)

            @pl.when(first)
            def _():
                res_ref[sl, :] = part

            @pl.when(jnp.logical_not(first))
            def _():
                res_ref[sl, :] += part

        if adj:
            def over_time(xt_ref, yr_ref, yi_ref):
                tot = jnp.zeros((LANE, 2 * LANE), f32)
                for i in range(n_chunks):
                    sl = slice(i * chunk, (i + 1) * chunk)
                    y2 = jnp.concatenate([yr_ref[sl, :], yi_ref[sl, :]], axis=1).astype(bf16)
                    tot += jnp.dot(xt_ref[:, sl], y2, preferred_element_type=f32)
                return tot[:, :LANE], tot[:, LANE:]

            dbr_ref[...], dbi_ref[...] = over_time(ut_ref, or_ref, oi_ref)
            dcr_ref[...], dci_ref[...] = over_time(dyt_ref, hr_ref, hi_ref)

    col = pl.BlockSpec((s_len, LANE), lambda l: (0, l))
    chan = pl.BlockSpec((s_len, LANE), lambda l: (0, l // per))
    in_spec = pl.BlockSpec((2, nseg, LANE), lambda l: (0, 0, l))
    w_spec = pl.BlockSpec((None, LANE, LANE), lambda l: (l, 0, 0))
    w2_spec = pl.BlockSpec((None, LANE, 2 * LANE), lambda l: (l, 0, 0))
    ins = [src, jnp.concatenate(w_in, axis=2), jnp.concatenate(w_out, axis=2), a2]
    in_specs = [chan, w2_spec, w2_spec, pl.BlockSpec((8, LANE), lambda l: (0, l))]
    outs, out_specs = [_sds((s_len, n_ch), f32)], [chan]
    scratch = []
    if adj:
        ins += list(adjoint_of)
        chan_t = pl.BlockSpec((LANE, s_len), lambda l: (l // per, 0))
        in_specs += [chan_t, chan_t, col, col, in_spec]
        outs += [_sds((2, 1, gp), f32)] + [_sds((gp // LANE, LANE, LANE), f32)] * 4
        out_specs += [pl.BlockSpec((2, 1, LANE), lambda l: (0, 0, l))] + [w_spec] * 4
        scratch = [pltpu.VMEM((s_len, LANE), f32)] * 2
    else:
        outs += [_sds((s_len, gp), f32)] * 2 + [_sds((2, nseg, gp), f32)]
        out_specs += [col, col, in_spec]
    res = _pcall(body, name=name, grid=(gp // LANE,), in_specs=in_specs, out_specs=out_specs, out_shape=outs,
                 scratch_shapes=scratch, sem=("arbitrary",))(*ins)
    if adj:
        return res[0], res[1], (res[2], res[3]), (res[4], res[5])
    return res


def _ssm_param_fn(lam_re, lam_im, log_step, b_re2, b_im2, expand):
    step = jnp.exp(log_step)
    xr, xi = lam_re * step, lam_im * step
    mag = jnp.exp(xr)
    ar, ai = mag * jnp.cos(xi), mag * jnp.sin(xi)
    nr, ni = ar - 1.0, ai
    den = lam_re * lam_re + lam_im * lam_im
    cr = (nr * lam_re + ni * lam_im) / den
    ci = (ni * lam_re - nr * lam_im) / den
    cre = jnp.dot(cr, expand, precision=HI, preferred_element_type=f32)
    cie = jnp.dot(ci, expand, precision=HI, preferred_element_type=f32)
    return ar, ai, cre * b_re2 - cie * b_im2, cre * b_im2 + cie * b_re2


def _ssm_params(lam_re, lam_im, log_step, b_re2, b_im2, expand):
    def body(*refs):
        res = _ssm_param_fn(*[r[...] for r in refs[:6]])
        for r, o in zip(refs[6:], res):
            r[...] = o

    g, p = lam_re.shape
    return _pcall(body, name="ssm_params", out_shape=[_sds((g, p), f32)] * 2 + [_sds(b_re2.shape, f32)] * 2)(
        lam_re, lam_im, log_step, b_re2, b_im2, expand)


def _ssm_params_bwd(lam_re, lam_im, log_step, b_re2, b_im2, expand, d_ar, d_ai, d_bbr, d_bbi):
    def body(*refs):
        prim = [r[...] for r in refs[:5]]
        ex = refs[5][...]
        cot = tuple(r[...] for r in refs[6:10])
        _, vjp = jax.vjp(lambda *p_: _ssm_param_fn(*p_, ex), *prim)
        for r, o in zip(refs[10:], vjp(cot)):
            r[...] = o

    shapes = [lam_re.shape, lam_im.shape, log_step.shape, b_re2.shape, b_im2.shape]
    return _pcall(body, name="ssm_params_bwd", out_shape=[_sds(s, f32) for s in shapes])(
        lam_re, lam_im, log_step, b_re2, b_im2, expand, d_ar, d_ai, d_bbr, d_bbi)


def _slope_table(n_heads):
    s = 2.0 ** (-8.0 * (jnp.arange(n_heads, dtype=f32) + 1.0) / n_heads)
    return jnp.broadcast_to(s[:, None, None], (n_heads, 1, LANE))


def _band_bias(slope_d, shift):
    qi = lax.broadcasted_iota(jnp.int32, (BAND, BAND), 0)
    ki = lax.broadcasted_iota(jnp.int32, (BAND, BAND), 1)
    mask = (ki >= qi) if shift else (ki <= qi)
    return jnp.where(mask, -slope_d * (qi - ki + shift).astype(f32), NEG)


def _window_bias(slope_d, has_prev):
    own = _band_bias(slope_d, 0)
    mid = jnp.concatenate([_band_bias(slope_d, BAND), own], axis=1)
    none = jnp.concatenate([jnp.full((BAND, BAND), NEG, f32), own], axis=1)
    return mid, jnp.where(has_prev, mid, none)


def _window_scores(q, k2, bias):
    return lax.dot_general(q, k2, (NT, ((), ())), preferred_element_type=f32) * (HEAD ** -0.5) + bias


def _attn_geometry(s_len, dil, rows=1024):
    piece = BAND * dil
    m = max(1, rows // piece)
    while s_len % (piece * m):
        m //= 2
    return m, piece


def _stream_rows(start, dil):
    return pl.ds(start, BAND, stride=dil) if dil > 1 else pl.ds(start, BAND)


def _attn_fwd(qn, kn, proj, v_blk, dil, slopes):
    s_len, aw = qn.shape
    n_heads = aw // HEAD
    m, piece = _attn_geometry(s_len, dil)
    rows = m * piece

    def body(q_ref, k_ref, kp_ref, v_ref, vp_ref, sl_ref, o_ref, lse_ref):
        bias_mid, bias_first = _window_bias(sl_ref[:, 0:1] * float(dil), pl.program_id(1) > 0)
        for b in range(m):
            for r in range(dil):
                idx = _stream_rows(b * piece + r, dil)
                q, kc, vc = (ref[idx, :].astype(bf16) for ref in (q_ref, k_ref, v_ref))
                if b:
                    pidx = _stream_rows((b - 1) * piece + r, dil)
                    kp, vp = k_ref[pidx, :].astype(bf16), v_ref[pidx, :].astype(bf16)
                else:
                    pidx = _stream_rows(r, dil)
                    kp, vp = kp_ref[pidx, :].astype(bf16), vp_ref[pidx, :].astype(bf16)
                k2, v2 = jnp.concatenate([kp, kc], axis=0), jnp.concatenate([vp, vc], axis=0)
                s = _window_scores(q, k2, bias_mid if b else bias_first)
                mx = jnp.max(s, axis=1, keepdims=True)
                p = jnp.exp(s - mx)
                den = jnp.sum(p, axis=1, keepdims=True)
                o_ref[idx, :] = jnp.dot(p.astype(bf16), v2, preferred_element_type=f32) / den
                lse_ref[idx, :] = jnp.broadcast_to(mx + jnp.log(den), (BAND, HEAD))

    def cur(blk0):
        return pl.BlockSpec((rows, HEAD), lambda h, t: (t, blk0 + h))

    def prev(blk0):
        return pl.BlockSpec((piece, HEAD), lambda h, t: (jnp.maximum(t * m - 1, 0), blk0 + h))

    sl = pl.BlockSpec((None, 1, LANE), lambda h, t: (h, 0, 0))
    return _pcall(body, name=f"attn_fwd_d{dil}", grid=(n_heads, s_len // rows),
                  in_specs=[cur(0), cur(0), prev(0), cur(v_blk), prev(v_blk), sl], out_specs=[cur(0), cur(0)],
                  out_shape=[_sds((s_len, aw), f32)] * 2, sem=("parallel", "parallel"))(
        qn, kn, kn, proj, proj, slopes)


def _attn_bwd(qn, kn, proj, v_blk, do, lse, dd, dil, slopes, acc=None):
    s_len, aw = qn.shape
    n_heads = aw // HEAD
    m, piece = _attn_geometry(s_len, dil, max(1024, 2 * BAND * dil))
    rows = m * piece
    n_tiles = s_len // rows
    scale = HEAD ** -0.5

    def body(q_ref, qx_ref, k_ref, kp_ref, v_ref, vp_ref, do_ref, dox_ref, l_ref, lx_ref, d_ref, dx_ref, sl_ref,
             *rest):
        dq_ref, dk_ref, dv_ref = rest[-3:]

        def put(ref, which, idx, val):
            ref[idx, :] = val if acc is None else val + rest[which][idx, :]

        t = pl.program_id(1)
        slope_d = sl_ref[:, 0:1] * float(dil)
        bias_mid, bias_first = _window_bias(slope_d, t > 0)
        bias_next = _band_bias(slope_d, BAND)

        def query_side(ref_q, ref_do, ref_l, ref_d, idx):
            return (ref_q[idx, :].astype(bf16), ref_do[idx, :].astype(bf16), ref_l[idx, :][:, 0:1],
                    ref_d[idx, :][:, 0:1])

        def probs(qs, keys, values, bias):
            q, do_, l_col, d_col = qs
            p = jnp.exp(_window_scores(q, keys, bias) - l_col)
            dp = lax.dot_general(do_, values, (NT, ((), ())), preferred_element_type=f32)
            return p.astype(bf16), (p * (dp - d_col)).astype(bf16)

        def tn(a_, b_):
            return lax.dot_general(a_, b_, (TN, ((), ())), preferred_element_type=f32)

        for r in range(dil):
            pend = None
            for b in range(m):
                idx = _stream_rows(b * piece + r, dil)
                qs = query_side(q_ref, do_ref, l_ref, d_ref, idx)
                kc, vc = k_ref[idx, :].astype(bf16), v_ref[idx, :].astype(bf16)
                if b:
                    kp, vp = kc_prev, vc_prev
                else:
                    pidx = _stream_rows(r, dil)
                    kp, vp = kp_ref[pidx, :].astype(bf16), vp_ref[pidx, :].astype(bf16)
                k2, v2 = jnp.concatenate([kp, kc], axis=0), jnp.concatenate([vp, vc], axis=0)
                p, ds = probs(qs, k2, v2, bias_mid if b else bias_first)
                put(dq_ref, 0, idx, jnp.dot(ds, k2, preferred_element_type=f32) * scale)
                dk2, dv2 = tn(ds, qs[0]), tn(p, qs[1])
                if pend is not None:
                    put(dk_ref, 1, pend[0], (pend[1] + dk2[:BAND]) * scale)
                    put(dv_ref, 2, pend[0], pend[2] + dv2[:BAND])
                pend = (idx, dk2[BAND:], dv2[BAND:])
                kc_prev, vc_prev = kc, vc
            qs = query_side(qx_ref, dox_ref, lx_ref, dx_ref, _stream_rows(r, dil))
            p, ds = probs(qs, kc_prev, vc_prev, bias_next)
            live = t < n_tiles - 1
            put(dk_ref, 1, pend[0], (pend[1] + jnp.where(live, tn(ds, qs[0]), 0.0)) * scale)
            put(dv_ref, 2, pend[0], pend[2] + jnp.where(live, tn(p, qs[1]), 0.0))

    def cur(blk0):
        return pl.BlockSpec((rows, HEAD), lambda h, t: (t, blk0 + h))

    def prev(blk0):
        return pl.BlockSpec((piece, HEAD), lambda h, t: (jnp.maximum(t * m - 1, 0), blk0 + h))

    def nxt(blk0):
        return pl.BlockSpec((piece, HEAD), lambda h, t: (jnp.minimum(t * m + m, n_tiles * m - 1), blk0 + h))

    sl = pl.BlockSpec((None, 1, LANE), lambda h, t: (h, 0, 0))
    in_specs = [cur(0), nxt(0), cur(0), prev(0), cur(v_blk), prev(v_blk), cur(0), nxt(0), cur(0), nxt(0), cur(0),
                nxt(0), sl]
    ins = [qn, qn, kn, kn, proj, proj, do, do, lse, lse, dd, dd, slopes]
    extra = {}
    if acc is not None:
        extra = dict(input_output_aliases={len(ins) + i: i for i in range(3)})
        in_specs += [cur(0)] * 3
        ins += list(acc)
    return _pcall(body, name=f"attn_bwd_d{dil}", grid=(n_heads, n_tiles), in_specs=in_specs,
                  out_specs=[cur(0)] * 3, out_shape=[_sds((s_len, aw), f32)] * 3,
                  sem=("parallel", "parallel"), **extra)(*ins)


def _exchange(name, srcs, scatter):
    n = len(srcs)

    def body(*refs):
        src, out = refs[:n], refs[n:2 * n]
        send_sems, recv_sems, local_sems = refs[2 * n:]
        x, y, c = lax.axis_index("x"), lax.axis_index("y"), lax.axis_index("c")
        me = 4 * x + 2 * y + c

        def peer(r):
            return ((1 - x) if r & 4 else x, (1 - y) if r & 2 else y, (1 - c) if r & 1 else c)

        def lin(p):
            return 4 * p[0] + 2 * p[1] + p[2]

        def piece(a, idx):
            return src[a].at[idx] if scatter[a] else src[a]

        local, sends = [], []
        for a in range(n):
            cp = pltpu.make_async_copy(piece(a, me), out[a].at[me], local_sems.at[a])
            cp.start()
            local.append(cp)
        for r in range(1, N_DEV):
            p = peer(r)
            for a in range(n):
                cp = pltpu.make_async_remote_copy(src_ref=piece(a, lin(p)), dst_ref=out[a].at[me],
                                                  send_sem=send_sems.at[a, r - 1], recv_sem=recv_sems.at[a, r - 1],
                                                  device_id=p, device_id_type=MESH)
                cp.start()
                sends.append(cp)
        for r in range(1, N_DEV):
            p = peer(r)
            for a in range(n):
                pltpu.make_async_remote_copy(src_ref=piece(a, lin(p)), dst_ref=out[a].at[lin(p)],
                                             send_sem=send_sems.at[a, r - 1], recv_sem=recv_sems.at[a, r - 1],
                                             device_id=p, device_id_type=MESH).wait_recv()
        for cp in sends:
            cp.wait_send()
        for cp in local:
            cp.wait()

    def piece_shape(a):
        return srcs[a].shape[1:] if scatter[a] else srcs[a].shape

    any_spec = pl.BlockSpec(memory_space=pl.ANY)
    return _pcall(body, name=name, in_specs=[any_spec] * n, out_specs=[any_spec] * n,
                  out_shape=[_sds((N_DEV, *piece_shape(a)), srcs[a].dtype) for a in range(n)],
                  scratch_shapes=[pltpu.SemaphoreType.DMA((n, N_DEV - 1)), pltpu.SemaphoreType.DMA((n, N_DEV - 1)),
                                  pltpu.SemaphoreType.DMA((n,))])(*srcs)


_HBM = pl.BlockSpec(memory_space=pltpu.HBM)
_SEM = pl.BlockSpec(memory_space=pltpu.SEMAPHORE)
_EFFECT = pltpu.SideEffectType.DATAFLOW_SIDE_EFFECTING


def _peer_ids():
    x, y, c = lax.axis_index("x"), lax.axis_index("y"), lax.axis_index("c")
    peers = [((1 - x) if r & 4 else x, (1 - y) if r & 2 else y, (1 - c) if r & 1 else c) for r in range(1, N_DEV)]
    return 4 * x + 2 * y + c, peers, [4 * p[0] + 2 * p[1] + p[2] for p in peers]


def _exchange_start(name, src, scatter, after):
    piece_shape = src.shape[1:] if scatter else src.shape

    def body(src_ref, land_ref, after_ref, send_sems, recv_sems, local_sem, src_thru, land_thru, token):
        me, peers, lins = _peer_ids()

        def piece(idx):
            return src_ref.at[idx] if scatter else src_ref

        pltpu.make_async_copy(piece(me), land_ref.at[me], local_sem).start()
        for r, (p, lp) in enumerate(zip(peers, lins)):
            pltpu.make_async_remote_copy(src_ref=piece(lp), dst_ref=land_ref.at[me], send_sem=send_sems.at[r],
                                         recv_sem=recv_sems.at[r], device_id=p, device_id_type=MESH).start()
        token[...] = jnp.zeros_like(token)

    land = pltpu.with_memory_space_constraint(lax.empty((N_DEV, *piece_shape), src.dtype), pltpu.HBM)
    send_sems, recv_sems, local_sem, src_thru, land_thru, token = pl.pallas_call(
        body, name=name,
        out_shape=(pltpu.SemaphoreType.DMA((N_DEV - 1,)), pltpu.SemaphoreType.DMA((N_DEV - 1,)),
                   pltpu.SemaphoreType.DMA(()), pltpu.HBM(src.shape, src.dtype),
                   pltpu.HBM((N_DEV, *piece_shape), src.dtype), _sds((8, LANE), f32)),
        in_specs=(_HBM, _HBM, pl.BlockSpec(memory_space=pl.ANY)),
        out_specs=(_SEM, _SEM, _SEM, _HBM, _HBM, pl.BlockSpec(memory_space=pltpu.VMEM)),
        input_output_aliases={0: 3, 1: 4},
        compiler_params=pltpu.CompilerParams(has_side_effects=_EFFECT),
    )(pltpu.with_memory_space_constraint(src, pltpu.HBM), land, after)
    return (send_sems, recv_sems, local_sem, src_thru, land_thru, scatter), token


def _exchange_wait(name, handle, *after):
    send_sems, recv_sems, local_sem, src_thru, land_thru, scatter = handle

    def body(src_ref, land_ref, send_sems_, recv_sems_, local_sem_, *rest):
        me, peers, lins = _peer_ids()

        def piece(idx):
            return src_ref.at[idx] if scatter else src_ref

        pltpu.make_async_copy(piece(me), land_ref.at[me], local_sem_).wait()
        for r, (p, lp) in enumerate(zip(peers, lins)):
            pltpu.make_async_remote_copy(src_ref=piece(lp), dst_ref=land_ref.at[me], send_sem=send_sems_.at[r],
                                         recv_sem=recv_sems_.at[r], device_id=p, device_id_type=MESH).wait_send()
            pltpu.make_async_remote_copy(src_ref=piece(lp), dst_ref=land_ref.at[lp], send_sem=send_sems_.at[r],
                                         recv_sem=recv_sems_.at[r], device_id=p, device_id_type=MESH).wait_recv()

    return pl.pallas_call(
        body, name=name,
        out_shape=(pltpu.HBM(src_thru.shape, src_thru.dtype), pltpu.HBM(land_thru.shape, land_thru.dtype)),
        in_specs=(_HBM, _HBM, _SEM, _SEM, _SEM, *[pl.BlockSpec(memory_space=pl.ANY)] * len(after)),
        out_specs=(_HBM, _HBM), input_output_aliases={0: 0, 1: 1},
        compiler_params=pltpu.CompilerParams(has_side_effects=_EFFECT),
    )(src_thru, land_thru, send_sems, recv_sems, local_sem, *after)[1]


def _adamw(name, w, m, v, g_or_stack, stacked, rows=256):
    r, c = w.shape
    tr = _tile(r, rows)

    def fn(w_, m_, v_, g_):
        if stacked:
            g = g_[0].astype(f32)
            for j in range(1, N_DEV):
                g = g + g_[j].astype(f32)
        else:
            g = g_
        m_new = ADAM_B1 * m_ + (1.0 - ADAM_B1) * g
        v_new = ADAM_B2 * v_ + (1.0 - ADAM_B2) * (g * g)
        m_hat = m_new / (1.0 - ADAM_B1 ** ADAM_STEP)
        v_hat = v_new / (1.0 - ADAM_B2 ** ADAM_STEP)
        delta = -ADAM_LR * (m_hat / (jnp.sqrt(v_hat) + ADAM_EPS) + ADAM_WD * w_)
        return g, delta, m_new, v_new

    blk = _row(tr, c)
    g_spec = pl.BlockSpec((N_DEV, tr, c), lambda i: (0, i, 0)) if stacked else blk
    return _rowwise(name, fn, r // tr, [w, m, v, g_or_stack], [blk, blk, blk, g_spec],
                    [_sds((r, c), f32)] * 4, [blk] * 4, [False] * 4)


def _ada_fwd(c_all, w_shard, b_shard):
    nb_, d = c_all.shape
    n = w_shard.shape[1]
    tn = _tile(n, 512)

    def body(c_ref, w_ref, b_ref, o_ref):
        a = jax.nn.silu(c_ref[...]).astype(bf16)
        o_ref[...] = jnp.dot(a, w_ref[...].astype(bf16), preferred_element_type=f32) + b_ref[...]

    return _pcall(body, name="ada_fwd", grid=(n // tn,),
                  in_specs=[pl.BlockSpec((nb_, d), lambda j: (0, 0)), pl.BlockSpec((d, tn), lambda j: (0, j)),
                            pl.BlockSpec((1, tn), lambda j: (0, j))],
                  out_specs=pl.BlockSpec((nb_, tn), lambda j: (0, j)), out_shape=_sds((nb_, n), f32),
                  sem=("parallel",))(c_all, w_shard, b_shard)


def _ada_bwd(c_all, dmod_cols):
    nb_, d = c_all.shape
    n = dmod_cols.shape[1]
    tn = _tile(n, 512)

    def body(c_ref, g_ref, o_ref):
        a = jax.nn.silu(c_ref[...]).astype(bf16).astype(f32)
        g = g_ref[...].astype(bf16).astype(f32)
        o_ref[...] = lax.dot_general(a, g, (TN, ((), ())), precision=HI, preferred_element_type=f32)

    return _pcall(body, name="ada_bwd", grid=(n // tn,),
                  in_specs=[pl.BlockSpec((nb_, d), lambda j: (0, 0)), pl.BlockSpec((nb_, tn), lambda j: (0, j))],
                  out_specs=pl.BlockSpec((d, tn), lambda j: (0, j)), out_shape=_sds((d, n), f32),
                  sem=("parallel",))(c_all, dmod_cols)


SMALL_LATE = ("b_ada", "norm1_g", "q_norm_g", "k_norm_g")
SMALL_EARLY = ("lam_re", "lam_im", "log_step", "b_re", "b_im", "c_re", "c_im", "d_skip", "b_glu", "attn_out_g",
               "ssm_out_g", "norm2_g")
ORDER = ("w_ada", "b_ada", "norm1_g", "w_in", "q_norm_g", "k_norm_g", "lam_re", "lam_im", "log_step", "b_re", "b_im",
         "c_re", "c_im", "d_skip", "w_glu", "b_glu", "attn_out_g", "ssm_out_g", "w_out", "norm2_g", "w_ff1", "w_ff2")


def _pack(parts):
    flat = jnp.concatenate([p.reshape(-1) for p in parts])
    pad = (-flat.shape[0]) % (8 * LANE)
    return jnp.pad(flat, (0, pad)).reshape(-1, LANE)


def kernel(x, c, w_ada, b_ada, norm1_g, w_in, q_norm_g, k_norm_g, lam_re, lam_im, log_step, b_re, b_im, c_re, c_im, d_skip, w_glu, b_glu, attn_out_g, ssm_out_g, w_out, norm2_g, w_ff1, w_ff2, loss_target, m_w_ada, m_b_ada, m_norm1_g, m_w_in, m_q_norm_g, m_k_norm_g, m_lam_re, m_lam_im, m_log_step, m_b_re, m_b_im, m_c_re, m_c_im, m_d_skip, m_w_glu, m_b_glu, m_attn_out_g, m_ssm_out_g, m_w_out, m_norm2_g, m_w_ff1, m_w_ff2, v_w_ada, v_b_ada, v_norm1_g, v_w_in, v_q_norm_g, v_k_norm_g, v_lam_re, v_lam_im, v_log_step, v_b_re, v_b_im, v_c_re, v_c_im, v_d_skip, v_w_glu, v_b_glu, v_attn_out_g, v_ssm_out_g, v_w_out, v_norm2_g, v_w_ff1, v_w_ff2):
    env = dict(locals())
    wts = {n: env[n] for n in ORDER}
    mom = {n: env["m_" + n] for n in ORDER}
    var = {n: env["v_" + n] for n in ORDER}

    xs, tgt = x[0], loss_target[0]
    s_len, d = xs.shape
    aw = d // 2
    sw = d - aw
    n_heads = aw // HEAD
    n_groups = sw // SSM_GROUP
    n_state = lam_re.shape[-1]
    gp = n_groups * n_state
    tm = _tile(s_len, 256)
    steps = s_len // tm
    me = 4 * lax.axis_index("x") + 2 * lax.axis_index("y") + lax.axis_index("c")

    (c_all,) = _exchange("gather_c", [c], [False])
    c_all = c_all.reshape(N_DEV, d)

    n_ada = w_ada.shape[-1]
    b_ada_cols = lax.dynamic_slice_in_dim(b_ada, me * n_ada, n_ada, axis=1)
    mod_part = _ada_fwd(c_all, w_ada[0], b_ada_cols)
    (mod_all,) = _exchange("gather_mod", [mod_part], [False])
    mod = lax.dynamic_index_in_dim(mod_all, me, axis=1, keepdims=False).reshape(1, 6 * d)
    sh1, sc1, g1, sh2, sc2, g2 = (mod[:, i * d:(i + 1) * d] for i in range(6))

    gather, started = {}, jnp.zeros((1, 1), f32)
    for name in ("w_in", "w_glu", "w_out", "w_ff1", "w_ff2"):
        gather[name], token = _exchange_start("gather_" + name, wts[name][0].astype(bf16), False, mod_all)
        started = started + token[0:1, 0:1]
    sc1 = sc1 + started

    (h,) = _rowwise("norm1", _norm_mod, steps, [xs, norm1_g, sc1, sh1],
                    [_row(tm, d), _vec(d), _vec(d), _vec(d)], [_sds((s_len, d), bf16)], [_row(tm, d)], [False])
    lam_re2, lam_im2 = lam_re[0], lam_im[0]
    log_step2 = log_step[0].reshape(n_groups, 1)
    b_re2 = b_re[0].reshape(n_groups, n_state * SSM_GROUP)
    b_im2 = b_im[0].reshape(n_groups, n_state * SSM_GROUP)
    expand = jnp.repeat(jnp.eye(n_state, dtype=f32), SSM_GROUP, axis=1)
    a_re, a_im, bb_re2, bb_im2 = _ssm_params(lam_re2, lam_im2, log_step2, b_re2, b_im2, expand)
    a2 = jnp.zeros((8, gp), f32).at[0].set(a_re.reshape(gp)).at[1].set(a_im.reshape(gp))
    w_bu = tuple(_lane_block_weights(t.reshape(n_groups, n_state, SSM_GROUP).transpose(0, 2, 1), n_state)
                 for t in (bb_re2, bb_im2))
    w_c = (_lane_block_weights(c_re[0], n_state), _lane_block_weights(-c_im[0], n_state))

    packed = {names: tuple(_pack([t[n] for n in names]) for t in (wts, mom, var))
              for names in (SMALL_LATE, SMALL_EARLY)}

    win_g = _exchange_wait("gathered_w_in", gather["w_in"], h, a2, *w_bu, *w_c, *packed[SMALL_LATE],
                           *packed[SMALL_EARLY])
    (proj,) = _mm_nn_sharded("in_proj", h, win_g)

    def qk_fn(q, k, gq, gk):
        return _head_rms(q, gq), _head_rms(k, gk)

    qn, kn = _rowwise("qk_norm", qk_fn, steps, [proj, proj, q_norm_g, k_norm_g],
                      [_row(tm, aw, 0), _row(tm, aw, 1), _vec(HEAD), _vec(HEAD)],
                      [_sds((s_len, aw), f32)] * 2, [_row(tm, aw)] * 2, [False] * 2)
    v_blk = 2 * aw // HEAD

    slopes = _slope_table(n_heads)
    pat = [_attn_fwd(qn, kn, proj, v_blk, dil, slopes) for _, dil in DILATIONS]

    def attn_mix_fn(o1, l1, o2, l2, o3, l3):
        m = jnp.maximum(jnp.maximum(l1, l2), l3)
        e1, e2, e3 = jnp.exp(l1 - m), jnp.exp(l2 - m), jnp.exp(l3 - m)
        tot = e1 + e2 + e3
        return (e1 * o1 + e2 * o2 + e3 * o3) / tot, m + jnp.log(tot)

    attn, lse = _rowwise("attn_mix", attn_mix_fn, steps, [t for ol in pat for t in ol], [_row(tm, aw)] * 6,
                         [_sds((s_len, aw), f32)] * 2, [_row(tm, aw)] * 2, [False] * 2)

    nseg = _scan_segments(s_len)
    u_seg = _to_segments(proj[:, 3 * aw:].astype(bf16), nseg)
    y_seg, h_re, h_im, hin_f = _scan("ssm_scan", u_seg, w_bu, w_c, a2, reverse=False)
    ymm = _from_segments(y_seg, nseg)

    u_spec = _row(tm, sw, 3 * aw // sw)
    (yg,) = _rowwise("ssm_gelu", _ypre_fn, steps, [ymm, proj, d_skip], [_row(tm, sw), u_spec, _vec(sw)],
                     [_sds((s_len, sw), f32)], [_row(tm, sw)], [False])
    wglu_g = _exchange_wait("gathered_w_glu", gather["w_glu"], yg).reshape(sw, sw)
    (z,) = _mm_nn("glu_proj", yg, wglu_g)
    (cat,) = _rowwise("mix_norm", _mix_fn, steps, [attn, yg, z, b_glu, attn_out_g, ssm_out_g],
                      [_row(tm, aw), _row(tm, sw), _row(tm, sw), _vec(sw), _vec(aw), _vec(sw)],
                      [_sds((s_len, d), bf16)], [_row(tm, d)], [False])
    wout_g = _exchange_wait("gathered_w_out", gather["w_out"], cat).reshape(d, d)
    (mixed,) = _mm_nn("out_proj", cat, wout_g)

    def res_norm2_fn(x_, mixed_, g1_, gn, sc, sh):
        x1_ = x_ + g1_ * mixed_
        return x1_, _norm_mod(x1_, gn, sc, sh)

    x1, h2 = _rowwise("norm2", res_norm2_fn, steps, [xs, mixed, g1, norm2_g, sc2, sh2],
                      [_row(tm, d), _row(tm, d)] + [_vec(d)] * 4,
                      [_sds((s_len, d), f32), _sds((s_len, d), bf16)], [_row(tm, d)] * 2, [False] * 2)

    def act_epilogue(acc):
        r = jnp.maximum(acc, 0.0)
        return r, r * r

    wff1_g = _exchange_wait("gathered_w_ff1", gather["w_ff1"], h2)
    r_ff, act = _mm_nn_sharded("ff1", h2, wff1_g, epilogue=act_epilogue,
                               outs=[_sds((s_len, 4 * d), bf16), _sds((s_len, 4 * d), bf16)])
    wff2_g = _exchange_wait("gathered_w_ff2", gather["w_ff2"], act).reshape(4 * d, d)
    (ff,) = _mm_nn("ff2", act, wff2_g)

    def loss_fn(x1_, ff_, tgt_, g2_):
        e = x1_ + g2_ * ff_ - tgt_
        dy_ = e * (1.0 / d)
        part = jnp.full((1, LANE), 0.5 / d, f32) * jnp.sum(e * e)
        return dy_, g2_ * dy_, part, jnp.sum(dy_ * ff_, axis=0, keepdims=True)

    dy, dff, loss_part, d_g2 = _rowwise(
        "loss", loss_fn, steps, [x1, ff, tgt, g2], [_row(tm, d)] * 3 + [_vec(d)],
        [_sds((s_len, d), f32), _sds((s_len, d), bf16), _sds((1, LANE), f32), _sds((1, d), f32)],
        [_row(tm, d), _row(tm, d), _vec(LANE), _vec(d)], [False, False, True, True])
    loss = lax.psum(loss_part[0, 0], ("x", "y", "c"))

    def dact_epilogue(acc, r_):
        return (acc * (2.0 * r_.astype(f32)),)

    (da,) = _mm_nt("ff2_dx", dff, wff2_g, epilogue=dact_epilogue, extra=[r_ff], outs=[_sds((s_len, 4 * d), bf16)])
    scatter = {}
    g_wff2 = _mm_tn("ff2_dw", act, dff, after=loss.reshape(1, 1)).reshape(N_DEV, 4 * d // N_DEV, d)
    scatter["w_ff2"], tok_ff2 = _exchange_start("scatter_w_ff2", g_wff2, True, loss.reshape(1, 1))
    dh2 = _mm_nt("ff1_dx", da, wff1_g.transpose(1, 0, 2).reshape(d, 4 * d))[0]
    g_wff1 = _mm_tn_sharded("ff1_dw", h2, da, N_DEV)
    scatter["w_ff1"], tok_ff1 = _exchange_start("scatter_w_ff1", g_wff1, True, started)
    norm2_g_t = norm2_g + (tok_ff2[0:1, 0:1] + tok_ff1[0:1, 0:1])

    def norm2_bwd_fn(dh2_, x1_, dy_, mixed_, gn, sc, sh, g1_):
        _, vjp = jax.vjp(_norm_mod, x1_, gn, sc, sh)
        dx, dgn, dsc, dsh = vjp(dh2_)
        dx1_ = dy_ + dx
        return dx1_, g1_ * dx1_, dgn, dsc, dsh, jnp.sum(dx1_ * mixed_, axis=0, keepdims=True)

    dx1, dmixed, d_norm2_g, d_sc2, d_sh2, d_g1 = _rowwise(
        "norm2_bwd", norm2_bwd_fn, steps, [dh2, x1, dy, mixed, norm2_g_t, sc2, sh2, g1],
        [_row(tm, d)] * 4 + [_vec(d)] * 4,
        [_sds((s_len, d), f32), _sds((s_len, d), bf16)] + [_sds((1, d), f32)] * 4,
        [_row(tm, d)] * 2 + [_vec(d)] * 4, [False, False, True, True, True, True])

    (dcat,) = _mm_nt("out_dx", dmixed, wout_g)
    g_wout = _mm_tn("out_dw", cat, dmixed).reshape(N_DEV, d // N_DEV, d)
    scatter["w_out"], tok_out = _exchange_start("scatter_w_out", g_wout, True, started)
    b_glu_t = b_glu + tok_out[0:1, 0:1]

    def mix_bwd_fn(dcat_, attn_, yg_, z_, bglu, ga, gs):
        _, vjp = jax.vjp(_mix_fn, attn_, yg_, z_, bglu, ga, gs)
        dattn_, dyg_, dz_, dbglu, dga, dgs = vjp(dcat_)
        prod = dattn_ * attn_
        dd_ = jnp.concatenate([jnp.broadcast_to(jnp.sum(prod[:, i * HEAD:(i + 1) * HEAD], axis=1, keepdims=True),
                                                (prod.shape[0], HEAD)) for i in range(n_heads)], axis=1)
        return dattn_, dd_, dyg_, dz_, dbglu, dga, dgs

    dattn, dd, dyg1, dz, d_b_glu, d_attn_out_g, d_ssm_out_g = _rowwise(
        "mix_bwd", mix_bwd_fn, steps, [dcat, attn, yg, z, b_glu_t, attn_out_g, ssm_out_g],
        [_row(tm, d), _row(tm, aw), _row(tm, sw), _row(tm, sw), _vec(sw), _vec(aw), _vec(sw)],
        [_sds((s_len, aw), f32), _sds((s_len, aw), f32), _sds((s_len, sw), f32), _sds((s_len, sw), bf16),
         _sds((1, sw), f32), _sds((1, aw), f32), _sds((1, sw), f32)],
        [_row(tm, aw), _row(tm, aw), _row(tm, sw), _row(tm, sw), _vec(sw), _vec(aw), _vec(sw)],
        [False] * 4 + [True] * 3)

    (dyg2,) = _mm_nt("glu_dx", dz, wglu_g)
    g_wglu = _mm_tn("glu_dw", yg, dz).reshape(N_DEV, sw // N_DEV, sw)
    scatter["w_glu"], tok_glu = _exchange_start("scatter_w_glu", g_wglu, True, started)
    d_skip_t = d_skip + tok_glu[0:1, 0:1]

    def gelu_bwd_fn(dyg1_, dyg2_, ymm_, u_, dskip):
        _, vjp = jax.vjp(_ypre_fn, ymm_, u_, dskip)
        dymm, du_, ddskip = vjp(dyg1_ + dyg2_)
        return dymm, du_, ddskip

    dymm, du_skip, d_d_skip = _rowwise(
        "ssm_gelu_bwd", gelu_bwd_fn, steps, [dyg1, dyg2, ymm, proj, d_skip_t],
        [_row(tm, sw)] * 3 + [u_spec, _vec(sw)],
        [_sds((s_len, sw), bf16), _sds((s_len, sw), f32), _sds((1, sw), f32)],
        [_row(tm, sw), _row(tm, sw), _vec(sw)], [False, False, True])

    dymm_seg = _to_segments(dymm, nseg)
    du_seg, da_seg, dbb_c, dc_c = _scan("ssm_adj", dymm_seg, w_c, w_bu, a2, reverse=True,
                                        adjoint_of=(u_seg.T, dymm_seg.T, h_re, h_im, hin_f))
    du_ssm = _from_segments(du_seg, nseg)

    def to_gpi(w):
        return _lane_block_diag(w, n_state).transpose(0, 2, 1).reshape(n_groups, n_state * SSM_GROUP)

    d_lam_re, d_lam_im, d_log_step, d_b_re2, d_b_im2 = _ssm_params_bwd(
        lam_re2, lam_im2, log_step2, b_re2, b_im2, expand,
        da_seg[0, 0].reshape(n_groups, n_state), da_seg[1, 0].reshape(n_groups, n_state),
        to_gpi(dbb_c[0]), to_gpi(dbb_c[1]))
    d_c_re = _lane_block_diag(dc_c[0], n_state)
    d_c_im = -_lane_block_diag(dc_c[1], n_state)

    grads_qkv = None
    for _, dil in reversed(DILATIONS):
        grads_qkv = _attn_bwd(qn, kn, proj, v_blk, dattn, lse, dd, dil, slopes, acc=grads_qkv)

    def qkv_bwd_fn(q, k, gq, gk, dqn, dkn, dv, du1, du2):
        _, vjp = jax.vjp(lambda q_, k_, gq_, gk_: (_head_rms(q_, gq_), _head_rms(k_, gk_)), q, k, gq, gk)
        dq, dk, dgq, dgk = vjp((dqn, dkn))
        return jnp.concatenate([dq, dk, dv, du1 + du2], axis=1), dgq, dgk

    small_g = {"lam_re": d_lam_re, "lam_im": d_lam_im, "log_step": d_log_step, "b_re": d_b_re2, "b_im": d_b_im2,
               "c_re": d_c_re, "c_im": d_c_im, "d_skip": d_d_skip, "b_glu": d_b_glu,
               "attn_out_g": d_attn_out_g, "ssm_out_g": d_ssm_out_g, "norm2_g": d_norm2_g}
    early, tok_early = _exchange_start("gather_early_grads", _pack([small_g[n] for n in SMALL_EARLY]), False, started)

    dproj, small_g["q_norm_g"], small_g["k_norm_g"] = _rowwise(
        "qk_norm_bwd", qkv_bwd_fn, steps,
        [proj, proj, q_norm_g + tok_early[0:1, 0:1], k_norm_g, *grads_qkv, du_skip, du_ssm],
        [_row(tm, aw, 0), _row(tm, aw, 1), _vec(HEAD), _vec(HEAD)] + [_row(tm, aw)] * 3 + [_row(tm, sw)] * 2,
        [_sds((s_len, 3 * aw + sw), bf16), _sds((1, HEAD), f32), _sds((1, HEAD), f32)],
        [_row(tm, 3 * aw + sw), _vec(HEAD), _vec(HEAD)], [False, True, True])

    g_win = _mm_tn_sharded("in_dw", h, dproj, N_DEV)
    scatter["w_in"], tok_in = _exchange_start("scatter_w_in", g_win, True, tok_early)
    dh = _mm_nt("in_dx", dproj, win_g.transpose(1, 0, 2).reshape(d, 3 * aw + sw), after=tok_in)[0]
    norm1_g_t = norm1_g + tok_in[0:1, 0:1]

    def norm1_bwd_fn(dh_, x_, dx1_, gn, sc, sh):
        _, vjp = jax.vjp(_norm_mod, x_, gn, sc, sh)
        dx, dgn, dsc, dsh = vjp(dh_)
        return dx1_ + dx, dgn, dsc, dsh

    grad_x, d_norm1_g, d_sc1, d_sh1 = _rowwise(
        "norm1_bwd", norm1_bwd_fn, steps, [dh, xs, dx1, norm1_g_t, sc1, sh1], [_row(tm, d)] * 3 + [_vec(d)] * 3,
        [_sds((s_len, d), f32)] + [_sds((1, d), f32)] * 3, [_row(tm, d)] + [_vec(d)] * 3,
        [False, True, True, True])

    small_g["b_ada"] = jnp.concatenate([d_sh1, d_sc1, d_g1, d_sh2, d_sc2, d_g2], axis=1)
    small_g["norm1_g"] = d_norm1_g
    (r_late,) = _exchange("gather_late_grads", [_pack([small_g[n] for n in SMALL_LATE])], [False])

    res = {}
    dmod_all = r_late.reshape(N_DEV, -1)[:, :6 * d]
    g_wada = _ada_bwd(c_all, lax.dynamic_slice_in_dim(dmod_all, me * n_ada, n_ada, axis=1))
    res["w_ada"] = _adamw("adamw_w_ada", w_ada[0], m_w_ada[0], v_w_ada[0], g_wada, False)
    after = res["w_ada"][1]
    for name in ("w_ff2", "w_ff1", "w_out", "w_glu", "w_in"):
        stack = _exchange_wait("scattered_" + name, scatter[name], after)
        res[name] = _adamw("adamw_" + name, wts[name][0], mom[name][0], var[name][0], stack, True)
        after = res[name][1]
    r_early = _exchange_wait("gathered_early_grads", early, after)
    for label, names, stack in (("late", SMALL_LATE, r_late), ("early", SMALL_EARLY, r_early)):
        small_res = _adamw("adamw_small_" + label, *packed[names], stack, True, rows=4096)
        off = 0
        for n in names:
            size = wts[n].size
            res[n] = [t.reshape(-1)[off:off + size] for t in small_res]
            off += size

    out = [loss, grad_x[None]]
    for i in range(4):
        out += [res[n][i].reshape(wts[n].shape) for n in ORDER]
    return tuple(out)
```

```python
import math

import jax
import jax.numpy as jnp
from jax import lax
from jax.experimental import pallas as pl
from jax.experimental.pallas import tpu as pltpu

f32, bf16 = jnp.float32, jnp.bfloat16

N_DEV = 8
LANE = 128
HEAD = 128
SSM_GROUP = 16
DILATIONS = ((128, 1), (512, 4), (2048, 16))
BAND = 128
EPS = 1e-6
ADAM_LR, ADAM_B1, ADAM_B2, ADAM_EPS, ADAM_WD, ADAM_STEP = 0.001, 0.9, 0.999, 1e-08, 0.01, 10
NEG = -1e30
VMEM_LIMIT = 60 * 1024 * 1024
HI = lax.Precision.HIGHEST
MESH = pl.DeviceIdType.MESH


def _pcall(body, **kw):
    sem = kw.pop("sem", None)
    kw["compiler_params"] = pltpu.CompilerParams(dimension_semantics=sem, vmem_limit_bytes=VMEM_LIMIT)
    return pl.pallas_call(body, **kw)


def _tile(n, pref):
    t = min(n, pref)
    while n % t:
        t //= 2
    return t


def _sds(shape, dtype):
    return jax.ShapeDtypeStruct(shape, dtype)


def _rowwise(name, fn, steps, ins, in_specs, outs, out_specs, acc):
    n_in = len(ins)

    def body(*refs):
        res = fn(*[r[...] for r in refs[:n_in]])
        res = res if isinstance(res, (tuple, list)) else (res,)
        for r, o, a in zip(refs[n_in:], res, acc):
            if a:
                @pl.when(pl.program_id(0) == 0)
                def _():
                    r[...] = jnp.zeros_like(r)
                r[...] += o
            else:
                r[...] = o.astype(r.dtype)

    return _pcall(body, name=name, grid=(steps,), in_specs=in_specs, out_specs=out_specs, out_shape=outs,
                  sem=("arbitrary",))(*ins)


def _row(tm, c, blk=0):
    return pl.BlockSpec((tm, c), lambda i: (i, blk))


def _vec(c, blk=0):
    return pl.BlockSpec((1, c), lambda i: (0, blk))


def _rms(x, g):
    return x * lax.rsqrt(jnp.mean(x * x, axis=-1, keepdims=True) + EPS) * g


def _norm_mod(x, g, sc, sh):
    return _rms(x, g) * (1.0 + sc) + sh


def _head_rms(t, g):
    return jnp.concatenate([_rms(t[:, h * HEAD:(h + 1) * HEAD], g) for h in range(t.shape[1] // HEAD)], axis=1)


def _mix_fn(attn, yg, z, bglu, ga, gs):
    ssm = yg * jax.nn.sigmoid(z + bglu)
    return jnp.concatenate([_rms(attn, ga), _rms(ssm, gs)], axis=1)


def _ypre_fn(ymm, u, dskip):
    return jax.nn.gelu(ymm + dskip * u)


def _matmul(name, a, b, *, dims, grid, a_spec, b_spec, acc_shape, outs, out_specs, extra=(), extra_specs=(),
            epilogue=None, after=None):
    gk = grid[2]
    n_x = len(extra)
    placed = [] if after is None else [after]
    first_out = n_x + len(placed)
    ins = [a, b, *extra, *placed]
    in_specs = [a_spec, b_spec, *extra_specs] + [pl.BlockSpec(memory_space=pl.ANY)] * len(placed)

    def product(a_ref, b_ref):
        return lax.dot_general(a_ref[...].astype(bf16), b_ref[...].astype(bf16), (dims, ((), ())),
                               preferred_element_type=f32)

    def finish(res, x_refs, o_refs):
        res = epilogue(res, *[r[...] for r in x_refs]) if epilogue is not None else (res,)
        for r, o in zip(o_refs, res):
            r[...] = o.astype(r.dtype)

    def body_single(a_ref, b_ref, *rest):
        finish(product(a_ref, b_ref), rest[:n_x], rest[first_out:])

    def body_pair(a_ref, b_ref, *rest):
        acc = rest[-1]
        prod = product(a_ref, b_ref)

        @pl.when(pl.program_id(2) == 0)
        def _():
            acc[...] = prod

        @pl.when(pl.program_id(2) == 1)
        def _():
            finish(acc[...] + prod, rest[:n_x], rest[first_out:-1])

    def body(a_ref, b_ref, *rest):
        acc = rest[-1]
        k = pl.program_id(2)

        @pl.when(k == 0)
        def _():
            acc[...] = jnp.zeros_like(acc)

        acc[...] += product(a_ref, b_ref)

        @pl.when(k == gk - 1)
        def _():
            finish(acc[...], rest[:n_x], rest[first_out:-1])

    sem = ("parallel", "parallel", "arbitrary")
    if gk == 1:
        return _pcall(body_single, name=name, grid=grid, in_specs=in_specs, out_specs=out_specs, out_shape=outs,
                      sem=sem)(*ins)
    return _pcall(body_pair if gk == 2 else body, name=name, grid=grid, in_specs=in_specs, out_specs=out_specs,
                  out_shape=outs, scratch_shapes=[pltpu.VMEM(acc_shape, f32)], sem=sem)(*ins)


NN = ((1,), (0,))
NT = ((1,), (1,))
TN = ((0,), (0,))


def _mm_nn(name, a, b, out_dtype=f32, tm=1024, tn=1024, tk=2048, epilogue=None, extra=(), outs=None):
    m, kd = a.shape
    n = b.shape[1]
    tm, tn, tk = _tile(m, tm), _tile(n, tn), _tile(kd, tk)
    o_spec = pl.BlockSpec((tm, tn), lambda i, j, k: (i, j))
    outs = outs if outs is not None else [_sds((m, n), out_dtype)]
    return _matmul(name, a, b, dims=NN, grid=(m // tm, n // tn, kd // tk),
                   a_spec=pl.BlockSpec((tm, tk), lambda i, j, k: (i, k)),
                   b_spec=pl.BlockSpec((tk, tn), lambda i, j, k: (k, j)),
                   acc_shape=(tm, tn), outs=outs, out_specs=[o_spec] * len(outs),
                   extra=extra, extra_specs=[o_spec] * len(extra), epilogue=epilogue)


def _mm_nn_sharded(name, a, b3, out_dtype=f32, tm=1024, tk=2048, epilogue=None, outs=None):
    m, kd = a.shape
    nsh, _, n = b3.shape
    tm, tk = _tile(m, tm), _tile(kd, tk)
    o_spec = pl.BlockSpec((tm, n), lambda i, j, k: (i, j))
    outs = outs if outs is not None else [_sds((m, nsh * n), out_dtype)]
    return _matmul(name, a, b3, dims=NN, grid=(m // tm, nsh, kd // tk),
                   a_spec=pl.BlockSpec((tm, tk), lambda i, j, k: (i, k)),
                   b_spec=pl.BlockSpec((None, tk, n), lambda i, j, k: (j, k, 0)),
                   acc_shape=(tm, n), outs=outs, out_specs=[o_spec] * len(outs), epilogue=epilogue)


def _mm_nt(name, a, b, out_dtype=f32, tm=1024, tn=1024, tk=2048, epilogue=None, extra=(), outs=None, after=None):
    m, kd = a.shape
    n = b.shape[0]
    tm, tn, tk = _tile(m, tm), _tile(n, tn), _tile(kd, tk)
    o_spec = pl.BlockSpec((tm, tn), lambda i, j, k: (i, j))
    outs = outs if outs is not None else [_sds((m, n), out_dtype)]
    return _matmul(name, a, b, dims=NT, grid=(m // tm, n // tn, kd // tk),
                   a_spec=pl.BlockSpec((tm, tk), lambda i, j, k: (i, k)),
                   b_spec=pl.BlockSpec((tn, tk), lambda i, j, k: (j, k)),
                   acc_shape=(tm, tn), outs=outs, out_specs=[o_spec] * len(outs),
                   extra=extra, extra_specs=[o_spec] * len(extra), epilogue=epilogue, after=after)


def _mm_tn(name, a, b, out_dtype=bf16, tm=1024, tn=1024, tk=2048, after=None):
    t, m = a.shape
    n = b.shape[1]
    tm, tn, tk = _tile(m, tm), _tile(n, tn), _tile(t, tk)
    return _matmul(name, a, b, dims=TN, grid=(m // tm, n // tn, t // tk),
                   a_spec=pl.BlockSpec((tk, tm), lambda i, j, k: (k, i)),
                   b_spec=pl.BlockSpec((tk, tn), lambda i, j, k: (k, j)),
                   acc_shape=(tm, tn), outs=[_sds((m, n), out_dtype)],
                   out_specs=[pl.BlockSpec((tm, tn), lambda i, j, k: (i, j))], after=after)[0]


def _mm_tn_sharded(name, a, b, nsh, out_dtype=bf16, tm=1024, tk=2048):
    t, m = a.shape
    n = b.shape[1] // nsh
    tm, tk = _tile(m, tm), _tile(t, tk)
    return _matmul(name, a, b, dims=TN, grid=(m // tm, nsh, t // tk),
                   a_spec=pl.BlockSpec((tk, tm), lambda i, j, k: (k, i)),
                   b_spec=pl.BlockSpec((tk, n), lambda i, j, k: (k, j)),
                   acc_shape=(tm, n), outs=[_sds((nsh, m, n), out_dtype)],
                   out_specs=[pl.BlockSpec((None, tm, n), lambda i, j, k: (j, i, 0))])[0]


SCAN_CHAINS = 8


def _scan_segments(s_len):
    nch = SCAN_CHAINS
    while s_len % (8 * nch) or (s_len // (8 * nch)) & (s_len // (8 * nch) - 1):
        nch //= 2
    return 8 * nch


def _to_segments(t, nseg):
    s_len, c = t.shape
    return t.reshape(nseg, s_len // nseg, c).transpose(1, 0, 2).reshape(s_len, c)


def _from_segments(t, nseg):
    s_len, c = t.shape
    return t.reshape(s_len // nseg, nseg, c).transpose(1, 0, 2).reshape(s_len, c)


def _lane_block_weights(t3, n_state):
    n_groups = t3.shape[0]
    gpl = LANE // n_state
    per = LANE // (gpl * SSM_GROUP)
    n_lb = n_groups // gpl
    t5 = t3.reshape(n_lb // per, per, gpl, SSM_GROUP, n_state)
    w = jnp.einsum("aqgic,gh,qs->aqsgihc", t5, jnp.eye(gpl, dtype=t3.dtype), jnp.eye(per, dtype=t3.dtype))
    return w.reshape(n_lb, LANE, LANE).astype(bf16)


def _lane_block_diag(w, n_state):
    gpl = LANE // n_state
    per = LANE // (gpl * SSM_GROUP)
    n_lb = w.shape[0]
    w7 = w.reshape(n_lb // per, per, per, gpl, SSM_GROUP, gpl, n_state)
    t5 = jnp.einsum("aqsgihc,gh,qs->aqgic", w7, jnp.eye(gpl, dtype=w.dtype), jnp.eye(per, dtype=w.dtype))
    return t5.reshape(n_lb * gpl, SSM_GROUP, n_state)


def _scan(name, src, w_in, w_out, a2, *, reverse, adjoint_of=None):
    s_len, n_ch = src.shape
    gp = a2.shape[1]
    per = (gp // LANE) // (n_ch // LANE)
    nseg = _scan_segments(s_len)
    nch = nseg // 8
    seg = s_len // nseg
    n_sq = int(math.log2(seg))
    assert 2 ** n_sq == seg
    adj = adjoint_of is not None
    chunk = _tile(s_len, 1024)
    n_chunks = s_len // chunk

    def body(*refs):
        it = iter(refs)
        src_ref, wi_ref, wo_ref, a_ref = (next(it) for _ in range(4))
        if adj:
            ut_ref, dyt_ref, hr_ref, hi_ref, hin_ref = (next(it) for _ in range(5))
        res_ref = next(it)
        if adj:
            da_ref, dbr_ref, dbi_ref, dcr_ref, dci_ref = (next(it) for _ in range(5))
        else:
            or_ref, oi_ref, oin_ref = (next(it) for _ in range(3))
        if adj:
            or_ref, oi_ref = next(it), next(it)

        for i in range(n_chunks):
            x2 = jnp.dot(src_ref[i * chunk:(i + 1) * chunk, :], wi_ref[...], preferred_element_type=f32)
            or_ref[i * chunk:(i + 1) * chunk, :] = x2[:, :LANE]
            oi_ref[i * chunk:(i + 1) * chunk, :] = x2[:, LANE:]

        ar = a_ref[0:1, :]
        ai = -a_ref[1:2, :] if reverse else a_ref[1:2, :]
        arb, aib = jnp.broadcast_to(ar, (8, LANE)), jnp.broadcast_to(ai, (8, LANE))

        def rows(ch, k):
            return pl.ds(pl.multiple_of(k * nseg + ch * 8, 8), 8)

        def advance(h, ch, k):
            hr, hi = h
            return (arb * hr - aib * hi + or_ref[rows(ch, k), :], arb * hi + aib * hr + oi_ref[rows(ch, k), :])

        def kk(n):
            return seg - 1 - n if reverse else n

        zero = jnp.zeros((8, LANE), f32)

        def sweep1(n, hs):
            return tuple(advance(hs[ch], ch, kk(n)) for ch in range(nch))

        ends = lax.fori_loop(0, seg, sweep1, tuple((zero, zero) for _ in range(nch)))

        pr, pi = ar, ai
        for _ in range(n_sq):
            pr, pi = pr * pr - pi * pi, 2.0 * pr * pi
        in_r, in_i = [None] * nseg, [None] * nseg
        cr = ci = jnp.zeros((1, LANE), f32)
        for j in (range(nseg - 1, -1, -1) if reverse else range(nseg)):
            in_r[j], in_i[j] = cr, ci
            er, ei = ends[j // 8][0][j % 8:j % 8 + 1, :], ends[j // 8][1][j % 8:j % 8 + 1, :]
            cr, ci = er + pr * cr - pi * ci, ei + pr * ci + pi * cr
        h0 = tuple((jnp.concatenate(in_r[8 * ch:8 * ch + 8], axis=0), jnp.concatenate(in_i[8 * ch:8 * ch + 8], axis=0))
                   for ch in range(nch))
        if not adj:
            for ch in range(nch):
                oin_ref[0, 8 * ch:8 * ch + 8, :] = h0[ch][0]
                oin_ref[1, 8 * ch:8 * ch + 8, :] = h0[ch][1]

        def emit(ch, k, h):
            or_ref[rows(ch, k), :] = h[0]
            oi_ref[rows(ch, k), :] = h[1]

        def pair(h, p):
            return h[0] * p[0] + h[1] * p[1], h[1] * p[0] - h[0] * p[1]

        def sweep2(n, carry):
            k = kk(n)
            new = tuple(advance(carry[ch], ch, k) for ch in range(nch))
            for ch in range(nch):
                emit(ch, k, new[ch])
            if not adj:
                return new
            dr, di = carry[nch]
            for ch in range(nch):
                qr, qi = pair(new[ch], (hr_ref[rows(ch, k - 1), :], hi_ref[rows(ch, k - 1), :]))
                dr, di = dr + qr, di + qi
            return new + ((dr, di),)

        if adj:
            carry = lax.fori_loop(0, seg - 1, sweep2, h0 + ((zero, zero),))
            dr, di = carry[nch]
            for ch in range(nch):
                new = advance(carry[ch], ch, 0)
                emit(ch, 0, new)
                qr, qi = pair(new, (hin_ref[0, 8 * ch:8 * ch + 8, :], hin_ref[1, 8 * ch:8 * ch + 8, :]))
                dr, di = dr + qr, di + qi
            da_ref[0] = jnp.sum(dr, axis=0, keepdims=True)
            da_ref[1] = jnp.sum(di, axis=0, keepdims=True)
        else:
            lax.fori_loop(0, seg, sweep2, h0)

        first = pl.program_id(0) % per == 0
        for i in range(n_chunks):
            sl = slice(i * chunk, (i + 1) * chunk)
            h2 = jnp.concatenate([or_ref[sl, :], oi_ref[sl, :]], axis=1).astype(bf16)
            part = lax.dot_general(h2, wo_ref[...], (NT, ((), ())), preferred_element_type=f32)

            @pl.when(first)
            def _():
                res_ref[sl, :] = part

            @pl.when(jnp.logical_not(first))
            def _():
                res_ref[sl, :] += part

        if adj:
            def over_time(xt_ref, yr_ref, yi_ref):
                tot = jnp.zeros((LANE, 2 * LANE), f32)
                for i in range(n_chunks):
                    sl = slice(i * chunk, (i + 1) * chunk)
                    y2 = jnp.concatenate([yr_ref[sl, :], yi_ref[sl, :]], axis=1).astype(bf16)
                    tot += jnp.dot(xt_ref[:, sl], y2, preferred_element_type=f32)
                return tot[:, :LANE], tot[:, LANE:]

            dbr_ref[...], dbi_ref[...] = over_time(ut_ref, or_ref, oi_ref)
            dcr_ref[...], dci_ref[...] = over_time(dyt_ref, hr_ref, hi_ref)

    col = pl.BlockSpec((s_len, LANE), lambda l: (0, l))
    chan = pl.BlockSpec((s_len, LANE), lambda l: (0, l // per))
    in_spec = pl.BlockSpec((2, nseg, LANE), lambda l: (0, 0, l))
    w_spec = pl.BlockSpec((None, LANE, LANE), lambda l: (l, 0, 0))
    w2_spec = pl.BlockSpec((None, LANE, 2 * LANE), lambda l: (l, 0, 0))
    ins = [src, jnp.concatenate(w_in, axis=2), jnp.concatenate(w_out, axis=2), a2]
    in_specs = [chan, w2_spec, w2_spec, pl.BlockSpec((8, LANE), lambda l: (0, l))]
    outs, out_specs = [_sds((s_len, n_ch), f32)], [chan]
    scratch = []
    if adj:
        ins += list(adjoint_of)
        chan_t = pl.BlockSpec((LANE, s_len), lambda l: (l // per, 0))
        in_specs += [chan_t, chan_t, col, col, in_spec]
        outs += [_sds((2, 1, gp), f32)] + [_sds((gp // LANE, LANE, LANE), f32)] * 4
        out_specs += [pl.BlockSpec((2, 1, LANE), lambda l: (0, 0, l))] + [w_spec] * 4
        scratch = [pltpu.VMEM((s_len, LANE), f32)] * 2
    else:
        outs += [_sds((s_len, gp), f32)] * 2 + [_sds((2, nseg, gp), f32)]
        out_specs += [col, col, in_spec]
    res = _pcall(body, name=name, grid=(gp // LANE,), in_specs=in_specs, out_specs=out_specs, out_shape=outs,
                 scratch_shapes=scratch, sem=("arbitrary",))(*ins)
    if adj:
        return res[0], res[1], (res[2], res[3]), (res[4], res[5])
    return res


def _ssm_param_fn(lam_re, lam_im, log_step, b_re2, b_im2, expand):
    step = jnp.exp(log_step)
    xr, xi = lam_re * step, lam_im * step
    mag = jnp.exp(xr)
    ar, ai = mag * jnp.cos(xi), mag * jnp.sin(xi)
    nr, ni = ar - 1.0, ai
    den = lam_re * lam_re + lam_im * lam_im
    cr = (nr * lam_re + ni * lam_im) / den
    ci = (ni * lam_re - nr * lam_im) / den
    cre = jnp.dot(cr, expand, precision=HI, preferred_element_type=f32)
    cie = jnp.dot(ci, expand, precision=HI, preferred_element_type=f32)
    return ar, ai, cre * b_re2 - cie * b_im2, cre * b_im2 + cie * b_re2


def _ssm_params(lam_re, lam_im, log_step, b_re2, b_im2, expand):
    def body(*refs):
        res = _ssm_param_fn(*[r[...] for r in refs[:6]])
        for r, o in zip(refs[6:], res):
            r[...] = o

    g, p = lam_re.shape
    return _pcall(body, name="ssm_params", out_shape=[_sds((g, p), f32)] * 2 + [_sds(b_re2.shape, f32)] * 2)(
        lam_re, lam_im, log_step, b_re2, b_im2, expand)


def _ssm_params_bwd(lam_re, lam_im, log_step, b_re2, b_im2, expand, d_ar, d_ai, d_bbr, d_bbi):
    def body(*refs):
        prim = [r[...] for r in refs[:5]]
        ex = refs[5][...]
        cot = tuple(r[...] for r in refs[6:10])
        _, vjp = jax.vjp(lambda *p_: _ssm_param_fn(*p_, ex), *prim)
        for r, o in zip(refs[10:], vjp(cot)):
            r[...] = o

    shapes = [lam_re.shape, lam_im.shape, log_step.shape, b_re2.shape, b_im2.shape]
    return _pcall(body, name="ssm_params_bwd", out_shape=[_sds(s, f32) for s in shapes])(
        lam_re, lam_im, log_step, b_re2, b_im2, expand, d_ar, d_ai, d_bbr, d_bbi)


def _slope_table(n_heads):
    s = 2.0 ** (-8.0 * (jnp.arange(n_heads, dtype=f32) + 1.0) / n_heads)
    return jnp.broadcast_to(s[:, None, None], (n_heads, 1, LANE))


def _band_bias(slope_d, shift):
    qi = lax.broadcasted_iota(jnp.int32, (BAND, BAND), 0)
    ki = lax.broadcasted_iota(jnp.int32, (BAND, BAND), 1)
    mask = (ki >= qi) if shift else (ki <= qi)
    return jnp.where(mask, -slope_d * (qi - ki + shift).astype(f32), NEG)


def _window_bias(slope_d, has_prev):
    own = _band_bias(slope_d, 0)
    mid = jnp.concatenate([_band_bias(slope_d, BAND), own], axis=1)
    none = jnp.concatenate([jnp.full((BAND, BAND), NEG, f32), own], axis=1)
    return mid, jnp.where(has_prev, mid, none)


def _window_scores(q, k2, bias):
    return lax.dot_general(q, k2, (NT, ((), ())), preferred_element_type=f32) * (HEAD ** -0.5) + bias


def _attn_geometry(s_len, dil, rows=1024):
    piece = BAND * dil
    m = max(1, rows // piece)
    while s_len % (piece * m):
        m //= 2
    return m, piece


def _stream_rows(start, dil):
    return pl.ds(start, BAND, stride=dil) if dil > 1 else pl.ds(start, BAND)


def _attn_fwd(qn, kn, proj, v_blk, dil, slopes):
    s_len, aw = qn.shape
    n_heads = aw // HEAD
    m, piece = _attn_geometry(s_len, dil, 2048)
    rows = m * piece

    def body(q_ref, k_ref, kp_ref, v_ref, vp_ref, sl_ref, o_ref, lse_ref):
        bias_mid, bias_first = _window_bias(sl_ref[:, 0:1] * float(dil), pl.program_id(1) > 0)
        for b in range(m):
            for r in range(dil):
                idx = _stream_rows(b * piece + r, dil)
                q, kc, vc = (ref[idx, :].astype(bf16) for ref in (q_ref, k_ref, v_ref))
                if b:
                    pidx = _stream_rows((b - 1) * piece + r, dil)
                    kp, vp = k_ref[pidx, :].astype(bf16), v_ref[pidx, :].astype(bf16)
                else:
                    pidx = _stream_rows(r, dil)
                    kp, vp = kp_ref[pidx, :].astype(bf16), vp_ref[pidx, :].astype(bf16)
                k2, v2 = jnp.concatenate([kp, kc], axis=0), jnp.concatenate([vp, vc], axis=0)
                s = _window_scores(q, k2, bias_mid if b else bias_first)
                mx = jnp.max(s, axis=1, keepdims=True)
                p = jnp.exp(s - mx)
                den = jnp.sum(p, axis=1, keepdims=True)
                o_ref[idx, :] = jnp.dot(p.astype(bf16), v2, preferred_element_type=f32) / den
                lse_ref[idx, :] = jnp.broadcast_to(mx + jnp.log(den), (BAND, HEAD))

    def cur(blk0):
        return pl.BlockSpec((rows, HEAD), lambda h, t: (t, blk0 + h))

    def prev(blk0):
        return pl.BlockSpec((piece, HEAD), lambda h, t: (jnp.maximum(t * m - 1, 0), blk0 + h))

    sl = pl.BlockSpec((None, 1, LANE), lambda h, t: (h, 0, 0))
    return _pcall(body, name=f"attn_fwd_d{dil}", grid=(n_heads, s_len // rows),
                  in_specs=[cur(0), cur(0), prev(0), cur(v_blk), prev(v_blk), sl], out_specs=[cur(0), cur(0)],
                  out_shape=[_sds((s_len, aw), f32)] * 2, sem=("parallel", "parallel"))(
        qn, kn, kn, proj, proj, slopes)


def _attn_bwd(qn, kn, proj, v_blk, do, lse, dd, dil, slopes, acc=None):
    s_len, aw = qn.shape
    n_heads = aw // HEAD
    m, piece = _attn_geometry(s_len, dil, max(1024, 2 * BAND * dil))
    rows = m * piece
    n_tiles = s_len // rows
    scale = HEAD ** -0.5

    def body(q_ref, qx_ref, k_ref, kp_ref, v_ref, vp_ref, do_ref, dox_ref, l_ref, lx_ref, d_ref, dx_ref, sl_ref,
             *rest):
        dq_ref, dk_ref, dv_ref = rest[-3:]

        def put(ref, which, idx, val):
            ref[idx, :] = val if acc is None else val + rest[which][idx, :]

        t = pl.program_id(1)
        slope_d = sl_ref[:, 0:1] * float(dil)
        bias_mid, bias_first = _window_bias(slope_d, t > 0)
        bias_next = _band_bias(slope_d, BAND)

        def query_side(ref_q, ref_do, ref_l, ref_d, idx):
            return (ref_q[idx, :].astype(bf16), ref_do[idx, :].astype(bf16), ref_l[idx, :][:, 0:1],
                    ref_d[idx, :][:, 0:1])

        def probs(qs, keys, values, bias):
            q, do_, l_col, d_col = qs
            p = jnp.exp(_window_scores(q, keys, bias) - l_col)
            dp = lax.dot_general(do_, values, (NT, ((), ())), preferred_element_type=f32)
            return p.astype(bf16), (p * (dp - d_col)).astype(bf16)

        def tn(a_, b_):
            return lax.dot_general(a_, b_, (TN, ((), ())), preferred_element_type=f32)

        for r in range(dil):
            pend = None
            for b in range(m):
                idx = _stream_rows(b * piece + r, dil)
                qs = query_side(q_ref, do_ref, l_ref, d_ref, idx)
                kc, vc = k_ref[idx, :].astype(bf16), v_ref[idx, :].astype(bf16)
                if b:
                    kp, vp = kc_prev, vc_prev
                else:
                    pidx = _stream_rows(r, dil)
                    kp, vp = kp_ref[pidx, :].astype(bf16), vp_ref[pidx, :].astype(bf16)
                k2, v2 = jnp.concatenate([kp, kc], axis=0), jnp.concatenate([vp, vc], axis=0)
                p, ds = probs(qs, k2, v2, bias_mid if b else bias_first)
                put(dq_ref, 0, idx, jnp.dot(ds, k2, preferred_element_type=f32) * scale)
                dk2, dv2 = tn(ds, qs[0]), tn(p, qs[1])
                if pend is not None:
                    put(dk_ref, 1, pend[0], (pend[1] + dk2[:BAND]) * scale)
                    put(dv_ref, 2, pend[0], pend[2] + dv2[:BAND])
                pend = (idx, dk2[BAND:], dv2[BAND:])
                kc_prev, vc_prev = kc, vc
            qs = query_side(qx_ref, dox_ref, lx_ref, dx_ref, _stream_rows(r, dil))
            p, ds = probs(qs, kc_prev, vc_prev, bias_next)
            live = t < n_tiles - 1
            put(dk_ref, 1, pend[0], (pend[1] + jnp.where(live, tn(ds, qs[0]), 0.0)) * scale)
            put(dv_ref, 2, pend[0], pend[2] + jnp.where(live, tn(p, qs[1]), 0.0))

    def cur(blk0):
        return pl.BlockSpec((rows, HEAD), lambda h, t: (t, blk0 + h))

    def prev(blk0):
        return pl.BlockSpec((piece, HEAD), lambda h, t: (jnp.maximum(t * m - 1, 0), blk0 + h))

    def nxt(blk0):
        return pl.BlockSpec((piece, HEAD), lambda h, t: (jnp.minimum(t * m + m, n_tiles * m - 1), blk0 + h))

    sl = pl.BlockSpec((None, 1, LANE), lambda h, t: (h, 0, 0))
    in_specs = [cur(0), nxt(0), cur(0), prev(0), cur(v_blk), prev(v_blk), cur(0), nxt(0), cur(0), nxt(0), cur(0),
                nxt(0), sl]
    ins = [qn, qn, kn, kn, proj, proj, do, do, lse, lse, dd, dd, slopes]
    extra = {}
    if acc is not None:
        extra = dict(input_output_aliases={len(ins) + i: i for i in range(3)})
        in_specs += [cur(0)] * 3
        ins += list(acc)
    return _pcall(body, name=f"attn_bwd_d{dil}", grid=(n_heads, n_tiles), in_specs=in_specs,
                  out_specs=[cur(0)] * 3, out_shape=[_sds((s_len, aw), f32)] * 3,
                  sem=("parallel", "parallel"), **extra)(*ins)


def _exchange(name, srcs, scatter):
    n = len(srcs)

    def body(*refs):
        src, out = refs[:n], refs[n:2 * n]
        send_sems, recv_sems, local_sems = refs[2 * n:]
        x, y, c = lax.axis_index("x"), lax.axis_index("y"), lax.axis_index("c")
        me = 4 * x + 2 * y + c

        def peer(r):
            return ((1 - x) if r & 4 else x, (1 - y) if r & 2 else y, (1 - c) if r & 1 else c)

        def lin(p):
            return 4 * p[0] + 2 * p[1] + p[2]

        def piece(a, idx):
            return src[a].at[idx] if scatter[a] else src[a]

        local, sends = [], []
        for a in range(n):
            cp = pltpu.make_async_copy(piece(a, me), out[a].at[me], local_sems.at[a])
            cp.start()
            local.append(cp)
        for r in range(1, N_DEV):
            p = peer(r)
            for a in range(n):
                cp = pltpu.make_async_remote_copy(src_ref=piece(a, lin(p)), dst_ref=out[a].at[me],
                                                  send_sem=send_sems.at[a, r - 1], recv_sem=recv_sems.at[a, r - 1],
                                                  device_id=p, device_id_type=MESH)
                cp.start()
                sends.append(cp)
        for r in range(1, N_DEV):
            p = peer(r)
            for a in range(n):
                pltpu.make_async_remote_copy(src_ref=piece(a, lin(p)), dst_ref=out[a].at[lin(p)],
                                             send_sem=send_sems.at[a, r - 1], recv_sem=recv_sems.at[a, r - 1],
                                             device_id=p, device_id_type=MESH).wait_recv()
        for cp in sends:
            cp.wait_send()
        for cp in local:
            cp.wait()

    def piece_shape(a):
        return srcs[a].shape[1:] if scatter[a] else srcs[a].shape

    any_spec = pl.BlockSpec(memory_space=pl.ANY)
    return _pcall(body, name=name, in_specs=[any_spec] * n, out_specs=[any_spec] * n,
                  out_shape=[_sds((N_DEV, *piece_shape(a)), srcs[a].dtype) for a in range(n)],
                  scratch_shapes=[pltpu.SemaphoreType.DMA((n, N_DEV - 1)), pltpu.SemaphoreType.DMA((n, N_DEV - 1)),
                                  pltpu.SemaphoreType.DMA((n,))])(*srcs)


_HBM = pl.BlockSpec(memory_space=pltpu.HBM)
_SEM = pl.BlockSpec(memory_space=pltpu.SEMAPHORE)
_EFFECT = pltpu.SideEffectType.DATAFLOW_SIDE_EFFECTING


def _peer_ids():
    x, y, c = lax.axis_index("x"), lax.axis_index("y"), lax.axis_index("c")
    peers = [((1 - x) if r & 4 else x, (1 - y) if r & 2 else y, (1 - c) if r & 1 else c) for r in range(1, N_DEV)]
    return 4 * x + 2 * y + c, peers, [4 * p[0] + 2 * p[1] + p[2] for p in peers]


def _exchange_start(name, src, scatter, after):
    piece_shape = src.shape[1:] if scatter else src.shape

    def body(src_ref, land_ref, after_ref, send_sems, recv_sems, local_sem, src_thru, land_thru, token):
        me, peers, lins = _peer_ids()

        def piece(idx):
            return src_ref.at[idx] if scatter else src_ref

        pltpu.make_async_copy(piece(me), land_ref.at[me], local_sem).start()
        for r, (p, lp) in enumerate(zip(peers, lins)):
            pltpu.make_async_remote_copy(src_ref=piece(lp), dst_ref=land_ref.at[me], send_sem=send_sems.at[r],
                                         recv_sem=recv_sems.at[r], device_id=p, device_id_type=MESH).start()
        token[...] = jnp.zeros_like(token)

    land = pltpu.with_memory_space_constraint(lax.empty((N_DEV, *piece_shape), src.dtype), pltpu.HBM)
    send_sems, recv_sems, local_sem, src_thru, land_thru, token = pl.pallas_call(
        body, name=name,
        out_shape=(pltpu.SemaphoreType.DMA((N_DEV - 1,)), pltpu.SemaphoreType.DMA((N_DEV - 1,)),
                   pltpu.SemaphoreType.DMA(()), pltpu.HBM(src.shape, src.dtype),
                   pltpu.HBM((N_DEV, *piece_shape), src.dtype), _sds((8, LANE), f32)),
        in_specs=(_HBM, _HBM, pl.BlockSpec(memory_space=pl.ANY)),
        out_specs=(_SEM, _SEM, _SEM, _HBM, _HBM, pl.BlockSpec(memory_space=pltpu.VMEM)),
        input_output_aliases={0: 3, 1: 4},
        compiler_params=pltpu.CompilerParams(has_side_effects=_EFFECT),
    )(pltpu.with_memory_space_constraint(src, pltpu.HBM), land, after)
    return (send_sems, recv_sems, local_sem, src_thru, land_thru, scatter), token


def _exchange_wait(name, handle, *after):
    send_sems, recv_sems, local_sem, src_thru, land_thru, scatter = handle

    def body(src_ref, land_ref, send_sems_, recv_sems_, local_sem_, *rest):
        me, peers, lins = _peer_ids()

        def piece(idx):
            return src_ref.at[idx] if scatter else src_ref

        pltpu.make_async_copy(piece(me), land_ref.at[me], local_sem_).wait()
        for r, (p, lp) in enumerate(zip(peers, lins)):
            pltpu.make_async_remote_copy(src_ref=piece(lp), dst_ref=land_ref.at[me], send_sem=send_sems_.at[r],
                                         recv_sem=recv_sems_.at[r], device_id=p, device_id_type=MESH).wait_send()
            pltpu.make_async_remote_copy(src_ref=piece(lp), dst_ref=land_ref.at[lp], send_sem=send_sems_.at[r],
                                         recv_sem=recv_sems_.at[r], device_id=p, device_id_type=MESH).wait_recv()

    return pl.pallas_call(
        body, name=name,
        out_shape=(pltpu.HBM(src_thru.shape, src_thru.dtype), pltpu.HBM(land_thru.shape, land_thru.dtype)),
        in_specs=(_HBM, _HBM, _SEM, _SEM, _SEM, *[pl.BlockSpec(memory_space=pl.ANY)] * len(after)),
        out_specs=(_HBM, _HBM), input_output_aliases={0: 0, 1: 1},
        compiler_params=pltpu.CompilerParams(has_side_effects=_EFFECT),
    )(src_thru, land_thru, send_sems, recv_sems, local_sem, *after)[1]


def _adamw(name, w, m, v, g_or_stack, stacked, rows=256):
    r, c = w.shape
    tr = _tile(r, rows)

    def fn(w_, m_, v_, g_):
        if stacked:
            g = g_[0].astype(f32)
            for j in range(1, N_DEV):
                g = g + g_[j].astype(f32)
        else:
            g = g_
        m_new = ADAM_B1 * m_ + (1.0 - ADAM_B1) * g
        v_new = ADAM_B2 * v_ + (1.0 - ADAM_B2) * (g * g)
        m_hat = m_new / (1.0 - ADAM_B1 ** ADAM_STEP)
        v_hat = v_new / (1.0 - ADAM_B2 ** ADAM_STEP)
        delta = -ADAM_LR * (m_hat / (jnp.sqrt(v_hat) + ADAM_EPS) + ADAM_WD * w_)
        return g, delta, m_new, v_new

    blk = _row(tr, c)
    g_spec = pl.BlockSpec((N_DEV, tr, c), lambda i: (0, i, 0)) if stacked else blk
    return _rowwise(name, fn, r // tr, [w, m, v, g_or_stack], [blk, blk, blk, g_spec],
                    [_sds((r, c), f32)] * 4, [blk] * 4, [False] * 4)


def _ada_fwd(c_all, w_shard, b_shard):
    nb_, d = c_all.shape
    n = w_shard.shape[1]
    tn = _tile(n, 512)

    def body(c_ref, w_ref, b_ref, o_ref):
        a = jax.nn.silu(c_ref[...]).astype(bf16)
        o_ref[...] = jnp.dot(a, w_ref[...].astype(bf16), preferred_element_type=f32) + b_ref[...]

    return _pcall(body, name="ada_fwd", grid=(n // tn,),
                  in_specs=[pl.BlockSpec((nb_, d), lambda j: (0, 0)), pl.BlockSpec((d, tn), lambda j: (0, j)),
                            pl.BlockSpec((1, tn), lambda j: (0, j))],
                  out_specs=pl.BlockSpec((nb_, tn), lambda j: (0, j)), out_shape=_sds((nb_, n), f32),
                  sem=("parallel",))(c_all, w_shard, b_shard)


def _ada_bwd(c_all, dmod_cols):
    nb_, d = c_all.shape
    n = dmod_cols.shape[1]
    tn = _tile(n, 512)

    def body(c_ref, g_ref, o_ref):
        a = jax.nn.silu(c_ref[...]).astype(bf16).astype(f32)
        g = g_ref[...].astype(bf16).astype(f32)
        o_ref[...] = lax.dot_general(a, g, (TN, ((), ())), precision=HI, preferred_element_type=f32)

    return _pcall(body, name="ada_bwd", grid=(n // tn,),
                  in_specs=[pl.BlockSpec((nb_, d), lambda j: (0, 0)), pl.BlockSpec((nb_, tn), lambda j: (0, j))],
                  out_specs=pl.BlockSpec((d, tn), lambda j: (0, j)), out_shape=_sds((d, n), f32),
                  sem=("parallel",))(c_all, dmod_cols)


SMALL_LATE = ("b_ada", "norm1_g", "q_norm_g", "k_norm_g")
SMALL_EARLY = ("lam_re", "lam_im", "log_step", "b_re", "b_im", "c_re", "c_im", "d_skip", "b_glu", "attn_out_g",
               "ssm_out_g", "norm2_g")
ORDER = ("w_ada", "b_ada", "norm1_g", "w_in", "q_norm_g", "k_norm_g", "lam_re", "lam_im", "log_step", "b_re", "b_im",
         "c_re", "c_im", "d_skip", "w_glu", "b_glu", "attn_out_g", "ssm_out_g", "w_out", "norm2_g", "w_ff1", "w_ff2")


def _pack(parts):
    flat = jnp.concatenate([p.reshape(-1) for p in parts])
    pad = (-flat.shape[0]) % (8 * LANE)
    return jnp.pad(flat, (0, pad)).reshape(-1, LANE)


def kernel(x, c, w_ada, b_ada, norm1_g, w_in, q_norm_g, k_norm_g, lam_re, lam_im, log_step, b_re, b_im, c_re, c_im, d_skip, w_glu, b_glu, attn_out_g, ssm_out_g, w_out, norm2_g, w_ff1, w_ff2, loss_target, m_w_ada, m_b_ada, m_norm1_g, m_w_in, m_q_norm_g, m_k_norm_g, m_lam_re, m_lam_im, m_log_step, m_b_re, m_b_im, m_c_re, m_c_im, m_d_skip, m_w_glu, m_b_glu, m_attn_out_g, m_ssm_out_g, m_w_out, m_norm2_g, m_w_ff1, m_w_ff2, v_w_ada, v_b_ada, v_norm1_g, v_w_in, v_q_norm_g, v_k_norm_g, v_lam_re, v_lam_im, v_log_step, v_b_re, v_b_im, v_c_re, v_c_im, v_d_skip, v_w_glu, v_b_glu, v_attn_out_g, v_ssm_out_g, v_w_out, v_norm2_g, v_w_ff1, v_w_ff2):
    env = dict(locals())
    wts = {n: env[n] for n in ORDER}
    mom = {n: env["m_" + n] for n in ORDER}
    var = {n: env["v_" + n] for n in ORDER}

    xs, tgt = x[0], loss_target[0]
    s_len, d = xs.shape
    aw = d // 2
    sw = d - aw
    n_heads = aw // HEAD
    n_groups = sw // SSM_GROUP
    n_state = lam_re.shape[-1]
    gp = n_groups * n_state
    tm = _tile(s_len, 256)
    steps = s_len // tm
    me = 4 * lax.axis_index("x") + 2 * lax.axis_index("y") + lax.axis_index("c")

    (c_all,) = _exchange("gather_c", [c], [False])
    c_all = c_all.reshape(N_DEV, d)

    n_ada = w_ada.shape[-1]
    b_ada_cols = lax.dynamic_slice_in_dim(b_ada, me * n_ada, n_ada, axis=1)
    mod_part = _ada_fwd(c_all, w_ada[0], b_ada_cols)
    (mod_all,) = _exchange("gather_mod", [mod_part], [False])
    mod = lax.dynamic_index_in_dim(mod_all, me, axis=1, keepdims=False).reshape(1, 6 * d)
    sh1, sc1, g1, sh2, sc2, g2 = (mod[:, i * d:(i + 1) * d] for i in range(6))

    gather, started = {}, jnp.zeros((1, 1), f32)
    for name in ("w_in", "w_glu", "w_out", "w_ff1", "w_ff2"):
        gather[name], token = _exchange_start("gather_" + name, wts[name][0].astype(bf16), False, mod_all)
        started = started + token[0:1, 0:1]
    sc1 = sc1 + started

    (h,) = _rowwise("norm1", _norm_mod, steps, [xs, norm1_g, sc1, sh1],
                    [_row(tm, d), _vec(d), _vec(d), _vec(d)], [_sds((s_len, d), bf16)], [_row(tm, d)], [False])
    lam_re2, lam_im2 = lam_re[0], lam_im[0]
    log_step2 = log_step[0].reshape(n_groups, 1)
    b_re2 = b_re[0].reshape(n_groups, n_state * SSM_GROUP)
    b_im2 = b_im[0].reshape(n_groups, n_state * SSM_GROUP)
    expand = jnp.repeat(jnp.eye(n_state, dtype=f32), SSM_GROUP, axis=1)
    a_re, a_im, bb_re2, bb_im2 = _ssm_params(lam_re2, lam_im2, log_step2, b_re2, b_im2, expand)
    a2 = jnp.zeros((8, gp), f32).at[0].set(a_re.reshape(gp)).at[1].set(a_im.reshape(gp))
    w_bu = tuple(_lane_block_weights(t.reshape(n_groups, n_state, SSM_GROUP).transpose(0, 2, 1), n_state)
                 for t in (bb_re2, bb_im2))
    w_c = (_lane_block_weights(c_re[0], n_state), _lane_block_weights(-c_im[0], n_state))

    packed = {names: tuple(_pack([t[n] for n in names]) for t in (wts, mom, var))
              for names in (SMALL_LATE, SMALL_EARLY)}

    win_g = _exchange_wait("gathered_w_in", gather["w_in"], h, a2, *w_bu, *w_c, *packed[SMALL_LATE],
                           *packed[SMALL_EARLY])
    (proj,) = _mm_nn_sharded("in_proj", h, win_g)

    def qk_fn(q, k, gq, gk):
        return _head_rms(q, gq), _head_rms(k, gk)

    qn, kn = _rowwise("qk_norm", qk_fn, steps, [proj, proj, q_norm_g, k_norm_g],
                      [_row(tm, aw, 0), _row(tm, aw, 1), _vec(HEAD), _vec(HEAD)],
                      [_sds((s_len, aw), f32)] * 2, [_row(tm, aw)] * 2, [False] * 2)
    v_blk = 2 * aw // HEAD

    slopes = _slope_table(n_heads)
    pat = [_attn_fwd(qn, kn, proj, v_blk, dil, slopes) for _, dil in DILATIONS]

    def attn_mix_fn(o1, l1, o2, l2, o3, l3):
        m = jnp.maximum(jnp.maximum(l1, l2), l3)
        e1, e2, e3 = jnp.exp(l1 - m), jnp.exp(l2 - m), jnp.exp(l3 - m)
        tot = e1 + e2 + e3
        return (e1 * o1 + e2 * o2 + e3 * o3) / tot, m + jnp.log(tot)

    attn, lse = _rowwise("attn_mix", attn_mix_fn, steps, [t for ol in pat for t in ol], [_row(tm, aw)] * 6,
                         [_sds((s_len, aw), f32)] * 2, [_row(tm, aw)] * 2, [False] * 2)

    nseg = _scan_segments(s_len)
    u_seg = _to_segments(proj[:, 3 * aw:].astype(bf16), nseg)
    y_seg, h_re, h_im, hin_f = _scan("ssm_scan", u_seg, w_bu, w_c, a2, reverse=False)
    ymm = _from_segments(y_seg, nseg)

    u_spec = _row(tm, sw, 3 * aw // sw)
    (yg,) = _rowwise("ssm_gelu", _ypre_fn, steps, [ymm, proj, d_skip], [_row(tm, sw), u_spec, _vec(sw)],
                     [_sds((s_len, sw), f32)], [_row(tm, sw)], [False])
    wglu_g = _exchange_wait("gathered_w_glu", gather["w_glu"], yg).reshape(sw, sw)
    (z,) = _mm_nn("glu_proj", yg, wglu_g)
    (cat,) = _rowwise("mix_norm", _mix_fn, steps, [attn, yg, z, b_glu, attn_out_g, ssm_out_g],
                      [_row(tm, aw), _row(tm, sw), _row(tm, sw), _vec(sw), _vec(aw), _vec(sw)],
                      [_sds((s_len, d), bf16)], [_row(tm, d)], [False])
    wout_g = _exchange_wait("gathered_w_out", gather["w_out"], cat).reshape(d, d)
    (mixed,) = _mm_nn("out_proj", cat, wout_g)

    def res_norm2_fn(x_, mixed_, g1_, gn, sc, sh):
        x1_ = x_ + g1_ * mixed_
        return x1_, _norm_mod(x1_, gn, sc, sh)

    x1, h2 = _rowwise("norm2", res_norm2_fn, steps, [xs, mixed, g1, norm2_g, sc2, sh2],
                      [_row(tm, d), _row(tm, d)] + [_vec(d)] * 4,
                      [_sds((s_len, d), f32), _sds((s_len, d), bf16)], [_row(tm, d)] * 2, [False] * 2)

    def act_epilogue(acc):
        r = jnp.maximum(acc, 0.0)
        return r, r * r

    wff1_g = _exchange_wait("gathered_w_ff1", gather["w_ff1"], h2)
    r_ff, act = _mm_nn_sharded("ff1", h2, wff1_g, epilogue=act_epilogue,
                               outs=[_sds((s_len, 4 * d), bf16), _sds((s_len, 4 * d), bf16)])
    wff2_g = _exchange_wait("gathered_w_ff2", gather["w_ff2"], act).reshape(4 * d, d)
    (ff,) = _mm_nn("ff2", act, wff2_g)

    def loss_fn(x1_, ff_, tgt_, g2_):
        e = x1_ + g2_ * ff_ - tgt_
        dy_ = e * (1.0 / d)
        part = jnp.full((1, LANE), 0.5 / d, f32) * jnp.sum(e * e)
        return dy_, g2_ * dy_, part, jnp.sum(dy_ * ff_, axis=0, keepdims=True)

    dy, dff, loss_part, d_g2 = _rowwise(
        "loss", loss_fn, steps, [x1, ff, tgt, g2], [_row(tm, d)] * 3 + [_vec(d)],
        [_sds((s_len, d), f32), _sds((s_len, d), bf16), _sds((1, LANE), f32), _sds((1, d), f32)],
        [_row(tm, d), _row(tm, d), _vec(LANE), _vec(d)], [False, False, True, True])
    loss = lax.psum(loss_part[0, 0], ("x", "y", "c"))

    def dact_epilogue(acc, r_):
        return (acc * (2.0 * r_.astype(f32)),)

    (da,) = _mm_nt("ff2_dx", dff, wff2_g, epilogue=dact_epilogue, extra=[r_ff], outs=[_sds((s_len, 4 * d), bf16)])
    scatter = {}
    g_wff2 = _mm_tn("ff2_dw", act, dff, after=loss.reshape(1, 1)).reshape(N_DEV, 4 * d // N_DEV, d)
    scatter["w_ff2"], tok_ff2 = _exchange_start("scatter_w_ff2", g_wff2, True, loss.reshape(1, 1))
    dh2 = _mm_nt("ff1_dx", da, wff1_g.transpose(1, 0, 2).reshape(d, 4 * d))[0]
    g_wff1 = _mm_tn_sharded("ff1_dw", h2, da, N_DEV)
    scatter["w_ff1"], tok_ff1 = _exchange_start("scatter_w_ff1", g_wff1, True, started)
    norm2_g_t = norm2_g + (tok_ff2[0:1, 0:1] + tok_ff1[0:1, 0:1])

    def norm2_bwd_fn(dh2_, x1_, dy_, mixed_, gn, sc, sh, g1_):
        _, vjp = jax.vjp(_norm_mod, x1_, gn, sc, sh)
        dx, dgn, dsc, dsh = vjp(dh2_)
        dx1_ = dy_ + dx
        return dx1_, g1_ * dx1_, dgn, dsc, dsh, jnp.sum(dx1_ * mixed_, axis=0, keepdims=True)

    dx1, dmixed, d_norm2_g, d_sc2, d_sh2, d_g1 = _rowwise(
        "norm2_bwd", norm2_bwd_fn, steps, [dh2, x1, dy, mixed, norm2_g_t, sc2, sh2, g1],
        [_row(tm, d)] * 4 + [_vec(d)] * 4,
        [_sds((s_len, d), f32), _sds((s_len, d), bf16)] + [_sds((1, d), f32)] * 4,
        [_row(tm, d)] * 2 + [_vec(d)] * 4, [False, False, True, True, True, True])

    (dcat,) = _mm_nt("out_dx", dmixed, wout_g)
    g_wout = _mm_tn("out_dw", cat, dmixed).reshape(N_DEV, d // N_DEV, d)
    scatter["w_out"], tok_out = _exchange_start("scatter_w_out", g_wout, True, started)
    b_glu_t = b_glu + tok_out[0:1, 0:1]

    def mix_bwd_fn(dcat_, attn_, yg_, z_, bglu, ga, gs):
        _, vjp = jax.vjp(_mix_fn, attn_, yg_, z_, bglu, ga, gs)
        dattn_, dyg_, dz_, dbglu, dga, dgs = vjp(dcat_)
        prod = dattn_ * attn_
        dd_ = jnp.concatenate([jnp.broadcast_to(jnp.sum(prod[:, i * HEAD:(i + 1) * HEAD], axis=1, keepdims=True),
                                                (prod.shape[0], HEAD)) for i in range(n_heads)], axis=1)
        return dattn_, dd_, dyg_, dz_, dbglu, dga, dgs

    dattn, dd, dyg1, dz, d_b_glu, d_attn_out_g, d_ssm_out_g = _rowwise(
        "mix_bwd", mix_bwd_fn, steps, [dcat, attn, yg, z, b_glu_t, attn_out_g, ssm_out_g],
        [_row(tm, d), _row(tm, aw), _row(tm, sw), _row(tm, sw), _vec(sw), _vec(aw), _vec(sw)],
        [_sds((s_len, aw), f32), _sds((s_len, aw), f32), _sds((s_len, sw), f32), _sds((s_len, sw), bf16),
         _sds((1, sw), f32), _sds((1, aw), f32), _sds((1, sw), f32)],
        [_row(tm, aw), _row(tm, aw), _row(tm, sw), _row(tm, sw), _vec(sw), _vec(aw), _vec(sw)],
        [False] * 4 + [True] * 3)

    (dyg2,) = _mm_nt("glu_dx", dz, wglu_g)
    g_wglu = _mm_tn("glu_dw", yg, dz).reshape(N_DEV, sw // N_DEV, sw)
    scatter["w_glu"], tok_glu = _exchange_start("scatter_w_glu", g_wglu, True, started)
    d_skip_t = d_skip + tok_glu[0:1, 0:1]

    def gelu_bwd_fn(dyg1_, dyg2_, ymm_, u_, dskip):
        _, vjp = jax.vjp(_ypre_fn, ymm_, u_, dskip)
        dymm, du_, ddskip = vjp(dyg1_ + dyg2_)
        return dymm, du_, ddskip

    dymm, du_skip, d_d_skip = _rowwise(
        "ssm_gelu_bwd", gelu_bwd_fn, steps, [dyg1, dyg2, ymm, proj, d_skip_t],
        [_row(tm, sw)] * 3 + [u_spec, _vec(sw)],
        [_sds((s_len, sw), bf16), _sds((s_len, sw), f32), _sds((1, sw), f32)],
        [_row(tm, sw), _row(tm, sw), _vec(sw)], [False, False, True])

    dymm_seg = _to_segments(dymm, nseg)
    du_seg, da_seg, dbb_c, dc_c = _scan("ssm_adj", dymm_seg, w_c, w_bu, a2, reverse=True,
                                        adjoint_of=(u_seg.T, dymm_seg.T, h_re, h_im, hin_f))
    du_ssm = _from_segments(du_seg, nseg)

    def to_gpi(w):
        return _lane_block_diag(w, n_state).transpose(0, 2, 1).reshape(n_groups, n_state * SSM_GROUP)

    d_lam_re, d_lam_im, d_log_step, d_b_re2, d_b_im2 = _ssm_params_bwd(
        lam_re2, lam_im2, log_step2, b_re2, b_im2, expand,
        da_seg[0, 0].reshape(n_groups, n_state), da_seg[1, 0].reshape(n_groups, n_state),
        to_gpi(dbb_c[0]), to_gpi(dbb_c[1]))
    d_c_re = _lane_block_diag(dc_c[0], n_state)
    d_c_im = -_lane_block_diag(dc_c[1], n_state)

    grads_qkv = None
    for _, dil in reversed(DILATIONS):
        grads_qkv = _attn_bwd(qn, kn, proj, v_blk, dattn, lse, dd, dil, slopes, acc=grads_qkv)

    def qkv_bwd_fn(q, k, gq, gk, dqn, dkn, dv, du1, du2):
        _, vjp = jax.vjp(lambda q_, k_, gq_, gk_: (_head_rms(q_, gq_), _head_rms(k_, gk_)), q, k, gq, gk)
        dq, dk, dgq, dgk = vjp((dqn, dkn))
        return jnp.concatenate([dq, dk, dv, du1 + du2], axis=1), dgq, dgk

    small_g = {"lam_re": d_lam_re, "lam_im": d_lam_im, "log_step": d_log_step, "b_re": d_b_re2, "b_im": d_b_im2,
               "c_re": d_c_re, "c_im": d_c_im, "d_skip": d_d_skip, "b_glu": d_b_glu,
               "attn_out_g": d_attn_out_g, "ssm_out_g": d_ssm_out_g, "norm2_g": d_norm2_g}
    early, tok_early = _exchange_start("gather_early_grads", _pack([small_g[n] for n in SMALL_EARLY]), False, started)

    dproj, small_g["q_norm_g"], small_g["k_norm_g"] = _rowwise(
        "qk_norm_bwd", qkv_bwd_fn, steps,
        [proj, proj, q_norm_g + tok_early[0:1, 0:1], k_norm_g, *grads_qkv, du_skip, du_ssm],
        [_row(tm, aw, 0), _row(tm, aw, 1), _vec(HEAD), _vec(HEAD)] + [_row(tm, aw)] * 3 + [_row(tm, sw)] * 2,
        [_sds((s_len, 3 * aw + sw), bf16), _sds((1, HEAD), f32), _sds((1, HEAD), f32)],
        [_row(tm, 3 * aw + sw), _vec(HEAD), _vec(HEAD)], [False, True, True])

    g_win = _mm_tn_sharded("in_dw", h, dproj, N_DEV)
    scatter["w_in"], tok_in = _exchange_start("scatter_w_in", g_win, True, tok_early)
    dh = _mm_nt("in_dx", dproj, win_g.transpose(1, 0, 2).reshape(d, 3 * aw + sw), after=tok_in)[0]
    norm1_g_t = norm1_g + tok_in[0:1, 0:1]

    def norm1_bwd_fn(dh_, x_, dx1_, gn, sc, sh):
        _, vjp = jax.vjp(_norm_mod, x_, gn, sc, sh)
        dx, dgn, dsc, dsh = vjp(dh_)
        return dx1_ + dx, dgn, dsc, dsh

    grad_x, d_norm1_g, d_sc1, d_sh1 = _rowwise(
        "norm1_bwd", norm1_bwd_fn, steps, [dh, xs, dx1, norm1_g_t, sc1, sh1], [_row(tm, d)] * 3 + [_vec(d)] * 3,
        [_sds((s_len, d), f32)] + [_sds((1, d), f32)] * 3, [_row(tm, d)] + [_vec(d)] * 3,
        [False, True, True, True])

    small_g["b_ada"] = jnp.concatenate([d_sh1, d_sc1, d_g1, d_sh2, d_sc2, d_g2], axis=1)
    small_g["norm1_g"] = d_norm1_g
    (r_late,) = _exchange("gather_late_grads", [_pack([small_g[n] for n in SMALL_LATE])], [False])

    res = {}
    dmod_all = r_late.reshape(N_DEV, -1)[:, :6 * d]
    g_wada = _ada_bwd(c_all, lax.dynamic_slice_in_dim(dmod_all, me * n_ada, n_ada, axis=1))
    res["w_ada"] = _adamw("adamw_w_ada", w_ada[0], m_w_ada[0], v_w_ada[0], g_wada, False)
    after = res["w_ada"][1]
    for name in ("w_ff2", "w_ff1", "w_out", "w_glu", "w_in"):
        stack = _exchange_wait("scattered_" + name, scatter[name], after)
        res[name] = _adamw("adamw_" + name, wts[name][0], mom[name][0], var[name][0], stack, True)
        after = res[name][1]
    r_early = _exchange_wait("gathered_early_grads", early, after)
    for label, names, stack in (("late", SMALL_LATE, r_late), ("early", SMALL_EARLY, r_early)):
        small_res = _adamw("adamw_small_" + label, *packed[names], stack, True, rows=4096)
        off = 0
        for n in names:
            size = wts[n].size
            res[n] = [t.reshape(-1)[off:off + size] for t in small_res]
            off += size

    out = [loss, grad_x[None]]
    for i in range(4):
        out += [res[n][i].reshape(wts[n].shape) for n in ORDER]
    return tuple(out)
```

```python
import math

import jax
import jax.numpy as jnp
from jax import lax
from jax.experimental import pallas as pl
from jax.experimental.pallas import tpu as pltpu

f32, bf16 = jnp.float32, jnp.bfloat16

N_DEV = 8
LANE = 128
HEAD = 128
SSM_GROUP = 16
DILATIONS = ((128, 1), (512, 4), (2048, 16))
BAND = 128
EPS = 1e-6
ADAM_LR, ADAM_B1, ADAM_B2, ADAM_EPS, ADAM_WD, ADAM_STEP = 0.001, 0.9, 0.999, 1e-08, 0.01, 10
NEG = -1e30
VMEM_LIMIT = 60 * 1024 * 1024
HI = lax.Precision.HIGHEST
MESH = pl.DeviceIdType.MESH


def _pcall(body, **kw):
    sem = kw.pop("sem", None)
    kw["compiler_params"] = pltpu.CompilerParams(dimension_semantics=sem, vmem_limit_bytes=VMEM_LIMIT)
    return pl.pallas_call(body, **kw)


def _tile(n, pref):
    t = min(n, pref)
    while n % t:
        t //= 2
    return t


def _sds(shape, dtype):
    return jax.ShapeDtypeStruct(shape, dtype)


def _rowwise(name, fn, steps, ins, in_specs, outs, out_specs, acc):
    n_in = len(ins)

    def body(*refs):
        res = fn(*[r[...] for r in refs[:n_in]])
        res = res if isinstance(res, (tuple, list)) else (res,)
        for r, o, a in zip(refs[n_in:], res, acc):
            if a:
                @pl.when(pl.program_id(0) == 0)
                def _():
                    r[...] = jnp.zeros_like(r)
                r[...] += o
            else:
                r[...] = o.astype(r.dtype)

    return _pcall(body, name=name, grid=(steps,), in_specs=in_specs, out_specs=out_specs, out_shape=outs,
                  sem=("arbitrary",))(*ins)


def _row(tm, c, blk=0):
    return pl.BlockSpec((tm, c), lambda i: (i, blk))


def _vec(c, blk=0):
    return pl.BlockSpec((1, c), lambda i: (0, blk))


def _rms(x, g):
    return x * lax.rsqrt(jnp.mean(x * x, axis=-1, keepdims=True) + EPS) * g


def _norm_mod(x, g, sc, sh):
    return _rms(x, g) * (1.0 + sc) + sh


def _head_rms(t, g):
    return jnp.concatenate([_rms(t[:, h * HEAD:(h + 1) * HEAD], g) for h in range(t.shape[1] // HEAD)], axis=1)


def _mix_fn(attn, yg, z, bglu, ga, gs):
    ssm = yg * jax.nn.sigmoid(z + bglu)
    return jnp.concatenate([_rms(attn, ga), _rms(ssm, gs)], axis=1)


def _ypre_fn(ymm, u, dskip):
    return jax.nn.gelu(ymm + dskip * u)


def _matmul(name, a, b, *, dims, grid, a_spec, b_spec, acc_shape, outs, out_specs, extra=(), extra_specs=(),
            epilogue=None, after=None):
    gk = grid[2]
    n_x = len(extra)
    placed = [] if after is None else [after]
    first_out = n_x + len(placed)
    ins = [a, b, *extra, *placed]
    in_specs = [a_spec, b_spec, *extra_specs] + [pl.BlockSpec(memory_space=pl.ANY)] * len(placed)

    def product(a_ref, b_ref):
        return lax.dot_general(a_ref[...].astype(bf16), b_ref[...].astype(bf16), (dims, ((), ())),
                               preferred_element_type=f32)

    def finish(res, x_refs, o_refs):
        res = epilogue(res, *[r[...] for r in x_refs]) if epilogue is not None else (res,)
        for r, o in zip(o_refs, res):
            r[...] = o.astype(r.dtype)

    def body_single(a_ref, b_ref, *rest):
        finish(product(a_ref, b_ref), rest[:n_x], rest[first_out:])

    def body_pair(a_ref, b_ref, *rest):
        acc = rest[-1]
        prod = product(a_ref, b_ref)

        @pl.when(pl.program_id(2) == 0)
        def _():
            acc[...] = prod

        @pl.when(pl.program_id(2) == 1)
        def _():
            finish(acc[...] + prod, rest[:n_x], rest[first_out:-1])

    def body(a_ref, b_ref, *rest):
        acc = rest[-1]
        k = pl.program_id(2)

        @pl.when(k == 0)
        def _():
            acc[...] = jnp.zeros_like(acc)

        acc[...] += product(a_ref, b_ref)

        @pl.when(k == gk - 1)
        def _():
            finish(acc[...], rest[:n_x], rest[first_out:-1])

    sem = ("parallel", "parallel", "arbitrary")
    if gk == 1:
        return _pcall(body_single, name=name, grid=grid, in_specs=in_specs, out_specs=out_specs, out_shape=outs,
                      sem=sem)(*ins)
    return _pcall(body_pair if gk == 2 else body, name=name, grid=grid, in_specs=in_specs, out_specs=out_specs,
                  out_shape=outs, scratch_shapes=[pltpu.VMEM(acc_shape, f32)], sem=sem)(*ins)


NN = ((1,), (0,))
NT = ((1,), (1,))
TN = ((0,), (0,))


def _mm_nn(name, a, b, out_dtype=f32, tm=1024, tn=1024, tk=2048, epilogue=None, extra=(), outs=None):
    m, kd = a.shape
    n = b.shape[1]
    tm, tn, tk = _tile(m, tm), _tile(n, tn), _tile(kd, tk)
    o_spec = pl.BlockSpec((tm, tn), lambda i, j, k: (i, j))
    outs = outs if outs is not None else [_sds((m, n), out_dtype)]
    return _matmul(name, a, b, dims=NN, grid=(m // tm, n // tn, kd // tk),
                   a_spec=pl.BlockSpec((tm, tk), lambda i, j, k: (i, k)),
                   b_spec=pl.BlockSpec((tk, tn), lambda i, j, k: (k, j)),
                   acc_shape=(tm, tn), outs=outs, out_specs=[o_spec] * len(outs),
                   extra=extra, extra_specs=[o_spec] * len(extra), epilogue=epilogue)


def _mm_nn_sharded(name, a, b3, out_dtype=f32, tm=1024, tk=2048, epilogue=None, outs=None):
    m, kd = a.shape
    nsh, _, n = b3.shape
    tm, tk = _tile(m, tm), _tile(kd, tk)
    o_spec = pl.BlockSpec((tm, n), lambda i, j, k: (i, j))
    outs = outs if outs is not None else [_sds((m, nsh * n), out_dtype)]
    return _matmul(name, a, b3, dims=NN, grid=(m // tm, nsh, kd // tk),
                   a_spec=pl.BlockSpec((tm, tk), lambda i, j, k: (i, k)),
                   b_spec=pl.BlockSpec((None, tk, n), lambda i, j, k: (j, k, 0)),
                   acc_shape=(tm, n), outs=outs, out_specs=[o_spec] * len(outs), epilogue=epilogue)


def _mm_nt(name, a, b, out_dtype=f32, tm=1024, tn=1024, tk=2048, epilogue=None, extra=(), outs=None, after=None):
    m, kd = a.shape
    n = b.shape[0]
    tm, tn, tk = _tile(m, tm), _tile(n, tn), _tile(kd, tk)
    o_spec = pl.BlockSpec((tm, tn), lambda i, j, k: (i, j))
    outs = outs if outs is not None else [_sds((m, n), out_dtype)]
    return _matmul(name, a, b, dims=NT, grid=(m // tm, n // tn, kd // tk),
                   a_spec=pl.BlockSpec((tm, tk), lambda i, j, k: (i, k)),
                   b_spec=pl.BlockSpec((tn, tk), lambda i, j, k: (j, k)),
                   acc_shape=(tm, tn), outs=outs, out_specs=[o_spec] * len(outs),
                   extra=extra, extra_specs=[o_spec] * len(extra), epilogue=epilogue, after=after)


def _mm_tn(name, a, b, out_dtype=bf16, tm=1024, tn=1024, tk=2048, after=None):
    t, m = a.shape
    n = b.shape[1]
    tm, tn, tk = _tile(m, tm), _tile(n, tn), _tile(t, tk)
    return _matmul(name, a, b, dims=TN, grid=(m // tm, n // tn, t // tk),
                   a_spec=pl.BlockSpec((tk, tm), lambda i, j, k: (k, i)),
                   b_spec=pl.BlockSpec((tk, tn), lambda i, j, k: (k, j)),
                   acc_shape=(tm, tn), outs=[_sds((m, n), out_dtype)],
                   out_specs=[pl.BlockSpec((tm, tn), lambda i, j, k: (i, j))], after=after)[0]


def _mm_tn_sharded(name, a, b, nsh, out_dtype=bf16, tm=1024, tk=2048):
    t, m = a.shape
    n = b.shape[1] // nsh
    tm, tk = _tile(m, tm), _tile(t, tk)
    return _matmul(name, a, b, dims=TN, grid=(m // tm, nsh, t // tk),
                   a_spec=pl.BlockSpec((tk, tm), lambda i, j, k: (k, i)),
                   b_spec=pl.BlockSpec((tk, n), lambda i, j, k: (k, j)),
                   acc_shape=(tm, n), outs=[_sds((nsh, m, n), out_dtype)],
                   out_specs=[pl.BlockSpec((None, tm, n), lambda i, j, k: (j, i, 0))])[0]


SCAN_CHAINS = 8


def _scan_segments(s_len):
    nch = SCAN_CHAINS
    while s_len % (8 * nch) or (s_len // (8 * nch)) & (s_len // (8 * nch) - 1):
        nch //= 2
    return 8 * nch


def _to_segments(t, nseg):
    s_len, c = t.shape
    return t.reshape(nseg, s_len // nseg, c).transpose(1, 0, 2).reshape(s_len, c)


def _from_segments(t, nseg):
    s_len, c = t.shape
    return t.reshape(s_len // nseg, nseg, c).transpose(1, 0, 2).reshape(s_len, c)


def _lane_block_weights(t3, n_state):
    n_groups = t3.shape[0]
    gpl = LANE // n_state
    per = LANE // (gpl * SSM_GROUP)
    n_lb = n_groups // gpl
    t5 = t3.reshape(n_lb // per, per, gpl, SSM_GROUP, n_state)
    w = jnp.einsum("aqgic,gh,qs->aqsgihc", t5, jnp.eye(gpl, dtype=t3.dtype), jnp.eye(per, dtype=t3.dtype))
    return w.reshape(n_lb, LANE, LANE).astype(bf16)


def _lane_block_diag(w, n_state):
    gpl = LANE // n_state
    per = LANE // (gpl * SSM_GROUP)
    n_lb = w.shape[0]
    w7 = w.reshape(n_lb // per, per, per, gpl, SSM_GROUP, gpl, n_state)
    t5 = jnp.einsum("aqsgihc,gh,qs->aqgic", w7, jnp.eye(gpl, dtype=w.dtype), jnp.eye(per, dtype=w.dtype))
    return t5.reshape(n_lb * gpl, SSM_GROUP, n_state)


def _scan(name, src, w_in, w_out, a2, *, reverse, adjoint_of=None):
    s_len, n_ch = src.shape
    gp = a2.shape[1]
    per = (gp // LANE) // (n_ch // LANE)
    nseg = _scan_segments(s_len)
    nch = nseg // 8
    seg = s_len // nseg
    n_sq = int(math.log2(seg))
    assert 2 ** n_sq == seg
    adj = adjoint_of is not None
    chunk = _tile(s_len, 1024)
    n_chunks = s_len // chunk

    def body(*refs):
        it = iter(refs)
        src_ref, wi_ref, wo_ref, a_ref = (next(it) for _ in range(4))
        if adj:
            ut_ref, dyt_ref, hr_ref, hi_ref, hin_ref = (next(it) for _ in range(5))
        res_ref = next(it)
        if adj:
            da_ref, dbr_ref, dbi_ref, dcr_ref, dci_ref = (next(it) for _ in range(5))
        else:
            or_ref, oi_ref, oin_ref = (next(it) for _ in range(3))
        if adj:
            or_ref, oi_ref = next(it), next(it)

        for i in range(n_chunks):
            x2 = jnp.dot(src_ref[i * chunk:(i + 1) * chunk, :], wi_ref[...], preferred_element_type=f32)
            or_ref[i * chunk:(i + 1) * chunk, :] = x2[:, :LANE]
            oi_ref[i * chunk:(i + 1) * chunk, :] = x2[:, LANE:]

        ar = a_ref[0:1, :]
        ai = -a_ref[1:2, :] if reverse else a_ref[1:2, :]
        arb, aib = jnp.broadcast_to(ar, (8, LANE)), jnp.broadcast_to(ai, (8, LANE))

        def rows(ch, k):
            return pl.ds(pl.multiple_of(k * nseg + ch * 8, 8), 8)

        def advance(h, ch, k):
            hr, hi = h
            return (arb * hr - aib * hi + or_ref[rows(ch, k), :], arb * hi + aib * hr + oi_ref[rows(ch, k), :])

        def kk(n):
            return seg - 1 - n if reverse else n

        zero = jnp.zeros((8, LANE), f32)

        def sweep1(n, hs):
            return tuple(advance(hs[ch], ch, kk(n)) for ch in range(nch))

        ends = lax.fori_loop(0, seg, sweep1, tuple((zero, zero) for _ in range(nch)))

        pr, pi = ar, ai
        for _ in range(n_sq):
            pr, pi = pr * pr - pi * pi, 2.0 * pr * pi
        in_r, in_i = [None] * nseg, [None] * nseg
        cr = ci = jnp.zeros((1, LANE), f32)
        for j in (range(nseg - 1, -1, -1) if reverse else range(nseg)):
            in_r[j], in_i[j] = cr, ci
            er, ei = ends[j // 8][0][j % 8:j % 8 + 1, :], ends[j // 8][1][j % 8:j % 8 + 1, :]
            cr, ci = er + pr * cr - pi * ci, ei + pr * ci + pi * cr
        h0 = tuple((jnp.concatenate(in_r[8 * ch:8 * ch + 8], axis=0), jnp.concatenate(in_i[8 * ch:8 * ch + 8], axis=0))
                   for ch in range(nch))
        if not adj:
            for ch in range(nch):
                oin_ref[0, 8 * ch:8 * ch + 8, :] = h0[ch][0]
                oin_ref[1, 8 * ch:8 * ch + 8, :] = h0[ch][1]

        def emit(ch, k, h):
            or_ref[rows(ch, k), :] = h[0]
            oi_ref[rows(ch, k), :] = h[1]

        def pair(h, p):
            return h[0] * p[0] + h[1] * p[1], h[1] * p[0] - h[0] * p[1]

        def sweep2(n, carry):
            k = kk(n)
            new = tuple(advance(carry[ch], ch, k) for ch in range(nch))
            for ch in range(nch):
                emit(ch, k, new[ch])
            if not adj:
                return new
            dr, di = carry[nch]
            for ch in range(nch):
                qr, qi = pair(new[ch], (hr_ref[rows(ch, k - 1), :], hi_ref[rows(ch, k - 1), :]))
                dr, di = dr + qr, di + qi
            return new + ((dr, di),)

        if adj:
            carry = lax.fori_loop(0, seg - 1, sweep2, h0 + ((zero, zero),))
            dr, di = carry[nch]
            for ch in range(nch):
                new = advance(carry[ch], ch, 0)
                emit(ch, 0, new)
                qr, qi = pair(new, (hin_ref[0, 8 * ch:8 * ch + 8, :], hin_ref[1, 8 * ch:8 * ch + 8, :]))
                dr, di = dr + qr, di + qi
            da_ref[0] = jnp.sum(dr, axis=0, keepdims=True)
            da_ref[1] = jnp.sum(di, axis=0, keepdims=True)
        else:
            lax.fori_loop(0, seg, sweep2, h0)

        first = pl.program_id(0) % per == 0
        for i in range(n_chunks):
            sl = slice(i * chunk, (i + 1) * chunk)
            h2 = jnp.concatenate([or_ref[sl, :], oi_ref[sl, :]], axis=1).astype(bf16)
            part = lax.dot_general(h2, wo_ref[...], (NT, ((), ())), preferred_element_type=f32)

            @pl.when(first)
            def _():
                res_ref[sl, :] = part

            @pl.when(jnp.logical_not(first))
            def _():
                res_ref[sl, :] += part

        if adj:
            def over_time(xt_ref, yr_ref, yi_ref):
                tot = jnp.zeros((LANE, 2 * LANE), f32)
                for i in range(n_chunks):
                    sl = slice(i * chunk, (i + 1) * chunk)
                    y2 = jnp.concatenate([yr_ref[sl, :], yi_ref[sl, :]], axis=1).astype(bf16)
                    tot += jnp.dot(xt_ref[:, sl], y2, preferred_element_type=f32)
                return tot[:, :LANE], tot[:, LANE:]

            dbr_ref[...], dbi_ref[...] = over_time(ut_ref, or_ref, oi_ref)
            dcr_ref[...], dci_ref[...] = over_time(dyt_ref, hr_ref, hi_ref)

    col = pl.BlockSpec((s_len, LANE), lambda l: (0, l))
    chan = pl.BlockSpec((s_len, LANE), lambda l: (0, l // per))
    in_spec = pl.BlockSpec((2, nseg, LANE), lambda l: (0, 0, l))
    w_spec = pl.BlockSpec((None, LANE, LANE), lambda l: (l, 0, 0))
    w2_spec = pl.BlockSpec((None, LANE, 2 * LANE), lambda l: (l, 0, 0))
    ins = [src, jnp.concatenate(w_in, axis=2), jnp.concatenate(w_out, axis=2), a2]
    in_specs = [chan, w2_spec, w2_spec, pl.BlockSpec((8, LANE), lambda l: (0, l))]
    outs, out_specs = [_sds((s_len, n_ch), f32)], [chan]
    scratch = []
    if adj:
        ins += list(adjoint_of)
        chan_t = pl.BlockSpec((LANE, s_len), lambda l: (l // per, 0))
        in_specs += [chan_t, chan_t, col, col, in_spec]
        outs += [_sds((2, 1, gp), f32)] + [_sds((gp // LANE, LANE, LANE), f32)] * 4
        out_specs += [pl.BlockSpec((2, 1, LANE), lambda l: (0, 0, l))] + [w_spec] * 4
        scratch = [pltpu.VMEM((s_len, LANE), f32)] * 2
    else:
        outs += [_sds((s_len, gp), f32)] * 2 + [_sds((2, nseg, gp), f32)]
        out_specs += [col, col, in_spec]
    res = _pcall(body, name=name, grid=(gp // LANE,), in_specs=in_specs, out_specs=out_specs, out_shape=outs,
                 scratch_shapes=scratch, sem=("arbitrary",))(*ins)
    if adj:
        return res[0], res[1], (res[2], res[3]), (res[4], res[5])
    return res


def _ssm_param_fn(lam_re, lam_im, log_step, b_re2, b_im2, expand):
    step = jnp.exp(log_step)
    xr, xi = lam_re * step, lam_im * step
    mag = jnp.exp(xr)
    ar, ai = mag * jnp.cos(xi), mag * jnp.sin(xi)
    nr, ni = ar - 1.0, ai
    den = lam_re * lam_re + lam_im * lam_im
    cr = (nr * lam_re + ni * lam_im) / den
    ci = (ni * lam_re - nr * lam_im) / den
    cre = jnp.dot(cr, expand, precision=HI, preferred_element_type=f32)
    cie = jnp.dot(ci, expand, precision=HI, preferred_element_type=f32)
    return ar, ai, cre * b_re2 - cie * b_im2, cre * b_im2 + cie * b_re2


def _ssm_params(lam_re, lam_im, log_step, b_re2, b_im2, expand):
    def body(*refs):
        res = _ssm_param_fn(*[r[...] for r in refs[:6]])
        for r, o in zip(refs[6:], res):
            r[...] = o

    g, p = lam_re.shape
    return _pcall(body, name="ssm_params", out_shape=[_sds((g, p), f32)] * 2 + [_sds(b_re2.shape, f32)] * 2)(
        lam_re, lam_im, log_step, b_re2, b_im2, expand)


def _ssm_params_bwd(lam_re, lam_im, log_step, b_re2, b_im2, expand, d_ar, d_ai, d_bbr, d_bbi):
    def body(*refs):
        prim = [r[...] for r in refs[:5]]
        ex = refs[5][...]
        cot = tuple(r[...] for r in refs[6:10])
        _, vjp = jax.vjp(lambda *p_: _ssm_param_fn(*p_, ex), *prim)
        for r, o in zip(refs[10:], vjp(cot)):
            r[...] = o

    shapes = [lam_re.shape, lam_im.shape, log_step.shape, b_re2.shape, b_im2.shape]
    return _pcall(body, name="ssm_params_bwd", out_shape=[_sds(s, f32) for s in shapes])(
        lam_re, lam_im, log_step, b_re2, b_im2, expand, d_ar, d_ai, d_bbr, d_bbi)


def _slope_table(n_heads):
    s = 2.0 ** (-8.0 * (jnp.arange(n_heads, dtype=f32) + 1.0) / n_heads)
    return jnp.broadcast_to(s[:, None, None], (n_heads, 1, LANE))


def _band_bias(slope_d, shift):
    qi = lax.broadcasted_iota(jnp.int32, (BAND, BAND), 0)
    ki = lax.broadcasted_iota(jnp.int32, (BAND, BAND), 1)
    mask = (ki >= qi) if shift else (ki <= qi)
    return jnp.where(mask, -slope_d * (qi - ki + shift).astype(f32), NEG)


def _window_bias(slope_d, has_prev):
    own = _band_bias(slope_d, 0)
    mid = jnp.concatenate([_band_bias(slope_d, BAND), own], axis=1)
    none = jnp.concatenate([jnp.full((BAND, BAND), NEG, f32), own], axis=1)
    return mid, jnp.where(has_prev, mid, none)


def _window_scores(q, k2, bias):
    return lax.dot_general(q, k2, (NT, ((), ())), preferred_element_type=f32) * (HEAD ** -0.5) + bias


def _attn_geometry(s_len, dil, rows=1024):
    piece = BAND * dil
    m = max(1, rows // piece)
    while s_len % (piece * m):
        m //= 2
    return m, piece


def _stream_rows(start, dil):
    return pl.ds(start, BAND, stride=dil) if dil > 1 else pl.ds(start, BAND)


def _attn_fwd(qn, kn, proj, v_blk, dil, slopes):
    s_len, aw = qn.shape
    n_heads = aw // HEAD
    m, piece = _attn_geometry(s_len, dil, 2048)
    rows = m * piece

    def body(q_ref, k_ref, kp_ref, v_ref, vp_ref, sl_ref, o_ref, lse_ref):
        bias_mid, bias_first = _window_bias(sl_ref[:, 0:1] * float(dil), pl.program_id(1) > 0)
        for b in range(m):
            for r in range(dil):
                idx = _stream_rows(b * piece + r, dil)
                q, kc, vc = (ref[idx, :].astype(bf16) for ref in (q_ref, k_ref, v_ref))
                if b:
                    pidx = _stream_rows((b - 1) * piece + r, dil)
                    kp, vp = k_ref[pidx, :].astype(bf16), v_ref[pidx, :].astype(bf16)
                else:
                    pidx = _stream_rows(r, dil)
                    kp, vp = kp_ref[pidx, :].astype(bf16), vp_ref[pidx, :].astype(bf16)
                k2, v2 = jnp.concatenate([kp, kc], axis=0), jnp.concatenate([vp, vc], axis=0)
                s = _window_scores(q, k2, bias_mid if b else bias_first)
                mx = jnp.max(s, axis=1, keepdims=True)
                p = jnp.exp(s - mx)
                den = jnp.sum(p, axis=1, keepdims=True)
                o_ref[idx, :] = jnp.dot(p.astype(bf16), v2, preferred_element_type=f32) / den
                lse_ref[idx, :] = jnp.broadcast_to(mx + jnp.log(den), (BAND, HEAD))

    def cur(blk0):
        return pl.BlockSpec((rows, HEAD), lambda h, t: (t, blk0 + h))

    def prev(blk0):
        return pl.BlockSpec((piece, HEAD), lambda h, t: (jnp.maximum(t * m - 1, 0), blk0 + h))

    sl = pl.BlockSpec((None, 1, LANE), lambda h, t: (h, 0, 0))
    return _pcall(body, name=f"attn_fwd_d{dil}", grid=(n_heads, s_len // rows),
                  in_specs=[cur(0), cur(0), prev(0), cur(v_blk), prev(v_blk), sl], out_specs=[cur(0), cur(0)],
                  out_shape=[_sds((s_len, aw), f32)] * 2, sem=("parallel", "parallel"))(
        qn, kn, kn, proj, proj, slopes)


def _attn_bwd(qn, kn, proj, v_blk, do, lse, dd, dil, slopes, acc=None):
    s_len, aw = qn.shape
    n_heads = aw // HEAD
    m, piece = _attn_geometry(s_len, dil, max(2048, 2 * BAND * dil))
    rows = m * piece
    n_tiles = s_len // rows
    scale = HEAD ** -0.5

    def body(q_ref, qx_ref, k_ref, kp_ref, v_ref, vp_ref, do_ref, dox_ref, l_ref, lx_ref, d_ref, dx_ref, sl_ref,
             *rest):
        dq_ref, dk_ref, dv_ref = rest[-3:]

        def put(ref, which, idx, val):
            ref[idx, :] = val if acc is None else val + rest[which][idx, :]

        t = pl.program_id(1)
        slope_d = sl_ref[:, 0:1] * float(dil)
        bias_mid, bias_first = _window_bias(slope_d, t > 0)
        bias_next = _band_bias(slope_d, BAND)

        def query_side(ref_q, ref_do, ref_l, ref_d, idx):
            return (ref_q[idx, :].astype(bf16), ref_do[idx, :].astype(bf16), ref_l[idx, :][:, 0:1],
                    ref_d[idx, :][:, 0:1])

        def probs(qs, keys, values, bias):
            q, do_, l_col, d_col = qs
            p = jnp.exp(_window_scores(q, keys, bias) - l_col)
            dp = lax.dot_general(do_, values, (NT, ((), ())), preferred_element_type=f32)
            return p.astype(bf16), (p * (dp - d_col)).astype(bf16)

        def tn(a_, b_):
            return lax.dot_general(a_, b_, (TN, ((), ())), preferred_element_type=f32)

        for r in range(dil):
            pend = None
            for b in range(m):
                idx = _stream_rows(b * piece + r, dil)
                qs = query_side(q_ref, do_ref, l_ref, d_ref, idx)
                kc, vc = k_ref[idx, :].astype(bf16), v_ref[idx, :].astype(bf16)
                if b:
                    kp, vp = kc_prev, vc_prev
                else:
                    pidx = _stream_rows(r, dil)
                    kp, vp = kp_ref[pidx, :].astype(bf16), vp_ref[pidx, :].astype(bf16)
                k2, v2 = jnp.concatenate([kp, kc], axis=0), jnp.concatenate([vp, vc], axis=0)
                p, ds = probs(qs, k2, v2, bias_mid if b else bias_first)
                put(dq_ref, 0, idx, jnp.dot(ds, k2, preferred_element_type=f32) * scale)
                dk2, dv2 = tn(ds, qs[0]), tn(p, qs[1])
                if pend is not None:
                    put(dk_ref, 1, pend[0], (pend[1] + dk2[:BAND]) * scale)
                    put(dv_ref, 2, pend[0], pend[2] + dv2[:BAND])
                pend = (idx, dk2[BAND:], dv2[BAND:])
                kc_prev, vc_prev = kc, vc
            qs = query_side(qx_ref, dox_ref, lx_ref, dx_ref, _stream_rows(r, dil))
            p, ds = probs(qs, kc_prev, vc_prev, bias_next)
            live = t < n_tiles - 1
            put(dk_ref, 1, pend[0], (pend[1] + jnp.where(live, tn(ds, qs[0]), 0.0)) * scale)
            put(dv_ref, 2, pend[0], pend[2] + jnp.where(live, tn(p, qs[1]), 0.0))

    def cur(blk0):
        return pl.BlockSpec((rows, HEAD), lambda h, t: (t, blk0 + h))

    def prev(blk0):
        return pl.BlockSpec((piece, HEAD), lambda h, t: (jnp.maximum(t * m - 1, 0), blk0 + h))

    def nxt(blk0):
        return pl.BlockSpec((piece, HEAD), lambda h, t: (jnp.minimum(t * m + m, n_tiles * m - 1), blk0 + h))

    sl = pl.BlockSpec((None, 1, LANE), lambda h, t: (h, 0, 0))
    in_specs = [cur(0), nxt(0), cur(0), prev(0), cur(v_blk), prev(v_blk), cur(0), nxt(0), cur(0), nxt(0), cur(0),
                nxt(0), sl]
    ins = [qn, qn, kn, kn, proj, proj, do, do, lse, lse, dd, dd, slopes]
    extra = {}
    if acc is not None:
        extra = dict(input_output_aliases={len(ins) + i: i for i in range(3)})
        in_specs += [cur(0)] * 3
        ins += list(acc)
    return _pcall(body, name=f"attn_bwd_d{dil}", grid=(n_heads, n_tiles), in_specs=in_specs,
                  out_specs=[cur(0)] * 3, out_shape=[_sds((s_len, aw), f32)] * 3,
                  sem=("parallel", "parallel"), **extra)(*ins)


def _exchange(name, srcs, scatter):
    n = len(srcs)

    def body(*refs):
        src, out = refs[:n], refs[n:2 * n]
        send_sems, recv_sems, local_sems = refs[2 * n:]
        x, y, c = lax.axis_index("x"), lax.axis_index("y"), lax.axis_index("c")
        me = 4 * x + 2 * y + c

        def peer(r):
            return ((1 - x) if r & 4 else x, (1 - y) if r & 2 else y, (1 - c) if r & 1 else c)

        def lin(p):
            return 4 * p[0] + 2 * p[1] + p[2]

        def piece(a, idx):
            return src[a].at[idx] if scatter[a] else src[a]

        local, sends = [], []
        for a in range(n):
            cp = pltpu.make_async_copy(piece(a, me), out[a].at[me], local_sems.at[a])
            cp.start()
            local.append(cp)
        for r in range(1, N_DEV):
            p = peer(r)
            for a in range(n):
                cp = pltpu.make_async_remote_copy(src_ref=piece(a, lin(p)), dst_ref=out[a].at[me],
                                                  send_sem=send_sems.at[a, r - 1], recv_sem=recv_sems.at[a, r - 1],
                                                  device_id=p, device_id_type=MESH)
                cp.start()
                sends.append(cp)
        for r in range(1, N_DEV):
            p = peer(r)
            for a in range(n):
                pltpu.make_async_remote_copy(src_ref=piece(a, lin(p)), dst_ref=out[a].at[lin(p)],
                                             send_sem=send_sems.at[a, r - 1], recv_sem=recv_sems.at[a, r - 1],
                                             device_id=p, device_id_type=MESH).wait_recv()
        for cp in sends:
            cp.wait_send()
        for cp in local:
            cp.wait()

    def piece_shape(a):
        return srcs[a].shape[1:] if scatter[a] else srcs[a].shape

    any_spec = pl.BlockSpec(memory_space=pl.ANY)
    return _pcall(body, name=name, in_specs=[any_spec] * n, out_specs=[any_spec] * n,
                  out_shape=[_sds((N_DEV, *piece_shape(a)), srcs[a].dtype) for a in range(n)],
                  scratch_shapes=[pltpu.SemaphoreType.DMA((n, N_DEV - 1)), pltpu.SemaphoreType.DMA((n, N_DEV - 1)),
                                  pltpu.SemaphoreType.DMA((n,))])(*srcs)


_HBM = pl.BlockSpec(memory_space=pltpu.HBM)
_SEM = pl.BlockSpec(memory_space=pltpu.SEMAPHORE)
_EFFECT = pltpu.SideEffectType.DATAFLOW_SIDE_EFFECTING


def _peer_ids():
    x, y, c = lax.axis_index("x"), lax.axis_index("y"), lax.axis_index("c")
    peers = [((1 - x) if r & 4 else x, (1 - y) if r & 2 else y, (1 - c) if r & 1 else c) for r in range(1, N_DEV)]
    return 4 * x + 2 * y + c, peers, [4 * p[0] + 2 * p[1] + p[2] for p in peers]


def _exchange_start(name, src, scatter, after):
    piece_shape = src.shape[1:] if scatter else src.shape

    def body(src_ref, land_ref, after_ref, send_sems, recv_sems, local_sem, src_thru, land_thru, token):
        me, peers, lins = _peer_ids()

        def piece(idx):
            return src_ref.at[idx] if scatter else src_ref

        pltpu.make_async_copy(piece(me), land_ref.at[me], local_sem).start()
        for r, (p, lp) in enumerate(zip(peers, lins)):
            pltpu.make_async_remote_copy(src_ref=piece(lp), dst_ref=land_ref.at[me], send_sem=send_sems.at[r],
                                         recv_sem=recv_sems.at[r], device_id=p, device_id_type=MESH).start()
        token[...] = jnp.zeros_like(token)

    land = pltpu.with_memory_space_constraint(lax.empty((N_DEV, *piece_shape), src.dtype), pltpu.HBM)
    send_sems, recv_sems, local_sem, src_thru, land_thru, token = pl.pallas_call(
        body, name=name,
        out_shape=(pltpu.SemaphoreType.DMA((N_DEV - 1,)), pltpu.SemaphoreType.DMA((N_DEV - 1,)),
                   pltpu.SemaphoreType.DMA(()), pltpu.HBM(src.shape, src.dtype),
                   pltpu.HBM((N_DEV, *piece_shape), src.dtype), _sds((8, LANE), f32)),
        in_specs=(_HBM, _HBM, pl.BlockSpec(memory_space=pl.ANY)),
        out_specs=(_SEM, _SEM, _SEM, _HBM, _HBM, pl.BlockSpec(memory_space=pltpu.VMEM)),
        input_output_aliases={0: 3, 1: 4},
        compiler_params=pltpu.CompilerParams(has_side_effects=_EFFECT),
    )(pltpu.with_memory_space_constraint(src, pltpu.HBM), land, after)
    return (send_sems, recv_sems, local_sem, src_thru, land_thru, scatter), token


def _exchange_wait(name, handle, *after):
    send_sems, recv_sems, local_sem, src_thru, land_thru, scatter = handle

    def body(src_ref, land_ref, send_sems_, recv_sems_, local_sem_, *rest):
        me, peers, lins = _peer_ids()

        def piece(idx):
            return src_ref.at[idx] if scatter else src_ref

        pltpu.make_async_copy(piece(me), land_ref.at[me], local_sem_).wait()
        for r, (p, lp) in enumerate(zip(peers, lins)):
            pltpu.make_async_remote_copy(src_ref=piece(lp), dst_ref=land_ref.at[me], send_sem=send_sems_.at[r],
                                         recv_sem=recv_sems_.at[r], device_id=p, device_id_type=MESH).wait_send()
            pltpu.make_async_remote_copy(src_ref=piece(lp), dst_ref=land_ref.at[lp], send_sem=send_sems_.at[r],
                                         recv_sem=recv_sems_.at[r], device_id=p, device_id_type=MESH).wait_recv()

    return pl.pallas_call(
        body, name=name,
        out_shape=(pltpu.HBM(src_thru.shape, src_thru.dtype), pltpu.HBM(land_thru.shape, land_thru.dtype)),
        in_specs=(_HBM, _HBM, _SEM, _SEM, _SEM, *[pl.BlockSpec(memory_space=pl.ANY)] * len(after)),
        out_specs=(_HBM, _HBM), input_output_aliases={0: 0, 1: 1},
        compiler_params=pltpu.CompilerParams(has_side_effects=_EFFECT),
    )(src_thru, land_thru, send_sems, recv_sems, local_sem, *after)[1]


def _adamw(name, w, m, v, g_or_stack, stacked, rows=256):
    r, c = w.shape
    tr = _tile(r, rows)

    def fn(w_, m_, v_, g_):
        if stacked:
            g = g_[0].astype(f32)
            for j in range(1, N_DEV):
                g = g + g_[j].astype(f32)
        else:
            g = g_
        m_new = ADAM_B1 * m_ + (1.0 - ADAM_B1) * g
        v_new = ADAM_B2 * v_ + (1.0 - ADAM_B2) * (g * g)
        m_hat = m_new / (1.0 - ADAM_B1 ** ADAM_STEP)
        v_hat = v_new / (1.0 - ADAM_B2 ** ADAM_STEP)
        delta = -ADAM_LR * (m_hat / (jnp.sqrt(v_hat) + ADAM_EPS) + ADAM_WD * w_)
        return g, delta, m_new, v_new

    blk = _row(tr, c)
    g_spec = pl.BlockSpec((N_DEV, tr, c), lambda i: (0, i, 0)) if stacked else blk
    return _rowwise(name, fn, r // tr, [w, m, v, g_or_stack], [blk, blk, blk, g_spec],
                    [_sds((r, c), f32)] * 4, [blk] * 4, [False] * 4)


def _ada_fwd(c_all, w_shard, b_shard):
    nb_, d = c_all.shape
    n = w_shard.shape[1]
    tn = _tile(n, 512)

    def body(c_ref, w_ref, b_ref, o_ref):
        a = jax.nn.silu(c_ref[...]).astype(bf16)
        o_ref[...] = jnp.dot(a, w_ref[...].astype(bf16), preferred_element_type=f32) + b_ref[...]

    return _pcall(body, name="ada_fwd", grid=(n // tn,),
                  in_specs=[pl.BlockSpec((nb_, d), lambda j: (0, 0)), pl.BlockSpec((d, tn), lambda j: (0, j)),
                            pl.BlockSpec((1, tn), lambda j: (0, j))],
                  out_specs=pl.BlockSpec((nb_, tn), lambda j: (0, j)), out_shape=_sds((nb_, n), f32),
                  sem=("parallel",))(c_all, w_shard, b_shard)


def _ada_bwd(c_all, dmod_cols):
    nb_, d = c_all.shape
    n = dmod_cols.shape[1]
    tn = _tile(n, 512)

    def body(c_ref, g_ref, o_ref):
        a = jax.nn.silu(c_ref[...]).astype(bf16).astype(f32)
        g = g_ref[...].astype(bf16).astype(f32)
        o_ref[...] = lax.dot_general(a, g, (TN, ((), ())), precision=HI, preferred_element_type=f32)

    return _pcall(body, name="ada_bwd", grid=(n // tn,),
                  in_specs=[pl.BlockSpec((nb_, d), lambda j: (0, 0)), pl.BlockSpec((nb_, tn), lambda j: (0, j))],
                  out_specs=pl.BlockSpec((d, tn), lambda j: (0, j)), out_shape=_sds((d, n), f32),
                  sem=("parallel",))(c_all, dmod_cols)


SMALL_LATE = ("b_ada", "norm1_g", "q_norm_g", "k_norm_g")
SMALL_EARLY = ("lam_re", "lam_im", "log_step", "b_re", "b_im", "c_re", "c_im", "d_skip", "b_glu", "attn_out_g",
               "ssm_out_g", "norm2_g")
ORDER = ("w_ada", "b_ada", "norm1_g", "w_in", "q_norm_g", "k_norm_g", "lam_re", "lam_im", "log_step", "b_re", "b_im",
         "c_re", "c_im", "d_skip", "w_glu", "b_glu", "attn_out_g", "ssm_out_g", "w_out", "norm2_g", "w_ff1", "w_ff2")


def _pack(parts):
    flat = jnp.concatenate([p.reshape(-1) for p in parts])
    pad = (-flat.shape[0]) % (8 * LANE)
    return jnp.pad(flat, (0, pad)).reshape(-1, LANE)


def kernel(x, c, w_ada, b_ada, norm1_g, w_in, q_norm_g, k_norm_g, lam_re, lam_im, log_step, b_re, b_im, c_re, c_im, d_skip, w_glu, b_glu, attn_out_g, ssm_out_g, w_out, norm2_g, w_ff1, w_ff2, loss_target, m_w_ada, m_b_ada, m_norm1_g, m_w_in, m_q_norm_g, m_k_norm_g, m_lam_re, m_lam_im, m_log_step, m_b_re, m_b_im, m_c_re, m_c_im, m_d_skip, m_w_glu, m_b_glu, m_attn_out_g, m_ssm_out_g, m_w_out, m_norm2_g, m_w_ff1, m_w_ff2, v_w_ada, v_b_ada, v_norm1_g, v_w_in, v_q_norm_g, v_k_norm_g, v_lam_re, v_lam_im, v_log_step, v_b_re, v_b_im, v_c_re, v_c_im, v_d_skip, v_w_glu, v_b_glu, v_attn_out_g, v_ssm_out_g, v_w_out, v_norm2_g, v_w_ff1, v_w_ff2):
    env = dict(locals())
    wts = {n: env[n] for n in ORDER}
    mom = {n: env["m_" + n] for n in ORDER}
    var = {n: env["v_" + n] for n in ORDER}

    xs, tgt = x[0], loss_target[0]
    s_len, d = xs.shape
    aw = d // 2
    sw = d - aw
    n_heads = aw // HEAD
    n_groups = sw // SSM_GROUP
    n_state = lam_re.shape[-1]
    gp = n_groups * n_state
    tm = _tile(s_len, 256)
    steps = s_len // tm
    me = 4 * lax.axis_index("x") + 2 * lax.axis_index("y") + lax.axis_index("c")

    (c_all,) = _exchange("gather_c", [c], [False])
    c_all = c_all.reshape(N_DEV, d)

    n_ada = w_ada.shape[-1]
    b_ada_cols = lax.dynamic_slice_in_dim(b_ada, me * n_ada, n_ada, axis=1)
    mod_part = _ada_fwd(c_all, w_ada[0], b_ada_cols)
    (mod_all,) = _exchange("gather_mod", [mod_part], [False])
    mod = lax.dynamic_index_in_dim(mod_all, me, axis=1, keepdims=False).reshape(1, 6 * d)
    sh1, sc1, g1, sh2, sc2, g2 = (mod[:, i * d:(i + 1) * d] for i in range(6))

    gather, started = {}, jnp.zeros((1, 1), f32)
    for name in ("w_in", "w_glu", "w_out", "w_ff1", "w_ff2"):
        gather[name], token = _exchange_start("gather_" + name, wts[name][0].astype(bf16), False, mod_all)
        started = started + token[0:1, 0:1]
    sc1 = sc1 + started

    (h,) = _rowwise("norm1", _norm_mod, steps, [xs, norm1_g, sc1, sh1],
                    [_row(tm, d), _vec(d), _vec(d), _vec(d)], [_sds((s_len, d), bf16)], [_row(tm, d)], [False])
    lam_re2, lam_im2 = lam_re[0], lam_im[0]
    log_step2 = log_step[0].reshape(n_groups, 1)
    b_re2 = b_re[0].reshape(n_groups, n_state * SSM_GROUP)
    b_im2 = b_im[0].reshape(n_groups, n_state * SSM_GROUP)
    expand = jnp.repeat(jnp.eye(n_state, dtype=f32), SSM_GROUP, axis=1)
    a_re, a_im, bb_re2, bb_im2 = _ssm_params(lam_re2, lam_im2, log_step2, b_re2, b_im2, expand)
    a2 = jnp.zeros((8, gp), f32).at[0].set(a_re.reshape(gp)).at[1].set(a_im.reshape(gp))
    w_bu = tuple(_lane_block_weights(t.reshape(n_groups, n_state, SSM_GROUP).transpose(0, 2, 1), n_state)
                 for t in (bb_re2, bb_im2))
    w_c = (_lane_block_weights(c_re[0], n_state), _lane_block_weights(-c_im[0], n_state))

    packed = {names: tuple(_pack([t[n] for n in names]) for t in (wts, mom, var))
              for names in (SMALL_LATE, SMALL_EARLY)}

    win_g = _exchange_wait("gathered_w_in", gather["w_in"], h, a2, *w_bu, *w_c, *packed[SMALL_LATE],
                           *packed[SMALL_EARLY])
    (proj,) = _mm_nn_sharded("in_proj", h, win_g)

    def qk_fn(q, k, gq, gk):
        return _head_rms(q, gq), _head_rms(k, gk)

    qn, kn = _rowwise("qk_norm", qk_fn, steps, [proj, proj, q_norm_g, k_norm_g],
                      [_row(tm, aw, 0), _row(tm, aw, 1), _vec(HEAD), _vec(HEAD)],
                      [_sds((s_len, aw), f32)] * 2, [_row(tm, aw)] * 2, [False] * 2)
    v_blk = 2 * aw // HEAD

    slopes = _slope_table(n_heads)
    pat = [_attn_fwd(qn, kn, proj, v_blk, dil, slopes) for _, dil in DILATIONS]

    def attn_mix_fn(o1, l1, o2, l2, o3, l3):
        m = jnp.maximum(jnp.maximum(l1, l2), l3)
        e1, e2, e3 = jnp.exp(l1 - m), jnp.exp(l2 - m), jnp.exp(l3 - m)
        tot = e1 + e2 + e3
        return (e1 * o1 + e2 * o2 + e3 * o3) / tot, m + jnp.log(tot)

    attn, lse = _rowwise("attn_mix", attn_mix_fn, steps, [t for ol in pat for t in ol], [_row(tm, aw)] * 6,
                         [_sds((s_len, aw), f32)] * 2, [_row(tm, aw)] * 2, [False] * 2)

    nseg = _scan_segments(s_len)
    u_seg = _to_segments(proj[:, 3 * aw:].astype(bf16), nseg)
    y_seg, h_re, h_im, hin_f = _scan("ssm_scan", u_seg, w_bu, w_c, a2, reverse=False)
    ymm = _from_segments(y_seg, nseg)

    u_spec = _row(tm, sw, 3 * aw // sw)
    (yg,) = _rowwise("ssm_gelu", _ypre_fn, steps, [ymm, proj, d_skip], [_row(tm, sw), u_spec, _vec(sw)],
                     [_sds((s_len, sw), f32)], [_row(tm, sw)], [False])
    wglu_g = _exchange_wait("gathered_w_glu", gather["w_glu"], yg).reshape(sw, sw)
    (z,) = _mm_nn("glu_proj", yg, wglu_g)
    (cat,) = _rowwise("mix_norm", _mix_fn, steps, [attn, yg, z, b_glu, attn_out_g, ssm_out_g],
                      [_row(tm, aw), _row(tm, sw), _row(tm, sw), _vec(sw), _vec(aw), _vec(sw)],
                      [_sds((s_len, d), bf16)], [_row(tm, d)], [False])
    wout_g = _exchange_wait("gathered_w_out", gather["w_out"], cat).reshape(d, d)
    (mixed,) = _mm_nn("out_proj", cat, wout_g)

    def res_norm2_fn(x_, mixed_, g1_, gn, sc, sh):
        x1_ = x_ + g1_ * mixed_
        return x1_, _norm_mod(x1_, gn, sc, sh)

    x1, h2 = _rowwise("norm2", res_norm2_fn, steps, [xs, mixed, g1, norm2_g, sc2, sh2],
                      [_row(tm, d), _row(tm, d)] + [_vec(d)] * 4,
                      [_sds((s_len, d), f32), _sds((s_len, d), bf16)], [_row(tm, d)] * 2, [False] * 2)

    def act_epilogue(acc):
        r = jnp.maximum(acc, 0.0)
        return r, r * r

    wff1_g = _exchange_wait("gathered_w_ff1", gather["w_ff1"], h2)
    r_ff, act = _mm_nn_sharded("ff1", h2, wff1_g, epilogue=act_epilogue,
                               outs=[_sds((s_len, 4 * d), bf16), _sds((s_len, 4 * d), bf16)])
    wff2_g = _exchange_wait("gathered_w_ff2", gather["w_ff2"], act).reshape(4 * d, d)
    (ff,) = _mm_nn("ff2", act, wff2_g)

    def loss_fn(x1_, ff_, tgt_, g2_):
        e = x1_ + g2_ * ff_ - tgt_
        dy_ = e * (1.0 / d)
        part = jnp.full((1, LANE), 0.5 / d, f32) * jnp.sum(e * e)
        return dy_, g2_ * dy_, part, jnp.sum(dy_ * ff_, axis=0, keepdims=True)

    dy, dff, loss_part, d_g2 = _rowwise(
        "loss", loss_fn, steps, [x1, ff, tgt, g2], [_row(tm, d)] * 3 + [_vec(d)],
        [_sds((s_len, d), f32), _sds((s_len, d), bf16), _sds((1, LANE), f32), _sds((1, d), f32)],
        [_row(tm, d), _row(tm, d), _vec(LANE), _vec(d)], [False, False, True, True])
    loss = lax.psum(loss_part[0, 0], ("x", "y", "c"))

    def dact_epilogue(acc, r_):
        return (acc * (2.0 * r_.astype(f32)),)

    (da,) = _mm_nt("ff2_dx", dff, wff2_g, epilogue=dact_epilogue, extra=[r_ff], outs=[_sds((s_len, 4 * d), bf16)])
    scatter = {}
    g_wff2 = _mm_tn("ff2_dw", act, dff, after=loss.reshape(1, 1)).reshape(N_DEV, 4 * d // N_DEV, d)
    scatter["w_ff2"], tok_ff2 = _exchange_start("scatter_w_ff2", g_wff2, True, loss.reshape(1, 1))
    dh2 = _mm_nt("ff1_dx", da, wff1_g.transpose(1, 0, 2).reshape(d, 4 * d))[0]
    g_wff1 = _mm_tn_sharded("ff1_dw", h2, da, N_DEV)
    scatter["w_ff1"], tok_ff1 = _exchange_start("scatter_w_ff1", g_wff1, True, started)
    norm2_g_t = norm2_g + (tok_ff2[0:1, 0:1] + tok_ff1[0:1, 0:1])

    def norm2_bwd_fn(dh2_, x1_, dy_, mixed_, gn, sc, sh, g1_):
        _, vjp = jax.vjp(_norm_mod, x1_, gn, sc, sh)
        dx, dgn, dsc, dsh = vjp(dh2_)
        dx1_ = dy_ + dx
        return dx1_, g1_ * dx1_, dgn, dsc, dsh, jnp.sum(dx1_ * mixed_, axis=0, keepdims=True)

    dx1, dmixed, d_norm2_g, d_sc2, d_sh2, d_g1 = _rowwise(
        "norm2_bwd", norm2_bwd_fn, steps, [dh2, x1, dy, mixed, norm2_g_t, sc2, sh2, g1],
        [_row(tm, d)] * 4 + [_vec(d)] * 4,
        [_sds((s_len, d), f32), _sds((s_len, d), bf16)] + [_sds((1, d), f32)] * 4,
        [_row(tm, d)] * 2 + [_vec(d)] * 4, [False, False, True, True, True, True])

    (dcat,) = _mm_nt("out_dx", dmixed, wout_g)
    g_wout = _mm_tn("out_dw", cat, dmixed).reshape(N_DEV, d // N_DEV, d)
    scatter["w_out"], tok_out = _exchange_start("scatter_w_out", g_wout, True, started)
    b_glu_t = b_glu + tok_out[0:1, 0:1]

    def mix_bwd_fn(dcat_, attn_, yg_, z_, bglu, ga, gs):
        _, vjp = jax.vjp(_mix_fn, attn_, yg_, z_, bglu, ga, gs)
        dattn_, dyg_, dz_, dbglu, dga, dgs = vjp(dcat_)
        prod = dattn_ * attn_
        dd_ = jnp.concatenate([jnp.broadcast_to(jnp.sum(prod[:, i * HEAD:(i + 1) * HEAD], axis=1, keepdims=True),
                                                (prod.shape[0], HEAD)) for i in range(n_heads)], axis=1)
        return dattn_, dd_, dyg_, dz_, dbglu, dga, dgs

    dattn, dd, dyg1, dz, d_b_glu, d_attn_out_g, d_ssm_out_g = _rowwise(
        "mix_bwd", mix_bwd_fn, steps, [dcat, attn, yg, z, b_glu_t, attn_out_g, ssm_out_g],
        [_row(tm, d), _row(tm, aw), _row(tm, sw), _row(tm, sw), _vec(sw), _vec(aw), _vec(sw)],
        [_sds((s_len, aw), f32), _sds((s_len, aw), f32), _sds((s_len, sw), f32), _sds((s_len, sw), bf16),
         _sds((1, sw), f32), _sds((1, aw), f32), _sds((1, sw), f32)],
        [_row(tm, aw), _row(tm, aw), _row(tm, sw), _row(tm, sw), _vec(sw), _vec(aw), _vec(sw)],
        [False] * 4 + [True] * 3)

    (dyg2,) = _mm_nt("glu_dx", dz, wglu_g)
    g_wglu = _mm_tn("glu_dw", yg, dz).reshape(N_DEV, sw // N_DEV, sw)
    scatter["w_glu"], tok_glu = _exchange_start("scatter_w_glu", g_wglu, True, started)
    d_skip_t = d_skip + tok_glu[0:1, 0:1]

    def gelu_bwd_fn(dyg1_, dyg2_, ymm_, u_, dskip):
        _, vjp = jax.vjp(_ypre_fn, ymm_, u_, dskip)
        dymm, du_, ddskip = vjp(dyg1_ + dyg2_)
        return dymm, du_, ddskip

    dymm, du_skip, d_d_skip = _rowwise(
        "ssm_gelu_bwd", gelu_bwd_fn, steps, [dyg1, dyg2, ymm, proj, d_skip_t],
        [_row(tm, sw)] * 3 + [u_spec, _vec(sw)],
        [_sds((s_len, sw), bf16), _sds((s_len, sw), f32), _sds((1, sw), f32)],
        [_row(tm, sw), _row(tm, sw), _vec(sw)], [False, False, True])

    dymm_seg = _to_segments(dymm, nseg)
    du_seg, da_seg, dbb_c, dc_c = _scan("ssm_adj", dymm_seg, w_c, w_bu, a2, reverse=True,
                                        adjoint_of=(u_seg.T, dymm_seg.T, h_re, h_im, hin_f))
    du_ssm = _from_segments(du_seg, nseg)

    def to_gpi(w):
        return _lane_block_diag(w, n_state).transpose(0, 2, 1).reshape(n_groups, n_state * SSM_GROUP)

    d_lam_re, d_lam_im, d_log_step, d_b_re2, d_b_im2 = _ssm_params_bwd(
        lam_re2, lam_im2, log_step2, b_re2, b_im2, expand,
        da_seg[0, 0].reshape(n_groups, n_state), da_seg[1, 0].reshape(n_groups, n_state),
        to_gpi(dbb_c[0]), to_gpi(dbb_c[1]))
    d_c_re = _lane_block_diag(dc_c[0], n_state)
    d_c_im = -_lane_block_diag(dc_c[1], n_state)

    grads_qkv = None
    for _, dil in reversed(DILATIONS):
        grads_qkv = _attn_bwd(qn, kn, proj, v_blk, dattn, lse, dd, dil, slopes, acc=grads_qkv)

    def qkv_bwd_fn(q, k, gq, gk, dqn, dkn, dv, du1, du2):
        _, vjp = jax.vjp(lambda q_, k_, gq_, gk_: (_head_rms(q_, gq_), _head_rms(k_, gk_)), q, k, gq, gk)
        dq, dk, dgq, dgk = vjp((dqn, dkn))
        return jnp.concatenate([dq, dk, dv, du1 + du2], axis=1), dgq, dgk

    small_g = {"lam_re": d_lam_re, "lam_im": d_lam_im, "log_step": d_log_step, "b_re": d_b_re2, "b_im": d_b_im2,
               "c_re": d_c_re, "c_im": d_c_im, "d_skip": d_d_skip, "b_glu": d_b_glu,
               "attn_out_g": d_attn_out_g, "ssm_out_g": d_ssm_out_g, "norm2_g": d_norm2_g}
    early, tok_early = _exchange_start("gather_early_grads", _pack([small_g[n] for n in SMALL_EARLY]), False, started)

    dproj, small_g["q_norm_g"], small_g["k_norm_g"] = _rowwise(
        "qk_norm_bwd", qkv_bwd_fn, steps,
        [proj, proj, q_norm_g + tok_early[0:1, 0:1], k_norm_g, *grads_qkv, du_skip, du_ssm],
        [_row(tm, aw, 0), _row(tm, aw, 1), _vec(HEAD), _vec(HEAD)] + [_row(tm, aw)] * 3 + [_row(tm, sw)] * 2,
        [_sds((s_len, 3 * aw + sw), bf16), _sds((1, HEAD), f32), _sds((1, HEAD), f32)],
        [_row(tm, 3 * aw + sw), _vec(HEAD), _vec(HEAD)], [False, True, True])

    g_win = _mm_tn_sharded("in_dw", h, dproj, N_DEV)
    scatter["w_in"], tok_in = _exchange_start("scatter_w_in", g_win, True, tok_early)
    dh = _mm_nt("in_dx", dproj, win_g.transpose(1, 0, 2).reshape(d, 3 * aw + sw), after=tok_in)[0]
    norm1_g_t = norm1_g + tok_in[0:1, 0:1]

    def norm1_bwd_fn(dh_, x_, dx1_, gn, sc, sh):
        _, vjp = jax.vjp(_norm_mod, x_, gn, sc, sh)
        dx, dgn, dsc, dsh = vjp(dh_)
        return dx1_ + dx, dgn, dsc, dsh

    grad_x, d_norm1_g, d_sc1, d_sh1 = _rowwise(
        "norm1_bwd", norm1_bwd_fn, steps, [dh, xs, dx1, norm1_g_t, sc1, sh1], [_row(tm, d)] * 3 + [_vec(d)] * 3,
        [_sds((s_len, d), f32)] + [_sds((1, d), f32)] * 3, [_row(tm, d)] + [_vec(d)] * 3,
        [False, True, True, True])

    small_g["b_ada"] = jnp.concatenate([d_sh1, d_sc1, d_g1, d_sh2, d_sc2, d_g2], axis=1)
    small_g["norm1_g"] = d_norm1_g
    (r_late,) = _exchange("gather_late_grads", [_pack([small_g[n] for n in SMALL_LATE])], [False])

    res = {}
    dmod_all = r_late.reshape(N_DEV, -1)[:, :6 * d]
    g_wada = _ada_bwd(c_all, lax.dynamic_slice_in_dim(dmod_all, me * n_ada, n_ada, axis=1))
    res["w_ada"] = _adamw("adamw_w_ada", w_ada[0], m_w_ada[0], v_w_ada[0], g_wada, False)
    after = res["w_ada"][1]
    for name in ("w_ff2", "w_ff1", "w_out", "w_glu", "w_in"):
        stack = _exchange_wait("scattered_" + name, scatter[name], after)
        res[name] = _adamw("adamw_" + name, wts[name][0], mom[name][0], var[name][0], stack, True)
        after = res[name][1]
    r_early = _exchange_wait("gathered_early_grads", early, after)
    for label, names, stack in (("late", SMALL_LATE, r_late), ("early", SMALL_EARLY, r_early)):
        small_res = _adamw("adamw_small_" + label, *packed[names], stack, True, rows=4096)
        off = 0
        for n in names:
            size = wts[n].size
            res[n] = [t.reshape(-1)[off:off + size] for t in small_res]
            off += size

    out = [loss, grad_x[None]]
    for i in range(4):
        out += [res[n][i].reshape(wts[n].shape) for n in ORDER]
    return tuple(out)
```
